```python
import jax
import jax.numpy as jnp
from jax import lax
import numpy as np

D_MODEL = 1024
BATCH = 4
SEQ = 4096
DEPTH = 1
DEC_BATCH = 32
DEC_SEQ = 1
PAST_LEN = 8192
PAGE_SIZE = 128

HEAD_DIM = 64
POOL_WINDOWS = (2, 4, 8, 16)
POOL_WIDTH = D_MODEL // 4
POOL_GROUP = POOL_WIDTH // len(POOL_WINDOWS)
POOL_STATE = max(POOL_WINDOWS) - 1
NSA_WIDTH = D_MODEL // 2
NSA_HEADS = NSA_WIDTH // HEAD_DIM
NSA_KV_HEADS = 2
NSA_REP = NSA_HEADS // NSA_KV_HEADS
N_BRANCH = 3
CMP_BLOCK = 32
CMP_STRIDE = 16
CMP_HIDDEN = 2 * HEAD_DIM
SEL_BLOCK = 64
SEL_TOPK = 16
CMP_PER_SEL = SEL_BLOCK // CMP_STRIDE
WINDOW = 512
FORCE_BONUS = 1000.0
MEM_LEN = 256
MEM_WIDTH = D_MODEL - POOL_WIDTH - NSA_WIDTH
MEM_HEADS = 4
MEM_HEAD_DIM = MEM_WIDTH // MEM_HEADS
ROPE_DIM = HEAD_DIM // 4
ROPE_THETA = 500000.0
N_EXPERT_GROUPS = 4
EXPERTS_PER_GROUP = 8
TOP_K_INNER = 2
EXPERT_FF = 512
Q_BLOCK = 128
N_KV_SLOTS = 4
NSA_KV_WIDTH = N_BRANCH * 2 * NSA_KV_HEADS * HEAD_DIM
GATE_WIDTH = NSA_HEADS * N_BRANCH
IN_WIDTH = POOL_WIDTH + NSA_WIDTH + NSA_KV_WIDTH + GATE_WIDTH + MEM_WIDTH
EPS = 1e-6

kernel_name = 'hybrid_pool_nsa_hmoe_decode_step'


def rmsnorm(x, g):
    xf = x.astype(jnp.float32)
    y = xf * lax.rsqrt(jnp.mean(xf * xf, axis=-1, keepdims=True) + EPS)
    return (y * g.astype(jnp.float32)).astype(x.dtype)


def rope_partial(x, pos):
    half = ROPE_DIM // 2
    inv = jnp.power(ROPE_THETA, -jnp.arange(half, dtype=jnp.float32) * 2.0 / ROPE_DIM)
    ang = pos.astype(jnp.float32)[:, None] * inv[None, :]
    shape = (ang.shape[0],) + (1,) * (x.ndim - 3) + (half,)
    cos = jnp.cos(ang).reshape(shape)
    sin = jnp.sin(ang).reshape(shape)
    xf = x.astype(jnp.float32)
    x1 = xf[..., :half]
    x2 = xf[..., half:ROPE_DIM]
    out = jnp.concatenate([x1 * cos - x2 * sin, x2 * cos + x1 * sin, xf[..., ROPE_DIM:]], axis=-1)
    return out.astype(x.dtype)


def masked_softmax(s, mask):
    s = jnp.where(mask, s.astype(jnp.float32), -jnp.inf)
    m = jnp.max(s, axis=-1, keepdims=True)
    m = jnp.where(jnp.isfinite(m), m, 0.0)
    e = jnp.where(mask, jnp.exp(s - m), 0.0)
    return e / jnp.maximum(jnp.sum(e, axis=-1, keepdims=True), 1e-30)


def project(xn, pos, p):
    b, t, _ = xn.shape
    z = xn @ p['w_in']
    o1 = POOL_WIDTH
    o2 = o1 + NSA_WIDTH
    o3 = o2 + NSA_KV_WIDTH
    o4 = o3 + GATE_WIDTH
    u = z[..., :o1]
    q = rmsnorm(z[..., o1:o2].reshape(b, t, NSA_HEADS, HEAD_DIM), p['g_q'])
    q = rope_partial(q, pos).reshape(b, t, NSA_KV_HEADS, NSA_REP, HEAD_DIM)
    kv = z[..., o2:o3].reshape(b, t, N_BRANCH, 2, NSA_KV_HEADS, HEAD_DIM)
    k = rope_partial(rmsnorm(kv[:, :, :, 0], p['g_k'][:, None, :]), pos)
    v = kv[:, :, :, 1]
    rows4 = jnp.stack([k[:, :, 0], v[:, :, 0], k[:, :, 1], v[:, :, 1]], axis=2)
    rows_w = jnp.stack([k[:, :, 2], v[:, :, 2]], axis=2)
    gates = jax.nn.sigmoid(z[..., o3:o4].astype(jnp.float32)).astype(xn.dtype)
    gates = gates.reshape(b, t, NSA_KV_HEADS, NSA_REP, N_BRANCH)
    qm = rmsnorm(z[..., o4:].reshape(b, t, MEM_HEADS, MEM_HEAD_DIM), p['g_mq'])
    return u, q, rows4, rows_w, gates, qm


def pool_mix(u, prev, pos0, p):
    b, t, _ = u.shape
    ext = jnp.concatenate([prev, u], axis=1).astype(jnp.float32)
    cs = jnp.concatenate([jnp.zeros_like(ext[:, :1]), jnp.cumsum(ext, axis=1)], axis=1)
    pos = pos0 + jnp.arange(t, dtype=jnp.int32)
    means = []
    for gi, w in enumerate(POOL_WINDOWS):
        sl = slice(gi * POOL_GROUP, (gi + 1) * POOL_GROUP)
        hi = cs[:, POOL_STATE + 1:POOL_STATE + 1 + t, sl]
        lo = cs[:, POOL_STATE + 1 - w:POOL_STATE + 1 - w + t, sl]
        cnt = jnp.minimum(pos + 1, w).astype(jnp.float32)[None, :, None]
        means.append((hi - lo) / cnt)
    mean = jnp.concatenate(means, axis=-1).astype(u.dtype)
    r = (mean - u).reshape(b, t, len(POOL_WINDOWS), POOL_GROUP)
    y = jnp.einsum('btgc,gcd->btgd', r, p['w_pool']).reshape(b, t, POOL_WIDTH)
    return y * p['s_pool']


def compress_blocks(rows, pe, w1, w2):
    b, l, g, d = rows.shape
    n_chunk = -(-l // CMP_STRIDE)
    rows = jnp.pad(rows, ((0, 0), (0, n_chunk * CMP_STRIDE - l), (0, 0), (0, 0)))
    chunks = rows.reshape(b, n_chunk, CMP_STRIDE, g, d)
    blocks = jnp.concatenate([chunks[:, :-1], chunks[:, 1:]], axis=2)
    blocks = blocks + pe[None, None, :, None, :]
    flat = jnp.moveaxis(blocks, 3, 2).reshape(b, n_chunk - 1, g, CMP_BLOCK * d)
    return jax.nn.gelu(flat @ w1) @ w2


def sel_blocks(rows):
    b, l, g, d = rows.shape
    ns = -(-l // SEL_BLOCK)
    rows = jnp.pad(rows, ((0, 0), (0, ns * SEL_BLOCK - l), (0, 0), (0, 0)))
    return rows.reshape(b, ns, SEL_BLOCK, g, d).transpose(0, 3, 1, 2, 4).reshape(b, g, ns, SEL_BLOCK * d)


def nsa_context(rows4, p):
    kc = compress_blocks(rows4[:, :, 0], p['pe_cmp'][0], p['w_cmp1'][0], p['w_cmp2'][0])
    vc = compress_blocks(rows4[:, :, 1], p['pe_cmp'][1], p['w_cmp1'][1], p['w_cmp2'][1])
    kc_end = jnp.arange(kc.shape[1], dtype=jnp.int32) * CMP_STRIDE + (CMP_BLOCK - 1)
    return kc, vc, kc_end, sel_blocks(rows4[:, :, 2]), sel_blocks(rows4[:, :, 3])


def nsa_core(q, q_pos, gates, kc, vc, kc_end, ks_blk, vs_blk, kw, vw, kw_pos):
    b, tq = q.shape[0], q.shape[1]
    scale = HEAD_DIM ** -0.5
    t = q_pos
    s_c = jnp.einsum('bqgrd,bkgd->bgrqk', q, kc) * scale
    p_c = masked_softmax(s_c, kc_end[None, :] <= t[:, None])
    o_c = jnp.einsum('bgrqk,bkgd->bqgrd', p_c.astype(vc.dtype), vc)
    ns = ks_blk.shape[2]
    nc = kc.shape[1]
    imp = jnp.sum(p_c, axis=2)
    imp = jnp.pad(imp, ((0, 0), (0, 0), (0, 0), (0, ns * CMP_PER_SEL - nc)))
    imp = imp.reshape(b, NSA_KV_HEADS, tq, ns, CMP_PER_SEL).sum(-1)
    blk = jnp.arange(ns, dtype=jnp.int32)[None, :]
    cur = (t // SEL_BLOCK)[:, None]
    valid = blk * SEL_BLOCK <= t[:, None]
    forced = (blk == 0) | (blk == cur) | (blk == cur - 1)
    score = jnp.where(valid, imp + jnp.where(forced, FORCE_BONUS, 0.0), -jnp.inf)
    n_top = min(SEL_TOPK, ns)
    top_s, idx = lax.top_k(score, n_top)
    bi = jnp.arange(b)[:, None, None, None]
    gi = jnp.arange(NSA_KV_HEADS)[None, :, None, None]
    k_sel = ks_blk[bi, gi, idx].reshape(b, NSA_KV_HEADS, tq, n_top * SEL_BLOCK, HEAD_DIM)
    v_sel = vs_blk[bi, gi, idx].reshape(b, NSA_KV_HEADS, tq, n_top * SEL_BLOCK, HEAD_DIM)
    k_pos = (idx[..., None] * SEL_BLOCK + jnp.arange(SEL_BLOCK, dtype=jnp.int32)).reshape(b, NSA_KV_HEADS, tq, n_top * SEL_BLOCK)
    mask_s = jnp.repeat(jnp.isfinite(top_s), SEL_BLOCK, axis=-1) & (k_pos <= t[:, None])
    s_s = jnp.einsum('bqgrd,bgqkd->bgrqk', q, k_sel) * scale
    p_s = masked_softmax(s_s, mask_s[:, :, None])
    o_s = jnp.einsum('bgrqk,bgqkd->bqgrd', p_s.astype(v_sel.dtype), v_sel)
    dist = t[:, None] - kw_pos[None, :]
    mask_w = (dist >= 0) & (dist < WINDOW) & (kw_pos[None, :] >= 0)
    s_w = jnp.einsum('bqgrd,bkgd->bgrqk', q, kw) * scale
    p_w = masked_softmax(s_w, mask_w)
    o_w = jnp.einsum('bgrqk,bkgd->bqgrd', p_w.astype(vw.dtype), vw)
    return gates[..., 0:1] * o_c + gates[..., 1:2] * o_s + gates[..., 2:3] * o_w


def memory_kv(mem, p):
    b, m, _ = mem.shape
    kv = (rmsnorm(mem, p['g_mem']) @ p['w_mem_kv']).reshape(b, m, 2, MEM_HEADS, MEM_HEAD_DIM)
    return jnp.stack([rmsnorm(kv[:, :, 0], p['g_mk']), kv[:, :, 1]], axis=2)


def mem_attend(qm, mem_kv):
    b, t = qm.shape[0], qm.shape[1]
    s = jnp.einsum('bqhd,bkhd->bhqk', qm, mem_kv[:, :, 0]).astype(jnp.float32) * MEM_HEAD_DIM ** -0.5
    pr = jax.nn.softmax(s, axis=-1).astype(mem_kv.dtype)
    return jnp.einsum('bhqk,bkhd->bqhd', pr, mem_kv[:, :, 1]).reshape(b, t, MEM_WIDTH)


def hier_moe(x, p):
    n = x.shape[0]
    l1 = (x @ p['w_router1']).astype(jnp.float32) + p['b_router1']
    p1 = jax.nn.softmax(l1, axis=-1)
    top1_p, grp = lax.top_k(p1, 1)
    grp = grp[:, 0]
    l2 = jnp.einsum('nd,dge->nge', x, p['w_router2']).astype(jnp.float32) + p['b_router2']
    l2 = l2[jnp.arange(n), grp]
    top_v, top_i = lax.top_k(l2, TOP_K_INNER)
    top_w = jax.nn.softmax(top_v, axis=-1) * top1_p
    inner = jnp.sum(jax.nn.one_hot(top_i, EXPERTS_PER_GROUP, dtype=jnp.float32) * top_w[..., None], axis=1)
    combine = (jax.nn.one_hot(grp, N_EXPERT_GROUPS, dtype=jnp.float32)[:, :, None] * inner[:, None, :]).astype(x.dtype)
    y = jnp.zeros_like(x)
    for gi in range(N_EXPERT_GROUPS):
        hg = jnp.einsum('nd,edf->nef', x, p['w_gate'][gi])
        hu = jnp.einsum('nd,edf->nef', x, p['w_up'][gi])
        h = jax.nn.silu(hg) * hu * combine[:, gi, :, None]
        y = y + jnp.einsum('nef,efd->nd', h, p['w_down'][gi])
    return y


def finish(x, y_pool, o_nsa, o_mem, p):
    b, t, d = x.shape
    g = p['g_mix']
    mixed = jnp.concatenate([
        rmsnorm(y_pool, g[:POOL_WIDTH]),
        rmsnorm(o_nsa, g[POOL_WIDTH:POOL_WIDTH + NSA_WIDTH]),
        rmsnorm(o_mem, g[POOL_WIDTH + NSA_WIDTH:])], axis=-1)
    h = x + mixed @ p['w_out']
    hn = rmsnorm(h, p['g_ffn']).reshape(b * t, d)
    return h + hier_moe(hn, p).reshape(b, t, d)


def layer_prompt(x, mem, p):
    b, t, _ = x.shape
    pos = jnp.arange(t, dtype=jnp.int32)
    u, q, rows4, rows_w, gates, qm = project(rmsnorm(x, p['g_attn']), pos, p)
    y_pool = pool_mix(u, jnp.zeros((b, POOL_STATE, POOL_WIDTH), u.dtype), 0, p)
    kc, vc, kc_end, ks, vs = nsa_context(rows4, p)
    kw_pad = jnp.pad(rows_w, ((0, 0), (WINDOW, 0), (0, 0), (0, 0), (0, 0)))
    n_chunks = t // Q_BLOCK

    def chunk(args):
        qc, gc, c = args
        start = c * Q_BLOCK
        kw = lax.dynamic_slice_in_dim(kw_pad, start, WINDOW + Q_BLOCK, axis=1)
        kw_pos = start - WINDOW + jnp.arange(WINDOW + Q_BLOCK, dtype=jnp.int32)
        q_pos = start + jnp.arange(Q_BLOCK, dtype=jnp.int32)
        return nsa_core(qc, q_pos, gc, kc, vc, kc_end, ks, vs, kw[:, :, 0], kw[:, :, 1], kw_pos)

    qs = jnp.swapaxes(q.reshape(b, n_chunks, Q_BLOCK, NSA_KV_HEADS, NSA_REP, HEAD_DIM), 0, 1)
    gs = jnp.swapaxes(gates.reshape(b, n_chunks, Q_BLOCK, NSA_KV_HEADS, NSA_REP, N_BRANCH), 0, 1)
    o = lax.map(chunk, (qs, gs, jnp.arange(n_chunks, dtype=jnp.int32)))
    o_nsa = jnp.swapaxes(o, 0, 1).reshape(b, t, NSA_WIDTH)
    mem_kv = memory_kv(mem, p)
    y = finish(x, y_pool, o_nsa, mem_attend(qm, mem_kv), p)
    keep = min(WINDOW, t)
    return y, rows4, rows_w[:, t - keep:], u[:, t - POOL_STATE:], mem_kv


def layer_sample(x, pool_prev, kv_pool, page_table, win_buf, mem_kv, p):
    b, t, _ = x.shape
    pos = PAST_LEN + jnp.arange(t, dtype=jnp.int32)
    u, q, rows4, rows_w, gates, qm = project(rmsnorm(x, p['g_attn']), pos, p)
    y_pool = pool_mix(u, pool_prev, PAST_LEN, p)
    past = kv_pool[page_table].reshape(b, -1, N_KV_SLOTS, NSA_KV_HEADS, HEAD_DIM)
    kc, vc, kc_end, ks, vs = nsa_context(jnp.concatenate([past, rows4], axis=1), p)
    n_buf = win_buf.shape[1]
    kw = jnp.concatenate([win_buf, rows_w], axis=1)
    kw_pos = PAST_LEN - n_buf + jnp.arange(n_buf + t, dtype=jnp.int32)
    o = nsa_core(q, pos, gates, kc, vc, kc_end, ks, vs, kw[:, :, 0], kw[:, :, 1], kw_pos)
    y = finish(x, y_pool, o.reshape(b, t, NSA_WIDTH), mem_attend(qm, mem_kv), p)
    pool_ext = jnp.concatenate([pool_prev, u], axis=1)
    return y, rows4, kw[:, t:], pool_ext[:, t:]


def setup_inputs(seed: int = 0) -> dict:
    key = jax.random.key(seed)
    keys = iter(jax.random.split(key, 40))

    def nrm(shape, scale):
        return jax.random.normal(next(keys), shape, jnp.float32) * scale

    def gain(shape):
        return 1.0 + 0.02 * jax.random.normal(next(keys), shape, jnp.float32)

    n_pages = PAST_LEN // PAGE_SIZE
    n_used = DEC_BATCH * n_pages
    n_phys = n_used + max(1, n_used // 4)
    n_win = min(WINDOW, PAST_LEN)
    g = NSA_KV_HEADS
    x_prompt = nrm((BATCH, SEQ, D_MODEL), 1.0)
    x_sample = nrm((DEC_BATCH, DEC_SEQ, D_MODEL), 1.0)
    cache_kv = nrm((DEPTH, n_phys, PAGE_SIZE, N_KV_SLOTS, g, HEAD_DIM), 1.0)
    cache_win = nrm((DEPTH, DEC_BATCH, n_win, 2, g, HEAD_DIM), 1.0)
    state_pool = nrm((DEPTH, DEC_BATCH, POOL_STATE, POOL_WIDTH), 1.0)
    cache_mem_kv = nrm((DEPTH, DEC_BATCH, MEM_LEN, 2, MEM_HEADS, MEM_HEAD_DIM), 1.0)
    page_table = jax.random.permutation(next(keys), n_phys)[:n_used].reshape(DEC_BATCH, n_pages).astype(jnp.int32)
    mem_prompt = nrm((BATCH, MEM_LEN, D_MODEL), 1.0)
    return {
        'x_prompt': x_prompt,
        'x_sample': x_sample,
        'cache_kv': cache_kv,
        'cache_win': cache_win,
        'state_pool': state_pool,
        'cache_mem_kv': cache_mem_kv,
        'page_table': page_table,
        'mem_prompt': mem_prompt,
        'g_attn': gain((DEPTH, D_MODEL)),
        'w_in': nrm((DEPTH, D_MODEL, IN_WIDTH), D_MODEL ** -0.5),
        'g_q': gain((DEPTH, HEAD_DIM)),
        'g_k': gain((DEPTH, N_BRANCH, HEAD_DIM)),
        'pe_cmp': nrm((DEPTH, 2, CMP_BLOCK, HEAD_DIM), 0.1),
        'w_cmp1': nrm((DEPTH, 2, CMP_BLOCK * HEAD_DIM, CMP_HIDDEN), (CMP_BLOCK * HEAD_DIM) ** -0.5),
        'w_cmp2': nrm((DEPTH, 2, CMP_HIDDEN, HEAD_DIM), CMP_HIDDEN ** -0.5),
        'w_pool': nrm((DEPTH, len(POOL_WINDOWS), POOL_GROUP, POOL_GROUP), POOL_GROUP ** -0.5),
        's_pool': gain((DEPTH, POOL_WIDTH)),
        'g_mem': gain((DEPTH, D_MODEL)),
        'w_mem_kv': nrm((DEPTH, D_MODEL, 2 * MEM_WIDTH), D_MODEL ** -0.5),
        'g_mq': gain((DEPTH, MEM_HEAD_DIM)),
        'g_mk': gain((DEPTH, MEM_HEAD_DIM)),
        'g_mix': gain((DEPTH, D_MODEL)),
        'w_out': nrm((DEPTH, D_MODEL, D_MODEL), D_MODEL ** -0.5),
        'g_ffn': gain((DEPTH, D_MODEL)),
        'w_router1': nrm((DEPTH, D_MODEL, N_EXPERT_GROUPS), D_MODEL ** -0.5),
        'b_router1': nrm((DEPTH, N_EXPERT_GROUPS), 0.01),
        'w_router2': nrm((DEPTH, D_MODEL, N_EXPERT_GROUPS, EXPERTS_PER_GROUP), D_MODEL ** -0.5),
        'b_router2': nrm((DEPTH, N_EXPERT_GROUPS, EXPERTS_PER_GROUP), 0.01),
        'w_gate': nrm((DEPTH, N_EXPERT_GROUPS, EXPERTS_PER_GROUP, D_MODEL, EXPERT_FF), D_MODEL ** -0.5),
        'w_up': nrm((DEPTH, N_EXPERT_GROUPS, EXPERTS_PER_GROUP, D_MODEL, EXPERT_FF), D_MODEL ** -0.5),
        'w_down': nrm((DEPTH, N_EXPERT_GROUPS, EXPERTS_PER_GROUP, EXPERT_FF, D_MODEL), EXPERT_FF ** -0.5),
    }


def reference(x_prompt, x_sample, cache_kv, cache_win, state_pool, cache_mem_kv, page_table, mem_prompt,
              g_attn, w_in, g_q, g_k, pe_cmp, w_cmp1, w_cmp2, w_pool, s_pool, g_mem, w_mem_kv, g_mq, g_mk,
              g_mix, w_out, g_ffn, w_router1, b_router1, w_router2, b_router2, w_gate, w_up, w_down):
    hp = x_prompt
    hs = x_sample
    kv_p, kv_s, win_p, win_s, pool_p, pool_s, mem_p = [], [], [], [], [], [], []
    for l in range(DEPTH):
        p = {
            'g_attn': g_attn[l], 'w_in': w_in[l], 'g_q': g_q[l], 'g_k': g_k[l],
            'pe_cmp': pe_cmp[l], 'w_cmp1': w_cmp1[l], 'w_cmp2': w_cmp2[l],
            'w_pool': w_pool[l], 's_pool': s_pool[l],
            'g_mem': g_mem[l], 'w_mem_kv': w_mem_kv[l], 'g_mq': g_mq[l], 'g_mk': g_mk[l],
            'g_mix': g_mix[l], 'w_out': w_out[l], 'g_ffn': g_ffn[l],
            'w_router1': w_router1[l], 'b_router1': b_router1[l],
            'w_router2': w_router2[l], 'b_router2': b_router2[l],
            'w_gate': w_gate[l], 'w_up': w_up[l], 'w_down': w_down[l],
        }
        hp, r4p, rwp, plp, mkv = layer_prompt(hp, mem_prompt, p)
        hs, r4s, rws, pls = layer_sample(hs, state_pool[l], cache_kv[l], page_table, cache_win[l], cache_mem_kv[l], p)
        kv_p.append(r4p)
        kv_s.append(r4s)
        win_p.append(rwp)
        win_s.append(rws)
        pool_p.append(plp)
        pool_s.append(pls)
        mem_p.append(mkv)
    return (hp, hs, jnp.stack(kv_p), jnp.stack(kv_s), jnp.stack(win_p), jnp.stack(win_s),
            jnp.stack(pool_p), jnp.stack(pool_s), jnp.stack(mem_p))
```

```python
import functools

import numpy as np
import jax
import jax.numpy as jnp
from jax import lax
from jax.experimental import pallas as pl
from jax.experimental.pallas import tpu as pltpu

F32 = jnp.float32
BF16 = jnp.bfloat16
I32 = jnp.int32

HEAD_DIM = 64
POOL_WINDOWS = (2, 4, 8, 16)
POOL_STATE = max(POOL_WINDOWS) - 1
NSA_HEADS = 8
NSA_KV_HEADS = 2
NSA_REP = NSA_HEADS // NSA_KV_HEADS
N_BRANCH = 3
CMP_BLOCK = 32
CMP_STRIDE = 16
CMP_HIDDEN = 2 * HEAD_DIM
SEL_BLOCK = 64
SEL_TOPK = 16
CMP_PER_SEL = SEL_BLOCK // CMP_STRIDE
WINDOW = 512
FORCE_BONUS = 1000.0
MEM_HEADS = 4
ROPE_DIM = HEAD_DIM // 4
ROPE_THETA = 500000.0
N_EXPERT_GROUPS = 4
EXPERTS_PER_GROUP = 8
N_EXPERTS = N_EXPERT_GROUPS * EXPERTS_PER_GROUP
EPS = 1e-6
N_KV_SLOTS = 4

LANES = 128
POOL_WIDTH = 256
NSA_WIDTH = NSA_HEADS * HEAD_DIM
NSA_KV_WIDTH = N_BRANCH * 2 * NSA_KV_HEADS * HEAD_DIM
GATE_WIDTH = NSA_HEADS * N_BRANCH
MEM_WIDTH = MEM_HEADS * HEAD_DIM
Q_PAD = NSA_HEADS * LANES
QM_PAD = MEM_HEADS * LANES
C_U = 0
C_Q = C_U + POOL_WIDTH
C_KV = C_Q + Q_PAD
C_QM = C_KV + NSA_KV_WIDTH
C_G = C_QM + QM_PAD
C_END = C_G + LANES

NEG = -1e30
VMEM_LIMIT = 48 * 1024 * 1024

_NT = (((1,), (1,)), ((), ()))


def _cparams(*sem):
    return pltpu.CompilerParams(dimension_semantics=tuple(sem), vmem_limit_bytes=VMEM_LIMIT)


def _dot(a, b):
    return jnp.dot(a, b, preferred_element_type=F32)


def _dot_nt(a, b):
    return lax.dot_general(a, b, _NT, preferred_element_type=F32)


def _rms(x):
    return x * lax.rsqrt(jnp.mean(x * x, axis=-1, keepdims=True) + EPS)


def _masked_softmax(s, mask):
    sm = jnp.where(mask, s, NEG)
    m = jnp.max(sm, axis=-1, keepdims=True)
    e = jnp.where(mask, jnp.exp(sm - m), 0.0)
    return e / jnp.maximum(jnp.sum(e, axis=-1, keepdims=True), 1e-30)


def _split3(x):
    hi = x.astype(BF16)
    r1 = x - hi.astype(F32)
    mid = r1.astype(BF16)
    lo = (r1 - mid.astype(F32)).astype(BF16)
    return hi, mid, lo


def _proj_kernel(x_ref, ga_ref, w_ref, gq_ref, gk_ref, gmq_ref, inv_ref, seg_ref,
                 u_ref, q_ref, rows4_ref, rowsw_ref, kvb_ref, gates_ref, qm_ref, *, tm, seq_len, pos0):
    i = pl.program_id(0)
    a = _rms(x_ref[...]) * ga_ref[...]
    z = _dot(a.astype(BF16), w_ref[...])
    u_ref[...] = z[:, C_U:C_U + POOL_WIDTH]

    row = i * tm + lax.broadcasted_iota(I32, (tm, 1), 0)
    pos = (pos0 + row % seq_len).astype(F32)
    ang = pos * inv_ref[...]
    d = lax.broadcasted_iota(I32, (1, LANES), 1) % HEAD_DIM
    cos = jnp.cos(ang)
    sin = jnp.sin(ang)
    half = ROPE_DIM // 2
    s_next = jnp.where(d < half, -sin, 0.0)
    s_prev = jnp.where((d >= half) & (d < ROPE_DIM), sin, 0.0)
    seg = seg_ref[...]

    def head_norm(xc, g):
        ssq = _dot((xc * xc).astype(BF16), seg)
        return xc * lax.rsqrt(ssq * (1.0 / HEAD_DIM) + EPS) * g

    def rope(xc):
        return xc * cos + pltpu.roll(xc, LANES - half, 1) * s_next + pltpu.roll(xc, half, 1) * s_prev

    scale = HEAD_DIM ** -0.5
    for c in range(NSA_HEADS):
        sl = slice(c * LANES, (c + 1) * LANES)
        qc = rope(head_norm(z[:, C_Q + c * LANES:C_Q + (c + 1) * LANES], gq_ref[:, sl]))
        q_ref[:, sl] = (qc * scale).astype(BF16)
    for br in range(N_BRANCH):
        k0 = C_KV + br * 2 * LANES
        kn = rope(head_norm(z[:, k0:k0 + LANES], gk_ref[:, br * LANES:(br + 1) * LANES]))
        vv = z[:, k0 + LANES:k0 + 2 * LANES]
        kvb_ref[:, br * 2 * LANES:br * 2 * LANES + LANES] = kn.astype(BF16)
        kvb_ref[:, br * 2 * LANES + LANES:(br + 1) * 2 * LANES] = vv.astype(BF16)
        if br < 2:
            rows4_ref[:, br * 2 * LANES:br * 2 * LANES + LANES] = kn
            rows4_ref[:, br * 2 * LANES + LANES:(br + 1) * 2 * LANES] = vv
        else:
            rowsw_ref[:, :LANES] = kn
            rowsw_ref[:, LANES:] = vv
    for c in range(MEM_HEADS):
        sl = slice(c * LANES, (c + 1) * LANES)
        qmc = head_norm(z[:, C_QM + c * LANES:C_QM + (c + 1) * LANES], gmq_ref[:, sl])
        qm_ref[:, sl] = (qmc * scale).astype(BF16)
    gates_ref[...] = jax.nn.sigmoid(z[:, C_G:C_END])


def _proj_call(x2d, pw, *, seq_len, pos0, tm):
    n, dm = x2d.shape
    full = lambda shape: pl.BlockSpec(shape, lambda i: (0,) * len(shape))
    rows = lambda w: pl.BlockSpec((tm, w), lambda i: (i, 0))
    outs = [(POOL_WIDTH, F32), (Q_PAD, BF16), (4 * LANES, F32), (2 * LANES, F32), (NSA_KV_WIDTH, BF16),
            (LANES, F32), (QM_PAD, BF16)]
    return pl.pallas_call(
        functools.partial(_proj_kernel, tm=tm, seq_len=seq_len, pos0=pos0),
        grid=(n // tm,),
        in_specs=[rows(dm), full((1, dm)), full((dm, C_END)), full((1, Q_PAD)), full((1, N_BRANCH * LANES)),
                  full((1, QM_PAD)), full((1, LANES)), full((LANES, LANES))],
        out_specs=[rows(w) for w, _ in outs],
        out_shape=[jax.ShapeDtypeStruct((n, w), dt) for w, dt in outs],
        compiler_params=_cparams("parallel"),
        name="proj",
    )(x2d, pw["g_attn"], pw["w_in"], pw["gq"], pw["gk"], pw["gmq"], pw["inv"], pw["seg"])


def _pool_kernel(u_ref, halo_ref, w_ref, s_ref, y_ref, *, tp, pos0):
    i = pl.program_id(1)
    u = u_ref[0]
    halo = halo_ref[0] * (i > 0).astype(F32)
    n_h = POOL_STATE + 1
    ext = jnp.concatenate([halo, u], axis=0)
    sums = {1: ext}
    w = 1
    while w < max(POOL_WINDOWS):
        sums[2 * w] = sums[w] + pltpu.roll(sums[w], w, 0)
        w *= 2
    pos = pos0 + i * tp + lax.broadcasted_iota(I32, (tp, 1), 0)
    lane_grp = lax.broadcasted_iota(I32, (1, POOL_WIDTH), 1) // (POOL_WIDTH // len(POOL_WINDOWS))
    mean = jnp.zeros((tp, POOL_WIDTH), F32)
    for gi, wdw in enumerate(POOL_WINDOWS):
        cnt = jnp.minimum(pos + 1, wdw).astype(F32)
        mean = jnp.where(lane_grp == gi, sums[wdw][n_h:] / cnt, mean)
    r = mean - u
    y_ref[0] = _dot(r.astype(BF16), w_ref[...]) * s_ref[...]


def _pool_call(u3, w_bd, s_pool, *, tp, pos0):
    b, t, c = u3.shape
    n_h = POOL_STATE + 1
    return pl.pallas_call(
        functools.partial(_pool_kernel, tp=tp, pos0=pos0),
        grid=(b, t // tp),
        in_specs=[pl.BlockSpec((1, tp, c), lambda bi, i: (bi, i, 0)),
                  pl.BlockSpec((1, n_h, c), lambda bi, i: (bi, jnp.maximum(i * (tp // n_h) - 1, 0), 0)),
                  pl.BlockSpec((c, c), lambda bi, i: (0, 0)),
                  pl.BlockSpec((1, c), lambda bi, i: (0, 0))],
        out_specs=pl.BlockSpec((1, tp, c), lambda bi, i: (bi, i, 0)),
        out_shape=jax.ShapeDtypeStruct((b, t, c), F32),
        compiler_params=_cparams("parallel", "parallel"),
        name="pool",
    )(u3, u3, w_bd, s_pool)


def _compress_core(load_rows, tail_ref, pea_ref, peb_ref, wa_ref, wb_ref, w2_ref, n):
    xa = jnp.concatenate([load_rows(r, c) for r in range(CMP_STRIDE) for c in range(2)], axis=1)
    first = _dot((xa + pea_ref[...]).astype(BF16), wa_ref[...])
    second = _dot((xa + peb_ref[...]).astype(BF16), wb_ref[...])
    tail = jnp.broadcast_to(tail_ref[0] + peb_ref[...], (8, xa.shape[1]))
    second_tail = _dot(tail.astype(BF16), wb_ref[...])
    shifted = pltpu.roll(second, n - 1, 0)
    rowid = lax.broadcasted_iota(I32, (n, 1), 0)
    h = first + jnp.where(rowid == n - 1, second_tail[0:1, :], shifted)
    h = jax.nn.gelu(h, approximate=True)
    return _dot(h.astype(BF16), w2_ref[...])


def _compress_prompt_kernel(k_ref, v_ref, tail_ref, pea_ref, peb_ref, wa_ref, wb_ref, w2_ref, out_ref, *, n):
    load = lambda r, c: (k_ref, v_ref)[c][0, pl.ds(r, n, stride=CMP_STRIDE), :]
    out_ref[0] = _compress_core(load, tail_ref, pea_ref, peb_ref, wa_ref, wb_ref, w2_ref, n).astype(BF16)


def _compress_prompt_call(rows4_3d, tail, cw):
    b, t, _ = rows4_3d.shape
    n = t // CMP_STRIDE
    kw = CMP_STRIDE * 2 * LANES
    full = lambda shape: pl.BlockSpec(shape, lambda bi: (0,) * len(shape))
    return pl.pallas_call(
        functools.partial(_compress_prompt_kernel, n=n),
        grid=(b,),
        in_specs=[pl.BlockSpec((1, t, LANES), lambda bi: (bi, 0, 0)),
                  pl.BlockSpec((1, t, LANES), lambda bi: (bi, 0, 1)),
                  pl.BlockSpec((1, 1, kw), lambda bi: (bi, 0, 0)),
                  full((1, kw)), full((1, kw)), full((kw, 4 * CMP_HIDDEN)), full((kw, 4 * CMP_HIDDEN)),
                  full((4 * CMP_HIDDEN, 2 * LANES))],
        out_specs=pl.BlockSpec((1, n, 2 * LANES), lambda bi: (bi, 0, 0)),
        out_shape=jax.ShapeDtypeStruct((b, n, 2 * LANES), BF16),
        compiler_params=_cparams("parallel"),
        name="compress_prompt",
    )(rows4_3d, rows4_3d, tail, cw["pea"], cw["peb"], cw["wa"], cw["wb"], cw["w2"])


def _compress_sample_kernel(pt_ref, cache_ref, tail_ref, pea_ref, peb_ref, wa_ref, wb_ref, w2_ref, out_ref,
                            buf, sem, *, n, n_pages, page):
    b = pl.program_id(0)
    nb = pl.num_programs(0)

    def page_copy(bb, slot, p, c):
        return pltpu.make_async_copy(cache_ref.at[pt_ref[bb * n_pages + p], :, pl.ds(c * LANES, LANES)],
                                     buf.at[slot, c, pl.ds(p * page, page), :], sem.at[slot])

    def issue(bb, slot):
        for p in range(n_pages):
            for c in range(2):
                page_copy(bb, slot, p, c).start()

    @pl.when(b == 0)
    def _():
        issue(0, 0)

    @pl.when(b + 1 < nb)
    def _():
        issue(b + 1, (b + 1) % 2)

    slot = b % 2
    for p in range(n_pages):
        for c in range(2):
            page_copy(b, slot, p, c).wait()
    load = lambda r, c: buf[slot, c, pl.ds(r, n, stride=CMP_STRIDE), :]
    out_ref[0] = _compress_core(load, tail_ref, pea_ref, peb_ref, wa_ref, wb_ref, w2_ref, n).astype(BF16)


def _compress_sample_call(page_table, cache3, tail, cw):
    b, n_pages = page_table.shape
    page = cache3.shape[1]
    n = n_pages * page // CMP_STRIDE
    kw = CMP_STRIDE * 2 * LANES
    full = lambda shape: pl.BlockSpec(shape, lambda bi, pt: (0,) * len(shape))
    grid_spec = pltpu.PrefetchScalarGridSpec(
        num_scalar_prefetch=1,
        grid=(b,),
        in_specs=[pl.BlockSpec(memory_space=pl.ANY),
                  pl.BlockSpec((1, 1, kw), lambda bi, pt: (bi, 0, 0)),
                  full((1, kw)), full((1, kw)), full((kw, 4 * CMP_HIDDEN)), full((kw, 4 * CMP_HIDDEN)),
                  full((4 * CMP_HIDDEN, 2 * LANES))],
        out_specs=pl.BlockSpec((1, n, 2 * LANES), lambda bi, pt: (bi, 0, 0)),
        scratch_shapes=[pltpu.VMEM((2, 2, n_pages * page, LANES), F32), pltpu.SemaphoreType.DMA((2,))],
    )
    return pl.pallas_call(
        functools.partial(_compress_sample_kernel, n=n, n_pages=n_pages, page=page),
        grid_spec=grid_spec,
        out_shape=jax.ShapeDtypeStruct((b, n, 2 * LANES), BF16),
        compiler_params=_cparams("arbitrary"),
        name="compress_sample",
    )(page_table.reshape(-1), cache3, tail, cw["pea"], cw["peb"], cw["wa"], cw["wb"], cw["w2"])


def _topk_mask(score, ids, n_top, axis):
    sel = jnp.zeros(score.shape, jnp.bool_)
    work = score
    firsts = []
    big = float(score.shape[axis])
    for _ in range(n_top):
        m = jnp.max(work, axis=axis, keepdims=True)
        first = jnp.min(jnp.where(work == m, ids, big), axis=axis, keepdims=True)
        pick = (ids == first) & (m > -jnp.inf)
        sel = sel | pick
        work = jnp.where(pick, -jnp.inf, work)
        firsts.append(jnp.where(m > -jnp.inf, first, -1.0))
    return sel, firsts


def _gate_and_pack(o_ref_store, gates, o_c, o_s, o_w, rows):
    lane = lax.broadcasted_iota(I32, (1, LANES), 1)
    heads = []
    for h in range(NSA_HEADS):
        g, r = divmod(h, NSA_REP)
        rs = slice(r * rows, (r + 1) * rows)
        gc, gs, gw = (gates[:, N_BRANCH * h + k:N_BRANCH * h + k + 1] for k in range(N_BRANCH))
        heads.append(gc * o_c[g][rs] + gs * o_s[g][rs] + gw * o_w[g][rs])
    for j in range(NSA_HEADS // 2):
        a, b = heads[2 * j], heads[2 * j + 1]
        if (2 * j) // NSA_REP == 0:
            chunk = jnp.where(lane < HEAD_DIM, a, pltpu.roll(b, HEAD_DIM, 1))
        else:
            chunk = jnp.where(lane < HEAD_DIM, pltpu.roll(a, HEAD_DIM, 1), b)
        o_ref_store(j, chunk)


def _nsa_prompt_kernel(q_ref, gate_ref, kc_ref, kv_ref, poolt_ref, e_ref, o_ref, selx_ref, acc_ref,
                       *, tq, tk, t_len):
    i = pl.program_id(1)
    s0 = i * tq
    rows4 = NSA_REP * tq
    tpos = s0 + lax.broadcasted_iota(I32, (tq, 1), 0)
    tpos4 = jnp.concatenate([tpos] * NSA_REP, axis=0)
    n_cmp = kc_ref.shape[1]
    kc128 = kc_ref[0, :, :LANES]
    vc128 = kc_ref[0, :, LANES:]
    kc_end = lax.broadcasted_iota(I32, (1, n_cmp), 1) * CMP_STRIDE + (CMP_BLOCK - 1)
    mask_c = kc_end <= tpos4
    n_sel = t_len // SEL_BLOCK
    blk = lax.broadcasted_iota(I32, (n_sel, 1), 0)
    blk_f = blk.astype(F32)
    tq_lane = s0 + lax.broadcasted_iota(I32, (1, tq), 1)
    cur = tq_lane // SEL_BLOCK
    valid = blk * SEL_BLOCK <= tq_lane
    forced = (blk == 0) | (blk == cur) | (blk == cur - 1)
    w_start = pl.multiple_of(jnp.maximum(s0 - WINDOW, 0), LANES)
    w_len = WINDOW + tq
    kpos_w = w_start + lax.broadcasted_iota(I32, (1, w_len), 1)
    dist = tpos4 - kpos_w
    mask_w = (dist >= 0) & (dist < WINDOW)
    n_ch = (s0 + tq + tk - 1) // tk

    o_c, o_s, o_w = [], [], []
    for g in range(NSA_KV_HEADS):
        qg = jnp.concatenate([q_ref[0, :, (g * NSA_REP + r) * LANES:(g * NSA_REP + r + 1) * LANES]
                              for r in range(NSA_REP)], axis=0)
        p_c = _masked_softmax(_dot_nt(qg, kc128), mask_c)
        o_c.append(_dot(p_c.astype(BF16), vc128))
        imp = p_c[0:tq]
        for r in range(1, NSA_REP):
            imp = imp + p_c[r * tq:(r + 1) * tq]
        imp_t = sum(_dot_nt(poolt_ref[...], piece) for piece in _split3(imp))
        score = jnp.where(valid, imp_t + jnp.where(forced, FORCE_BONUS, 0.0), -jnp.inf)
        sel_t, _ = _topk_mask(score, blk_f, min(SEL_TOPK, n_sel), 0)
        sel_pad = jnp.concatenate([sel_t.astype(F32), jnp.zeros(((-n_sel) % LANES, tq), F32)], axis=0)
        sel = sel_pad.T[:, :n_sel].astype(BF16)
        for c in range(t_len // tk):
            selx_ref[c] = _dot(sel, e_ref[:, c * tk:(c + 1) * tk])

        acc_ref[...] = jnp.zeros(acc_ref.shape, F32)

        def body(c, carry):
            m, l = carry
            k0 = pl.multiple_of(c * tk, tk)
            ks = kv_ref[0, pl.ds(k0, tk), 2 * LANES:3 * LANES]
            vs = kv_ref[0, pl.ds(k0, tk), 3 * LANES:4 * LANES]
            s = _dot_nt(qg, ks)
            kpos = c * tk + lax.broadcasted_iota(I32, (1, tk), 1)
            vis = (jnp.concatenate([selx_ref[c]] * NSA_REP, axis=0) > 0.5) & (kpos <= tpos4)
            s = jnp.where(vis, s, NEG)
            m_new = jnp.maximum(m, jnp.max(s, axis=-1, keepdims=True))
            alpha = jnp.exp(m - m_new)
            p = jnp.where(vis, jnp.exp(s - m_new), 0.0)
            l_new = alpha * l + jnp.sum(p, axis=-1, keepdims=True)
            acc_ref[...] = alpha * acc_ref[...] + _dot(p.astype(BF16), vs)
            return m_new, l_new

        _, l_fin = lax.fori_loop(0, n_ch, body, (jnp.full((rows4, 1), NEG, F32), jnp.zeros((rows4, 1), F32)))
        o_s.append(acc_ref[...] / jnp.maximum(l_fin, 1e-30))

        kw = kv_ref[0, pl.ds(w_start, w_len), 4 * LANES:5 * LANES]
        vw = kv_ref[0, pl.ds(w_start, w_len), 5 * LANES:6 * LANES]
        p_w = _masked_softmax(_dot_nt(qg, kw), mask_w)
        o_w.append(_dot(p_w.astype(BF16), vw))

    def store(j, chunk):
        o_ref[0, :, j * LANES:(j + 1) * LANES] = chunk

    _gate_and_pack(store, gate_ref[0], o_c, o_s, o_w, tq)


def _nsa_prompt_call(q3, gates3, kcvc, kvb3, poolt, e_mat, *, tq, tk):
    b, t, _ = q3.shape
    n_cmp = kcvc.shape[1]
    return pl.pallas_call(
        functools.partial(_nsa_prompt_kernel, tq=tq, tk=tk, t_len=t),
        grid=(b, t // tq),
        in_specs=[pl.BlockSpec((1, tq, Q_PAD), lambda bi, i: (bi, i, 0)),
                  pl.BlockSpec((1, tq, LANES), lambda bi, i: (bi, i, 0)),
                  pl.BlockSpec((1, n_cmp, 2 * LANES), lambda bi, i: (bi, 0, 0)),
                  pl.BlockSpec((1, t, NSA_KV_WIDTH), lambda bi, i: (bi, 0, 0)),
                  pl.BlockSpec(poolt.shape, lambda bi, i: (0, 0)),
                  pl.BlockSpec(e_mat.shape, lambda bi, i: (0, 0))],
        out_specs=pl.BlockSpec((1, tq, NSA_WIDTH), lambda bi, i: (bi, i, 0)),
        out_shape=jax.ShapeDtypeStruct((b, t, NSA_WIDTH), F32),
        scratch_shapes=[pltpu.VMEM((t // tk, tq, tk), F32), pltpu.VMEM((NSA_REP * tq, LANES), F32)],
        compiler_params=_cparams("parallel", "arbitrary"),
        name="nsa_prompt",
    )(q3, gates3, kcvc, kvb3, poolt, e_mat)


def _memkv_kernel(x_ref, g_ref, w_ref, gk_ref, seg_ref, o_ref, ob_ref):
    a = _rms(x_ref[...]) * g_ref[...]
    z = _dot(a.astype(BF16), w_ref[...])
    for c in range(MEM_WIDTH // LANES):
        sl = slice(c * LANES, (c + 1) * LANES)
        xc = z[:, sl]
        ssq = _dot((xc * xc).astype(BF16), seg_ref[...])
        kn = xc * lax.rsqrt(ssq * (1.0 / HEAD_DIM) + EPS) * gk_ref[:, sl]
        o_ref[:, sl] = kn
        ob_ref[:, sl] = kn.astype(BF16)
    o_ref[:, MEM_WIDTH:] = z[:, MEM_WIDTH:]
    ob_ref[:, MEM_WIDTH:] = z[:, MEM_WIDTH:].astype(BF16)


def _memkv_call(mem2d, g_mem, w_mem, gmk, seg, *, tm):
    n, dm = mem2d.shape
    full = lambda shape: pl.BlockSpec(shape, lambda i: (0,) * len(shape))
    return pl.pallas_call(
        _memkv_kernel,
        grid=(n // tm,),
        in_specs=[pl.BlockSpec((tm, dm), lambda i: (i, 0)), full((1, dm)), full((dm, 2 * MEM_WIDTH)),
                  full((1, MEM_WIDTH)), full((LANES, LANES))],
        out_specs=[pl.BlockSpec((tm, 2 * MEM_WIDTH), lambda i: (i, 0))] * 2,
        out_shape=[jax.ShapeDtypeStruct((n, 2 * MEM_WIDTH), F32), jax.ShapeDtypeStruct((n, 2 * MEM_WIDTH), BF16)],
        compiler_params=_cparams("parallel"),
        name="memkv",
    )(mem2d, g_mem, w_mem, gmk, seg)


def _mem_attend(qm, mkv, rows):
    lane = lax.broadcasted_iota(I32, (1, LANES), 1)
    chunks = []
    for j in range(MEM_HEADS // 2):
        k128 = mkv[:, j * LANES:(j + 1) * LANES]
        v128 = mkv[:, MEM_WIDTH + j * LANES:MEM_WIDTH + (j + 1) * LANES]
        q2 = jnp.concatenate([qm[2 * j], qm[2 * j + 1]], axis=0)
        s = _dot_nt(q2, k128)
        e = jnp.exp(s - jnp.max(s, axis=-1, keepdims=True))
        p = e / jnp.sum(e, axis=-1, keepdims=True)
        o = _dot(p.astype(BF16), v128)
        chunks.append(jnp.where(lane < HEAD_DIM, o[:rows], o[rows:2 * rows]))
    return chunks


def _memattn_kernel(qm_ref, mkv_ref, o_ref, *, tq):
    qm = [qm_ref[0, :, h * LANES:(h + 1) * LANES] for h in range(MEM_HEADS)]
    for j, chunk in enumerate(_mem_attend(qm, mkv_ref[0], tq)):
        o_ref[0, :, j * LANES:(j + 1) * LANES] = chunk


def _memattn_call(qm3, mkv3, *, tq):
    b, t, _ = qm3.shape
    m = mkv3.shape[1]
    return pl.pallas_call(
        functools.partial(_memattn_kernel, tq=tq),
        grid=(b, t // tq),
        in_specs=[pl.BlockSpec((1, tq, QM_PAD), lambda bi, i: (bi, i, 0)),
                  pl.BlockSpec((1, m, 2 * MEM_WIDTH), lambda bi, i: (bi, 0, 0))],
        out_specs=pl.BlockSpec((1, tq, MEM_WIDTH), lambda bi, i: (bi, i, 0)),
        out_shape=jax.ShapeDtypeStruct((b, t, MEM_WIDTH), F32),
        compiler_params=_cparams("parallel", "parallel"),
        name="memattn",
    )(qm3, mkv3)


def _pad_rows(rows_list):
    x = jnp.concatenate(rows_list, axis=0)
    return jnp.concatenate([x, jnp.zeros((8 - x.shape[0], x.shape[1]), x.dtype)], axis=0)


def _sample_attn1_kernel(q_ref, kc_ref, win_ref, mkv_ref, qm_ref, pool_ref,
                         oc_ref, ow_ref, om_ref, idx_ref, *, t_pos, n_win):
    q = q_ref[0].astype(F32)
    n_cmp = kc_ref.shape[1]
    kc128 = kc_ref[0, :, :LANES]
    vc128 = kc_ref[0, :, LANES:]
    kw128 = win_ref[0, :, :LANES].astype(BF16)
    vw128 = win_ref[0, :, LANES:].astype(BF16)
    kc_end = lax.broadcasted_iota(I32, (1, n_cmp), 1) * CMP_STRIDE + (CMP_BLOCK - 1)
    mask_c = kc_end <= t_pos
    kw_pos = t_pos - (n_win - 1) + lax.broadcasted_iota(I32, (1, n_win), 1)
    mask_w = (t_pos - kw_pos >= 0) & (t_pos - kw_pos < WINDOW) & (kw_pos >= 0)
    n_slot = pool_ref.shape[1]
    blk = lax.broadcasted_iota(I32, (1, n_slot), 1)
    cur = t_pos // SEL_BLOCK
    valid = blk * SEL_BLOCK <= t_pos
    forced = (blk == 0) | (blk == cur) | (blk == cur - 1)
    lane = lax.broadcasted_iota(I32, (1, LANES), 1)
    for g in range(NSA_KV_HEADS):
        qg = _pad_rows([q[:, (g * NSA_REP + r) * LANES:(g * NSA_REP + r + 1) * LANES]
                        for r in range(NSA_REP)]).astype(BF16)
        p_c = _masked_softmax(_dot_nt(qg, kc128), mask_c)
        oc_ref[0, g * NSA_REP:(g + 1) * NSA_REP, :] = _dot(p_c.astype(BF16), vc128)[:NSA_REP]
        imp = jnp.sum(p_c[:NSA_REP], axis=0, keepdims=True)
        imp8 = jnp.broadcast_to(imp, (8, n_cmp))
        imp_b = sum(_dot(piece, pool_ref[...]) for piece in _split3(imp8))[0:1]
        score = jnp.where(valid, imp_b + jnp.where(forced, FORCE_BONUS, 0.0), -jnp.inf)
        n_valid_blocks = t_pos // SEL_BLOCK + 1
        _, firsts = _topk_mask(score, blk.astype(F32), min(SEL_TOPK, n_valid_blocks), 1)
        idxv = jnp.full((1, LANES), -1, I32)
        for j, f in enumerate(firsts):
            idxv = jnp.where(lane == j, f.astype(I32), idxv)
        idx_ref[0, g:g + 1, :] = idxv
        p_w = _masked_softmax(_dot_nt(qg, kw128), mask_w)
        ow_ref[0, g * NSA_REP:(g + 1) * NSA_REP, :] = _dot(p_w.astype(BF16), vw128)[:NSA_REP]
    qm = qm_ref[0].astype(F32)
    qmh = []
    for h in range(MEM_HEADS):
        row = qm[:, h * LANES:(h + 1) * LANES]
        qmh.append(jnp.concatenate([row, jnp.zeros((7, LANES), F32)], axis=0).astype(BF16))
    for j, chunk in enumerate(_mem_attend(qmh, mkv_ref[0].astype(BF16), 8)):
        om_ref[0, :, j * LANES:(j + 1) * LANES] = chunk[0:1]


def _sample_attn1_call(q3, kcvc, win3, mkv3, qm3, pool_mat, *, t_pos):
    b = q3.shape[0]
    n_cmp = kcvc.shape[1]
    n_win = win3.shape[1]
    m = mkv3.shape[1]
    per_b = lambda shape: pl.BlockSpec((1,) + shape, lambda bi: (bi, 0, 0))
    return pl.pallas_call(
        functools.partial(_sample_attn1_kernel, t_pos=t_pos, n_win=n_win),
        grid=(b,),
        in_specs=[per_b((1, Q_PAD)), per_b((n_cmp, 2 * LANES)), per_b((n_win, 2 * LANES)),
                  per_b((m, 2 * MEM_WIDTH)), per_b((1, QM_PAD)),
                  pl.BlockSpec(pool_mat.shape, lambda bi: (0, 0))],
        out_specs=[per_b((NSA_HEADS, LANES)), per_b((NSA_HEADS, LANES)), per_b((1, MEM_WIDTH)),
                   per_b((NSA_KV_HEADS, LANES))],
        out_shape=[jax.ShapeDtypeStruct((b, NSA_HEADS, LANES), F32), jax.ShapeDtypeStruct((b, NSA_HEADS, LANES), F32),
                   jax.ShapeDtypeStruct((b, 1, MEM_WIDTH), F32), jax.ShapeDtypeStruct((b, NSA_KV_HEADS, LANES), I32)],
        compiler_params=_cparams("parallel"),
        name="sample_attn1",
    )(q3, kcvc, win3, mkv3, qm3, pool_mat)


def _sample_sel_kernel(pt_ref, idx_ref, cache_ref, q_ref, knew_ref, gate_ref, oc_ref, ow_ref, o_ref,
                       buf, sem, *, t_pos, n_pages, n_top):
    b = pl.program_id(0)
    nb = pl.num_programs(0)
    n_past_blk = n_pages * 2
    idx_stride = NSA_KV_HEADS * LANES

    def blk_at(bb, g, j):
        return idx_ref[bb * idx_stride + g * LANES + j]

    def blk_copy(bb, slot, g, j):
        blkc = jnp.clip(blk_at(bb, g, j), 0, n_past_blk - 1)
        pg = pt_ref[bb * n_pages + blkc // 2]
        off = pl.multiple_of((blkc % 2) * SEL_BLOCK, SEL_BLOCK)
        return pltpu.make_async_copy(cache_ref.at[pg, pl.ds(off, SEL_BLOCK), pl.ds(2 * LANES, 2 * LANES)],
                                     buf.at[slot, g, pl.ds(j * SEL_BLOCK, SEL_BLOCK), :], sem.at[slot])

    def issue(bb, slot):
        for g in range(NSA_KV_HEADS):
            for j in range(n_top):
                blk_copy(bb, slot, g, j).start()

    @pl.when(b == 0)
    def _():
        issue(0, 0)

    @pl.when(b + 1 < nb)
    def _():
        issue(b + 1, (b + 1) % 2)

    slot = b % 2
    for g in range(NSA_KV_HEADS):
        for j in range(n_top):
            blk_copy(b, slot, g, j).wait()

    q = q_ref[0].astype(F32)
    knew = knew_ref[0]
    k_new = knew[:, :LANES].astype(BF16).astype(F32)
    v_new = knew[:, LANES:].astype(BF16).astype(F32)
    n_keys = n_top * SEL_BLOCK
    key_lane = lax.broadcasted_iota(I32, (1, n_keys), 1)
    key_slot = key_lane // SEL_BLOCK
    key_row = key_lane % SEL_BLOCK
    cur_blk = t_pos // SEL_BLOCK
    o_s = []
    for g in range(NSA_KV_HEADS):
        qg = _pad_rows([q[:, (g * NSA_REP + r) * LANES:(g * NSA_REP + r + 1) * LANES] for r in range(NSA_REP)])
        blkvec = jnp.full((1, n_keys), -1, I32)
        has_cur = jnp.zeros((1, 1), jnp.bool_)
        for j in range(n_top):
            bj = blk_at(b, g, j)
            blkvec = jnp.where(key_slot == j, bj, blkvec)
            has_cur = has_cur | (bj == cur_blk)
        vis = (blkvec >= 0) & (blkvec < n_past_blk) & (blkvec * SEL_BLOCK + key_row <= t_pos)
        kv = buf[slot, g]
        s_past = jnp.where(vis, _dot_nt(qg.astype(BF16), kv[:, :LANES].astype(BF16)), NEG)
        s_new = jnp.where(has_cur, jnp.sum(qg * k_new, axis=-1, keepdims=True), NEG)
        m = jnp.maximum(jnp.max(s_past, axis=-1, keepdims=True), s_new)
        e_p = jnp.where(vis, jnp.exp(s_past - m), 0.0)
        e_n = jnp.where(has_cur, jnp.exp(s_new - m), 0.0)
        den = jnp.maximum(jnp.sum(e_p, axis=-1, keepdims=True) + e_n, 1e-30)
        num = _dot(e_p.astype(BF16), kv[:, LANES:].astype(BF16)) + e_n.astype(BF16).astype(F32) * v_new
        o_s.append(num / den)
    o_c = [jnp.concatenate([oc_ref[0, g * NSA_REP:(g + 1) * NSA_REP, :]] * 2, axis=0) for g in range(NSA_KV_HEADS)]
    o_w = [jnp.concatenate([ow_ref[0, g * NSA_REP:(g + 1) * NSA_REP, :]] * 2, axis=0) for g in range(NSA_KV_HEADS)]

    def store(j, chunk):
        o_ref[0, :, j * LANES:(j + 1) * LANES] = chunk

    _gate_and_pack(store, gate_ref[0], o_c, o_s, o_w, 1)


def _sample_sel_call(page_table, idx, cache3, q3, knew3, gates3, oc, ow, *, t_pos, n_top):
    b, n_pages = page_table.shape
    per_b = lambda shape: pl.BlockSpec((1,) + shape, lambda bi, pt, ix: (bi, 0, 0))
    grid_spec = pltpu.PrefetchScalarGridSpec(
        num_scalar_prefetch=2,
        grid=(b,),
        in_specs=[pl.BlockSpec(memory_space=pl.ANY), per_b((1, Q_PAD)), per_b((1, 2 * LANES)), per_b((1, LANES)),
                  per_b((NSA_HEADS, LANES)), per_b((NSA_HEADS, LANES))],
        out_specs=per_b((1, NSA_WIDTH)),
        scratch_shapes=[pltpu.VMEM((2, NSA_KV_HEADS, n_top * SEL_BLOCK, 2 * LANES), F32),
                        pltpu.SemaphoreType.DMA((2,))],
    )
    return pl.pallas_call(
        functools.partial(_sample_sel_kernel, t_pos=t_pos, n_pages=n_pages, n_top=n_top),
        grid_spec=grid_spec,
        out_shape=jax.ShapeDtypeStruct((b, 1, NSA_WIDTH), F32),
        compiler_params=_cparams("arbitrary"),
        name="sample_sel",
    )(page_table.reshape(-1), idx.reshape(-1), cache3, q3, knew3, gates3, oc, ow)


def _finish_kernel(yp_ref, on_ref, om_ref, x_ref, gm_ref, wo_ref, gf_ref, wr_ref, br_ref,
                   h_ref, hn_ref, route_ref):
    gm = gm_ref[...]
    o1 = POOL_WIDTH
    o2 = o1 + NSA_WIDTH
    mixed = jnp.concatenate([_rms(yp_ref[...]) * gm[:, :o1], _rms(on_ref[...]) * gm[:, o1:o2],
                             _rms(om_ref[...]) * gm[:, o2:]], axis=-1)
    h = x_ref[...] + _dot(mixed.astype(BF16), wo_ref[...])
    h_ref[...] = h
    hn = _rms(h) * gf_ref[...]
    hn_ref[...] = hn
    logits = _dot(hn.astype(BF16), wr_ref[...]) + br_ref[...]
    lane = lax.broadcasted_iota(I32, (1, LANES), 1)
    lane_f = lane.astype(F32)
    is1 = lane < N_EXPERT_GROUPS
    m1 = jnp.max(jnp.where(is1, logits, -jnp.inf), axis=-1, keepdims=True)
    e1 = jnp.where(is1, jnp.exp(logits - m1), 0.0)
    p1 = e1 / jnp.sum(e1, axis=-1, keepdims=True)
    top1_p = jnp.max(p1, axis=-1, keepdims=True)
    grp = jnp.min(jnp.where((p1 == top1_p) & is1, lane_f, float(LANES)), axis=-1, keepdims=True)
    base = N_EXPERT_GROUPS + grp * EXPERTS_PER_GROUP
    in_g = (lane_f >= base) & (lane_f < base + EXPERTS_PER_GROUP)
    l2 = jnp.where(in_g, logits, -jnp.inf)
    v0 = jnp.max(l2, axis=-1, keepdims=True)
    i0 = jnp.min(jnp.where(l2 == v0, lane_f, float(LANES)), axis=-1, keepdims=True)
    l2b = jnp.where(lane_f == i0, -jnp.inf, l2)
    v1 = jnp.max(l2b, axis=-1, keepdims=True)
    i1 = jnp.min(jnp.where(l2b == v1, lane_f, float(LANES)), axis=-1, keepdims=True)
    ex = jnp.exp(v1 - v0)
    w0 = top1_p / (1.0 + ex)
    w1 = top1_p * ex / (1.0 + ex)
    route = jnp.where(lane == 0, i0 - N_EXPERT_GROUPS,
                      jnp.where(lane == 1, i1 - N_EXPERT_GROUPS,
                                jnp.where(lane == 2, w0, jnp.where(lane == 3, w1, 0.0))))
    route_ref[...] = route


def _finish_call(yp, on, om, x2d, fw, *, tm):
    n, dm = x2d.shape
    full = lambda shape: pl.BlockSpec(shape, lambda i: (0,) * len(shape))
    rows = lambda w: pl.BlockSpec((tm, w), lambda i: (i, 0))
    return pl.pallas_call(
        _finish_kernel,
        grid=(n // tm,),
        in_specs=[rows(POOL_WIDTH), rows(NSA_WIDTH), rows(MEM_WIDTH), rows(dm), full((1, dm)), full((dm, dm)),
                  full((1, dm)), full((dm, LANES)), full((1, LANES))],
        out_specs=[rows(dm), rows(dm), rows(LANES)],
        out_shape=[jax.ShapeDtypeStruct((n, dm), F32), jax.ShapeDtypeStruct((n, dm), F32),
                   jax.ShapeDtypeStruct((n, LANES), F32)],
        compiler_params=_cparams("parallel"),
        name="finish",
    )(yp, on, om, x2d, fw["g_mix"], fw["w_out"], fw["g_ffn"], fw["w_r"], fw["b_r"])


def _moe_kernel(te_ref, nv_ref, src_ref, dst_ref, hn_ref, ws_ref, wg_ref, wu_ref, wd_ref, ys_ref,
                xbuf, ybuf, gsem, ssem, *, tm):
    i = pl.program_id(0)
    nt = pl.num_programs(0)

    def row_in(t, slot, r):
        tok = src_ref[t * tm + r]
        return pltpu.make_async_copy(hn_ref.at[pl.ds(tok, 1), :], xbuf.at[slot, pl.ds(r, 1), :], gsem.at[slot])

    def row_out(t, slot, r):
        dst = dst_ref[t * tm + r]
        return pltpu.make_async_copy(ybuf.at[slot, pl.ds(r, 1), :], ys_ref.at[pl.ds(dst, 1), :], ssem.at[slot])

    def for_rows(t, fn):
        def body(r, c):
            fn(r)
            return c
        lax.fori_loop(0, nv_ref[t], body, 0)

    @pl.when(i == 0)
    def _():
        xbuf[...] = jnp.zeros(xbuf.shape, F32)
        for_rows(0, lambda r: row_in(0, 0, r).start())

    slot = i % 2

    @pl.when(i + 1 < nt)
    def _():
        for_rows(i + 1, lambda r: row_in(i + 1, 1 - slot, r).start())

    for_rows(i, lambda r: row_in(i, slot, r).wait())

    @pl.when(i >= 2)
    def _():
        for_rows(i - 2, lambda r: row_out(i - 2, slot, r).wait())

    @pl.when(nv_ref[i] > 0)
    def _():
        x = xbuf[slot].astype(BF16)
        hg = _dot(x, wg_ref[0].astype(BF16))
        hu = _dot(x, wu_ref[0].astype(BF16))
        hmid = hg * jax.nn.sigmoid(hg) * hu * ws_ref[...]
        ybuf[slot] = _dot(hmid.astype(BF16), wd_ref[0].astype(BF16))

    for_rows(i, lambda r: row_out(i, slot, r).start())

    @pl.when(i == nt - 1)
    def _():
        for_rows(i, lambda r: row_out(i, slot, r).wait())

        @pl.when(i >= 1)
        def _():
            for_rows(i - 1, lambda r: row_out(i - 1, 1 - slot, r).wait())


def _moe_call(tile_expert, tile_nvalid, src_tok, dst_row, hn, w_sorted, wg, wu, wd, *, tm, n_out):
    n_tiles = tile_expert.shape[0]
    dm = hn.shape[1]
    ff = wg.shape[2]
    grid_spec = pltpu.PrefetchScalarGridSpec(
        num_scalar_prefetch=4,
        grid=(n_tiles,),
        in_specs=[pl.BlockSpec(memory_space=pl.ANY),
                  pl.BlockSpec((tm, 1), lambda i, te, nv, s, d: (i, 0)),
                  pl.BlockSpec((1, dm, ff), lambda i, te, nv, s, d: (te[i], 0, 0)),
                  pl.BlockSpec((1, dm, ff), lambda i, te, nv, s, d: (te[i], 0, 0)),
                  pl.BlockSpec((1, ff, dm), lambda i, te, nv, s, d: (te[i], 0, 0))],
        out_specs=pl.BlockSpec(memory_space=pl.ANY),
        scratch_shapes=[pltpu.VMEM((2, tm, dm), F32), pltpu.VMEM((2, tm, dm), F32),
                        pltpu.SemaphoreType.DMA((2,)), pltpu.SemaphoreType.DMA((2,))],
    )
    return pl.pallas_call(
        functools.partial(_moe_kernel, tm=tm),
        grid_spec=grid_spec,
        out_shape=jax.ShapeDtypeStruct((n_out, dm), F32),
        compiler_params=_cparams("arbitrary"),
        name="moe_routed",
    )(tile_expert, tile_nvalid, src_tok, dst_row, hn, w_sorted, wg, wu, wd)


def _route_tables(route, tm):
    n = route.shape[0]
    eid = jnp.concatenate([route[:, 0], route[:, 1]]).astype(I32)
    wts = jnp.concatenate([route[:, 2], route[:, 3]])
    onehot = (eid[:, None] == jnp.arange(N_EXPERTS, dtype=I32)[None, :]).astype(I32)
    rank = jnp.sum((jnp.cumsum(onehot, axis=0) - onehot) * onehot, axis=1)
    counts = jnp.sum(onehot, axis=0)
    tiles_per = (counts + tm - 1) // tm
    tile_end = jnp.cumsum(tiles_per)
    tile_start = tile_end - tiles_per
    pos = tile_start[eid] * tm + rank
    n_tiles = (2 * n) // tm + N_EXPERTS
    p_rows = n_tiles * tm
    a = jnp.arange(2 * n, dtype=I32)
    src_tok = jnp.zeros((p_rows,), I32).at[pos].set(a % n)
    dst_row = jnp.zeros((p_rows,), I32).at[pos].set(a)
    w_sorted = jnp.zeros((p_rows,), F32).at[pos].set(wts)
    tj = jnp.arange(n_tiles, dtype=I32)
    tile_expert = jnp.minimum(jnp.searchsorted(tile_end, tj, side="right").astype(I32), N_EXPERTS - 1)
    rows_left = counts[tile_expert] - (tj - tile_start[tile_expert]) * tm
    tile_nvalid = jnp.where(tj < tile_end[-1], jnp.clip(rows_left, 0, tm), 0).astype(I32)
    return tile_expert, tile_nvalid, src_tok, dst_row, w_sorted.reshape(p_rows, 1)


def _combine_kernel(h_ref, y0_ref, y1_ref, o_ref):
    o_ref[...] = h_ref[...] + (y0_ref[...] + y1_ref[...])


def _combine_call(h, ys, *, tm):
    n, dm = h.shape
    nb = n // tm
    return pl.pallas_call(
        _combine_kernel,
        grid=(nb,),
        in_specs=[pl.BlockSpec((tm, dm), lambda i: (i, 0)), pl.BlockSpec((tm, dm), lambda i: (i, 0)),
                  pl.BlockSpec((tm, dm), lambda i: (i + nb, 0))],
        out_specs=pl.BlockSpec((tm, dm), lambda i: (i, 0)),
        out_shape=jax.ShapeDtypeStruct((n, dm), F32),
        compiler_params=_cparams("parallel"),
        name="moe_combine",
    )(h, ys, ys)


def _moe_sample_kernel(hn_ref, route_ref, h_ref, wg_ref, wu_ref, wd_ref, o_ref, acc_ref):
    e = pl.program_id(0)

    @pl.when(e == 0)
    def _():
        acc_ref[...] = jnp.zeros(acc_ref.shape, F32)

    route = route_ref[...]
    ef = e.astype(F32)
    comb = jnp.where(route[:, 0:1] == ef, route[:, 2:3], 0.0) + jnp.where(route[:, 1:2] == ef, route[:, 3:4], 0.0)
    x = hn_ref[...].astype(BF16)
    hg = _dot(x, wg_ref[0].astype(BF16))
    hu = _dot(x, wu_ref[0].astype(BF16))
    hmid = hg * jax.nn.sigmoid(hg) * hu * comb
    acc_ref[...] += _dot(hmid.astype(BF16), wd_ref[0].astype(BF16))

    @pl.when(e == pl.num_programs(0) - 1)
    def _():
        o_ref[...] = h_ref[...] + acc_ref[...]


def _moe_sample_call(hn, route, h, wg, wu, wd):
    n, dm = hn.shape
    ff = wg.shape[2]
    full = lambda shape: pl.BlockSpec(shape, lambda e: (0,) * len(shape))
    return pl.pallas_call(
        _moe_sample_kernel,
        grid=(N_EXPERTS,),
        in_specs=[full((n, dm)), full((n, LANES)), full((n, dm)),
                  pl.BlockSpec((1, dm, ff), lambda e: (e, 0, 0)), pl.BlockSpec((1, dm, ff), lambda e: (e, 0, 0)),
                  pl.BlockSpec((1, ff, dm), lambda e: (e, 0, 0))],
        out_specs=full((n, dm)),
        out_shape=jax.ShapeDtypeStruct((n, dm), F32),
        scratch_shapes=[pltpu.VMEM((n, dm), F32)],
        compiler_params=_cparams("arbitrary"),
        name="moe_sample",
    )(hn, route, h, wg, wu, wd)


def _prep_weights(l, g_attn, w_in, g_q, g_k, pe_cmp, w_cmp1, w_cmp2, w_pool, s_pool, g_mem, w_mem_kv, g_mq, g_mk,
                  g_mix, w_out, g_ffn, w_router1, b_router1, w_router2, b_router2):
    dm = w_in.shape[1]
    w = w_in[l]
    o1 = POOL_WIDTH
    o2 = o1 + NSA_WIDTH
    o3 = o2 + NSA_KV_WIDTH
    o4 = o3 + GATE_WIDTH
    wq = w[:, o1:o2].reshape(dm, NSA_HEADS, 1, HEAD_DIM)
    q_slot = jnp.asarray(np.eye(NSA_KV_HEADS, dtype=np.float32)[np.arange(NSA_HEADS) // NSA_REP])
    wq_pad = (wq * q_slot[None, :, :, None]).reshape(dm, Q_PAD)
    wqm = w[:, o4:].reshape(dm, MEM_HEADS, 1, HEAD_DIM)
    m_slot = jnp.asarray(np.eye(2, dtype=np.float32)[np.arange(MEM_HEADS) % 2])
    wqm_pad = (wqm * m_slot[None, :, :, None]).reshape(dm, QM_PAD)
    wg_pad = jnp.pad(w[:, o3:o4], ((0, 0), (0, LANES - GATE_WIDTH)))
    w_packed = jnp.concatenate([w[:, :o1], wq_pad, w[:, o2:o3], wqm_pad, wg_pad], axis=1).astype(BF16)
    half = ROPE_DIM // 2
    inv = jnp.power(ROPE_THETA, -jnp.arange(half, dtype=F32) * 2.0 / ROPE_DIM)
    d = np.arange(LANES) % HEAD_DIM
    inv_lane = jnp.where(jnp.asarray(d < ROPE_DIM), inv[jnp.asarray(d % half)], 0.0).reshape(1, LANES)
    seg = jnp.asarray((np.arange(LANES)[:, None] // HEAD_DIM == np.arange(LANES)[None, :] // HEAD_DIM), BF16)
    gk = jnp.concatenate([jnp.tile(g_k[l, br], 2) for br in range(N_BRANCH)]).reshape(1, N_BRANCH * LANES)
    proj = dict(g_attn=g_attn[l].reshape(1, dm), w_in=w_packed, gq=jnp.tile(g_q[l], Q_PAD // HEAD_DIM).reshape(1, Q_PAD),
                gk=gk, gmq=jnp.tile(g_mq[l], QM_PAD // HEAD_DIM).reshape(1, QM_PAD), inv=inv_lane, seg=seg)

    n_grp = len(POOL_WINDOWS)
    pg = POOL_WIDTH // n_grp
    w_bd = (jnp.asarray(np.eye(n_grp, dtype=np.float32))[:, None, :, None] * w_pool[l][:, :, None, :]
            ).reshape(POOL_WIDTH, POOL_WIDTH).astype(BF16)
    pool = dict(w=w_bd, s=s_pool[l].reshape(1, POOL_WIDTH))

    n_slab = 2 * NSA_KV_HEADS
    slab_kv = np.arange(n_slab) // NSA_KV_HEADS
    w1 = w_cmp1[l].reshape(2, CMP_BLOCK, HEAD_DIM, CMP_HIDDEN)[slab_kv]
    eye_s = jnp.asarray(np.eye(n_slab, dtype=np.float32))
    w1_bd = jnp.einsum("srdj,st->rsdtj", w1, eye_s)
    kw = CMP_STRIDE * n_slab * HEAD_DIM
    wa = w1_bd[:CMP_STRIDE].reshape(kw, n_slab * CMP_HIDDEN).astype(BF16)
    wb = w1_bd[CMP_STRIDE:].reshape(kw, n_slab * CMP_HIDDEN).astype(BF16)
    pe = jnp.transpose(pe_cmp[l][slab_kv], (1, 0, 2))
    pea = pe[:CMP_STRIDE].reshape(1, kw)
    peb = pe[CMP_STRIDE:].reshape(1, kw)
    w2 = jnp.einsum("sjd,st->sjtd", w_cmp2[l][slab_kv], eye_s).reshape(n_slab * CMP_HIDDEN, n_slab * HEAD_DIM)
    cmp_w = dict(wa=wa, wb=wb, pea=pea, peb=peb, w2=w2.astype(BF16))

    mem = dict(g=g_mem[l].reshape(1, dm), w=w_mem_kv[l].astype(BF16),
               gk=jnp.tile(g_mk[l], MEM_HEADS).reshape(1, MEM_WIDTH))
    w_r = jnp.concatenate([w_router1[l], w_router2[l].reshape(dm, N_EXPERTS)], axis=1)
    w_r = jnp.pad(w_r, ((0, 0), (0, LANES - w_r.shape[1]))).astype(BF16)
    b_r = jnp.concatenate([b_router1[l], b_router2[l].reshape(-1)])
    b_r = jnp.pad(b_r, (0, LANES - b_r.shape[0])).reshape(1, LANES)
    fin = dict(g_mix=g_mix[l].reshape(1, dm), w_out=w_out[l].astype(BF16), g_ffn=g_ffn[l].reshape(1, dm),
               w_r=w_r, b_r=b_r)
    return proj, pool, cmp_w, mem, fin


def _pick(n, prefs):
    for p in prefs:
        if n % p == 0:
            return p
    return n


def kernel(x_prompt, x_sample, cache_kv, cache_win, state_pool, cache_mem_kv, page_table, mem_prompt, g_attn, w_in, g_q, g_k, pe_cmp, w_cmp1, w_cmp2, w_pool, s_pool, g_mem, w_mem_kv, g_mq, g_mk, g_mix, w_out, g_ffn, w_router1, b_router1, w_router2, b_router2, w_gate, w_up, w_down):
    depth = w_in.shape[0]
    bp, t, dm = x_prompt.shape
    bs, ts, _ = x_sample.shape
    n_pages = page_table.shape[1]
    page = cache_kv.shape[2]
    past_len = n_pages * page
    n_win = cache_win.shape[2]
    assert ts == 1 and n_win == WINDOW and past_len % SEL_BLOCK == 0 and t % WINDOW == 0 and t >= WINDOW + 128
    ff = w_gate.shape[-1]
    tq, tk = 128, 512
    n_sel = t // SEL_BLOCK
    poolt = jnp.asarray(np.arange(t // CMP_STRIDE)[None, :] // CMP_PER_SEL == np.arange(n_sel)[:, None], BF16)
    e_mat = jnp.asarray(np.arange(t)[None, :] // SEL_BLOCK == np.arange(n_sel)[:, None], BF16)
    n_cmp_s = past_len // CMP_STRIDE
    n_slot_s = -(-(past_len // SEL_BLOCK + 1) // LANES) * LANES
    pool_s = jnp.asarray(np.arange(n_cmp_s)[:, None] // CMP_PER_SEL == np.arange(n_slot_s)[None, :], BF16)

    hp, hs = x_prompt, x_sample
    outs = [[] for _ in range(7)]
    for l in range(depth):
        proj_w, pool_w, cmp_w, mem_w, fin_w = _prep_weights(
            l, g_attn, w_in, g_q, g_k, pe_cmp, w_cmp1, w_cmp2, w_pool, s_pool, g_mem, w_mem_kv, g_mq, g_mk,
            g_mix, w_out, g_ffn, w_router1, b_router1, w_router2, b_router2)
        wg = w_gate[l].reshape(N_EXPERTS, dm, ff)
        wu = w_up[l].reshape(N_EXPERTS, dm, ff)
        wd = w_down[l].reshape(N_EXPERTS, ff, dm)
        kw_cols = CMP_STRIDE * 2 * LANES

        n = bp * t
        xp2 = hp.reshape(n, dm)
        u, qb, rows4, rowsw, kvb, gates, qmb = _proj_call(xp2, proj_w, seq_len=t, pos0=0, tm=_pick(n, (512, 256, 128)))
        u3 = u.reshape(bp, t, POOL_WIDTH)
        y_pool = _pool_call(u3, pool_w["w"], pool_w["s"], tp=_pick(t, (512, 256, 128)), pos0=0)
        kcvc = _compress_prompt_call(rows4.reshape(bp, t, 4 * LANES), jnp.zeros((bp, 1, kw_cols), F32), cmp_w)
        o_nsa = _nsa_prompt_call(qb.reshape(bp, t, Q_PAD), gates.reshape(bp, t, LANES), kcvc,
                                 kvb.reshape(bp, t, NSA_KV_WIDTH), poolt, e_mat, tq=tq, tk=tk)
        m_len = mem_prompt.shape[1]
        mkv, mkvb = _memkv_call(mem_prompt.reshape(bp * m_len, dm), mem_w["g"], mem_w["w"], mem_w["gk"],
                                proj_w["seg"], tm=_pick(bp * m_len, (256, 128)))
        o_mem = _memattn_call(qmb.reshape(bp, t, QM_PAD), mkvb.reshape(bp, m_len, 2 * MEM_WIDTH),
                              tq=_pick(t, (512, 256, 128)))
        tmf = _pick(n, (512, 256, 128))
        h_p, hn_p, route_p = _finish_call(y_pool.reshape(n, POOL_WIDTH), o_nsa.reshape(n, NSA_WIDTH),
                                          o_mem.reshape(n, MEM_WIDTH), xp2, fin_w, tm=tmf)
        tme = 256
        te, nv, src, dst, wsorted = _route_tables(route_p, tme)
        ys = _moe_call(te, nv, src, dst, hn_p, wsorted, wg, wu, wd, tm=tme, n_out=2 * n)
        y_p = _combine_call(h_p, ys, tm=tmf).reshape(bp, t, dm)

        xs2 = hs.reshape(bs, dm)
        u_s, qb_s, rows4_s, rowsw_s, _, gates_s, qmb_s = _proj_call(xs2, proj_w, seq_len=1, pos0=past_len, tm=bs)
        ext = jnp.concatenate([state_pool[l], u_s[:, None, :]], axis=1)
        y_pool_s = _pool_call(ext, pool_w["w"], pool_w["s"], tp=POOL_STATE + 1, pos0=past_len - POOL_STATE)[:, -1, :]
        cache3 = cache_kv[l].reshape(cache_kv.shape[1], page, N_KV_SLOTS * 2 * HEAD_DIM)
        tail_s = jnp.pad(rows4_s[:, :2 * LANES], ((0, 0), (0, kw_cols - 2 * LANES))).reshape(bs, 1, kw_cols)
        kcvc_s = _compress_sample_call(page_table, cache3, tail_s, cmp_w)
        win_s = jnp.concatenate([cache_win[l].reshape(bs, n_win, 2 * LANES)[:, 1:], rowsw_s[:, None, :]], axis=1)
        mkv_s = cache_mem_kv[l].reshape(bs, cache_mem_kv.shape[2], 2 * MEM_WIDTH)
        oc, ow, om_s, idx = _sample_attn1_call(qb_s.reshape(bs, 1, Q_PAD), kcvc_s, win_s, mkv_s,
                                               qmb_s.reshape(bs, 1, QM_PAD), pool_s, t_pos=past_len)
        n_top = min(SEL_TOPK, past_len // SEL_BLOCK + 1)
        o_nsa_s = _sample_sel_call(page_table, idx, cache3, qb_s.reshape(bs, 1, Q_PAD),
                                   rows4_s[:, 2 * LANES:].reshape(bs, 1, 2 * LANES), gates_s.reshape(bs, 1, LANES),
                                   oc, ow, t_pos=past_len, n_top=n_top)
        h_s, hn_s, route_s = _finish_call(y_pool_s, o_nsa_s.reshape(bs, NSA_WIDTH), om_s.reshape(bs, MEM_WIDTH),
                                          xs2, fin_w, tm=bs)
        y_s = _moe_sample_call(hn_s, route_s, h_s, wg, wu, wd).reshape(bs, 1, dm)

        keep = min(WINDOW, t)
        outs[0].append(rows4.reshape(bp, t, N_KV_SLOTS, NSA_KV_HEADS, HEAD_DIM))
        outs[1].append(rows4_s.reshape(bs, 1, N_KV_SLOTS, NSA_KV_HEADS, HEAD_DIM))
        outs[2].append(rowsw.reshape(bp, t, 2, NSA_KV_HEADS, HEAD_DIM)[:, t - keep:])
        outs[3].append(win_s.reshape(bs, n_win, 2, NSA_KV_HEADS, HEAD_DIM))
        outs[4].append(u3[:, t - POOL_STATE:])
        outs[5].append(ext[:, 1:])
        outs[6].append(mkv.reshape(bp, m_len, 2, MEM_HEADS, HEAD_DIM))
        hp, hs = y_p, y_s
    return (hp, hs) + tuple(jnp.stack(o) for o in outs)
```

```python
import functools

import numpy as np
import jax
import jax.numpy as jnp
from jax import lax
from jax.experimental import pallas as pl
from jax.experimental.pallas import tpu as pltpu

F32 = jnp.float32
BF16 = jnp.bfloat16
I32 = jnp.int32

HEAD_DIM = 64
POOL_WINDOWS = (2, 4, 8, 16)
POOL_STATE = max(POOL_WINDOWS) - 1
NSA_HEADS = 8
NSA_KV_HEADS = 2
NSA_REP = NSA_HEADS // NSA_KV_HEADS
N_BRANCH = 3
CMP_BLOCK = 32
CMP_STRIDE = 16
CMP_HIDDEN = 2 * HEAD_DIM
SEL_BLOCK = 64
SEL_TOPK = 16
CMP_PER_SEL = SEL_BLOCK // CMP_STRIDE
WINDOW = 512
FORCE_BONUS = 1000.0
MEM_HEADS = 4
ROPE_DIM = HEAD_DIM // 4
ROPE_THETA = 500000.0
N_EXPERT_GROUPS = 4
EXPERTS_PER_GROUP = 8
N_EXPERTS = N_EXPERT_GROUPS * EXPERTS_PER_GROUP
EPS = 1e-6
N_KV_SLOTS = 4

LANES = 128
POOL_WIDTH = 256
NSA_WIDTH = NSA_HEADS * HEAD_DIM
NSA_KV_WIDTH = N_BRANCH * 2 * NSA_KV_HEADS * HEAD_DIM
GATE_WIDTH = NSA_HEADS * N_BRANCH
MEM_WIDTH = MEM_HEADS * HEAD_DIM
Q_PAD = NSA_HEADS * LANES
QM_PAD = MEM_HEADS * LANES
C_U = 0
C_Q = C_U + POOL_WIDTH
C_KV = C_Q + Q_PAD
C_QM = C_KV + NSA_KV_WIDTH
C_G = C_QM + QM_PAD
C_END = C_G + LANES

NEG = -1e30
VMEM_LIMIT = 48 * 1024 * 1024

_NT = (((1,), (1,)), ((), ()))


def _cparams(*sem):
    return pltpu.CompilerParams(dimension_semantics=tuple(sem), vmem_limit_bytes=VMEM_LIMIT)


def _dot(a, b):
    return jnp.dot(a, b, preferred_element_type=F32)


def _dot_nt(a, b):
    return lax.dot_general(a, b, _NT, preferred_element_type=F32)


def _rms(x):
    return x * lax.rsqrt(jnp.mean(x * x, axis=-1, keepdims=True) + EPS)


def _masked_softmax(s, mask):
    sm = jnp.where(mask, s, NEG)
    m = jnp.max(sm, axis=-1, keepdims=True)
    e = jnp.where(mask, jnp.exp(sm - m), 0.0)
    return e / jnp.maximum(jnp.sum(e, axis=-1, keepdims=True), 1e-30)


def _split3(x):
    hi = x.astype(BF16)
    r1 = x - hi.astype(F32)
    mid = r1.astype(BF16)
    lo = (r1 - mid.astype(F32)).astype(BF16)
    return hi, mid, lo


def _proj_kernel(x_ref, ga_ref, w_ref, gq_ref, gk_ref, gmq_ref, inv_ref, seg_ref,
                 u_ref, q_ref, rows4_ref, rowsw_ref, kvb_ref, gates_ref, qm_ref, *, tm, seq_len, pos0):
    i = pl.program_id(0)
    a = _rms(x_ref[...]) * ga_ref[...]
    z = _dot(a.astype(BF16), w_ref[...])
    u_ref[...] = z[:, C_U:C_U + POOL_WIDTH]

    row = i * tm + lax.broadcasted_iota(I32, (tm, 1), 0)
    pos = (pos0 + row % seq_len).astype(F32)
    ang = pos * inv_ref[...]
    d = lax.broadcasted_iota(I32, (1, LANES), 1) % HEAD_DIM
    cos = jnp.cos(ang)
    sin = jnp.sin(ang)
    half = ROPE_DIM // 2
    s_next = jnp.where(d < half, -sin, 0.0)
    s_prev = jnp.where((d >= half) & (d < ROPE_DIM), sin, 0.0)
    seg = seg_ref[...]

    def head_norm(xc, g):
        ssq = _dot((xc * xc).astype(BF16), seg)
        return xc * lax.rsqrt(ssq * (1.0 / HEAD_DIM) + EPS) * g

    def rope(xc):
        return xc * cos + pltpu.roll(xc, LANES - half, 1) * s_next + pltpu.roll(xc, half, 1) * s_prev

    scale = HEAD_DIM ** -0.5
    for c in range(NSA_HEADS):
        sl = slice(c * LANES, (c + 1) * LANES)
        qc = rope(head_norm(z[:, C_Q + c * LANES:C_Q + (c + 1) * LANES], gq_ref[:, sl]))
        q_ref[:, sl] = (qc * scale).astype(BF16)
    for br in range(N_BRANCH):
        k0 = C_KV + br * 2 * LANES
        kn = rope(head_norm(z[:, k0:k0 + LANES], gk_ref[:, br * LANES:(br + 1) * LANES]))
        vv = z[:, k0 + LANES:k0 + 2 * LANES]
        kvb_ref[:, br * 2 * LANES:br * 2 * LANES + LANES] = kn.astype(BF16)
        kvb_ref[:, br * 2 * LANES + LANES:(br + 1) * 2 * LANES] = vv.astype(BF16)
        if br < 2:
            rows4_ref[:, br * 2 * LANES:br * 2 * LANES + LANES] = kn
            rows4_ref[:, br * 2 * LANES + LANES:(br + 1) * 2 * LANES] = vv
        else:
            rowsw_ref[:, :LANES] = kn
            rowsw_ref[:, LANES:] = vv
    for c in range(MEM_HEADS):
        sl = slice(c * LANES, (c + 1) * LANES)
        qmc = head_norm(z[:, C_QM + c * LANES:C_QM + (c + 1) * LANES], gmq_ref[:, sl])
        qm_ref[:, sl] = (qmc * scale).astype(BF16)
    gates_ref[...] = jax.nn.sigmoid(z[:, C_G:C_END])


def _proj_call(x2d, pw, *, seq_len, pos0, tm):
    n, dm = x2d.shape
    full = lambda shape: pl.BlockSpec(shape, lambda i: (0,) * len(shape))
    rows = lambda w: pl.BlockSpec((tm, w), lambda i: (i, 0))
    outs = [(POOL_WIDTH, F32), (Q_PAD, BF16), (4 * LANES, F32), (2 * LANES, F32), (NSA_KV_WIDTH, BF16),
            (LANES, F32), (QM_PAD, BF16)]
    return pl.pallas_call(
        functools.partial(_proj_kernel, tm=tm, seq_len=seq_len, pos0=pos0),
        grid=(n // tm,),
        in_specs=[rows(dm), full((1, dm)), full((dm, C_END)), full((1, Q_PAD)), full((1, N_BRANCH * LANES)),
                  full((1, QM_PAD)), full((1, LANES)), full((LANES, LANES))],
        out_specs=[rows(w) for w, _ in outs],
        out_shape=[jax.ShapeDtypeStruct((n, w), dt) for w, dt in outs],
        compiler_params=_cparams("parallel"),
        name="proj",
    )(x2d, pw["g_attn"], pw["w_in"], pw["gq"], pw["gk"], pw["gmq"], pw["inv"], pw["seg"])


def _pool_kernel(u_ref, halo_ref, w_ref, s_ref, y_ref, *, tp, pos0):
    i = pl.program_id(1)
    u = u_ref[0]
    halo = halo_ref[0] * (i > 0).astype(F32)
    n_h = POOL_STATE + 1
    ext = jnp.concatenate([halo, u], axis=0)
    sums = {1: ext}
    w = 1
    while w < max(POOL_WINDOWS):
        sums[2 * w] = sums[w] + pltpu.roll(sums[w], w, 0)
        w *= 2
    pos = pos0 + i * tp + lax.broadcasted_iota(I32, (tp, 1), 0)
    lane_grp = lax.broadcasted_iota(I32, (1, POOL_WIDTH), 1) // (POOL_WIDTH // len(POOL_WINDOWS))
    mean = jnp.zeros((tp, POOL_WIDTH), F32)
    for gi, wdw in enumerate(POOL_WINDOWS):
        cnt = jnp.minimum(pos + 1, wdw).astype(F32)
        mean = jnp.where(lane_grp == gi, sums[wdw][n_h:] / cnt, mean)
    r = mean - u
    y_ref[0] = _dot(r.astype(BF16), w_ref[...]) * s_ref[...]


def _pool_call(u3, w_bd, s_pool, *, tp, pos0):
    b, t, c = u3.shape
    n_h = POOL_STATE + 1
    return pl.pallas_call(
        functools.partial(_pool_kernel, tp=tp, pos0=pos0),
        grid=(b, t // tp),
        in_specs=[pl.BlockSpec((1, tp, c), lambda bi, i: (bi, i, 0)),
                  pl.BlockSpec((1, n_h, c), lambda bi, i: (bi, jnp.maximum(i * (tp // n_h) - 1, 0), 0)),
                  pl.BlockSpec((c, c), lambda bi, i: (0, 0)),
                  pl.BlockSpec((1, c), lambda bi, i: (0, 0))],
        out_specs=pl.BlockSpec((1, tp, c), lambda bi, i: (bi, i, 0)),
        out_shape=jax.ShapeDtypeStruct((b, t, c), F32),
        compiler_params=_cparams("parallel", "parallel"),
        name="pool",
    )(u3, u3, w_bd, s_pool)


def _compress_core(load_rows, tail_ref, pea_ref, peb_ref, wa_ref, wb_ref, w2_ref, n):
    xa = jnp.concatenate([load_rows(r, c) for r in range(CMP_STRIDE) for c in range(2)], axis=1)
    first = _dot((xa + pea_ref[...]).astype(BF16), wa_ref[...])
    second = _dot((xa + peb_ref[...]).astype(BF16), wb_ref[...])
    tail = jnp.broadcast_to(tail_ref[0] + peb_ref[...], (8, xa.shape[1]))
    second_tail = _dot(tail.astype(BF16), wb_ref[...])
    shifted = pltpu.roll(second, n - 1, 0)
    rowid = lax.broadcasted_iota(I32, (n, 1), 0)
    h = first + jnp.where(rowid == n - 1, second_tail[0:1, :], shifted)
    h = jax.nn.gelu(h, approximate=True)
    return _dot(h.astype(BF16), w2_ref[...])


def _compress_prompt_kernel(k_ref, v_ref, tail_ref, pea_ref, peb_ref, wa_ref, wb_ref, w2_ref, out_ref, *, n):
    load = lambda r, c: (k_ref, v_ref)[c][0, pl.ds(r, n, stride=CMP_STRIDE), :]
    out_ref[0] = _compress_core(load, tail_ref, pea_ref, peb_ref, wa_ref, wb_ref, w2_ref, n).astype(BF16)


def _compress_prompt_call(rows4_3d, tail, cw):
    b, t, _ = rows4_3d.shape
    n = t // CMP_STRIDE
    kw = CMP_STRIDE * 2 * LANES
    full = lambda shape: pl.BlockSpec(shape, lambda bi: (0,) * len(shape))
    return pl.pallas_call(
        functools.partial(_compress_prompt_kernel, n=n),
        grid=(b,),
        in_specs=[pl.BlockSpec((1, t, LANES), lambda bi: (bi, 0, 0)),
                  pl.BlockSpec((1, t, LANES), lambda bi: (bi, 0, 1)),
                  pl.BlockSpec((1, 1, kw), lambda bi: (bi, 0, 0)),
                  full((1, kw)), full((1, kw)), full((kw, 4 * CMP_HIDDEN)), full((kw, 4 * CMP_HIDDEN)),
                  full((4 * CMP_HIDDEN, 2 * LANES))],
        out_specs=pl.BlockSpec((1, n, 2 * LANES), lambda bi: (bi, 0, 0)),
        out_shape=jax.ShapeDtypeStruct((b, n, 2 * LANES), BF16),
        compiler_params=_cparams("parallel"),
        name="compress_prompt",
    )(rows4_3d, rows4_3d, tail, cw["pea"], cw["peb"], cw["wa"], cw["wb"], cw["w2"])


def _compress_sample_kernel(pt_ref, cache_ref, tail_ref, pea_ref, peb_ref, wa_ref, wb_ref, w2_ref, out_ref,
                            buf, sem, *, n, n_pages, page):
    b = pl.program_id(0)
    nb = pl.num_programs(0)

    def page_copy(bb, slot, p, c):
        return pltpu.make_async_copy(cache_ref.at[pt_ref[bb * n_pages + p], :, pl.ds(c * LANES, LANES)],
                                     buf.at[slot, c, pl.ds(p * page, page), :], sem.at[slot])

    def for_all(fn):
        for p in range(n_pages):
            for c in range(2):
                fn(p, c)

    def issue(bb, slot):
        for_all(lambda p, c: page_copy(bb, slot, p, c).start())

    @pl.when(b == 0)
    def _():
        issue(0, 0)

    @pl.when(b + 1 < nb)
    def _():
        issue(b + 1, (b + 1) % 2)

    slot = b % 2
    for_all(lambda p, c: page_copy(b, slot, p, c).wait())
    load = lambda r, c: buf[slot, c, pl.ds(r, n, stride=CMP_STRIDE), :]
    out_ref[0] = _compress_core(load, tail_ref, pea_ref, peb_ref, wa_ref, wb_ref, w2_ref, n).astype(BF16)


def _compress_sample_call(page_table, cache3, tail, cw):
    b, n_pages = page_table.shape
    page = cache3.shape[1]
    n = n_pages * page // CMP_STRIDE
    kw = CMP_STRIDE * 2 * LANES
    full = lambda shape: pl.BlockSpec(shape, lambda bi, pt: (0,) * len(shape))
    grid_spec = pltpu.PrefetchScalarGridSpec(
        num_scalar_prefetch=1,
        grid=(b,),
        in_specs=[pl.BlockSpec(memory_space=pl.ANY),
                  pl.BlockSpec((1, 1, kw), lambda bi, pt: (bi, 0, 0)),
                  full((1, kw)), full((1, kw)), full((kw, 4 * CMP_HIDDEN)), full((kw, 4 * CMP_HIDDEN)),
                  full((4 * CMP_HIDDEN, 2 * LANES))],
        out_specs=pl.BlockSpec((1, n, 2 * LANES), lambda bi, pt: (bi, 0, 0)),
        scratch_shapes=[pltpu.VMEM((2, 2, n_pages * page, LANES), F32), pltpu.SemaphoreType.DMA((2,))],
    )
    return pl.pallas_call(
        functools.partial(_compress_sample_kernel, n=n, n_pages=n_pages, page=page),
        grid_spec=grid_spec,
        out_shape=jax.ShapeDtypeStruct((b, n, 2 * LANES), BF16),
        compiler_params=_cparams("arbitrary"),
        name="compress_sample",
    )(page_table.reshape(-1), cache3, tail, cw["pea"], cw["peb"], cw["wa"], cw["wb"], cw["w2"])


def _topk_mask(score, ids, n_top, axis):
    sel = jnp.zeros(score.shape, jnp.bool_)
    work = score
    firsts = []
    big = float(score.shape[axis])
    for _ in range(n_top):
        m = jnp.max(work, axis=axis, keepdims=True)
        first = jnp.min(jnp.where(work == m, ids, big), axis=axis, keepdims=True)
        pick = (ids == first) & (m > -jnp.inf)
        sel = sel | pick
        work = jnp.where(pick, -jnp.inf, work)
        firsts.append(jnp.where(m > -jnp.inf, first, -1.0))
    return sel, firsts


def _gate_and_pack(o_ref_store, gates, o_c, o_s, o_w, rows):
    lane = lax.broadcasted_iota(I32, (1, LANES), 1)
    heads = []
    for h in range(NSA_HEADS):
        g, r = divmod(h, NSA_REP)
        rs = slice(r * rows, (r + 1) * rows)
        gc, gs, gw = (gates[:, N_BRANCH * h + k:N_BRANCH * h + k + 1] for k in range(N_BRANCH))
        heads.append(gc * o_c[g][rs] + gs * o_s[g][rs] + gw * o_w[g][rs])
    for j in range(NSA_HEADS // 2):
        a, b = heads[2 * j], heads[2 * j + 1]
        if (2 * j) // NSA_REP == 0:
            chunk = jnp.where(lane < HEAD_DIM, a, pltpu.roll(b, HEAD_DIM, 1))
        else:
            chunk = jnp.where(lane < HEAD_DIM, pltpu.roll(a, HEAD_DIM, 1), b)
        o_ref_store(j, chunk)


def _nsa_prompt_kernel(q_ref, gate_ref, kc_ref, kv_ref, poolt_ref, e_ref, o_ref, selx_ref, acc_ref,
                       *, tq, tk, t_len):
    i = pl.program_id(1)
    s0 = i * tq
    rows4 = NSA_REP * tq
    tpos = s0 + lax.broadcasted_iota(I32, (tq, 1), 0)
    tpos4 = jnp.concatenate([tpos] * NSA_REP, axis=0)
    n_cmp = kc_ref.shape[1]
    kc128 = kc_ref[0, :, :LANES]
    vc128 = kc_ref[0, :, LANES:]
    kc_end = lax.broadcasted_iota(I32, (1, n_cmp), 1) * CMP_STRIDE + (CMP_BLOCK - 1)
    mask_c = kc_end <= tpos4
    n_sel = t_len // SEL_BLOCK
    blk = lax.broadcasted_iota(I32, (n_sel, 1), 0)
    blk_f = blk.astype(F32)
    tq_lane = s0 + lax.broadcasted_iota(I32, (1, tq), 1)
    cur = tq_lane // SEL_BLOCK
    valid = blk * SEL_BLOCK <= tq_lane
    forced = (blk == 0) | (blk == cur) | (blk == cur - 1)
    w_start = pl.multiple_of(jnp.maximum(s0 - WINDOW, 0), LANES)
    w_len = WINDOW + tq
    kpos_w = w_start + lax.broadcasted_iota(I32, (1, w_len), 1)
    dist = tpos4 - kpos_w
    mask_w = (dist >= 0) & (dist < WINDOW)
    n_ch = (s0 + tq + tk - 1) // tk

    o_c, o_s, o_w = [], [], []
    for g in range(NSA_KV_HEADS):
        qg = jnp.concatenate([q_ref[0, :, (g * NSA_REP + r) * LANES:(g * NSA_REP + r + 1) * LANES]
                              for r in range(NSA_REP)], axis=0)
        p_c = _masked_softmax(_dot_nt(qg, kc128), mask_c)
        o_c.append(_dot(p_c.astype(BF16), vc128))
        imp = p_c[0:tq]
        for r in range(1, NSA_REP):
            imp = imp + p_c[r * tq:(r + 1) * tq]
        imp_t = sum(_dot_nt(poolt_ref[...], piece) for piece in _split3(imp))
        score = jnp.where(valid, imp_t + jnp.where(forced, FORCE_BONUS, 0.0), -jnp.inf)
        sel_t, _ = _topk_mask(score, blk_f, min(SEL_TOPK, n_sel), 0)
        sel_pad = jnp.concatenate([sel_t.astype(F32), jnp.zeros(((-n_sel) % LANES, tq), F32)], axis=0)
        sel = sel_pad.T[:, :n_sel].astype(BF16)
        for c in range(t_len // tk):
            selx_ref[c] = _dot(sel, e_ref[:, c * tk:(c + 1) * tk])

        acc_ref[...] = jnp.zeros(acc_ref.shape, F32)

        def body(c, carry):
            m, l = carry
            k0 = pl.multiple_of(c * tk, tk)
            ks = kv_ref[0, pl.ds(k0, tk), 2 * LANES:3 * LANES]
            vs = kv_ref[0, pl.ds(k0, tk), 3 * LANES:4 * LANES]
            s = _dot_nt(qg, ks)
            kpos = c * tk + lax.broadcasted_iota(I32, (1, tk), 1)
            vis = (jnp.concatenate([selx_ref[c]] * NSA_REP, axis=0) > 0.5) & (kpos <= tpos4)
            s = jnp.where(vis, s, NEG)
            m_new = jnp.maximum(m, jnp.max(s, axis=-1, keepdims=True))
            alpha = jnp.exp(m - m_new)
            p = jnp.where(vis, jnp.exp(s - m_new), 0.0)
            l_new = alpha * l + jnp.sum(p, axis=-1, keepdims=True)
            acc_ref[...] = alpha * acc_ref[...] + _dot(p.astype(BF16), vs)
            return m_new, l_new

        _, l_fin = lax.fori_loop(0, n_ch, body, (jnp.full((rows4, 1), NEG, F32), jnp.zeros((rows4, 1), F32)))
        o_s.append(acc_ref[...] / jnp.maximum(l_fin, 1e-30))

        kw = kv_ref[0, pl.ds(w_start, w_len), 4 * LANES:5 * LANES]
        vw = kv_ref[0, pl.ds(w_start, w_len), 5 * LANES:6 * LANES]
        p_w = _masked_softmax(_dot_nt(qg, kw), mask_w)
        o_w.append(_dot(p_w.astype(BF16), vw))

    def store(j, chunk):
        o_ref[0, :, j * LANES:(j + 1) * LANES] = chunk

    _gate_and_pack(store, gate_ref[0], o_c, o_s, o_w, tq)


def _nsa_prompt_call(q3, gates3, kcvc, kvb3, poolt, e_mat, *, tq, tk):
    b, t, _ = q3.shape
    n_cmp = kcvc.shape[1]
    return pl.pallas_call(
        functools.partial(_nsa_prompt_kernel, tq=tq, tk=tk, t_len=t),
        grid=(b, t // tq),
        in_specs=[pl.BlockSpec((1, tq, Q_PAD), lambda bi, i: (bi, i, 0)),
                  pl.BlockSpec((1, tq, LANES), lambda bi, i: (bi, i, 0)),
                  pl.BlockSpec((1, n_cmp, 2 * LANES), lambda bi, i: (bi, 0, 0)),
                  pl.BlockSpec((1, t, NSA_KV_WIDTH), lambda bi, i: (bi, 0, 0)),
                  pl.BlockSpec(poolt.shape, lambda bi, i: (0, 0)),
                  pl.BlockSpec(e_mat.shape, lambda bi, i: (0, 0))],
        out_specs=pl.BlockSpec((1, tq, NSA_WIDTH), lambda bi, i: (bi, i, 0)),
        out_shape=jax.ShapeDtypeStruct((b, t, NSA_WIDTH), F32),
        scratch_shapes=[pltpu.VMEM((t // tk, tq, tk), F32), pltpu.VMEM((NSA_REP * tq, LANES), F32)],
        compiler_params=_cparams("parallel", "arbitrary"),
        name="nsa_prompt",
    )(q3, gates3, kcvc, kvb3, poolt, e_mat)


def _memkv_kernel(x_ref, g_ref, w_ref, gk_ref, seg_ref, o_ref, ob_ref):
    a = _rms(x_ref[...]) * g_ref[...]
    z = _dot(a.astype(BF16), w_ref[...])
    for c in range(MEM_WIDTH // LANES):
        sl = slice(c * LANES, (c + 1) * LANES)
        xc = z[:, sl]
        ssq = _dot((xc * xc).astype(BF16), seg_ref[...])
        kn = xc * lax.rsqrt(ssq * (1.0 / HEAD_DIM) + EPS) * gk_ref[:, sl]
        o_ref[:, sl] = kn
        ob_ref[:, sl] = kn.astype(BF16)
    o_ref[:, MEM_WIDTH:] = z[:, MEM_WIDTH:]
    ob_ref[:, MEM_WIDTH:] = z[:, MEM_WIDTH:].astype(BF16)


def _memkv_call(mem2d, g_mem, w_mem, gmk, seg, *, tm):
    n, dm = mem2d.shape
    full = lambda shape: pl.BlockSpec(shape, lambda i: (0,) * len(shape))
    return pl.pallas_call(
        _memkv_kernel,
        grid=(n // tm,),
        in_specs=[pl.BlockSpec((tm, dm), lambda i: (i, 0)), full((1, dm)), full((dm, 2 * MEM_WIDTH)),
                  full((1, MEM_WIDTH)), full((LANES, LANES))],
        out_specs=[pl.BlockSpec((tm, 2 * MEM_WIDTH), lambda i: (i, 0))] * 2,
        out_shape=[jax.ShapeDtypeStruct((n, 2 * MEM_WIDTH), F32), jax.ShapeDtypeStruct((n, 2 * MEM_WIDTH), BF16)],
        compiler_params=_cparams("parallel"),
        name="memkv",
    )(mem2d, g_mem, w_mem, gmk, seg)


def _mem_attend(qm, mkv, rows):
    lane = lax.broadcasted_iota(I32, (1, LANES), 1)
    chunks = []
    for j in range(MEM_HEADS // 2):
        k128 = mkv[:, j * LANES:(j + 1) * LANES]
        v128 = mkv[:, MEM_WIDTH + j * LANES:MEM_WIDTH + (j + 1) * LANES]
        q2 = jnp.concatenate([qm[2 * j], qm[2 * j + 1]], axis=0)
        s = _dot_nt(q2, k128)
        e = jnp.exp(s - jnp.max(s, axis=-1, keepdims=True))
        p = e / jnp.sum(e, axis=-1, keepdims=True)
        o = _dot(p.astype(BF16), v128)
        chunks.append(jnp.where(lane < HEAD_DIM, o[:rows], o[rows:2 * rows]))
    return chunks


def _memattn_kernel(qm_ref, mkv_ref, o_ref, *, tq):
    qm = [qm_ref[0, :, h * LANES:(h + 1) * LANES] for h in range(MEM_HEADS)]
    for j, chunk in enumerate(_mem_attend(qm, mkv_ref[0], tq)):
        o_ref[0, :, j * LANES:(j + 1) * LANES] = chunk


def _memattn_call(qm3, mkv3, *, tq):
    b, t, _ = qm3.shape
    m = mkv3.shape[1]
    return pl.pallas_call(
        functools.partial(_memattn_kernel, tq=tq),
        grid=(b, t // tq),
        in_specs=[pl.BlockSpec((1, tq, QM_PAD), lambda bi, i: (bi, i, 0)),
                  pl.BlockSpec((1, m, 2 * MEM_WIDTH), lambda bi, i: (bi, 0, 0))],
        out_specs=pl.BlockSpec((1, tq, MEM_WIDTH), lambda bi, i: (bi, i, 0)),
        out_shape=jax.ShapeDtypeStruct((b, t, MEM_WIDTH), F32),
        compiler_params=_cparams("parallel", "parallel"),
        name="memattn",
    )(qm3, mkv3)


def _pad_rows(rows_list):
    x = jnp.concatenate(rows_list, axis=0)
    return jnp.concatenate([x, jnp.zeros((8 - x.shape[0], x.shape[1]), x.dtype)], axis=0)


def _sample_attn1_kernel(q_ref, kc_ref, win_ref, mkv_ref, qm_ref, pool_ref,
                         oc_ref, ow_ref, om_ref, idx_ref, *, t_pos, n_win):
    q = q_ref[0].astype(F32)
    n_cmp = kc_ref.shape[1]
    kc128 = kc_ref[0, :, :LANES]
    vc128 = kc_ref[0, :, LANES:]
    kw128 = win_ref[0, :, :LANES].astype(BF16)
    vw128 = win_ref[0, :, LANES:].astype(BF16)
    kc_end = lax.broadcasted_iota(I32, (1, n_cmp), 1) * CMP_STRIDE + (CMP_BLOCK - 1)
    mask_c = kc_end <= t_pos
    kw_pos = t_pos - (n_win - 1) + lax.broadcasted_iota(I32, (1, n_win), 1)
    mask_w = (t_pos - kw_pos >= 0) & (t_pos - kw_pos < WINDOW) & (kw_pos >= 0)
    n_slot = pool_ref.shape[1]
    blk = lax.broadcasted_iota(I32, (1, n_slot), 1)
    cur = t_pos // SEL_BLOCK
    valid = blk * SEL_BLOCK <= t_pos
    forced = (blk == 0) | (blk == cur) | (blk == cur - 1)
    lane = lax.broadcasted_iota(I32, (1, LANES), 1)
    for g in range(NSA_KV_HEADS):
        qg = _pad_rows([q[:, (g * NSA_REP + r) * LANES:(g * NSA_REP + r + 1) * LANES]
                        for r in range(NSA_REP)]).astype(BF16)
        p_c = _masked_softmax(_dot_nt(qg, kc128), mask_c)
        oc_ref[0, g * NSA_REP:(g + 1) * NSA_REP, :] = _dot(p_c.astype(BF16), vc128)[:NSA_REP]
        imp = jnp.sum(p_c[:NSA_REP], axis=0, keepdims=True)
        imp8 = jnp.broadcast_to(imp, (8, n_cmp))
        imp_b = sum(_dot(piece, pool_ref[...]) for piece in _split3(imp8))[0:1]
        score = jnp.where(valid, imp_b + jnp.where(forced, FORCE_BONUS, 0.0), -jnp.inf)
        n_valid_blocks = t_pos // SEL_BLOCK + 1
        _, firsts = _topk_mask(score, blk.astype(F32), min(SEL_TOPK, n_valid_blocks), 1)
        idxv = jnp.full((1, LANES), -1, I32)
        for j, f in enumerate(firsts):
            idxv = jnp.where(lane == j, f.astype(I32), idxv)
        idx_ref[0, g:g + 1, :] = idxv
        p_w = _masked_softmax(_dot_nt(qg, kw128), mask_w)
        ow_ref[0, g * NSA_REP:(g + 1) * NSA_REP, :] = _dot(p_w.astype(BF16), vw128)[:NSA_REP]
    qm = qm_ref[0].astype(F32)
    qmh = []
    for h in range(MEM_HEADS):
        row = qm[:, h * LANES:(h + 1) * LANES]
        qmh.append(jnp.concatenate([row, jnp.zeros((7, LANES), F32)], axis=0).astype(BF16))
    for j, chunk in enumerate(_mem_attend(qmh, mkv_ref[0].astype(BF16), 8)):
        om_ref[0, :, j * LANES:(j + 1) * LANES] = chunk[0:1]


def _sample_attn1_call(q3, kcvc, win3, mkv3, qm3, pool_mat, *, t_pos):
    b = q3.shape[0]
    n_cmp = kcvc.shape[1]
    n_win = win3.shape[1]
    m = mkv3.shape[1]
    per_b = lambda shape: pl.BlockSpec((1,) + shape, lambda bi: (bi, 0, 0))
    return pl.pallas_call(
        functools.partial(_sample_attn1_kernel, t_pos=t_pos, n_win=n_win),
        grid=(b,),
        in_specs=[per_b((1, Q_PAD)), per_b((n_cmp, 2 * LANES)), per_b((n_win, 2 * LANES)),
                  per_b((m, 2 * MEM_WIDTH)), per_b((1, QM_PAD)),
                  pl.BlockSpec(pool_mat.shape, lambda bi: (0, 0))],
        out_specs=[per_b((NSA_HEADS, LANES)), per_b((NSA_HEADS, LANES)), per_b((1, MEM_WIDTH)),
                   per_b((NSA_KV_HEADS, LANES))],
        out_shape=[jax.ShapeDtypeStruct((b, NSA_HEADS, LANES), F32), jax.ShapeDtypeStruct((b, NSA_HEADS, LANES), F32),
                   jax.ShapeDtypeStruct((b, 1, MEM_WIDTH), F32), jax.ShapeDtypeStruct((b, NSA_KV_HEADS, LANES), I32)],
        compiler_params=_cparams("parallel"),
        name="sample_attn1",
    )(q3, kcvc, win3, mkv3, qm3, pool_mat)


def _sample_sel_kernel(pt_ref, idx_ref, cache_ref, q_ref, knew_ref, gate_ref, oc_ref, ow_ref, o_ref,
                       buf, sem, *, t_pos, n_pages, n_top):
    b = pl.program_id(0)
    nb = pl.num_programs(0)
    n_past_blk = n_pages * 2
    idx_stride = NSA_KV_HEADS * LANES

    def blk_at(bb, g, j):
        return idx_ref[bb * idx_stride + g * LANES + j]

    def blk_copy(bb, slot, g, j):
        blkc = jnp.clip(blk_at(bb, g, j), 0, n_past_blk - 1)
        pg = pt_ref[bb * n_pages + blkc // 2]
        off = pl.multiple_of((blkc % 2) * SEL_BLOCK, SEL_BLOCK)
        return pltpu.make_async_copy(cache_ref.at[pg, pl.ds(off, SEL_BLOCK), pl.ds(2 * LANES, 2 * LANES)],
                                     buf.at[slot, g, pl.ds(j * SEL_BLOCK, SEL_BLOCK), :], sem.at[slot])

    def for_all(fn):
        for g in range(NSA_KV_HEADS):
            for j in range(n_top):
                fn(g, j)

    def issue(bb, slot):
        for_all(lambda g, j: blk_copy(bb, slot, g, j).start())

    @pl.when(b == 0)
    def _():
        issue(0, 0)

    @pl.when(b + 1 < nb)
    def _():
        issue(b + 1, (b + 1) % 2)

    slot = b % 2
    for_all(lambda g, j: blk_copy(b, slot, g, j).wait())

    q = q_ref[0].astype(F32)
    knew = knew_ref[0]
    k_new = knew[:, :LANES].astype(BF16).astype(F32)
    v_new = knew[:, LANES:].astype(BF16).astype(F32)
    n_keys = n_top * SEL_BLOCK
    key_lane = lax.broadcasted_iota(I32, (1, n_keys), 1)
    key_slot = key_lane // SEL_BLOCK
    key_row = key_lane % SEL_BLOCK
    cur_blk = t_pos // SEL_BLOCK
    o_s = []
    for g in range(NSA_KV_HEADS):
        qg = _pad_rows([q[:, (g * NSA_REP + r) * LANES:(g * NSA_REP + r + 1) * LANES] for r in range(NSA_REP)])
        blkvec = jnp.full((1, n_keys), -1, I32)
        has_cur = jnp.zeros((1, 1), jnp.bool_)
        for j in range(n_top):
            bj = blk_at(b, g, j)
            blkvec = jnp.where(key_slot == j, bj, blkvec)
            has_cur = has_cur | (bj == cur_blk)
        vis = (blkvec >= 0) & (blkvec < n_past_blk) & (blkvec * SEL_BLOCK + key_row <= t_pos)
        kv = buf[slot, g]
        s_past = jnp.where(vis, _dot_nt(qg.astype(BF16), kv[:, :LANES].astype(BF16)), NEG)
        s_new = jnp.where(has_cur, jnp.sum(qg * k_new, axis=-1, keepdims=True), NEG)
        m = jnp.maximum(jnp.max(s_past, axis=-1, keepdims=True), s_new)
        e_p = jnp.where(vis, jnp.exp(s_past - m), 0.0)
        e_n = jnp.where(has_cur, jnp.exp(s_new - m), 0.0)
        den = jnp.maximum(jnp.sum(e_p, axis=-1, keepdims=True) + e_n, 1e-30)
        num = _dot(e_p.astype(BF16), kv[:, LANES:].astype(BF16)) + e_n.astype(BF16).astype(F32) * v_new
        o_s.append(num / den)
    o_c = [jnp.concatenate([oc_ref[0, g * NSA_REP:(g + 1) * NSA_REP, :]] * 2, axis=0) for g in range(NSA_KV_HEADS)]
    o_w = [jnp.concatenate([ow_ref[0, g * NSA_REP:(g + 1) * NSA_REP, :]] * 2, axis=0) for g in range(NSA_KV_HEADS)]

    def store(j, chunk):
        o_ref[0, :, j * LANES:(j + 1) * LANES] = chunk

    _gate_and_pack(store, gate_ref[0], o_c, o_s, o_w, 1)


def _sample_sel_call(page_table, idx, cache3, q3, knew3, gates3, oc, ow, *, t_pos, n_top):
    b, n_pages = page_table.shape
    per_b = lambda shape: pl.BlockSpec((1,) + shape, lambda bi, pt, ix: (bi, 0, 0))
    grid_spec = pltpu.PrefetchScalarGridSpec(
        num_scalar_prefetch=2,
        grid=(b,),
        in_specs=[pl.BlockSpec(memory_space=pl.ANY), per_b((1, Q_PAD)), per_b((1, 2 * LANES)), per_b((1, LANES)),
                  per_b((NSA_HEADS, LANES)), per_b((NSA_HEADS, LANES))],
        out_specs=per_b((1, NSA_WIDTH)),
        scratch_shapes=[pltpu.VMEM((2, NSA_KV_HEADS, n_top * SEL_BLOCK, 2 * LANES), F32),
                        pltpu.SemaphoreType.DMA((2,))],
    )
    return pl.pallas_call(
        functools.partial(_sample_sel_kernel, t_pos=t_pos, n_pages=n_pages, n_top=n_top),
        grid_spec=grid_spec,
        out_shape=jax.ShapeDtypeStruct((b, 1, NSA_WIDTH), F32),
        compiler_params=_cparams("arbitrary"),
        name="sample_sel",
    )(page_table.reshape(-1), idx.reshape(-1), cache3, q3, knew3, gates3, oc, ow)


def _finish_kernel(yp_ref, on_ref, om_ref, x_ref, gm_ref, wo_ref, gf_ref, wr_ref, br_ref, tri_ref,
                   h_ref, hn_ref, route_ref, counts_ref, cnt_ref):
    gm = gm_ref[...]
    o1 = POOL_WIDTH
    o2 = o1 + NSA_WIDTH
    mixed = jnp.concatenate([_rms(yp_ref[...]) * gm[:, :o1], _rms(on_ref[...]) * gm[:, o1:o2],
                             _rms(om_ref[...]) * gm[:, o2:]], axis=-1)
    h = x_ref[...] + _dot(mixed.astype(BF16), wo_ref[...])
    h_ref[...] = h
    hn = _rms(h) * gf_ref[...]
    hn_ref[...] = hn
    logits = _dot(hn.astype(BF16), wr_ref[...]) + br_ref[...]
    lane = lax.broadcasted_iota(I32, (1, LANES), 1)
    lane_f = lane.astype(F32)
    is1 = lane < N_EXPERT_GROUPS
    m1 = jnp.max(jnp.where(is1, logits, -jnp.inf), axis=-1, keepdims=True)
    e1 = jnp.where(is1, jnp.exp(logits - m1), 0.0)
    p1 = e1 / jnp.sum(e1, axis=-1, keepdims=True)
    top1_p = jnp.max(p1, axis=-1, keepdims=True)
    grp = jnp.min(jnp.where((p1 == top1_p) & is1, lane_f, float(LANES)), axis=-1, keepdims=True)
    base = N_EXPERT_GROUPS + grp * EXPERTS_PER_GROUP
    in_g = (lane_f >= base) & (lane_f < base + EXPERTS_PER_GROUP)
    l2 = jnp.where(in_g, logits, -jnp.inf)
    v0 = jnp.max(l2, axis=-1, keepdims=True)
    i0 = jnp.min(jnp.where(l2 == v0, lane_f, float(LANES)), axis=-1, keepdims=True)
    l2b = jnp.where(lane_f == i0, -jnp.inf, l2)
    v1 = jnp.max(l2b, axis=-1, keepdims=True)
    i1 = jnp.min(jnp.where(l2b == v1, lane_f, float(LANES)), axis=-1, keepdims=True)
    ex = jnp.exp(v1 - v0)
    w0 = top1_p / (1.0 + ex)
    w1 = top1_p * ex / (1.0 + ex)
    ex0 = i0 - N_EXPERT_GROUPS
    ex1 = i1 - N_EXPERT_GROUPS
    is0 = lane_f == ex0
    is1e = lane_f == ex1
    oh0 = jnp.where(is0, 1.0, 0.0)
    oh1 = jnp.where(is1e, 1.0, 0.0)
    before0 = _dot(tri_ref[...], oh0.astype(BF16))
    before1 = _dot(tri_ref[...], oh1.astype(BF16))
    tot0 = jnp.sum(oh0, axis=0, keepdims=True)
    tot1 = jnp.sum(oh1, axis=0, keepdims=True)

    @pl.when(pl.program_id(0) == 0)
    def _():
        cnt_ref[...] = jnp.zeros(cnt_ref.shape, F32)

    seen = cnt_ref[...]
    rank0 = jnp.sum(jnp.where(is0, before0 + seen, 0.0), axis=-1, keepdims=True)
    rank1 = jnp.sum(jnp.where(is1e, before1 + seen + tot0, 0.0), axis=-1, keepdims=True)
    cnt_ref[...] = seen + tot0 + tot1
    counts_ref[...] = seen + tot0 + tot1
    route = jnp.where(lane == 0, ex0, jnp.where(lane == 1, ex1, jnp.where(lane == 2, w0, jnp.where(lane == 3, w1,
            jnp.where(lane == 4, rank0, jnp.where(lane == 5, rank1, 0.0))))))
    route_ref[...] = route


def _finish_call(yp, on, om, x2d, fw, *, tm):
    n, dm = x2d.shape
    full = lambda shape: pl.BlockSpec(shape, lambda i: (0,) * len(shape))
    rows = lambda w: pl.BlockSpec((tm, w), lambda i: (i, 0))
    tri = jnp.asarray(np.arange(tm)[None, :] < np.arange(tm)[:, None], BF16)
    return pl.pallas_call(
        _finish_kernel,
        grid=(n // tm,),
        in_specs=[rows(POOL_WIDTH), rows(NSA_WIDTH), rows(MEM_WIDTH), rows(dm), full((1, dm)), full((dm, dm)),
                  full((1, dm)), full((dm, LANES)), full((1, LANES)), full((tm, tm))],
        out_specs=[rows(dm), rows(dm), rows(LANES), full((1, LANES))],
        out_shape=[jax.ShapeDtypeStruct((n, dm), F32), jax.ShapeDtypeStruct((n, dm), F32),
                   jax.ShapeDtypeStruct((n, LANES), F32), jax.ShapeDtypeStruct((1, LANES), F32)],
        scratch_shapes=[pltpu.VMEM((1, LANES), F32)],
        compiler_params=_cparams("arbitrary"),
        name="finish",
    )(yp, on, om, x2d, fw["g_mix"], fw["w_out"], fw["g_ffn"], fw["w_r"], fw["b_r"], tri)


def _route_tables(route, counts, tm):
    n = route.shape[0]
    eid = route[:, 0:2].astype(I32)
    rank = route[:, 4:6].astype(I32)
    cnt = counts[0, :N_EXPERTS].astype(I32)
    tiles_per = (cnt + tm - 1) // tm
    tile_end = jnp.cumsum(tiles_per)
    tile_start = tile_end - tiles_per
    experts = jnp.arange(N_EXPERTS, dtype=I32)
    start_of = jnp.sum(jnp.where(eid[:, :, None] == experts, tile_start, 0), axis=-1)
    pos = (start_of * tm + rank).reshape(-1)
    n_tiles = (2 * n) // tm + N_EXPERTS
    tj = jnp.arange(n_tiles, dtype=I32)
    tile_expert = jnp.minimum(jnp.sum((tj[:, None] >= tile_end[None, :]).astype(I32), axis=1), N_EXPERTS - 1)
    sel = tile_expert[:, None] == experts
    rows_left = jnp.sum(jnp.where(sel, cnt, 0), axis=-1) - (tj - jnp.sum(jnp.where(sel, tile_start, 0), axis=-1)) * tm
    tile_nvalid = jnp.where(tj < tile_end[-1], jnp.clip(rows_left, 0, tm), 0).astype(I32)
    return tile_expert, tile_nvalid, pos


def _row_wait_all(src_row, dst_row, sem, count):
    for _ in range(count):
        pltpu.make_async_copy(src_row, dst_row, sem).wait()


def _dispatch_kernel(pos_ref, nv_ref, hn_ref, xs_ref, stage, zbuf, sem, zsem, *, tm, tme, n_tiles):
    i = pl.program_id(0)
    nt = pl.num_programs(0)
    slot = i % 2
    wait_slot = lambda s: _row_wait_all(stage.at[s, pl.ds(0, 1), :], xs_ref.at[pl.ds(0, 1), :], sem.at[s], 2 * tm)

    @pl.when(i == 0)
    def _():
        zbuf[...] = jnp.zeros(zbuf.shape, F32)

        def fill(j, c):
            @pl.when(nv_ref[j] < tme)
            def _():
                pltpu.make_async_copy(zbuf, xs_ref.at[pl.ds(pl.multiple_of(j * tme, tme), tme), :], zsem).start()
            return c

        def drain(j, c):
            @pl.when(nv_ref[j] < tme)
            def _():
                pltpu.make_async_copy(zbuf, xs_ref.at[pl.ds(0, tme), :], zsem).wait()
            return c

        lax.fori_loop(0, n_tiles, fill, 0)
        lax.fori_loop(0, n_tiles, drain, 0)

    @pl.when(i >= 2)
    def _():
        wait_slot(slot)

    stage[slot] = hn_ref[...]
    base = i * (2 * tm)
    for r in range(tm):
        for k in range(2):
            dst = pos_ref[base + 2 * r + k]
            pltpu.make_async_copy(stage.at[slot, pl.ds(r, 1), :], xs_ref.at[pl.ds(dst, 1), :], sem.at[slot]).start()

    @pl.when(i == nt - 1)
    def _():
        wait_slot(slot)

        @pl.when(i >= 1)
        def _():
            wait_slot(1 - slot)


def _dispatch_call(pos, tile_nvalid, hn, *, tm, tme):
    n, dm = hn.shape
    n_tiles = tile_nvalid.shape[0]
    grid_spec = pltpu.PrefetchScalarGridSpec(
        num_scalar_prefetch=2,
        grid=(n // tm,),
        in_specs=[pl.BlockSpec((tm, dm), lambda i, pos, nv: (i, 0))],
        out_specs=pl.BlockSpec(memory_space=pl.ANY),
        scratch_shapes=[pltpu.VMEM((2, tm, dm), F32), pltpu.VMEM((tme, dm), F32),
                        pltpu.SemaphoreType.DMA((2,)), pltpu.SemaphoreType.DMA(())],
    )
    return pl.pallas_call(
        functools.partial(_dispatch_kernel, tm=tm, tme=tme, n_tiles=n_tiles),
        grid_spec=grid_spec,
        out_shape=jax.ShapeDtypeStruct((n_tiles * tme, dm), F32),
        compiler_params=_cparams("arbitrary"),
        name="moe_dispatch",
    )(pos, tile_nvalid, hn)


def _expert_kernel(te_ref, nv_ref, x_ref, wg_ref, wu_ref, wd_ref, y_ref, wgb, wub, wdb, *, tm):
    i = pl.program_id(0)
    prev = te_ref[jnp.maximum(i - 1, 0)]

    @pl.when((i == 0) | (te_ref[i] != prev))
    def _():
        wgb[...] = wg_ref[0].astype(BF16)
        wub[...] = wu_ref[0].astype(BF16)
        wdb[...] = wd_ref[0].astype(BF16)

    nv = nv_ref[i]

    @pl.when(nv == 0)
    def _():
        y_ref[...] = jnp.zeros(y_ref.shape, F32)

    @pl.when(nv > 0)
    def _():
        x = x_ref[...].astype(BF16)
        hg = _dot(x, wgb[...])
        hu = _dot(x, wub[...])
        hmid = hg * jax.nn.sigmoid(hg) * hu
        y_ref[...] = _dot(hmid.astype(BF16), wdb[...])


def _expert_call(tile_expert, tile_nvalid, xs, wg, wu, wd, *, tm):
    n_tiles = tile_expert.shape[0]
    dm = xs.shape[1]
    ff = wg.shape[2]
    grid_spec = pltpu.PrefetchScalarGridSpec(
        num_scalar_prefetch=2,
        grid=(n_tiles,),
        in_specs=[pl.BlockSpec((tm, dm), lambda i, te, nv: (i, 0)),
                  pl.BlockSpec((1, dm, ff), lambda i, te, nv: (te[i], 0, 0)),
                  pl.BlockSpec((1, dm, ff), lambda i, te, nv: (te[i], 0, 0)),
                  pl.BlockSpec((1, ff, dm), lambda i, te, nv: (te[i], 0, 0))],
        out_specs=pl.BlockSpec((tm, dm), lambda i, te, nv: (i, 0)),
        scratch_shapes=[pltpu.VMEM((dm, ff), BF16), pltpu.VMEM((dm, ff), BF16), pltpu.VMEM((ff, dm), BF16)],
    )
    return pl.pallas_call(
        functools.partial(_expert_kernel, tm=tm),
        grid_spec=grid_spec,
        out_shape=jax.ShapeDtypeStruct(xs.shape, F32),
        compiler_params=_cparams("arbitrary"),
        name="moe_experts",
    )(tile_expert, tile_nvalid, xs, wg, wu, wd)


def _combine_kernel(pos_ref, ys_ref, h_ref, route_ref, o_ref, gbuf, sem, *, tm):
    i = pl.program_id(0)
    nt = pl.num_programs(0)
    slot = i % 2

    def issue(step, s):
        base = step * (2 * tm)
        for r in range(tm):
            for k in range(2):
                src = pos_ref[base + 2 * r + k]
                pltpu.make_async_copy(ys_ref.at[pl.ds(src, 1), :], gbuf.at[s, k, pl.ds(r, 1), :], sem.at[s]).start()

    @pl.when(i == 0)
    def _():
        issue(0, 0)

    @pl.when(i + 1 < nt)
    def _():
        issue(i + 1, 1 - slot)

    _row_wait_all(ys_ref.at[pl.ds(0, 1), :], gbuf.at[slot, 0, pl.ds(0, 1), :], sem.at[slot], 2 * tm)
    w0 = route_ref[:, 2:3]
    w1 = route_ref[:, 3:4]
    o_ref[...] = h_ref[...] + (w0 * gbuf[slot, 0] + w1 * gbuf[slot, 1])


def _combine_call(pos, ys, h, route, *, tm):
    n, dm = h.shape
    grid_spec = pltpu.PrefetchScalarGridSpec(
        num_scalar_prefetch=1,
        grid=(n // tm,),
        in_specs=[pl.BlockSpec(memory_space=pl.ANY),
                  pl.BlockSpec((tm, dm), lambda i, pos: (i, 0)),
                  pl.BlockSpec((tm, LANES), lambda i, pos: (i, 0))],
        out_specs=pl.BlockSpec((tm, dm), lambda i, pos: (i, 0)),
        scratch_shapes=[pltpu.VMEM((2, 2, tm, dm), F32), pltpu.SemaphoreType.DMA((2,))],
    )
    return pl.pallas_call(
        functools.partial(_combine_kernel, tm=tm),
        grid_spec=grid_spec,
        out_shape=jax.ShapeDtypeStruct((n, dm), F32),
        compiler_params=_cparams("arbitrary"),
        name="moe_combine",
    )(pos, ys, h, route)


def _moe_sample_kernel(hn_ref, route_ref, h_ref, wg_ref, wu_ref, wd_ref, o_ref, acc_ref):
    e = pl.program_id(0)

    @pl.when(e == 0)
    def _():
        acc_ref[...] = jnp.zeros(acc_ref.shape, F32)

    route = route_ref[...]
    ef = e.astype(F32)
    comb = jnp.where(route[:, 0:1] == ef, route[:, 2:3], 0.0) + jnp.where(route[:, 1:2] == ef, route[:, 3:4], 0.0)
    x = hn_ref[...].astype(BF16)
    hg = _dot(x, wg_ref[0].astype(BF16))
    hu = _dot(x, wu_ref[0].astype(BF16))
    hmid = hg * jax.nn.sigmoid(hg) * hu * comb
    acc_ref[...] += _dot(hmid.astype(BF16), wd_ref[0].astype(BF16))

    @pl.when(e == pl.num_programs(0) - 1)
    def _():
        o_ref[...] = h_ref[...] + acc_ref[...]


def _moe_sample_call(hn, route, h, wg, wu, wd):
    n, dm = hn.shape
    ff = wg.shape[2]
    full = lambda shape: pl.BlockSpec(shape, lambda e: (0,) * len(shape))
    return pl.pallas_call(
        _moe_sample_kernel,
        grid=(N_EXPERTS,),
        in_specs=[full((n, dm)), full((n, LANES)), full((n, dm)),
                  pl.BlockSpec((1, dm, ff), lambda e: (e, 0, 0)), pl.BlockSpec((1, dm, ff), lambda e: (e, 0, 0)),
                  pl.BlockSpec((1, ff, dm), lambda e: (e, 0, 0))],
        out_specs=full((n, dm)),
        out_shape=jax.ShapeDtypeStruct((n, dm), F32),
        scratch_shapes=[pltpu.VMEM((n, dm), F32)],
        compiler_params=_cparams("arbitrary"),
        name="moe_sample",
    )(hn, route, h, wg, wu, wd)


def _prep_weights(l, g_attn, w_in, g_q, g_k, pe_cmp, w_cmp1, w_cmp2, w_pool, s_pool, g_mem, w_mem_kv, g_mq, g_mk,
                  g_mix, w_out, g_ffn, w_router1, b_router1, w_router2, b_router2):
    dm = w_in.shape[1]
    w = w_in[l]
    o1 = POOL_WIDTH
    o2 = o1 + NSA_WIDTH
    o3 = o2 + NSA_KV_WIDTH
    o4 = o3 + GATE_WIDTH
    wq = w[:, o1:o2].reshape(dm, NSA_HEADS, 1, HEAD_DIM)
    q_slot = jnp.asarray(np.eye(NSA_KV_HEADS, dtype=np.float32)[np.arange(NSA_HEADS) // NSA_REP])
    wq_pad = (wq * q_slot[None, :, :, None]).reshape(dm, Q_PAD)
    wqm = w[:, o4:].reshape(dm, MEM_HEADS, 1, HEAD_DIM)
    m_slot = jnp.asarray(np.eye(2, dtype=np.float32)[np.arange(MEM_HEADS) % 2])
    wqm_pad = (wqm * m_slot[None, :, :, None]).reshape(dm, QM_PAD)
    wg_pad = jnp.pad(w[:, o3:o4], ((0, 0), (0, LANES - GATE_WIDTH)))
    w_packed = jnp.concatenate([w[:, :o1], wq_pad, w[:, o2:o3], wqm_pad, wg_pad], axis=1).astype(BF16)
    half = ROPE_DIM // 2
    inv = jnp.power(ROPE_THETA, -jnp.arange(half, dtype=F32) * 2.0 / ROPE_DIM)
    d = np.arange(LANES) % HEAD_DIM
    inv_lane = jnp.where(jnp.asarray(d < ROPE_DIM), inv[jnp.asarray(d % half)], 0.0).reshape(1, LANES)
    seg = jnp.asarray((np.arange(LANES)[:, None] // HEAD_DIM == np.arange(LANES)[None, :] // HEAD_DIM), BF16)
    gk = jnp.concatenate([jnp.tile(g_k[l, br], 2) for br in range(N_BRANCH)]).reshape(1, N_BRANCH * LANES)
    proj = dict(g_attn=g_attn[l].reshape(1, dm), w_in=w_packed, gq=jnp.tile(g_q[l], Q_PAD // HEAD_DIM).reshape(1, Q_PAD),
                gk=gk, gmq=jnp.tile(g_mq[l], QM_PAD // HEAD_DIM).reshape(1, QM_PAD), inv=inv_lane, seg=seg)

    n_grp = len(POOL_WINDOWS)
    pg = POOL_WIDTH // n_grp
    w_bd = (jnp.asarray(np.eye(n_grp, dtype=np.float32))[:, None, :, None] * w_pool[l][:, :, None, :]
            ).reshape(POOL_WIDTH, POOL_WIDTH).astype(BF16)
    pool = dict(w=w_bd, s=s_pool[l].reshape(1, POOL_WIDTH))

    n_slab = 2 * NSA_KV_HEADS
    slab_kv = np.arange(n_slab) // NSA_KV_HEADS
    w1 = w_cmp1[l].reshape(2, CMP_BLOCK, HEAD_DIM, CMP_HIDDEN)[slab_kv]
    eye_s = jnp.asarray(np.eye(n_slab, dtype=np.float32))
    w1_bd = jnp.einsum("srdj,st->rsdtj", w1, eye_s)
    kw = CMP_STRIDE * n_slab * HEAD_DIM
    wa = w1_bd[:CMP_STRIDE].reshape(kw, n_slab * CMP_HIDDEN).astype(BF16)
    wb = w1_bd[CMP_STRIDE:].reshape(kw, n_slab * CMP_HIDDEN).astype(BF16)
    pe = jnp.transpose(pe_cmp[l][slab_kv], (1, 0, 2))
    pea = pe[:CMP_STRIDE].reshape(1, kw)
    peb = pe[CMP_STRIDE:].reshape(1, kw)
    w2 = jnp.einsum("sjd,st->sjtd", w_cmp2[l][slab_kv], eye_s).reshape(n_slab * CMP_HIDDEN, n_slab * HEAD_DIM)
    cmp_w = dict(wa=wa, wb=wb, pea=pea, peb=peb, w2=w2.astype(BF16))

    mem = dict(g=g_mem[l].reshape(1, dm), w=w_mem_kv[l].astype(BF16),
               gk=jnp.tile(g_mk[l], MEM_HEADS).reshape(1, MEM_WIDTH))
    w_r = jnp.concatenate([w_router1[l], w_router2[l].reshape(dm, N_EXPERTS)], axis=1)
    w_r = jnp.pad(w_r, ((0, 0), (0, LANES - w_r.shape[1]))).astype(BF16)
    b_r = jnp.concatenate([b_router1[l], b_router2[l].reshape(-1)])
    b_r = jnp.pad(b_r, (0, LANES - b_r.shape[0])).reshape(1, LANES)
    fin = dict(g_mix=g_mix[l].reshape(1, dm), w_out=w_out[l].astype(BF16), g_ffn=g_ffn[l].reshape(1, dm),
               w_r=w_r, b_r=b_r)
    return proj, pool, cmp_w, mem, fin


def _pick(n, prefs):
    for p in prefs:
        if n % p == 0:
            return p
    return n


def kernel(x_prompt, x_sample, cache_kv, cache_win, state_pool, cache_mem_kv, page_table, mem_prompt, g_attn, w_in, g_q, g_k, pe_cmp, w_cmp1, w_cmp2, w_pool, s_pool, g_mem, w_mem_kv, g_mq, g_mk, g_mix, w_out, g_ffn, w_router1, b_router1, w_router2, b_router2, w_gate, w_up, w_down):
    depth = w_in.shape[0]
    bp, t, dm = x_prompt.shape
    bs, ts, _ = x_sample.shape
    n_pages = page_table.shape[1]
    page = cache_kv.shape[2]
    past_len = n_pages * page
    n_win = cache_win.shape[2]
    assert ts == 1 and n_win == WINDOW and past_len % SEL_BLOCK == 0 and t % WINDOW == 0 and t >= WINDOW + 128
    ff = w_gate.shape[-1]
    tq, tk = 128, 512
    n_sel = t // SEL_BLOCK
    poolt = jnp.asarray(np.arange(t // CMP_STRIDE)[None, :] // CMP_PER_SEL == np.arange(n_sel)[:, None], BF16)
    e_mat = jnp.asarray(np.arange(t)[None, :] // SEL_BLOCK == np.arange(n_sel)[:, None], BF16)
    n_cmp_s = past_len // CMP_STRIDE
    n_slot_s = -(-(past_len // SEL_BLOCK + 1) // LANES) * LANES
    pool_s = jnp.asarray(np.arange(n_cmp_s)[:, None] // CMP_PER_SEL == np.arange(n_slot_s)[None, :], BF16)

    hp, hs = x_prompt, x_sample
    outs = [[] for _ in range(7)]
    for l in range(depth):
        proj_w, pool_w, cmp_w, mem_w, fin_w = _prep_weights(
            l, g_attn, w_in, g_q, g_k, pe_cmp, w_cmp1, w_cmp2, w_pool, s_pool, g_mem, w_mem_kv, g_mq, g_mk,
            g_mix, w_out, g_ffn, w_router1, b_router1, w_router2, b_router2)
        wg = w_gate[l].reshape(N_EXPERTS, dm, ff)
        wu = w_up[l].reshape(N_EXPERTS, dm, ff)
        wd = w_down[l].reshape(N_EXPERTS, ff, dm)
        kw_cols = CMP_STRIDE * 2 * LANES

        n = bp * t
        xp2 = hp.reshape(n, dm)
        u, qb, rows4, rowsw, kvb, gates, qmb = _proj_call(xp2, proj_w, seq_len=t, pos0=0, tm=_pick(n, (512, 256, 128)))
        u3 = u.reshape(bp, t, POOL_WIDTH)
        y_pool = _pool_call(u3, pool_w["w"], pool_w["s"], tp=_pick(t, (512, 256, 128)), pos0=0)
        kcvc = _compress_prompt_call(rows4.reshape(bp, t, 4 * LANES), jnp.zeros((bp, 1, kw_cols), F32), cmp_w)
        o_nsa = _nsa_prompt_call(qb.reshape(bp, t, Q_PAD), gates.reshape(bp, t, LANES), kcvc,
                                 kvb.reshape(bp, t, NSA_KV_WIDTH), poolt, e_mat, tq=tq, tk=tk)
        m_len = mem_prompt.shape[1]
        mkv, mkvb = _memkv_call(mem_prompt.reshape(bp * m_len, dm), mem_w["g"], mem_w["w"], mem_w["gk"],
                                proj_w["seg"], tm=_pick(bp * m_len, (256, 128)))
        o_mem = _memattn_call(qmb.reshape(bp, t, QM_PAD), mkvb.reshape(bp, m_len, 2 * MEM_WIDTH),
                              tq=_pick(t, (512, 256, 128)))
        tmf = _pick(n, (512, 256, 128))
        h_p, hn_p, route_p, counts_p = _finish_call(y_pool.reshape(n, POOL_WIDTH), o_nsa.reshape(n, NSA_WIDTH),
                                                    o_mem.reshape(n, MEM_WIDTH), xp2, fin_w, tm=tmf)
        tme = 256
        tmd = _pick(n, (256, 128))
        te, nv, pos = _route_tables(route_p, counts_p, tme)
        xs = _dispatch_call(pos, nv, hn_p, tm=tmd, tme=tme)
        ys = _expert_call(te, nv, xs, wg, wu, wd, tm=tme)
        y_p = _combine_call(pos, ys, h_p, route_p, tm=tmd).reshape(bp, t, dm)

        xs2 = hs.reshape(bs, dm)
        u_s, qb_s, rows4_s, rowsw_s, _, gates_s, qmb_s = _proj_call(xs2, proj_w, seq_len=1, pos0=past_len, tm=bs)
        ext = jnp.concatenate([state_pool[l], u_s[:, None, :]], axis=1)
        y_pool_s = _pool_call(ext, pool_w["w"], pool_w["s"], tp=POOL_STATE + 1, pos0=past_len - POOL_STATE)[:, -1, :]
        cache3 = cache_kv[l].reshape(cache_kv.shape[1], page, N_KV_SLOTS * 2 * HEAD_DIM)
        tail_s = jnp.pad(rows4_s[:, :2 * LANES], ((0, 0), (0, kw_cols - 2 * LANES))).reshape(bs, 1, kw_cols)
        kcvc_s = _compress_sample_call(page_table, cache3, tail_s, cmp_w)
        win_s = jnp.concatenate([cache_win[l].reshape(bs, n_win, 2 * LANES)[:, 1:], rowsw_s[:, None, :]], axis=1)
        mkv_s = cache_mem_kv[l].reshape(bs, cache_mem_kv.shape[2], 2 * MEM_WIDTH)
        oc, ow, om_s, idx = _sample_attn1_call(qb_s.reshape(bs, 1, Q_PAD), kcvc_s, win_s, mkv_s,
                                               qmb_s.reshape(bs, 1, QM_PAD), pool_s, t_pos=past_len)
        n_top = min(SEL_TOPK, past_len // SEL_BLOCK + 1)
        o_nsa_s = _sample_sel_call(page_table, idx, cache3, qb_s.reshape(bs, 1, Q_PAD),
                                   rows4_s[:, 2 * LANES:].reshape(bs, 1, 2 * LANES), gates_s.reshape(bs, 1, LANES),
                                   oc, ow, t_pos=past_len, n_top=n_top)
        h_s, hn_s, route_s, _ = _finish_call(y_pool_s, o_nsa_s.reshape(bs, NSA_WIDTH), om_s.reshape(bs, MEM_WIDTH),
                                          xs2, fin_w, tm=bs)
        y_s = _moe_sample_call(hn_s, route_s, h_s, wg, wu, wd).reshape(bs, 1, dm)

        keep = min(WINDOW, t)
        outs[0].append(rows4.reshape(bp, t, N_KV_SLOTS, NSA_KV_HEADS, HEAD_DIM))
        outs[1].append(rows4_s.reshape(bs, 1, N_KV_SLOTS, NSA_KV_HEADS, HEAD_DIM))
        outs[2].append(rowsw.reshape(bp, t, 2, NSA_KV_HEADS, HEAD_DIM)[:, t - keep:])
        outs[3].append(win_s.reshape(bs, n_win, 2, NSA_KV_HEADS, HEAD_DIM))
        outs[4].append(u3[:, t - POOL_STATE:])
        outs[5].append(ext[:, 1:])
        outs[6].append(mkv.reshape(bp, m_len, 2, MEM_HEADS, HEAD_DIM))
        hp, hs = y_p, y_s
    return (hp, hs) + tuple(jnp.stack(o) for o in outs)
```

```python
import functools

import numpy as np
import jax
import jax.numpy as jnp
from jax import lax
from jax.experimental import pallas as pl
from jax.experimental.pallas import tpu as pltpu

F32 = jnp.float32
BF16 = jnp.bfloat16
I32 = jnp.int32

HEAD_DIM = 64
POOL_WINDOWS = (2, 4, 8, 16)
POOL_STATE = max(POOL_WINDOWS) - 1
NSA_HEADS = 8
NSA_KV_HEADS = 2
NSA_REP = NSA_HEADS // NSA_KV_HEADS
N_BRANCH = 3
CMP_BLOCK = 32
CMP_STRIDE = 16
CMP_HIDDEN = 2 * HEAD_DIM
SEL_BLOCK = 64
SEL_TOPK = 16
CMP_PER_SEL = SEL_BLOCK // CMP_STRIDE
WINDOW = 512
FORCE_BONUS = 1000.0
MEM_HEADS = 4
ROPE_DIM = HEAD_DIM // 4
ROPE_THETA = 500000.0
N_EXPERT_GROUPS = 4
EXPERTS_PER_GROUP = 8
N_EXPERTS = N_EXPERT_GROUPS * EXPERTS_PER_GROUP
EPS = 1e-6
N_KV_SLOTS = 4

LANES = 128
POOL_WIDTH = 256
NSA_WIDTH = NSA_HEADS * HEAD_DIM
NSA_KV_WIDTH = N_BRANCH * 2 * NSA_KV_HEADS * HEAD_DIM
GATE_WIDTH = NSA_HEADS * N_BRANCH
MEM_WIDTH = MEM_HEADS * HEAD_DIM
Q_PAD = NSA_HEADS * LANES
QM_PAD = MEM_HEADS * LANES
C_U = 0
C_Q = C_U + POOL_WIDTH
C_KV = C_Q + Q_PAD
C_QM = C_KV + NSA_KV_WIDTH
C_G = C_QM + QM_PAD
C_END = C_G + LANES

NEG = -1e30
VMEM_LIMIT = 48 * 1024 * 1024

_NT = (((1,), (1,)), ((), ()))


def _cparams(*sem):
    return pltpu.CompilerParams(dimension_semantics=tuple(sem), vmem_limit_bytes=VMEM_LIMIT)


def _dot(a, b):
    return jnp.dot(a, b, preferred_element_type=F32)


def _dot_nt(a, b):
    return lax.dot_general(a, b, _NT, preferred_element_type=F32)


def _rms(x):
    return x * lax.rsqrt(jnp.mean(x * x, axis=-1, keepdims=True) + EPS)


def _masked_softmax(s, mask):
    sm = jnp.where(mask, s, NEG)
    m = jnp.max(sm, axis=-1, keepdims=True)
    e = jnp.where(mask, jnp.exp(sm - m), 0.0)
    return e / jnp.maximum(jnp.sum(e, axis=-1, keepdims=True), 1e-30)


def _split3(x):
    hi = x.astype(BF16)
    r1 = x - hi.astype(F32)
    mid = r1.astype(BF16)
    lo = (r1 - mid.astype(F32)).astype(BF16)
    return hi, mid, lo


def _proj_kernel(x_ref, ga_ref, w_ref, gq_ref, gk_ref, gmq_ref, inv_ref, seg_ref,
                 u_ref, q_ref, rows4_ref, rowsw_ref, kvb_ref, gates_ref, qm_ref, *, tm, seq_len, pos0):
    i = pl.program_id(0)
    a = _rms(x_ref[...]) * ga_ref[...]
    z = _dot(a.astype(BF16), w_ref[...])
    u_ref[...] = z[:, C_U:C_U + POOL_WIDTH]

    row = i * tm + lax.broadcasted_iota(I32, (tm, 1), 0)
    pos = (pos0 + row % seq_len).astype(F32)
    ang = pos * inv_ref[...]
    d = lax.broadcasted_iota(I32, (1, LANES), 1) % HEAD_DIM
    cos = jnp.cos(ang)
    sin = jnp.sin(ang)
    half = ROPE_DIM // 2
    s_next = jnp.where(d < half, -sin, 0.0)
    s_prev = jnp.where((d >= half) & (d < ROPE_DIM), sin, 0.0)
    seg = seg_ref[...]

    def head_norm(xc, g):
        ssq = _dot((xc * xc).astype(BF16), seg)
        return xc * lax.rsqrt(ssq * (1.0 / HEAD_DIM) + EPS) * g

    def rope(xc):
        return xc * cos + pltpu.roll(xc, LANES - half, 1) * s_next + pltpu.roll(xc, half, 1) * s_prev

    scale = HEAD_DIM ** -0.5
    for c in range(NSA_HEADS):
        sl = slice(c * LANES, (c + 1) * LANES)
        qc = rope(head_norm(z[:, C_Q + c * LANES:C_Q + (c + 1) * LANES], gq_ref[:, sl]))
        q_ref[:, sl] = (qc * scale).astype(BF16)
    for br in range(N_BRANCH):
        k0 = C_KV + br * 2 * LANES
        kn = rope(head_norm(z[:, k0:k0 + LANES], gk_ref[:, br * LANES:(br + 1) * LANES]))
        vv = z[:, k0 + LANES:k0 + 2 * LANES]
        kvb_ref[:, br * 2 * LANES:br * 2 * LANES + LANES] = kn.astype(BF16)
        kvb_ref[:, br * 2 * LANES + LANES:(br + 1) * 2 * LANES] = vv.astype(BF16)
        if br < 2:
            rows4_ref[:, br * 2 * LANES:br * 2 * LANES + LANES] = kn
            rows4_ref[:, br * 2 * LANES + LANES:(br + 1) * 2 * LANES] = vv
        else:
            rowsw_ref[:, :LANES] = kn
            rowsw_ref[:, LANES:] = vv
    for c in range(MEM_HEADS):
        sl = slice(c * LANES, (c + 1) * LANES)
        qmc = head_norm(z[:, C_QM + c * LANES:C_QM + (c + 1) * LANES], gmq_ref[:, sl])
        qm_ref[:, sl] = (qmc * scale).astype(BF16)
    gates_ref[...] = jax.nn.sigmoid(z[:, C_G:C_END])


def _proj_call(x2d, pw, *, seq_len, pos0, tm):
    n, dm = x2d.shape
    full = lambda shape: pl.BlockSpec(shape, lambda i: (0,) * len(shape))
    rows = lambda w: pl.BlockSpec((tm, w), lambda i: (i, 0))
    outs = [(POOL_WIDTH, F32), (Q_PAD, BF16), (4 * LANES, F32), (2 * LANES, F32), (NSA_KV_WIDTH, BF16),
            (LANES, F32), (QM_PAD, BF16)]
    return pl.pallas_call(
        functools.partial(_proj_kernel, tm=tm, seq_len=seq_len, pos0=pos0),
        grid=(n // tm,),
        in_specs=[rows(dm), full((1, dm)), full((dm, C_END)), full((1, Q_PAD)), full((1, N_BRANCH * LANES)),
                  full((1, QM_PAD)), full((1, LANES)), full((LANES, LANES))],
        out_specs=[rows(w) for w, _ in outs],
        out_shape=[jax.ShapeDtypeStruct((n, w), dt) for w, dt in outs],
        compiler_params=_cparams("parallel"),
        name="proj",
    )(x2d, pw["g_attn"], pw["w_in"], pw["gq"], pw["gk"], pw["gmq"], pw["inv"], pw["seg"])


def _pool_kernel(u_ref, halo_ref, w_ref, s_ref, y_ref, *, tp, pos0):
    i = pl.program_id(1)
    u = u_ref[0]
    halo = halo_ref[0] * (i > 0).astype(F32)
    n_h = POOL_STATE + 1
    ext = jnp.concatenate([halo, u], axis=0)
    sums = {1: ext}
    w = 1
    while w < max(POOL_WINDOWS):
        sums[2 * w] = sums[w] + pltpu.roll(sums[w], w, 0)
        w *= 2
    pos = pos0 + i * tp + lax.broadcasted_iota(I32, (tp, 1), 0)
    lane_grp = lax.broadcasted_iota(I32, (1, POOL_WIDTH), 1) // (POOL_WIDTH // len(POOL_WINDOWS))
    mean = jnp.zeros((tp, POOL_WIDTH), F32)
    for gi, wdw in enumerate(POOL_WINDOWS):
        cnt = jnp.minimum(pos + 1, wdw).astype(F32)
        mean = jnp.where(lane_grp == gi, sums[wdw][n_h:] / cnt, mean)
    r = mean - u
    y_ref[0] = _dot(r.astype(BF16), w_ref[...]) * s_ref[...]


def _pool_call(u3, w_bd, s_pool, *, tp, pos0):
    b, t, c = u3.shape
    n_h = POOL_STATE + 1
    return pl.pallas_call(
        functools.partial(_pool_kernel, tp=tp, pos0=pos0),
        grid=(b, t // tp),
        in_specs=[pl.BlockSpec((1, tp, c), lambda bi, i: (bi, i, 0)),
                  pl.BlockSpec((1, n_h, c), lambda bi, i: (bi, jnp.maximum(i * (tp // n_h) - 1, 0), 0)),
                  pl.BlockSpec((c, c), lambda bi, i: (0, 0)),
                  pl.BlockSpec((1, c), lambda bi, i: (0, 0))],
        out_specs=pl.BlockSpec((1, tp, c), lambda bi, i: (bi, i, 0)),
        out_shape=jax.ShapeDtypeStruct((b, t, c), F32),
        compiler_params=_cparams("parallel", "parallel"),
        name="pool",
    )(u3, u3, w_bd, s_pool)


def _compress_core(load_piece, tail_ref, pe_ref, w_ref, w2_ref, n):
    rowid = lax.broadcasted_iota(I32, (n, 1), 0)
    outs = []
    for c in range(2):
        w = w_ref[c]
        pe8 = jnp.broadcast_to(pe_ref[c], (8, 2 * CMP_STRIDE * HEAD_DIM)).astype(BF16)
        pe_first = _dot(pe8[:, :CMP_STRIDE * HEAD_DIM], w)[0:1, :CMP_HIDDEN]
        pe_second = _dot(pe8[:, CMP_STRIDE * HEAD_DIM:], w)[0:1, CMP_HIDDEN:]
        pe_const = pe_first + pe_second
        x2 = jnp.concatenate(
            [jnp.concatenate([load_piece(c, g, r) for r in range(CMP_STRIDE)], axis=1) for g in range(NSA_KV_HEADS)],
            axis=0).astype(BF16)
        z = _dot(x2, w)
        hs = []
        for g in range(NSA_KV_HEADS):
            tail8 = jnp.broadcast_to(tail_ref[0, 2 * c + g:2 * c + g + 1, :], (8, CMP_STRIDE * HEAD_DIM)).astype(BF16)
            second_tail = _dot(tail8, w)[0:1, CMP_HIDDEN:]
            first = z[g * n:(g + 1) * n, :CMP_HIDDEN]
            second = z[g * n:(g + 1) * n, CMP_HIDDEN:]
            shifted = pltpu.roll(second, n - 1, 0)
            h = first + jnp.where(rowid == n - 1, second_tail, shifted) + pe_const
            hs.append(jax.nn.gelu(h, approximate=True))
        o = _dot(jnp.concatenate(hs, axis=0).astype(BF16), w2_ref[c])
        outs += [o[:n], o[n:]]
    return jnp.concatenate(outs, axis=1)


def _compress_prompt_kernel(k_ref, v_ref, tail_ref, pe_ref, w_ref, w2_ref, out_ref, *, n):
    def piece(c, g, r):
        return (k_ref, v_ref)[c][0, pl.ds(r, n, stride=CMP_STRIDE), :][:, g * HEAD_DIM:(g + 1) * HEAD_DIM]

    out_ref[0] = _compress_core(piece, tail_ref, pe_ref, w_ref, w2_ref, n).astype(BF16)


def _compress_prompt_call(rows4_3d, tail, cw):
    b, t, _ = rows4_3d.shape
    n = t // CMP_STRIDE
    full = lambda a: pl.BlockSpec(a.shape, lambda bi: (0,) * a.ndim)
    return pl.pallas_call(
        functools.partial(_compress_prompt_kernel, n=n),
        grid=(b,),
        in_specs=[pl.BlockSpec((1, t, LANES), lambda bi: (bi, 0, 0)),
                  pl.BlockSpec((1, t, LANES), lambda bi: (bi, 0, 1)),
                  pl.BlockSpec((1,) + tail.shape[1:], lambda bi: (bi, 0, 0)),
                  full(cw["pe"]), full(cw["w"]), full(cw["w2"])],
        out_specs=pl.BlockSpec((1, n, 2 * LANES), lambda bi: (bi, 0, 0)),
        out_shape=jax.ShapeDtypeStruct((b, n, 2 * LANES), BF16),
        compiler_params=_cparams("parallel"),
        name="compress_prompt",
    )(rows4_3d, rows4_3d, tail, cw["pe"], cw["w"], cw["w2"])


def _compress_sample_kernel(pt_ref, cache_ref, tail_ref, pe_ref, w_ref, w2_ref, out_ref,
                            buf, sem, *, n, n_pages, page):
    b = pl.program_id(0)
    nb = pl.num_programs(0)

    def page_copy(bb, slot, p, c):
        return pltpu.make_async_copy(cache_ref.at[pt_ref[bb * n_pages + p], :, c],
                                     buf.at[slot, c, pl.ds(p * page, page)], sem.at[slot])

    def for_all(fn):
        for p in range(n_pages):
            for c in range(2):
                fn(p, c)

    def issue(bb, slot):
        for_all(lambda p, c: page_copy(bb, slot, p, c).start())

    @pl.when(b == 0)
    def _():
        issue(0, 0)

    @pl.when(b + 1 < nb)
    def _():
        issue(b + 1, (b + 1) % 2)

    slot = b % 2
    for_all(lambda p, c: page_copy(b, slot, p, c).wait())
    piece = lambda c, g, r: buf[slot, c, pl.ds(r, n, stride=CMP_STRIDE), g, :]
    out_ref[0] = _compress_core(piece, tail_ref, pe_ref, w_ref, w2_ref, n).astype(BF16)


def _compress_sample_call(page_table, cache5, tail, cw):
    b, n_pages = page_table.shape
    page = cache5.shape[1]
    n = n_pages * page // CMP_STRIDE
    full = lambda a: pl.BlockSpec(a.shape, lambda bi, pt: (0,) * a.ndim)
    grid_spec = pltpu.PrefetchScalarGridSpec(
        num_scalar_prefetch=1,
        grid=(b,),
        in_specs=[pl.BlockSpec(memory_space=pl.ANY),
                  pl.BlockSpec((1,) + tail.shape[1:], lambda bi, pt: (bi, 0, 0)),
                  full(cw["pe"]), full(cw["w"]), full(cw["w2"])],
        out_specs=pl.BlockSpec((1, n, 2 * LANES), lambda bi, pt: (bi, 0, 0)),
        scratch_shapes=[pltpu.VMEM((2, 2, n_pages * page, NSA_KV_HEADS, HEAD_DIM), F32),
                        pltpu.SemaphoreType.DMA((2,))],
    )
    return pl.pallas_call(
        functools.partial(_compress_sample_kernel, n=n, n_pages=n_pages, page=page),
        grid_spec=grid_spec,
        out_shape=jax.ShapeDtypeStruct((b, n, 2 * LANES), BF16),
        compiler_params=_cparams("arbitrary"),
        name="compress_sample",
    )(page_table.reshape(-1), cache5, tail, cw["pe"], cw["w"], cw["w2"])


def _topk_mask(score, ids, n_top, axis):
    sel = jnp.zeros(score.shape, jnp.bool_)
    work = score
    firsts = []
    big = float(score.shape[axis])
    for _ in range(n_top):
        m = jnp.max(work, axis=axis, keepdims=True)
        first = jnp.min(jnp.where(work == m, ids, big), axis=axis, keepdims=True)
        pick = (ids == first) & (m > -jnp.inf)
        sel = sel | pick
        work = jnp.where(pick, -jnp.inf, work)
        firsts.append(jnp.where(m > -jnp.inf, first, -1.0))
    return sel, firsts


def _gate_and_pack(o_ref_store, gates, o_c, o_s, o_w, rows):
    lane = lax.broadcasted_iota(I32, (1, LANES), 1)
    heads = []
    for h in range(NSA_HEADS):
        g, r = divmod(h, NSA_REP)
        rs = slice(r * rows, (r + 1) * rows)
        gc, gs, gw = (gates[:, N_BRANCH * h + k:N_BRANCH * h + k + 1] for k in range(N_BRANCH))
        heads.append(gc * o_c[g][rs] + gs * o_s[g][rs] + gw * o_w[g][rs])
    for j in range(NSA_HEADS // 2):
        a, b = heads[2 * j], heads[2 * j + 1]
        if (2 * j) // NSA_REP == 0:
            chunk = jnp.where(lane < HEAD_DIM, a, pltpu.roll(b, HEAD_DIM, 1))
        else:
            chunk = jnp.where(lane < HEAD_DIM, pltpu.roll(a, HEAD_DIM, 1), b)
        o_ref_store(j, chunk)


def _nsa_prompt_kernel(q_ref, gate_ref, kc_ref, kv_ref, poolt_ref, e_ref, o_ref, selx_ref, acc_ref,
                       *, tq, tk, t_len):
    i = pl.program_id(1)
    s0 = i * tq
    rows4 = NSA_REP * tq
    tpos = s0 + lax.broadcasted_iota(I32, (tq, 1), 0)
    rep = lambda x: jnp.concatenate([x] * NSA_REP, axis=0)
    n_cmp = kc_ref.shape[1]
    kc128 = kc_ref[0, :, :LANES]
    vc128 = kc_ref[0, :, LANES:]
    kc_end = lax.broadcasted_iota(I32, (1, n_cmp), 1) * CMP_STRIDE + (CMP_BLOCK - 1)
    bias_c = rep(jnp.where(kc_end <= tpos, 0.0, NEG))
    any_c = rep((tpos >= CMP_BLOCK - 1).astype(F32))
    n_sel = t_len // SEL_BLOCK
    blk = lax.broadcasted_iota(I32, (n_sel, 1), 0)
    blk_f = blk.astype(F32)
    tq_lane = s0 + lax.broadcasted_iota(I32, (1, tq), 1)
    cur = tq_lane // SEL_BLOCK
    valid = blk * SEL_BLOCK <= tq_lane
    forced = (blk == 0) | (blk == cur) | (blk == cur - 1)
    w_start = pl.multiple_of(jnp.maximum(s0 - WINDOW, 0), LANES)
    w_len = WINDOW + tq
    kpos_w = w_start + lax.broadcasted_iota(I32, (1, w_len), 1)
    dist = tpos - kpos_w
    bias_w = rep(jnp.where((dist >= 0) & (dist < WINDOW), 0.0, NEG))
    n_ch = (s0 + tq + tk - 1) // tk

    o_c, o_s, o_w = [], [], []
    for g in range(NSA_KV_HEADS):
        qg = jnp.concatenate([q_ref[0, :, (g * NSA_REP + r) * LANES:(g * NSA_REP + r + 1) * LANES]
                              for r in range(NSA_REP)], axis=0)
        s_c = _dot_nt(qg, kc128) + bias_c
        e_c = jnp.exp(s_c - jnp.max(s_c, axis=-1, keepdims=True))
        p_c = e_c * (any_c / jnp.sum(e_c, axis=-1, keepdims=True))
        o_c.append(_dot(p_c.astype(BF16), vc128))
        imp = p_c[0:tq]
        for r in range(1, NSA_REP):
            imp = imp + p_c[r * tq:(r + 1) * tq]
        imp_t = sum(_dot_nt(poolt_ref[...], piece) for piece in _split3(imp))
        score = jnp.where(valid, imp_t + jnp.where(forced, FORCE_BONUS, 0.0), -jnp.inf)
        sel_t, _ = _topk_mask(score, blk_f, min(SEL_TOPK, n_sel), 0)
        sel_pad = jnp.concatenate([sel_t.astype(F32), jnp.zeros(((-n_sel) % LANES, tq), F32)], axis=0)
        sel = sel_pad.T[:, :n_sel].astype(BF16)
        for c in range(t_len // tk):
            kpos = c * tk + lax.broadcasted_iota(I32, (1, tk), 1)
            hit = (_dot(sel, e_ref[:, c * tk:(c + 1) * tk]) > 0.5) & (kpos <= tpos)
            selx_ref[c] = jnp.where(hit, 0.0, NEG)

        acc_ref[...] = jnp.zeros(acc_ref.shape, F32)

        def body(c, carry):
            m, l = carry
            k0 = pl.multiple_of(c * tk, tk)
            ks = kv_ref[0, pl.ds(k0, tk), 2 * LANES:3 * LANES]
            vs = kv_ref[0, pl.ds(k0, tk), 3 * LANES:4 * LANES]
            s = _dot_nt(qg, ks) + rep(selx_ref[c])
            m_new = jnp.maximum(m, jnp.max(s, axis=-1, keepdims=True))
            alpha = jnp.exp(m - m_new)
            p = jnp.exp(s - m_new)
            l_new = alpha * l + jnp.sum(p, axis=-1, keepdims=True)
            acc_ref[...] = alpha * acc_ref[...] + _dot(p.astype(BF16), vs)
            return m_new, l_new

        _, l_fin = lax.fori_loop(0, n_ch, body, (jnp.full((rows4, 1), NEG, F32), jnp.zeros((rows4, 1), F32)))
        o_s.append(acc_ref[...] * (1.0 / jnp.maximum(l_fin, 1e-30)))

        kw = kv_ref[0, pl.ds(w_start, w_len), 4 * LANES:5 * LANES]
        vw = kv_ref[0, pl.ds(w_start, w_len), 5 * LANES:6 * LANES]
        s_w = _dot_nt(qg, kw) + bias_w
        e_w = jnp.exp(s_w - jnp.max(s_w, axis=-1, keepdims=True))
        o_w.append(_dot(e_w.astype(BF16), vw) * (1.0 / jnp.sum(e_w, axis=-1, keepdims=True)))

    def store(j, chunk):
        o_ref[0, :, j * LANES:(j + 1) * LANES] = chunk

    _gate_and_pack(store, gate_ref[0], o_c, o_s, o_w, tq)


def _nsa_prompt_call(q3, gates3, kcvc, kvb3, poolt, e_mat, *, tq, tk):
    b, t, _ = q3.shape
    n_cmp = kcvc.shape[1]
    return pl.pallas_call(
        functools.partial(_nsa_prompt_kernel, tq=tq, tk=tk, t_len=t),
        grid=(b, t // tq),
        in_specs=[pl.BlockSpec((1, tq, Q_PAD), lambda bi, i: (bi, i, 0)),
                  pl.BlockSpec((1, tq, LANES), lambda bi, i: (bi, i, 0)),
                  pl.BlockSpec((1, n_cmp, 2 * LANES), lambda bi, i: (bi, 0, 0)),
                  pl.BlockSpec((1, t, NSA_KV_WIDTH), lambda bi, i: (bi, 0, 0)),
                  pl.BlockSpec(poolt.shape, lambda bi, i: (0, 0)),
                  pl.BlockSpec(e_mat.shape, lambda bi, i: (0, 0))],
        out_specs=pl.BlockSpec((1, tq, NSA_WIDTH), lambda bi, i: (bi, i, 0)),
        out_shape=jax.ShapeDtypeStruct((b, t, NSA_WIDTH), F32),
        scratch_shapes=[pltpu.VMEM((t // tk, tq, tk), F32), pltpu.VMEM((NSA_REP * tq, LANES), F32)],
        compiler_params=_cparams("parallel", "arbitrary"),
        name="nsa_prompt",
    )(q3, gates3, kcvc, kvb3, poolt, e_mat)


def _memkv_kernel(x_ref, g_ref, w_ref, gk_ref, seg_ref, o_ref, ob_ref):
    a = _rms(x_ref[...]) * g_ref[...]
    z = _dot(a.astype(BF16), w_ref[...])
    for c in range(MEM_WIDTH // LANES):
        sl = slice(c * LANES, (c + 1) * LANES)
        xc = z[:, sl]
        ssq = _dot((xc * xc).astype(BF16), seg_ref[...])
        kn = xc * lax.rsqrt(ssq * (1.0 / HEAD_DIM) + EPS) * gk_ref[:, sl]
        o_ref[:, sl] = kn
        ob_ref[:, sl] = kn.astype(BF16)
    o_ref[:, MEM_WIDTH:] = z[:, MEM_WIDTH:]
    ob_ref[:, MEM_WIDTH:] = z[:, MEM_WIDTH:].astype(BF16)


def _memkv_call(mem2d, g_mem, w_mem, gmk, seg, *, tm):
    n, dm = mem2d.shape
    full = lambda shape: pl.BlockSpec(shape, lambda i: (0,) * len(shape))
    return pl.pallas_call(
        _memkv_kernel,
        grid=(n // tm,),
        in_specs=[pl.BlockSpec((tm, dm), lambda i: (i, 0)), full((1, dm)), full((dm, 2 * MEM_WIDTH)),
                  full((1, MEM_WIDTH)), full((LANES, LANES))],
        out_specs=[pl.BlockSpec((tm, 2 * MEM_WIDTH), lambda i: (i, 0))] * 2,
        out_shape=[jax.ShapeDtypeStruct((n, 2 * MEM_WIDTH), F32), jax.ShapeDtypeStruct((n, 2 * MEM_WIDTH), BF16)],
        compiler_params=_cparams("parallel"),
        name="memkv",
    )(mem2d, g_mem, w_mem, gmk, seg)


def _mem_attend(qm, mkv, rows):
    lane = lax.broadcasted_iota(I32, (1, LANES), 1)
    chunks = []
    for j in range(MEM_HEADS // 2):
        k128 = mkv[:, j * LANES:(j + 1) * LANES]
        v128 = mkv[:, MEM_WIDTH + j * LANES:MEM_WIDTH + (j + 1) * LANES]
        q2 = jnp.concatenate([qm[2 * j], qm[2 * j + 1]], axis=0)
        s = _dot_nt(q2, k128)
        e = jnp.exp(s - jnp.max(s, axis=-1, keepdims=True))
        p = e / jnp.sum(e, axis=-1, keepdims=True)
        o = _dot(p.astype(BF16), v128)
        chunks.append(jnp.where(lane < HEAD_DIM, o[:rows], o[rows:2 * rows]))
    return chunks


def _memattn_kernel(qm_ref, mkv_ref, o_ref, *, tq):
    qm = [qm_ref[0, :, h * LANES:(h + 1) * LANES] for h in range(MEM_HEADS)]
    for j, chunk in enumerate(_mem_attend(qm, mkv_ref[0], tq)):
        o_ref[0, :, j * LANES:(j + 1) * LANES] = chunk


def _memattn_call(qm3, mkv3, *, tq):
    b, t, _ = qm3.shape
    m = mkv3.shape[1]
    return pl.pallas_call(
        functools.partial(_memattn_kernel, tq=tq),
        grid=(b, t // tq),
        in_specs=[pl.BlockSpec((1, tq, QM_PAD), lambda bi, i: (bi, i, 0)),
                  pl.BlockSpec((1, m, 2 * MEM_WIDTH), lambda bi, i: (bi, 0, 0))],
        out_specs=pl.BlockSpec((1, tq, MEM_WIDTH), lambda bi, i: (bi, i, 0)),
        out_shape=jax.ShapeDtypeStruct((b, t, MEM_WIDTH), F32),
        compiler_params=_cparams("parallel", "parallel"),
        name="memattn",
    )(qm3, mkv3)


def _pad_rows(rows_list):
    x = jnp.concatenate(rows_list, axis=0)
    return jnp.concatenate([x, jnp.zeros((8 - x.shape[0], x.shape[1]), x.dtype)], axis=0)


def _sample_attn1_kernel(q_ref, kc_ref, win_ref, mkv_ref, qm_ref, pool_ref,
                         oc_ref, ow_ref, om_ref, idx_ref, *, t_pos, n_win):
    q = q_ref[0].astype(F32)
    n_cmp = kc_ref.shape[1]
    kc128 = kc_ref[0, :, :LANES]
    vc128 = kc_ref[0, :, LANES:]
    kw128 = win_ref[0, :, :LANES].astype(BF16)
    vw128 = win_ref[0, :, LANES:].astype(BF16)
    kc_end = lax.broadcasted_iota(I32, (1, n_cmp), 1) * CMP_STRIDE + (CMP_BLOCK - 1)
    mask_c = kc_end <= t_pos
    kw_pos = t_pos - (n_win - 1) + lax.broadcasted_iota(I32, (1, n_win), 1)
    mask_w = (t_pos - kw_pos >= 0) & (t_pos - kw_pos < WINDOW) & (kw_pos >= 0)
    n_slot = pool_ref.shape[1]
    blk = lax.broadcasted_iota(I32, (1, n_slot), 1)
    cur = t_pos // SEL_BLOCK
    valid = blk * SEL_BLOCK <= t_pos
    forced = (blk == 0) | (blk == cur) | (blk == cur - 1)
    lane = lax.broadcasted_iota(I32, (1, LANES), 1)
    for g in range(NSA_KV_HEADS):
        qg = _pad_rows([q[:, (g * NSA_REP + r) * LANES:(g * NSA_REP + r + 1) * LANES]
                        for r in range(NSA_REP)]).astype(BF16)
        p_c = _masked_softmax(_dot_nt(qg, kc128), mask_c)
        oc_ref[0, g * NSA_REP:(g + 1) * NSA_REP, :] = _dot(p_c.astype(BF16), vc128)[:NSA_REP]
        imp = jnp.sum(p_c[:NSA_REP], axis=0, keepdims=True)
        imp8 = jnp.broadcast_to(imp, (8, n_cmp))
        imp_b = sum(_dot(piece, pool_ref[...]) for piece in _split3(imp8))[0:1]
        score = jnp.where(valid, imp_b + jnp.where(forced, FORCE_BONUS, 0.0), -jnp.inf)
        n_valid_blocks = t_pos // SEL_BLOCK + 1
        _, firsts = _topk_mask(score, blk.astype(F32), min(SEL_TOPK, n_valid_blocks), 1)
        idxv = jnp.full((1, LANES), -1, I32)
        for j, f in enumerate(firsts):
            idxv = jnp.where(lane == j, f.astype(I32), idxv)
        idx_ref[0, g:g + 1, :] = idxv
        p_w = _masked_softmax(_dot_nt(qg, kw128), mask_w)
        ow_ref[0, g * NSA_REP:(g + 1) * NSA_REP, :] = _dot(p_w.astype(BF16), vw128)[:NSA_REP]
    qm = qm_ref[0].astype(F32)
    qmh = []
    for h in range(MEM_HEADS):
        row = qm[:, h * LANES:(h + 1) * LANES]
        qmh.append(jnp.concatenate([row, jnp.zeros((7, LANES), F32)], axis=0).astype(BF16))
    for j, chunk in enumerate(_mem_attend(qmh, mkv_ref[0].astype(BF16), 8)):
        om_ref[0, :, j * LANES:(j + 1) * LANES] = chunk[0:1]


def _sample_attn1_call(q3, kcvc, win3, mkv3, qm3, pool_mat, *, t_pos):
    b = q3.shape[0]
    n_cmp = kcvc.shape[1]
    n_win = win3.shape[1]
    m = mkv3.shape[1]
    per_b = lambda shape: pl.BlockSpec((1,) + shape, lambda bi: (bi, 0, 0))
    return pl.pallas_call(
        functools.partial(_sample_attn1_kernel, t_pos=t_pos, n_win=n_win),
        grid=(b,),
        in_specs=[per_b((1, Q_PAD)), per_b((n_cmp, 2 * LANES)), per_b((n_win, 2 * LANES)),
                  per_b((m, 2 * MEM_WIDTH)), per_b((1, QM_PAD)),
                  pl.BlockSpec(pool_mat.shape, lambda bi: (0, 0))],
        out_specs=[per_b((NSA_HEADS, LANES)), per_b((NSA_HEADS, LANES)), per_b((1, MEM_WIDTH)),
                   per_b((NSA_KV_HEADS, LANES))],
        out_shape=[jax.ShapeDtypeStruct((b, NSA_HEADS, LANES), F32), jax.ShapeDtypeStruct((b, NSA_HEADS, LANES), F32),
                   jax.ShapeDtypeStruct((b, 1, MEM_WIDTH), F32), jax.ShapeDtypeStruct((b, NSA_KV_HEADS, LANES), I32)],
        compiler_params=_cparams("parallel"),
        name="sample_attn1",
    )(q3, kcvc, win3, mkv3, qm3, pool_mat)


def _sample_sel_kernel(pt_ref, idx_ref, cache_ref, q_ref, knew_ref, gate_ref, oc_ref, ow_ref, o_ref,
                       buf, sem, *, t_pos, n_pages, n_top):
    b = pl.program_id(0)
    nb = pl.num_programs(0)
    n_past_blk = n_pages * 2
    idx_stride = NSA_KV_HEADS * LANES

    def blk_at(bb, g, j):
        return idx_ref[bb * idx_stride + g * LANES + j]

    def blk_copy(bb, slot, g, j, kv):
        blkc = jnp.clip(blk_at(bb, g, j), 0, n_past_blk - 1)
        pg = pt_ref[bb * n_pages + blkc // 2]
        off = pl.multiple_of((blkc % 2) * SEL_BLOCK, SEL_BLOCK)
        return pltpu.make_async_copy(cache_ref.at[pg, pl.ds(off, SEL_BLOCK), 2 + kv],
                                     buf.at[slot, g, kv, pl.ds(j * SEL_BLOCK, SEL_BLOCK)], sem.at[slot])

    def for_all(fn):
        for g in range(NSA_KV_HEADS):
            for j in range(n_top):
                for kv in range(2):
                    fn(g, j, kv)

    def issue(bb, slot):
        for_all(lambda g, j, kv: blk_copy(bb, slot, g, j, kv).start())

    @pl.when(b == 0)
    def _():
        issue(0, 0)

    @pl.when(b + 1 < nb)
    def _():
        issue(b + 1, (b + 1) % 2)

    slot = b % 2
    for_all(lambda g, j, kv: blk_copy(b, slot, g, j, kv).wait())

    q = q_ref[0].astype(F32)
    knew = knew_ref[0]
    k_new = knew[:, :LANES].astype(BF16).astype(F32)
    v_new = knew[:, LANES:].astype(BF16).astype(F32)
    n_keys = n_top * SEL_BLOCK
    key_lane = lax.broadcasted_iota(I32, (1, n_keys), 1)
    key_slot = key_lane // SEL_BLOCK
    key_row = key_lane % SEL_BLOCK
    cur_blk = t_pos // SEL_BLOCK
    o_s = []
    for g in range(NSA_KV_HEADS):
        qg = _pad_rows([q[:, (g * NSA_REP + r) * LANES:(g * NSA_REP + r + 1) * LANES] for r in range(NSA_REP)])
        blkvec = jnp.full((1, n_keys), -1, I32)
        has_cur = jnp.zeros((1, 1), jnp.bool_)
        for j in range(n_top):
            bj = blk_at(b, g, j)
            blkvec = jnp.where(key_slot == j, bj, blkvec)
            has_cur = has_cur | (bj == cur_blk)
        vis = (blkvec >= 0) & (blkvec < n_past_blk) & (blkvec * SEL_BLOCK + key_row <= t_pos)
        zeros = jnp.zeros((n_keys, HEAD_DIM), F32)
        pad = (lambda x: jnp.concatenate([x, zeros], axis=1)) if g == 0 else (lambda x: jnp.concatenate([zeros, x], axis=1))
        k128 = pad(buf[slot, g, 0, :, g, :]).astype(BF16)
        v128 = pad(buf[slot, g, 1, :, g, :]).astype(BF16)
        s_past = jnp.where(vis, _dot_nt(qg.astype(BF16), k128), NEG)
        s_new = jnp.where(has_cur, jnp.sum(qg * k_new, axis=-1, keepdims=True), NEG)
        m = jnp.maximum(jnp.max(s_past, axis=-1, keepdims=True), s_new)
        e_p = jnp.where(vis, jnp.exp(s_past - m), 0.0)
        e_n = jnp.where(has_cur, jnp.exp(s_new - m), 0.0)
        den = jnp.maximum(jnp.sum(e_p, axis=-1, keepdims=True) + e_n, 1e-30)
        num = _dot(e_p.astype(BF16), v128) + e_n.astype(BF16).astype(F32) * v_new
        o_s.append(num / den)
    o_c = [jnp.concatenate([oc_ref[0, g * NSA_REP:(g + 1) * NSA_REP, :]] * 2, axis=0) for g in range(NSA_KV_HEADS)]
    o_w = [jnp.concatenate([ow_ref[0, g * NSA_REP:(g + 1) * NSA_REP, :]] * 2, axis=0) for g in range(NSA_KV_HEADS)]

    def store(j, chunk):
        o_ref[0, :, j * LANES:(j + 1) * LANES] = chunk

    _gate_and_pack(store, gate_ref[0], o_c, o_s, o_w, 1)


def _sample_sel_call(page_table, idx, cache3, q3, knew3, gates3, oc, ow, *, t_pos, n_top):
    b, n_pages = page_table.shape
    per_b = lambda shape: pl.BlockSpec((1,) + shape, lambda bi, pt, ix: (bi, 0, 0))
    grid_spec = pltpu.PrefetchScalarGridSpec(
        num_scalar_prefetch=2,
        grid=(b,),
        in_specs=[pl.BlockSpec(memory_space=pl.ANY), per_b((1, Q_PAD)), per_b((1, 2 * LANES)), per_b((1, LANES)),
                  per_b((NSA_HEADS, LANES)), per_b((NSA_HEADS, LANES))],
        out_specs=per_b((1, NSA_WIDTH)),
        scratch_shapes=[pltpu.VMEM((2, NSA_KV_HEADS, 2, n_top * SEL_BLOCK, NSA_KV_HEADS, HEAD_DIM), F32),
                        pltpu.SemaphoreType.DMA((2,))],
    )
    return pl.pallas_call(
        functools.partial(_sample_sel_kernel, t_pos=t_pos, n_pages=n_pages, n_top=n_top),
        grid_spec=grid_spec,
        out_shape=jax.ShapeDtypeStruct((b, 1, NSA_WIDTH), F32),
        compiler_params=_cparams("arbitrary"),
        name="sample_sel",
    )(page_table.reshape(-1), idx.reshape(-1), cache3, q3, knew3, gates3, oc, ow)


def _finish_kernel(yp_ref, on_ref, om_ref, x_ref, gm_ref, wo_ref, gf_ref, wr_ref, br_ref, tri_ref,
                   h_ref, hn_ref, route_ref, counts_ref, cnt_ref):
    gm = gm_ref[...]
    o1 = POOL_WIDTH
    o2 = o1 + NSA_WIDTH
    mixed = jnp.concatenate([_rms(yp_ref[...]) * gm[:, :o1], _rms(on_ref[...]) * gm[:, o1:o2],
                             _rms(om_ref[...]) * gm[:, o2:]], axis=-1)
    h = x_ref[...] + _dot(mixed.astype(BF16), wo_ref[...])
    h_ref[...] = h
    hn = _rms(h) * gf_ref[...]
    hn_ref[...] = hn
    logits = _dot(hn.astype(BF16), wr_ref[...]) + br_ref[...]
    lane = lax.broadcasted_iota(I32, (1, LANES), 1)
    lane_f = lane.astype(F32)
    is1 = lane < N_EXPERT_GROUPS
    m1 = jnp.max(jnp.where(is1, logits, -jnp.inf), axis=-1, keepdims=True)
    e1 = jnp.where(is1, jnp.exp(logits - m1), 0.0)
    p1 = e1 / jnp.sum(e1, axis=-1, keepdims=True)
    top1_p = jnp.max(p1, axis=-1, keepdims=True)
    grp = jnp.min(jnp.where((p1 == top1_p) & is1, lane_f, float(LANES)), axis=-1, keepdims=True)
    base = N_EXPERT_GROUPS + grp * EXPERTS_PER_GROUP
    in_g = (lane_f >= base) & (lane_f < base + EXPERTS_PER_GROUP)
    l2 = jnp.where(in_g, logits, -jnp.inf)
    v0 = jnp.max(l2, axis=-1, keepdims=True)
    i0 = jnp.min(jnp.where(l2 == v0, lane_f, float(LANES)), axis=-1, keepdims=True)
    l2b = jnp.where(lane_f == i0, -jnp.inf, l2)
    v1 = jnp.max(l2b, axis=-1, keepdims=True)
    i1 = jnp.min(jnp.where(l2b == v1, lane_f, float(LANES)), axis=-1, keepdims=True)
    ex = jnp.exp(v1 - v0)
    w0 = top1_p / (1.0 + ex)
    w1 = top1_p * ex / (1.0 + ex)
    ex0 = i0 - N_EXPERT_GROUPS
    ex1 = i1 - N_EXPERT_GROUPS
    is0 = lane_f == ex0
    is1e = lane_f == ex1
    oh0 = jnp.where(is0, 1.0, 0.0)
    oh1 = jnp.where(is1e, 1.0, 0.0)
    before0 = _dot(tri_ref[...], oh0.astype(BF16))
    before1 = _dot(tri_ref[...], oh1.astype(BF16))
    tot0 = jnp.sum(oh0, axis=0, keepdims=True)
    tot1 = jnp.sum(oh1, axis=0, keepdims=True)

    @pl.when(pl.program_id(0) == 0)
    def _():
        cnt_ref[...] = jnp.zeros(cnt_ref.shape, F32)

    seen = cnt_ref[...]
    rank0 = jnp.sum(jnp.where(is0, before0 + seen, 0.0), axis=-1, keepdims=True)
    rank1 = jnp.sum(jnp.where(is1e, before1 + seen + tot0, 0.0), axis=-1, keepdims=True)
    cnt_ref[...] = seen + tot0 + tot1
    counts_ref[...] = seen + tot0 + tot1
    route = jnp.where(lane == 0, ex0, jnp.where(lane == 1, ex1, jnp.where(lane == 2, w0, jnp.where(lane == 3, w1,
            jnp.where(lane == 4, rank0, jnp.where(lane == 5, rank1, 0.0))))))
    route_ref[...] = route


def _finish_call(yp, on, om, x2d, fw, *, tm):
    n, dm = x2d.shape
    full = lambda shape: pl.BlockSpec(shape, lambda i: (0,) * len(shape))
    rows = lambda w: pl.BlockSpec((tm, w), lambda i: (i, 0))
    tri = jnp.asarray(np.arange(tm)[None, :] < np.arange(tm)[:, None], BF16)
    return pl.pallas_call(
        _finish_kernel,
        grid=(n // tm,),
        in_specs=[rows(POOL_WIDTH), rows(NSA_WIDTH), rows(MEM_WIDTH), rows(dm), full((1, dm)), full((dm, dm)),
                  full((1, dm)), full((dm, LANES)), full((1, LANES)), full((tm, tm))],
        out_specs=[rows(dm), rows(dm), rows(LANES), full((1, LANES))],
        out_shape=[jax.ShapeDtypeStruct((n, dm), F32), jax.ShapeDtypeStruct((n, dm), F32),
                   jax.ShapeDtypeStruct((n, LANES), F32), jax.ShapeDtypeStruct((1, LANES), F32)],
        scratch_shapes=[pltpu.VMEM((1, LANES), F32)],
        compiler_params=_cparams("arbitrary"),
        name="finish",
    )(yp, on, om, x2d, fw["g_mix"], fw["w_out"], fw["g_ffn"], fw["w_r"], fw["b_r"], tri)


def _route_tables(route, counts, tm):
    n = route.shape[0]
    eid = route[:, 0:2].astype(I32)
    rank = route[:, 4:6].astype(I32)
    cnt = counts[0, :N_EXPERTS].astype(I32)
    tiles_per = (cnt + tm - 1) // tm
    tile_end = jnp.cumsum(tiles_per)
    tile_start = tile_end - tiles_per
    experts = jnp.arange(N_EXPERTS, dtype=I32)
    start_of = jnp.sum(jnp.where(eid[:, :, None] == experts, tile_start, 0), axis=-1)
    pos = (start_of * tm + rank).reshape(-1)
    n_tiles = (2 * n) // tm + N_EXPERTS
    tj = jnp.arange(n_tiles, dtype=I32)
    tile_expert = jnp.minimum(jnp.sum((tj[:, None] >= tile_end[None, :]).astype(I32), axis=1), N_EXPERTS - 1)
    sel = tile_expert[:, None] == experts
    rows_left = jnp.sum(jnp.where(sel, cnt, 0), axis=-1) - (tj - jnp.sum(jnp.where(sel, tile_start, 0), axis=-1)) * tm
    tile_nvalid = jnp.where(tj < tile_end[-1], jnp.clip(rows_left, 0, tm), 0).astype(I32)
    return tile_expert, tile_nvalid, pos


def _row_wait_all(src_row, dst_row, sem, count):
    for _ in range(count):
        pltpu.make_async_copy(src_row, dst_row, sem).wait()


def _dispatch_kernel(pos_ref, nv_ref, hn_ref, xs_ref, stage, zbuf, sem, zsem, *, tm, tme, n_tiles):
    i = pl.program_id(0)
    nt = pl.num_programs(0)
    slot = i % 2
    wait_slot = lambda s: _row_wait_all(stage.at[s, pl.ds(0, 1), :], xs_ref.at[pl.ds(0, 1), :], sem.at[s], 2 * tm)

    @pl.when(i == 0)
    def _():
        zbuf[...] = jnp.zeros(zbuf.shape, F32)

        def fill(j, c):
            @pl.when(nv_ref[j] < tme)
            def _():
                pltpu.make_async_copy(zbuf, xs_ref.at[pl.ds(pl.multiple_of(j * tme, tme), tme), :], zsem).start()
            return c

        def drain(j, c):
            @pl.when(nv_ref[j] < tme)
            def _():
                pltpu.make_async_copy(zbuf, xs_ref.at[pl.ds(0, tme), :], zsem).wait()
            return c

        lax.fori_loop(0, n_tiles, fill, 0)
        lax.fori_loop(0, n_tiles, drain, 0)

    @pl.when(i >= 2)
    def _():
        wait_slot(slot)

    stage[slot] = hn_ref[...]
    base = i * (2 * tm)
    for r in range(tm):
        for k in range(2):
            dst = pos_ref[base + 2 * r + k]
            pltpu.make_async_copy(stage.at[slot, pl.ds(r, 1), :], xs_ref.at[pl.ds(dst, 1), :], sem.at[slot]).start()

    @pl.when(i == nt - 1)
    def _():
        wait_slot(slot)

        @pl.when(i >= 1)
        def _():
            wait_slot(1 - slot)


def _dispatch_call(pos, tile_nvalid, hn, *, tm, tme):
    n, dm = hn.shape
    n_tiles = tile_nvalid.shape[0]
    grid_spec = pltpu.PrefetchScalarGridSpec(
        num_scalar_prefetch=2,
        grid=(n // tm,),
        in_specs=[pl.BlockSpec((tm, dm), lambda i, pos, nv: (i, 0))],
        out_specs=pl.BlockSpec(memory_space=pl.ANY),
        scratch_shapes=[pltpu.VMEM((2, tm, dm), F32), pltpu.VMEM((tme, dm), F32),
                        pltpu.SemaphoreType.DMA((2,)), pltpu.SemaphoreType.DMA(())],
    )
    return pl.pallas_call(
        functools.partial(_dispatch_kernel, tm=tm, tme=tme, n_tiles=n_tiles),
        grid_spec=grid_spec,
        out_shape=jax.ShapeDtypeStruct((n_tiles * tme, dm), F32),
        compiler_params=_cparams("arbitrary"),
        name="moe_dispatch",
    )(pos, tile_nvalid, hn)


def _expert_kernel(te_ref, nv_ref, x_ref, wg_ref, wu_ref, wd_ref, y_ref, wgb, wub, wdb, *, tm):
    i = pl.program_id(0)
    prev = te_ref[jnp.maximum(i - 1, 0)]

    @pl.when((i == 0) | (te_ref[i] != prev))
    def _():
        wgb[...] = wg_ref[0].astype(BF16)
        wub[...] = wu_ref[0].astype(BF16)
        wdb[...] = wd_ref[0].astype(BF16)

    nv = nv_ref[i]

    @pl.when(nv == 0)
    def _():
        y_ref[...] = jnp.zeros(y_ref.shape, F32)

    @pl.when(nv > 0)
    def _():
        x = x_ref[...].astype(BF16)
        hg = _dot(x, wgb[...])
        hu = _dot(x, wub[...])
        hmid = hg * jax.nn.sigmoid(hg) * hu
        y_ref[...] = _dot(hmid.astype(BF16), wdb[...])


def _expert_call(tile_expert, tile_nvalid, xs, wg, wu, wd, *, tm):
    n_tiles = tile_expert.shape[0]
    dm = xs.shape[1]
    ff = wg.shape[2]
    grid_spec = pltpu.PrefetchScalarGridSpec(
        num_scalar_prefetch=2,
        grid=(n_tiles,),
        in_specs=[pl.BlockSpec((tm, dm), lambda i, te, nv: (i, 0)),
                  pl.BlockSpec((1, dm, ff), lambda i, te, nv: (te[i], 0, 0)),
                  pl.BlockSpec((1, dm, ff), lambda i, te, nv: (te[i], 0, 0)),
                  pl.BlockSpec((1, ff, dm), lambda i, te, nv: (te[i], 0, 0))],
        out_specs=pl.BlockSpec((tm, dm), lambda i, te, nv: (i, 0)),
        scratch_shapes=[pltpu.VMEM((dm, ff), BF16), pltpu.VMEM((dm, ff), BF16), pltpu.VMEM((ff, dm), BF16)],
    )
    return pl.pallas_call(
        functools.partial(_expert_kernel, tm=tm),
        grid_spec=grid_spec,
        out_shape=jax.ShapeDtypeStruct(xs.shape, F32),
        compiler_params=_cparams("arbitrary"),
        name="moe_experts",
    )(tile_expert, tile_nvalid, xs, wg, wu, wd)


def _combine_kernel(pos_ref, ys_ref, h_ref, route_ref, o_ref, gbuf, sem, *, tm):
    i = pl.program_id(0)
    nt = pl.num_programs(0)
    slot = i % 2

    def issue(step, s):
        base = step * (2 * tm)
        for r in range(tm):
            for k in range(2):
                src = pos_ref[base + 2 * r + k]
                pltpu.make_async_copy(ys_ref.at[pl.ds(src, 1), :], gbuf.at[s, k, pl.ds(r, 1), :], sem.at[s]).start()

    @pl.when(i == 0)
    def _():
        issue(0, 0)

    @pl.when(i + 1 < nt)
    def _():
        issue(i + 1, 1 - slot)

    _row_wait_all(ys_ref.at[pl.ds(0, 1), :], gbuf.at[slot, 0, pl.ds(0, 1), :], sem.at[slot], 2 * tm)
    w0 = route_ref[:, 2:3]
    w1 = route_ref[:, 3:4]
    o_ref[...] = h_ref[...] + (w0 * gbuf[slot, 0] + w1 * gbuf[slot, 1])


def _combine_call(pos, ys, h, route, *, tm):
    n, dm = h.shape
    grid_spec = pltpu.PrefetchScalarGridSpec(
        num_scalar_prefetch=1,
        grid=(n // tm,),
        in_specs=[pl.BlockSpec(memory_space=pl.ANY),
                  pl.BlockSpec((tm, dm), lambda i, pos: (i, 0)),
                  pl.BlockSpec((tm, LANES), lambda i, pos: (i, 0))],
        out_specs=pl.BlockSpec((tm, dm), lambda i, pos: (i, 0)),
        scratch_shapes=[pltpu.VMEM((2, 2, tm, dm), F32), pltpu.SemaphoreType.DMA((2,))],
    )
    return pl.pallas_call(
        functools.partial(_combine_kernel, tm=tm),
        grid_spec=grid_spec,
        out_shape=jax.ShapeDtypeStruct((n, dm), F32),
        compiler_params=_cparams("arbitrary"),
        name="moe_combine",
    )(pos, ys, h, route)


def _moe_sample_kernel(hn_ref, route_ref, h_ref, wg_ref, wu_ref, wd_ref, o_ref, acc_ref):
    e = pl.program_id(0)

    @pl.when(e == 0)
    def _():
        acc_ref[...] = jnp.zeros(acc_ref.shape, F32)

    route = route_ref[...]
    ef = e.astype(F32)
    comb = jnp.where(route[:, 0:1] == ef, route[:, 2:3], 0.0) + jnp.where(route[:, 1:2] == ef, route[:, 3:4], 0.0)
    x = hn_ref[...].astype(BF16)
    hg = _dot(x, wg_ref[0].astype(BF16))
    hu = _dot(x, wu_ref[0].astype(BF16))
    hmid = hg * jax.nn.sigmoid(hg) * hu * comb
    acc_ref[...] += _dot(hmid.astype(BF16), wd_ref[0].astype(BF16))

    @pl.when(e == pl.num_programs(0) - 1)
    def _():
        o_ref[...] = h_ref[...] + acc_ref[...]


def _moe_sample_call(hn, route, h, wg, wu, wd):
    n, dm = hn.shape
    ff = wg.shape[2]
    full = lambda shape: pl.BlockSpec(shape, lambda e: (0,) * len(shape))
    return pl.pallas_call(
        _moe_sample_kernel,
        grid=(N_EXPERTS,),
        in_specs=[full((n, dm)), full((n, LANES)), full((n, dm)),
                  pl.BlockSpec((1, dm, ff), lambda e: (e, 0, 0)), pl.BlockSpec((1, dm, ff), lambda e: (e, 0, 0)),
                  pl.BlockSpec((1, ff, dm), lambda e: (e, 0, 0))],
        out_specs=full((n, dm)),
        out_shape=jax.ShapeDtypeStruct((n, dm), F32),
        scratch_shapes=[pltpu.VMEM((n, dm), F32)],
        compiler_params=_cparams("arbitrary"),
        name="moe_sample",
    )(hn, route, h, wg, wu, wd)


def _prep_weights(l, g_attn, w_in, g_q, g_k, pe_cmp, w_cmp1, w_cmp2, w_pool, s_pool, g_mem, w_mem_kv, g_mq, g_mk,
                  g_mix, w_out, g_ffn, w_router1, b_router1, w_router2, b_router2):
    dm = w_in.shape[1]
    w = w_in[l]
    o1 = POOL_WIDTH
    o2 = o1 + NSA_WIDTH
    o3 = o2 + NSA_KV_WIDTH
    o4 = o3 + GATE_WIDTH
    wq = w[:, o1:o2].reshape(dm, NSA_HEADS, 1, HEAD_DIM)
    q_slot = jnp.asarray(np.eye(NSA_KV_HEADS, dtype=np.float32)[np.arange(NSA_HEADS) // NSA_REP])
    wq_pad = (wq * q_slot[None, :, :, None]).reshape(dm, Q_PAD)
    wqm = w[:, o4:].reshape(dm, MEM_HEADS, 1, HEAD_DIM)
    m_slot = jnp.asarray(np.eye(2, dtype=np.float32)[np.arange(MEM_HEADS) % 2])
    wqm_pad = (wqm * m_slot[None, :, :, None]).reshape(dm, QM_PAD)
    wg_pad = jnp.pad(w[:, o3:o4], ((0, 0), (0, LANES - GATE_WIDTH)))
    w_packed = jnp.concatenate([w[:, :o1], wq_pad, w[:, o2:o3], wqm_pad, wg_pad], axis=1).astype(BF16)
    half = ROPE_DIM // 2
    inv = jnp.power(ROPE_THETA, -jnp.arange(half, dtype=F32) * 2.0 / ROPE_DIM)
    d = np.arange(LANES) % HEAD_DIM
    inv_lane = jnp.where(jnp.asarray(d < ROPE_DIM), inv[jnp.asarray(d % half)], 0.0).reshape(1, LANES)
    seg = jnp.asarray((np.arange(LANES)[:, None] // HEAD_DIM == np.arange(LANES)[None, :] // HEAD_DIM), BF16)
    gk = jnp.concatenate([jnp.tile(g_k[l, br], 2) for br in range(N_BRANCH)]).reshape(1, N_BRANCH * LANES)
    proj = dict(g_attn=g_attn[l].reshape(1, dm), w_in=w_packed, gq=jnp.tile(g_q[l], Q_PAD // HEAD_DIM).reshape(1, Q_PAD),
                gk=gk, gmq=jnp.tile(g_mq[l], QM_PAD // HEAD_DIM).reshape(1, QM_PAD), inv=inv_lane, seg=seg)

    n_grp = len(POOL_WINDOWS)
    pg = POOL_WIDTH // n_grp
    w_bd = (jnp.asarray(np.eye(n_grp, dtype=np.float32))[:, None, :, None] * w_pool[l][:, :, None, :]
            ).reshape(POOL_WIDTH, POOL_WIDTH).astype(BF16)
    pool = dict(w=w_bd, s=s_pool[l].reshape(1, POOL_WIDTH))

    half_rows = CMP_STRIDE * HEAD_DIM
    w1 = jnp.concatenate([w_cmp1[l][:, :half_rows], w_cmp1[l][:, half_rows:]], axis=2).astype(BF16)
    cmp_w = dict(w=w1, pe=pe_cmp[l].reshape(2, 1, CMP_BLOCK * HEAD_DIM), w2=w_cmp2[l].astype(BF16))

    mem = dict(g=g_mem[l].reshape(1, dm), w=w_mem_kv[l].astype(BF16),
               gk=jnp.tile(g_mk[l], MEM_HEADS).reshape(1, MEM_WIDTH))
    w_r = jnp.concatenate([w_router1[l], w_router2[l].reshape(dm, N_EXPERTS)], axis=1)
    w_r = jnp.pad(w_r, ((0, 0), (0, LANES - w_r.shape[1]))).astype(BF16)
    b_r = jnp.concatenate([b_router1[l], b_router2[l].reshape(-1)])
    b_r = jnp.pad(b_r, (0, LANES - b_r.shape[0])).reshape(1, LANES)
    fin = dict(g_mix=g_mix[l].reshape(1, dm), w_out=w_out[l].astype(BF16), g_ffn=g_ffn[l].reshape(1, dm),
               w_r=w_r, b_r=b_r)
    return proj, pool, cmp_w, mem, fin


def _pick(n, prefs):
    for p in prefs:
        if n % p == 0:
            return p
    return n


def kernel(x_prompt, x_sample, cache_kv, cache_win, state_pool, cache_mem_kv, page_table, mem_prompt, g_attn, w_in, g_q, g_k, pe_cmp, w_cmp1, w_cmp2, w_pool, s_pool, g_mem, w_mem_kv, g_mq, g_mk, g_mix, w_out, g_ffn, w_router1, b_router1, w_router2, b_router2, w_gate, w_up, w_down):
    depth = w_in.shape[0]
    bp, t, dm = x_prompt.shape
    bs, ts, _ = x_sample.shape
    n_pages = page_table.shape[1]
    page = cache_kv.shape[2]
    past_len = n_pages * page
    n_win = cache_win.shape[2]
    assert ts == 1 and n_win == WINDOW and past_len % SEL_BLOCK == 0 and t % WINDOW == 0 and t >= WINDOW + 128
    ff = w_gate.shape[-1]
    tq, tk = 128, 512
    n_sel = t // SEL_BLOCK
    assert min(SEL_TOPK, n_sel) >= 3
    poolt =jnp.asarray(np.arange(t // CMP_STRIDE)[None, :] // CMP_PER_SEL == np.arange(n_sel)[:, None], BF16)
    e_mat = jnp.asarray(np.arange(t)[None, :] // SEL_BLOCK == np.arange(n_sel)[:, None], BF16)
    n_cmp_s = past_len // CMP_STRIDE
    n_slot_s = -(-(past_len // SEL_BLOCK + 1) // LANES) * LANES
    pool_s = jnp.asarray(np.arange(n_cmp_s)[:, None] // CMP_PER_SEL == np.arange(n_slot_s)[None, :], BF16)

    hp, hs = x_prompt, x_sample
    outs = [[] for _ in range(7)]
    for l in range(depth):
        proj_w, pool_w, cmp_w, mem_w, fin_w = _prep_weights(
            l, g_attn, w_in, g_q, g_k, pe_cmp, w_cmp1, w_cmp2, w_pool, s_pool, g_mem, w_mem_kv, g_mq, g_mk,
            g_mix, w_out, g_ffn, w_router1, b_router1, w_router2, b_router2)
        wg = w_gate[l].reshape(N_EXPERTS, dm, ff)
        wu = w_up[l].reshape(N_EXPERTS, dm, ff)
        wd = w_down[l].reshape(N_EXPERTS, ff, dm)
        kw_cols = CMP_STRIDE * 2 * LANES

        n = bp * t
        xp2 = hp.reshape(n, dm)
        u, qb, rows4, rowsw, kvb, gates, qmb = _proj_call(xp2, proj_w, seq_len=t, pos0=0, tm=_pick(n, (512, 256, 128)))
        u3 = u.reshape(bp, t, POOL_WIDTH)
        y_pool = _pool_call(u3, pool_w["w"], pool_w["s"], tp=_pick(t, (512, 256, 128)), pos0=0)
        n_slab = 2 * NSA_KV_HEADS
        chunk_w = CMP_STRIDE * HEAD_DIM
        kcvc = _compress_prompt_call(rows4.reshape(bp, t, 4 * LANES), jnp.zeros((bp, n_slab, chunk_w), F32), cmp_w)
        o_nsa = _nsa_prompt_call(qb.reshape(bp, t, Q_PAD), gates.reshape(bp, t, LANES), kcvc,
                                 kvb.reshape(bp, t, NSA_KV_WIDTH), poolt, e_mat, tq=tq, tk=tk)
        m_len = mem_prompt.shape[1]
        mkv, mkvb = _memkv_call(mem_prompt.reshape(bp * m_len, dm), mem_w["g"], mem_w["w"], mem_w["gk"],
                                proj_w["seg"], tm=_pick(bp * m_len, (256, 128)))
        o_mem = _memattn_call(qmb.reshape(bp, t, QM_PAD), mkvb.reshape(bp, m_len, 2 * MEM_WIDTH),
                              tq=_pick(t, (512, 256, 128)))
        tmf = _pick(n, (512, 256, 128))
        h_p, hn_p, route_p, counts_p = _finish_call(y_pool.reshape(n, POOL_WIDTH), o_nsa.reshape(n, NSA_WIDTH),
                                                    o_mem.reshape(n, MEM_WIDTH), xp2, fin_w, tm=tmf)
        tme = 256
        tmd = _pick(n, (256, 128))
        te, nv, pos = _route_tables(route_p, counts_p, tme)
        xs = _dispatch_call(pos, nv, hn_p, tm=tmd, tme=tme)
        ys = _expert_call(te, nv, xs, wg, wu, wd, tm=tme)
        y_p = _combine_call(pos, ys, h_p, route_p, tm=tmd).reshape(bp, t, dm)

        xs2 = hs.reshape(bs, dm)
        u_s, qb_s, rows4_s, rowsw_s, _, gates_s, qmb_s = _proj_call(xs2, proj_w, seq_len=1, pos0=past_len, tm=bs)
        ext = jnp.concatenate([state_pool[l], u_s[:, None, :]], axis=1)
        y_pool_s = _pool_call(ext, pool_w["w"], pool_w["s"], tp=POOL_STATE + 1, pos0=past_len - POOL_STATE)[:, -1, :]
        cache5 = cache_kv[l]
        tail_s = jnp.pad(rows4_s[:, :2 * LANES].reshape(bs, n_slab, HEAD_DIM), ((0, 0), (0, 0), (0, chunk_w - HEAD_DIM)))
        kcvc_s = _compress_sample_call(page_table, cache5, tail_s, cmp_w)
        win_s = jnp.concatenate([cache_win[l].reshape(bs, n_win, 2 * LANES)[:, 1:], rowsw_s[:, None, :]], axis=1)
        mkv_s = cache_mem_kv[l].reshape(bs, cache_mem_kv.shape[2], 2 * MEM_WIDTH)
        oc, ow, om_s, idx = _sample_attn1_call(qb_s.reshape(bs, 1, Q_PAD), kcvc_s, win_s, mkv_s,
                                               qmb_s.reshape(bs, 1, QM_PAD), pool_s, t_pos=past_len)
        n_top = min(SEL_TOPK, past_len // SEL_BLOCK + 1)
        o_nsa_s = _sample_sel_call(page_table, idx, cache5, qb_s.reshape(bs, 1, Q_PAD),
                                   rows4_s[:, 2 * LANES:].reshape(bs, 1, 2 * LANES), gates_s.reshape(bs, 1, LANES),
                                   oc, ow, t_pos=past_len, n_top=n_top)
        h_s, hn_s, route_s, _ = _finish_call(y_pool_s, o_nsa_s.reshape(bs, NSA_WIDTH), om_s.reshape(bs, MEM_WIDTH),
                                          xs2, fin_w, tm=bs)
        y_s = _moe_sample_call(hn_s, route_s, h_s, wg, wu, wd).reshape(bs, 1, dm)

        keep = min(WINDOW, t)
        outs[0].append(rows4.reshape(bp, t, N_KV_SLOTS, NSA_KV_HEADS, HEAD_DIM))
        outs[1].append(rows4_s.reshape(bs, 1, N_KV_SLOTS, NSA_KV_HEADS, HEAD_DIM))
        outs[2].append(rowsw.reshape(bp, t, 2, NSA_KV_HEADS, HEAD_DIM)[:, t - keep:])
        outs[3].append(win_s.reshape(bs, n_win, 2, NSA_KV_HEADS, HEAD_DIM))
        outs[4].append(u3[:, t - POOL_STATE:])
        outs[5].append(ext[:, 1:])
        outs[6].append(mkv.reshape(bp, m_len, 2, MEM_HEADS, HEAD_DIM))
        hp, hs = y_p, y_s
    return (hp, hs) + tuple(jnp.stack(o) for o in outs)
```

```python
import functools

import numpy as np
import jax
import jax.numpy as jnp
from jax import lax
from jax.experimental import pallas as pl
from jax.experimental.pallas import tpu as pltpu

F32 = jnp.float32
BF16 = jnp.bfloat16
I32 = jnp.int32

HEAD_DIM = 64
POOL_WINDOWS = (2, 4, 8, 16)
POOL_STATE = max(POOL_WINDOWS) - 1
NSA_HEADS = 8
NSA_KV_HEADS = 2
NSA_REP = NSA_HEADS // NSA_KV_HEADS
N_BRANCH = 3
CMP_BLOCK = 32
CMP_STRIDE = 16
CMP_HIDDEN = 2 * HEAD_DIM
SEL_BLOCK = 64
SEL_TOPK = 16
CMP_PER_SEL = SEL_BLOCK // CMP_STRIDE
WINDOW = 512
FORCE_BONUS = 1000.0
MEM_HEADS = 4
ROPE_DIM = HEAD_DIM // 4
ROPE_THETA = 500000.0
N_EXPERT_GROUPS = 4
EXPERTS_PER_GROUP = 8
N_EXPERTS = N_EXPERT_GROUPS * EXPERTS_PER_GROUP
EPS = 1e-6
N_KV_SLOTS = 4

LANES = 128
POOL_WIDTH = 256
NSA_WIDTH = NSA_HEADS * HEAD_DIM
NSA_KV_WIDTH = N_BRANCH * 2 * NSA_KV_HEADS * HEAD_DIM
GATE_WIDTH = NSA_HEADS * N_BRANCH
MEM_WIDTH = MEM_HEADS * HEAD_DIM
Q_PAD = NSA_HEADS * LANES
QM_PAD = MEM_HEADS * LANES
C_U = 0
C_Q = C_U + POOL_WIDTH
C_KV = C_Q + Q_PAD
C_QM = C_KV + NSA_KV_WIDTH
C_G = C_QM + QM_PAD
C_END = C_G + LANES

NEG = -1e30
VMEM_LIMIT = 48 * 1024 * 1024

_NT = (((1,), (1,)), ((), ()))


def _cparams(*sem):
    return pltpu.CompilerParams(dimension_semantics=tuple(sem), vmem_limit_bytes=VMEM_LIMIT)


def _dot(a, b):
    return jnp.dot(a, b, preferred_element_type=F32)


def _dot_nt(a, b):
    return lax.dot_general(a, b, _NT, preferred_element_type=F32)


def _rms(x):
    return x * lax.rsqrt(jnp.mean(x * x, axis=-1, keepdims=True) + EPS)


def _masked_softmax(s, mask):
    sm = jnp.where(mask, s, NEG)
    m = jnp.max(sm, axis=-1, keepdims=True)
    e = jnp.where(mask, jnp.exp(sm - m), 0.0)
    return e / jnp.maximum(jnp.sum(e, axis=-1, keepdims=True), 1e-30)


def _split3(x):
    hi = x.astype(BF16)
    r1 = x - hi.astype(F32)
    mid = r1.astype(BF16)
    lo = (r1 - mid.astype(F32)).astype(BF16)
    return hi, mid, lo


def _proj_kernel(x_ref, ga_ref, w_ref, gq_ref, gk_ref, gmq_ref, inv_ref, seg_ref,
                 u_ref, q_ref, rows4_ref, rowsw_ref, kvb_ref, gates_ref, qm_ref, *maybe_rows4t_ref,
                 tm, seq_len, pos0):
    i = pl.program_id(0)
    a = _rms(x_ref[...]) * ga_ref[...]
    z = _dot(a.astype(BF16), w_ref[...])
    u_ref[...] = z[:, C_U:C_U + POOL_WIDTH]

    row = i * tm + lax.broadcasted_iota(I32, (tm, 1), 0)
    pos = (pos0 + row % seq_len).astype(F32)
    ang = pos * inv_ref[...]
    d = lax.broadcasted_iota(I32, (1, LANES), 1) % HEAD_DIM
    cos = jnp.cos(ang)
    sin = jnp.sin(ang)
    half = ROPE_DIM // 2
    s_next = jnp.where(d < half, -sin, 0.0)
    s_prev = jnp.where((d >= half) & (d < ROPE_DIM), sin, 0.0)
    seg = seg_ref[...]

    def head_norm(xc, g):
        ssq = _dot((xc * xc).astype(BF16), seg)
        return xc * lax.rsqrt(ssq * (1.0 / HEAD_DIM) + EPS) * g

    def rope(xc):
        return xc * cos + pltpu.roll(xc, LANES - half, 1) * s_next + pltpu.roll(xc, half, 1) * s_prev

    scale = HEAD_DIM ** -0.5
    for c in range(NSA_HEADS):
        sl = slice(c * LANES, (c + 1) * LANES)
        qc = rope(head_norm(z[:, C_Q + c * LANES:C_Q + (c + 1) * LANES], gq_ref[:, sl]))
        q_ref[:, sl] = (qc * scale).astype(BF16)
    for br in range(N_BRANCH):
        k0 = C_KV + br * 2 * LANES
        kn = rope(head_norm(z[:, k0:k0 + LANES], gk_ref[:, br * LANES:(br + 1) * LANES]))
        vv = z[:, k0 + LANES:k0 + 2 * LANES]
        kvb_ref[:, br * 2 * LANES:br * 2 * LANES + LANES] = kn.astype(BF16)
        kvb_ref[:, br * 2 * LANES + LANES:(br + 1) * 2 * LANES] = vv.astype(BF16)
        if br < 2:
            rows4_ref[:, br * 2 * LANES:br * 2 * LANES + LANES] = kn
            rows4_ref[:, br * 2 * LANES + LANES:(br + 1) * 2 * LANES] = vv
            for rows4t_ref in maybe_rows4t_ref:
                rows4t_ref[0, br * 2 * LANES:br * 2 * LANES + LANES, :] = kn.T
                rows4t_ref[0, br * 2 * LANES + LANES:(br + 1) * 2 * LANES, :] = vv.T
        else:
            rowsw_ref[:, :LANES] = kn
            rowsw_ref[:, LANES:] = vv
    for c in range(MEM_HEADS):
        sl = slice(c * LANES, (c + 1) * LANES)
        qmc = head_norm(z[:, C_QM + c * LANES:C_QM + (c + 1) * LANES], gmq_ref[:, sl])
        qm_ref[:, sl] = (qmc * scale).astype(BF16)
    gates_ref[...] = jax.nn.sigmoid(z[:, C_G:C_END])


def _proj_call(x2d, pw, *, seq_len, pos0, tm, feature_major_rows4=False):
    n, dm = x2d.shape
    full = lambda shape: pl.BlockSpec(shape, lambda i: (0,) * len(shape))
    rows = lambda w: pl.BlockSpec((tm, w), lambda i: (i, 0))
    outs = [(POOL_WIDTH, F32), (Q_PAD, BF16), (4 * LANES, F32), (2 * LANES, F32), (NSA_KV_WIDTH, BF16),
            (LANES, F32), (QM_PAD, BF16)]
    out_specs = [rows(w) for w, _ in outs]
    out_shape = [jax.ShapeDtypeStruct((n, w), dt) for w, dt in outs]
    if feature_major_rows4:
        per_seq = seq_len // tm
        out_specs.append(pl.BlockSpec((1, 4 * LANES, tm), lambda i: (i // per_seq, 0, i % per_seq)))
        out_shape.append(jax.ShapeDtypeStruct((n // seq_len, 4 * LANES, seq_len), F32))
    return pl.pallas_call(
        functools.partial(_proj_kernel, tm=tm, seq_len=seq_len, pos0=pos0),
        grid=(n // tm,),
        in_specs=[rows(dm), full((1, dm)), full((dm, C_END)), full((1, Q_PAD)), full((1, N_BRANCH * LANES)),
                  full((1, QM_PAD)), full((1, LANES)), full((LANES, LANES))],
        out_specs=out_specs,
        out_shape=out_shape,
        compiler_params=_cparams("parallel"),
        name="proj",
    )(x2d, pw["g_attn"], pw["w_in"], pw["gq"], pw["gk"], pw["gmq"], pw["inv"], pw["seg"])


def _pool_kernel(u_ref, halo_ref, w_ref, s_ref, y_ref, *, tp, pos0):
    i = pl.program_id(1)
    u = u_ref[0]
    halo = halo_ref[0] * (i > 0).astype(F32)
    n_h = POOL_STATE + 1
    ext = jnp.concatenate([halo, u], axis=0)
    sums = {1: ext}
    w = 1
    while w < max(POOL_WINDOWS):
        sums[2 * w] = sums[w] + pltpu.roll(sums[w], w, 0)
        w *= 2
    pos = pos0 + i * tp + lax.broadcasted_iota(I32, (tp, 1), 0)
    lane_grp = lax.broadcasted_iota(I32, (1, POOL_WIDTH), 1) // (POOL_WIDTH // len(POOL_WINDOWS))
    mean = jnp.zeros((tp, POOL_WIDTH), F32)
    for gi, wdw in enumerate(POOL_WINDOWS):
        cnt = jnp.minimum(pos + 1, wdw).astype(F32)
        mean = jnp.where(lane_grp == gi, sums[wdw][n_h:] / cnt, mean)
    r = mean - u
    y_ref[0] = _dot(r.astype(BF16), w_ref[...]) * s_ref[...]


def _pool_call(u3, w_bd, s_pool, *, tp, pos0):
    b, t, c = u3.shape
    n_h = POOL_STATE + 1
    return pl.pallas_call(
        functools.partial(_pool_kernel, tp=tp, pos0=pos0),
        grid=(b, t // tp),
        in_specs=[pl.BlockSpec((1, tp, c), lambda bi, i: (bi, i, 0)),
                  pl.BlockSpec((1, n_h, c), lambda bi, i: (bi, jnp.maximum(i * (tp // n_h) - 1, 0), 0)),
                  pl.BlockSpec((c, c), lambda bi, i: (0, 0)),
                  pl.BlockSpec((1, c), lambda bi, i: (0, 0))],
        out_specs=pl.BlockSpec((1, tp, c), lambda bi, i: (bi, i, 0)),
        out_shape=jax.ShapeDtypeStruct((b, t, c), F32),
        compiler_params=_cparams("parallel", "parallel"),
        name="pool",
    )(u3, u3, w_bd, s_pool)


def _compress_core(load_piece, tail_ref, pe_ref, w_ref, w2_ref, n):
    rowid = lax.broadcasted_iota(I32, (n, 1), 0)
    outs = []
    for c in range(2):
        w = w_ref[c]
        pe8 = jnp.broadcast_to(pe_ref[c], (8, 2 * CMP_STRIDE * HEAD_DIM)).astype(BF16)
        pe_first = _dot(pe8[:, :CMP_STRIDE * HEAD_DIM], w)[0:1, :CMP_HIDDEN]
        pe_second = _dot(pe8[:, CMP_STRIDE * HEAD_DIM:], w)[0:1, CMP_HIDDEN:]
        pe_const = pe_first + pe_second
        x2 = jnp.concatenate(
            [jnp.concatenate([load_piece(c, g, r) for r in range(CMP_STRIDE)], axis=1) for g in range(NSA_KV_HEADS)],
            axis=0).astype(BF16)
        z = _dot(x2, w)
        hs = []
        for g in range(NSA_KV_HEADS):
            tail8 = jnp.broadcast_to(tail_ref[0, 2 * c + g:2 * c + g + 1, :], (8, CMP_STRIDE * HEAD_DIM)).astype(BF16)
            second_tail = _dot(tail8, w)[0:1, CMP_HIDDEN:]
            first = z[g * n:(g + 1) * n, :CMP_HIDDEN]
            second = z[g * n:(g + 1) * n, CMP_HIDDEN:]
            shifted = pltpu.roll(second, n - 1, 0)
            h = first + jnp.where(rowid == n - 1, second_tail, shifted) + pe_const
            hs.append(jax.nn.gelu(h, approximate=True))
        o = _dot(jnp.concatenate(hs, axis=0).astype(BF16), w2_ref[c])
        outs += [o[:n], o[n:]]
    return jnp.concatenate(outs, axis=1)


def _compress_prompt_kernel(k_ref, v_ref, tail_ref, pe_ref, w_ref, w2_ref, out_ref, *, n):
    def piece(c, g, r):
        return (k_ref, v_ref)[c][0, pl.ds(r, n, stride=CMP_STRIDE), :][:, g * HEAD_DIM:(g + 1) * HEAD_DIM]

    out_ref[0] = _compress_core(piece, tail_ref, pe_ref, w_ref, w2_ref, n).astype(BF16)


def _compress_prompt_call(rows4_3d, tail, cw):
    b, t, _ = rows4_3d.shape
    n = t // CMP_STRIDE
    full = lambda a: pl.BlockSpec(a.shape, lambda bi: (0,) * a.ndim)
    return pl.pallas_call(
        functools.partial(_compress_prompt_kernel, n=n),
        grid=(b,),
        in_specs=[pl.BlockSpec((1, t, LANES), lambda bi: (bi, 0, 0)),
                  pl.BlockSpec((1, t, LANES), lambda bi: (bi, 0, 1)),
                  pl.BlockSpec((1,) + tail.shape[1:], lambda bi: (bi, 0, 0)),
                  full(cw["pe"]), full(cw["w"]), full(cw["w2"])],
        out_specs=pl.BlockSpec((1, n, 2 * LANES), lambda bi: (bi, 0, 0)),
        out_shape=jax.ShapeDtypeStruct((b, n, 2 * LANES), BF16),
        compiler_params=_cparams("parallel"),
        name="compress_prompt",
    )(rows4_3d, rows4_3d, tail, cw["pe"], cw["w"], cw["w2"])


def _compress_sample_kernel(pt_ref, cache_ref, tail_ref, pe_ref, w_ref, w2_ref, out_ref,
                            buf, rows, sem, *, n, n_pages, page):
    b = pl.program_id(0)
    nb = pl.num_programs(0)

    def page_copy(bb, slot, p, c):
        return pltpu.make_async_copy(cache_ref.at[pt_ref[bb * n_pages + p], c], buf.at[slot, c, p], sem.at[slot])

    def for_all(fn):
        for p in range(n_pages):
            for c in range(2):
                fn(p, c)

    def issue(bb, slot):
        for_all(lambda p, c: page_copy(bb, slot, p, c).start())

    @pl.when(b == 0)
    def _():
        issue(0, 0)

    @pl.when(b + 1 < nb)
    def _():
        issue(b + 1, (b + 1) % 2)

    slot = b % 2
    for_all(lambda p, c: page_copy(b, slot, p, c).wait())

    def to_rows(p, carry):
        for c in range(2):
            rows[c, pl.ds(pl.multiple_of(p * page, page), page), :] = buf[slot, c, p].T
        return carry

    lax.fori_loop(0, n_pages, to_rows, 0)
    piece = lambda c, g, r: rows[c, pl.ds(r, n, stride=CMP_STRIDE), :][:, g * HEAD_DIM:(g + 1) * HEAD_DIM]
    out_ref[0] = _compress_core(piece, tail_ref, pe_ref, w_ref, w2_ref, n).astype(BF16)


def _compress_sample_call(page_table, cache_t, tail, cw):
    b, n_pages = page_table.shape
    page = cache_t.shape[3]
    n = n_pages * page // CMP_STRIDE
    full = lambda a: pl.BlockSpec(a.shape, lambda bi, pt: (0,) * a.ndim)
    grid_spec = pltpu.PrefetchScalarGridSpec(
        num_scalar_prefetch=1,
        grid=(b,),
        in_specs=[pl.BlockSpec(memory_space=pl.ANY),
                  pl.BlockSpec((1,) + tail.shape[1:], lambda bi, pt: (bi, 0, 0)),
                  full(cw["pe"]), full(cw["w"]), full(cw["w2"])],
        out_specs=pl.BlockSpec((1, n, 2 * LANES), lambda bi, pt: (bi, 0, 0)),
        scratch_shapes=[pltpu.VMEM((2, 2, n_pages, LANES, page), F32), pltpu.VMEM((2, n_pages * page, LANES), F32),
                        pltpu.SemaphoreType.DMA((2,))],
    )
    return pl.pallas_call(
        functools.partial(_compress_sample_kernel, n=n, n_pages=n_pages, page=page),
        grid_spec=grid_spec,
        out_shape=jax.ShapeDtypeStruct((b, n, 2 * LANES), BF16),
        compiler_params=_cparams("arbitrary"),
        name="compress_sample",
    )(page_table.reshape(-1), cache_t, tail, cw["pe"], cw["w"], cw["w2"])


def _topk_mask(score, ids, n_top, axis):
    sel = jnp.zeros(score.shape, jnp.bool_)
    work = score
    firsts = []
    big = float(score.shape[axis])
    for _ in range(n_top):
        m = jnp.max(work, axis=axis, keepdims=True)
        first = jnp.min(jnp.where(work == m, ids, big), axis=axis, keepdims=True)
        pick = (ids == first) & (m > -jnp.inf)
        sel = sel | pick
        work = jnp.where(pick, -jnp.inf, work)
        firsts.append(jnp.where(m > -jnp.inf, first, -1.0))
    return sel, firsts


def _gate_and_pack(o_ref_store, gates, o_c, o_s, o_w, rows):
    lane = lax.broadcasted_iota(I32, (1, LANES), 1)
    heads = []
    for h in range(NSA_HEADS):
        g, r = divmod(h, NSA_REP)
        rs = slice(r * rows, (r + 1) * rows)
        gc, gs, gw = (gates[:, N_BRANCH * h + k:N_BRANCH * h + k + 1] for k in range(N_BRANCH))
        heads.append(gc * o_c[g][rs] + gs * o_s[g][rs] + gw * o_w[g][rs])
    for j in range(NSA_HEADS // 2):
        a, b = heads[2 * j], heads[2 * j + 1]
        if (2 * j) // NSA_REP == 0:
            chunk = jnp.where(lane < HEAD_DIM, a, pltpu.roll(b, HEAD_DIM, 1))
        else:
            chunk = jnp.where(lane < HEAD_DIM, pltpu.roll(a, HEAD_DIM, 1), b)
        o_ref_store(j, chunk)


def _nsa_prompt_kernel(q_ref, gate_ref, kc_ref, kv_ref, poolt_ref, e_ref, o_ref, selx_ref, acc_ref,
                       *, tq, tk, t_len):
    i = pl.program_id(1)
    s0 = i * tq
    rows4 = NSA_REP * tq
    tpos = s0 + lax.broadcasted_iota(I32, (tq, 1), 0)
    rep = lambda x: jnp.concatenate([x] * NSA_REP, axis=0)
    n_cmp = kc_ref.shape[1]
    kc128 = kc_ref[0, :, :LANES]
    vc128 = kc_ref[0, :, LANES:]
    kc_end = lax.broadcasted_iota(I32, (1, n_cmp), 1) * CMP_STRIDE + (CMP_BLOCK - 1)
    bias_c = rep(jnp.where(kc_end <= tpos, 0.0, NEG))
    any_c = rep((tpos >= CMP_BLOCK - 1).astype(F32))
    n_sel = t_len // SEL_BLOCK
    blk = lax.broadcasted_iota(I32, (n_sel, 1), 0)
    blk_f = blk.astype(F32)
    tq_lane = s0 + lax.broadcasted_iota(I32, (1, tq), 1)
    cur = tq_lane // SEL_BLOCK
    valid = blk * SEL_BLOCK <= tq_lane
    forced = (blk == 0) | (blk == cur) | (blk == cur - 1)
    w_start = pl.multiple_of(jnp.maximum(s0 - WINDOW, 0), LANES)
    w_len = WINDOW + tq
    kpos_w = w_start + lax.broadcasted_iota(I32, (1, w_len), 1)
    dist = tpos - kpos_w
    bias_w = rep(jnp.where((dist >= 0) & (dist < WINDOW), 0.0, NEG))
    n_ch = (s0 + tq + tk - 1) // tk

    o_c, o_s, o_w = [], [], []
    for g in range(NSA_KV_HEADS):
        qg = jnp.concatenate([q_ref[0, :, (g * NSA_REP + r) * LANES:(g * NSA_REP + r + 1) * LANES]
                              for r in range(NSA_REP)], axis=0)
        s_c = _dot_nt(qg, kc128) + bias_c
        e_c = jnp.exp(s_c - jnp.max(s_c, axis=-1, keepdims=True))
        p_c = e_c * (any_c / jnp.sum(e_c, axis=-1, keepdims=True))
        o_c.append(_dot(p_c.astype(BF16), vc128))
        imp = p_c[0:tq]
        for r in range(1, NSA_REP):
            imp = imp + p_c[r * tq:(r + 1) * tq]
        imp_t = sum(_dot_nt(poolt_ref[...], piece) for piece in _split3(imp))
        score = jnp.where(valid, imp_t + jnp.where(forced, FORCE_BONUS, 0.0), -jnp.inf)
        sel_t, _ = _topk_mask(score, blk_f, min(SEL_TOPK, n_sel), 0)
        sel_pad = jnp.concatenate([sel_t.astype(F32), jnp.zeros(((-n_sel) % LANES, tq), F32)], axis=0)
        sel = sel_pad.T[:, :n_sel].astype(BF16)
        for c in range(t_len // tk):
            kpos = c * tk + lax.broadcasted_iota(I32, (1, tk), 1)
            hit = (_dot(sel, e_ref[:, c * tk:(c + 1) * tk]) > 0.5) & (kpos <= tpos)
            selx_ref[c] = jnp.where(hit, 0.0, NEG)

        acc_ref[...] = jnp.zeros(acc_ref.shape, F32)

        def body(c, carry):
            m, l = carry
            k0 = pl.multiple_of(c * tk, tk)
            ks = kv_ref[0, pl.ds(k0, tk), 2 * LANES:3 * LANES]
            vs = kv_ref[0, pl.ds(k0, tk), 3 * LANES:4 * LANES]
            s = _dot_nt(qg, ks) + rep(selx_ref[c])
            m_new = jnp.maximum(m, jnp.max(s, axis=-1, keepdims=True))
            alpha = jnp.exp(m - m_new)
            p = jnp.exp(s - m_new)
            l_new = alpha * l + jnp.sum(p, axis=-1, keepdims=True)
            acc_ref[...] = alpha * acc_ref[...] + _dot(p.astype(BF16), vs)
            return m_new, l_new

        _, l_fin = lax.fori_loop(0, n_ch, body, (jnp.full((rows4, 1), NEG, F32), jnp.zeros((rows4, 1), F32)))
        o_s.append(acc_ref[...] * (1.0 / jnp.maximum(l_fin, 1e-30)))

        kw = kv_ref[0, pl.ds(w_start, w_len), 4 * LANES:5 * LANES]
        vw = kv_ref[0, pl.ds(w_start, w_len), 5 * LANES:6 * LANES]
        s_w = _dot_nt(qg, kw) + bias_w
        e_w = jnp.exp(s_w - jnp.max(s_w, axis=-1, keepdims=True))
        o_w.append(_dot(e_w.astype(BF16), vw) * (1.0 / jnp.sum(e_w, axis=-1, keepdims=True)))

    def store(j, chunk):
        o_ref[0, :, j * LANES:(j + 1) * LANES] = chunk

    _gate_and_pack(store, gate_ref[0], o_c, o_s, o_w, tq)


def _nsa_prompt_call(q3, gates3, kcvc, kvb3, poolt, e_mat, *, tq, tk):
    b, t, _ = q3.shape
    n_cmp = kcvc.shape[1]
    return pl.pallas_call(
        functools.partial(_nsa_prompt_kernel, tq=tq, tk=tk, t_len=t),
        grid=(b, t // tq),
        in_specs=[pl.BlockSpec((1, tq, Q_PAD), lambda bi, i: (bi, i, 0)),
                  pl.BlockSpec((1, tq, LANES), lambda bi, i: (bi, i, 0)),
                  pl.BlockSpec((1, n_cmp, 2 * LANES), lambda bi, i: (bi, 0, 0)),
                  pl.BlockSpec((1, t, NSA_KV_WIDTH), lambda bi, i: (bi, 0, 0)),
                  pl.BlockSpec(poolt.shape, lambda bi, i: (0, 0)),
                  pl.BlockSpec(e_mat.shape, lambda bi, i: (0, 0))],
        out_specs=pl.BlockSpec((1, tq, NSA_WIDTH), lambda bi, i: (bi, i, 0)),
        out_shape=jax.ShapeDtypeStruct((b, t, NSA_WIDTH), F32),
        scratch_shapes=[pltpu.VMEM((t // tk, tq, tk), F32), pltpu.VMEM((NSA_REP * tq, LANES), F32)],
        compiler_params=_cparams("parallel", "arbitrary"),
        name="nsa_prompt",
    )(q3, gates3, kcvc, kvb3, poolt, e_mat)


def _memkv_kernel(x_ref, g_ref, w_ref, gk_ref, seg_ref, o_ref, ob_ref):
    a = _rms(x_ref[...]) * g_ref[...]
    z = _dot(a.astype(BF16), w_ref[...])
    for c in range(MEM_WIDTH // LANES):
        sl = slice(c * LANES, (c + 1) * LANES)
        xc = z[:, sl]
        ssq = _dot((xc * xc).astype(BF16), seg_ref[...])
        kn = xc * lax.rsqrt(ssq * (1.0 / HEAD_DIM) + EPS) * gk_ref[:, sl]
        o_ref[:, sl] = kn
        ob_ref[:, sl] = kn.astype(BF16)
    o_ref[:, MEM_WIDTH:] = z[:, MEM_WIDTH:]
    ob_ref[:, MEM_WIDTH:] = z[:, MEM_WIDTH:].astype(BF16)


def _memkv_call(mem2d, g_mem, w_mem, gmk, seg, *, tm):
    n, dm = mem2d.shape
    full = lambda shape: pl.BlockSpec(shape, lambda i: (0,) * len(shape))
    return pl.pallas_call(
        _memkv_kernel,
        grid=(n // tm,),
        in_specs=[pl.BlockSpec((tm, dm), lambda i: (i, 0)), full((1, dm)), full((dm, 2 * MEM_WIDTH)),
                  full((1, MEM_WIDTH)), full((LANES, LANES))],
        out_specs=[pl.BlockSpec((tm, 2 * MEM_WIDTH), lambda i: (i, 0))] * 2,
        out_shape=[jax.ShapeDtypeStruct((n, 2 * MEM_WIDTH), F32), jax.ShapeDtypeStruct((n, 2 * MEM_WIDTH), BF16)],
        compiler_params=_cparams("parallel"),
        name="memkv",
    )(mem2d, g_mem, w_mem, gmk, seg)


def _mem_attend(qm, mkv, rows):
    lane = lax.broadcasted_iota(I32, (1, LANES), 1)
    chunks = []
    for j in range(MEM_HEADS // 2):
        k128 = mkv[:, j * LANES:(j + 1) * LANES]
        v128 = mkv[:, MEM_WIDTH + j * LANES:MEM_WIDTH + (j + 1) * LANES]
        q2 = jnp.concatenate([qm[2 * j], qm[2 * j + 1]], axis=0)
        s = _dot_nt(q2, k128)
        e = jnp.exp(s - jnp.max(s, axis=-1, keepdims=True))
        p = e / jnp.sum(e, axis=-1, keepdims=True)
        o = _dot(p.astype(BF16), v128)
        chunks.append(jnp.where(lane < HEAD_DIM, o[:rows], o[rows:2 * rows]))
    return chunks


def _memattn_kernel(qm_ref, mkv_ref, o_ref, *, tq):
    qm = [qm_ref[0, :, h * LANES:(h + 1) * LANES] for h in range(MEM_HEADS)]
    for j, chunk in enumerate(_mem_attend(qm, mkv_ref[0], tq)):
        o_ref[0, :, j * LANES:(j + 1) * LANES] = chunk


def _memattn_call(qm3, mkv3, *, tq):
    b, t, _ = qm3.shape
    m = mkv3.shape[1]
    return pl.pallas_call(
        functools.partial(_memattn_kernel, tq=tq),
        grid=(b, t // tq),
        in_specs=[pl.BlockSpec((1, tq, QM_PAD), lambda bi, i: (bi, i, 0)),
                  pl.BlockSpec((1, m, 2 * MEM_WIDTH), lambda bi, i: (bi, 0, 0))],
        out_specs=pl.BlockSpec((1, tq, MEM_WIDTH), lambda bi, i: (bi, i, 0)),
        out_shape=jax.ShapeDtypeStruct((b, t, MEM_WIDTH), F32),
        compiler_params=_cparams("parallel", "parallel"),
        name="memattn",
    )(qm3, mkv3)


def _pad_rows(rows_list):
    x = jnp.concatenate(rows_list, axis=0)
    return jnp.concatenate([x, jnp.zeros((8 - x.shape[0], x.shape[1]), x.dtype)], axis=0)


def _sample_attn1_kernel(q_ref, kc_ref, win_ref, mkv_ref, qm_ref, pool_ref,
                         oc_ref, ow_ref, om_ref, idx_ref, *, t_pos, n_win):
    q = q_ref[0].astype(F32)
    n_cmp = kc_ref.shape[1]
    kc128 = kc_ref[0, :, :LANES]
    vc128 = kc_ref[0, :, LANES:]
    kw128 = win_ref[0, :, :LANES].astype(BF16)
    vw128 = win_ref[0, :, LANES:].astype(BF16)
    kc_end = lax.broadcasted_iota(I32, (1, n_cmp), 1) * CMP_STRIDE + (CMP_BLOCK - 1)
    mask_c = kc_end <= t_pos
    kw_pos = t_pos - (n_win - 1) + lax.broadcasted_iota(I32, (1, n_win), 1)
    mask_w = (t_pos - kw_pos >= 0) & (t_pos - kw_pos < WINDOW) & (kw_pos >= 0)
    n_slot = pool_ref.shape[1]
    blk = lax.broadcasted_iota(I32, (1, n_slot), 1)
    cur = t_pos // SEL_BLOCK
    valid = blk * SEL_BLOCK <= t_pos
    forced = (blk == 0) | (blk == cur) | (blk == cur - 1)
    lane = lax.broadcasted_iota(I32, (1, LANES), 1)
    for g in range(NSA_KV_HEADS):
        qg = _pad_rows([q[:, (g * NSA_REP + r) * LANES:(g * NSA_REP + r + 1) * LANES]
                        for r in range(NSA_REP)]).astype(BF16)
        p_c = _masked_softmax(_dot_nt(qg, kc128), mask_c)
        oc_ref[0, g * NSA_REP:(g + 1) * NSA_REP, :] = _dot(p_c.astype(BF16), vc128)[:NSA_REP]
        imp = jnp.sum(p_c[:NSA_REP], axis=0, keepdims=True)
        imp8 = jnp.broadcast_to(imp, (8, n_cmp))
        imp_b = sum(_dot(piece, pool_ref[...]) for piece in _split3(imp8))[0:1]
        score = jnp.where(valid, imp_b + jnp.where(forced, FORCE_BONUS, 0.0), -jnp.inf)
        n_valid_blocks = t_pos // SEL_BLOCK + 1
        _, firsts = _topk_mask(score, blk.astype(F32), min(SEL_TOPK, n_valid_blocks), 1)
        idxv = jnp.full((1, LANES), -1, I32)
        for j, f in enumerate(firsts):
            idxv = jnp.where(lane == j, f.astype(I32), idxv)
        idx_ref[0, g:g + 1, :] = idxv
        p_w = _masked_softmax(_dot_nt(qg, kw128), mask_w)
        ow_ref[0, g * NSA_REP:(g + 1) * NSA_REP, :] = _dot(p_w.astype(BF16), vw128)[:NSA_REP]
    qm = qm_ref[0].astype(F32)
    qmh = []
    for h in range(MEM_HEADS):
        row = qm[:, h * LANES:(h + 1) * LANES]
        qmh.append(jnp.concatenate([row, jnp.zeros((7, LANES), F32)], axis=0).astype(BF16))
    for j, chunk in enumerate(_mem_attend(qmh, mkv_ref[0].astype(BF16), 8)):
        om_ref[0, :, j * LANES:(j + 1) * LANES] = chunk[0:1]


def _sample_attn1_call(q3, kcvc, win3, mkv3, qm3, pool_mat, *, t_pos):
    b = q3.shape[0]
    n_cmp = kcvc.shape[1]
    n_win = win3.shape[1]
    m = mkv3.shape[1]
    per_b = lambda shape: pl.BlockSpec((1,) + shape, lambda bi: (bi, 0, 0))
    return pl.pallas_call(
        functools.partial(_sample_attn1_kernel, t_pos=t_pos, n_win=n_win),
        grid=(b,),
        in_specs=[per_b((1, Q_PAD)), per_b((n_cmp, 2 * LANES)), per_b((n_win, 2 * LANES)),
                  per_b((m, 2 * MEM_WIDTH)), per_b((1, QM_PAD)),
                  pl.BlockSpec(pool_mat.shape, lambda bi: (0, 0))],
        out_specs=[per_b((NSA_HEADS, LANES)), per_b((NSA_HEADS, LANES)), per_b((1, MEM_WIDTH)),
                   per_b((NSA_KV_HEADS, LANES))],
        out_shape=[jax.ShapeDtypeStruct((b, NSA_HEADS, LANES), F32), jax.ShapeDtypeStruct((b, NSA_HEADS, LANES), F32),
                   jax.ShapeDtypeStruct((b, 1, MEM_WIDTH), F32), jax.ShapeDtypeStruct((b, NSA_KV_HEADS, LANES), I32)],
        compiler_params=_cparams("parallel"),
        name="sample_attn1",
    )(q3, kcvc, win3, mkv3, qm3, pool_mat)


def _sample_sel_kernel(pt_ref, idx_ref, cache_ref, q_ref, knew_ref, gate_ref, oc_ref, ow_ref, o_ref,
                       buf, sem, *, t_pos, n_pages, n_top, page):
    b = pl.program_id(0)
    nb = pl.num_programs(0)
    blk_per_page = page // SEL_BLOCK
    n_past_blk = n_pages * blk_per_page
    idx_stride = NSA_KV_HEADS * LANES

    def blk_at(bb, g, j):
        return idx_ref[bb * idx_stride + g * LANES + j]

    def blk_copy(bb, slot, g, j, kv):
        blkc = jnp.clip(blk_at(bb, g, j), 0, n_past_blk - 1)
        pg = pt_ref[bb * n_pages + blkc // blk_per_page]
        return pltpu.make_async_copy(cache_ref.at[pg, 2 + kv], buf.at[slot, g, kv, j], sem.at[slot])

    def for_all(fn):
        for g in range(NSA_KV_HEADS):
            for j in range(n_top):
                for kv in range(2):
                    fn(g, j, kv)

    def issue(bb, slot):
        for_all(lambda g, j, kv: blk_copy(bb, slot, g, j, kv).start())

    @pl.when(b == 0)
    def _():
        issue(0, 0)

    @pl.when(b + 1 < nb)
    def _():
        issue(b + 1, (b + 1) % 2)

    slot = b % 2
    for_all(lambda g, j, kv: blk_copy(b, slot, g, j, kv).wait())

    q = q_ref[0].astype(F32)
    knew = knew_ref[0]
    k_new = knew[:, :LANES].astype(BF16).astype(F32)
    v_new = knew[:, LANES:].astype(BF16).astype(F32)
    n_keys = n_top * page
    key_lane = lax.broadcasted_iota(I32, (1, n_keys), 1)
    key_slot = key_lane // page
    key_row = key_lane % page
    cur_blk = t_pos // SEL_BLOCK
    o_s = []
    for g in range(NSA_KV_HEADS):
        qg = _pad_rows([q[:, (g * NSA_REP + r) * LANES:(g * NSA_REP + r + 1) * LANES] for r in range(NSA_REP)])
        blkvec = jnp.full((1, n_keys), -1, I32)
        has_cur = jnp.zeros((1, 1), jnp.bool_)
        for j in range(n_top):
            bj = blk_at(b, g, j)
            blkvec = jnp.where(key_slot == j, bj, blkvec)
            has_cur = has_cur | (bj == cur_blk)
        in_blk = key_row // SEL_BLOCK == blkvec % blk_per_page
        key_pos = (blkvec // blk_per_page) * page + key_row
        vis = (blkvec >= 0) & (blkvec < n_past_blk) & in_blk & (key_pos <= t_pos)
        kt = jnp.concatenate([buf[slot, g, 0, j] for j in range(n_top)], axis=1).astype(BF16)
        vt = jnp.concatenate([buf[slot, g, 1, j] for j in range(n_top)], axis=1).astype(BF16)
        s_past = jnp.where(vis, _dot(qg.astype(BF16), kt), NEG)
        s_new = jnp.where(has_cur, jnp.sum(qg * k_new, axis=-1, keepdims=True), NEG)
        m = jnp.maximum(jnp.max(s_past, axis=-1, keepdims=True), s_new)
        e_p = jnp.where(vis, jnp.exp(s_past - m), 0.0)
        e_n = jnp.where(has_cur, jnp.exp(s_new - m), 0.0)
        den = jnp.maximum(jnp.sum(e_p, axis=-1, keepdims=True) + e_n, 1e-30)
        num = _dot_nt(e_p.astype(BF16), vt) + e_n.astype(BF16).astype(F32) * v_new
        o_s.append(num / den)
    o_c = [jnp.concatenate([oc_ref[0, g * NSA_REP:(g + 1) * NSA_REP, :]] * 2, axis=0) for g in range(NSA_KV_HEADS)]
    o_w = [jnp.concatenate([ow_ref[0, g * NSA_REP:(g + 1) * NSA_REP, :]] * 2, axis=0) for g in range(NSA_KV_HEADS)]

    def store(j, chunk):
        o_ref[0, :, j * LANES:(j + 1) * LANES] = chunk

    _gate_and_pack(store, gate_ref[0], o_c, o_s, o_w, 1)


def _sample_sel_call(page_table, idx, cache_t, q3, knew3, gates3, oc, ow, *, t_pos, n_top):
    b, n_pages = page_table.shape
    page = cache_t.shape[3]
    per_b = lambda shape: pl.BlockSpec((1,) + shape, lambda bi, pt, ix: (bi, 0, 0))
    grid_spec = pltpu.PrefetchScalarGridSpec(
        num_scalar_prefetch=2,
        grid=(b,),
        in_specs=[pl.BlockSpec(memory_space=pl.ANY), per_b((1, Q_PAD)), per_b((1, 2 * LANES)), per_b((1, LANES)),
                  per_b((NSA_HEADS, LANES)), per_b((NSA_HEADS, LANES))],
        out_specs=per_b((1, NSA_WIDTH)),
        scratch_shapes=[pltpu.VMEM((2, NSA_KV_HEADS, 2, n_top, LANES, page), F32), pltpu.SemaphoreType.DMA((2,))],
    )
    return pl.pallas_call(
        functools.partial(_sample_sel_kernel, t_pos=t_pos, n_pages=n_pages, n_top=n_top, page=page),
        grid_spec=grid_spec,
        out_shape=jax.ShapeDtypeStruct((b, 1, NSA_WIDTH), F32),
        compiler_params=_cparams("arbitrary"),
        name="sample_sel",
    )(page_table.reshape(-1), idx.reshape(-1), cache_t, q3, knew3, gates3, oc, ow)


def _finish_kernel(yp_ref, on_ref, om_ref, x_ref, gm_ref, wo_ref, gf_ref, wr_ref, br_ref, tri_ref,
                   h_ref, hn_ref, route_ref, counts_ref, cnt_ref):
    gm = gm_ref[...]
    o1 = POOL_WIDTH
    o2 = o1 + NSA_WIDTH
    mixed = jnp.concatenate([_rms(yp_ref[...]) * gm[:, :o1], _rms(on_ref[...]) * gm[:, o1:o2],
                             _rms(om_ref[...]) * gm[:, o2:]], axis=-1)
    h = x_ref[...] + _dot(mixed.astype(BF16), wo_ref[...])
    h_ref[...] = h
    hn = _rms(h) * gf_ref[...]
    hn_ref[...] = hn
    logits = _dot(hn.astype(BF16), wr_ref[...]) + br_ref[...]
    lane = lax.broadcasted_iota(I32, (1, LANES), 1)
    lane_f = lane.astype(F32)
    is1 = lane < N_EXPERT_GROUPS
    m1 = jnp.max(jnp.where(is1, logits, -jnp.inf), axis=-1, keepdims=True)
    e1 = jnp.where(is1, jnp.exp(logits - m1), 0.0)
    p1 = e1 / jnp.sum(e1, axis=-1, keepdims=True)
    top1_p = jnp.max(p1, axis=-1, keepdims=True)
    grp = jnp.min(jnp.where((p1 == top1_p) & is1, lane_f, float(LANES)), axis=-1, keepdims=True)
    base = N_EXPERT_GROUPS + grp * EXPERTS_PER_GROUP
    in_g = (lane_f >= base) & (lane_f < base + EXPERTS_PER_GROUP)
    l2 = jnp.where(in_g, logits, -jnp.inf)
    v0 = jnp.max(l2, axis=-1, keepdims=True)
    i0 = jnp.min(jnp.where(l2 == v0, lane_f, float(LANES)), axis=-1, keepdims=True)
    l2b = jnp.where(lane_f == i0, -jnp.inf, l2)
    v1 = jnp.max(l2b, axis=-1, keepdims=True)
    i1 = jnp.min(jnp.where(l2b == v1, lane_f, float(LANES)), axis=-1, keepdims=True)
    ex = jnp.exp(v1 - v0)
    w0 = top1_p / (1.0 + ex)
    w1 = top1_p * ex / (1.0 + ex)
    ex0 = i0 - N_EXPERT_GROUPS
    ex1 = i1 - N_EXPERT_GROUPS
    is0 = lane_f == ex0
    is1e = lane_f == ex1
    oh0 = jnp.where(is0, 1.0, 0.0)
    oh1 = jnp.where(is1e, 1.0, 0.0)
    before0 = _dot(tri_ref[...], oh0.astype(BF16))
    before1 = _dot(tri_ref[...], oh1.astype(BF16))
    tot0 = jnp.sum(oh0, axis=0, keepdims=True)
    tot1 = jnp.sum(oh1, axis=0, keepdims=True)

    @pl.when(pl.program_id(0) == 0)
    def _():
        cnt_ref[...] = jnp.zeros(cnt_ref.shape, F32)

    seen = cnt_ref[...]
    rank0 = jnp.sum(jnp.where(is0, before0 + seen, 0.0), axis=-1, keepdims=True)
    rank1 = jnp.sum(jnp.where(is1e, before1 + seen + tot0, 0.0), axis=-1, keepdims=True)
    cnt_ref[...] = seen + tot0 + tot1
    counts_ref[...] = seen + tot0 + tot1
    route = jnp.where(lane == 0, ex0, jnp.where(lane == 1, ex1, jnp.where(lane == 2, w0, jnp.where(lane == 3, w1,
            jnp.where(lane == 4, rank0, jnp.where(lane == 5, rank1, 0.0))))))
    route_ref[...] = route


def _finish_call(yp, on, om, x2d, fw, *, tm):
    n, dm = x2d.shape
    full = lambda shape: pl.BlockSpec(shape, lambda i: (0,) * len(shape))
    rows = lambda w: pl.BlockSpec((tm, w), lambda i: (i, 0))
    tri = jnp.asarray(np.arange(tm)[None, :] < np.arange(tm)[:, None], BF16)
    return pl.pallas_call(
        _finish_kernel,
        grid=(n // tm,),
        in_specs=[rows(POOL_WIDTH), rows(NSA_WIDTH), rows(MEM_WIDTH), rows(dm), full((1, dm)), full((dm, dm)),
                  full((1, dm)), full((dm, LANES)), full((1, LANES)), full((tm, tm))],
        out_specs=[rows(dm), rows(dm), rows(LANES), full((1, LANES))],
        out_shape=[jax.ShapeDtypeStruct((n, dm), F32), jax.ShapeDtypeStruct((n, dm), F32),
                   jax.ShapeDtypeStruct((n, LANES), F32), jax.ShapeDtypeStruct((1, LANES), F32)],
        scratch_shapes=[pltpu.VMEM((1, LANES), F32)],
        compiler_params=_cparams("arbitrary"),
        name="finish",
    )(yp, on, om, x2d, fw["g_mix"], fw["w_out"], fw["g_ffn"], fw["w_r"], fw["b_r"], tri)


def _route_tables(route, counts, tm):
    n = route.shape[0]
    eid = route[:, 0:2].astype(I32)
    rank = route[:, 4:6].astype(I32)
    cnt = counts[0, :N_EXPERTS].astype(I32)
    tiles_per = (cnt + tm - 1) // tm
    tile_end = jnp.cumsum(tiles_per)
    tile_start = tile_end - tiles_per
    experts = jnp.arange(N_EXPERTS, dtype=I32)
    start_of = jnp.sum(jnp.where(eid[:, :, None] == experts, tile_start, 0), axis=-1)
    pos = (start_of * tm + rank).reshape(-1)
    n_tiles = (2 * n) // tm + N_EXPERTS
    tj = jnp.arange(n_tiles, dtype=I32)
    tile_expert = jnp.minimum(jnp.sum((tj[:, None] >= tile_end[None, :]).astype(I32), axis=1), N_EXPERTS - 1)
    sel = tile_expert[:, None] == experts
    rows_left = jnp.sum(jnp.where(sel, cnt, 0), axis=-1) - (tj - jnp.sum(jnp.where(sel, tile_start, 0), axis=-1)) * tm
    tile_nvalid = jnp.where(tj < tile_end[-1], jnp.clip(rows_left, 0, tm), 0).astype(I32)
    return tile_expert, tile_nvalid, pos


def _row_wait_all(src_row, dst_row, sem, count):
    for _ in range(count):
        pltpu.make_async_copy(src_row, dst_row, sem).wait()


def _dispatch_kernel(pos_ref, nv_ref, hn_ref, xs_ref, stage, zbuf, sem, zsem, *, tm, tme, n_tiles):
    i = pl.program_id(0)
    nt = pl.num_programs(0)
    slot = i % 2
    wait_slot = lambda s: _row_wait_all(stage.at[s, pl.ds(0, 1), :], xs_ref.at[pl.ds(0, 1), :], sem.at[s], 2 * tm)

    @pl.when(i == 0)
    def _():
        zbuf[...] = jnp.zeros(zbuf.shape, F32)

        def fill(j, c):
            @pl.when(nv_ref[j] < tme)
            def _():
                pltpu.make_async_copy(zbuf, xs_ref.at[pl.ds(pl.multiple_of(j * tme, tme), tme), :], zsem).start()
            return c

        def drain(j, c):
            @pl.when(nv_ref[j] < tme)
            def _():
                pltpu.make_async_copy(zbuf, xs_ref.at[pl.ds(0, tme), :], zsem).wait()
            return c

        lax.fori_loop(0, n_tiles, fill, 0)
        lax.fori_loop(0, n_tiles, drain, 0)

    @pl.when(i >= 2)
    def _():
        wait_slot(slot)

    stage[slot] = hn_ref[...]
    base = i * (2 * tm)
    for r in range(tm):
        for k in range(2):
            dst = pos_ref[base + 2 * r + k]
            pltpu.make_async_copy(stage.at[slot, pl.ds(r, 1), :], xs_ref.at[pl.ds(dst, 1), :], sem.at[slot]).start()

    @pl.when(i == nt - 1)
    def _():
        wait_slot(slot)

        @pl.when(i >= 1)
        def _():
            wait_slot(1 - slot)


def _dispatch_call(pos, tile_nvalid, hn, *, tm, tme):
    n, dm = hn.shape
    n_tiles = tile_nvalid.shape[0]
    grid_spec = pltpu.PrefetchScalarGridSpec(
        num_scalar_prefetch=2,
        grid=(n // tm,),
        in_specs=[pl.BlockSpec((tm, dm), lambda i, pos, nv: (i, 0))],
        out_specs=pl.BlockSpec(memory_space=pl.ANY),
        scratch_shapes=[pltpu.VMEM((2, tm, dm), F32), pltpu.VMEM((tme, dm), F32),
                        pltpu.SemaphoreType.DMA((2,)), pltpu.SemaphoreType.DMA(())],
    )
    return pl.pallas_call(
        functools.partial(_dispatch_kernel, tm=tm, tme=tme, n_tiles=n_tiles),
        grid_spec=grid_spec,
        out_shape=jax.ShapeDtypeStruct((n_tiles * tme, dm), F32),
        compiler_params=_cparams("arbitrary"),
        name="moe_dispatch",
    )(pos, tile_nvalid, hn)


def _expert_kernel(te_ref, nv_ref, x_ref, wg_ref, wu_ref, wd_ref, y_ref, wgb, wub, wdb, *, tm):
    i = pl.program_id(0)
    prev = te_ref[jnp.maximum(i - 1, 0)]

    @pl.when((i == 0) | (te_ref[i] != prev))
    def _():
        wgb[...] = wg_ref[0].astype(BF16)
        wub[...] = wu_ref[0].astype(BF16)
        wdb[...] = wd_ref[0].astype(BF16)

    nv = nv_ref[i]

    @pl.when(nv == 0)
    def _():
        y_ref[...] = jnp.zeros(y_ref.shape, F32)

    @pl.when(nv > 0)
    def _():
        x = x_ref[...].astype(BF16)
        hg = _dot(x, wgb[...])
        hu = _dot(x, wub[...])
        hmid = hg * jax.nn.sigmoid(hg) * hu
        y_ref[...] = _dot(hmid.astype(BF16), wdb[...])


def _expert_call(tile_expert, tile_nvalid, xs, wg, wu, wd, *, tm):
    n_tiles = tile_expert.shape[0]
    dm = xs.shape[1]
    ff = wg.shape[2]
    grid_spec = pltpu.PrefetchScalarGridSpec(
        num_scalar_prefetch=2,
        grid=(n_tiles,),
        in_specs=[pl.BlockSpec((tm, dm), lambda i, te, nv: (i, 0)),
                  pl.BlockSpec((1, dm, ff), lambda i, te, nv: (te[i], 0, 0)),
                  pl.BlockSpec((1, dm, ff), lambda i, te, nv: (te[i], 0, 0)),
                  pl.BlockSpec((1, ff, dm), lambda i, te, nv: (te[i], 0, 0))],
        out_specs=pl.BlockSpec((tm, dm), lambda i, te, nv: (i, 0)),
        scratch_shapes=[pltpu.VMEM((dm, ff), BF16), pltpu.VMEM((dm, ff), BF16), pltpu.VMEM((ff, dm), BF16)],
    )
    return pl.pallas_call(
        functools.partial(_expert_kernel, tm=tm),
        grid_spec=grid_spec,
        out_shape=jax.ShapeDtypeStruct(xs.shape, F32),
        compiler_params=_cparams("arbitrary"),
        name="moe_experts",
    )(tile_expert, tile_nvalid, xs, wg, wu, wd)


def _combine_kernel(pos_ref, ys_ref, h_ref, route_ref, o_ref, gbuf, sem, *, tm):
    i = pl.program_id(0)
    nt = pl.num_programs(0)
    slot = i % 2

    def issue(step, s):
        base = step * (2 * tm)
        for r in range(tm):
            for k in range(2):
                src = pos_ref[base + 2 * r + k]
                pltpu.make_async_copy(ys_ref.at[pl.ds(src, 1), :], gbuf.at[s, k, pl.ds(r, 1), :], sem.at[s]).start()

    @pl.when(i == 0)
    def _():
        issue(0, 0)

    @pl.when(i + 1 < nt)
    def _():
        issue(i + 1, 1 - slot)

    _row_wait_all(ys_ref.at[pl.ds(0, 1), :], gbuf.at[slot, 0, pl.ds(0, 1), :], sem.at[slot], 2 * tm)
    w0 = route_ref[:, 2:3]
    w1 = route_ref[:, 3:4]
    o_ref[...] = h_ref[...] + (w0 * gbuf[slot, 0] + w1 * gbuf[slot, 1])


def _combine_call(pos, ys, h, route, *, tm):
    n, dm = h.shape
    grid_spec = pltpu.PrefetchScalarGridSpec(
        num_scalar_prefetch=1,
        grid=(n // tm,),
        in_specs=[pl.BlockSpec(memory_space=pl.ANY),
                  pl.BlockSpec((tm, dm), lambda i, pos: (i, 0)),
                  pl.BlockSpec((tm, LANES), lambda i, pos: (i, 0))],
        out_specs=pl.BlockSpec((tm, dm), lambda i, pos: (i, 0)),
        scratch_shapes=[pltpu.VMEM((2, 2, tm, dm), F32), pltpu.SemaphoreType.DMA((2,))],
    )
    return pl.pallas_call(
        functools.partial(_combine_kernel, tm=tm),
        grid_spec=grid_spec,
        out_shape=jax.ShapeDtypeStruct((n, dm), F32),
        compiler_params=_cparams("arbitrary"),
        name="moe_combine",
    )(pos, ys, h, route)


def _moe_sample_kernel(hn_ref, route_ref, h_ref, wg_ref, wu_ref, wd_ref, o_ref, acc_ref):
    e = pl.program_id(0)

    @pl.when(e == 0)
    def _():
        acc_ref[...] = jnp.zeros(acc_ref.shape, F32)

    route = route_ref[...]
    ef = e.astype(F32)
    comb = jnp.where(route[:, 0:1] == ef, route[:, 2:3], 0.0) + jnp.where(route[:, 1:2] == ef, route[:, 3:4], 0.0)
    x = hn_ref[...].astype(BF16)
    hg = _dot(x, wg_ref[0].astype(BF16))
    hu = _dot(x, wu_ref[0].astype(BF16))
    hmid = hg * jax.nn.sigmoid(hg) * hu * comb
    acc_ref[...] += _dot(hmid.astype(BF16), wd_ref[0].astype(BF16))

    @pl.when(e == pl.num_programs(0) - 1)
    def _():
        o_ref[...] = h_ref[...] + acc_ref[...]


def _moe_sample_call(hn, route, h, wg, wu, wd):
    n, dm = hn.shape
    ff = wg.shape[2]
    full = lambda shape: pl.BlockSpec(shape, lambda e: (0,) * len(shape))
    return pl.pallas_call(
        _moe_sample_kernel,
        grid=(N_EXPERTS,),
        in_specs=[full((n, dm)), full((n, LANES)), full((n, dm)),
                  pl.BlockSpec((1, dm, ff), lambda e: (e, 0, 0)), pl.BlockSpec((1, dm, ff), lambda e: (e, 0, 0)),
                  pl.BlockSpec((1, ff, dm), lambda e: (e, 0, 0))],
        out_specs=full((n, dm)),
        out_shape=jax.ShapeDtypeStruct((n, dm), F32),
        scratch_shapes=[pltpu.VMEM((n, dm), F32)],
        compiler_params=_cparams("arbitrary"),
        name="moe_sample",
    )(hn, route, h, wg, wu, wd)


def _prep_weights(l, g_attn, w_in, g_q, g_k, pe_cmp, w_cmp1, w_cmp2, w_pool, s_pool, g_mem, w_mem_kv, g_mq, g_mk,
                  g_mix, w_out, g_ffn, w_router1, b_router1, w_router2, b_router2):
    dm = w_in.shape[1]
    w = w_in[l]
    o1 = POOL_WIDTH
    o2 = o1 + NSA_WIDTH
    o3 = o2 + NSA_KV_WIDTH
    o4 = o3 + GATE_WIDTH
    wq = w[:, o1:o2].reshape(dm, NSA_HEADS, 1, HEAD_DIM)
    q_slot = jnp.asarray(np.eye(NSA_KV_HEADS, dtype=np.float32)[np.arange(NSA_HEADS) // NSA_REP])
    wq_pad = (wq * q_slot[None, :, :, None]).reshape(dm, Q_PAD)
    wqm = w[:, o4:].reshape(dm, MEM_HEADS, 1, HEAD_DIM)
    m_slot = jnp.asarray(np.eye(2, dtype=np.float32)[np.arange(MEM_HEADS) % 2])
    wqm_pad = (wqm * m_slot[None, :, :, None]).reshape(dm, QM_PAD)
    wg_pad = jnp.pad(w[:, o3:o4], ((0, 0), (0, LANES - GATE_WIDTH)))
    w_packed = jnp.concatenate([w[:, :o1], wq_pad, w[:, o2:o3], wqm_pad, wg_pad], axis=1).astype(BF16)
    half = ROPE_DIM // 2
    inv = jnp.power(ROPE_THETA, -jnp.arange(half, dtype=F32) * 2.0 / ROPE_DIM)
    d = np.arange(LANES) % HEAD_DIM
    inv_lane = jnp.where(jnp.asarray(d < ROPE_DIM), inv[jnp.asarray(d % half)], 0.0).reshape(1, LANES)
    seg = jnp.asarray((np.arange(LANES)[:, None] // HEAD_DIM == np.arange(LANES)[None, :] // HEAD_DIM), BF16)
    gk = jnp.concatenate([jnp.tile(g_k[l, br], 2) for br in range(N_BRANCH)]).reshape(1, N_BRANCH * LANES)
    proj = dict(g_attn=g_attn[l].reshape(1, dm), w_in=w_packed, gq=jnp.tile(g_q[l], Q_PAD // HEAD_DIM).reshape(1, Q_PAD),
                gk=gk, gmq=jnp.tile(g_mq[l], QM_PAD // HEAD_DIM).reshape(1, QM_PAD), inv=inv_lane, seg=seg)

    n_grp = len(POOL_WINDOWS)
    pg = POOL_WIDTH // n_grp
    w_bd = (jnp.asarray(np.eye(n_grp, dtype=np.float32))[:, None, :, None] * w_pool[l][:, :, None, :]
            ).reshape(POOL_WIDTH, POOL_WIDTH).astype(BF16)
    pool = dict(w=w_bd, s=s_pool[l].reshape(1, POOL_WIDTH))

    half_rows = CMP_STRIDE * HEAD_DIM
    w1 = jnp.concatenate([w_cmp1[l][:, :half_rows], w_cmp1[l][:, half_rows:]], axis=2).astype(BF16)
    cmp_w = dict(w=w1, pe=pe_cmp[l].reshape(2, 1, CMP_BLOCK * HEAD_DIM), w2=w_cmp2[l].astype(BF16))

    mem = dict(g=g_mem[l].reshape(1, dm), w=w_mem_kv[l].astype(BF16),
               gk=jnp.tile(g_mk[l], MEM_HEADS).reshape(1, MEM_WIDTH))
    w_r = jnp.concatenate([w_router1[l], w_router2[l].reshape(dm, N_EXPERTS)], axis=1)
    w_r = jnp.pad(w_r, ((0, 0), (0, LANES - w_r.shape[1]))).astype(BF16)
    b_r = jnp.concatenate([b_router1[l], b_router2[l].reshape(-1)])
    b_r = jnp.pad(b_r, (0, LANES - b_r.shape[0])).reshape(1, LANES)
    fin = dict(g_mix=g_mix[l].reshape(1, dm), w_out=w_out[l].astype(BF16), g_ffn=g_ffn[l].reshape(1, dm),
               w_r=w_r, b_r=b_r)
    return proj, pool, cmp_w, mem, fin


def _pick(n, prefs):
    for p in prefs:
        if n % p == 0:
            return p
    return n


def kernel(x_prompt, x_sample, cache_kv, cache_win, state_pool, cache_mem_kv, page_table, mem_prompt, g_attn, w_in, g_q, g_k, pe_cmp, w_cmp1, w_cmp2, w_pool, s_pool, g_mem, w_mem_kv, g_mq, g_mk, g_mix, w_out, g_ffn, w_router1, b_router1, w_router2, b_router2, w_gate, w_up, w_down):
    depth = w_in.shape[0]
    bp, t, dm = x_prompt.shape
    bs, ts, _ = x_sample.shape
    n_pages = page_table.shape[1]
    page = cache_kv.shape[2]
    past_len = n_pages * page
    n_win = cache_win.shape[2]
    assert ts == 1 and n_win == WINDOW and past_len % SEL_BLOCK == 0 and t % WINDOW == 0 and t >= WINDOW + 128
    ff = w_gate.shape[-1]
    tq, tk = 128, 512
    n_sel = t // SEL_BLOCK
    assert min(SEL_TOPK, n_sel) >= 3
    poolt =jnp.asarray(np.arange(t // CMP_STRIDE)[None, :] // CMP_PER_SEL == np.arange(n_sel)[:, None], BF16)
    e_mat = jnp.asarray(np.arange(t)[None, :] // SEL_BLOCK == np.arange(n_sel)[:, None], BF16)
    n_cmp_s = past_len // CMP_STRIDE
    n_slot_s = -(-(past_len // SEL_BLOCK + 1) // LANES) * LANES
    pool_s = jnp.asarray(np.arange(n_cmp_s)[:, None] // CMP_PER_SEL == np.arange(n_slot_s)[None, :], BF16)

    hp, hs = x_prompt, x_sample
    outs = [[] for _ in range(7)]
    for l in range(depth):
        proj_w, pool_w, cmp_w, mem_w, fin_w = _prep_weights(
            l, g_attn, w_in, g_q, g_k, pe_cmp, w_cmp1, w_cmp2, w_pool, s_pool, g_mem, w_mem_kv, g_mq, g_mk,
            g_mix, w_out, g_ffn, w_router1, b_router1, w_router2, b_router2)
        wg = w_gate[l].reshape(N_EXPERTS, dm, ff)
        wu = w_up[l].reshape(N_EXPERTS, dm, ff)
        wd = w_down[l].reshape(N_EXPERTS, ff, dm)
        kw_cols = CMP_STRIDE * 2 * LANES

        n = bp * t
        xp2 = hp.reshape(n, dm)
        u, qb, rows4, rowsw, kvb, gates, qmb, rows4t = _proj_call(
            xp2, proj_w, seq_len=t, pos0=0, tm=_pick(t, (512, 256, 128)), feature_major_rows4=True)
        u3 = u.reshape(bp, t, POOL_WIDTH)
        y_pool = _pool_call(u3, pool_w["w"], pool_w["s"], tp=_pick(t, (512, 256, 128)), pos0=0)
        n_slab = 2 * NSA_KV_HEADS
        chunk_w = CMP_STRIDE * HEAD_DIM
        kcvc = _compress_prompt_call(rows4.reshape(bp, t, 4 * LANES), jnp.zeros((bp, n_slab, chunk_w), F32), cmp_w)
        o_nsa = _nsa_prompt_call(qb.reshape(bp, t, Q_PAD), gates.reshape(bp, t, LANES), kcvc,
                                 kvb.reshape(bp, t, NSA_KV_WIDTH), poolt, e_mat, tq=tq, tk=tk)
        m_len = mem_prompt.shape[1]
        mkv, mkvb = _memkv_call(mem_prompt.reshape(bp * m_len, dm), mem_w["g"], mem_w["w"], mem_w["gk"],
                                proj_w["seg"], tm=_pick(bp * m_len, (256, 128)))
        o_mem = _memattn_call(qmb.reshape(bp, t, QM_PAD), mkvb.reshape(bp, m_len, 2 * MEM_WIDTH),
                              tq=_pick(t, (512, 256, 128)))
        tmf = _pick(n, (512, 256, 128))
        h_p, hn_p, route_p, counts_p = _finish_call(y_pool.reshape(n, POOL_WIDTH), o_nsa.reshape(n, NSA_WIDTH),
                                                    o_mem.reshape(n, MEM_WIDTH), xp2, fin_w, tm=tmf)
        tme = 256
        tmd = _pick(n, (256, 128))
        te, nv, pos = _route_tables(route_p, counts_p, tme)
        xs = _dispatch_call(pos, nv, hn_p, tm=tmd, tme=tme)
        ys = _expert_call(te, nv, xs, wg, wu, wd, tm=tme)
        y_p = _combine_call(pos, ys, h_p, route_p, tm=tmd).reshape(bp, t, dm)

        xs2 = hs.reshape(bs, dm)
        u_s, qb_s, rows4_s, rowsw_s, _, gates_s, qmb_s = _proj_call(xs2, proj_w, seq_len=1, pos0=past_len, tm=bs)
        ext = jnp.concatenate([state_pool[l], u_s[:, None, :]], axis=1)
        y_pool_s = _pool_call(ext, pool_w["w"], pool_w["s"], tp=POOL_STATE + 1, pos0=past_len - POOL_STATE)[:, -1, :]
        cache_t = jnp.transpose(cache_kv[l], (0, 2, 3, 4, 1)).reshape(
            cache_kv.shape[1], N_KV_SLOTS, NSA_KV_HEADS * HEAD_DIM, page)
        tail_s = jnp.pad(rows4_s[:, :2 * LANES].reshape(bs, n_slab, HEAD_DIM), ((0, 0), (0, 0), (0, chunk_w - HEAD_DIM)))
        kcvc_s = _compress_sample_call(page_table, cache_t, tail_s, cmp_w)
        win_s = jnp.concatenate([cache_win[l].reshape(bs, n_win, 2 * LANES)[:, 1:], rowsw_s[:, None, :]], axis=1)
        mkv_s = cache_mem_kv[l].reshape(bs, cache_mem_kv.shape[2], 2 * MEM_WIDTH)
        oc, ow, om_s, idx = _sample_attn1_call(qb_s.reshape(bs, 1, Q_PAD), kcvc_s, win_s, mkv_s,
                                               qmb_s.reshape(bs, 1, QM_PAD), pool_s, t_pos=past_len)
        n_top = min(SEL_TOPK, past_len // SEL_BLOCK + 1)
        o_nsa_s = _sample_sel_call(page_table, idx, cache_t, qb_s.reshape(bs, 1, Q_PAD),
                                   rows4_s[:, 2 * LANES:].reshape(bs, 1, 2 * LANES), gates_s.reshape(bs, 1, LANES),
                                   oc, ow, t_pos=past_len, n_top=n_top)
        h_s, hn_s, route_s, _ = _finish_call(y_pool_s, o_nsa_s.reshape(bs, NSA_WIDTH), om_s.reshape(bs, MEM_WIDTH),
                                          xs2, fin_w, tm=bs)
        y_s = _moe_sample_call(hn_s, route_s, h_s, wg, wu, wd).reshape(bs, 1, dm)

        keep = min(WINDOW, t)
        outs[0].append(jnp.transpose(rows4t.reshape(bp, N_KV_SLOTS, NSA_KV_HEADS, HEAD_DIM, t), (0, 4, 1, 2, 3)))
        outs[1].append(rows4_s.reshape(bs, 1, N_KV_SLOTS, NSA_KV_HEADS, HEAD_DIM))
        outs[2].append(rowsw.reshape(bp, t, 2, NSA_KV_HEADS, HEAD_DIM)[:, t - keep:])
        outs[3].append(win_s.reshape(bs, n_win, 2, NSA_KV_HEADS, HEAD_DIM))
        outs[4].append(u3[:, t - POOL_STATE:])
        outs[5].append(ext[:, 1:])
        outs[6].append(mkv.reshape(bp, m_len, 2, MEM_HEADS, HEAD_DIM))
        hp, hs = y_p, y_s
    return (hp, hs) + tuple(jnp.stack(o) for o in outs)
```

```python
import functools

import numpy as np
import jax
import jax.numpy as jnp
from jax import lax
from jax.experimental import pallas as pl
from jax.experimental.pallas import tpu as pltpu

F32 = jnp.float32
BF16 = jnp.bfloat16
I32 = jnp.int32

HEAD_DIM = 64
POOL_WINDOWS = (2, 4, 8, 16)
POOL_STATE = max(POOL_WINDOWS) - 1
NSA_HEADS = 8
NSA_KV_HEADS = 2
NSA_REP = NSA_HEADS // NSA_KV_HEADS
N_BRANCH = 3
CMP_BLOCK = 32
CMP_STRIDE = 16
CMP_HIDDEN = 2 * HEAD_DIM
SEL_BLOCK = 64
SEL_TOPK = 16
CMP_PER_SEL = SEL_BLOCK // CMP_STRIDE
WINDOW = 512
FORCE_BONUS = 1000.0
MEM_HEADS = 4
ROPE_DIM = HEAD_DIM // 4
ROPE_THETA = 500000.0
N_EXPERT_GROUPS = 4
EXPERTS_PER_GROUP = 8
N_EXPERTS = N_EXPERT_GROUPS * EXPERTS_PER_GROUP
EPS = 1e-6
N_KV_SLOTS = 4

LANES = 128
POOL_WIDTH = 256
NSA_WIDTH = NSA_HEADS * HEAD_DIM
NSA_KV_WIDTH = N_BRANCH * 2 * NSA_KV_HEADS * HEAD_DIM
GATE_WIDTH = NSA_HEADS * N_BRANCH
MEM_WIDTH = MEM_HEADS * HEAD_DIM
Q_PAD = NSA_HEADS * LANES
QM_PAD = MEM_HEADS * LANES
C_U = 0
C_Q = C_U + POOL_WIDTH
C_KV = C_Q + Q_PAD
C_QM = C_KV + NSA_KV_WIDTH
C_G = C_QM + QM_PAD
C_END = C_G + LANES

NEG = -1e30
VMEM_LIMIT = 48 * 1024 * 1024

_NT = (((1,), (1,)), ((), ()))


def _cparams(*sem):
    return pltpu.CompilerParams(dimension_semantics=tuple(sem), vmem_limit_bytes=VMEM_LIMIT)


def _dot(a, b):
    return jnp.dot(a, b, preferred_element_type=F32)


def _dot_nt(a, b):
    return lax.dot_general(a, b, _NT, preferred_element_type=F32)


def _rms(x):
    return x * lax.rsqrt(jnp.mean(x * x, axis=-1, keepdims=True) + EPS)


def _masked_softmax(s, mask):
    sm = jnp.where(mask, s, NEG)
    m = jnp.max(sm, axis=-1, keepdims=True)
    e = jnp.where(mask, jnp.exp(sm - m), 0.0)
    return e / jnp.maximum(jnp.sum(e, axis=-1, keepdims=True), 1e-30)


def _split3(x):
    hi = x.astype(BF16)
    r1 = x - hi.astype(F32)
    mid = r1.astype(BF16)
    lo = (r1 - mid.astype(F32)).astype(BF16)
    return hi, mid, lo


def _proj_kernel(x_ref, ga_ref, w_ref, gq_ref, gk_ref, gmq_ref, inv_ref, seg_ref,
                 u_ref, q_ref, rows4_ref, rowsw_ref, kvb_ref, gates_ref, qm_ref, *maybe_rows4t_ref,
                 tm, seq_len, pos0):
    i = pl.program_id(0)
    a = _rms(x_ref[...]) * ga_ref[...]
    z = _dot(a.astype(BF16), w_ref[...])
    u_ref[...] = z[:, C_U:C_U + POOL_WIDTH]

    row = i * tm + lax.broadcasted_iota(I32, (tm, 1), 0)
    pos = (pos0 + row % seq_len).astype(F32)
    ang = pos * inv_ref[...]
    d = lax.broadcasted_iota(I32, (1, LANES), 1) % HEAD_DIM
    cos = jnp.cos(ang)
    sin = jnp.sin(ang)
    half = ROPE_DIM // 2
    s_next = jnp.where(d < half, -sin, 0.0)
    s_prev = jnp.where((d >= half) & (d < ROPE_DIM), sin, 0.0)
    seg = seg_ref[...]

    def head_norm(xc, g):
        ssq = _dot((xc * xc).astype(BF16), seg)
        return xc * lax.rsqrt(ssq * (1.0 / HEAD_DIM) + EPS) * g

    def rope(xc):
        return xc * cos + pltpu.roll(xc, LANES - half, 1) * s_next + pltpu.roll(xc, half, 1) * s_prev

    scale = HEAD_DIM ** -0.5
    for c in range(NSA_HEADS):
        sl = slice(c * LANES, (c + 1) * LANES)
        qc = rope(head_norm(z[:, C_Q + c * LANES:C_Q + (c + 1) * LANES], gq_ref[:, sl]))
        q_ref[:, sl] = (qc * scale).astype(BF16)
    for br in range(N_BRANCH):
        k0 = C_KV + br * 2 * LANES
        kn = rope(head_norm(z[:, k0:k0 + LANES], gk_ref[:, br * LANES:(br + 1) * LANES]))
        vv = z[:, k0 + LANES:k0 + 2 * LANES]
        kvb_ref[:, br * 2 * LANES:br * 2 * LANES + LANES] = kn.astype(BF16)
        kvb_ref[:, br * 2 * LANES + LANES:(br + 1) * 2 * LANES] = vv.astype(BF16)
        if br < 2:
            rows4_ref[:, br * 2 * LANES:br * 2 * LANES + LANES] = kn
            rows4_ref[:, br * 2 * LANES + LANES:(br + 1) * 2 * LANES] = vv
            for rows4t_ref in maybe_rows4t_ref:
                rows4t_ref[0, br * 2 * LANES:br * 2 * LANES + LANES, :] = kn.T
                rows4t_ref[0, br * 2 * LANES + LANES:(br + 1) * 2 * LANES, :] = vv.T
        else:
            rowsw_ref[:, :LANES] = kn
            rowsw_ref[:, LANES:] = vv
    for c in range(MEM_HEADS):
        sl = slice(c * LANES, (c + 1) * LANES)
        qmc = head_norm(z[:, C_QM + c * LANES:C_QM + (c + 1) * LANES], gmq_ref[:, sl])
        qm_ref[:, sl] = (qmc * scale).astype(BF16)
    gates_ref[...] = jax.nn.sigmoid(z[:, C_G:C_END])


def _proj_call(x2d, pw, *, seq_len, pos0, tm, feature_major_rows4=False):
    n, dm = x2d.shape
    full = lambda shape: pl.BlockSpec(shape, lambda i: (0,) * len(shape))
    rows = lambda w: pl.BlockSpec((tm, w), lambda i: (i, 0))
    outs = [(POOL_WIDTH, F32), (Q_PAD, BF16), (4 * LANES, F32), (2 * LANES, F32), (NSA_KV_WIDTH, BF16),
            (LANES, F32), (QM_PAD, BF16)]
    out_specs = [rows(w) for w, _ in outs]
    out_shape = [jax.ShapeDtypeStruct((n, w), dt) for w, dt in outs]
    if feature_major_rows4:
        per_seq = seq_len // tm
        out_specs.append(pl.BlockSpec((1, 4 * LANES, tm), lambda i: (i // per_seq, 0, i % per_seq)))
        out_shape.append(jax.ShapeDtypeStruct((n // seq_len, 4 * LANES, seq_len), F32))
    return pl.pallas_call(
        functools.partial(_proj_kernel, tm=tm, seq_len=seq_len, pos0=pos0),
        grid=(n // tm,),
        in_specs=[rows(dm), full((1, dm)), full((dm, C_END)), full((1, Q_PAD)), full((1, N_BRANCH * LANES)),
                  full((1, QM_PAD)), full((1, LANES)), full((LANES, LANES))],
        out_specs=out_specs,
        out_shape=out_shape,
        compiler_params=_cparams("parallel"),
        name="proj",
    )(x2d, pw["g_attn"], pw["w_in"], pw["gq"], pw["gk"], pw["gmq"], pw["inv"], pw["seg"])


def _pool_kernel(u_ref, halo_ref, w_ref, s_ref, y_ref, *, tp, pos0):
    i = pl.program_id(1)
    u = u_ref[0]
    halo = halo_ref[0] * (i > 0).astype(F32)
    n_h = POOL_STATE + 1
    ext = jnp.concatenate([halo, u], axis=0)
    sums = {1: ext}
    w = 1
    while w < max(POOL_WINDOWS):
        sums[2 * w] = sums[w] + pltpu.roll(sums[w], w, 0)
        w *= 2
    pos = pos0 + i * tp + lax.broadcasted_iota(I32, (tp, 1), 0)
    lane_grp = lax.broadcasted_iota(I32, (1, POOL_WIDTH), 1) // (POOL_WIDTH // len(POOL_WINDOWS))
    mean = jnp.zeros((tp, POOL_WIDTH), F32)
    for gi, wdw in enumerate(POOL_WINDOWS):
        cnt = jnp.minimum(pos + 1, wdw).astype(F32)
        mean = jnp.where(lane_grp == gi, sums[wdw][n_h:] / cnt, mean)
    r = mean - u
    y_ref[0] = _dot(r.astype(BF16), w_ref[...]) * s_ref[...]


def _pool_call(u3, w_bd, s_pool, *, tp, pos0):
    b, t, c = u3.shape
    n_h = POOL_STATE + 1
    return pl.pallas_call(
        functools.partial(_pool_kernel, tp=tp, pos0=pos0),
        grid=(b, t // tp),
        in_specs=[pl.BlockSpec((1, tp, c), lambda bi, i: (bi, i, 0)),
                  pl.BlockSpec((1, n_h, c), lambda bi, i: (bi, jnp.maximum(i * (tp // n_h) - 1, 0), 0)),
                  pl.BlockSpec((c, c), lambda bi, i: (0, 0)),
                  pl.BlockSpec((1, c), lambda bi, i: (0, 0))],
        out_specs=pl.BlockSpec((1, tp, c), lambda bi, i: (bi, i, 0)),
        out_shape=jax.ShapeDtypeStruct((b, t, c), F32),
        compiler_params=_cparams("parallel", "parallel"),
        name="pool",
    )(u3, u3, w_bd, s_pool)


def _compress_core(load_piece, tail_ref, pe_ref, w_ref, w2_ref, n):
    rowid = lax.broadcasted_iota(I32, (n, 1), 0)
    outs = []
    for c in range(2):
        w = w_ref[c]
        pe8 = jnp.broadcast_to(pe_ref[c], (8, 2 * CMP_STRIDE * HEAD_DIM)).astype(BF16)
        pe_first = _dot(pe8[:, :CMP_STRIDE * HEAD_DIM], w)[0:1, :CMP_HIDDEN]
        pe_second = _dot(pe8[:, CMP_STRIDE * HEAD_DIM:], w)[0:1, CMP_HIDDEN:]
        pe_const = pe_first + pe_second
        x2 = jnp.concatenate(
            [jnp.concatenate([load_piece(c, g, r) for r in range(CMP_STRIDE)], axis=1) for g in range(NSA_KV_HEADS)],
            axis=0).astype(BF16)
        z = _dot(x2, w)
        hs = []
        for g in range(NSA_KV_HEADS):
            tail8 = jnp.broadcast_to(tail_ref[0, 2 * c + g:2 * c + g + 1, :], (8, CMP_STRIDE * HEAD_DIM)).astype(BF16)
            second_tail = _dot(tail8, w)[0:1, CMP_HIDDEN:]
            first = z[g * n:(g + 1) * n, :CMP_HIDDEN]
            second = z[g * n:(g + 1) * n, CMP_HIDDEN:]
            shifted = pltpu.roll(second, n - 1, 0)
            h = first + jnp.where(rowid == n - 1, second_tail, shifted) + pe_const
            hs.append(jax.nn.gelu(h, approximate=True))
        o = _dot(jnp.concatenate(hs, axis=0).astype(BF16), w2_ref[c])
        outs += [o[:n], o[n:]]
    return jnp.concatenate(outs, axis=1)


def _compress_prompt_kernel(k_ref, v_ref, tail_ref, pe_ref, w_ref, w2_ref, out_ref, *, n):
    def piece(c, g, r):
        return (k_ref, v_ref)[c][0, pl.ds(r, n, stride=CMP_STRIDE), :][:, g * HEAD_DIM:(g + 1) * HEAD_DIM]

    out_ref[0] = _compress_core(piece, tail_ref, pe_ref, w_ref, w2_ref, n).astype(BF16)


def _compress_prompt_call(rows4_3d, tail, cw):
    b, t, _ = rows4_3d.shape
    n = t // CMP_STRIDE
    full = lambda a: pl.BlockSpec(a.shape, lambda bi: (0,) * a.ndim)
    return pl.pallas_call(
        functools.partial(_compress_prompt_kernel, n=n),
        grid=(b,),
        in_specs=[pl.BlockSpec((1, t, LANES), lambda bi: (bi, 0, 0)),
                  pl.BlockSpec((1, t, LANES), lambda bi: (bi, 0, 1)),
                  pl.BlockSpec((1,) + tail.shape[1:], lambda bi: (bi, 0, 0)),
                  full(cw["pe"]), full(cw["w"]), full(cw["w2"])],
        out_specs=pl.BlockSpec((1, n, 2 * LANES), lambda bi: (bi, 0, 0)),
        out_shape=jax.ShapeDtypeStruct((b, n, 2 * LANES), BF16),
        compiler_params=_cparams("parallel"),
        name="compress_prompt",
    )(rows4_3d, rows4_3d, tail, cw["pe"], cw["w"], cw["w2"])


def _compress_sample_kernel(pt_ref, cache_ref, tail_ref, pe_ref, w_ref, w2_ref, out_ref,
                            buf, rows, sem, *, n, n_pages, page):
    b = pl.program_id(0)
    nb = pl.num_programs(0)

    def page_copy(bb, slot, p, c):
        return pltpu.make_async_copy(cache_ref.at[pt_ref[bb * n_pages + p], c], buf.at[slot, c, p], sem.at[slot])

    def for_all(fn):
        for p in range(n_pages):
            for c in range(2):
                fn(p, c)

    def issue(bb, slot):
        for_all(lambda p, c: page_copy(bb, slot, p, c).start())

    @pl.when(b == 0)
    def _():
        issue(0, 0)

    @pl.when(b + 1 < nb)
    def _():
        issue(b + 1, (b + 1) % 2)

    slot = b % 2
    for_all(lambda p, c: page_copy(b, slot, p, c).wait())

    def to_rows(p, carry):
        for c in range(2):
            rows[c, pl.ds(pl.multiple_of(p * page, page), page), :] = buf[slot, c, p].T
        return carry

    lax.fori_loop(0, n_pages, to_rows, 0)
    piece = lambda c, g, r: rows[c, pl.ds(r, n, stride=CMP_STRIDE), :][:, g * HEAD_DIM:(g + 1) * HEAD_DIM]
    out_ref[0] = _compress_core(piece, tail_ref, pe_ref, w_ref, w2_ref, n).astype(BF16)


def _compress_sample_call(page_table, cache_t, tail, cw):
    b, n_pages = page_table.shape
    page = cache_t.shape[3]
    n = n_pages * page // CMP_STRIDE
    full = lambda a: pl.BlockSpec(a.shape, lambda bi, pt: (0,) * a.ndim)
    grid_spec = pltpu.PrefetchScalarGridSpec(
        num_scalar_prefetch=1,
        grid=(b,),
        in_specs=[pl.BlockSpec(memory_space=pl.ANY),
                  pl.BlockSpec((1,) + tail.shape[1:], lambda bi, pt: (bi, 0, 0)),
                  full(cw["pe"]), full(cw["w"]), full(cw["w2"])],
        out_specs=pl.BlockSpec((1, n, 2 * LANES), lambda bi, pt: (bi, 0, 0)),
        scratch_shapes=[pltpu.VMEM((2, 2, n_pages, LANES, page), F32), pltpu.VMEM((2, n_pages * page, LANES), F32),
                        pltpu.SemaphoreType.DMA((2,))],
    )
    return pl.pallas_call(
        functools.partial(_compress_sample_kernel, n=n, n_pages=n_pages, page=page),
        grid_spec=grid_spec,
        out_shape=jax.ShapeDtypeStruct((b, n, 2 * LANES), BF16),
        compiler_params=_cparams("arbitrary"),
        name="compress_sample",
    )(page_table.reshape(-1), cache_t, tail, cw["pe"], cw["w"], cw["w2"])


def _topk_mask(score, ids, n_top, axis):
    sel = jnp.zeros(score.shape, jnp.bool_)
    work = score
    firsts = []
    big = float(score.shape[axis])
    for _ in range(n_top):
        m = jnp.max(work, axis=axis, keepdims=True)
        first = jnp.min(jnp.where(work == m, ids, big), axis=axis, keepdims=True)
        pick = (ids == first) & (m > -jnp.inf)
        sel = sel | pick
        work = jnp.where(pick, -jnp.inf, work)
        firsts.append(jnp.where(m > -jnp.inf, first, -1.0))
    return sel, firsts


def _gate_and_pack(o_ref_store, gates, o_c, o_s, o_w, rows):
    lane = lax.broadcasted_iota(I32, (1, LANES), 1)
    heads = []
    for h in range(NSA_HEADS):
        g, r = divmod(h, NSA_REP)
        rs = slice(r * rows, (r + 1) * rows)
        gc, gs, gw = (gates[:, N_BRANCH * h + k:N_BRANCH * h + k + 1] for k in range(N_BRANCH))
        heads.append(gc * o_c[g][rs] + gs * o_s[g][rs] + gw * o_w[g][rs])
    for j in range(NSA_HEADS // 2):
        a, b = heads[2 * j], heads[2 * j + 1]
        if (2 * j) // NSA_REP == 0:
            chunk = jnp.where(lane < HEAD_DIM, a, pltpu.roll(b, HEAD_DIM, 1))
        else:
            chunk = jnp.where(lane < HEAD_DIM, pltpu.roll(a, HEAD_DIM, 1), b)
        o_ref_store(j, chunk)


def _nsa_prompt_kernel(q_ref, gate_ref, kc_ref, kv_ref, poolt_ref, e_ref, o_ref, acc_ref, *, tq, tk, t_len):
    i = pl.program_id(1)
    s0 = i * tq
    rows4 = NSA_REP * tq
    tpos = s0 + lax.broadcasted_iota(I32, (tq, 1), 0)
    rep = lambda x: jnp.concatenate([x] * NSA_REP, axis=0)
    n_cmp = kc_ref.shape[1]
    kc128 = kc_ref[0, :, :LANES]
    vc128 = kc_ref[0, :, LANES:]
    kc_end = lax.broadcasted_iota(I32, (1, n_cmp), 1) * CMP_STRIDE + (CMP_BLOCK - 1)
    bias_c = rep(jnp.where(kc_end <= tpos, 0.0, NEG))
    any_c = rep((tpos >= CMP_BLOCK - 1).astype(F32))
    n_sel = t_len // SEL_BLOCK
    blk = lax.broadcasted_iota(I32, (n_sel, 1), 0)
    blk_f = blk.astype(F32)
    tq_lane = s0 + lax.broadcasted_iota(I32, (1, tq), 1)
    cur = tq_lane // SEL_BLOCK
    valid = blk * SEL_BLOCK <= tq_lane
    forced = (blk == 0) | (blk == cur) | (blk == cur - 1)
    w_start = pl.multiple_of(jnp.maximum(s0 - WINDOW, 0), LANES)
    w_len = WINDOW + tq
    kpos_w = w_start + lax.broadcasted_iota(I32, (1, w_len), 1)
    dist = tpos - kpos_w
    bias_w = rep(jnp.where((dist >= 0) & (dist < WINDOW), 0.0, NEG))
    c_diag = s0 // tk
    kpos_d = c_diag * tk + lax.broadcasted_iota(I32, (1, tk), 1)
    bias_causal = jnp.where(kpos_d <= tpos, 0.0, NEG)

    o_c, o_s, o_w, qgs, selbs = [], [], [], [], []
    for g in range(NSA_KV_HEADS):
        qg = jnp.concatenate([q_ref[0, :, (g * NSA_REP + r) * LANES:(g * NSA_REP + r + 1) * LANES]
                              for r in range(NSA_REP)], axis=0)
        qgs.append(qg)
        s_c = _dot_nt(qg, kc128) + bias_c
        e_c = jnp.exp(s_c - jnp.max(s_c, axis=-1, keepdims=True))
        p_c = e_c * (any_c / jnp.sum(e_c, axis=-1, keepdims=True))
        o_c.append(_dot(p_c.astype(BF16), vc128))
        imp = p_c[0:tq]
        for r in range(1, NSA_REP):
            imp = imp + p_c[r * tq:(r + 1) * tq]
        imp_t = sum(_dot_nt(poolt_ref[...], piece) for piece in _split3(imp))
        score = jnp.where(valid, imp_t + jnp.where(forced, FORCE_BONUS, 0.0), -jnp.inf)
        sel_t, _ = _topk_mask(score, blk_f, min(SEL_TOPK, n_sel), 0)
        selb_t = jnp.concatenate([jnp.where(sel_t, 0.0, NEG), jnp.full(((-n_sel) % LANES, tq), NEG, F32)], axis=0)
        selbs.append(selb_t.T.astype(BF16))

        kw = kv_ref[0, pl.ds(w_start, w_len), 4 * LANES:5 * LANES]
        vw = kv_ref[0, pl.ds(w_start, w_len), 5 * LANES:6 * LANES]
        s_w = _dot_nt(qg, kw) + bias_w
        e_w = jnp.exp(s_w - jnp.max(s_w, axis=-1, keepdims=True))
        o_w.append(_dot(e_w.astype(BF16), vw) * (1.0 / jnp.sum(e_w, axis=-1, keepdims=True)))

    acc_ref[...] = jnp.zeros(acc_ref.shape, F32)

    def chunk_step(c, carry, extra_bias):
        k0 = pl.multiple_of(c * tk, tk)
        ks = kv_ref[0, pl.ds(k0, tk), 2 * LANES:3 * LANES]
        vs = kv_ref[0, pl.ds(k0, tk), 3 * LANES:4 * LANES]
        out = []
        for g in range(NSA_KV_HEADS):
            m, l = carry[g]
            bias = _dot(selbs[g], e_ref[c])
            if extra_bias is not None:
                bias = bias + extra_bias
            s = _dot_nt(qgs[g], ks) + rep(bias)
            m_new = jnp.maximum(m, jnp.max(s, axis=-1, keepdims=True))
            alpha = jnp.exp(m - m_new)
            p = jnp.exp(s - m_new)
            l_new = alpha * l + jnp.sum(p, axis=-1, keepdims=True)
            acc_ref[g] = alpha * acc_ref[g] + _dot(p.astype(BF16), vs)
            out.append((m_new, l_new))
        return tuple(out)

    init = tuple((jnp.full((rows4, 1), NEG, F32), jnp.zeros((rows4, 1), F32)) for _ in range(NSA_KV_HEADS))
    carry = lax.fori_loop(0, c_diag, lambda c, cr: chunk_step(c, cr, None), init)
    carry = chunk_step(c_diag, carry, bias_causal)
    for g in range(NSA_KV_HEADS):
        o_s.append(acc_ref[g] * (1.0 / jnp.maximum(carry[g][1], 1e-30)))

    def store(j, chunk):
        o_ref[0, :, j * LANES:(j + 1) * LANES] = chunk

    _gate_and_pack(store, gate_ref[0], o_c, o_s, o_w, tq)


def _nsa_prompt_call(q3, gates3, kcvc, kvb3, poolt, e_mat, *, tq, tk):
    b, t, _ = q3.shape
    n_cmp = kcvc.shape[1]
    return pl.pallas_call(
        functools.partial(_nsa_prompt_kernel, tq=tq, tk=tk, t_len=t),
        grid=(b, t // tq),
        in_specs=[pl.BlockSpec((1, tq, Q_PAD), lambda bi, i: (bi, i, 0)),
                  pl.BlockSpec((1, tq, LANES), lambda bi, i: (bi, i, 0)),
                  pl.BlockSpec((1, n_cmp, 2 * LANES), lambda bi, i: (bi, 0, 0)),
                  pl.BlockSpec((1, t, NSA_KV_WIDTH), lambda bi, i: (bi, 0, 0)),
                  pl.BlockSpec(poolt.shape, lambda bi, i: (0, 0)),
                  pl.BlockSpec(e_mat.shape, lambda bi, i: (0, 0, 0))],
        out_specs=pl.BlockSpec((1, tq, NSA_WIDTH), lambda bi, i: (bi, i, 0)),
        out_shape=jax.ShapeDtypeStruct((b, t, NSA_WIDTH), F32),
        scratch_shapes=[pltpu.VMEM((NSA_KV_HEADS, NSA_REP * tq, LANES), F32)],
        compiler_params=_cparams("parallel", "arbitrary"),
        name="nsa_prompt",
    )(q3, gates3, kcvc, kvb3, poolt, e_mat)


def _memkv_kernel(x_ref, g_ref, w_ref, gk_ref, seg_ref, o_ref, ob_ref):
    a = _rms(x_ref[...]) * g_ref[...]
    z = _dot(a.astype(BF16), w_ref[...])
    for c in range(MEM_WIDTH // LANES):
        sl = slice(c * LANES, (c + 1) * LANES)
        xc = z[:, sl]
        ssq = _dot((xc * xc).astype(BF16), seg_ref[...])
        kn = xc * lax.rsqrt(ssq * (1.0 / HEAD_DIM) + EPS) * gk_ref[:, sl]
        o_ref[:, sl] = kn
        ob_ref[:, sl] = kn.astype(BF16)
    o_ref[:, MEM_WIDTH:] = z[:, MEM_WIDTH:]
    ob_ref[:, MEM_WIDTH:] = z[:, MEM_WIDTH:].astype(BF16)


def _memkv_call(mem2d, g_mem, w_mem, gmk, seg, *, tm):
    n, dm = mem2d.shape
    full = lambda shape: pl.BlockSpec(shape, lambda i: (0,) * len(shape))
    return pl.pallas_call(
        _memkv_kernel,
        grid=(n // tm,),
        in_specs=[pl.BlockSpec((tm, dm), lambda i: (i, 0)), full((1, dm)), full((dm, 2 * MEM_WIDTH)),
                  full((1, MEM_WIDTH)), full((LANES, LANES))],
        out_specs=[pl.BlockSpec((tm, 2 * MEM_WIDTH), lambda i: (i, 0))] * 2,
        out_shape=[jax.ShapeDtypeStruct((n, 2 * MEM_WIDTH), F32), jax.ShapeDtypeStruct((n, 2 * MEM_WIDTH), BF16)],
        compiler_params=_cparams("parallel"),
        name="memkv",
    )(mem2d, g_mem, w_mem, gmk, seg)


def _mem_attend(qm, mkv, rows):
    lane = lax.broadcasted_iota(I32, (1, LANES), 1)
    chunks = []
    for j in range(MEM_HEADS // 2):
        k128 = mkv[:, j * LANES:(j + 1) * LANES]
        v128 = mkv[:, MEM_WIDTH + j * LANES:MEM_WIDTH + (j + 1) * LANES]
        q2 = jnp.concatenate([qm[2 * j], qm[2 * j + 1]], axis=0)
        s = _dot_nt(q2, k128)
        e = jnp.exp(s - jnp.max(s, axis=-1, keepdims=True))
        p = e / jnp.sum(e, axis=-1, keepdims=True)
        o = _dot(p.astype(BF16), v128)
        chunks.append(jnp.where(lane < HEAD_DIM, o[:rows], o[rows:2 * rows]))
    return chunks


def _memattn_kernel(qm_ref, mkv_ref, o_ref, *, tq):
    qm = [qm_ref[0, :, h * LANES:(h + 1) * LANES] for h in range(MEM_HEADS)]
    for j, chunk in enumerate(_mem_attend(qm, mkv_ref[0], tq)):
        o_ref[0, :, j * LANES:(j + 1) * LANES] = chunk


def _memattn_call(qm3, mkv3, *, tq):
    b, t, _ = qm3.shape
    m = mkv3.shape[1]
    return pl.pallas_call(
        functools.partial(_memattn_kernel, tq=tq),
        grid=(b, t // tq),
        in_specs=[pl.BlockSpec((1, tq, QM_PAD), lambda bi, i: (bi, i, 0)),
                  pl.BlockSpec((1, m, 2 * MEM_WIDTH), lambda bi, i: (bi, 0, 0))],
        out_specs=pl.BlockSpec((1, tq, MEM_WIDTH), lambda bi, i: (bi, i, 0)),
        out_shape=jax.ShapeDtypeStruct((b, t, MEM_WIDTH), F32),
        compiler_params=_cparams("parallel", "parallel"),
        name="memattn",
    )(qm3, mkv3)


def _pad_rows(rows_list):
    x = jnp.concatenate(rows_list, axis=0)
    return jnp.concatenate([x, jnp.zeros((8 - x.shape[0], x.shape[1]), x.dtype)], axis=0)


def _sample_attn1_kernel(q_ref, kc_ref, win_ref, mkv_ref, qm_ref, pool_ref,
                         oc_ref, ow_ref, om_ref, idx_ref, *, t_pos, n_win):
    q = q_ref[0].astype(F32)
    n_cmp = kc_ref.shape[1]
    kc128 = kc_ref[0, :, :LANES]
    vc128 = kc_ref[0, :, LANES:]
    kw128 = win_ref[0, :, :LANES].astype(BF16)
    vw128 = win_ref[0, :, LANES:].astype(BF16)
    kc_end = lax.broadcasted_iota(I32, (1, n_cmp), 1) * CMP_STRIDE + (CMP_BLOCK - 1)
    mask_c = kc_end <= t_pos
    kw_pos = t_pos - (n_win - 1) + lax.broadcasted_iota(I32, (1, n_win), 1)
    mask_w = (t_pos - kw_pos >= 0) & (t_pos - kw_pos < WINDOW) & (kw_pos >= 0)
    n_slot = pool_ref.shape[1]
    blk = lax.broadcasted_iota(I32, (1, n_slot), 1)
    cur = t_pos // SEL_BLOCK
    valid = blk * SEL_BLOCK <= t_pos
    forced = (blk == 0) | (blk == cur) | (blk == cur - 1)
    lane = lax.broadcasted_iota(I32, (1, LANES), 1)
    for g in range(NSA_KV_HEADS):
        qg = _pad_rows([q[:, (g * NSA_REP + r) * LANES:(g * NSA_REP + r + 1) * LANES]
                        for r in range(NSA_REP)]).astype(BF16)
        p_c = _masked_softmax(_dot_nt(qg, kc128), mask_c)
        oc_ref[0, g * NSA_REP:(g + 1) * NSA_REP, :] = _dot(p_c.astype(BF16), vc128)[:NSA_REP]
        imp = jnp.sum(p_c[:NSA_REP], axis=0, keepdims=True)
        imp8 = jnp.broadcast_to(imp, (8, n_cmp))
        imp_b = sum(_dot(piece, pool_ref[...]) for piece in _split3(imp8))[0:1]
        score = jnp.where(valid, imp_b + jnp.where(forced, FORCE_BONUS, 0.0), -jnp.inf)
        n_valid_blocks = t_pos // SEL_BLOCK + 1
        _, firsts = _topk_mask(score, blk.astype(F32), min(SEL_TOPK, n_valid_blocks), 1)
        idxv = jnp.full((1, LANES), -1, I32)
        for j, f in enumerate(firsts):
            idxv = jnp.where(lane == j, f.astype(I32), idxv)
        idx_ref[0, g:g + 1, :] = idxv
        p_w = _masked_softmax(_dot_nt(qg, kw128), mask_w)
        ow_ref[0, g * NSA_REP:(g + 1) * NSA_REP, :] = _dot(p_w.astype(BF16), vw128)[:NSA_REP]
    qm = qm_ref[0].astype(F32)
    qmh = []
    for h in range(MEM_HEADS):
        row = qm[:, h * LANES:(h + 1) * LANES]
        qmh.append(jnp.concatenate([row, jnp.zeros((7, LANES), F32)], axis=0).astype(BF16))
    for j, chunk in enumerate(_mem_attend(qmh, mkv_ref[0].astype(BF16), 8)):
        om_ref[0, :, j * LANES:(j + 1) * LANES] = chunk[0:1]


def _sample_attn1_call(q3, kcvc, win3, mkv3, qm3, pool_mat, *, t_pos):
    b = q3.shape[0]
    n_cmp = kcvc.shape[1]
    n_win = win3.shape[1]
    m = mkv3.shape[1]
    per_b = lambda shape: pl.BlockSpec((1,) + shape, lambda bi: (bi, 0, 0))
    return pl.pallas_call(
        functools.partial(_sample_attn1_kernel, t_pos=t_pos, n_win=n_win),
        grid=(b,),
        in_specs=[per_b((1, Q_PAD)), per_b((n_cmp, 2 * LANES)), per_b((n_win, 2 * LANES)),
                  per_b((m, 2 * MEM_WIDTH)), per_b((1, QM_PAD)),
                  pl.BlockSpec(pool_mat.shape, lambda bi: (0, 0))],
        out_specs=[per_b((NSA_HEADS, LANES)), per_b((NSA_HEADS, LANES)), per_b((1, MEM_WIDTH)),
                   per_b((NSA_KV_HEADS, LANES))],
        out_shape=[jax.ShapeDtypeStruct((b, NSA_HEADS, LANES), F32), jax.ShapeDtypeStruct((b, NSA_HEADS, LANES), F32),
                   jax.ShapeDtypeStruct((b, 1, MEM_WIDTH), F32), jax.ShapeDtypeStruct((b, NSA_KV_HEADS, LANES), I32)],
        compiler_params=_cparams("parallel"),
        name="sample_attn1",
    )(q3, kcvc, win3, mkv3, qm3, pool_mat)


def _sample_sel_kernel(pt_ref, idx_ref, cache_ref, q_ref, knew_ref, gate_ref, oc_ref, ow_ref, o_ref,
                       buf, sem, *, t_pos, n_pages, n_top, page):
    b = pl.program_id(0)
    nb = pl.num_programs(0)
    blk_per_page = page // SEL_BLOCK
    n_past_blk = n_pages * blk_per_page
    idx_stride = NSA_KV_HEADS * LANES

    def blk_at(bb, g, j):
        return idx_ref[bb * idx_stride + g * LANES + j]

    def blk_copy(bb, slot, g, j, kv):
        blkc = jnp.clip(blk_at(bb, g, j), 0, n_past_blk - 1)
        pg = pt_ref[bb * n_pages + blkc // blk_per_page]
        return pltpu.make_async_copy(cache_ref.at[pg, 2 + kv], buf.at[slot, g, kv, j], sem.at[slot])

    def for_all(fn):
        for g in range(NSA_KV_HEADS):
            for j in range(n_top):
                for kv in range(2):
                    fn(g, j, kv)

    def issue(bb, slot):
        for_all(lambda g, j, kv: blk_copy(bb, slot, g, j, kv).start())

    @pl.when(b == 0)
    def _():
        issue(0, 0)

    @pl.when(b + 1 < nb)
    def _():
        issue(b + 1, (b + 1) % 2)

    slot = b % 2
    for_all(lambda g, j, kv: blk_copy(b, slot, g, j, kv).wait())

    q = q_ref[0].astype(F32)
    knew = knew_ref[0]
    k_new = knew[:, :LANES].astype(BF16).astype(F32)
    v_new = knew[:, LANES:].astype(BF16).astype(F32)
    n_keys = n_top * page
    key_lane = lax.broadcasted_iota(I32, (1, n_keys), 1)
    key_slot = key_lane // page
    key_row = key_lane % page
    cur_blk = t_pos // SEL_BLOCK
    o_s = []
    for g in range(NSA_KV_HEADS):
        qg = _pad_rows([q[:, (g * NSA_REP + r) * LANES:(g * NSA_REP + r + 1) * LANES] for r in range(NSA_REP)])
        blkvec = jnp.full((1, n_keys), -1, I32)
        has_cur = jnp.zeros((1, 1), jnp.bool_)
        for j in range(n_top):
            bj = blk_at(b, g, j)
            blkvec = jnp.where(key_slot == j, bj, blkvec)
            has_cur = has_cur | (bj == cur_blk)
        in_blk = key_row // SEL_BLOCK == blkvec % blk_per_page
        key_pos = (blkvec // blk_per_page) * page + key_row
        vis = (blkvec >= 0) & (blkvec < n_past_blk) & in_blk & (key_pos <= t_pos)
        kt = jnp.concatenate([buf[slot, g, 0, j] for j in range(n_top)], axis=1).astype(BF16)
        vt = jnp.concatenate([buf[slot, g, 1, j] for j in range(n_top)], axis=1).astype(BF16)
        s_past = jnp.where(vis, _dot(qg.astype(BF16), kt), NEG)
        s_new = jnp.where(has_cur, jnp.sum(qg * k_new, axis=-1, keepdims=True), NEG)
        m = jnp.maximum(jnp.max(s_past, axis=-1, keepdims=True), s_new)
        e_p = jnp.where(vis, jnp.exp(s_past - m), 0.0)
        e_n = jnp.where(has_cur, jnp.exp(s_new - m), 0.0)
        den = jnp.maximum(jnp.sum(e_p, axis=-1, keepdims=True) + e_n, 1e-30)
        num = _dot_nt(e_p.astype(BF16), vt) + e_n.astype(BF16).astype(F32) * v_new
        o_s.append(num / den)
    o_c = [jnp.concatenate([oc_ref[0, g * NSA_REP:(g + 1) * NSA_REP, :]] * 2, axis=0) for g in range(NSA_KV_HEADS)]
    o_w = [jnp.concatenate([ow_ref[0, g * NSA_REP:(g + 1) * NSA_REP, :]] * 2, axis=0) for g in range(NSA_KV_HEADS)]

    def store(j, chunk):
        o_ref[0, :, j * LANES:(j + 1) * LANES] = chunk

    _gate_and_pack(store, gate_ref[0], o_c, o_s, o_w, 1)


def _sample_sel_call(page_table, idx, cache_t, q3, knew3, gates3, oc, ow, *, t_pos, n_top):
    b, n_pages = page_table.shape
    page = cache_t.shape[3]
    per_b = lambda shape: pl.BlockSpec((1,) + shape, lambda bi, pt, ix: (bi, 0, 0))
    grid_spec = pltpu.PrefetchScalarGridSpec(
        num_scalar_prefetch=2,
        grid=(b,),
        in_specs=[pl.BlockSpec(memory_space=pl.ANY), per_b((1, Q_PAD)), per_b((1, 2 * LANES)), per_b((1, LANES)),
                  per_b((NSA_HEADS, LANES)), per_b((NSA_HEADS, LANES))],
        out_specs=per_b((1, NSA_WIDTH)),
        scratch_shapes=[pltpu.VMEM((2, NSA_KV_HEADS, 2, n_top, LANES, page), F32), pltpu.SemaphoreType.DMA((2,))],
    )
    return pl.pallas_call(
        functools.partial(_sample_sel_kernel, t_pos=t_pos, n_pages=n_pages, n_top=n_top, page=page),
        grid_spec=grid_spec,
        out_shape=jax.ShapeDtypeStruct((b, 1, NSA_WIDTH), F32),
        compiler_params=_cparams("arbitrary"),
        name="sample_sel",
    )(page_table.reshape(-1), idx.reshape(-1), cache_t, q3, knew3, gates3, oc, ow)


def _finish_kernel(yp_ref, on_ref, om_ref, x_ref, gm_ref, wo_ref, gf_ref, wr_ref, br_ref, tri_ref,
                   h_ref, hn_ref, route_ref, counts_ref, cnt_ref):
    gm = gm_ref[...]
    o1 = POOL_WIDTH
    o2 = o1 + NSA_WIDTH
    mixed = jnp.concatenate([_rms(yp_ref[...]) * gm[:, :o1], _rms(on_ref[...]) * gm[:, o1:o2],
                             _rms(om_ref[...]) * gm[:, o2:]], axis=-1)
    h = x_ref[...] + _dot(mixed.astype(BF16), wo_ref[...])
    h_ref[...] = h
    hn = _rms(h) * gf_ref[...]
    hn_ref[...] = hn
    logits = _dot(hn.astype(BF16), wr_ref[...]) + br_ref[...]
    lane = lax.broadcasted_iota(I32, (1, LANES), 1)
    lane_f = lane.astype(F32)
    is1 = lane < N_EXPERT_GROUPS
    m1 = jnp.max(jnp.where(is1, logits, -jnp.inf), axis=-1, keepdims=True)
    e1 = jnp.where(is1, jnp.exp(logits - m1), 0.0)
    p1 = e1 / jnp.sum(e1, axis=-1, keepdims=True)
    top1_p = jnp.max(p1, axis=-1, keepdims=True)
    grp = jnp.min(jnp.where((p1 == top1_p) & is1, lane_f, float(LANES)), axis=-1, keepdims=True)
    base = N_EXPERT_GROUPS + grp * EXPERTS_PER_GROUP
    in_g = (lane_f >= base) & (lane_f < base + EXPERTS_PER_GROUP)
    l2 = jnp.where(in_g, logits, -jnp.inf)
    v0 = jnp.max(l2, axis=-1, keepdims=True)
    i0 = jnp.min(jnp.where(l2 == v0, lane_f, float(LANES)), axis=-1, keepdims=True)
    l2b = jnp.where(lane_f == i0, -jnp.inf, l2)
    v1 = jnp.max(l2b, axis=-1, keepdims=True)
    i1 = jnp.min(jnp.where(l2b == v1, lane_f, float(LANES)), axis=-1, keepdims=True)
    ex = jnp.exp(v1 - v0)
    w0 = top1_p / (1.0 + ex)
    w1 = top1_p * ex / (1.0 + ex)
    ex0 = i0 - N_EXPERT_GROUPS
    ex1 = i1 - N_EXPERT_GROUPS
    is0 = lane_f == ex0
    is1e = lane_f == ex1
    oh0 = jnp.where(is0, 1.0, 0.0)
    oh1 = jnp.where(is1e, 1.0, 0.0)
    before0 = _dot(tri_ref[...], oh0.astype(BF16))
    before1 = _dot(tri_ref[...], oh1.astype(BF16))
    tot0 = jnp.sum(oh0, axis=0, keepdims=True)
    tot1 = jnp.sum(oh1, axis=0, keepdims=True)

    @pl.when(pl.program_id(0) == 0)
    def _():
        cnt_ref[...] = jnp.zeros(cnt_ref.shape, F32)

    seen = cnt_ref[...]
    rank0 = jnp.sum(jnp.where(is0, before0 + seen, 0.0), axis=-1, keepdims=True)
    rank1 = jnp.sum(jnp.where(is1e, before1 + seen + tot0, 0.0), axis=-1, keepdims=True)
    cnt_ref[...] = seen + tot0 + tot1
    counts_ref[...] = seen + tot0 + tot1
    route = jnp.where(lane == 0, ex0, jnp.where(lane == 1, ex1, jnp.where(lane == 2, w0, jnp.where(lane == 3, w1,
            jnp.where(lane == 4, rank0, jnp.where(lane == 5, rank1, 0.0))))))
    route_ref[...] = route


def _finish_call(yp, on, om, x2d, fw, *, tm):
    n, dm = x2d.shape
    full = lambda shape: pl.BlockSpec(shape, lambda i: (0,) * len(shape))
    rows = lambda w: pl.BlockSpec((tm, w), lambda i: (i, 0))
    tri = jnp.asarray(np.arange(tm)[None, :] < np.arange(tm)[:, None], BF16)
    return pl.pallas_call(
        _finish_kernel,
        grid=(n // tm,),
        in_specs=[rows(POOL_WIDTH), rows(NSA_WIDTH), rows(MEM_WIDTH), rows(dm), full((1, dm)), full((dm, dm)),
                  full((1, dm)), full((dm, LANES)), full((1, LANES)), full((tm, tm))],
        out_specs=[rows(dm), rows(dm), rows(LANES), full((1, LANES))],
        out_shape=[jax.ShapeDtypeStruct((n, dm), F32), jax.ShapeDtypeStruct((n, dm), F32),
                   jax.ShapeDtypeStruct((n, LANES), F32), jax.ShapeDtypeStruct((1, LANES), F32)],
        scratch_shapes=[pltpu.VMEM((1, LANES), F32)],
        compiler_params=_cparams("arbitrary"),
        name="finish",
    )(yp, on, om, x2d, fw["g_mix"], fw["w_out"], fw["g_ffn"], fw["w_r"], fw["b_r"], tri)


def _route_tables(route, counts, tm):
    n = route.shape[0]
    eid = route[:, 0:2].astype(I32)
    rank = route[:, 4:6].astype(I32)
    cnt = counts[0, :N_EXPERTS].astype(I32)
    tiles_per = (cnt + tm - 1) // tm
    tile_end = jnp.cumsum(tiles_per)
    tile_start = tile_end - tiles_per
    experts = jnp.arange(N_EXPERTS, dtype=I32)
    start_of = jnp.sum(jnp.where(eid[:, :, None] == experts, tile_start, 0), axis=-1)
    pos = (start_of * tm + rank).reshape(-1)
    n_tiles = (2 * n) // tm + N_EXPERTS
    tj = jnp.arange(n_tiles, dtype=I32)
    tile_expert = jnp.minimum(jnp.sum((tj[:, None] >= tile_end[None, :]).astype(I32), axis=1), N_EXPERTS - 1)
    sel = tile_expert[:, None] == experts
    rows_left = jnp.sum(jnp.where(sel, cnt, 0), axis=-1) - (tj - jnp.sum(jnp.where(sel, tile_start, 0), axis=-1)) * tm
    tile_nvalid = jnp.where(tj < tile_end[-1], jnp.clip(rows_left, 0, tm), 0).astype(I32)
    return tile_expert, tile_nvalid, pos


def _row_wait_all(src_row, dst_row, sem, count):
    for _ in range(count):
        pltpu.make_async_copy(src_row, dst_row, sem).wait()


def _dispatch_kernel(pos_ref, nv_ref, hn_ref, xs_ref, stage, zbuf, sem, zsem, *, tm, tme, n_tiles):
    i = pl.program_id(0)
    nt = pl.num_programs(0)
    slot = i % 2
    wait_slot = lambda s: _row_wait_all(stage.at[s, pl.ds(0, 1), :], xs_ref.at[pl.ds(0, 1), :], sem.at[s], 2 * tm)

    @pl.when(i == 0)
    def _():
        zbuf[...] = jnp.zeros(zbuf.shape, F32)

        def fill(j, c):
            @pl.when(nv_ref[j] < tme)
            def _():
                pltpu.make_async_copy(zbuf, xs_ref.at[pl.ds(pl.multiple_of(j * tme, tme), tme), :], zsem).start()
            return c

        def drain(j, c):
            @pl.when(nv_ref[j] < tme)
            def _():
                pltpu.make_async_copy(zbuf, xs_ref.at[pl.ds(0, tme), :], zsem).wait()
            return c

        lax.fori_loop(0, n_tiles, fill, 0)
        lax.fori_loop(0, n_tiles, drain, 0)

    @pl.when(i >= 2)
    def _():
        wait_slot(slot)

    stage[slot] = hn_ref[...]
    base = i * (2 * tm)
    for r in range(tm):
        for k in range(2):
            dst = pos_ref[base + 2 * r + k]
            pltpu.make_async_copy(stage.at[slot, pl.ds(r, 1), :], xs_ref.at[pl.ds(dst, 1), :], sem.at[slot]).start()

    @pl.when(i == nt - 1)
    def _():
        wait_slot(slot)

        @pl.when(i >= 1)
        def _():
            wait_slot(1 - slot)


def _dispatch_call(pos, tile_nvalid, hn, *, tm, tme):
    n, dm = hn.shape
    n_tiles = tile_nvalid.shape[0]
    grid_spec = pltpu.PrefetchScalarGridSpec(
        num_scalar_prefetch=2,
        grid=(n // tm,),
        in_specs=[pl.BlockSpec((tm, dm), lambda i, pos, nv: (i, 0))],
        out_specs=pl.BlockSpec(memory_space=pl.ANY),
        scratch_shapes=[pltpu.VMEM((2, tm, dm), F32), pltpu.VMEM((tme, dm), F32),
                        pltpu.SemaphoreType.DMA((2,)), pltpu.SemaphoreType.DMA(())],
    )
    return pl.pallas_call(
        functools.partial(_dispatch_kernel, tm=tm, tme=tme, n_tiles=n_tiles),
        grid_spec=grid_spec,
        out_shape=jax.ShapeDtypeStruct((n_tiles * tme, dm), F32),
        compiler_params=_cparams("arbitrary"),
        name="moe_dispatch",
    )(pos, tile_nvalid, hn)


def _expert_kernel(te_ref, nv_ref, x_ref, wg_ref, wu_ref, wd_ref, y_ref, wgb, wub, wdb, *, tm):
    i = pl.program_id(0)
    prev = te_ref[jnp.maximum(i - 1, 0)]

    @pl.when((i == 0) | (te_ref[i] != prev))
    def _():
        wgb[...] = wg_ref[0].astype(BF16)
        wub[...] = wu_ref[0].astype(BF16)
        wdb[...] = wd_ref[0].astype(BF16)

    nv = nv_ref[i]

    @pl.when(nv == 0)
    def _():
        y_ref[...] = jnp.zeros(y_ref.shape, F32)

    @pl.when(nv > 0)
    def _():
        x = x_ref[...].astype(BF16)
        hg = _dot(x, wgb[...])
        hu = _dot(x, wub[...])
        hmid = hg * jax.nn.sigmoid(hg) * hu
        y_ref[...] = _dot(hmid.astype(BF16), wdb[...])


def _expert_call(tile_expert, tile_nvalid, xs, wg, wu, wd, *, tm):
    n_tiles = tile_expert.shape[0]
    dm = xs.shape[1]
    ff = wg.shape[2]
    grid_spec = pltpu.PrefetchScalarGridSpec(
        num_scalar_prefetch=2,
        grid=(n_tiles,),
        in_specs=[pl.BlockSpec((tm, dm), lambda i, te, nv: (i, 0)),
                  pl.BlockSpec((1, dm, ff), lambda i, te, nv: (te[i], 0, 0)),
                  pl.BlockSpec((1, dm, ff), lambda i, te, nv: (te[i], 0, 0)),
                  pl.BlockSpec((1, ff, dm), lambda i, te, nv: (te[i], 0, 0))],
        out_specs=pl.BlockSpec((tm, dm), lambda i, te, nv: (i, 0)),
        scratch_shapes=[pltpu.VMEM((dm, ff), BF16), pltpu.VMEM((dm, ff), BF16), pltpu.VMEM((ff, dm), BF16)],
    )
    return pl.pallas_call(
        functools.partial(_expert_kernel, tm=tm),
        grid_spec=grid_spec,
        out_shape=jax.ShapeDtypeStruct(xs.shape, F32),
        compiler_params=_cparams("arbitrary"),
        name="moe_experts",
    )(tile_expert, tile_nvalid, xs, wg, wu, wd)


def _combine_kernel(pos_ref, ys_ref, h_ref, route_ref, o_ref, gbuf, sem, *, tm):
    i = pl.program_id(0)
    nt = pl.num_programs(0)
    slot = i % 2

    def issue(step, s):
        base = step * (2 * tm)
        for r in range(tm):
            for k in range(2):
                src = pos_ref[base + 2 * r + k]
                pltpu.make_async_copy(ys_ref.at[pl.ds(src, 1), :], gbuf.at[s, k, pl.ds(r, 1), :], sem.at[s]).start()

    @pl.when(i == 0)
    def _():
        issue(0, 0)

    @pl.when(i + 1 < nt)
    def _():
        issue(i + 1, 1 - slot)

    _row_wait_all(ys_ref.at[pl.ds(0, 1), :], gbuf.at[slot, 0, pl.ds(0, 1), :], sem.at[slot], 2 * tm)
    w0 = route_ref[:, 2:3]
    w1 = route_ref[:, 3:4]
    o_ref[...] = h_ref[...] + (w0 * gbuf[slot, 0] + w1 * gbuf[slot, 1])


def _combine_call(pos, ys, h, route, *, tm):
    n, dm = h.shape
    grid_spec = pltpu.PrefetchScalarGridSpec(
        num_scalar_prefetch=1,
        grid=(n // tm,),
        in_specs=[pl.BlockSpec(memory_space=pl.ANY),
                  pl.BlockSpec((tm, dm), lambda i, pos: (i, 0)),
                  pl.BlockSpec((tm, LANES), lambda i, pos: (i, 0))],
        out_specs=pl.BlockSpec((tm, dm), lambda i, pos: (i, 0)),
        scratch_shapes=[pltpu.VMEM((2, 2, tm, dm), F32), pltpu.SemaphoreType.DMA((2,))],
    )
    return pl.pallas_call(
        functools.partial(_combine_kernel, tm=tm),
        grid_spec=grid_spec,
        out_shape=jax.ShapeDtypeStruct((n, dm), F32),
        compiler_params=_cparams("arbitrary"),
        name="moe_combine",
    )(pos, ys, h, route)


def _moe_sample_kernel(hn_ref, route_ref, h_ref, wg_ref, wu_ref, wd_ref, o_ref, acc_ref):
    e = pl.program_id(0)

    @pl.when(e == 0)
    def _():
        acc_ref[...] = jnp.zeros(acc_ref.shape, F32)

    route = route_ref[...]
    ef = e.astype(F32)
    comb = jnp.where(route[:, 0:1] == ef, route[:, 2:3], 0.0) + jnp.where(route[:, 1:2] == ef, route[:, 3:4], 0.0)
    x = hn_ref[...].astype(BF16)
    hg = _dot(x, wg_ref[0].astype(BF16))
    hu = _dot(x, wu_ref[0].astype(BF16))
    hmid = hg * jax.nn.sigmoid(hg) * hu * comb
    acc_ref[...] += _dot(hmid.astype(BF16), wd_ref[0].astype(BF16))

    @pl.when(e == pl.num_programs(0) - 1)
    def _():
        o_ref[...] = h_ref[...] + acc_ref[...]


def _moe_sample_call(hn, route, h, wg, wu, wd):
    n, dm = hn.shape
    ff = wg.shape[2]
    full = lambda shape: pl.BlockSpec(shape, lambda e: (0,) * len(shape))
    return pl.pallas_call(
        _moe_sample_kernel,
        grid=(N_EXPERTS,),
        in_specs=[full((n, dm)), full((n, LANES)), full((n, dm)),
                  pl.BlockSpec((1, dm, ff), lambda e: (e, 0, 0)), pl.BlockSpec((1, dm, ff), lambda e: (e, 0, 0)),
                  pl.BlockSpec((1, ff, dm), lambda e: (e, 0, 0))],
        out_specs=full((n, dm)),
        out_shape=jax.ShapeDtypeStruct((n, dm), F32),
        scratch_shapes=[pltpu.VMEM((n, dm), F32)],
        compiler_params=_cparams("arbitrary"),
        name="moe_sample",
    )(hn, route, h, wg, wu, wd)


def _prep_weights(l, g_attn, w_in, g_q, g_k, pe_cmp, w_cmp1, w_cmp2, w_pool, s_pool, g_mem, w_mem_kv, g_mq, g_mk,
                  g_mix, w_out, g_ffn, w_router1, b_router1, w_router2, b_router2):
    dm = w_in.shape[1]
    w = w_in[l]
    o1 = POOL_WIDTH
    o2 = o1 + NSA_WIDTH
    o3 = o2 + NSA_KV_WIDTH
    o4 = o3 + GATE_WIDTH
    wq = w[:, o1:o2].reshape(dm, NSA_HEADS, 1, HEAD_DIM)
    q_slot = jnp.asarray(np.eye(NSA_KV_HEADS, dtype=np.float32)[np.arange(NSA_HEADS) // NSA_REP])
    wq_pad = (wq * q_slot[None, :, :, None]).reshape(dm, Q_PAD)
    wqm = w[:, o4:].reshape(dm, MEM_HEADS, 1, HEAD_DIM)
    m_slot = jnp.asarray(np.eye(2, dtype=np.float32)[np.arange(MEM_HEADS) % 2])
    wqm_pad = (wqm * m_slot[None, :, :, None]).reshape(dm, QM_PAD)
    wg_pad = jnp.pad(w[:, o3:o4], ((0, 0), (0, LANES - GATE_WIDTH)))
    w_packed = jnp.concatenate([w[:, :o1], wq_pad, w[:, o2:o3], wqm_pad, wg_pad], axis=1).astype(BF16)
    half = ROPE_DIM // 2
    inv = jnp.power(ROPE_THETA, -jnp.arange(half, dtype=F32) * 2.0 / ROPE_DIM)
    d = np.arange(LANES) % HEAD_DIM
    inv_lane = jnp.where(jnp.asarray(d < ROPE_DIM), inv[jnp.asarray(d % half)], 0.0).reshape(1, LANES)
    seg = jnp.asarray((np.arange(LANES)[:, None] // HEAD_DIM == np.arange(LANES)[None, :] // HEAD_DIM), BF16)
    gk = jnp.concatenate([jnp.tile(g_k[l, br], 2) for br in range(N_BRANCH)]).reshape(1, N_BRANCH * LANES)
    proj = dict(g_attn=g_attn[l].reshape(1, dm), w_in=w_packed, gq=jnp.tile(g_q[l], Q_PAD // HEAD_DIM).reshape(1, Q_PAD),
                gk=gk, gmq=jnp.tile(g_mq[l], QM_PAD // HEAD_DIM).reshape(1, QM_PAD), inv=inv_lane, seg=seg)

    n_grp = len(POOL_WINDOWS)
    pg = POOL_WIDTH // n_grp
    w_bd = (jnp.asarray(np.eye(n_grp, dtype=np.float32))[:, None, :, None] * w_pool[l][:, :, None, :]
            ).reshape(POOL_WIDTH, POOL_WIDTH).astype(BF16)
    pool = dict(w=w_bd, s=s_pool[l].reshape(1, POOL_WIDTH))

    half_rows = CMP_STRIDE * HEAD_DIM
    w1 = jnp.concatenate([w_cmp1[l][:, :half_rows], w_cmp1[l][:, half_rows:]], axis=2).astype(BF16)
    cmp_w = dict(w=w1, pe=pe_cmp[l].reshape(2, 1, CMP_BLOCK * HEAD_DIM), w2=w_cmp2[l].astype(BF16))

    mem = dict(g=g_mem[l].reshape(1, dm), w=w_mem_kv[l].astype(BF16),
               gk=jnp.tile(g_mk[l], MEM_HEADS).reshape(1, MEM_WIDTH))
    w_r = jnp.concatenate([w_router1[l], w_router2[l].reshape(dm, N_EXPERTS)], axis=1)
    w_r = jnp.pad(w_r, ((0, 0), (0, LANES - w_r.shape[1]))).astype(BF16)
    b_r = jnp.concatenate([b_router1[l], b_router2[l].reshape(-1)])
    b_r = jnp.pad(b_r, (0, LANES - b_r.shape[0])).reshape(1, LANES)
    fin = dict(g_mix=g_mix[l].reshape(1, dm), w_out=w_out[l].astype(BF16), g_ffn=g_ffn[l].reshape(1, dm),
               w_r=w_r, b_r=b_r)
    return proj, pool, cmp_w, mem, fin


def _pick(n, prefs):
    for p in prefs:
        if n % p == 0:
            return p
    return n


def kernel(x_prompt, x_sample, cache_kv, cache_win, state_pool, cache_mem_kv, page_table, mem_prompt, g_attn, w_in, g_q, g_k, pe_cmp, w_cmp1, w_cmp2, w_pool, s_pool, g_mem, w_mem_kv, g_mq, g_mk, g_mix, w_out, g_ffn, w_router1, b_router1, w_router2, b_router2, w_gate, w_up, w_down):
    depth = w_in.shape[0]
    bp, t, dm = x_prompt.shape
    bs, ts, _ = x_sample.shape
    n_pages = page_table.shape[1]
    page = cache_kv.shape[2]
    past_len = n_pages * page
    n_win = cache_win.shape[2]
    ff = w_gate.shape[-1]
    tq, tk = 256, 512
    assert ts == 1 and n_win == WINDOW and page % SEL_BLOCK == 0 and t % tk == 0 and t >= WINDOW + tq
    n_sel = t // SEL_BLOCK
    assert min(SEL_TOPK, n_sel) >= 3 and n_sel <= LANES
    poolt = jnp.asarray(np.arange(t // CMP_STRIDE)[None, :] // CMP_PER_SEL == np.arange(n_sel)[:, None], BF16)
    key_blk = np.arange(t // tk)[:, None, None] * (tk // SEL_BLOCK) + np.arange(tk)[None, None, :] // SEL_BLOCK
    e_mat = jnp.asarray(np.arange(LANES)[None, :, None] == key_blk, BF16)
    n_cmp_s = past_len // CMP_STRIDE
    n_slot_s = -(-(past_len // SEL_BLOCK + 1) // LANES) * LANES
    pool_s = jnp.asarray(np.arange(n_cmp_s)[:, None] // CMP_PER_SEL == np.arange(n_slot_s)[None, :], BF16)

    hp, hs = x_prompt, x_sample
    outs = [[] for _ in range(7)]
    for l in range(depth):
        proj_w, pool_w, cmp_w, mem_w, fin_w = _prep_weights(
            l, g_attn, w_in, g_q, g_k, pe_cmp, w_cmp1, w_cmp2, w_pool, s_pool, g_mem, w_mem_kv, g_mq, g_mk,
            g_mix, w_out, g_ffn, w_router1, b_router1, w_router2, b_router2)
        wg = w_gate[l].reshape(N_EXPERTS, dm, ff)
        wu = w_up[l].reshape(N_EXPERTS, dm, ff)
        wd = w_down[l].reshape(N_EXPERTS, ff, dm)
        kw_cols = CMP_STRIDE * 2 * LANES

        n = bp * t
        xp2 = hp.reshape(n, dm)
        u, qb, rows4, rowsw, kvb, gates, qmb, rows4t = _proj_call(
            xp2, proj_w, seq_len=t, pos0=0, tm=_pick(t, (512, 256, 128)), feature_major_rows4=True)
        u3 = u.reshape(bp, t, POOL_WIDTH)
        y_pool = _pool_call(u3, pool_w["w"], pool_w["s"], tp=_pick(t, (512, 256, 128)), pos0=0)
        n_slab = 2 * NSA_KV_HEADS
        chunk_w = CMP_STRIDE * HEAD_DIM
        kcvc = _compress_prompt_call(rows4.reshape(bp, t, 4 * LANES), jnp.zeros((bp, n_slab, chunk_w), F32), cmp_w)
        o_nsa = _nsa_prompt_call(qb.reshape(bp, t, Q_PAD), gates.reshape(bp, t, LANES), kcvc,
                                 kvb.reshape(bp, t, NSA_KV_WIDTH), poolt, e_mat, tq=tq, tk=tk)
        m_len = mem_prompt.shape[1]
        mkv, mkvb = _memkv_call(mem_prompt.reshape(bp * m_len, dm), mem_w["g"], mem_w["w"], mem_w["gk"],
                                proj_w["seg"], tm=_pick(bp * m_len, (256, 128)))
        o_mem = _memattn_call(qmb.reshape(bp, t, QM_PAD), mkvb.reshape(bp, m_len, 2 * MEM_WIDTH),
                              tq=_pick(t, (512, 256, 128)))
        tmf = _pick(n, (512, 256, 128))
        h_p, hn_p, route_p, counts_p = _finish_call(y_pool.reshape(n, POOL_WIDTH), o_nsa.reshape(n, NSA_WIDTH),
                                                    o_mem.reshape(n, MEM_WIDTH), xp2, fin_w, tm=tmf)
        tme = 256
        tmd = _pick(n, (256, 128))
        te, nv, pos = _route_tables(route_p, counts_p, tme)
        xs = _dispatch_call(pos, nv, hn_p, tm=tmd, tme=tme)
        ys = _expert_call(te, nv, xs, wg, wu, wd, tm=tme)
        y_p = _combine_call(pos, ys, h_p, route_p, tm=tmd).reshape(bp, t, dm)

        xs2 = hs.reshape(bs, dm)
        u_s, qb_s, rows4_s, rowsw_s, _, gates_s, qmb_s = _proj_call(xs2, proj_w, seq_len=1, pos0=past_len, tm=bs)
        ext = jnp.concatenate([state_pool[l], u_s[:, None, :]], axis=1)
        y_pool_s = _pool_call(ext, pool_w["w"], pool_w["s"], tp=POOL_STATE + 1, pos0=past_len - POOL_STATE)[:, -1, :]
        cache_t = jnp.transpose(cache_kv[l], (0, 2, 3, 4, 1)).reshape(
            cache_kv.shape[1], N_KV_SLOTS, NSA_KV_HEADS * HEAD_DIM, page)
        tail_s = jnp.pad(rows4_s[:, :2 * LANES].reshape(bs, n_slab, HEAD_DIM), ((0, 0), (0, 0), (0, chunk_w - HEAD_DIM)))
        kcvc_s = _compress_sample_call(page_table, cache_t, tail_s, cmp_w)
        win_s = jnp.concatenate([cache_win[l].reshape(bs, n_win, 2 * LANES)[:, 1:], rowsw_s[:, None, :]], axis=1)
        mkv_s = cache_mem_kv[l].reshape(bs, cache_mem_kv.shape[2], 2 * MEM_WIDTH)
        oc, ow, om_s, idx = _sample_attn1_call(qb_s.reshape(bs, 1, Q_PAD), kcvc_s, win_s, mkv_s,
                                               qmb_s.reshape(bs, 1, QM_PAD), pool_s, t_pos=past_len)
        n_top = min(SEL_TOPK, past_len // SEL_BLOCK + 1)
        o_nsa_s = _sample_sel_call(page_table, idx, cache_t, qb_s.reshape(bs, 1, Q_PAD),
                                   rows4_s[:, 2 * LANES:].reshape(bs, 1, 2 * LANES), gates_s.reshape(bs, 1, LANES),
                                   oc, ow, t_pos=past_len, n_top=n_top)
        h_s, hn_s, route_s, _ = _finish_call(y_pool_s, o_nsa_s.reshape(bs, NSA_WIDTH), om_s.reshape(bs, MEM_WIDTH),
                                          xs2, fin_w, tm=bs)
        y_s = _moe_sample_call(hn_s, route_s, h_s, wg, wu, wd).reshape(bs, 1, dm)

        keep = min(WINDOW, t)
        outs[0].append(jnp.transpose(rows4t.reshape(bp, N_KV_SLOTS, NSA_KV_HEADS, HEAD_DIM, t), (0, 4, 1, 2, 3)))
        outs[1].append(rows4_s.reshape(bs, 1, N_KV_SLOTS, NSA_KV_HEADS, HEAD_DIM))
        outs[2].append(rowsw.reshape(bp, t, 2, NSA_KV_HEADS, HEAD_DIM)[:, t - keep:])
        outs[3].append(win_s.reshape(bs, n_win, 2, NSA_KV_HEADS, HEAD_DIM))
        outs[4].append(u3[:, t - POOL_STATE:])
        outs[5].append(ext[:, 1:])
        outs[6].append(mkv.reshape(bp, m_len, 2, MEM_HEADS, HEAD_DIM))
        hp, hs = y_p, y_s
    return (hp, hs) + tuple(jnp.stack(o) for o in outs)
```

```python
import functools

import numpy as np
import jax
import jax.numpy as jnp
from jax import lax
from jax.experimental import pallas as pl
from jax.experimental.pallas import tpu as pltpu

F32 = jnp.float32
BF16 = jnp.bfloat16
I32 = jnp.int32

HEAD_DIM = 64
POOL_WINDOWS = (2, 4, 8, 16)
POOL_STATE = max(POOL_WINDOWS) - 1
NSA_HEADS = 8
NSA_KV_HEADS = 2
NSA_REP = NSA_HEADS // NSA_KV_HEADS
N_BRANCH = 3
CMP_BLOCK = 32
CMP_STRIDE = 16
CMP_HIDDEN = 2 * HEAD_DIM
SEL_BLOCK = 64
SEL_TOPK = 16
CMP_PER_SEL = SEL_BLOCK // CMP_STRIDE
WINDOW = 512
FORCE_BONUS = 1000.0
MEM_HEADS = 4
ROPE_DIM = HEAD_DIM // 4
ROPE_THETA = 500000.0
N_EXPERT_GROUPS = 4
EXPERTS_PER_GROUP = 8
N_EXPERTS = N_EXPERT_GROUPS * EXPERTS_PER_GROUP
EPS = 1e-6
N_KV_SLOTS = 4

LANES = 128
POOL_WIDTH = 256
NSA_WIDTH = NSA_HEADS * HEAD_DIM
NSA_KV_WIDTH = N_BRANCH * 2 * NSA_KV_HEADS * HEAD_DIM
GATE_WIDTH = NSA_HEADS * N_BRANCH
MEM_WIDTH = MEM_HEADS * HEAD_DIM
Q_PAD = NSA_HEADS * LANES
QM_PAD = MEM_HEADS * LANES
C_U = 0
C_Q = C_U + POOL_WIDTH
C_KV = C_Q + Q_PAD
C_QM = C_KV + NSA_KV_WIDTH
C_G = C_QM + QM_PAD
C_END = C_G + LANES

NEG = -1e30
VMEM_LIMIT = 48 * 1024 * 1024

_NT = (((1,), (1,)), ((), ()))


def _cparams(*sem):
    return pltpu.CompilerParams(dimension_semantics=tuple(sem), vmem_limit_bytes=VMEM_LIMIT)


def _dot(a, b):
    return jnp.dot(a, b, preferred_element_type=F32)


def _dot_nt(a, b):
    return lax.dot_general(a, b, _NT, preferred_element_type=F32)


def _rms(x):
    return x * lax.rsqrt(jnp.mean(x * x, axis=-1, keepdims=True) + EPS)


def _masked_softmax(s, mask):
    sm = jnp.where(mask, s, NEG)
    m = jnp.max(sm, axis=-1, keepdims=True)
    e = jnp.where(mask, jnp.exp(sm - m), 0.0)
    return e / jnp.maximum(jnp.sum(e, axis=-1, keepdims=True), 1e-30)


def _split3(x):
    hi = x.astype(BF16)
    r1 = x - hi.astype(F32)
    mid = r1.astype(BF16)
    lo = (r1 - mid.astype(F32)).astype(BF16)
    return hi, mid, lo


def _proj_kernel(x_ref, ga_ref, w_ref, gq_ref, gk_ref, gmq_ref, inv_ref, seg_ref,
                 u_ref, q_ref, rows4_ref, rowsw_ref, kvb_ref, gates_ref, qm_ref, *maybe_rows4t_ref,
                 tm, seq_len, pos0):
    i = pl.program_id(0)
    a = _rms(x_ref[...]) * ga_ref[...]
    z = _dot(a.astype(BF16), w_ref[...])
    u_ref[...] = z[:, C_U:C_U + POOL_WIDTH]

    row = i * tm + lax.broadcasted_iota(I32, (tm, 1), 0)
    pos = (pos0 + row % seq_len).astype(F32)
    ang = pos * inv_ref[...]
    d = lax.broadcasted_iota(I32, (1, LANES), 1) % HEAD_DIM
    cos = jnp.cos(ang)
    sin = jnp.sin(ang)
    half = ROPE_DIM // 2
    s_next = jnp.where(d < half, -sin, 0.0)
    s_prev = jnp.where((d >= half) & (d < ROPE_DIM), sin, 0.0)
    seg = seg_ref[...]

    def head_norm(xc, g):
        ssq = _dot((xc * xc).astype(BF16), seg)
        return xc * lax.rsqrt(ssq * (1.0 / HEAD_DIM) + EPS) * g

    def rope(xc):
        return xc * cos + pltpu.roll(xc, LANES - half, 1) * s_next + pltpu.roll(xc, half, 1) * s_prev

    scale = HEAD_DIM ** -0.5
    for c in range(NSA_HEADS):
        sl = slice(c * LANES, (c + 1) * LANES)
        qc = rope(head_norm(z[:, C_Q + c * LANES:C_Q + (c + 1) * LANES], gq_ref[:, sl]))
        q_ref[:, sl] = (qc * scale).astype(BF16)
    for br in range(N_BRANCH):
        k0 = C_KV + br * 2 * LANES
        kn = rope(head_norm(z[:, k0:k0 + LANES], gk_ref[:, br * LANES:(br + 1) * LANES]))
        vv = z[:, k0 + LANES:k0 + 2 * LANES]
        kvb_ref[:, br * 2 * LANES:br * 2 * LANES + LANES] = kn.astype(BF16)
        kvb_ref[:, br * 2 * LANES + LANES:(br + 1) * 2 * LANES] = vv.astype(BF16)
        if br < 2:
            rows4_ref[:, br * 2 * LANES:br * 2 * LANES + LANES] = kn
            rows4_ref[:, br * 2 * LANES + LANES:(br + 1) * 2 * LANES] = vv
            for rows4t_ref in maybe_rows4t_ref:
                rows4t_ref[0, br * 2 * LANES:br * 2 * LANES + LANES, :] = kn.T
                rows4t_ref[0, br * 2 * LANES + LANES:(br + 1) * 2 * LANES, :] = vv.T
        else:
            rowsw_ref[:, :LANES] = kn
            rowsw_ref[:, LANES:] = vv
    for c in range(MEM_HEADS):
        sl = slice(c * LANES, (c + 1) * LANES)
        qmc = head_norm(z[:, C_QM + c * LANES:C_QM + (c + 1) * LANES], gmq_ref[:, sl])
        qm_ref[:, sl] = (qmc * scale).astype(BF16)
    gates_ref[...] = jax.nn.sigmoid(z[:, C_G:C_END])


def _proj_call(x2d, pw, *, seq_len, pos0, tm, feature_major_rows4=False):
    n, dm = x2d.shape
    full = lambda shape: pl.BlockSpec(shape, lambda i: (0,) * len(shape))
    rows = lambda w: pl.BlockSpec((tm, w), lambda i: (i, 0))
    outs = [(POOL_WIDTH, F32), (Q_PAD, BF16), (4 * LANES, F32), (2 * LANES, F32), (NSA_KV_WIDTH, BF16),
            (LANES, F32), (QM_PAD, BF16)]
    out_specs = [rows(w) for w, _ in outs]
    out_shape = [jax.ShapeDtypeStruct((n, w), dt) for w, dt in outs]
    if feature_major_rows4:
        per_seq = seq_len // tm
        out_specs.append(pl.BlockSpec((1, 4 * LANES, tm), lambda i: (i // per_seq, 0, i % per_seq)))
        out_shape.append(jax.ShapeDtypeStruct((n // seq_len, 4 * LANES, seq_len), F32))
    return pl.pallas_call(
        functools.partial(_proj_kernel, tm=tm, seq_len=seq_len, pos0=pos0),
        grid=(n // tm,),
        in_specs=[rows(dm), full((1, dm)), full((dm, C_END)), full((1, Q_PAD)), full((1, N_BRANCH * LANES)),
                  full((1, QM_PAD)), full((1, LANES)), full((LANES, LANES))],
        out_specs=out_specs,
        out_shape=out_shape,
        compiler_params=_cparams("parallel"),
        name="proj",
    )(x2d, pw["g_attn"], pw["w_in"], pw["gq"], pw["gk"], pw["gmq"], pw["inv"], pw["seg"])


def _pool_kernel(u_ref, halo_ref, w_ref, s_ref, y_ref, *, tp, pos0):
    i = pl.program_id(1)
    u = u_ref[0]
    halo = halo_ref[0] * (i > 0).astype(F32)
    n_h = POOL_STATE + 1
    ext = jnp.concatenate([halo, u], axis=0)
    sums = {1: ext}
    w = 1
    while w < max(POOL_WINDOWS):
        sums[2 * w] = sums[w] + pltpu.roll(sums[w], w, 0)
        w *= 2
    pos = pos0 + i * tp + lax.broadcasted_iota(I32, (tp, 1), 0)
    lane_grp = lax.broadcasted_iota(I32, (1, POOL_WIDTH), 1) // (POOL_WIDTH // len(POOL_WINDOWS))
    mean = jnp.zeros((tp, POOL_WIDTH), F32)
    for gi, wdw in enumerate(POOL_WINDOWS):
        cnt = jnp.minimum(pos + 1, wdw).astype(F32)
        mean = jnp.where(lane_grp == gi, sums[wdw][n_h:] / cnt, mean)
    r = mean - u
    y_ref[0] = _dot(r.astype(BF16), w_ref[...]) * s_ref[...]


def _pool_call(u3, w_bd, s_pool, *, tp, pos0):
    b, t, c = u3.shape
    n_h = POOL_STATE + 1
    return pl.pallas_call(
        functools.partial(_pool_kernel, tp=tp, pos0=pos0),
        grid=(b, t // tp),
        in_specs=[pl.BlockSpec((1, tp, c), lambda bi, i: (bi, i, 0)),
                  pl.BlockSpec((1, n_h, c), lambda bi, i: (bi, jnp.maximum(i * (tp // n_h) - 1, 0), 0)),
                  pl.BlockSpec((c, c), lambda bi, i: (0, 0)),
                  pl.BlockSpec((1, c), lambda bi, i: (0, 0))],
        out_specs=pl.BlockSpec((1, tp, c), lambda bi, i: (bi, i, 0)),
        out_shape=jax.ShapeDtypeStruct((b, t, c), F32),
        compiler_params=_cparams("parallel", "parallel"),
        name="pool",
    )(u3, u3, w_bd, s_pool)


def _compress_core(load_rows, tail_ref, pe_ref, w_ref, w2_ref, n):
    rowid = lax.broadcasted_iota(I32, (n, 1), 0)
    outs = []
    for c in range(2):
        w = w_ref[c]
        pe8 = jnp.broadcast_to(pe_ref[c], (8, 2 * CMP_STRIDE * HEAD_DIM)).astype(BF16)
        pe_first = _dot(pe8[:, :CMP_STRIDE * HEAD_DIM], w)[0:1, :CMP_HIDDEN]
        pe_second = _dot(pe8[:, CMP_STRIDE * HEAD_DIM:], w)[0:1, CMP_HIDDEN:]
        pe_const = pe_first + pe_second
        xr = jnp.transpose(load_rows(c).reshape(n, CMP_STRIDE, LANES), (1, 0, 2))
        lane = lax.broadcasted_iota(I32, (1, LANES), 1)
        per_group = [[], []]
        for q in range(CMP_STRIDE // 2):
            a, b = xr[2 * q], xr[2 * q + 1]
            per_group[0].append(jnp.where(lane < HEAD_DIM, a, pltpu.roll(b, HEAD_DIM, 1)))
            per_group[1].append(jnp.where(lane < HEAD_DIM, pltpu.roll(a, HEAD_DIM, 1), b))
        x2 = jnp.concatenate([jnp.concatenate(pg, axis=1) for pg in per_group], axis=0).astype(BF16)
        z = _dot(x2, w)
        hs = []
        for g in range(NSA_KV_HEADS):
            tail8 = jnp.broadcast_to(tail_ref[0, 2 * c + g:2 * c + g + 1, :], (8, CMP_STRIDE * HEAD_DIM)).astype(BF16)
            second_tail = _dot(tail8, w)[0:1, CMP_HIDDEN:]
            first = z[g * n:(g + 1) * n, :CMP_HIDDEN]
            second = z[g * n:(g + 1) * n, CMP_HIDDEN:]
            shifted = pltpu.roll(second, n - 1, 0)
            h = first + jnp.where(rowid == n - 1, second_tail, shifted) + pe_const
            hs.append(jax.nn.gelu(h, approximate=True))
        o = _dot(jnp.concatenate(hs, axis=0).astype(BF16), w2_ref[c])
        outs += [o[:n], o[n:]]
    return jnp.concatenate(outs, axis=1)


def _compress_prompt_kernel(k_ref, v_ref, tail_ref, pe_ref, w_ref, w2_ref, out_ref, *, n):
    load = lambda c: (k_ref, v_ref)[c][0]
    out_ref[0] = _compress_core(load, tail_ref, pe_ref, w_ref, w2_ref, n).astype(BF16)


def _compress_prompt_call(rows4_3d, tail, cw):
    b, t, _ = rows4_3d.shape
    n = t // CMP_STRIDE
    full = lambda a: pl.BlockSpec(a.shape, lambda bi: (0,) * a.ndim)
    return pl.pallas_call(
        functools.partial(_compress_prompt_kernel, n=n),
        grid=(b,),
        in_specs=[pl.BlockSpec((1, t, LANES), lambda bi: (bi, 0, 0)),
                  pl.BlockSpec((1, t, LANES), lambda bi: (bi, 0, 1)),
                  pl.BlockSpec((1,) + tail.shape[1:], lambda bi: (bi, 0, 0)),
                  full(cw["pe"]), full(cw["w"]), full(cw["w2"])],
        out_specs=pl.BlockSpec((1, n, 2 * LANES), lambda bi: (bi, 0, 0)),
        out_shape=jax.ShapeDtypeStruct((b, n, 2 * LANES), BF16),
        compiler_params=_cparams("parallel"),
        name="compress_prompt",
    )(rows4_3d, rows4_3d, tail, cw["pe"], cw["w"], cw["w2"])


def _compress_sample_kernel(pt_ref, cache_ref, tail_ref, pe_ref, w_ref, w2_ref, out_ref,
                            buf, rows, sem, *, n, n_pages, page):
    b = pl.program_id(0)
    nb = pl.num_programs(0)

    def page_copy(bb, slot, p, c):
        return pltpu.make_async_copy(cache_ref.at[pt_ref[bb * n_pages + p], c], buf.at[slot, c, p], sem.at[slot])

    def for_all(fn):
        for p in range(n_pages):
            for c in range(2):
                fn(p, c)

    def issue(bb, slot):
        for_all(lambda p, c: page_copy(bb, slot, p, c).start())

    @pl.when(b == 0)
    def _():
        issue(0, 0)

    @pl.when(b + 1 < nb)
    def _():
        issue(b + 1, (b + 1) % 2)

    slot = b % 2
    for_all(lambda p, c: page_copy(b, slot, p, c).wait())

    unroll = next(u for u in (8, 4, 2, 1) if n_pages % u == 0)

    def to_rows(pp, carry):
        for k in range(unroll):
            p = pp * unroll + k
            for c in range(2):
                rows[c, pl.ds(pl.multiple_of(p * page, page), page), :] = buf[slot, c, p].T
        return carry

    lax.fori_loop(0, n_pages // unroll, to_rows, 0)
    out_ref[0] = _compress_core(lambda c: rows[c], tail_ref, pe_ref, w_ref, w2_ref, n).astype(BF16)


def _compress_sample_call(page_table, cache_t, tail, cw):
    b, n_pages = page_table.shape
    page = cache_t.shape[3]
    n = n_pages * page // CMP_STRIDE
    full = lambda a: pl.BlockSpec(a.shape, lambda bi, pt: (0,) * a.ndim)
    grid_spec = pltpu.PrefetchScalarGridSpec(
        num_scalar_prefetch=1,
        grid=(b,),
        in_specs=[pl.BlockSpec(memory_space=pl.ANY),
                  pl.BlockSpec((1,) + tail.shape[1:], lambda bi, pt: (bi, 0, 0)),
                  full(cw["pe"]), full(cw["w"]), full(cw["w2"])],
        out_specs=pl.BlockSpec((1, n, 2 * LANES), lambda bi, pt: (bi, 0, 0)),
        scratch_shapes=[pltpu.VMEM((2, 2, n_pages, LANES, page), F32), pltpu.VMEM((2, n_pages * page, LANES), F32),
                        pltpu.SemaphoreType.DMA((2,))],
    )
    return pl.pallas_call(
        functools.partial(_compress_sample_kernel, n=n, n_pages=n_pages, page=page),
        grid_spec=grid_spec,
        out_shape=jax.ShapeDtypeStruct((b, n, 2 * LANES), BF16),
        compiler_params=_cparams("arbitrary"),
        name="compress_sample",
    )(page_table.reshape(-1), cache_t, tail, cw["pe"], cw["w"], cw["w2"])


def _topk_mask(score, ids, n_top, axis):
    sel = jnp.zeros(score.shape, jnp.bool_)
    work = score
    firsts = []
    big = float(score.shape[axis])
    for _ in range(n_top):
        m = jnp.max(work, axis=axis, keepdims=True)
        first = jnp.min(jnp.where(work == m, ids, big), axis=axis, keepdims=True)
        pick = (ids == first) & (m > -jnp.inf)
        sel = sel | pick
        work = jnp.where(pick, -jnp.inf, work)
        firsts.append(jnp.where(m > -jnp.inf, first, -1.0))
    return sel, firsts


def _gate_and_pack(o_ref_store, gates, o_c, o_s, o_w, rows):
    lane = lax.broadcasted_iota(I32, (1, LANES), 1)
    heads = []
    for h in range(NSA_HEADS):
        g, r = divmod(h, NSA_REP)
        rs = slice(r * rows, (r + 1) * rows)
        gc, gs, gw = (gates[:, N_BRANCH * h + k:N_BRANCH * h + k + 1] for k in range(N_BRANCH))
        heads.append(gc * o_c[g][rs] + gs * o_s[g][rs] + gw * o_w[g][rs])
    for j in range(NSA_HEADS // 2):
        a, b = heads[2 * j], heads[2 * j + 1]
        if (2 * j) // NSA_REP == 0:
            chunk = jnp.where(lane < HEAD_DIM, a, pltpu.roll(b, HEAD_DIM, 1))
        else:
            chunk = jnp.where(lane < HEAD_DIM, pltpu.roll(a, HEAD_DIM, 1), b)
        o_ref_store(j, chunk)


def _nsa_prompt_kernel(q_ref, gate_ref, kc_ref, kv_ref, poolt_ref, e_ref, o_ref, acc_ref, *, tq, tk, t_len):
    i = pl.program_id(1)
    s0 = i * tq
    rows4 = NSA_REP * tq
    tpos = s0 + lax.broadcasted_iota(I32, (tq, 1), 0)
    rep = lambda x: jnp.concatenate([x] * NSA_REP, axis=0)
    n_cmp = kc_ref.shape[1]
    kc128 = kc_ref[0, :, :LANES]
    vc128 = kc_ref[0, :, LANES:]
    kc_end = lax.broadcasted_iota(I32, (1, n_cmp), 1) * CMP_STRIDE + (CMP_BLOCK - 1)
    bias_c = rep(jnp.where(kc_end <= tpos, 0.0, NEG))
    any_c = rep((tpos >= CMP_BLOCK - 1).astype(F32))
    n_sel = t_len // SEL_BLOCK
    blk = lax.broadcasted_iota(I32, (n_sel, 1), 0)
    blk_f = blk.astype(F32)
    tq_lane = s0 + lax.broadcasted_iota(I32, (1, tq), 1)
    cur = tq_lane // SEL_BLOCK
    valid = blk * SEL_BLOCK <= tq_lane
    forced = (blk == 0) | (blk == cur) | (blk == cur - 1)
    w_start = pl.multiple_of(jnp.maximum(s0 - WINDOW, 0), LANES)
    w_len = WINDOW + tq
    kpos_w = w_start + lax.broadcasted_iota(I32, (1, w_len), 1)
    dist = tpos - kpos_w
    bias_w = rep(jnp.where((dist >= 0) & (dist < WINDOW), 0.0, NEG))
    c_diag = s0 // tk
    kpos_d = c_diag * tk + lax.broadcasted_iota(I32, (1, tk), 1)
    bias_causal = jnp.where(kpos_d <= tpos, 0.0, NEG)

    o_c, o_s, o_w, qgs, selbs = [], [], [], [], []
    for g in range(NSA_KV_HEADS):
        qg = jnp.concatenate([q_ref[0, :, (g * NSA_REP + r) * LANES:(g * NSA_REP + r + 1) * LANES]
                              for r in range(NSA_REP)], axis=0)
        qgs.append(qg)
        s_c = _dot_nt(qg, kc128) + bias_c
        e_c = jnp.exp(s_c - jnp.max(s_c, axis=-1, keepdims=True))
        p_c = e_c * (any_c / jnp.sum(e_c, axis=-1, keepdims=True))
        o_c.append(_dot(p_c.astype(BF16), vc128))
        imp = p_c[0:tq]
        for r in range(1, NSA_REP):
            imp = imp + p_c[r * tq:(r + 1) * tq]
        imp_t = sum(_dot_nt(poolt_ref[...], piece) for piece in _split3(imp))
        score = jnp.where(valid, imp_t + jnp.where(forced, FORCE_BONUS, 0.0), -jnp.inf)
        sel_t, _ = _topk_mask(score, blk_f, min(SEL_TOPK, n_sel), 0)
        selb_t = jnp.concatenate([jnp.where(sel_t, 0.0, NEG), jnp.full(((-n_sel) % LANES, tq), NEG, F32)], axis=0)
        selbs.append(selb_t.T.astype(BF16))

        kw = kv_ref[0, pl.ds(w_start, w_len), 4 * LANES:5 * LANES]
        vw = kv_ref[0, pl.ds(w_start, w_len), 5 * LANES:6 * LANES]
        s_w = _dot_nt(qg, kw) + bias_w
        e_w = jnp.exp(s_w - jnp.max(s_w, axis=-1, keepdims=True))
        o_w.append(_dot(e_w.astype(BF16), vw) * (1.0 / jnp.sum(e_w, axis=-1, keepdims=True)))

    acc_ref[...] = jnp.zeros(acc_ref.shape, F32)

    def chunk_step(c, carry, extra_bias):
        k0 = pl.multiple_of(c * tk, tk)
        ks = kv_ref[0, pl.ds(k0, tk), 2 * LANES:3 * LANES]
        vs = kv_ref[0, pl.ds(k0, tk), 3 * LANES:4 * LANES]
        out = []
        for g in range(NSA_KV_HEADS):
            m, l = carry[g]
            bias = _dot(selbs[g], e_ref[c])
            if extra_bias is not None:
                bias = bias + extra_bias
            s = _dot_nt(qgs[g], ks) + rep(bias)
            m_new = jnp.maximum(m, jnp.max(s, axis=-1, keepdims=True))
            alpha = jnp.exp(m - m_new)
            p = jnp.exp(s - m_new)
            l_new = alpha * l + jnp.sum(p, axis=-1, keepdims=True)
            acc_ref[g] = alpha * acc_ref[g] + _dot(p.astype(BF16), vs)
            out.append((m_new, l_new))
        return tuple(out)

    init = tuple((jnp.full((rows4, 1), NEG, F32), jnp.zeros((rows4, 1), F32)) for _ in range(NSA_KV_HEADS))
    carry = lax.fori_loop(0, c_diag, lambda c, cr: chunk_step(c, cr, None), init)
    carry = chunk_step(c_diag, carry, bias_causal)
    for g in range(NSA_KV_HEADS):
        o_s.append(acc_ref[g] * (1.0 / jnp.maximum(carry[g][1], 1e-30)))

    def store(j, chunk):
        o_ref[0, :, j * LANES:(j + 1) * LANES] = chunk

    _gate_and_pack(store, gate_ref[0], o_c, o_s, o_w, tq)


def _nsa_prompt_call(q3, gates3, kcvc, kvb3, poolt, e_mat, *, tq, tk):
    b, t, _ = q3.shape
    n_cmp = kcvc.shape[1]
    return pl.pallas_call(
        functools.partial(_nsa_prompt_kernel, tq=tq, tk=tk, t_len=t),
        grid=(b, t // tq),
        in_specs=[pl.BlockSpec((1, tq, Q_PAD), lambda bi, i: (bi, i, 0)),
                  pl.BlockSpec((1, tq, LANES), lambda bi, i: (bi, i, 0)),
                  pl.BlockSpec((1, n_cmp, 2 * LANES), lambda bi, i: (bi, 0, 0)),
                  pl.BlockSpec((1, t, NSA_KV_WIDTH), lambda bi, i: (bi, 0, 0)),
                  pl.BlockSpec(poolt.shape, lambda bi, i: (0, 0)),
                  pl.BlockSpec(e_mat.shape, lambda bi, i: (0, 0, 0))],
        out_specs=pl.BlockSpec((1, tq, NSA_WIDTH), lambda bi, i: (bi, i, 0)),
        out_shape=jax.ShapeDtypeStruct((b, t, NSA_WIDTH), F32),
        scratch_shapes=[pltpu.VMEM((NSA_KV_HEADS, NSA_REP * tq, LANES), F32)],
        compiler_params=_cparams("parallel", "arbitrary"),
        name="nsa_prompt",
    )(q3, gates3, kcvc, kvb3, poolt, e_mat)


def _memkv_kernel(x_ref, g_ref, w_ref, gk_ref, seg_ref, o_ref, ob_ref):
    a = _rms(x_ref[...]) * g_ref[...]
    z = _dot(a.astype(BF16), w_ref[...])
    for c in range(MEM_WIDTH // LANES):
        sl = slice(c * LANES, (c + 1) * LANES)
        xc = z[:, sl]
        ssq = _dot((xc * xc).astype(BF16), seg_ref[...])
        kn = xc * lax.rsqrt(ssq * (1.0 / HEAD_DIM) + EPS) * gk_ref[:, sl]
        o_ref[:, sl] = kn
        ob_ref[:, sl] = kn.astype(BF16)
    o_ref[:, MEM_WIDTH:] = z[:, MEM_WIDTH:]
    ob_ref[:, MEM_WIDTH:] = z[:, MEM_WIDTH:].astype(BF16)


def _memkv_call(mem2d, g_mem, w_mem, gmk, seg, *, tm):
    n, dm = mem2d.shape
    full = lambda shape: pl.BlockSpec(shape, lambda i: (0,) * len(shape))
    return pl.pallas_call(
        _memkv_kernel,
        grid=(n // tm,),
        in_specs=[pl.BlockSpec((tm, dm), lambda i: (i, 0)), full((1, dm)), full((dm, 2 * MEM_WIDTH)),
                  full((1, MEM_WIDTH)), full((LANES, LANES))],
        out_specs=[pl.BlockSpec((tm, 2 * MEM_WIDTH), lambda i: (i, 0))] * 2,
        out_shape=[jax.ShapeDtypeStruct((n, 2 * MEM_WIDTH), F32), jax.ShapeDtypeStruct((n, 2 * MEM_WIDTH), BF16)],
        compiler_params=_cparams("parallel"),
        name="memkv",
    )(mem2d, g_mem, w_mem, gmk, seg)


def _mem_attend(qm, mkv, rows):
    lane = lax.broadcasted_iota(I32, (1, LANES), 1)
    chunks = []
    for j in range(MEM_HEADS // 2):
        k128 = mkv[:, j * LANES:(j + 1) * LANES]
        v128 = mkv[:, MEM_WIDTH + j * LANES:MEM_WIDTH + (j + 1) * LANES]
        q2 = jnp.concatenate([qm[2 * j], qm[2 * j + 1]], axis=0)
        s = _dot_nt(q2, k128)
        e = jnp.exp(s - jnp.max(s, axis=-1, keepdims=True))
        p = e / jnp.sum(e, axis=-1, keepdims=True)
        o = _dot(p.astype(BF16), v128)
        chunks.append(jnp.where(lane < HEAD_DIM, o[:rows], o[rows:2 * rows]))
    return chunks


def _memattn_kernel(qm_ref, mkv_ref, o_ref, *, tq):
    qm = [qm_ref[0, :, h * LANES:(h + 1) * LANES] for h in range(MEM_HEADS)]
    for j, chunk in enumerate(_mem_attend(qm, mkv_ref[0], tq)):
        o_ref[0, :, j * LANES:(j + 1) * LANES] = chunk


def _memattn_call(qm3, mkv3, *, tq):
    b, t, _ = qm3.shape
    m = mkv3.shape[1]
    return pl.pallas_call(
        functools.partial(_memattn_kernel, tq=tq),
        grid=(b, t // tq),
        in_specs=[pl.BlockSpec((1, tq, QM_PAD), lambda bi, i: (bi, i, 0)),
                  pl.BlockSpec((1, m, 2 * MEM_WIDTH), lambda bi, i: (bi, 0, 0))],
        out_specs=pl.BlockSpec((1, tq, MEM_WIDTH), lambda bi, i: (bi, i, 0)),
        out_shape=jax.ShapeDtypeStruct((b, t, MEM_WIDTH), F32),
        compiler_params=_cparams("parallel", "parallel"),
        name="memattn",
    )(qm3, mkv3)


def _pad_rows(rows_list):
    x = jnp.concatenate(rows_list, axis=0)
    return jnp.concatenate([x, jnp.zeros((8 - x.shape[0], x.shape[1]), x.dtype)], axis=0)


def _sample_attn1_kernel(q_ref, kc_ref, win_ref, mkv_ref, qm_ref, pool_ref,
                         oc_ref, ow_ref, om_ref, score_ref, *, t_pos, n_win):
    q = q_ref[0].astype(F32)
    n_cmp = kc_ref.shape[1]
    kc128 = kc_ref[0, :, :LANES]
    vc128 = kc_ref[0, :, LANES:]
    kw128 = win_ref[0, :, :LANES].astype(BF16)
    vw128 = win_ref[0, :, LANES:].astype(BF16)
    kc_end = lax.broadcasted_iota(I32, (1, n_cmp), 1) * CMP_STRIDE + (CMP_BLOCK - 1)
    mask_c = kc_end <= t_pos
    kw_pos = t_pos - (n_win - 1) + lax.broadcasted_iota(I32, (1, n_win), 1)
    mask_w = (t_pos - kw_pos >= 0) & (t_pos - kw_pos < WINDOW) & (kw_pos >= 0)
    n_slot = pool_ref.shape[1]
    blk = lax.broadcasted_iota(I32, (1, n_slot), 1)
    cur = t_pos // SEL_BLOCK
    valid = blk * SEL_BLOCK <= t_pos
    forced = (blk == 0) | (blk == cur) | (blk == cur - 1)
    for g in range(NSA_KV_HEADS):
        qg = _pad_rows([q[:, (g * NSA_REP + r) * LANES:(g * NSA_REP + r + 1) * LANES]
                        for r in range(NSA_REP)]).astype(BF16)
        p_c = _masked_softmax(_dot_nt(qg, kc128), mask_c)
        oc_ref[0, g * NSA_REP:(g + 1) * NSA_REP, :] = _dot(p_c.astype(BF16), vc128)[:NSA_REP]
        imp = jnp.sum(p_c[:NSA_REP], axis=0, keepdims=True)
        imp8 = jnp.broadcast_to(imp, (8, n_cmp))
        imp_b = sum(_dot(piece, pool_ref[...]) for piece in _split3(imp8))[0:1]
        score_ref[0, g:g + 1, :] = jnp.where(valid, imp_b + jnp.where(forced, FORCE_BONUS, 0.0), -jnp.inf)
        p_w = _masked_softmax(_dot_nt(qg, kw128), mask_w)
        ow_ref[0, g * NSA_REP:(g + 1) * NSA_REP, :] = _dot(p_w.astype(BF16), vw128)[:NSA_REP]
    qm = qm_ref[0].astype(F32)
    qmh = []
    for h in range(MEM_HEADS):
        row = qm[:, h * LANES:(h + 1) * LANES]
        qmh.append(jnp.concatenate([row, jnp.zeros((7, LANES), F32)], axis=0).astype(BF16))
    for j, chunk in enumerate(_mem_attend(qmh, mkv_ref[0].astype(BF16), 8)):
        om_ref[0, :, j * LANES:(j + 1) * LANES] = chunk[0:1]


def _sample_attn1_call(q3, kcvc, win3, mkv3, qm3, pool_mat, *, t_pos):
    b = q3.shape[0]
    n_cmp = kcvc.shape[1]
    n_win = win3.shape[1]
    m = mkv3.shape[1]
    per_b = lambda shape: pl.BlockSpec((1,) + shape, lambda bi: (bi, 0, 0))
    return pl.pallas_call(
        functools.partial(_sample_attn1_kernel, t_pos=t_pos, n_win=n_win),
        grid=(b,),
        in_specs=[per_b((1, Q_PAD)), per_b((n_cmp, 2 * LANES)), per_b((n_win, 2 * LANES)),
                  per_b((m, 2 * MEM_WIDTH)), per_b((1, QM_PAD)),
                  pl.BlockSpec(pool_mat.shape, lambda bi: (0, 0))],
        out_specs=[per_b((NSA_HEADS, LANES)), per_b((NSA_HEADS, LANES)), per_b((1, MEM_WIDTH)),
                   per_b((NSA_KV_HEADS, pool_mat.shape[1]))],
        out_shape=[jax.ShapeDtypeStruct((b, NSA_HEADS, LANES), F32), jax.ShapeDtypeStruct((b, NSA_HEADS, LANES), F32),
                   jax.ShapeDtypeStruct((b, 1, MEM_WIDTH), F32),
                   jax.ShapeDtypeStruct((b, NSA_KV_HEADS, pool_mat.shape[1]), F32)],
        compiler_params=_cparams("parallel"),
        name="sample_attn1",
    )(q3, kcvc, win3, mkv3, qm3, pool_mat)


def _sample_topk_kernel(score_ref, idx_ref, *, n_top):
    score = score_ref[...]
    ids = lax.broadcasted_iota(I32, (1, score.shape[1]), 1).astype(F32)
    _, firsts = _topk_mask(score, ids, n_top, 1)
    lane = lax.broadcasted_iota(I32, (1, LANES), 1)
    idx = jnp.full((score.shape[0], LANES), -1, I32)
    for j, f in enumerate(firsts):
        idx = jnp.where(lane == j, f.astype(I32), idx)
    idx_ref[...] = idx


def _sample_topk_call(score2d, *, n_top):
    rows = score2d.shape[0]
    return pl.pallas_call(
        functools.partial(_sample_topk_kernel, n_top=n_top),
        out_shape=jax.ShapeDtypeStruct((rows, LANES), I32),
        compiler_params=pltpu.CompilerParams(vmem_limit_bytes=VMEM_LIMIT),
        name="sample_topk",
    )(score2d)


def _sample_sel_kernel(pt_ref, idx_ref, cache_ref, q_ref, knew_ref, gate_ref, oc_ref, ow_ref, o_ref,
                       buf, sem, *, t_pos, n_pages, n_top, page):
    b = pl.program_id(0)
    nb = pl.num_programs(0)
    blk_per_page = page // SEL_BLOCK
    n_past_blk = n_pages * blk_per_page
    idx_stride = NSA_KV_HEADS * LANES

    def blk_at(bb, g, j):
        return idx_ref[bb * idx_stride + g * LANES + j]

    def blk_copy(bb, slot, g, j, kv):
        blkc = jnp.clip(blk_at(bb, g, j), 0, n_past_blk - 1)
        pg = pt_ref[bb * n_pages + blkc // blk_per_page]
        return pltpu.make_async_copy(cache_ref.at[pg, 2 + kv], buf.at[slot, g, kv, j], sem.at[slot])

    def for_all(fn):
        for g in range(NSA_KV_HEADS):
            for j in range(n_top):
                for kv in range(2):
                    fn(g, j, kv)

    def issue(bb, slot):
        for_all(lambda g, j, kv: blk_copy(bb, slot, g, j, kv).start())

    @pl.when(b == 0)
    def _():
        issue(0, 0)

    @pl.when(b + 1 < nb)
    def _():
        issue(b + 1, (b + 1) % 2)

    slot = b % 2
    for_all(lambda g, j, kv: blk_copy(b, slot, g, j, kv).wait())

    q = q_ref[0].astype(F32)
    knew = knew_ref[0]
    k_new = knew[:, :LANES].astype(BF16).astype(F32)
    v_new = knew[:, LANES:].astype(BF16).astype(F32)
    n_keys = n_top * page
    key_lane = lax.broadcasted_iota(I32, (1, n_keys), 1)
    key_slot = key_lane // page
    key_row = key_lane % page
    cur_blk = t_pos // SEL_BLOCK
    o_s = []
    for g in range(NSA_KV_HEADS):
        qg = _pad_rows([q[:, (g * NSA_REP + r) * LANES:(g * NSA_REP + r + 1) * LANES] for r in range(NSA_REP)])
        blkvec = jnp.full((1, n_keys), -1, I32)
        has_cur = jnp.zeros((1, 1), jnp.bool_)
        for j in range(n_top):
            bj = blk_at(b, g, j)
            blkvec = jnp.where(key_slot == j, bj, blkvec)
            has_cur = has_cur | (bj == cur_blk)
        in_blk = key_row // SEL_BLOCK == blkvec % blk_per_page
        key_pos = (blkvec // blk_per_page) * page + key_row
        vis = (blkvec >= 0) & (blkvec < n_past_blk) & in_blk & (key_pos <= t_pos)
        kt = jnp.concatenate([buf[slot, g, 0, j] for j in range(n_top)], axis=1).astype(BF16)
        vt = jnp.concatenate([buf[slot, g, 1, j] for j in range(n_top)], axis=1).astype(BF16)
        s_past = jnp.where(vis, _dot(qg.astype(BF16), kt), NEG)
        s_new = jnp.where(has_cur, jnp.sum(qg * k_new, axis=-1, keepdims=True), NEG)
        m = jnp.maximum(jnp.max(s_past, axis=-1, keepdims=True), s_new)
        e_p = jnp.where(vis, jnp.exp(s_past - m), 0.0)
        e_n = jnp.where(has_cur, jnp.exp(s_new - m), 0.0)
        den = jnp.maximum(jnp.sum(e_p, axis=-1, keepdims=True) + e_n, 1e-30)
        num = _dot_nt(e_p.astype(BF16), vt) + e_n.astype(BF16).astype(F32) * v_new
        o_s.append(num / den)
    o_c = [jnp.concatenate([oc_ref[0, g * NSA_REP:(g + 1) * NSA_REP, :]] * 2, axis=0) for g in range(NSA_KV_HEADS)]
    o_w = [jnp.concatenate([ow_ref[0, g * NSA_REP:(g + 1) * NSA_REP, :]] * 2, axis=0) for g in range(NSA_KV_HEADS)]

    def store(j, chunk):
        o_ref[0, :, j * LANES:(j + 1) * LANES] = chunk

    _gate_and_pack(store, gate_ref[0], o_c, o_s, o_w, 1)


def _sample_sel_call(page_table, idx, cache_t, q3, knew3, gates3, oc, ow, *, t_pos, n_top):
    b, n_pages = page_table.shape
    page = cache_t.shape[3]
    per_b = lambda shape: pl.BlockSpec((1,) + shape, lambda bi, pt, ix: (bi, 0, 0))
    grid_spec = pltpu.PrefetchScalarGridSpec(
        num_scalar_prefetch=2,
        grid=(b,),
        in_specs=[pl.BlockSpec(memory_space=pl.ANY), per_b((1, Q_PAD)), per_b((1, 2 * LANES)), per_b((1, LANES)),
                  per_b((NSA_HEADS, LANES)), per_b((NSA_HEADS, LANES))],
        out_specs=per_b((1, NSA_WIDTH)),
        scratch_shapes=[pltpu.VMEM((2, NSA_KV_HEADS, 2, n_top, LANES, page), F32), pltpu.SemaphoreType.DMA((2,))],
    )
    return pl.pallas_call(
        functools.partial(_sample_sel_kernel, t_pos=t_pos, n_pages=n_pages, n_top=n_top, page=page),
        grid_spec=grid_spec,
        out_shape=jax.ShapeDtypeStruct((b, 1, NSA_WIDTH), F32),
        compiler_params=_cparams("arbitrary"),
        name="sample_sel",
    )(page_table.reshape(-1), idx.reshape(-1), cache_t, q3, knew3, gates3, oc, ow)


def _finish_kernel(yp_ref, on_ref, om_ref, x_ref, gm_ref, wo_ref, gf_ref, wr_ref, br_ref, tri_ref,
                   h_ref, hn_ref, route_ref, counts_ref, cnt_ref):
    gm = gm_ref[...]
    o1 = POOL_WIDTH
    o2 = o1 + NSA_WIDTH
    mixed = jnp.concatenate([_rms(yp_ref[...]) * gm[:, :o1], _rms(on_ref[...]) * gm[:, o1:o2],
                             _rms(om_ref[...]) * gm[:, o2:]], axis=-1)
    h = x_ref[...] + _dot(mixed.astype(BF16), wo_ref[...])
    h_ref[...] = h
    hn = _rms(h) * gf_ref[...]
    hn_ref[...] = hn
    logits = _dot(hn.astype(BF16), wr_ref[...]) + br_ref[...]
    lane = lax.broadcasted_iota(I32, (1, LANES), 1)
    lane_f = lane.astype(F32)
    is1 = lane < N_EXPERT_GROUPS
    m1 = jnp.max(jnp.where(is1, logits, -jnp.inf), axis=-1, keepdims=True)
    e1 = jnp.where(is1, jnp.exp(logits - m1), 0.0)
    p1 = e1 / jnp.sum(e1, axis=-1, keepdims=True)
    top1_p = jnp.max(p1, axis=-1, keepdims=True)
    grp = jnp.min(jnp.where((p1 == top1_p) & is1, lane_f, float(LANES)), axis=-1, keepdims=True)
    base = N_EXPERT_GROUPS + grp * EXPERTS_PER_GROUP
    in_g = (lane_f >= base) & (lane_f < base + EXPERTS_PER_GROUP)
    l2 = jnp.where(in_g, logits, -jnp.inf)
    v0 = jnp.max(l2, axis=-1, keepdims=True)
    i0 = jnp.min(jnp.where(l2 == v0, lane_f, float(LANES)), axis=-1, keepdims=True)
    l2b = jnp.where(lane_f == i0, -jnp.inf, l2)
    v1 = jnp.max(l2b, axis=-1, keepdims=True)
    i1 = jnp.min(jnp.where(l2b == v1, lane_f, float(LANES)), axis=-1, keepdims=True)
    ex = jnp.exp(v1 - v0)
    w0 = top1_p / (1.0 + ex)
    w1 = top1_p * ex / (1.0 + ex)
    ex0 = i0 - N_EXPERT_GROUPS
    ex1 = i1 - N_EXPERT_GROUPS
    is0 = lane_f == ex0
    is1e = lane_f == ex1
    oh0 = jnp.where(is0, 1.0, 0.0)
    oh1 = jnp.where(is1e, 1.0, 0.0)
    before0 = _dot(tri_ref[...], oh0.astype(BF16))
    before1 = _dot(tri_ref[...], oh1.astype(BF16))
    tot0 = jnp.sum(oh0, axis=0, keepdims=True)
    tot1 = jnp.sum(oh1, axis=0, keepdims=True)

    @pl.when(pl.program_id(0) == 0)
    def _():
        cnt_ref[...] = jnp.zeros(cnt_ref.shape, F32)

    seen = cnt_ref[...]
    rank0 = jnp.sum(jnp.where(is0, before0 + seen, 0.0), axis=-1, keepdims=True)
    rank1 = jnp.sum(jnp.where(is1e, before1 + seen + tot0, 0.0), axis=-1, keepdims=True)
    cnt_ref[...] = seen + tot0 + tot1
    counts_ref[...] = seen + tot0 + tot1
    route = jnp.where(lane == 0, ex0, jnp.where(lane == 1, ex1, jnp.where(lane == 2, w0, jnp.where(lane == 3, w1,
            jnp.where(lane == 4, rank0, jnp.where(lane == 5, rank1, 0.0))))))
    route_ref[...] = route


def _finish_call(yp, on, om, x2d, fw, *, tm):
    n, dm = x2d.shape
    full = lambda shape: pl.BlockSpec(shape, lambda i: (0,) * len(shape))
    rows = lambda w: pl.BlockSpec((tm, w), lambda i: (i, 0))
    tri = jnp.asarray(np.arange(tm)[None, :] < np.arange(tm)[:, None], BF16)
    return pl.pallas_call(
        _finish_kernel,
        grid=(n // tm,),
        in_specs=[rows(POOL_WIDTH), rows(NSA_WIDTH), rows(MEM_WIDTH), rows(dm), full((1, dm)), full((dm, dm)),
                  full((1, dm)), full((dm, LANES)), full((1, LANES)), full((tm, tm))],
        out_specs=[rows(dm), rows(dm), rows(LANES), full((1, LANES))],
        out_shape=[jax.ShapeDtypeStruct((n, dm), F32), jax.ShapeDtypeStruct((n, dm), F32),
                   jax.ShapeDtypeStruct((n, LANES), F32), jax.ShapeDtypeStruct((1, LANES), F32)],
        scratch_shapes=[pltpu.VMEM((1, LANES), F32)],
        compiler_params=_cparams("arbitrary"),
        name="finish",
    )(yp, on, om, x2d, fw["g_mix"], fw["w_out"], fw["g_ffn"], fw["w_r"], fw["b_r"], tri)


def _route_tables(route, counts, tm):
    n = route.shape[0]
    eid = route[:, 0:2].astype(I32)
    rank = route[:, 4:6].astype(I32)
    cnt = counts[0, :N_EXPERTS].astype(I32)
    tiles_per = (cnt + tm - 1) // tm
    tile_end = jnp.cumsum(tiles_per)
    tile_start = tile_end - tiles_per
    experts = jnp.arange(N_EXPERTS, dtype=I32)
    start_of = jnp.sum(jnp.where(eid[:, :, None] == experts, tile_start, 0), axis=-1)
    pos = (start_of * tm + rank).reshape(-1)
    n_tiles = (2 * n) // tm + N_EXPERTS
    tj = jnp.arange(n_tiles, dtype=I32)
    tile_expert = jnp.minimum(jnp.sum((tj[:, None] >= tile_end[None, :]).astype(I32), axis=1), N_EXPERTS - 1)
    sel = tile_expert[:, None] == experts
    rows_left = jnp.sum(jnp.where(sel, cnt, 0), axis=-1) - (tj - jnp.sum(jnp.where(sel, tile_start, 0), axis=-1)) * tm
    tile_nvalid = jnp.where(tj < tile_end[-1], jnp.clip(rows_left, 0, tm), 0).astype(I32)
    return tile_expert, tile_nvalid, pos


def _row_wait_all(src_row, dst_row, sem, count):
    for _ in range(count):
        pltpu.make_async_copy(src_row, dst_row, sem).wait()


def _dispatch_kernel(pos_ref, nv_ref, hn_ref, xs_ref, stage, zbuf, sem, zsem, *, tm, tme, n_tiles):
    i = pl.program_id(0)
    nt = pl.num_programs(0)
    slot = i % 2
    wait_slot = lambda s: _row_wait_all(stage.at[s, pl.ds(0, 1), :], xs_ref.at[pl.ds(0, 1), :], sem.at[s], 2 * tm)

    @pl.when(i == 0)
    def _():
        zbuf[...] = jnp.zeros(zbuf.shape, F32)

        def fill(j, c):
            @pl.when(nv_ref[j] < tme)
            def _():
                pltpu.make_async_copy(zbuf, xs_ref.at[pl.ds(pl.multiple_of(j * tme, tme), tme), :], zsem).start()
            return c

        def drain(j, c):
            @pl.when(nv_ref[j] < tme)
            def _():
                pltpu.make_async_copy(zbuf, xs_ref.at[pl.ds(0, tme), :], zsem).wait()
            return c

        lax.fori_loop(0, n_tiles, fill, 0)
        lax.fori_loop(0, n_tiles, drain, 0)

    @pl.when(i >= 2)
    def _():
        wait_slot(slot)

    stage[slot] = hn_ref[...]
    base = i * (2 * tm)
    for r in range(tm):
        for k in range(2):
            dst = pos_ref[base + 2 * r + k]
            pltpu.make_async_copy(stage.at[slot, pl.ds(r, 1), :], xs_ref.at[pl.ds(dst, 1), :], sem.at[slot]).start()

    @pl.when(i == nt - 1)
    def _():
        wait_slot(slot)

        @pl.when(i >= 1)
        def _():
            wait_slot(1 - slot)


def _dispatch_call(pos, tile_nvalid, hn, *, tm, tme):
    n, dm = hn.shape
    n_tiles = tile_nvalid.shape[0]
    grid_spec = pltpu.PrefetchScalarGridSpec(
        num_scalar_prefetch=2,
        grid=(n // tm,),
        in_specs=[pl.BlockSpec((tm, dm), lambda i, pos, nv: (i, 0))],
        out_specs=pl.BlockSpec(memory_space=pl.ANY),
        scratch_shapes=[pltpu.VMEM((2, tm, dm), F32), pltpu.VMEM((tme, dm), F32),
                        pltpu.SemaphoreType.DMA((2,)), pltpu.SemaphoreType.DMA(())],
    )
    return pl.pallas_call(
        functools.partial(_dispatch_kernel, tm=tm, tme=tme, n_tiles=n_tiles),
        grid_spec=grid_spec,
        out_shape=jax.ShapeDtypeStruct((n_tiles * tme, dm), F32),
        compiler_params=_cparams("arbitrary"),
        name="moe_dispatch",
    )(pos, tile_nvalid, hn)


def _expert_kernel(te_ref, nv_ref, x_ref, wg_ref, wu_ref, wd_ref, y_ref, wgb, wub, wdb, *, tm):
    i = pl.program_id(0)
    prev = te_ref[jnp.maximum(i - 1, 0)]

    @pl.when((i == 0) | (te_ref[i] != prev))
    def _():
        wgb[...] = wg_ref[0].astype(BF16)
        wub[...] = wu_ref[0].astype(BF16)
        wdb[...] = wd_ref[0].astype(BF16)

    nv = nv_ref[i]

    @pl.when(nv == 0)
    def _():
        y_ref[...] = jnp.zeros(y_ref.shape, F32)

    @pl.when(nv > 0)
    def _():
        x = x_ref[...].astype(BF16)
        hg = _dot(x, wgb[...])
        hu = _dot(x, wub[...])
        hmid = hg * jax.nn.sigmoid(hg) * hu
        y_ref[...] = _dot(hmid.astype(BF16), wdb[...])


def _expert_call(tile_expert, tile_nvalid, xs, wg, wu, wd, *, tm):
    n_tiles = tile_expert.shape[0]
    dm = xs.shape[1]
    ff = wg.shape[2]
    grid_spec = pltpu.PrefetchScalarGridSpec(
        num_scalar_prefetch=2,
        grid=(n_tiles,),
        in_specs=[pl.BlockSpec((tm, dm), lambda i, te, nv: (i, 0)),
                  pl.BlockSpec((1, dm, ff), lambda i, te, nv: (te[i], 0, 0)),
                  pl.BlockSpec((1, dm, ff), lambda i, te, nv: (te[i], 0, 0)),
                  pl.BlockSpec((1, ff, dm), lambda i, te, nv: (te[i], 0, 0))],
        out_specs=pl.BlockSpec((tm, dm), lambda i, te, nv: (i, 0)),
        scratch_shapes=[pltpu.VMEM((dm, ff), BF16), pltpu.VMEM((dm, ff), BF16), pltpu.VMEM((ff, dm), BF16)],
    )
    return pl.pallas_call(
        functools.partial(_expert_kernel, tm=tm),
        grid_spec=grid_spec,
        out_shape=jax.ShapeDtypeStruct(xs.shape, F32),
        compiler_params=_cparams("arbitrary"),
        name="moe_experts",
    )(tile_expert, tile_nvalid, xs, wg, wu, wd)


def _combine_kernel(pos_ref, ys_ref, h_ref, route_ref, o_ref, gbuf, sem, *, tm):
    i = pl.program_id(0)
    nt = pl.num_programs(0)
    slot = i % 2

    def issue(step, s):
        base = step * (2 * tm)
        for r in range(tm):
            for k in range(2):
                src = pos_ref[base + 2 * r + k]
                pltpu.make_async_copy(ys_ref.at[pl.ds(src, 1), :], gbuf.at[s, k, pl.ds(r, 1), :], sem.at[s]).start()

    @pl.when(i == 0)
    def _():
        issue(0, 0)

    @pl.when(i + 1 < nt)
    def _():
        issue(i + 1, 1 - slot)

    _row_wait_all(ys_ref.at[pl.ds(0, 1), :], gbuf.at[slot, 0, pl.ds(0, 1), :], sem.at[slot], 2 * tm)
    w0 = route_ref[:, 2:3]
    w1 = route_ref[:, 3:4]
    o_ref[...] = h_ref[...] + (w0 * gbuf[slot, 0] + w1 * gbuf[slot, 1])


def _combine_call(pos, ys, h, route, *, tm):
    n, dm = h.shape
    grid_spec = pltpu.PrefetchScalarGridSpec(
        num_scalar_prefetch=1,
        grid=(n // tm,),
        in_specs=[pl.BlockSpec(memory_space=pl.ANY),
                  pl.BlockSpec((tm, dm), lambda i, pos: (i, 0)),
                  pl.BlockSpec((tm, LANES), lambda i, pos: (i, 0))],
        out_specs=pl.BlockSpec((tm, dm), lambda i, pos: (i, 0)),
        scratch_shapes=[pltpu.VMEM((2, 2, tm, dm), F32), pltpu.SemaphoreType.DMA((2,))],
    )
    return pl.pallas_call(
        functools.partial(_combine_kernel, tm=tm),
        grid_spec=grid_spec,
        out_shape=jax.ShapeDtypeStruct((n, dm), F32),
        compiler_params=_cparams("arbitrary"),
        name="moe_combine",
    )(pos, ys, h, route)


def _moe_sample_kernel(hn_ref, route_ref, h_ref, wg_ref, wu_ref, wd_ref, o_ref, acc_ref):
    e = pl.program_id(0)

    @pl.when(e == 0)
    def _():
        acc_ref[...] = jnp.zeros(acc_ref.shape, F32)

    route = route_ref[...]
    ef = e.astype(F32)
    comb = jnp.where(route[:, 0:1] == ef, route[:, 2:3], 0.0) + jnp.where(route[:, 1:2] == ef, route[:, 3:4], 0.0)
    x = hn_ref[...].astype(BF16)
    hg = _dot(x, wg_ref[0].astype(BF16))
    hu = _dot(x, wu_ref[0].astype(BF16))
    hmid = hg * jax.nn.sigmoid(hg) * hu * comb
    acc_ref[...] += _dot(hmid.astype(BF16), wd_ref[0].astype(BF16))

    @pl.when(e == pl.num_programs(0) - 1)
    def _():
        o_ref[...] = h_ref[...] + acc_ref[...]


def _moe_sample_call(hn, route, h, wg, wu, wd):
    n, dm = hn.shape
    ff = wg.shape[2]
    full = lambda shape: pl.BlockSpec(shape, lambda e: (0,) * len(shape))
    return pl.pallas_call(
        _moe_sample_kernel,
        grid=(N_EXPERTS,),
        in_specs=[full((n, dm)), full((n, LANES)), full((n, dm)),
                  pl.BlockSpec((1, dm, ff), lambda e: (e, 0, 0)), pl.BlockSpec((1, dm, ff), lambda e: (e, 0, 0)),
                  pl.BlockSpec((1, ff, dm), lambda e: (e, 0, 0))],
        out_specs=full((n, dm)),
        out_shape=jax.ShapeDtypeStruct((n, dm), F32),
        scratch_shapes=[pltpu.VMEM((n, dm), F32)],
        compiler_params=_cparams("arbitrary"),
        name="moe_sample",
    )(hn, route, h, wg, wu, wd)


def _prep_weights(l, g_attn, w_in, g_q, g_k, pe_cmp, w_cmp1, w_cmp2, w_pool, s_pool, g_mem, w_mem_kv, g_mq, g_mk,
                  g_mix, w_out, g_ffn, w_router1, b_router1, w_router2, b_router2):
    dm = w_in.shape[1]
    w = w_in[l]
    o1 = POOL_WIDTH
    o2 = o1 + NSA_WIDTH
    o3 = o2 + NSA_KV_WIDTH
    o4 = o3 + GATE_WIDTH
    wq = w[:, o1:o2].reshape(dm, NSA_HEADS, 1, HEAD_DIM)
    q_slot = jnp.asarray(np.eye(NSA_KV_HEADS, dtype=np.float32)[np.arange(NSA_HEADS) // NSA_REP])
    wq_pad = (wq * q_slot[None, :, :, None]).reshape(dm, Q_PAD)
    wqm = w[:, o4:].reshape(dm, MEM_HEADS, 1, HEAD_DIM)
    m_slot = jnp.asarray(np.eye(2, dtype=np.float32)[np.arange(MEM_HEADS) % 2])
    wqm_pad = (wqm * m_slot[None, :, :, None]).reshape(dm, QM_PAD)
    wg_pad = jnp.pad(w[:, o3:o4], ((0, 0), (0, LANES - GATE_WIDTH)))
    w_packed = jnp.concatenate([w[:, :o1], wq_pad, w[:, o2:o3], wqm_pad, wg_pad], axis=1).astype(BF16)
    half = ROPE_DIM // 2
    inv = jnp.power(ROPE_THETA, -jnp.arange(half, dtype=F32) * 2.0 / ROPE_DIM)
    d = np.arange(LANES) % HEAD_DIM
    inv_lane = jnp.where(jnp.asarray(d < ROPE_DIM), inv[jnp.asarray(d % half)], 0.0).reshape(1, LANES)
    seg = jnp.asarray((np.arange(LANES)[:, None] // HEAD_DIM == np.arange(LANES)[None, :] // HEAD_DIM), BF16)
    gk = jnp.concatenate([jnp.tile(g_k[l, br], 2) for br in range(N_BRANCH)]).reshape(1, N_BRANCH * LANES)
    proj = dict(g_attn=g_attn[l].reshape(1, dm), w_in=w_packed, gq=jnp.tile(g_q[l], Q_PAD // HEAD_DIM).reshape(1, Q_PAD),
                gk=gk, gmq=jnp.tile(g_mq[l], QM_PAD // HEAD_DIM).reshape(1, QM_PAD), inv=inv_lane, seg=seg)

    n_grp = len(POOL_WINDOWS)
    pg = POOL_WIDTH // n_grp
    w_bd = (jnp.asarray(np.eye(n_grp, dtype=np.float32))[:, None, :, None] * w_pool[l][:, :, None, :]
            ).reshape(POOL_WIDTH, POOL_WIDTH).astype(BF16)
    pool = dict(w=w_bd, s=s_pool[l].reshape(1, POOL_WIDTH))

    half_rows = CMP_STRIDE * HEAD_DIM
    w1 = jnp.concatenate([w_cmp1[l][:, :half_rows], w_cmp1[l][:, half_rows:]], axis=2).astype(BF16)
    cmp_w = dict(w=w1, pe=pe_cmp[l].reshape(2, 1, CMP_BLOCK * HEAD_DIM), w2=w_cmp2[l].astype(BF16))

    mem = dict(g=g_mem[l].reshape(1, dm), w=w_mem_kv[l].astype(BF16),
               gk=jnp.tile(g_mk[l], MEM_HEADS).reshape(1, MEM_WIDTH))
    w_r = jnp.concatenate([w_router1[l], w_router2[l].reshape(dm, N_EXPERTS)], axis=1)
    w_r = jnp.pad(w_r, ((0, 0), (0, LANES - w_r.shape[1]))).astype(BF16)
    b_r = jnp.concatenate([b_router1[l], b_router2[l].reshape(-1)])
    b_r = jnp.pad(b_r, (0, LANES - b_r.shape[0])).reshape(1, LANES)
    fin = dict(g_mix=g_mix[l].reshape(1, dm), w_out=w_out[l].astype(BF16), g_ffn=g_ffn[l].reshape(1, dm),
               w_r=w_r, b_r=b_r)
    return proj, pool, cmp_w, mem, fin


def _pick(n, prefs):
    for p in prefs:
        if n % p == 0:
            return p
    return n


def kernel(x_prompt, x_sample, cache_kv, cache_win, state_pool, cache_mem_kv, page_table, mem_prompt, g_attn, w_in, g_q, g_k, pe_cmp, w_cmp1, w_cmp2, w_pool, s_pool, g_mem, w_mem_kv, g_mq, g_mk, g_mix, w_out, g_ffn, w_router1, b_router1, w_router2, b_router2, w_gate, w_up, w_down):
    depth = w_in.shape[0]
    bp, t, dm = x_prompt.shape
    bs, ts, _ = x_sample.shape
    n_pages = page_table.shape[1]
    page = cache_kv.shape[2]
    past_len = n_pages * page
    n_win = cache_win.shape[2]
    ff = w_gate.shape[-1]
    tq, tk = 256, 512
    assert ts == 1 and n_win == WINDOW and page % SEL_BLOCK == 0 and t % tk == 0 and t >= WINDOW + tq
    n_sel = t // SEL_BLOCK
    assert min(SEL_TOPK, n_sel) >= 3 and n_sel <= LANES
    poolt = jnp.asarray(np.arange(t // CMP_STRIDE)[None, :] // CMP_PER_SEL == np.arange(n_sel)[:, None], BF16)
    key_blk = np.arange(t // tk)[:, None, None] * (tk // SEL_BLOCK) + np.arange(tk)[None, None, :] // SEL_BLOCK
    e_mat = jnp.asarray(np.arange(LANES)[None, :, None] == key_blk, BF16)
    n_cmp_s = past_len // CMP_STRIDE
    n_slot_s = -(-(past_len // SEL_BLOCK + 1) // LANES) * LANES
    pool_s = jnp.asarray(np.arange(n_cmp_s)[:, None] // CMP_PER_SEL == np.arange(n_slot_s)[None, :], BF16)

    hp, hs = x_prompt, x_sample
    outs = [[] for _ in range(7)]
    for l in range(depth):
        proj_w, pool_w, cmp_w, mem_w, fin_w = _prep_weights(
            l, g_attn, w_in, g_q, g_k, pe_cmp, w_cmp1, w_cmp2, w_pool, s_pool, g_mem, w_mem_kv, g_mq, g_mk,
            g_mix, w_out, g_ffn, w_router1, b_router1, w_router2, b_router2)
        wg = w_gate[l].reshape(N_EXPERTS, dm, ff)
        wu = w_up[l].reshape(N_EXPERTS, dm, ff)
        wd = w_down[l].reshape(N_EXPERTS, ff, dm)
        kw_cols = CMP_STRIDE * 2 * LANES

        n = bp * t
        xp2 = hp.reshape(n, dm)
        u, qb, rows4, rowsw, kvb, gates, qmb, rows4t = _proj_call(
            xp2, proj_w, seq_len=t, pos0=0, tm=_pick(t, (512, 256, 128)), feature_major_rows4=True)
        u3 = u.reshape(bp, t, POOL_WIDTH)
        y_pool = _pool_call(u3, pool_w["w"], pool_w["s"], tp=_pick(t, (512, 256, 128)), pos0=0)
        n_slab = 2 * NSA_KV_HEADS
        chunk_w = CMP_STRIDE * HEAD_DIM
        kcvc = _compress_prompt_call(rows4.reshape(bp, t, 4 * LANES), jnp.zeros((bp, n_slab, chunk_w), F32), cmp_w)
        o_nsa = _nsa_prompt_call(qb.reshape(bp, t, Q_PAD), gates.reshape(bp, t, LANES), kcvc,
                                 kvb.reshape(bp, t, NSA_KV_WIDTH), poolt, e_mat, tq=tq, tk=tk)
        m_len = mem_prompt.shape[1]
        mkv, mkvb = _memkv_call(mem_prompt.reshape(bp * m_len, dm), mem_w["g"], mem_w["w"], mem_w["gk"],
                                proj_w["seg"], tm=_pick(bp * m_len, (256, 128)))
        o_mem = _memattn_call(qmb.reshape(bp, t, QM_PAD), mkvb.reshape(bp, m_len, 2 * MEM_WIDTH),
                              tq=_pick(t, (512, 256, 128)))
        tmf = _pick(n, (512, 256, 128))
        h_p, hn_p, route_p, counts_p = _finish_call(y_pool.reshape(n, POOL_WIDTH), o_nsa.reshape(n, NSA_WIDTH),
                                                    o_mem.reshape(n, MEM_WIDTH), xp2, fin_w, tm=tmf)
        tme = 256
        tmd = _pick(n, (256, 128))
        te, nv, pos = _route_tables(route_p, counts_p, tme)
        xs = _dispatch_call(pos, nv, hn_p, tm=tmd, tme=tme)
        ys = _expert_call(te, nv, xs, wg, wu, wd, tm=tme)
        y_p = _combine_call(pos, ys, h_p, route_p, tm=tmd).reshape(bp, t, dm)

        xs2 = hs.reshape(bs, dm)
        u_s, qb_s, rows4_s, rowsw_s, _, gates_s, qmb_s = _proj_call(xs2, proj_w, seq_len=1, pos0=past_len, tm=bs)
        ext = jnp.concatenate([state_pool[l], u_s[:, None, :]], axis=1)
        y_pool_s = _pool_call(ext, pool_w["w"], pool_w["s"], tp=POOL_STATE + 1, pos0=past_len - POOL_STATE)[:, -1, :]
        cache_t = jnp.transpose(cache_kv[l], (0, 2, 3, 4, 1)).reshape(
            cache_kv.shape[1], N_KV_SLOTS, NSA_KV_HEADS * HEAD_DIM, page)
        tail_s = jnp.pad(rows4_s[:, :2 * LANES].reshape(bs, n_slab, HEAD_DIM), ((0, 0), (0, 0), (0, chunk_w - HEAD_DIM)))
        kcvc_s = _compress_sample_call(page_table, cache_t, tail_s, cmp_w)
        win_s = jnp.concatenate([cache_win[l].reshape(bs, n_win, 2 * LANES)[:, 1:], rowsw_s[:, None, :]], axis=1)
        mkv_s = cache_mem_kv[l].reshape(bs, cache_mem_kv.shape[2], 2 * MEM_WIDTH)
        oc, ow, om_s, score_s = _sample_attn1_call(qb_s.reshape(bs, 1, Q_PAD), kcvc_s, win_s, mkv_s,
                                                   qmb_s.reshape(bs, 1, QM_PAD), pool_s, t_pos=past_len)
        n_top = min(SEL_TOPK, past_len // SEL_BLOCK + 1)
        idx = _sample_topk_call(score_s.reshape(bs * NSA_KV_HEADS, n_slot_s), n_top=n_top)
        o_nsa_s = _sample_sel_call(page_table, idx, cache_t, qb_s.reshape(bs, 1, Q_PAD),
                                   rows4_s[:, 2 * LANES:].reshape(bs, 1, 2 * LANES), gates_s.reshape(bs, 1, LANES),
                                   oc, ow, t_pos=past_len, n_top=n_top)
        h_s, hn_s, route_s, _ = _finish_call(y_pool_s, o_nsa_s.reshape(bs, NSA_WIDTH), om_s.reshape(bs, MEM_WIDTH),
                                          xs2, fin_w, tm=bs)
        y_s = _moe_sample_call(hn_s, route_s, h_s, wg, wu, wd).reshape(bs, 1, dm)

        keep = min(WINDOW, t)
        outs[0].append(jnp.transpose(rows4t.reshape(bp, N_KV_SLOTS, NSA_KV_HEADS, HEAD_DIM, t), (0, 4, 1, 2, 3)))
        outs[1].append(rows4_s.reshape(bs, 1, N_KV_SLOTS, NSA_KV_HEADS, HEAD_DIM))
        outs[2].append(rowsw.reshape(bp, t, 2, NSA_KV_HEADS, HEAD_DIM)[:, t - keep:])
        outs[3].append(win_s.reshape(bs, n_win, 2, NSA_KV_HEADS, HEAD_DIM))
        outs[4].append(u3[:, t - POOL_STATE:])
        outs[5].append(ext[:, 1:])
        outs[6].append(mkv.reshape(bp, m_len, 2, MEM_HEADS, HEAD_DIM))
        hp, hs = y_p, y_s
    return (hp, hs) + tuple(jnp.stack(o) for o in outs)
```

```python
import functools

import numpy as np
import jax
import jax.numpy as jnp
from jax import lax
from jax.experimental import pallas as pl
from jax.experimental.pallas import tpu as pltpu

F32 = jnp.float32
BF16 = jnp.bfloat16
I32 = jnp.int32

HEAD_DIM = 64
POOL_WINDOWS = (2, 4, 8, 16)
POOL_STATE = max(POOL_WINDOWS) - 1
NSA_HEADS = 8
NSA_KV_HEADS = 2
NSA_REP = NSA_HEADS // NSA_KV_HEADS
N_BRANCH = 3
CMP_BLOCK = 32
CMP_STRIDE = 16
CMP_HIDDEN = 2 * HEAD_DIM
SEL_BLOCK = 64
SEL_TOPK = 16
CMP_PER_SEL = SEL_BLOCK // CMP_STRIDE
WINDOW = 512
FORCE_BONUS = 1000.0
MEM_HEADS = 4
ROPE_DIM = HEAD_DIM // 4
ROPE_THETA = 500000.0
N_EXPERT_GROUPS = 4
EXPERTS_PER_GROUP = 8
N_EXPERTS = N_EXPERT_GROUPS * EXPERTS_PER_GROUP
EPS = 1e-6
N_KV_SLOTS = 4

LANES = 128
POOL_WIDTH = 256
NSA_WIDTH = NSA_HEADS * HEAD_DIM
NSA_KV_WIDTH = N_BRANCH * 2 * NSA_KV_HEADS * HEAD_DIM
GATE_WIDTH = NSA_HEADS * N_BRANCH
MEM_WIDTH = MEM_HEADS * HEAD_DIM
Q_PAD = NSA_HEADS * LANES
QM_PAD = MEM_HEADS * LANES
C_U = 0
C_Q = C_U + POOL_WIDTH
C_KV = C_Q + Q_PAD
C_QM = C_KV + NSA_KV_WIDTH
C_G = C_QM + QM_PAD
C_END = C_G + LANES

NEG = -1e30
VMEM_LIMIT = 48 * 1024 * 1024

_NT = (((1,), (1,)), ((), ()))


def _cparams(*sem):
    return pltpu.CompilerParams(dimension_semantics=tuple(sem), vmem_limit_bytes=VMEM_LIMIT)


def _dot(a, b):
    return jnp.dot(a, b, preferred_element_type=F32)


def _dot_nt(a, b):
    return lax.dot_general(a, b, _NT, preferred_element_type=F32)


def _rms(x):
    return x * lax.rsqrt(jnp.mean(x * x, axis=-1, keepdims=True) + EPS)


def _masked_softmax(s, mask):
    sm = jnp.where(mask, s, NEG)
    m = jnp.max(sm, axis=-1, keepdims=True)
    e = jnp.where(mask, jnp.exp(sm - m), 0.0)
    return e / jnp.maximum(jnp.sum(e, axis=-1, keepdims=True), 1e-30)


def _split3(x):
    hi = x.astype(BF16)
    r1 = x - hi.astype(F32)
    mid = r1.astype(BF16)
    lo = (r1 - mid.astype(F32)).astype(BF16)
    return hi, mid, lo


def _proj_kernel(x_ref, ga_ref, w_ref, gq_ref, gk_ref, gmq_ref, inv_ref, seg_ref,
                 u_ref, q_ref, rows4_ref, rowsw_ref, kvb_ref, gates_ref, qm_ref, *rest,
                 tm, seq_len, pos0, consecutive):
    *maybe_rows4t_ref, cos_in_tile, sin_in_tile = rest
    i = pl.program_id(0)
    a = _rms(x_ref[...]) * ga_ref[...]
    z = _dot(a.astype(BF16), w_ref[...])
    u_ref[...] = z[:, C_U:C_U + POOL_WIDTH]

    inv = inv_ref[...]

    @pl.when(i == 0)
    def _():
        j = lax.broadcasted_iota(I32, (tm, 1), 0).astype(F32) if consecutive else jnp.zeros((tm, 1), F32)
        cos_in_tile[...] = jnp.cos(j * inv)
        sin_in_tile[...] = jnp.sin(j * inv)

    first = (pos0 + (i * tm) % seq_len).astype(F32) * jnp.broadcast_to(inv, (8, LANES))
    cos_f, sin_f = jnp.cos(first)[0:1], jnp.sin(first)[0:1]
    cos = cos_f * cos_in_tile[...] - sin_f * sin_in_tile[...]
    sin = sin_f * cos_in_tile[...] + cos_f * sin_in_tile[...]
    d = lax.broadcasted_iota(I32, (1, LANES), 1) % HEAD_DIM
    half = ROPE_DIM // 2
    s_next = jnp.where(d < half, -sin, 0.0)
    s_prev = jnp.where((d >= half) & (d < ROPE_DIM), sin, 0.0)
    seg = seg_ref[...]

    def head_norm(xc, g):
        ssq = _dot((xc * xc).astype(BF16), seg)
        return xc * lax.rsqrt(ssq * (1.0 / HEAD_DIM) + EPS) * g

    def rope(xc):
        return xc * cos + pltpu.roll(xc, LANES - half, 1) * s_next + pltpu.roll(xc, half, 1) * s_prev

    scale = HEAD_DIM ** -0.5
    for c in range(NSA_HEADS):
        sl = slice(c * LANES, (c + 1) * LANES)
        qc = rope(head_norm(z[:, C_Q + c * LANES:C_Q + (c + 1) * LANES], gq_ref[:, sl]))
        q_ref[:, sl] = (qc * scale).astype(BF16)
    for br in range(N_BRANCH):
        k0 = C_KV + br * 2 * LANES
        kn = rope(head_norm(z[:, k0:k0 + LANES], gk_ref[:, br * LANES:(br + 1) * LANES]))
        vv = z[:, k0 + LANES:k0 + 2 * LANES]
        kvb_ref[:, br * 2 * LANES:br * 2 * LANES + LANES] = kn.astype(BF16)
        kvb_ref[:, br * 2 * LANES + LANES:(br + 1) * 2 * LANES] = vv.astype(BF16)
        if br < 2:
            rows4_ref[:, br * 2 * LANES:br * 2 * LANES + LANES] = kn
            rows4_ref[:, br * 2 * LANES + LANES:(br + 1) * 2 * LANES] = vv
            for rows4t_ref in maybe_rows4t_ref:
                rows4t_ref[0, br * 2 * LANES:br * 2 * LANES + LANES, :] = kn.T
                rows4t_ref[0, br * 2 * LANES + LANES:(br + 1) * 2 * LANES, :] = vv.T
        else:
            rowsw_ref[:, :LANES] = kn
            rowsw_ref[:, LANES:] = vv
    for c in range(MEM_HEADS):
        sl = slice(c * LANES, (c + 1) * LANES)
        qmc = head_norm(z[:, C_QM + c * LANES:C_QM + (c + 1) * LANES], gmq_ref[:, sl])
        qm_ref[:, sl] = (qmc * scale).astype(BF16)
    gates_ref[...] = jax.nn.sigmoid(z[:, C_G:C_END])


def _proj_call(x2d, pw, *, seq_len, pos0, tm, feature_major_rows4=False):
    n, dm = x2d.shape
    full = lambda shape: pl.BlockSpec(shape, lambda i: (0,) * len(shape))
    rows = lambda w: pl.BlockSpec((tm, w), lambda i: (i, 0))
    outs = [(POOL_WIDTH, F32), (Q_PAD, BF16), (4 * LANES, F32), (2 * LANES, F32), (NSA_KV_WIDTH, BF16),
            (LANES, F32), (QM_PAD, BF16)]
    out_specs = [rows(w) for w, _ in outs]
    out_shape = [jax.ShapeDtypeStruct((n, w), dt) for w, dt in outs]
    if feature_major_rows4:
        per_seq = seq_len // tm
        out_specs.append(pl.BlockSpec((1, 4 * LANES, tm), lambda i: (i // per_seq, 0, i % per_seq)))
        out_shape.append(jax.ShapeDtypeStruct((n // seq_len, 4 * LANES, seq_len), F32))
    consecutive = seq_len % tm == 0
    assert consecutive or seq_len == 1
    return pl.pallas_call(
        functools.partial(_proj_kernel, tm=tm, seq_len=seq_len, pos0=pos0, consecutive=consecutive),
        grid=(n // tm,),
        in_specs=[rows(dm), full((1, dm)), full((dm, C_END)), full((1, Q_PAD)), full((1, N_BRANCH * LANES)),
                  full((1, QM_PAD)), full((1, LANES)), full((LANES, LANES))],
        out_specs=out_specs,
        out_shape=out_shape,
        scratch_shapes=[pltpu.VMEM((tm, LANES), F32), pltpu.VMEM((tm, LANES), F32)],
        compiler_params=_cparams("arbitrary"),
        name="proj",
    )(x2d, pw["g_attn"], pw["w_in"], pw["gq"], pw["gk"], pw["gmq"], pw["inv"], pw["seg"])


def _pool_kernel(u_ref, halo_ref, w_ref, s_ref, y_ref, *, tp, pos0):
    i = pl.program_id(1)
    u = u_ref[0]
    halo = halo_ref[0] * (i > 0).astype(F32)
    n_h = POOL_STATE + 1
    ext = jnp.concatenate([halo, u], axis=0)
    sums = {1: ext}
    w = 1
    while w < max(POOL_WINDOWS):
        sums[2 * w] = sums[w] + pltpu.roll(sums[w], w, 0)
        w *= 2
    pos = pos0 + i * tp + lax.broadcasted_iota(I32, (tp, 1), 0)
    lane_grp = lax.broadcasted_iota(I32, (1, POOL_WIDTH), 1) // (POOL_WIDTH // len(POOL_WINDOWS))
    mean = jnp.zeros((tp, POOL_WIDTH), F32)
    for gi, wdw in enumerate(POOL_WINDOWS):
        cnt = jnp.minimum(pos + 1, wdw).astype(F32)
        mean = jnp.where(lane_grp == gi, sums[wdw][n_h:] / cnt, mean)
    r = mean - u
    y_ref[0] = _dot(r.astype(BF16), w_ref[...]) * s_ref[...]


def _pool_call(u3, w_bd, s_pool, *, tp, pos0):
    b, t, c = u3.shape
    n_h = POOL_STATE + 1
    return pl.pallas_call(
        functools.partial(_pool_kernel, tp=tp, pos0=pos0),
        grid=(b, t // tp),
        in_specs=[pl.BlockSpec((1, tp, c), lambda bi, i: (bi, i, 0)),
                  pl.BlockSpec((1, n_h, c), lambda bi, i: (bi, jnp.maximum(i * (tp // n_h) - 1, 0), 0)),
                  pl.BlockSpec((c, c), lambda bi, i: (0, 0)),
                  pl.BlockSpec((1, c), lambda bi, i: (0, 0))],
        out_specs=pl.BlockSpec((1, tp, c), lambda bi, i: (bi, i, 0)),
        out_shape=jax.ShapeDtypeStruct((b, t, c), F32),
        compiler_params=_cparams("parallel", "parallel"),
        name="pool",
    )(u3, u3, w_bd, s_pool)


def _compress_core(load_rows, tail_ref, pe_ref, w_ref, w2_ref, n):
    rowid = lax.broadcasted_iota(I32, (n, 1), 0)
    outs = []
    for c in range(2):
        w = w_ref[c]
        pe8 = jnp.broadcast_to(pe_ref[c], (8, 2 * CMP_STRIDE * HEAD_DIM)).astype(BF16)
        pe_first = _dot(pe8[:, :CMP_STRIDE * HEAD_DIM], w)[0:1, :CMP_HIDDEN]
        pe_second = _dot(pe8[:, CMP_STRIDE * HEAD_DIM:], w)[0:1, CMP_HIDDEN:]
        pe_const = pe_first + pe_second
        xr = jnp.transpose(load_rows(c).reshape(n, CMP_STRIDE, LANES), (1, 0, 2))
        lane = lax.broadcasted_iota(I32, (1, LANES), 1)
        per_group = [[], []]
        for q in range(CMP_STRIDE // 2):
            a, b = xr[2 * q], xr[2 * q + 1]
            per_group[0].append(jnp.where(lane < HEAD_DIM, a, pltpu.roll(b, HEAD_DIM, 1)))
            per_group[1].append(jnp.where(lane < HEAD_DIM, pltpu.roll(a, HEAD_DIM, 1), b))
        x2 = jnp.concatenate([jnp.concatenate(pg, axis=1) for pg in per_group], axis=0).astype(BF16)
        z = _dot(x2, w)
        hs = []
        for g in range(NSA_KV_HEADS):
            tail8 = jnp.broadcast_to(tail_ref[0, 2 * c + g:2 * c + g + 1, :], (8, CMP_STRIDE * HEAD_DIM)).astype(BF16)
            second_tail = _dot(tail8, w)[0:1, CMP_HIDDEN:]
            first = z[g * n:(g + 1) * n, :CMP_HIDDEN]
            second = z[g * n:(g + 1) * n, CMP_HIDDEN:]
            shifted = pltpu.roll(second, n - 1, 0)
            h = first + jnp.where(rowid == n - 1, second_tail, shifted) + pe_const
            hs.append(jax.nn.gelu(h, approximate=True))
        o = _dot(jnp.concatenate(hs, axis=0).astype(BF16), w2_ref[c])
        outs += [o[:n], o[n:]]
    return jnp.concatenate(outs, axis=1)


def _compress_prompt_kernel(k_ref, v_ref, tail_ref, pe_ref, w_ref, w2_ref, out_ref, *, n):
    load = lambda c: (k_ref, v_ref)[c][0]
    out_ref[0] = _compress_core(load, tail_ref, pe_ref, w_ref, w2_ref, n).astype(BF16)


def _compress_prompt_call(rows4_3d, tail, cw):
    b, t, _ = rows4_3d.shape
    n = t // CMP_STRIDE
    full = lambda a: pl.BlockSpec(a.shape, lambda bi: (0,) * a.ndim)
    return pl.pallas_call(
        functools.partial(_compress_prompt_kernel, n=n),
        grid=(b,),
        in_specs=[pl.BlockSpec((1, t, LANES), lambda bi: (bi, 0, 0)),
                  pl.BlockSpec((1, t, LANES), lambda bi: (bi, 0, 1)),
                  pl.BlockSpec((1,) + tail.shape[1:], lambda bi: (bi, 0, 0)),
                  full(cw["pe"]), full(cw["w"]), full(cw["w2"])],
        out_specs=pl.BlockSpec((1, n, 2 * LANES), lambda bi: (bi, 0, 0)),
        out_shape=jax.ShapeDtypeStruct((b, n, 2 * LANES), BF16),
        compiler_params=_cparams("parallel"),
        name="compress_prompt",
    )(rows4_3d, rows4_3d, tail, cw["pe"], cw["w"], cw["w2"])


def _compress_sample_kernel(pt_ref, cache_ref, tail_ref, pe_ref, w_ref, w2_ref, out_ref,
                            buf, rows, sem, *, n, n_pages, page):
    b = pl.program_id(0)
    nb = pl.num_programs(0)

    def page_copy(bb, slot, p, c):
        return pltpu.make_async_copy(cache_ref.at[pt_ref[bb * n_pages + p], c], buf.at[slot, c, p], sem.at[slot])

    def for_all(fn):
        for p in range(n_pages):
            for c in range(2):
                fn(p, c)

    def issue(bb, slot):
        for_all(lambda p, c: page_copy(bb, slot, p, c).start())

    @pl.when(b == 0)
    def _():
        issue(0, 0)

    @pl.when(b + 1 < nb)
    def _():
        issue(b + 1, (b + 1) % 2)

    slot = b % 2
    for_all(lambda p, c: page_copy(b, slot, p, c).wait())

    unroll = next(u for u in (8, 4, 2, 1) if n_pages % u == 0)

    def to_rows(pp, carry):
        for k in range(unroll):
            p = pp * unroll + k
            for c in range(2):
                rows[c, pl.ds(pl.multiple_of(p * page, page), page), :] = buf[slot, c, p].T
        return carry

    lax.fori_loop(0, n_pages // unroll, to_rows, 0)
    out_ref[0] = _compress_core(lambda c: rows[c], tail_ref, pe_ref, w_ref, w2_ref, n).astype(BF16)


def _compress_sample_call(page_table, cache_t, tail, cw):
    b, n_pages = page_table.shape
    page = cache_t.shape[3]
    n = n_pages * page // CMP_STRIDE
    full = lambda a: pl.BlockSpec(a.shape, lambda bi, pt: (0,) * a.ndim)
    grid_spec = pltpu.PrefetchScalarGridSpec(
        num_scalar_prefetch=1,
        grid=(b,),
        in_specs=[pl.BlockSpec(memory_space=pl.ANY),
                  pl.BlockSpec((1,) + tail.shape[1:], lambda bi, pt: (bi, 0, 0)),
                  full(cw["pe"]), full(cw["w"]), full(cw["w2"])],
        out_specs=pl.BlockSpec((1, n, 2 * LANES), lambda bi, pt: (bi, 0, 0)),
        scratch_shapes=[pltpu.VMEM((2, 2, n_pages, LANES, page), F32), pltpu.VMEM((2, n_pages * page, LANES), F32),
                        pltpu.SemaphoreType.DMA((2,))],
    )
    return pl.pallas_call(
        functools.partial(_compress_sample_kernel, n=n, n_pages=n_pages, page=page),
        grid_spec=grid_spec,
        out_shape=jax.ShapeDtypeStruct((b, n, 2 * LANES), BF16),
        compiler_params=_cparams("arbitrary"),
        name="compress_sample",
    )(page_table.reshape(-1), cache_t, tail, cw["pe"], cw["w"], cw["w2"])


def _topk_mask(score, ids, n_top, axis):
    sel = jnp.zeros(score.shape, jnp.bool_)
    work = score
    firsts = []
    big = float(score.shape[axis])
    for _ in range(n_top):
        m = jnp.max(work, axis=axis, keepdims=True)
        first = jnp.min(jnp.where(work == m, ids, big), axis=axis, keepdims=True)
        pick = (ids == first) & (m > -jnp.inf)
        sel = sel | pick
        work = jnp.where(pick, -jnp.inf, work)
        firsts.append(jnp.where(m > -jnp.inf, first, -1.0))
    return sel, firsts


def _gate_and_pack(o_ref_store, gates, o_c, o_s, o_w, rows):
    lane = lax.broadcasted_iota(I32, (1, LANES), 1)
    heads = []
    for h in range(NSA_HEADS):
        g, r = divmod(h, NSA_REP)
        rs = slice(r * rows, (r + 1) * rows)
        gc, gs, gw = (gates[:, N_BRANCH * h + k:N_BRANCH * h + k + 1] for k in range(N_BRANCH))
        heads.append(gc * o_c[g][rs] + gs * o_s[g][rs] + gw * o_w[g][rs])
    for j in range(NSA_HEADS // 2):
        a, b = heads[2 * j], heads[2 * j + 1]
        if (2 * j) // NSA_REP == 0:
            chunk = jnp.where(lane < HEAD_DIM, a, pltpu.roll(b, HEAD_DIM, 1))
        else:
            chunk = jnp.where(lane < HEAD_DIM, pltpu.roll(a, HEAD_DIM, 1), b)
        o_ref_store(j, chunk)


def _nsa_prompt_kernel(q_ref, gate_ref, kc_ref, kv_ref, poolt_ref, e_ref, o_ref, acc_ref, *, tq, tk, t_len):
    i = pl.program_id(1)
    s0 = i * tq
    rows4 = NSA_REP * tq
    tpos = s0 + lax.broadcasted_iota(I32, (tq, 1), 0)
    rep = lambda x: jnp.concatenate([x] * NSA_REP, axis=0)
    n_cmp = kc_ref.shape[1]
    kc128 = kc_ref[0, :, :LANES]
    vc128 = kc_ref[0, :, LANES:]
    kc_end = lax.broadcasted_iota(I32, (1, n_cmp), 1) * CMP_STRIDE + (CMP_BLOCK - 1)
    bias_c = rep(jnp.where(kc_end <= tpos, 0.0, NEG))
    any_c = rep((tpos >= CMP_BLOCK - 1).astype(F32))
    n_sel = t_len // SEL_BLOCK
    blk = lax.broadcasted_iota(I32, (n_sel, 1), 0)
    blk_f = blk.astype(F32)
    tq_lane = s0 + lax.broadcasted_iota(I32, (1, tq), 1)
    cur = tq_lane // SEL_BLOCK
    valid = blk * SEL_BLOCK <= tq_lane
    forced = (blk == 0) | (blk == cur) | (blk == cur - 1)
    w_start = pl.multiple_of(jnp.maximum(s0 - WINDOW, 0), LANES)
    w_len = WINDOW + tq
    kpos_w = w_start + lax.broadcasted_iota(I32, (1, w_len), 1)
    dist = tpos - kpos_w
    bias_w = rep(jnp.where((dist >= 0) & (dist < WINDOW), 0.0, NEG))
    c_diag = s0 // tk
    kpos_d = c_diag * tk + lax.broadcasted_iota(I32, (1, tk), 1)
    bias_causal = jnp.where(kpos_d <= tpos, 0.0, NEG)

    o_c, o_s, o_w, qgs, selbs = [], [], [], [], []
    for g in range(NSA_KV_HEADS):
        qg = jnp.concatenate([q_ref[0, :, (g * NSA_REP + r) * LANES:(g * NSA_REP + r + 1) * LANES]
                              for r in range(NSA_REP)], axis=0)
        qgs.append(qg)
        s_c = _dot_nt(qg, kc128) + bias_c
        e_c = jnp.exp(s_c - jnp.max(s_c, axis=-1, keepdims=True))
        p_c = e_c * (any_c / jnp.sum(e_c, axis=-1, keepdims=True))
        o_c.append(_dot(p_c.astype(BF16), vc128))
        imp = p_c[0:tq]
        for r in range(1, NSA_REP):
            imp = imp + p_c[r * tq:(r + 1) * tq]
        imp_t = sum(_dot_nt(poolt_ref[...], piece) for piece in _split3(imp))
        score = jnp.where(valid, imp_t + jnp.where(forced, FORCE_BONUS, 0.0), -jnp.inf)
        sel_t, _ = _topk_mask(score, blk_f, min(SEL_TOPK, n_sel), 0)
        selb_t = jnp.concatenate([jnp.where(sel_t, 0.0, NEG), jnp.full(((-n_sel) % LANES, tq), NEG, F32)], axis=0)
        selbs.append(selb_t.T.astype(BF16))

        kw = kv_ref[0, pl.ds(w_start, w_len), 4 * LANES:5 * LANES]
        vw = kv_ref[0, pl.ds(w_start, w_len), 5 * LANES:6 * LANES]
        s_w = _dot_nt(qg, kw) + bias_w
        e_w = jnp.exp(s_w - jnp.max(s_w, axis=-1, keepdims=True))
        o_w.append(_dot(e_w.astype(BF16), vw) * (1.0 / jnp.sum(e_w, axis=-1, keepdims=True)))

    acc_ref[...] = jnp.zeros(acc_ref.shape, F32)

    def chunk_step(c, carry, extra_bias):
        k0 = pl.multiple_of(c * tk, tk)
        ks = kv_ref[0, pl.ds(k0, tk), 2 * LANES:3 * LANES]
        vs = kv_ref[0, pl.ds(k0, tk), 3 * LANES:4 * LANES]
        out = []
        for g in range(NSA_KV_HEADS):
            m, l = carry[g]
            bias = _dot(selbs[g], e_ref[c])
            if extra_bias is not None:
                bias = bias + extra_bias
            s = _dot_nt(qgs[g], ks) + rep(bias)
            m_new = jnp.maximum(m, jnp.max(s, axis=-1, keepdims=True))
            alpha = jnp.exp(m - m_new)
            p = jnp.exp(s - m_new)
            l_new = alpha * l + jnp.sum(p, axis=-1, keepdims=True)
            acc_ref[g] = alpha * acc_ref[g] + _dot(p.astype(BF16), vs)
            out.append((m_new, l_new))
        return tuple(out)

    init = tuple((jnp.full((rows4, 1), NEG, F32), jnp.zeros((rows4, 1), F32)) for _ in range(NSA_KV_HEADS))
    carry = lax.fori_loop(0, c_diag, lambda c, cr: chunk_step(c, cr, None), init)
    carry = chunk_step(c_diag, carry, bias_causal)
    for g in range(NSA_KV_HEADS):
        o_s.append(acc_ref[g] * (1.0 / jnp.maximum(carry[g][1], 1e-30)))

    def store(j, chunk):
        o_ref[0, :, j * LANES:(j + 1) * LANES] = chunk

    _gate_and_pack(store, gate_ref[0], o_c, o_s, o_w, tq)


def _nsa_prompt_call(q3, gates3, kcvc, kvb3, poolt, e_mat, *, tq, tk):
    b, t, _ = q3.shape
    n_cmp = kcvc.shape[1]
    return pl.pallas_call(
        functools.partial(_nsa_prompt_kernel, tq=tq, tk=tk, t_len=t),
        grid=(b, t // tq),
        in_specs=[pl.BlockSpec((1, tq, Q_PAD), lambda bi, i: (bi, i, 0)),
                  pl.BlockSpec((1, tq, LANES), lambda bi, i: (bi, i, 0)),
                  pl.BlockSpec((1, n_cmp, 2 * LANES), lambda bi, i: (bi, 0, 0)),
                  pl.BlockSpec((1, t, NSA_KV_WIDTH), lambda bi, i: (bi, 0, 0)),
                  pl.BlockSpec(poolt.shape, lambda bi, i: (0, 0)),
                  pl.BlockSpec(e_mat.shape, lambda bi, i: (0, 0, 0))],
        out_specs=pl.BlockSpec((1, tq, NSA_WIDTH), lambda bi, i: (bi, i, 0)),
        out_shape=jax.ShapeDtypeStruct((b, t, NSA_WIDTH), F32),
        scratch_shapes=[pltpu.VMEM((NSA_KV_HEADS, NSA_REP * tq, LANES), F32)],
        compiler_params=_cparams("parallel", "arbitrary"),
        name="nsa_prompt",
    )(q3, gates3, kcvc, kvb3, poolt, e_mat)


def _memkv_kernel(x_ref, g_ref, w_ref, gk_ref, seg_ref, o_ref, ob_ref):
    a = _rms(x_ref[...]) * g_ref[...]
    z = _dot(a.astype(BF16), w_ref[...])
    for c in range(MEM_WIDTH // LANES):
        sl = slice(c * LANES, (c + 1) * LANES)
        xc = z[:, sl]
        ssq = _dot((xc * xc).astype(BF16), seg_ref[...])
        kn = xc * lax.rsqrt(ssq * (1.0 / HEAD_DIM) + EPS) * gk_ref[:, sl]
        o_ref[:, sl] = kn
        ob_ref[:, sl] = kn.astype(BF16)
    o_ref[:, MEM_WIDTH:] = z[:, MEM_WIDTH:]
    ob_ref[:, MEM_WIDTH:] = z[:, MEM_WIDTH:].astype(BF16)


def _memkv_call(mem2d, g_mem, w_mem, gmk, seg, *, tm):
    n, dm = mem2d.shape
    full = lambda shape: pl.BlockSpec(shape, lambda i: (0,) * len(shape))
    return pl.pallas_call(
        _memkv_kernel,
        grid=(n // tm,),
        in_specs=[pl.BlockSpec((tm, dm), lambda i: (i, 0)), full((1, dm)), full((dm, 2 * MEM_WIDTH)),
                  full((1, MEM_WIDTH)), full((LANES, LANES))],
        out_specs=[pl.BlockSpec((tm, 2 * MEM_WIDTH), lambda i: (i, 0))] * 2,
        out_shape=[jax.ShapeDtypeStruct((n, 2 * MEM_WIDTH), F32), jax.ShapeDtypeStruct((n, 2 * MEM_WIDTH), BF16)],
        compiler_params=_cparams("parallel"),
        name="memkv",
    )(mem2d, g_mem, w_mem, gmk, seg)


def _mem_attend(qm, mkv, rows):
    lane = lax.broadcasted_iota(I32, (1, LANES), 1)
    chunks = []
    for j in range(MEM_HEADS // 2):
        k128 = mkv[:, j * LANES:(j + 1) * LANES]
        v128 = mkv[:, MEM_WIDTH + j * LANES:MEM_WIDTH + (j + 1) * LANES]
        q2 = jnp.concatenate([qm[2 * j], qm[2 * j + 1]], axis=0)
        s = _dot_nt(q2, k128)
        e = jnp.exp(s - jnp.max(s, axis=-1, keepdims=True))
        p = e / jnp.sum(e, axis=-1, keepdims=True)
        o = _dot(p.astype(BF16), v128)
        chunks.append(jnp.where(lane < HEAD_DIM, o[:rows], o[rows:2 * rows]))
    return chunks


def _memattn_kernel(qm_ref, mkv_ref, o_ref, *, tq):
    qm = [qm_ref[0, :, h * LANES:(h + 1) * LANES] for h in range(MEM_HEADS)]
    for j, chunk in enumerate(_mem_attend(qm, mkv_ref[0], tq)):
        o_ref[0, :, j * LANES:(j + 1) * LANES] = chunk


def _memattn_call(qm3, mkv3, *, tq):
    b, t, _ = qm3.shape
    m = mkv3.shape[1]
    return pl.pallas_call(
        functools.partial(_memattn_kernel, tq=tq),
        grid=(b, t // tq),
        in_specs=[pl.BlockSpec((1, tq, QM_PAD), lambda bi, i: (bi, i, 0)),
                  pl.BlockSpec((1, m, 2 * MEM_WIDTH), lambda bi, i: (bi, 0, 0))],
        out_specs=pl.BlockSpec((1, tq, MEM_WIDTH), lambda bi, i: (bi, i, 0)),
        out_shape=jax.ShapeDtypeStruct((b, t, MEM_WIDTH), F32),
        compiler_params=_cparams("parallel", "parallel"),
        name="memattn",
    )(qm3, mkv3)


def _pad_rows(rows_list):
    x = jnp.concatenate(rows_list, axis=0)
    return jnp.concatenate([x, jnp.zeros((8 - x.shape[0], x.shape[1]), x.dtype)], axis=0)


def _sample_attn1_kernel(q_ref, kc_ref, win_ref, mkv_ref, qm_ref, pool_ref,
                         oc_ref, ow_ref, om_ref, score_ref, *, t_pos, n_win):
    q = q_ref[0].astype(F32)
    n_cmp = kc_ref.shape[1]
    kc128 = kc_ref[0, :, :LANES]
    vc128 = kc_ref[0, :, LANES:]
    kw128 = win_ref[0, :, :LANES].astype(BF16)
    vw128 = win_ref[0, :, LANES:].astype(BF16)
    kc_end = lax.broadcasted_iota(I32, (1, n_cmp), 1) * CMP_STRIDE + (CMP_BLOCK - 1)
    mask_c = kc_end <= t_pos
    kw_pos = t_pos - (n_win - 1) + lax.broadcasted_iota(I32, (1, n_win), 1)
    mask_w = (t_pos - kw_pos >= 0) & (t_pos - kw_pos < WINDOW) & (kw_pos >= 0)
    n_slot = pool_ref.shape[1]
    blk = lax.broadcasted_iota(I32, (1, n_slot), 1)
    cur = t_pos // SEL_BLOCK
    valid = blk * SEL_BLOCK <= t_pos
    forced = (blk == 0) | (blk == cur) | (blk == cur - 1)
    for g in range(NSA_KV_HEADS):
        qg = _pad_rows([q[:, (g * NSA_REP + r) * LANES:(g * NSA_REP + r + 1) * LANES]
                        for r in range(NSA_REP)]).astype(BF16)
        p_c = _masked_softmax(_dot_nt(qg, kc128), mask_c)
        oc_ref[0, g * NSA_REP:(g + 1) * NSA_REP, :] = _dot(p_c.astype(BF16), vc128)[:NSA_REP]
        imp = jnp.sum(p_c[:NSA_REP], axis=0, keepdims=True)
        imp8 = jnp.broadcast_to(imp, (8, n_cmp))
        imp_b = sum(_dot(piece, pool_ref[...]) for piece in _split3(imp8))[0:1]
        score_ref[0, g:g + 1, :] = jnp.where(valid, imp_b + jnp.where(forced, FORCE_BONUS, 0.0), -jnp.inf)
        p_w = _masked_softmax(_dot_nt(qg, kw128), mask_w)
        ow_ref[0, g * NSA_REP:(g + 1) * NSA_REP, :] = _dot(p_w.astype(BF16), vw128)[:NSA_REP]
    qm = qm_ref[0].astype(F32)
    qmh = []
    for h in range(MEM_HEADS):
        row = qm[:, h * LANES:(h + 1) * LANES]
        qmh.append(jnp.concatenate([row, jnp.zeros((7, LANES), F32)], axis=0).astype(BF16))
    for j, chunk in enumerate(_mem_attend(qmh, mkv_ref[0].astype(BF16), 8)):
        om_ref[0, :, j * LANES:(j + 1) * LANES] = chunk[0:1]


def _sample_attn1_call(q3, kcvc, win3, mkv3, qm3, pool_mat, *, t_pos):
    b = q3.shape[0]
    n_cmp = kcvc.shape[1]
    n_win = win3.shape[1]
    m = mkv3.shape[1]
    per_b = lambda shape: pl.BlockSpec((1,) + shape, lambda bi: (bi, 0, 0))
    return pl.pallas_call(
        functools.partial(_sample_attn1_kernel, t_pos=t_pos, n_win=n_win),
        grid=(b,),
        in_specs=[per_b((1, Q_PAD)), per_b((n_cmp, 2 * LANES)), per_b((n_win, 2 * LANES)),
                  per_b((m, 2 * MEM_WIDTH)), per_b((1, QM_PAD)),
                  pl.BlockSpec(pool_mat.shape, lambda bi: (0, 0))],
        out_specs=[per_b((NSA_HEADS, LANES)), per_b((NSA_HEADS, LANES)), per_b((1, MEM_WIDTH)),
                   per_b((NSA_KV_HEADS, pool_mat.shape[1]))],
        out_shape=[jax.ShapeDtypeStruct((b, NSA_HEADS, LANES), F32), jax.ShapeDtypeStruct((b, NSA_HEADS, LANES), F32),
                   jax.ShapeDtypeStruct((b, 1, MEM_WIDTH), F32),
                   jax.ShapeDtypeStruct((b, NSA_KV_HEADS, pool_mat.shape[1]), F32)],
        compiler_params=_cparams("parallel"),
        name="sample_attn1",
    )(q3, kcvc, win3, mkv3, qm3, pool_mat)


def _sample_topk_kernel(score_ref, idx_ref, *, n_top):
    score = score_ref[...]
    ids = lax.broadcasted_iota(I32, (1, score.shape[1]), 1).astype(F32)
    _, firsts = _topk_mask(score, ids, n_top, 1)
    lane = lax.broadcasted_iota(I32, (1, LANES), 1)
    idx = jnp.full((score.shape[0], LANES), -1, I32)
    for j, f in enumerate(firsts):
        idx = jnp.where(lane == j, f.astype(I32), idx)
    idx_ref[...] = idx


def _sample_topk_call(score2d, *, n_top):
    rows = score2d.shape[0]
    return pl.pallas_call(
        functools.partial(_sample_topk_kernel, n_top=n_top),
        out_shape=jax.ShapeDtypeStruct((rows, LANES), I32),
        compiler_params=pltpu.CompilerParams(vmem_limit_bytes=VMEM_LIMIT),
        name="sample_topk",
    )(score2d)


def _sample_sel_kernel(pt_ref, idx_ref, cache_ref, q_ref, knew_ref, gate_ref, oc_ref, ow_ref, o_ref,
                       buf, sem, *, t_pos, n_pages, n_top, page):
    b = pl.program_id(0)
    nb = pl.num_programs(0)
    blk_per_page = page // SEL_BLOCK
    n_past_blk = n_pages * blk_per_page
    idx_stride = NSA_KV_HEADS * LANES

    def blk_at(bb, g, j):
        return idx_ref[bb * idx_stride + g * LANES + j]

    def blk_copy(bb, slot, g, j, kv):
        blkc = jnp.clip(blk_at(bb, g, j), 0, n_past_blk - 1)
        pg = pt_ref[bb * n_pages + blkc // blk_per_page]
        return pltpu.make_async_copy(cache_ref.at[pg, 2 + kv], buf.at[slot, g, kv, j], sem.at[slot])

    def for_all(fn):
        for g in range(NSA_KV_HEADS):
            for j in range(n_top):
                for kv in range(2):
                    fn(g, j, kv)

    def issue(bb, slot):
        for_all(lambda g, j, kv: blk_copy(bb, slot, g, j, kv).start())

    @pl.when(b == 0)
    def _():
        issue(0, 0)

    @pl.when(b + 1 < nb)
    def _():
        issue(b + 1, (b + 1) % 2)

    slot = b % 2
    for_all(lambda g, j, kv: blk_copy(b, slot, g, j, kv).wait())

    q = q_ref[0].astype(F32)
    knew = knew_ref[0]
    k_new = knew[:, :LANES].astype(BF16).astype(F32)
    v_new = knew[:, LANES:].astype(BF16).astype(F32)
    n_keys = n_top * page
    key_lane = lax.broadcasted_iota(I32, (1, n_keys), 1)
    key_slot = key_lane // page
    key_row = key_lane % page
    cur_blk = t_pos // SEL_BLOCK
    o_s = []
    for g in range(NSA_KV_HEADS):
        qg = _pad_rows([q[:, (g * NSA_REP + r) * LANES:(g * NSA_REP + r + 1) * LANES] for r in range(NSA_REP)])
        blkvec = jnp.full((1, n_keys), -1, I32)
        has_cur = jnp.zeros((1, 1), jnp.bool_)
        for j in range(n_top):
            bj = blk_at(b, g, j)
            blkvec = jnp.where(key_slot == j, bj, blkvec)
            has_cur = has_cur | (bj == cur_blk)
        in_blk = key_row // SEL_BLOCK == blkvec % blk_per_page
        key_pos = (blkvec // blk_per_page) * page + key_row
        vis = (blkvec >= 0) & (blkvec < n_past_blk) & in_blk & (key_pos <= t_pos)
        kt = jnp.concatenate([buf[slot, g, 0, j] for j in range(n_top)], axis=1).astype(BF16)
        vt = jnp.concatenate([buf[slot, g, 1, j] for j in range(n_top)], axis=1).astype(BF16)
        s_past = jnp.where(vis, _dot(qg.astype(BF16), kt), NEG)
        s_new = jnp.where(has_cur, jnp.sum(qg * k_new, axis=-1, keepdims=True), NEG)
        m = jnp.maximum(jnp.max(s_past, axis=-1, keepdims=True), s_new)
        e_p = jnp.where(vis, jnp.exp(s_past - m), 0.0)
        e_n = jnp.where(has_cur, jnp.exp(s_new - m), 0.0)
        den = jnp.maximum(jnp.sum(e_p, axis=-1, keepdims=True) + e_n, 1e-30)
        num = _dot_nt(e_p.astype(BF16), vt) + e_n.astype(BF16).astype(F32) * v_new
        o_s.append(num / den)
    o_c = [jnp.concatenate([oc_ref[0, g * NSA_REP:(g + 1) * NSA_REP, :]] * 2, axis=0) for g in range(NSA_KV_HEADS)]
    o_w = [jnp.concatenate([ow_ref[0, g * NSA_REP:(g + 1) * NSA_REP, :]] * 2, axis=0) for g in range(NSA_KV_HEADS)]

    def store(j, chunk):
        o_ref[0, :, j * LANES:(j + 1) * LANES] = chunk

    _gate_and_pack(store, gate_ref[0], o_c, o_s, o_w, 1)


def _sample_sel_call(page_table, idx, cache_t, q3, knew3, gates3, oc, ow, *, t_pos, n_top):
    b, n_pages = page_table.shape
    page = cache_t.shape[3]
    per_b = lambda shape: pl.BlockSpec((1,) + shape, lambda bi, pt, ix: (bi, 0, 0))
    grid_spec = pltpu.PrefetchScalarGridSpec(
        num_scalar_prefetch=2,
        grid=(b,),
        in_specs=[pl.BlockSpec(memory_space=pl.ANY), per_b((1, Q_PAD)), per_b((1, 2 * LANES)), per_b((1, LANES)),
                  per_b((NSA_HEADS, LANES)), per_b((NSA_HEADS, LANES))],
        out_specs=per_b((1, NSA_WIDTH)),
        scratch_shapes=[pltpu.VMEM((2, NSA_KV_HEADS, 2, n_top, LANES, page), F32), pltpu.SemaphoreType.DMA((2,))],
    )
    return pl.pallas_call(
        functools.partial(_sample_sel_kernel, t_pos=t_pos, n_pages=n_pages, n_top=n_top, page=page),
        grid_spec=grid_spec,
        out_shape=jax.ShapeDtypeStruct((b, 1, NSA_WIDTH), F32),
        compiler_params=_cparams("arbitrary"),
        name="sample_sel",
    )(page_table.reshape(-1), idx.reshape(-1), cache_t, q3, knew3, gates3, oc, ow)


def _finish_kernel(yp_ref, on_ref, om_ref, x_ref, gm_ref, wo_ref, gf_ref, wr_ref, br_ref, tri_ref,
                   h_ref, hn_ref, route_ref, counts_ref, cnt_ref):
    gm = gm_ref[...]
    o1 = POOL_WIDTH
    o2 = o1 + NSA_WIDTH
    mixed = jnp.concatenate([_rms(yp_ref[...]) * gm[:, :o1], _rms(on_ref[...]) * gm[:, o1:o2],
                             _rms(om_ref[...]) * gm[:, o2:]], axis=-1)
    h = x_ref[...] + _dot(mixed.astype(BF16), wo_ref[...])
    h_ref[...] = h
    hn = _rms(h) * gf_ref[...]
    hn_ref[...] = hn
    logits = _dot(hn.astype(BF16), wr_ref[...]) + br_ref[...]
    lane = lax.broadcasted_iota(I32, (1, LANES), 1)
    lane_f = lane.astype(F32)
    is1 = lane < N_EXPERT_GROUPS
    m1 = jnp.max(jnp.where(is1, logits, -jnp.inf), axis=-1, keepdims=True)
    e1 = jnp.where(is1, jnp.exp(logits - m1), 0.0)
    p1 = e1 / jnp.sum(e1, axis=-1, keepdims=True)
    top1_p = jnp.max(p1, axis=-1, keepdims=True)
    grp = jnp.min(jnp.where((p1 == top1_p) & is1, lane_f, float(LANES)), axis=-1, keepdims=True)
    base = N_EXPERT_GROUPS + grp * EXPERTS_PER_GROUP
    in_g = (lane_f >= base) & (lane_f < base + EXPERTS_PER_GROUP)
    l2 = jnp.where(in_g, logits, -jnp.inf)
    v0 = jnp.max(l2, axis=-1, keepdims=True)
    i0 = jnp.min(jnp.where(l2 == v0, lane_f, float(LANES)), axis=-1, keepdims=True)
    l2b = jnp.where(lane_f == i0, -jnp.inf, l2)
    v1 = jnp.max(l2b, axis=-1, keepdims=True)
    i1 = jnp.min(jnp.where(l2b == v1, lane_f, float(LANES)), axis=-1, keepdims=True)
    ex = jnp.exp(v1 - v0)
    w0 = top1_p / (1.0 + ex)
    w1 = top1_p * ex / (1.0 + ex)
    ex0 = i0 - N_EXPERT_GROUPS
    ex1 = i1 - N_EXPERT_GROUPS
    is0 = lane_f == ex0
    is1e = lane_f == ex1
    oh0 = jnp.where(is0, 1.0, 0.0)
    oh1 = jnp.where(is1e, 1.0, 0.0)
    before0 = _dot(tri_ref[...], oh0.astype(BF16))
    before1 = _dot(tri_ref[...], oh1.astype(BF16))
    tot0 = jnp.sum(oh0, axis=0, keepdims=True)
    tot1 = jnp.sum(oh1, axis=0, keepdims=True)

    @pl.when(pl.program_id(0) == 0)
    def _():
        cnt_ref[...] = jnp.zeros(cnt_ref.shape, F32)

    seen = cnt_ref[...]
    rank0 = jnp.sum(jnp.where(is0, before0 + seen, 0.0), axis=-1, keepdims=True)
    rank1 = jnp.sum(jnp.where(is1e, before1 + seen + tot0, 0.0), axis=-1, keepdims=True)
    cnt_ref[...] = seen + tot0 + tot1
    counts_ref[...] = seen + tot0 + tot1
    route = jnp.where(lane == 0, ex0, jnp.where(lane == 1, ex1, jnp.where(lane == 2, w0, jnp.where(lane == 3, w1,
            jnp.where(lane == 4, rank0, jnp.where(lane == 5, rank1, 0.0))))))
    route_ref[...] = route


def _finish_call(yp, on, om, x2d, fw, *, tm):
    n, dm = x2d.shape
    full = lambda shape: pl.BlockSpec(shape, lambda i: (0,) * len(shape))
    rows = lambda w: pl.BlockSpec((tm, w), lambda i: (i, 0))
    tri = jnp.asarray(np.arange(tm)[None, :] < np.arange(tm)[:, None], BF16)
    return pl.pallas_call(
        _finish_kernel,
        grid=(n // tm,),
        in_specs=[rows(POOL_WIDTH), rows(NSA_WIDTH), rows(MEM_WIDTH), rows(dm), full((1, dm)), full((dm, dm)),
                  full((1, dm)), full((dm, LANES)), full((1, LANES)), full((tm, tm))],
        out_specs=[rows(dm), rows(dm), rows(LANES), full((1, LANES))],
        out_shape=[jax.ShapeDtypeStruct((n, dm), F32), jax.ShapeDtypeStruct((n, dm), F32),
                   jax.ShapeDtypeStruct((n, LANES), F32), jax.ShapeDtypeStruct((1, LANES), F32)],
        scratch_shapes=[pltpu.VMEM((1, LANES), F32)],
        compiler_params=_cparams("arbitrary"),
        name="finish",
    )(yp, on, om, x2d, fw["g_mix"], fw["w_out"], fw["g_ffn"], fw["w_r"], fw["b_r"], tri)


def _route_tables(route, counts, tm):
    n = route.shape[0]
    eid = route[:, 0:2].astype(I32)
    rank = route[:, 4:6].astype(I32)
    cnt = counts[0, :N_EXPERTS].astype(I32)
    tiles_per = jnp.maximum((cnt + tm - 1) // tm, 1)
    tile_end = jnp.cumsum(tiles_per)
    tile_start = tile_end - tiles_per
    experts = jnp.arange(N_EXPERTS, dtype=I32)
    start_of = jnp.sum(jnp.where(eid[:, :, None] == experts, tile_start, 0), axis=-1)
    pos = (start_of * tm + rank).reshape(-1)
    n_tiles = (2 * n) // tm + N_EXPERTS
    tj = jnp.arange(n_tiles, dtype=I32)
    tile_expert = jnp.minimum(jnp.sum((tj[:, None] >= tile_end[None, :]).astype(I32), axis=1), N_EXPERTS - 1)
    sel = tile_expert[:, None] == experts
    rows_left = jnp.sum(jnp.where(sel, cnt, 0), axis=-1) - (tj - jnp.sum(jnp.where(sel, tile_start, 0), axis=-1)) * tm
    tile_nvalid = jnp.where(tj < tile_end[-1], jnp.clip(rows_left, 0, tm), 0).astype(I32)
    return tile_expert, tile_nvalid, pos


def _row_wait_all(src_row, dst_row, sem, count):
    for _ in range(count):
        pltpu.make_async_copy(src_row, dst_row, sem).wait()


def _dispatch_kernel(pos_ref, nv_ref, hn_ref, xs_ref, stage, zbuf, sem, zsem, *, tm, tme, n_tiles):
    i = pl.program_id(0)
    nt = pl.num_programs(0)
    slot = i % 2
    wait_slot = lambda s: _row_wait_all(stage.at[s, pl.ds(0, 1), :], xs_ref.at[pl.ds(0, 1), :], sem.at[s], 2 * tm)

    @pl.when(i == 0)
    def _():
        zbuf[...] = jnp.zeros(zbuf.shape, F32)

        def fill(j, c):
            @pl.when(nv_ref[j] < tme)
            def _():
                pltpu.make_async_copy(zbuf, xs_ref.at[pl.ds(pl.multiple_of(j * tme, tme), tme), :], zsem).start()
            return c

        def drain(j, c):
            @pl.when(nv_ref[j] < tme)
            def _():
                pltpu.make_async_copy(zbuf, xs_ref.at[pl.ds(0, tme), :], zsem).wait()
            return c

        lax.fori_loop(0, n_tiles, fill, 0)
        lax.fori_loop(0, n_tiles, drain, 0)

    @pl.when(i >= 2)
    def _():
        wait_slot(slot)

    stage[slot] = hn_ref[...]
    base = i * (2 * tm)
    for r in range(tm):
        for k in range(2):
            dst = pos_ref[base + 2 * r + k]
            pltpu.make_async_copy(stage.at[slot, pl.ds(r, 1), :], xs_ref.at[pl.ds(dst, 1), :], sem.at[slot]).start()

    @pl.when(i == nt - 1)
    def _():
        wait_slot(slot)

        @pl.when(i >= 1)
        def _():
            wait_slot(1 - slot)


def _dispatch_call(pos, tile_nvalid, hn, *, tm, tme):
    n, dm = hn.shape
    n_tiles = tile_nvalid.shape[0]
    grid_spec = pltpu.PrefetchScalarGridSpec(
        num_scalar_prefetch=2,
        grid=(n // tm,),
        in_specs=[pl.BlockSpec((tm, dm), lambda i, pos, nv: (i, 0))],
        out_specs=pl.BlockSpec(memory_space=pl.ANY),
        scratch_shapes=[pltpu.VMEM((2, tm, dm), F32), pltpu.VMEM((tme, dm), F32),
                        pltpu.SemaphoreType.DMA((2,)), pltpu.SemaphoreType.DMA(())],
    )
    return pl.pallas_call(
        functools.partial(_dispatch_kernel, tm=tm, tme=tme, n_tiles=n_tiles),
        grid_spec=grid_spec,
        out_shape=jax.ShapeDtypeStruct((n_tiles * tme, dm), F32),
        compiler_params=_cparams("arbitrary"),
        name="moe_dispatch",
    )(pos, tile_nvalid, hn)


def _expert_kernel(te_ref, nv_ref, x_ref, wg_ref, wu_ref, wd_ref, hns_ref, routes_ref, hs_ref,
                   y_ref, ysample_ref, wgb, wub, wdb, acc_s, *, tm):
    i = pl.program_id(0)
    prev = te_ref[jnp.maximum(i - 1, 0)]

    @pl.when(i == 0)
    def _():
        acc_s[...] = jnp.zeros(acc_s.shape, F32)

    @pl.when((i == 0) | (te_ref[i] != prev))
    def _():
        wgb[...] = wg_ref[0].astype(BF16)
        wub[...] = wu_ref[0].astype(BF16)
        wdb[...] = wd_ref[0].astype(BF16)
        route = routes_ref[...]
        ef = te_ref[i].astype(F32)
        comb = jnp.where(route[:, 0:1] == ef, route[:, 2:3], 0.0) + jnp.where(route[:, 1:2] == ef, route[:, 3:4], 0.0)
        xs = hns_ref[...].astype(BF16)
        hg = _dot(xs, wgb[...])
        hu = _dot(xs, wub[...])
        hmid = hg * jax.nn.sigmoid(hg) * hu * comb
        acc_s[...] += _dot(hmid.astype(BF16), wdb[...])

    @pl.when(i == pl.num_programs(0) - 1)
    def _():
        ysample_ref[...] = hs_ref[...] + acc_s[...]

    nv = nv_ref[i]

    @pl.when(nv == 0)
    def _():
        y_ref[...] = jnp.zeros(y_ref.shape, F32)

    @pl.when(nv > 0)
    def _():
        x = x_ref[...].astype(BF16)
        hg = _dot(x, wgb[...])
        hu = _dot(x, wub[...])
        hmid = hg * jax.nn.sigmoid(hg) * hu
        y_ref[...] = _dot(hmid.astype(BF16), wdb[...])


def _expert_call(tile_expert, tile_nvalid, xs, wg, wu, wd, hn_s, route_s, h_s, *, tm):
    n_tiles = tile_expert.shape[0]
    dm = xs.shape[1]
    ff = wg.shape[2]
    ns = hn_s.shape[0]
    full = lambda shape: pl.BlockSpec(shape, lambda i, te, nv: (0,) * len(shape))
    grid_spec = pltpu.PrefetchScalarGridSpec(
        num_scalar_prefetch=2,
        grid=(n_tiles,),
        in_specs=[pl.BlockSpec((tm, dm), lambda i, te, nv: (i, 0)),
                  pl.BlockSpec((1, dm, ff), lambda i, te, nv: (te[i], 0, 0)),
                  pl.BlockSpec((1, dm, ff), lambda i, te, nv: (te[i], 0, 0)),
                  pl.BlockSpec((1, ff, dm), lambda i, te, nv: (te[i], 0, 0)),
                  full((ns, dm)), full((ns, LANES)), full((ns, dm))],
        out_specs=[pl.BlockSpec((tm, dm), lambda i, te, nv: (i, 0)), full((ns, dm))],
        scratch_shapes=[pltpu.VMEM((dm, ff), BF16), pltpu.VMEM((dm, ff), BF16), pltpu.VMEM((ff, dm), BF16),
                        pltpu.VMEM((ns, dm), F32)],
    )
    return pl.pallas_call(
        functools.partial(_expert_kernel, tm=tm),
        grid_spec=grid_spec,
        out_shape=[jax.ShapeDtypeStruct(xs.shape, F32), jax.ShapeDtypeStruct((ns, dm), F32)],
        compiler_params=_cparams("arbitrary"),
        name="moe_experts",
    )(tile_expert, tile_nvalid, xs, wg, wu, wd, hn_s, route_s, h_s)


def _combine_kernel(pos_ref, ys_ref, h_ref, route_ref, o_ref, gbuf, sem, *, tm):
    i = pl.program_id(0)
    nt = pl.num_programs(0)
    slot = i % 2

    def issue(step, s):
        base = step * (2 * tm)
        for r in range(tm):
            for k in range(2):
                src = pos_ref[base + 2 * r + k]
                pltpu.make_async_copy(ys_ref.at[pl.ds(src, 1), :], gbuf.at[s, k, pl.ds(r, 1), :], sem.at[s]).start()

    @pl.when(i == 0)
    def _():
        issue(0, 0)

    @pl.when(i + 1 < nt)
    def _():
        issue(i + 1, 1 - slot)

    _row_wait_all(ys_ref.at[pl.ds(0, 1), :], gbuf.at[slot, 0, pl.ds(0, 1), :], sem.at[slot], 2 * tm)
    w0 = route_ref[:, 2:3]
    w1 = route_ref[:, 3:4]
    o_ref[...] = h_ref[...] + (w0 * gbuf[slot, 0] + w1 * gbuf[slot, 1])


def _combine_call(pos, ys, h, route, *, tm):
    n, dm = h.shape
    grid_spec = pltpu.PrefetchScalarGridSpec(
        num_scalar_prefetch=1,
        grid=(n // tm,),
        in_specs=[pl.BlockSpec(memory_space=pl.ANY),
                  pl.BlockSpec((tm, dm), lambda i, pos: (i, 0)),
                  pl.BlockSpec((tm, LANES), lambda i, pos: (i, 0))],
        out_specs=pl.BlockSpec((tm, dm), lambda i, pos: (i, 0)),
        scratch_shapes=[pltpu.VMEM((2, 2, tm, dm), F32), pltpu.SemaphoreType.DMA((2,))],
    )
    return pl.pallas_call(
        functools.partial(_combine_kernel, tm=tm),
        grid_spec=grid_spec,
        out_shape=jax.ShapeDtypeStruct((n, dm), F32),
        compiler_params=_cparams("arbitrary"),
        name="moe_combine",
    )(pos, ys, h, route)


def _prep_weights(l, g_attn, w_in, g_q, g_k, pe_cmp, w_cmp1, w_cmp2, w_pool, s_pool, g_mem, w_mem_kv, g_mq, g_mk,
                  g_mix, w_out, g_ffn, w_router1, b_router1, w_router2, b_router2):
    dm = w_in.shape[1]
    w = w_in[l]
    o1 = POOL_WIDTH
    o2 = o1 + NSA_WIDTH
    o3 = o2 + NSA_KV_WIDTH
    o4 = o3 + GATE_WIDTH
    wq = w[:, o1:o2].reshape(dm, NSA_HEADS, 1, HEAD_DIM)
    q_slot = jnp.asarray(np.eye(NSA_KV_HEADS, dtype=np.float32)[np.arange(NSA_HEADS) // NSA_REP])
    wq_pad = (wq * q_slot[None, :, :, None]).reshape(dm, Q_PAD)
    wqm = w[:, o4:].reshape(dm, MEM_HEADS, 1, HEAD_DIM)
    m_slot = jnp.asarray(np.eye(2, dtype=np.float32)[np.arange(MEM_HEADS) % 2])
    wqm_pad = (wqm * m_slot[None, :, :, None]).reshape(dm, QM_PAD)
    wg_pad = jnp.pad(w[:, o3:o4], ((0, 0), (0, LANES - GATE_WIDTH)))
    w_packed = jnp.concatenate([w[:, :o1], wq_pad, w[:, o2:o3], wqm_pad, wg_pad], axis=1).astype(BF16)
    half = ROPE_DIM // 2
    inv = jnp.power(ROPE_THETA, -jnp.arange(half, dtype=F32) * 2.0 / ROPE_DIM)
    d = np.arange(LANES) % HEAD_DIM
    inv_lane = jnp.where(jnp.asarray(d < ROPE_DIM), inv[jnp.asarray(d % half)], 0.0).reshape(1, LANES)
    seg = jnp.asarray((np.arange(LANES)[:, None] // HEAD_DIM == np.arange(LANES)[None, :] // HEAD_DIM), BF16)
    gk = jnp.concatenate([jnp.tile(g_k[l, br], 2) for br in range(N_BRANCH)]).reshape(1, N_BRANCH * LANES)
    proj = dict(g_attn=g_attn[l].reshape(1, dm), w_in=w_packed, gq=jnp.tile(g_q[l], Q_PAD // HEAD_DIM).reshape(1, Q_PAD),
                gk=gk, gmq=jnp.tile(g_mq[l], QM_PAD // HEAD_DIM).reshape(1, QM_PAD), inv=inv_lane, seg=seg)

    n_grp = len(POOL_WINDOWS)
    pg = POOL_WIDTH // n_grp
    w_bd = (jnp.asarray(np.eye(n_grp, dtype=np.float32))[:, None, :, None] * w_pool[l][:, :, None, :]
            ).reshape(POOL_WIDTH, POOL_WIDTH).astype(BF16)
    pool = dict(w=w_bd, s=s_pool[l].reshape(1, POOL_WIDTH))

    half_rows = CMP_STRIDE * HEAD_DIM
    w1 = jnp.concatenate([w_cmp1[l][:, :half_rows], w_cmp1[l][:, half_rows:]], axis=2).astype(BF16)
    cmp_w = dict(w=w1, pe=pe_cmp[l].reshape(2, 1, CMP_BLOCK * HEAD_DIM), w2=w_cmp2[l].astype(BF16))

    mem = dict(g=g_mem[l].reshape(1, dm), w=w_mem_kv[l].astype(BF16),
               gk=jnp.tile(g_mk[l], MEM_HEADS).reshape(1, MEM_WIDTH))
    w_r = jnp.concatenate([w_router1[l], w_router2[l].reshape(dm, N_EXPERTS)], axis=1)
    w_r = jnp.pad(w_r, ((0, 0), (0, LANES - w_r.shape[1]))).astype(BF16)
    b_r = jnp.concatenate([b_router1[l], b_router2[l].reshape(-1)])
    b_r = jnp.pad(b_r, (0, LANES - b_r.shape[0])).reshape(1, LANES)
    fin = dict(g_mix=g_mix[l].reshape(1, dm), w_out=w_out[l].astype(BF16), g_ffn=g_ffn[l].reshape(1, dm),
               w_r=w_r, b_r=b_r)
    return proj, pool, cmp_w, mem, fin


def _pick(n, prefs):
    for p in prefs:
        if n % p == 0:
            return p
    return n


def kernel(x_prompt, x_sample, cache_kv, cache_win, state_pool, cache_mem_kv, page_table, mem_prompt, g_attn, w_in, g_q, g_k, pe_cmp, w_cmp1, w_cmp2, w_pool, s_pool, g_mem, w_mem_kv, g_mq, g_mk, g_mix, w_out, g_ffn, w_router1, b_router1, w_router2, b_router2, w_gate, w_up, w_down):
    depth = w_in.shape[0]
    bp, t, dm = x_prompt.shape
    bs, ts, _ = x_sample.shape
    n_pages = page_table.shape[1]
    page = cache_kv.shape[2]
    past_len = n_pages * page
    n_win = cache_win.shape[2]
    ff = w_gate.shape[-1]
    tq, tk = 256, 512
    assert ts == 1 and n_win == WINDOW and page % SEL_BLOCK == 0 and t % tk == 0 and t >= WINDOW + tq
    n_sel = t // SEL_BLOCK
    assert min(SEL_TOPK, n_sel) >= 3 and n_sel <= LANES
    poolt = jnp.asarray(np.arange(t // CMP_STRIDE)[None, :] // CMP_PER_SEL == np.arange(n_sel)[:, None], BF16)
    key_blk = np.arange(t // tk)[:, None, None] * (tk // SEL_BLOCK) + np.arange(tk)[None, None, :] // SEL_BLOCK
    e_mat = jnp.asarray(np.arange(LANES)[None, :, None] == key_blk, BF16)
    n_cmp_s = past_len // CMP_STRIDE
    n_slot_s = -(-(past_len // SEL_BLOCK + 1) // LANES) * LANES
    pool_s = jnp.asarray(np.arange(n_cmp_s)[:, None] // CMP_PER_SEL == np.arange(n_slot_s)[None, :], BF16)

    hp, hs = x_prompt, x_sample
    outs = [[] for _ in range(7)]
    for l in range(depth):
        proj_w, pool_w, cmp_w, mem_w, fin_w = _prep_weights(
            l, g_attn, w_in, g_q, g_k, pe_cmp, w_cmp1, w_cmp2, w_pool, s_pool, g_mem, w_mem_kv, g_mq, g_mk,
            g_mix, w_out, g_ffn, w_router1, b_router1, w_router2, b_router2)
        wg = w_gate[l].reshape(N_EXPERTS, dm, ff)
        wu = w_up[l].reshape(N_EXPERTS, dm, ff)
        wd = w_down[l].reshape(N_EXPERTS, ff, dm)
        kw_cols = CMP_STRIDE * 2 * LANES

        n = bp * t
        xp2 = hp.reshape(n, dm)
        u, qb, rows4, rowsw, kvb, gates, qmb, rows4t = _proj_call(
            xp2, proj_w, seq_len=t, pos0=0, tm=_pick(t, (512, 256, 128)), feature_major_rows4=True)
        u3 = u.reshape(bp, t, POOL_WIDTH)
        y_pool = _pool_call(u3, pool_w["w"], pool_w["s"], tp=_pick(t, (512, 256, 128)), pos0=0)
        n_slab = 2 * NSA_KV_HEADS
        chunk_w = CMP_STRIDE * HEAD_DIM
        kcvc = _compress_prompt_call(rows4.reshape(bp, t, 4 * LANES), jnp.zeros((bp, n_slab, chunk_w), F32), cmp_w)
        o_nsa = _nsa_prompt_call(qb.reshape(bp, t, Q_PAD), gates.reshape(bp, t, LANES), kcvc,
                                 kvb.reshape(bp, t, NSA_KV_WIDTH), poolt, e_mat, tq=tq, tk=tk)
        m_len = mem_prompt.shape[1]
        mkv, mkvb = _memkv_call(mem_prompt.reshape(bp * m_len, dm), mem_w["g"], mem_w["w"], mem_w["gk"],
                                proj_w["seg"], tm=_pick(bp * m_len, (256, 128)))
        o_mem = _memattn_call(qmb.reshape(bp, t, QM_PAD), mkvb.reshape(bp, m_len, 2 * MEM_WIDTH),
                              tq=_pick(t, (512, 256, 128)))
        tmf = _pick(n, (512, 256, 128))
        h_p, hn_p, route_p, counts_p = _finish_call(y_pool.reshape(n, POOL_WIDTH), o_nsa.reshape(n, NSA_WIDTH),
                                                    o_mem.reshape(n, MEM_WIDTH), xp2, fin_w, tm=tmf)
        tme = 256
        tmd = _pick(n, (256, 128))
        te, nv, pos = _route_tables(route_p, counts_p, tme)
        xs = _dispatch_call(pos, nv, hn_p, tm=tmd, tme=tme)

        xs2 = hs.reshape(bs, dm)
        u_s, qb_s, rows4_s, rowsw_s, _, gates_s, qmb_s = _proj_call(xs2, proj_w, seq_len=1, pos0=past_len, tm=bs)
        ext = jnp.concatenate([state_pool[l], u_s[:, None, :]], axis=1)
        y_pool_s = _pool_call(ext, pool_w["w"], pool_w["s"], tp=POOL_STATE + 1, pos0=past_len - POOL_STATE)[:, -1, :]
        cache_t = jnp.transpose(cache_kv[l], (0, 2, 3, 4, 1)).reshape(
            cache_kv.shape[1], N_KV_SLOTS, NSA_KV_HEADS * HEAD_DIM, page)
        tail_s = jnp.pad(rows4_s[:, :2 * LANES].reshape(bs, n_slab, HEAD_DIM), ((0, 0), (0, 0), (0, chunk_w - HEAD_DIM)))
        kcvc_s = _compress_sample_call(page_table, cache_t, tail_s, cmp_w)
        win_s = jnp.concatenate([cache_win[l].reshape(bs, n_win, 2 * LANES)[:, 1:], rowsw_s[:, None, :]], axis=1)
        mkv_s = cache_mem_kv[l].reshape(bs, cache_mem_kv.shape[2], 2 * MEM_WIDTH)
        oc, ow, om_s, score_s = _sample_attn1_call(qb_s.reshape(bs, 1, Q_PAD), kcvc_s, win_s, mkv_s,
                                                   qmb_s.reshape(bs, 1, QM_PAD), pool_s, t_pos=past_len)
        n_top = min(SEL_TOPK, past_len // SEL_BLOCK + 1)
        idx = _sample_topk_call(score_s.reshape(bs * NSA_KV_HEADS, n_slot_s), n_top=n_top)
        o_nsa_s = _sample_sel_call(page_table, idx, cache_t, qb_s.reshape(bs, 1, Q_PAD),
                                   rows4_s[:, 2 * LANES:].reshape(bs, 1, 2 * LANES), gates_s.reshape(bs, 1, LANES),
                                   oc, ow, t_pos=past_len, n_top=n_top)
        h_s, hn_s, route_s, _ = _finish_call(y_pool_s, o_nsa_s.reshape(bs, NSA_WIDTH), om_s.reshape(bs, MEM_WIDTH),
                                          xs2, fin_w, tm=bs)

        ys, y_s = _expert_call(te, nv, xs, wg, wu, wd, hn_s, route_s, h_s, tm=tme)
        y_p = _combine_call(pos, ys, h_p, route_p, tm=tmd).reshape(bp, t, dm)
        y_s = y_s.reshape(bs, 1, dm)

        keep = min(WINDOW, t)
        outs[0].append(jnp.transpose(rows4t.reshape(bp, N_KV_SLOTS, NSA_KV_HEADS, HEAD_DIM, t), (0, 4, 1, 2, 3)))
        outs[1].append(rows4_s.reshape(bs, 1, N_KV_SLOTS, NSA_KV_HEADS, HEAD_DIM))
        outs[2].append(rowsw.reshape(bp, t, 2, NSA_KV_HEADS, HEAD_DIM)[:, t - keep:])
        outs[3].append(win_s.reshape(bs, n_win, 2, NSA_KV_HEADS, HEAD_DIM))
        outs[4].append(u3[:, t - POOL_STATE:])
        outs[5].append(ext[:, 1:])
        outs[6].append(mkv.reshape(bp, m_len, 2, MEM_HEADS, HEAD_DIM))
        hp, hs = y_p, y_s
    return (hp, hs) + tuple(jnp.stack(o) for o in outs)
```

```python
import functools

import numpy as np
import jax
import jax.numpy as jnp
from jax import lax
from jax.experimental import pallas as pl
from jax.experimental.pallas import tpu as pltpu

F32 = jnp.float32
BF16 = jnp.bfloat16
I32 = jnp.int32

HEAD_DIM = 64
POOL_WINDOWS = (2, 4, 8, 16)
POOL_STATE = max(POOL_WINDOWS) - 1
NSA_HEADS = 8
NSA_KV_HEADS = 2
NSA_REP = NSA_HEADS // NSA_KV_HEADS
N_BRANCH = 3
CMP_BLOCK = 32
CMP_STRIDE = 16
CMP_HIDDEN = 2 * HEAD_DIM
SEL_BLOCK = 64
SEL_TOPK = 16
CMP_PER_SEL = SEL_BLOCK // CMP_STRIDE
WINDOW = 512
FORCE_BONUS = 1000.0
MEM_HEADS = 4
ROPE_DIM = HEAD_DIM // 4
ROPE_THETA = 500000.0
N_EXPERT_GROUPS = 4
EXPERTS_PER_GROUP = 8
N_EXPERTS = N_EXPERT_GROUPS * EXPERTS_PER_GROUP
EPS = 1e-6
N_KV_SLOTS = 4

LANES = 128
POOL_WIDTH = 256
NSA_WIDTH = NSA_HEADS * HEAD_DIM
NSA_KV_WIDTH = N_BRANCH * 2 * NSA_KV_HEADS * HEAD_DIM
GATE_WIDTH = NSA_HEADS * N_BRANCH
MEM_WIDTH = MEM_HEADS * HEAD_DIM
Q_PAD = NSA_HEADS * LANES
QM_PAD = MEM_HEADS * LANES
C_U = 0
C_Q = C_U + POOL_WIDTH
C_KV = C_Q + Q_PAD
C_QM = C_KV + NSA_KV_WIDTH
C_G = C_QM + QM_PAD
C_END = C_G + LANES

NEG = -1e30
VMEM_LIMIT = 48 * 1024 * 1024

_NT = (((1,), (1,)), ((), ()))


def _cparams(*sem):
    return pltpu.CompilerParams(dimension_semantics=tuple(sem), vmem_limit_bytes=VMEM_LIMIT)


def _dot(a, b):
    return jnp.dot(a, b, preferred_element_type=F32)


def _dot_nt(a, b):
    return lax.dot_general(a, b, _NT, preferred_element_type=F32)


def _rms(x):
    return x * lax.rsqrt(jnp.mean(x * x, axis=-1, keepdims=True) + EPS)


def _masked_softmax(s, mask):
    sm = jnp.where(mask, s, NEG)
    m = jnp.max(sm, axis=-1, keepdims=True)
    e = jnp.where(mask, jnp.exp(sm - m), 0.0)
    return e / jnp.maximum(jnp.sum(e, axis=-1, keepdims=True), 1e-30)


def _split3(x):
    hi = x.astype(BF16)
    r1 = x - hi.astype(F32)
    mid = r1.astype(BF16)
    lo = (r1 - mid.astype(F32)).astype(BF16)
    return hi, mid, lo


def _proj_kernel(x_ref, ga_ref, w_ref, gq_ref, gk_ref, gmq_ref, inv_ref, seg_ref,
                 u_ref, q_ref, rows4_ref, rowsw_ref, kvb_ref, gates_ref, qm_ref, *rest,
                 tm, seq_len, pos0, consecutive):
    *maybe_rows4t_ref, cos_in_tile, sin_in_tile = rest
    i = pl.program_id(0)
    a = _rms(x_ref[...]) * ga_ref[...]
    z = _dot(a.astype(BF16), w_ref[...])
    u_ref[...] = z[:, C_U:C_U + POOL_WIDTH]

    inv = inv_ref[...]

    @pl.when(i == 0)
    def _():
        j = lax.broadcasted_iota(I32, (tm, 1), 0).astype(F32) if consecutive else jnp.zeros((tm, 1), F32)
        cos_in_tile[...] = jnp.cos(j * inv)
        sin_in_tile[...] = jnp.sin(j * inv)

    first = (pos0 + (i * tm) % seq_len).astype(F32) * jnp.broadcast_to(inv, (8, LANES))
    cos_f, sin_f = jnp.cos(first)[0:1], jnp.sin(first)[0:1]
    cos = cos_f * cos_in_tile[...] - sin_f * sin_in_tile[...]
    sin = sin_f * cos_in_tile[...] + cos_f * sin_in_tile[...]
    d = lax.broadcasted_iota(I32, (1, LANES), 1) % HEAD_DIM
    half = ROPE_DIM // 2
    s_next = jnp.where(d < half, -sin, 0.0)
    s_prev = jnp.where((d >= half) & (d < ROPE_DIM), sin, 0.0)
    seg = seg_ref[...]

    def head_norm(xc, g):
        ssq = _dot((xc * xc).astype(BF16), seg)
        return xc * lax.rsqrt(ssq * (1.0 / HEAD_DIM) + EPS) * g

    def rope(xc):
        return xc * cos + pltpu.roll(xc, LANES - half, 1) * s_next + pltpu.roll(xc, half, 1) * s_prev

    scale = HEAD_DIM ** -0.5
    for c in range(NSA_HEADS):
        sl = slice(c * LANES, (c + 1) * LANES)
        qc = rope(head_norm(z[:, C_Q + c * LANES:C_Q + (c + 1) * LANES], gq_ref[:, sl]))
        q_ref[:, sl] = (qc * scale).astype(BF16)
    for br in range(N_BRANCH):
        k0 = C_KV + br * 2 * LANES
        kn = rope(head_norm(z[:, k0:k0 + LANES], gk_ref[:, br * LANES:(br + 1) * LANES]))
        vv = z[:, k0 + LANES:k0 + 2 * LANES]
        kvb_ref[:, br * 2 * LANES:br * 2 * LANES + LANES] = kn.astype(BF16)
        kvb_ref[:, br * 2 * LANES + LANES:(br + 1) * 2 * LANES] = vv.astype(BF16)
        if br < 2:
            rows4_ref[:, br * 2 * LANES:br * 2 * LANES + LANES] = kn
            rows4_ref[:, br * 2 * LANES + LANES:(br + 1) * 2 * LANES] = vv
            for rows4t_ref in maybe_rows4t_ref:
                rows4t_ref[0, br * 2 * LANES:br * 2 * LANES + LANES, :] = kn.T
                rows4t_ref[0, br * 2 * LANES + LANES:(br + 1) * 2 * LANES, :] = vv.T
        else:
            rowsw_ref[:, :LANES] = kn
            rowsw_ref[:, LANES:] = vv
    for c in range(MEM_HEADS):
        sl = slice(c * LANES, (c + 1) * LANES)
        qmc = head_norm(z[:, C_QM + c * LANES:C_QM + (c + 1) * LANES], gmq_ref[:, sl])
        qm_ref[:, sl] = (qmc * scale).astype(BF16)
    gates_ref[...] = jax.nn.sigmoid(z[:, C_G:C_END])


def _proj_call(x2d, pw, *, seq_len, pos0, tm, feature_major_rows4=False):
    n, dm = x2d.shape
    full = lambda shape: pl.BlockSpec(shape, lambda i: (0,) * len(shape))
    rows = lambda w: pl.BlockSpec((tm, w), lambda i: (i, 0))
    outs = [(POOL_WIDTH, F32), (Q_PAD, BF16), (4 * LANES, F32), (2 * LANES, F32), (NSA_KV_WIDTH, BF16),
            (LANES, F32), (QM_PAD, BF16)]
    out_specs = [rows(w) for w, _ in outs]
    out_shape = [jax.ShapeDtypeStruct((n, w), dt) for w, dt in outs]
    if feature_major_rows4:
        per_seq = seq_len // tm
        out_specs.append(pl.BlockSpec((1, 4 * LANES, tm), lambda i: (i // per_seq, 0, i % per_seq)))
        out_shape.append(jax.ShapeDtypeStruct((n // seq_len, 4 * LANES, seq_len), F32))
    consecutive = seq_len % tm == 0
    assert consecutive or seq_len == 1
    return pl.pallas_call(
        functools.partial(_proj_kernel, tm=tm, seq_len=seq_len, pos0=pos0, consecutive=consecutive),
        grid=(n // tm,),
        in_specs=[rows(dm), full((1, dm)), full((dm, C_END)), full((1, Q_PAD)), full((1, N_BRANCH * LANES)),
                  full((1, QM_PAD)), full((1, LANES)), full((LANES, LANES))],
        out_specs=out_specs,
        out_shape=out_shape,
        scratch_shapes=[pltpu.VMEM((tm, LANES), F32), pltpu.VMEM((tm, LANES), F32)],
        compiler_params=_cparams("arbitrary"),
        name="proj",
    )(x2d, pw["g_attn"], pw["w_in"], pw["gq"], pw["gk"], pw["gmq"], pw["inv"], pw["seg"])


def _pool_kernel(u_ref, halo_ref, w_ref, s_ref, y_ref, *, tp, pos0):
    i = pl.program_id(1)
    u = u_ref[0]
    halo = halo_ref[0] * (i > 0).astype(F32)
    n_h = POOL_STATE + 1
    ext = jnp.concatenate([halo, u], axis=0)
    sums = {1: ext}
    w = 1
    while w < max(POOL_WINDOWS):
        sums[2 * w] = sums[w] + pltpu.roll(sums[w], w, 0)
        w *= 2
    pos = pos0 + i * tp + lax.broadcasted_iota(I32, (tp, 1), 0)
    lane_grp = lax.broadcasted_iota(I32, (1, POOL_WIDTH), 1) // (POOL_WIDTH // len(POOL_WINDOWS))
    mean = jnp.zeros((tp, POOL_WIDTH), F32)
    for gi, wdw in enumerate(POOL_WINDOWS):
        cnt = jnp.minimum(pos + 1, wdw).astype(F32)
        mean = jnp.where(lane_grp == gi, sums[wdw][n_h:] / cnt, mean)
    r = mean - u
    y_ref[0] = _dot(r.astype(BF16), w_ref[...]) * s_ref[...]


def _pool_call(u3, w_bd, s_pool, *, tp, pos0):
    b, t, c = u3.shape
    n_h = POOL_STATE + 1
    return pl.pallas_call(
        functools.partial(_pool_kernel, tp=tp, pos0=pos0),
        grid=(b, t // tp),
        in_specs=[pl.BlockSpec((1, tp, c), lambda bi, i: (bi, i, 0)),
                  pl.BlockSpec((1, n_h, c), lambda bi, i: (bi, jnp.maximum(i * (tp // n_h) - 1, 0), 0)),
                  pl.BlockSpec((c, c), lambda bi, i: (0, 0)),
                  pl.BlockSpec((1, c), lambda bi, i: (0, 0))],
        out_specs=pl.BlockSpec((1, tp, c), lambda bi, i: (bi, i, 0)),
        out_shape=jax.ShapeDtypeStruct((b, t, c), F32),
        compiler_params=_cparams("parallel", "parallel"),
        name="pool",
    )(u3, u3, w_bd, s_pool)


def _split_chunks(x, n_chunks):
    return jnp.transpose(x.reshape(n_chunks, CMP_STRIDE, LANES), (1, 0, 2))


def _compress_core(load_slab, tail_ref, pe_ref, w_ref, w2_ref, n):
    rowid = lax.broadcasted_iota(I32, (n, 1), 0)
    outs = []
    for c in range(2):
        w = w_ref[c]
        pe8 = jnp.broadcast_to(pe_ref[c], (8, 2 * CMP_STRIDE * HEAD_DIM)).astype(BF16)
        pe_first = _dot(pe8[:, :CMP_STRIDE * HEAD_DIM], w)[0:1, :CMP_HIDDEN]
        pe_second = _dot(pe8[:, CMP_STRIDE * HEAD_DIM:], w)[0:1, CMP_HIDDEN:]
        pe_const = pe_first + pe_second
        lane = lax.broadcasted_iota(I32, (1, LANES), 1)
        per_group = [[], []]
        for q in range(CMP_STRIDE // 2):
            a, b = load_slab(c, 2 * q), load_slab(c, 2 * q + 1)
            per_group[0].append(jnp.where(lane < HEAD_DIM, a, pltpu.roll(b, HEAD_DIM, 1)))
            per_group[1].append(jnp.where(lane < HEAD_DIM, pltpu.roll(a, HEAD_DIM, 1), b))
        x2 = jnp.concatenate([jnp.concatenate(pg, axis=1) for pg in per_group], axis=0).astype(BF16)
        z = _dot(x2, w)
        hs = []
        for g in range(NSA_KV_HEADS):
            tail8 = jnp.broadcast_to(tail_ref[0, 2 * c + g:2 * c + g + 1, :], (8, CMP_STRIDE * HEAD_DIM)).astype(BF16)
            second_tail = _dot(tail8, w)[0:1, CMP_HIDDEN:]
            first = z[g * n:(g + 1) * n, :CMP_HIDDEN]
            second = z[g * n:(g + 1) * n, CMP_HIDDEN:]
            shifted = pltpu.roll(second, n - 1, 0)
            h = first + jnp.where(rowid == n - 1, second_tail, shifted) + pe_const
            hs.append(jax.nn.gelu(h, approximate=True))
        o = _dot(jnp.concatenate(hs, axis=0).astype(BF16), w2_ref[c])
        outs += [o[:n], o[n:]]
    return jnp.concatenate(outs, axis=1)


def _compress_prompt_kernel(k_ref, v_ref, tail_ref, pe_ref, w_ref, w2_ref, out_ref, *, n):
    slabs = [_split_chunks(ref[0], n) for ref in (k_ref, v_ref)]
    out_ref[0] = _compress_core(lambda c, r: slabs[c][r], tail_ref, pe_ref, w_ref, w2_ref, n).astype(BF16)


def _compress_prompt_call(rows4_3d, tail, cw):
    b, t, _ = rows4_3d.shape
    n = t // CMP_STRIDE
    full = lambda a: pl.BlockSpec(a.shape, lambda bi: (0,) * a.ndim)
    return pl.pallas_call(
        functools.partial(_compress_prompt_kernel, n=n),
        grid=(b,),
        in_specs=[pl.BlockSpec((1, t, LANES), lambda bi: (bi, 0, 0)),
                  pl.BlockSpec((1, t, LANES), lambda bi: (bi, 0, 1)),
                  pl.BlockSpec((1,) + tail.shape[1:], lambda bi: (bi, 0, 0)),
                  full(cw["pe"]), full(cw["w"]), full(cw["w2"])],
        out_specs=pl.BlockSpec((1, n, 2 * LANES), lambda bi: (bi, 0, 0)),
        out_shape=jax.ShapeDtypeStruct((b, n, 2 * LANES), BF16),
        compiler_params=_cparams("parallel"),
        name="compress_prompt",
    )(rows4_3d, rows4_3d, tail, cw["pe"], cw["w"], cw["w2"])


def _compress_sample_kernel(pt_ref, cache_ref, tail_ref, pe_ref, w_ref, w2_ref, out_ref,
                            buf, slabs, sem, *, n, n_pages, page):
    b = pl.program_id(0)
    nb = pl.num_programs(0)

    def page_copy(bb, slot, p, c):
        return pltpu.make_async_copy(cache_ref.at[pt_ref[bb * n_pages + p], c], buf.at[slot, c, p], sem.at[slot])

    def for_all(fn):
        for p in range(n_pages):
            for c in range(2):
                fn(p, c)

    def issue(bb, slot):
        for_all(lambda p, c: page_copy(bb, slot, p, c).start())

    @pl.when(b == 0)
    def _():
        issue(0, 0)

    @pl.when(b + 1 < nb)
    def _():
        issue(b + 1, (b + 1) % 2)

    slot = b % 2
    for_all(lambda p, c: page_copy(b, slot, p, c).wait())

    unroll = next(u for u in (8, 4, 2, 1) if n_pages % u == 0)

    cpp = page // CMP_STRIDE

    def to_slabs(pp, carry):
        for k in range(unroll):
            p = pp * unroll + k
            for c in range(2):
                by_r = _split_chunks(buf[slot, c, p].T, cpp)
                for r in range(CMP_STRIDE):
                    slabs[c, r, pl.ds(pl.multiple_of(p * cpp, cpp), cpp), :] = by_r[r]
        return carry

    lax.fori_loop(0, n_pages // unroll, to_slabs, 0)
    out_ref[0] = _compress_core(lambda c, r: slabs[c, r], tail_ref, pe_ref, w_ref, w2_ref, n).astype(BF16)


def _compress_sample_call(page_table, cache_t, tail, cw):
    b, n_pages = page_table.shape
    page = cache_t.shape[3]
    n = n_pages * page // CMP_STRIDE
    full = lambda a: pl.BlockSpec(a.shape, lambda bi, pt: (0,) * a.ndim)
    grid_spec = pltpu.PrefetchScalarGridSpec(
        num_scalar_prefetch=1,
        grid=(b,),
        in_specs=[pl.BlockSpec(memory_space=pl.ANY),
                  pl.BlockSpec((1,) + tail.shape[1:], lambda bi, pt: (bi, 0, 0)),
                  full(cw["pe"]), full(cw["w"]), full(cw["w2"])],
        out_specs=pl.BlockSpec((1, n, 2 * LANES), lambda bi, pt: (bi, 0, 0)),
        scratch_shapes=[pltpu.VMEM((2, 2, n_pages, LANES, page), F32), pltpu.VMEM((2, CMP_STRIDE, n, LANES), F32),
                        pltpu.SemaphoreType.DMA((2,))],
    )
    return pl.pallas_call(
        functools.partial(_compress_sample_kernel, n=n, n_pages=n_pages, page=page),
        grid_spec=grid_spec,
        out_shape=jax.ShapeDtypeStruct((b, n, 2 * LANES), BF16),
        compiler_params=_cparams("arbitrary"),
        name="compress_sample",
    )(page_table.reshape(-1), cache_t, tail, cw["pe"], cw["w"], cw["w2"])


def _topk_mask(score, ids, n_top, axis):
    sel = jnp.zeros(score.shape, jnp.bool_)
    work = score
    firsts = []
    big = float(score.shape[axis])
    for _ in range(n_top):
        m = jnp.max(work, axis=axis, keepdims=True)
        first = jnp.min(jnp.where(work == m, ids, big), axis=axis, keepdims=True)
        pick = (ids == first) & (m > -jnp.inf)
        sel = sel | pick
        work = jnp.where(pick, -jnp.inf, work)
        firsts.append(jnp.where(m > -jnp.inf, first, -1.0))
    return sel, firsts


def _gate_and_pack(o_ref_store, gates, o_c, o_s, o_w, rows):
    lane = lax.broadcasted_iota(I32, (1, LANES), 1)
    heads = []
    for h in range(NSA_HEADS):
        g, r = divmod(h, NSA_REP)
        rs = slice(r * rows, (r + 1) * rows)
        gc, gs, gw = (gates[:, N_BRANCH * h + k:N_BRANCH * h + k + 1] for k in range(N_BRANCH))
        heads.append(gc * o_c[g][rs] + gs * o_s[g][rs] + gw * o_w[g][rs])
    for j in range(NSA_HEADS // 2):
        a, b = heads[2 * j], heads[2 * j + 1]
        if (2 * j) // NSA_REP == 0:
            chunk = jnp.where(lane < HEAD_DIM, a, pltpu.roll(b, HEAD_DIM, 1))
        else:
            chunk = jnp.where(lane < HEAD_DIM, pltpu.roll(a, HEAD_DIM, 1), b)
        o_ref_store(j, chunk)


def _nsa_prompt_kernel(q_ref, gate_ref, kc_ref, kv_ref, poolt_ref, e_ref, o_ref, acc_ref, *, tq, tk, t_len):
    i = pl.program_id(1)
    s0 = i * tq
    rows4 = NSA_REP * tq
    tpos = s0 + lax.broadcasted_iota(I32, (tq, 1), 0)
    rep = lambda x: jnp.concatenate([x] * NSA_REP, axis=0)
    n_cmp = kc_ref.shape[1]
    kc128 = kc_ref[0, :, :LANES]
    vc128 = kc_ref[0, :, LANES:]
    kc_end = lax.broadcasted_iota(I32, (1, n_cmp), 1) * CMP_STRIDE + (CMP_BLOCK - 1)
    bias_c = rep(jnp.where(kc_end <= tpos, 0.0, NEG))
    any_c = rep((tpos >= CMP_BLOCK - 1).astype(F32))
    n_sel = t_len // SEL_BLOCK
    blk = lax.broadcasted_iota(I32, (n_sel, 1), 0)
    blk_f = blk.astype(F32)
    tq_lane = s0 + lax.broadcasted_iota(I32, (1, tq), 1)
    cur = tq_lane // SEL_BLOCK
    valid = blk * SEL_BLOCK <= tq_lane
    forced = (blk == 0) | (blk == cur) | (blk == cur - 1)
    w_start = pl.multiple_of(jnp.maximum(s0 - WINDOW, 0), LANES)
    w_len = WINDOW + tq
    kpos_w = w_start + lax.broadcasted_iota(I32, (1, w_len), 1)
    dist = tpos - kpos_w
    bias_w = rep(jnp.where((dist >= 0) & (dist < WINDOW), 0.0, NEG))
    c_diag = s0 // tk
    kpos_d = c_diag * tk + lax.broadcasted_iota(I32, (1, tk), 1)
    bias_causal = jnp.where(kpos_d <= tpos, 0.0, NEG)

    o_c, o_s, o_w, qgs, selbs = [], [], [], [], []
    for g in range(NSA_KV_HEADS):
        qg = jnp.concatenate([q_ref[0, :, (g * NSA_REP + r) * LANES:(g * NSA_REP + r + 1) * LANES]
                              for r in range(NSA_REP)], axis=0)
        qgs.append(qg)
        s_c = _dot_nt(qg, kc128) + bias_c
        e_c = jnp.exp(s_c - jnp.max(s_c, axis=-1, keepdims=True))
        p_c = e_c * (any_c / jnp.sum(e_c, axis=-1, keepdims=True))
        o_c.append(_dot(p_c.astype(BF16), vc128))
        imp = p_c[0:tq]
        for r in range(1, NSA_REP):
            imp = imp + p_c[r * tq:(r + 1) * tq]
        imp_t = sum(_dot_nt(poolt_ref[...], piece) for piece in _split3(imp))
        score = jnp.where(valid, imp_t + jnp.where(forced, FORCE_BONUS, 0.0), -jnp.inf)
        sel_t, _ = _topk_mask(score, blk_f, min(SEL_TOPK, n_sel), 0)
        selb_t = jnp.concatenate([jnp.where(sel_t, 0.0, NEG), jnp.full(((-n_sel) % LANES, tq), NEG, F32)], axis=0)
        selbs.append(selb_t.T.astype(BF16))

        kw = kv_ref[0, pl.ds(w_start, w_len), 4 * LANES:5 * LANES]
        vw = kv_ref[0, pl.ds(w_start, w_len), 5 * LANES:6 * LANES]
        s_w = _dot_nt(qg, kw) + bias_w
        e_w = jnp.exp(s_w - jnp.max(s_w, axis=-1, keepdims=True))
        o_w.append(_dot(e_w.astype(BF16), vw) * (1.0 / jnp.sum(e_w, axis=-1, keepdims=True)))

    acc_ref[...] = jnp.zeros(acc_ref.shape, F32)

    q_all = jnp.concatenate(qgs, axis=0)
    selb_all = jnp.concatenate(selbs, axis=0)
    rows8 = NSA_KV_HEADS * rows4

    def chunk_step(c, carry, extra_bias):
        m, l = carry
        k0 = pl.multiple_of(c * tk, tk)
        ks = kv_ref[0, pl.ds(k0, tk), 2 * LANES:3 * LANES]
        vs = kv_ref[0, pl.ds(k0, tk), 3 * LANES:4 * LANES]
        bias = _dot(selb_all, e_ref[c])
        if extra_bias is not None:
            bias = bias + jnp.concatenate([extra_bias] * NSA_KV_HEADS, axis=0)
        bias_rows = jnp.concatenate([rep(bias[g * tq:(g + 1) * tq]) for g in range(NSA_KV_HEADS)], axis=0)
        s = _dot_nt(q_all, ks) + bias_rows
        m_new = jnp.maximum(m, jnp.max(s, axis=-1, keepdims=True))
        alpha = jnp.exp(m - m_new)
        p = jnp.exp(s - m_new)
        l_new = alpha * l + jnp.sum(p, axis=-1, keepdims=True)
        acc_ref[...] = alpha * acc_ref[...] + _dot(p.astype(BF16), vs)
        return m_new, l_new

    init = (jnp.full((rows8, 1), NEG, F32), jnp.zeros((rows8, 1), F32))
    carry = lax.fori_loop(0, c_diag, lambda c, cr: chunk_step(c, cr, None), init)
    _, l_fin = chunk_step(c_diag, carry, bias_causal)
    o_all = acc_ref[...] * (1.0 / jnp.maximum(l_fin, 1e-30))
    for g in range(NSA_KV_HEADS):
        o_s.append(o_all[g * rows4:(g + 1) * rows4])

    def store(j, chunk):
        o_ref[0, :, j * LANES:(j + 1) * LANES] = chunk

    _gate_and_pack(store, gate_ref[0], o_c, o_s, o_w, tq)


def _nsa_prompt_call(q3, gates3, kcvc, kvb3, poolt, e_mat, *, tq, tk):
    b, t, _ = q3.shape
    n_cmp = kcvc.shape[1]
    return pl.pallas_call(
        functools.partial(_nsa_prompt_kernel, tq=tq, tk=tk, t_len=t),
        grid=(b, t // tq),
        in_specs=[pl.BlockSpec((1, tq, Q_PAD), lambda bi, i: (bi, i, 0)),
                  pl.BlockSpec((1, tq, LANES), lambda bi, i: (bi, i, 0)),
                  pl.BlockSpec((1, n_cmp, 2 * LANES), lambda bi, i: (bi, 0, 0)),
                  pl.BlockSpec((1, t, NSA_KV_WIDTH), lambda bi, i: (bi, 0, 0)),
                  pl.BlockSpec(poolt.shape, lambda bi, i: (0, 0)),
                  pl.BlockSpec(e_mat.shape, lambda bi, i: (0, 0, 0))],
        out_specs=pl.BlockSpec((1, tq, NSA_WIDTH), lambda bi, i: (bi, i, 0)),
        out_shape=jax.ShapeDtypeStruct((b, t, NSA_WIDTH), F32),
        scratch_shapes=[pltpu.VMEM((NSA_KV_HEADS * NSA_REP * tq, LANES), F32)],
        compiler_params=_cparams("parallel", "arbitrary"),
        name="nsa_prompt",
    )(q3, gates3, kcvc, kvb3, poolt, e_mat)


def _memkv_kernel(x_ref, g_ref, w_ref, gk_ref, seg_ref, o_ref, ob_ref):
    a = _rms(x_ref[...]) * g_ref[...]
    z = _dot(a.astype(BF16), w_ref[...])
    for c in range(MEM_WIDTH // LANES):
        sl = slice(c * LANES, (c + 1) * LANES)
        xc = z[:, sl]
        ssq = _dot((xc * xc).astype(BF16), seg_ref[...])
        kn = xc * lax.rsqrt(ssq * (1.0 / HEAD_DIM) + EPS) * gk_ref[:, sl]
        o_ref[:, sl] = kn
        ob_ref[:, sl] = kn.astype(BF16)
    o_ref[:, MEM_WIDTH:] = z[:, MEM_WIDTH:]
    ob_ref[:, MEM_WIDTH:] = z[:, MEM_WIDTH:].astype(BF16)


def _memkv_call(mem2d, g_mem, w_mem, gmk, seg, *, tm):
    n, dm = mem2d.shape
    full = lambda shape: pl.BlockSpec(shape, lambda i: (0,) * len(shape))
    return pl.pallas_call(
        _memkv_kernel,
        grid=(n // tm,),
        in_specs=[pl.BlockSpec((tm, dm), lambda i: (i, 0)), full((1, dm)), full((dm, 2 * MEM_WIDTH)),
                  full((1, MEM_WIDTH)), full((LANES, LANES))],
        out_specs=[pl.BlockSpec((tm, 2 * MEM_WIDTH), lambda i: (i, 0))] * 2,
        out_shape=[jax.ShapeDtypeStruct((n, 2 * MEM_WIDTH), F32), jax.ShapeDtypeStruct((n, 2 * MEM_WIDTH), BF16)],
        compiler_params=_cparams("parallel"),
        name="memkv",
    )(mem2d, g_mem, w_mem, gmk, seg)


def _mem_attend(qm, mkv, rows):
    lane = lax.broadcasted_iota(I32, (1, LANES), 1)
    chunks = []
    for j in range(MEM_HEADS // 2):
        k128 = mkv[:, j * LANES:(j + 1) * LANES]
        v128 = mkv[:, MEM_WIDTH + j * LANES:MEM_WIDTH + (j + 1) * LANES]
        q2 = jnp.concatenate([qm[2 * j], qm[2 * j + 1]], axis=0)
        s = _dot_nt(q2, k128)
        e = jnp.exp(s - jnp.max(s, axis=-1, keepdims=True))
        p = e / jnp.sum(e, axis=-1, keepdims=True)
        o = _dot(p.astype(BF16), v128)
        chunks.append(jnp.where(lane < HEAD_DIM, o[:rows], o[rows:2 * rows]))
    return chunks


def _memattn_kernel(qm_ref, mkv_ref, o_ref, *, tq):
    qm = [qm_ref[0, :, h * LANES:(h + 1) * LANES] for h in range(MEM_HEADS)]
    for j, chunk in enumerate(_mem_attend(qm, mkv_ref[0], tq)):
        o_ref[0, :, j * LANES:(j + 1) * LANES] = chunk


def _memattn_call(qm3, mkv3, *, tq):
    b, t, _ = qm3.shape
    m = mkv3.shape[1]
    return pl.pallas_call(
        functools.partial(_memattn_kernel, tq=tq),
        grid=(b, t // tq),
        in_specs=[pl.BlockSpec((1, tq, QM_PAD), lambda bi, i: (bi, i, 0)),
                  pl.BlockSpec((1, m, 2 * MEM_WIDTH), lambda bi, i: (bi, 0, 0))],
        out_specs=pl.BlockSpec((1, tq, MEM_WIDTH), lambda bi, i: (bi, i, 0)),
        out_shape=jax.ShapeDtypeStruct((b, t, MEM_WIDTH), F32),
        compiler_params=_cparams("parallel", "parallel"),
        name="memattn",
    )(qm3, mkv3)


def _pad_rows(rows_list):
    x = jnp.concatenate(rows_list, axis=0)
    return jnp.concatenate([x, jnp.zeros((8 - x.shape[0], x.shape[1]), x.dtype)], axis=0)


def _sample_attn1_kernel(q_ref, kc_ref, win_ref, mkv_ref, qm_ref, pool_ref,
                         oc_ref, ow_ref, om_ref, score_ref, *, t_pos, n_win):
    q = q_ref[0].astype(F32)
    n_cmp = kc_ref.shape[1]
    kc128 = kc_ref[0, :, :LANES]
    vc128 = kc_ref[0, :, LANES:]
    kw128 = win_ref[0, :, :LANES].astype(BF16)
    vw128 = win_ref[0, :, LANES:].astype(BF16)
    kc_end = lax.broadcasted_iota(I32, (1, n_cmp), 1) * CMP_STRIDE + (CMP_BLOCK - 1)
    mask_c = kc_end <= t_pos
    kw_pos = t_pos - (n_win - 1) + lax.broadcasted_iota(I32, (1, n_win), 1)
    mask_w = (t_pos - kw_pos >= 0) & (t_pos - kw_pos < WINDOW) & (kw_pos >= 0)
    n_slot = pool_ref.shape[1]
    blk = lax.broadcasted_iota(I32, (1, n_slot), 1)
    cur = t_pos // SEL_BLOCK
    valid = blk * SEL_BLOCK <= t_pos
    forced = (blk == 0) | (blk == cur) | (blk == cur - 1)
    for g in range(NSA_KV_HEADS):
        qg = _pad_rows([q[:, (g * NSA_REP + r) * LANES:(g * NSA_REP + r + 1) * LANES]
                        for r in range(NSA_REP)]).astype(BF16)
        p_c = _masked_softmax(_dot_nt(qg, kc128), mask_c)
        oc_ref[0, g * NSA_REP:(g + 1) * NSA_REP, :] = _dot(p_c.astype(BF16), vc128)[:NSA_REP]
        imp = jnp.sum(p_c[:NSA_REP], axis=0, keepdims=True)
        imp8 = jnp.broadcast_to(imp, (8, n_cmp))
        imp_b = sum(_dot(piece, pool_ref[...]) for piece in _split3(imp8))[0:1]
        score_ref[0, g:g + 1, :] = jnp.where(valid, imp_b + jnp.where(forced, FORCE_BONUS, 0.0), -jnp.inf)
        p_w = _masked_softmax(_dot_nt(qg, kw128), mask_w)
        ow_ref[0, g * NSA_REP:(g + 1) * NSA_REP, :] = _dot(p_w.astype(BF16), vw128)[:NSA_REP]
    qm = qm_ref[0].astype(F32)
    qmh = []
    for h in range(MEM_HEADS):
        row = qm[:, h * LANES:(h + 1) * LANES]
        qmh.append(jnp.concatenate([row, jnp.zeros((7, LANES), F32)], axis=0).astype(BF16))
    for j, chunk in enumerate(_mem_attend(qmh, mkv_ref[0].astype(BF16), 8)):
        om_ref[0, :, j * LANES:(j + 1) * LANES] = chunk[0:1]


def _sample_attn1_call(q3, kcvc, win3, mkv3, qm3, pool_mat, *, t_pos):
    b = q3.shape[0]
    n_cmp = kcvc.shape[1]
    n_win = win3.shape[1]
    m = mkv3.shape[1]
    per_b = lambda shape: pl.BlockSpec((1,) + shape, lambda bi: (bi, 0, 0))
    return pl.pallas_call(
        functools.partial(_sample_attn1_kernel, t_pos=t_pos, n_win=n_win),
        grid=(b,),
        in_specs=[per_b((1, Q_PAD)), per_b((n_cmp, 2 * LANES)), per_b((n_win, 2 * LANES)),
                  per_b((m, 2 * MEM_WIDTH)), per_b((1, QM_PAD)),
                  pl.BlockSpec(pool_mat.shape, lambda bi: (0, 0))],
        out_specs=[per_b((NSA_HEADS, LANES)), per_b((NSA_HEADS, LANES)), per_b((1, MEM_WIDTH)),
                   per_b((NSA_KV_HEADS, pool_mat.shape[1]))],
        out_shape=[jax.ShapeDtypeStruct((b, NSA_HEADS, LANES), F32), jax.ShapeDtypeStruct((b, NSA_HEADS, LANES), F32),
                   jax.ShapeDtypeStruct((b, 1, MEM_WIDTH), F32),
                   jax.ShapeDtypeStruct((b, NSA_KV_HEADS, pool_mat.shape[1]), F32)],
        compiler_params=_cparams("parallel"),
        name="sample_attn1",
    )(q3, kcvc, win3, mkv3, qm3, pool_mat)


def _sample_topk_kernel(score_ref, idx_ref, *, n_top):
    score = score_ref[...]
    ids = lax.broadcasted_iota(I32, (1, score.shape[1]), 1).astype(F32)
    _, firsts = _topk_mask(score, ids, n_top, 1)
    lane = lax.broadcasted_iota(I32, (1, LANES), 1)
    idx = jnp.full((score.shape[0], LANES), -1, I32)
    for j, f in enumerate(firsts):
        idx = jnp.where(lane == j, f.astype(I32), idx)
    idx_ref[...] = idx


def _sample_topk_call(score2d, *, n_top):
    rows = score2d.shape[0]
    return pl.pallas_call(
        functools.partial(_sample_topk_kernel, n_top=n_top),
        out_shape=jax.ShapeDtypeStruct((rows, LANES), I32),
        compiler_params=pltpu.CompilerParams(vmem_limit_bytes=VMEM_LIMIT),
        name="sample_topk",
    )(score2d)


def _sample_sel_kernel(pt_ref, idx_ref, cache_ref, q_ref, knew_ref, gate_ref, oc_ref, ow_ref, o_ref,
                       buf, sem, *, t_pos, n_pages, n_top, page):
    b = pl.program_id(0)
    nb = pl.num_programs(0)
    blk_per_page = page // SEL_BLOCK
    n_past_blk = n_pages * blk_per_page
    idx_stride = NSA_KV_HEADS * LANES

    def blk_at(bb, g, j):
        return idx_ref[bb * idx_stride + g * LANES + j]

    def blk_copy(bb, slot, g, j, kv):
        blkc = jnp.clip(blk_at(bb, g, j), 0, n_past_blk - 1)
        pg = pt_ref[bb * n_pages + blkc // blk_per_page]
        return pltpu.make_async_copy(cache_ref.at[pg, 2 + kv], buf.at[slot, g, kv, j], sem.at[slot])

    def for_all(fn):
        for g in range(NSA_KV_HEADS):
            for j in range(n_top):
                for kv in range(2):
                    fn(g, j, kv)

    def issue(bb, slot):
        for_all(lambda g, j, kv: blk_copy(bb, slot, g, j, kv).start())

    @pl.when(b == 0)
    def _():
        issue(0, 0)

    @pl.when(b + 1 < nb)
    def _():
        issue(b + 1, (b + 1) % 2)

    slot = b % 2
    for_all(lambda g, j, kv: blk_copy(b, slot, g, j, kv).wait())

    q = q_ref[0].astype(F32)
    knew = knew_ref[0]
    k_new = knew[:, :LANES].astype(BF16).astype(F32)
    v_new = knew[:, LANES:].astype(BF16).astype(F32)
    n_keys = n_top * page
    key_lane = lax.broadcasted_iota(I32, (1, n_keys), 1)
    key_slot = key_lane // page
    key_row = key_lane % page
    cur_blk = t_pos // SEL_BLOCK
    o_s = []
    for g in range(NSA_KV_HEADS):
        qg = _pad_rows([q[:, (g * NSA_REP + r) * LANES:(g * NSA_REP + r + 1) * LANES] for r in range(NSA_REP)])
        blkvec = jnp.full((1, n_keys), -1, I32)
        has_cur = jnp.zeros((1, 1), jnp.bool_)
        for j in range(n_top):
            bj = blk_at(b, g, j)
            blkvec = jnp.where(key_slot == j, bj, blkvec)
            has_cur = has_cur | (bj == cur_blk)
        in_blk = key_row // SEL_BLOCK == blkvec % blk_per_page
        key_pos = (blkvec // blk_per_page) * page + key_row
        vis = (blkvec >= 0) & (blkvec < n_past_blk) & in_blk & (key_pos <= t_pos)
        kt = jnp.concatenate([buf[slot, g, 0, j] for j in range(n_top)], axis=1).astype(BF16)
        vt = jnp.concatenate([buf[slot, g, 1, j] for j in range(n_top)], axis=1).astype(BF16)
        s_past = jnp.where(vis, _dot(qg.astype(BF16), kt), NEG)
        s_new = jnp.where(has_cur, jnp.sum(qg * k_new, axis=-1, keepdims=True), NEG)
        m = jnp.maximum(jnp.max(s_past, axis=-1, keepdims=True), s_new)
        e_p = jnp.where(vis, jnp.exp(s_past - m), 0.0)
        e_n = jnp.where(has_cur, jnp.exp(s_new - m), 0.0)
        den = jnp.maximum(jnp.sum(e_p, axis=-1, keepdims=True) + e_n, 1e-30)
        num = _dot_nt(e_p.astype(BF16), vt) + e_n.astype(BF16).astype(F32) * v_new
        o_s.append(num / den)
    o_c = [jnp.concatenate([oc_ref[0, g * NSA_REP:(g + 1) * NSA_REP, :]] * 2, axis=0) for g in range(NSA_KV_HEADS)]
    o_w = [jnp.concatenate([ow_ref[0, g * NSA_REP:(g + 1) * NSA_REP, :]] * 2, axis=0) for g in range(NSA_KV_HEADS)]

    def store(j, chunk):
        o_ref[0, :, j * LANES:(j + 1) * LANES] = chunk

    _gate_and_pack(store, gate_ref[0], o_c, o_s, o_w, 1)


def _sample_sel_call(page_table, idx, cache_t, q3, knew3, gates3, oc, ow, *, t_pos, n_top):
    b, n_pages = page_table.shape
    page = cache_t.shape[3]
    per_b = lambda shape: pl.BlockSpec((1,) + shape, lambda bi, pt, ix: (bi, 0, 0))
    grid_spec = pltpu.PrefetchScalarGridSpec(
        num_scalar_prefetch=2,
        grid=(b,),
        in_specs=[pl.BlockSpec(memory_space=pl.ANY), per_b((1, Q_PAD)), per_b((1, 2 * LANES)), per_b((1, LANES)),
                  per_b((NSA_HEADS, LANES)), per_b((NSA_HEADS, LANES))],
        out_specs=per_b((1, NSA_WIDTH)),
        scratch_shapes=[pltpu.VMEM((2, NSA_KV_HEADS, 2, n_top, LANES, page), F32), pltpu.SemaphoreType.DMA((2,))],
    )
    return pl.pallas_call(
        functools.partial(_sample_sel_kernel, t_pos=t_pos, n_pages=n_pages, n_top=n_top, page=page),
        grid_spec=grid_spec,
        out_shape=jax.ShapeDtypeStruct((b, 1, NSA_WIDTH), F32),
        compiler_params=_cparams("arbitrary"),
        name="sample_sel",
    )(page_table.reshape(-1), idx.reshape(-1), cache_t, q3, knew3, gates3, oc, ow)


def _finish_kernel(yp_ref, on_ref, om_ref, x_ref, gm_ref, wo_ref, gf_ref, wr_ref, br_ref, tri_ref,
                   h_ref, hn_ref, route_ref, counts_ref, cnt_ref):
    gm = gm_ref[...]
    o1 = POOL_WIDTH
    o2 = o1 + NSA_WIDTH
    mixed = jnp.concatenate([_rms(yp_ref[...]) * gm[:, :o1], _rms(on_ref[...]) * gm[:, o1:o2],
                             _rms(om_ref[...]) * gm[:, o2:]], axis=-1)
    h = x_ref[...] + _dot(mixed.astype(BF16), wo_ref[...])
    h_ref[...] = h
    hn = _rms(h) * gf_ref[...]
    hn_ref[...] = hn
    logits = _dot(hn.astype(BF16), wr_ref[...]) + br_ref[...]
    lane = lax.broadcasted_iota(I32, (1, LANES), 1)
    lane_f = lane.astype(F32)
    is1 = lane < N_EXPERT_GROUPS
    m1 = jnp.max(jnp.where(is1, logits, -jnp.inf), axis=-1, keepdims=True)
    e1 = jnp.where(is1, jnp.exp(logits - m1), 0.0)
    p1 = e1 / jnp.sum(e1, axis=-1, keepdims=True)
    top1_p = jnp.max(p1, axis=-1, keepdims=True)
    grp = jnp.min(jnp.where((p1 == top1_p) & is1, lane_f, float(LANES)), axis=-1, keepdims=True)
    base = N_EXPERT_GROUPS + grp * EXPERTS_PER_GROUP
    in_g = (lane_f >= base) & (lane_f < base + EXPERTS_PER_GROUP)
    l2 = jnp.where(in_g, logits, -jnp.inf)
    v0 = jnp.max(l2, axis=-1, keepdims=True)
    i0 = jnp.min(jnp.where(l2 == v0, lane_f, float(LANES)), axis=-1, keepdims=True)
    l2b = jnp.where(lane_f == i0, -jnp.inf, l2)
    v1 = jnp.max(l2b, axis=-1, keepdims=True)
    i1 = jnp.min(jnp.where(l2b == v1, lane_f, float(LANES)), axis=-1, keepdims=True)
    ex = jnp.exp(v1 - v0)
    w0 = top1_p / (1.0 + ex)
    w1 = top1_p * ex / (1.0 + ex)
    ex0 = i0 - N_EXPERT_GROUPS
    ex1 = i1 - N_EXPERT_GROUPS
    is0 = lane_f == ex0
    is1e = lane_f == ex1
    oh0 = jnp.where(is0, 1.0, 0.0)
    oh1 = jnp.where(is1e, 1.0, 0.0)
    before0 = _dot(tri_ref[...], oh0.astype(BF16))
    before1 = _dot(tri_ref[...], oh1.astype(BF16))
    tot0 = jnp.sum(oh0, axis=0, keepdims=True)
    tot1 = jnp.sum(oh1, axis=0, keepdims=True)

    @pl.when(pl.program_id(0) == 0)
    def _():
        cnt_ref[...] = jnp.zeros(cnt_ref.shape, F32)

    seen = cnt_ref[...]
    rank0 = jnp.sum(jnp.where(is0, before0 + seen, 0.0), axis=-1, keepdims=True)
    rank1 = jnp.sum(jnp.where(is1e, before1 + seen + tot0, 0.0), axis=-1, keepdims=True)
    cnt_ref[...] = seen + tot0 + tot1
    counts_ref[...] = seen + tot0 + tot1
    route = jnp.where(lane == 0, ex0, jnp.where(lane == 1, ex1, jnp.where(lane == 2, w0, jnp.where(lane == 3, w1,
            jnp.where(lane == 4, rank0, jnp.where(lane == 5, rank1, 0.0))))))
    route_ref[...] = route


def _finish_call(yp, on, om, x2d, fw, *, tm):
    n, dm = x2d.shape
    full = lambda shape: pl.BlockSpec(shape, lambda i: (0,) * len(shape))
    rows = lambda w: pl.BlockSpec((tm, w), lambda i: (i, 0))
    tri = jnp.asarray(np.arange(tm)[None, :] < np.arange(tm)[:, None], BF16)
    return pl.pallas_call(
        _finish_kernel,
        grid=(n // tm,),
        in_specs=[rows(POOL_WIDTH), rows(NSA_WIDTH), rows(MEM_WIDTH), rows(dm), full((1, dm)), full((dm, dm)),
                  full((1, dm)), full((dm, LANES)), full((1, LANES)), full((tm, tm))],
        out_specs=[rows(dm), rows(dm), rows(LANES), full((1, LANES))],
        out_shape=[jax.ShapeDtypeStruct((n, dm), F32), jax.ShapeDtypeStruct((n, dm), F32),
                   jax.ShapeDtypeStruct((n, LANES), F32), jax.ShapeDtypeStruct((1, LANES), F32)],
        scratch_shapes=[pltpu.VMEM((1, LANES), F32)],
        compiler_params=_cparams("arbitrary"),
        name="finish",
    )(yp, on, om, x2d, fw["g_mix"], fw["w_out"], fw["g_ffn"], fw["w_r"], fw["b_r"], tri)


def _route_tables(route, counts, tm):
    n = route.shape[0]
    eid = route[:, 0:2].astype(I32)
    rank = route[:, 4:6].astype(I32)
    cnt = counts[0, :N_EXPERTS].astype(I32)
    tiles_per = jnp.maximum((cnt + tm - 1) // tm, 1)
    tile_end = jnp.cumsum(tiles_per)
    tile_start = tile_end - tiles_per
    experts = jnp.arange(N_EXPERTS, dtype=I32)
    start_of = jnp.sum(jnp.where(eid[:, :, None] == experts, tile_start, 0), axis=-1)
    pos = (start_of * tm + rank).reshape(-1)
    n_tiles = (2 * n) // tm + N_EXPERTS
    tj = jnp.arange(n_tiles, dtype=I32)
    tile_expert = jnp.minimum(jnp.sum((tj[:, None] >= tile_end[None, :]).astype(I32), axis=1), N_EXPERTS - 1)
    sel = tile_expert[:, None] == experts
    rows_left = jnp.sum(jnp.where(sel, cnt, 0), axis=-1) - (tj - jnp.sum(jnp.where(sel, tile_start, 0), axis=-1)) * tm
    tile_nvalid = jnp.where(tj < tile_end[-1], jnp.clip(rows_left, 0, tm), 0).astype(I32)
    return tile_expert, tile_nvalid, pos


def _row_wait_all(src_row, dst_row, sem, count):
    for _ in range(count):
        pltpu.make_async_copy(src_row, dst_row, sem).wait()


def _dispatch_kernel(pos_ref, nv_ref, hn_ref, xs_ref, stage, zbuf, sem, zsem, *, tm, tme, n_tiles):
    i = pl.program_id(0)
    nt = pl.num_programs(0)
    slot = i % 2
    wait_slot = lambda s: _row_wait_all(stage.at[s, pl.ds(0, 1), :], xs_ref.at[pl.ds(0, 1), :], sem.at[s], 2 * tm)

    @pl.when(i == 0)
    def _():
        zbuf[...] = jnp.zeros(zbuf.shape, F32)

        def fill(j, c):
            @pl.when(nv_ref[j] < tme)
            def _():
                pltpu.make_async_copy(zbuf, xs_ref.at[pl.ds(pl.multiple_of(j * tme, tme), tme), :], zsem).start()
            return c

        def drain(j, c):
            @pl.when(nv_ref[j] < tme)
            def _():
                pltpu.make_async_copy(zbuf, xs_ref.at[pl.ds(0, tme), :], zsem).wait()
            return c

        lax.fori_loop(0, n_tiles, fill, 0)
        lax.fori_loop(0, n_tiles, drain, 0)

    @pl.when(i >= 2)
    def _():
        wait_slot(slot)

    stage[slot] = hn_ref[...]
    base = i * (2 * tm)
    for r in range(tm):
        for k in range(2):
            dst = pos_ref[base + 2 * r + k]
            pltpu.make_async_copy(stage.at[slot, pl.ds(r, 1), :], xs_ref.at[pl.ds(dst, 1), :], sem.at[slot]).start()

    @pl.when(i == nt - 1)
    def _():
        wait_slot(slot)

        @pl.when(i >= 1)
        def _():
            wait_slot(1 - slot)


def _dispatch_call(pos, tile_nvalid, hn, *, tm, tme):
    n, dm = hn.shape
    n_tiles = tile_nvalid.shape[0]
    grid_spec = pltpu.PrefetchScalarGridSpec(
        num_scalar_prefetch=2,
        grid=(n // tm,),
        in_specs=[pl.BlockSpec((tm, dm), lambda i, pos, nv: (i, 0))],
        out_specs=pl.BlockSpec(memory_space=pl.ANY),
        scratch_shapes=[pltpu.VMEM((2, tm, dm), F32), pltpu.VMEM((tme, dm), F32),
                        pltpu.SemaphoreType.DMA((2,)), pltpu.SemaphoreType.DMA(())],
    )
    return pl.pallas_call(
        functools.partial(_dispatch_kernel, tm=tm, tme=tme, n_tiles=n_tiles),
        grid_spec=grid_spec,
        out_shape=jax.ShapeDtypeStruct((n_tiles * tme, dm), F32),
        compiler_params=_cparams("arbitrary"),
        name="moe_dispatch",
    )(pos, tile_nvalid, hn)


def _expert_kernel(te_ref, nv_ref, x_ref, wg_ref, wu_ref, wd_ref, hns_ref, routes_ref, hs_ref,
                   y_ref, ysample_ref, wgb, wub, wdb, acc_s, *, tm):
    i = pl.program_id(0)
    prev = te_ref[jnp.maximum(i - 1, 0)]

    @pl.when(i == 0)
    def _():
        acc_s[...] = jnp.zeros(acc_s.shape, F32)

    @pl.when((i == 0) | (te_ref[i] != prev))
    def _():
        wgb[...] = wg_ref[0].astype(BF16)
        wub[...] = wu_ref[0].astype(BF16)
        wdb[...] = wd_ref[0].astype(BF16)
        route = routes_ref[...]
        ef = te_ref[i].astype(F32)
        comb = jnp.where(route[:, 0:1] == ef, route[:, 2:3], 0.0) + jnp.where(route[:, 1:2] == ef, route[:, 3:4], 0.0)
        xs = hns_ref[...].astype(BF16)
        hg = _dot(xs, wgb[...])
        hu = _dot(xs, wub[...])
        hmid = hg * jax.nn.sigmoid(hg) * hu * comb
        acc_s[...] += _dot(hmid.astype(BF16), wdb[...])

    @pl.when(i == pl.num_programs(0) - 1)
    def _():
        ysample_ref[...] = hs_ref[...] + acc_s[...]

    nv = nv_ref[i]

    @pl.when(nv == 0)
    def _():
        y_ref[...] = jnp.zeros(y_ref.shape, F32)

    @pl.when(nv > 0)
    def _():
        x = x_ref[...].astype(BF16)
        hg = _dot(x, wgb[...])
        hu = _dot(x, wub[...])
        hmid = hg * jax.nn.sigmoid(hg) * hu
        y_ref[...] = _dot(hmid.astype(BF16), wdb[...])


def _expert_call(tile_expert, tile_nvalid, xs, wg, wu, wd, hn_s, route_s, h_s, *, tm):
    n_tiles = tile_expert.shape[0]
    dm = xs.shape[1]
    ff = wg.shape[2]
    ns = hn_s.shape[0]
    full = lambda shape: pl.BlockSpec(shape, lambda i, te, nv: (0,) * len(shape))
    grid_spec = pltpu.PrefetchScalarGridSpec(
        num_scalar_prefetch=2,
        grid=(n_tiles,),
        in_specs=[pl.BlockSpec((tm, dm), lambda i, te, nv: (i, 0)),
                  pl.BlockSpec((1, dm, ff), lambda i, te, nv: (te[i], 0, 0)),
                  pl.BlockSpec((1, dm, ff), lambda i, te, nv: (te[i], 0, 0)),
                  pl.BlockSpec((1, ff, dm), lambda i, te, nv: (te[i], 0, 0)),
                  full((ns, dm)), full((ns, LANES)), full((ns, dm))],
        out_specs=[pl.BlockSpec((tm, dm), lambda i, te, nv: (i, 0)), full((ns, dm))],
        scratch_shapes=[pltpu.VMEM((dm, ff), BF16), pltpu.VMEM((dm, ff), BF16), pltpu.VMEM((ff, dm), BF16),
                        pltpu.VMEM((ns, dm), F32)],
    )
    return pl.pallas_call(
        functools.partial(_expert_kernel, tm=tm),
        grid_spec=grid_spec,
        out_shape=[jax.ShapeDtypeStruct(xs.shape, F32), jax.ShapeDtypeStruct((ns, dm), F32)],
        compiler_params=_cparams("arbitrary"),
        name="moe_experts",
    )(tile_expert, tile_nvalid, xs, wg, wu, wd, hn_s, route_s, h_s)


def _combine_kernel(pos_ref, ys_ref, h_ref, route_ref, o_ref, gbuf, sem, *, tm):
    i = pl.program_id(0)
    nt = pl.num_programs(0)
    slot = i % 2

    def issue(step, s):
        base = step * (2 * tm)
        for r in range(tm):
            for k in range(2):
                src = pos_ref[base + 2 * r + k]
                pltpu.make_async_copy(ys_ref.at[pl.ds(src, 1), :], gbuf.at[s, k, pl.ds(r, 1), :], sem.at[s]).start()

    @pl.when(i == 0)
    def _():
        issue(0, 0)

    @pl.when(i + 1 < nt)
    def _():
        issue(i + 1, 1 - slot)

    _row_wait_all(ys_ref.at[pl.ds(0, 1), :], gbuf.at[slot, 0, pl.ds(0, 1), :], sem.at[slot], 2 * tm)
    w0 = route_ref[:, 2:3]
    w1 = route_ref[:, 3:4]
    o_ref[...] = h_ref[...] + (w0 * gbuf[slot, 0] + w1 * gbuf[slot, 1])


def _combine_call(pos, ys, h, route, *, tm):
    n, dm = h.shape
    grid_spec = pltpu.PrefetchScalarGridSpec(
        num_scalar_prefetch=1,
        grid=(n // tm,),
        in_specs=[pl.BlockSpec(memory_space=pl.ANY),
                  pl.BlockSpec((tm, dm), lambda i, pos: (i, 0)),
                  pl.BlockSpec((tm, LANES), lambda i, pos: (i, 0))],
        out_specs=pl.BlockSpec((tm, dm), lambda i, pos: (i, 0)),
        scratch_shapes=[pltpu.VMEM((2, 2, tm, dm), F32), pltpu.SemaphoreType.DMA((2,))],
    )
    return pl.pallas_call(
        functools.partial(_combine_kernel, tm=tm),
        grid_spec=grid_spec,
        out_shape=jax.ShapeDtypeStruct((n, dm), F32),
        compiler_params=_cparams("arbitrary"),
        name="moe_combine",
    )(pos, ys, h, route)


def _prep_weights(l, g_attn, w_in, g_q, g_k, pe_cmp, w_cmp1, w_cmp2, w_pool, s_pool, g_mem, w_mem_kv, g_mq, g_mk,
                  g_mix, w_out, g_ffn, w_router1, b_router1, w_router2, b_router2):
    dm = w_in.shape[1]
    w = w_in[l]
    o1 = POOL_WIDTH
    o2 = o1 + NSA_WIDTH
    o3 = o2 + NSA_KV_WIDTH
    o4 = o3 + GATE_WIDTH
    wq = w[:, o1:o2].reshape(dm, NSA_HEADS, 1, HEAD_DIM)
    q_slot = jnp.asarray(np.eye(NSA_KV_HEADS, dtype=np.float32)[np.arange(NSA_HEADS) // NSA_REP])
    wq_pad = (wq * q_slot[None, :, :, None]).reshape(dm, Q_PAD)
    wqm = w[:, o4:].reshape(dm, MEM_HEADS, 1, HEAD_DIM)
    m_slot = jnp.asarray(np.eye(2, dtype=np.float32)[np.arange(MEM_HEADS) % 2])
    wqm_pad = (wqm * m_slot[None, :, :, None]).reshape(dm, QM_PAD)
    wg_pad = jnp.pad(w[:, o3:o4], ((0, 0), (0, LANES - GATE_WIDTH)))
    w_packed = jnp.concatenate([w[:, :o1], wq_pad, w[:, o2:o3], wqm_pad, wg_pad], axis=1).astype(BF16)
    half = ROPE_DIM // 2
    inv = jnp.power(ROPE_THETA, -jnp.arange(half, dtype=F32) * 2.0 / ROPE_DIM)
    d = np.arange(LANES) % HEAD_DIM
    inv_lane = jnp.where(jnp.asarray(d < ROPE_DIM), inv[jnp.asarray(d % half)], 0.0).reshape(1, LANES)
    seg = jnp.asarray((np.arange(LANES)[:, None] // HEAD_DIM == np.arange(LANES)[None, :] // HEAD_DIM), BF16)
    gk = jnp.concatenate([jnp.tile(g_k[l, br], 2) for br in range(N_BRANCH)]).reshape(1, N_BRANCH * LANES)
    proj = dict(g_attn=g_attn[l].reshape(1, dm), w_in=w_packed, gq=jnp.tile(g_q[l], Q_PAD // HEAD_DIM).reshape(1, Q_PAD),
                gk=gk, gmq=jnp.tile(g_mq[l], QM_PAD // HEAD_DIM).reshape(1, QM_PAD), inv=inv_lane, seg=seg)

    n_grp = len(POOL_WINDOWS)
    pg = POOL_WIDTH // n_grp
    w_bd = (jnp.asarray(np.eye(n_grp, dtype=np.float32))[:, None, :, None] * w_pool[l][:, :, None, :]
            ).reshape(POOL_WIDTH, POOL_WIDTH).astype(BF16)
    pool = dict(w=w_bd, s=s_pool[l].reshape(1, POOL_WIDTH))

    half_rows = CMP_STRIDE * HEAD_DIM
    w1 = jnp.concatenate([w_cmp1[l][:, :half_rows], w_cmp1[l][:, half_rows:]], axis=2).astype(BF16)
    cmp_w = dict(w=w1, pe=pe_cmp[l].reshape(2, 1, CMP_BLOCK * HEAD_DIM), w2=w_cmp2[l].astype(BF16))

    mem = dict(g=g_mem[l].reshape(1, dm), w=w_mem_kv[l].astype(BF16),
               gk=jnp.tile(g_mk[l], MEM_HEADS).reshape(1, MEM_WIDTH))
    w_r = jnp.concatenate([w_router1[l], w_router2[l].reshape(dm, N_EXPERTS)], axis=1)
    w_r = jnp.pad(w_r, ((0, 0), (0, LANES - w_r.shape[1]))).astype(BF16)
    b_r = jnp.concatenate([b_router1[l], b_router2[l].reshape(-1)])
    b_r = jnp.pad(b_r, (0, LANES - b_r.shape[0])).reshape(1, LANES)
    fin = dict(g_mix=g_mix[l].reshape(1, dm), w_out=w_out[l].astype(BF16), g_ffn=g_ffn[l].reshape(1, dm),
               w_r=w_r, b_r=b_r)
    return proj, pool, cmp_w, mem, fin


def _pick(n, prefs):
    for p in prefs:
        if n % p == 0:
            return p
    return n


def kernel(x_prompt, x_sample, cache_kv, cache_win, state_pool, cache_mem_kv, page_table, mem_prompt, g_attn, w_in, g_q, g_k, pe_cmp, w_cmp1, w_cmp2, w_pool, s_pool, g_mem, w_mem_kv, g_mq, g_mk, g_mix, w_out, g_ffn, w_router1, b_router1, w_router2, b_router2, w_gate, w_up, w_down):
    depth = w_in.shape[0]
    bp, t, dm = x_prompt.shape
    bs, ts, _ = x_sample.shape
    n_pages = page_table.shape[1]
    page = cache_kv.shape[2]
    past_len = n_pages * page
    n_win = cache_win.shape[2]
    ff = w_gate.shape[-1]
    tq, tk = 256, 512
    assert ts == 1 and n_win == WINDOW and page % SEL_BLOCK == 0 and t % tk == 0 and t >= WINDOW + tq
    n_sel = t // SEL_BLOCK
    assert min(SEL_TOPK, n_sel) >= 3 and n_sel <= LANES
    poolt = jnp.asarray(np.arange(t // CMP_STRIDE)[None, :] // CMP_PER_SEL == np.arange(n_sel)[:, None], BF16)
    key_blk = np.arange(t // tk)[:, None, None] * (tk // SEL_BLOCK) + np.arange(tk)[None, None, :] // SEL_BLOCK
    e_mat = jnp.asarray(np.arange(LANES)[None, :, None] == key_blk, BF16)
    n_cmp_s = past_len // CMP_STRIDE
    n_slot_s = -(-(past_len // SEL_BLOCK + 1) // LANES) * LANES
    pool_s = jnp.asarray(np.arange(n_cmp_s)[:, None] // CMP_PER_SEL == np.arange(n_slot_s)[None, :], BF16)

    hp, hs = x_prompt, x_sample
    outs = [[] for _ in range(7)]
    for l in range(depth):
        proj_w, pool_w, cmp_w, mem_w, fin_w = _prep_weights(
            l, g_attn, w_in, g_q, g_k, pe_cmp, w_cmp1, w_cmp2, w_pool, s_pool, g_mem, w_mem_kv, g_mq, g_mk,
            g_mix, w_out, g_ffn, w_router1, b_router1, w_router2, b_router2)
        wg = w_gate[l].reshape(N_EXPERTS, dm, ff)
        wu = w_up[l].reshape(N_EXPERTS, dm, ff)
        wd = w_down[l].reshape(N_EXPERTS, ff, dm)
        kw_cols = CMP_STRIDE * 2 * LANES

        n = bp * t
        xp2 = hp.reshape(n, dm)
        u, qb, rows4, rowsw, kvb, gates, qmb, rows4t = _proj_call(
            xp2, proj_w, seq_len=t, pos0=0, tm=_pick(t, (512, 256, 128)), feature_major_rows4=True)
        u3 = u.reshape(bp, t, POOL_WIDTH)
        y_pool = _pool_call(u3, pool_w["w"], pool_w["s"], tp=_pick(t, (512, 256, 128)), pos0=0)
        n_slab = 2 * NSA_KV_HEADS
        chunk_w = CMP_STRIDE * HEAD_DIM
        kcvc = _compress_prompt_call(rows4.reshape(bp, t, 4 * LANES), jnp.zeros((bp, n_slab, chunk_w), F32), cmp_w)
        o_nsa = _nsa_prompt_call(qb.reshape(bp, t, Q_PAD), gates.reshape(bp, t, LANES), kcvc,
                                 kvb.reshape(bp, t, NSA_KV_WIDTH), poolt, e_mat, tq=tq, tk=tk)
        m_len = mem_prompt.shape[1]
        mkv, mkvb = _memkv_call(mem_prompt.reshape(bp * m_len, dm), mem_w["g"], mem_w["w"], mem_w["gk"],
                                proj_w["seg"], tm=_pick(bp * m_len, (256, 128)))
        o_mem = _memattn_call(qmb.reshape(bp, t, QM_PAD), mkvb.reshape(bp, m_len, 2 * MEM_WIDTH),
                              tq=_pick(t, (512, 256, 128)))
        tmf = _pick(n, (512, 256, 128))
        h_p, hn_p, route_p, counts_p = _finish_call(y_pool.reshape(n, POOL_WIDTH), o_nsa.reshape(n, NSA_WIDTH),
                                                    o_mem.reshape(n, MEM_WIDTH), xp2, fin_w, tm=tmf)
        tme = 256
        tmd = _pick(n, (256, 128))
        te, nv, pos = _route_tables(route_p, counts_p, tme)
        xs = _dispatch_call(pos, nv, hn_p, tm=tmd, tme=tme)

        xs2 = hs.reshape(bs, dm)
        u_s, qb_s, rows4_s, rowsw_s, _, gates_s, qmb_s = _proj_call(xs2, proj_w, seq_len=1, pos0=past_len, tm=bs)
        ext = jnp.concatenate([state_pool[l], u_s[:, None, :]], axis=1)
        y_pool_s = _pool_call(ext, pool_w["w"], pool_w["s"], tp=POOL_STATE + 1, pos0=past_len - POOL_STATE)[:, -1, :]
        cache_t = jnp.transpose(cache_kv[l], (0, 2, 3, 4, 1)).reshape(
            cache_kv.shape[1], N_KV_SLOTS, NSA_KV_HEADS * HEAD_DIM, page)
        tail_s = jnp.pad(rows4_s[:, :2 * LANES].reshape(bs, n_slab, HEAD_DIM), ((0, 0), (0, 0), (0, chunk_w - HEAD_DIM)))
        kcvc_s = _compress_sample_call(page_table, cache_t, tail_s, cmp_w)
        win_s = jnp.concatenate([cache_win[l].reshape(bs, n_win, 2 * LANES)[:, 1:], rowsw_s[:, None, :]], axis=1)
        mkv_s = cache_mem_kv[l].reshape(bs, cache_mem_kv.shape[2], 2 * MEM_WIDTH)
        oc, ow, om_s, score_s = _sample_attn1_call(qb_s.reshape(bs, 1, Q_PAD), kcvc_s, win_s, mkv_s,
                                                   qmb_s.reshape(bs, 1, QM_PAD), pool_s, t_pos=past_len)
        n_top = min(SEL_TOPK, past_len // SEL_BLOCK + 1)
        idx = _sample_topk_call(score_s.reshape(bs * NSA_KV_HEADS, n_slot_s), n_top=n_top)
        o_nsa_s = _sample_sel_call(page_table, idx, cache_t, qb_s.reshape(bs, 1, Q_PAD),
                                   rows4_s[:, 2 * LANES:].reshape(bs, 1, 2 * LANES), gates_s.reshape(bs, 1, LANES),
                                   oc, ow, t_pos=past_len, n_top=n_top)
        h_s, hn_s, route_s, _ = _finish_call(y_pool_s, o_nsa_s.reshape(bs, NSA_WIDTH), om_s.reshape(bs, MEM_WIDTH),
                                          xs2, fin_w, tm=bs)

        ys, y_s = _expert_call(te, nv, xs, wg, wu, wd, hn_s, route_s, h_s, tm=tme)
        y_p = _combine_call(pos, ys, h_p, route_p, tm=tmd).reshape(bp, t, dm)
        y_s = y_s.reshape(bs, 1, dm)

        keep = min(WINDOW, t)
        outs[0].append(jnp.transpose(rows4t.reshape(bp, N_KV_SLOTS, NSA_KV_HEADS, HEAD_DIM, t), (0, 4, 1, 2, 3)))
        outs[1].append(rows4_s.reshape(bs, 1, N_KV_SLOTS, NSA_KV_HEADS, HEAD_DIM))
        outs[2].append(rowsw.reshape(bp, t, 2, NSA_KV_HEADS, HEAD_DIM)[:, t - keep:])
        outs[3].append(win_s.reshape(bs, n_win, 2, NSA_KV_HEADS, HEAD_DIM))
        outs[4].append(u3[:, t - POOL_STATE:])
        outs[5].append(ext[:, 1:])
        outs[6].append(mkv.reshape(bp, m_len, 2, MEM_HEADS, HEAD_DIM))
        hp, hs = y_p, y_s
    return (hp, hs) + tuple(jnp.stack(o) for o in outs)
```

```python
import functools

import numpy as np
import jax
import jax.numpy as jnp
from jax import lax
from jax.experimental import pallas as pl
from jax.experimental.pallas import tpu as pltpu

F32 = jnp.float32
BF16 = jnp.bfloat16
I32 = jnp.int32

HEAD_DIM = 64
POOL_WINDOWS = (2, 4, 8, 16)
POOL_STATE = max(POOL_WINDOWS) - 1
NSA_HEADS = 8
NSA_KV_HEADS = 2
NSA_REP = NSA_HEADS // NSA_KV_HEADS
N_BRANCH = 3
CMP_BLOCK = 32
CMP_STRIDE = 16
CMP_HIDDEN = 2 * HEAD_DIM
SEL_BLOCK = 64
SEL_TOPK = 16
CMP_PER_SEL = SEL_BLOCK // CMP_STRIDE
WINDOW = 512
FORCE_BONUS = 1000.0
MEM_HEADS = 4
ROPE_DIM = HEAD_DIM // 4
ROPE_THETA = 500000.0
N_EXPERT_GROUPS = 4
EXPERTS_PER_GROUP = 8
N_EXPERTS = N_EXPERT_GROUPS * EXPERTS_PER_GROUP
EPS = 1e-6
N_KV_SLOTS = 4

LANES = 128
POOL_WIDTH = 256
NSA_WIDTH = NSA_HEADS * HEAD_DIM
NSA_KV_WIDTH = N_BRANCH * 2 * NSA_KV_HEADS * HEAD_DIM
GATE_WIDTH = NSA_HEADS * N_BRANCH
MEM_WIDTH = MEM_HEADS * HEAD_DIM
Q_PAD = NSA_HEADS * LANES
QM_PAD = MEM_HEADS * LANES
C_U = 0
C_Q = C_U + POOL_WIDTH
C_KV = C_Q + Q_PAD
C_QM = C_KV + NSA_KV_WIDTH
C_G = C_QM + QM_PAD
C_END = C_G + LANES

NEG = -1e30
VMEM_LIMIT = 48 * 1024 * 1024

_NT = (((1,), (1,)), ((), ()))


def _cparams(*sem):
    return pltpu.CompilerParams(dimension_semantics=tuple(sem), vmem_limit_bytes=VMEM_LIMIT)


def _dot(a, b):
    return jnp.dot(a, b, preferred_element_type=F32)


def _dot_nt(a, b):
    return lax.dot_general(a, b, _NT, preferred_element_type=F32)


def _rms(x):
    return x * lax.rsqrt(jnp.mean(x * x, axis=-1, keepdims=True) + EPS)


def _masked_softmax(s, mask):
    sm = jnp.where(mask, s, NEG)
    m = jnp.max(sm, axis=-1, keepdims=True)
    e = jnp.where(mask, jnp.exp(sm - m), 0.0)
    return e / jnp.maximum(jnp.sum(e, axis=-1, keepdims=True), 1e-30)


def _split3(x):
    hi = x.astype(BF16)
    r1 = x - hi.astype(F32)
    mid = r1.astype(BF16)
    lo = (r1 - mid.astype(F32)).astype(BF16)
    return hi, mid, lo


def _proj_kernel(x_ref, ga_ref, w_ref, gq_ref, gk_ref, gmq_ref, inv_ref, seg_ref,
                 u_ref, q_ref, rows4_ref, rowsw_ref, kvb_ref, gates_ref, qm_ref, *rest,
                 tm, seq_len, pos0, consecutive):
    *maybe_t_refs, cos_in_tile, sin_in_tile = rest
    i = pl.program_id(0)
    a = _rms(x_ref[...]) * ga_ref[...]
    z = _dot(a.astype(BF16), w_ref[...])
    u_ref[...] = z[:, C_U:C_U + POOL_WIDTH]

    inv = inv_ref[...]

    @pl.when(i == 0)
    def _():
        j = lax.broadcasted_iota(I32, (tm, 1), 0).astype(F32) if consecutive else jnp.zeros((tm, 1), F32)
        cos_in_tile[...] = jnp.cos(j * inv)
        sin_in_tile[...] = jnp.sin(j * inv)

    first = (pos0 + (i * tm) % seq_len).astype(F32) * jnp.broadcast_to(inv, (8, LANES))
    cos_f, sin_f = jnp.cos(first)[0:1], jnp.sin(first)[0:1]
    cos = cos_f * cos_in_tile[...] - sin_f * sin_in_tile[...]
    sin = sin_f * cos_in_tile[...] + cos_f * sin_in_tile[...]
    d = lax.broadcasted_iota(I32, (1, LANES), 1) % HEAD_DIM
    half = ROPE_DIM // 2
    s_next = jnp.where(d < half, -sin, 0.0)
    s_prev = jnp.where((d >= half) & (d < ROPE_DIM), sin, 0.0)
    seg = seg_ref[...]

    def head_norm(xc, g):
        ssq = _dot((xc * xc).astype(BF16), seg)
        return xc * lax.rsqrt(ssq * (1.0 / HEAD_DIM) + EPS) * g

    def rope(xc):
        return xc * cos + pltpu.roll(xc, LANES - half, 1) * s_next + pltpu.roll(xc, half, 1) * s_prev

    scale = HEAD_DIM ** -0.5
    for c in range(NSA_HEADS):
        sl = slice(c * LANES, (c + 1) * LANES)
        qc = rope(head_norm(z[:, C_Q + c * LANES:C_Q + (c + 1) * LANES], gq_ref[:, sl]))
        q_ref[:, sl] = (qc * scale).astype(BF16)
    for br in range(N_BRANCH):
        k0 = C_KV + br * 2 * LANES
        kn = rope(head_norm(z[:, k0:k0 + LANES], gk_ref[:, br * LANES:(br + 1) * LANES]))
        vv = z[:, k0 + LANES:k0 + 2 * LANES]
        kvb_ref[:, br * 2 * LANES:br * 2 * LANES + LANES] = kn.astype(BF16)
        kvb_ref[:, br * 2 * LANES + LANES:(br + 1) * 2 * LANES] = vv.astype(BF16)
        if br < 2:
            rows4_ref[:, br * 2 * LANES:br * 2 * LANES + LANES] = kn
            rows4_ref[:, br * 2 * LANES + LANES:(br + 1) * 2 * LANES] = vv
            if maybe_t_refs:
                maybe_t_refs[0][0, br * 2 * LANES:br * 2 * LANES + LANES, :] = kn.T
                maybe_t_refs[0][0, br * 2 * LANES + LANES:(br + 1) * 2 * LANES, :] = vv.T
        else:
            rowsw_ref[:, :LANES] = kn
            rowsw_ref[:, LANES:] = vv
            if maybe_t_refs:
                maybe_t_refs[1][0, :LANES, :] = kn.T
                maybe_t_refs[1][0, LANES:, :] = vv.T
    for c in range(MEM_HEADS):
        sl = slice(c * LANES, (c + 1) * LANES)
        qmc = head_norm(z[:, C_QM + c * LANES:C_QM + (c + 1) * LANES], gmq_ref[:, sl])
        qm_ref[:, sl] = (qmc * scale).astype(BF16)
    gates_ref[...] = jax.nn.sigmoid(z[:, C_G:C_END])


def _proj_call(x2d, pw, *, seq_len, pos0, tm, feature_major_rows4=False):
    n, dm = x2d.shape
    full = lambda shape: pl.BlockSpec(shape, lambda i: (0,) * len(shape))
    rows = lambda w: pl.BlockSpec((tm, w), lambda i: (i, 0))
    outs = [(POOL_WIDTH, F32), (Q_PAD, BF16), (4 * LANES, F32), (2 * LANES, F32), (NSA_KV_WIDTH, BF16),
            (LANES, F32), (QM_PAD, BF16)]
    out_specs = [rows(w) for w, _ in outs]
    out_shape = [jax.ShapeDtypeStruct((n, w), dt) for w, dt in outs]
    if feature_major_rows4:
        per_seq = seq_len // tm
        for width in (4 * LANES, 2 * LANES):
            out_specs.append(pl.BlockSpec((1, width, tm), lambda i: (i // per_seq, 0, i % per_seq)))
            out_shape.append(jax.ShapeDtypeStruct((n // seq_len, width, seq_len), F32))
    consecutive = seq_len % tm == 0
    assert consecutive or seq_len == 1
    return pl.pallas_call(
        functools.partial(_proj_kernel, tm=tm, seq_len=seq_len, pos0=pos0, consecutive=consecutive),
        grid=(n // tm,),
        in_specs=[rows(dm), full((1, dm)), full((dm, C_END)), full((1, Q_PAD)), full((1, N_BRANCH * LANES)),
                  full((1, QM_PAD)), full((1, LANES)), full((LANES, LANES))],
        out_specs=out_specs,
        out_shape=out_shape,
        scratch_shapes=[pltpu.VMEM((tm, LANES), F32), pltpu.VMEM((tm, LANES), F32)],
        compiler_params=_cparams("arbitrary"),
        name="proj",
    )(x2d, pw["g_attn"], pw["w_in"], pw["gq"], pw["gk"], pw["gmq"], pw["inv"], pw["seg"])


def _pool_kernel(u_ref, halo_ref, w_ref, s_ref, y_ref, *, tp, pos0):
    i = pl.program_id(1)
    u = u_ref[0]
    halo = halo_ref[0] * (i > 0).astype(F32)
    n_h = POOL_STATE + 1
    ext = jnp.concatenate([halo, u], axis=0)
    sums = {1: ext}
    w = 1
    while w < max(POOL_WINDOWS):
        sums[2 * w] = sums[w] + pltpu.roll(sums[w], w, 0)
        w *= 2
    pos = pos0 + i * tp + lax.broadcasted_iota(I32, (tp, 1), 0)
    lane_grp = lax.broadcasted_iota(I32, (1, POOL_WIDTH), 1) // (POOL_WIDTH // len(POOL_WINDOWS))
    mean = jnp.zeros((tp, POOL_WIDTH), F32)
    for gi, wdw in enumerate(POOL_WINDOWS):
        cnt = jnp.minimum(pos + 1, wdw).astype(F32)
        mean = jnp.where(lane_grp == gi, sums[wdw][n_h:] / cnt, mean)
    r = mean - u
    y_ref[0] = _dot(r.astype(BF16), w_ref[...]) * s_ref[...]


def _pool_call(u3, w_bd, s_pool, *, tp, pos0):
    b, t, c = u3.shape
    n_h = POOL_STATE + 1
    return pl.pallas_call(
        functools.partial(_pool_kernel, tp=tp, pos0=pos0),
        grid=(b, t // tp),
        in_specs=[pl.BlockSpec((1, tp, c), lambda bi, i: (bi, i, 0)),
                  pl.BlockSpec((1, n_h, c), lambda bi, i: (bi, jnp.maximum(i * (tp // n_h) - 1, 0), 0)),
                  pl.BlockSpec((c, c), lambda bi, i: (0, 0)),
                  pl.BlockSpec((1, c), lambda bi, i: (0, 0))],
        out_specs=pl.BlockSpec((1, tp, c), lambda bi, i: (bi, i, 0)),
        out_shape=jax.ShapeDtypeStruct((b, t, c), F32),
        compiler_params=_cparams("parallel", "parallel"),
        name="pool",
    )(u3, u3, w_bd, s_pool)


def _split_chunks(x, n_chunks):
    return jnp.transpose(x.reshape(n_chunks, CMP_STRIDE, LANES), (1, 0, 2))


def _compress_core(load_slab, tail_ref, pe_ref, w_ref, w2_ref, n):
    rowid = lax.broadcasted_iota(I32, (n, 1), 0)
    outs = []
    for c in range(2):
        w = w_ref[c]
        pe8 = jnp.broadcast_to(pe_ref[c], (8, 2 * CMP_STRIDE * HEAD_DIM)).astype(BF16)
        pe_first = _dot(pe8[:, :CMP_STRIDE * HEAD_DIM], w)[0:1, :CMP_HIDDEN]
        pe_second = _dot(pe8[:, CMP_STRIDE * HEAD_DIM:], w)[0:1, CMP_HIDDEN:]
        pe_const = pe_first + pe_second
        lane = lax.broadcasted_iota(I32, (1, LANES), 1)
        per_group = [[], []]
        for q in range(CMP_STRIDE // 2):
            a, b = load_slab(c, 2 * q), load_slab(c, 2 * q + 1)
            per_group[0].append(jnp.where(lane < HEAD_DIM, a, pltpu.roll(b, HEAD_DIM, 1)))
            per_group[1].append(jnp.where(lane < HEAD_DIM, pltpu.roll(a, HEAD_DIM, 1), b))
        x2 = jnp.concatenate([jnp.concatenate(pg, axis=1) for pg in per_group], axis=0).astype(BF16)
        z = _dot(x2, w)
        hs = []
        for g in range(NSA_KV_HEADS):
            tail8 = jnp.broadcast_to(tail_ref[0, 2 * c + g:2 * c + g + 1, :], (8, CMP_STRIDE * HEAD_DIM)).astype(BF16)
            second_tail = _dot(tail8, w)[0:1, CMP_HIDDEN:]
            first = z[g * n:(g + 1) * n, :CMP_HIDDEN]
            second = z[g * n:(g + 1) * n, CMP_HIDDEN:]
            shifted = pltpu.roll(second, n - 1, 0)
            h = first + jnp.where(rowid == n - 1, second_tail, shifted) + pe_const
            hs.append(jax.nn.gelu(h, approximate=True))
        o = _dot(jnp.concatenate(hs, axis=0).astype(BF16), w2_ref[c])
        outs += [o[:n], o[n:]]
    return jnp.concatenate(outs, axis=1)


def _compress_prompt_kernel(k_ref, v_ref, tail_ref, pe_ref, w_ref, w2_ref, out_ref, *, n):
    slabs = [_split_chunks(ref[0], n) for ref in (k_ref, v_ref)]
    out_ref[0] = _compress_core(lambda c, r: slabs[c][r], tail_ref, pe_ref, w_ref, w2_ref, n).astype(BF16)


def _compress_prompt_call(rows4_3d, tail, cw):
    b, t, _ = rows4_3d.shape
    n = t // CMP_STRIDE
    full = lambda a: pl.BlockSpec(a.shape, lambda bi: (0,) * a.ndim)
    return pl.pallas_call(
        functools.partial(_compress_prompt_kernel, n=n),
        grid=(b,),
        in_specs=[pl.BlockSpec((1, t, LANES), lambda bi: (bi, 0, 0)),
                  pl.BlockSpec((1, t, LANES), lambda bi: (bi, 0, 1)),
                  pl.BlockSpec((1,) + tail.shape[1:], lambda bi: (bi, 0, 0)),
                  full(cw["pe"]), full(cw["w"]), full(cw["w2"])],
        out_specs=pl.BlockSpec((1, n, 2 * LANES), lambda bi: (bi, 0, 0)),
        out_shape=jax.ShapeDtypeStruct((b, n, 2 * LANES), BF16),
        compiler_params=_cparams("parallel"),
        name="compress_prompt",
    )(rows4_3d, rows4_3d, tail, cw["pe"], cw["w"], cw["w2"])


def _compress_sample_kernel(pt_ref, cache_ref, tail_ref, pe_ref, w_ref, w2_ref, out_ref,
                            buf, slabs, sem, *, n, n_pages, page):
    b = pl.program_id(0)
    nb = pl.num_programs(0)

    def page_copy(bb, slot, p, c):
        return pltpu.make_async_copy(cache_ref.at[pt_ref[bb * n_pages + p], c], buf.at[slot, c, p], sem.at[slot])

    def for_all(fn):
        for p in range(n_pages):
            for c in range(2):
                fn(p, c)

    def issue(bb, slot):
        for_all(lambda p, c: page_copy(bb, slot, p, c).start())

    @pl.when(b == 0)
    def _():
        issue(0, 0)

    @pl.when(b + 1 < nb)
    def _():
        issue(b + 1, (b + 1) % 2)

    slot = b % 2
    for_all(lambda p, c: page_copy(b, slot, p, c).wait())

    unroll = next(u for u in (8, 4, 2, 1) if n_pages % u == 0)

    cpp = page // CMP_STRIDE

    def to_slabs(pp, carry):
        for k in range(unroll):
            p = pp * unroll + k
            for c in range(2):
                by_r = _split_chunks(buf[slot, c, p].T, cpp)
                for r in range(CMP_STRIDE):
                    slabs[c, r, pl.ds(pl.multiple_of(p * cpp, cpp), cpp), :] = by_r[r]
        return carry

    lax.fori_loop(0, n_pages // unroll, to_slabs, 0)
    out_ref[0] = _compress_core(lambda c, r: slabs[c, r], tail_ref, pe_ref, w_ref, w2_ref, n).astype(BF16)


def _compress_sample_call(page_table, cache_t, tail, cw):
    b, n_pages = page_table.shape
    page = cache_t.shape[3]
    n = n_pages * page // CMP_STRIDE
    full = lambda a: pl.BlockSpec(a.shape, lambda bi, pt: (0,) * a.ndim)
    grid_spec = pltpu.PrefetchScalarGridSpec(
        num_scalar_prefetch=1,
        grid=(b,),
        in_specs=[pl.BlockSpec(memory_space=pl.ANY),
                  pl.BlockSpec((1,) + tail.shape[1:], lambda bi, pt: (bi, 0, 0)),
                  full(cw["pe"]), full(cw["w"]), full(cw["w2"])],
        out_specs=pl.BlockSpec((1, n, 2 * LANES), lambda bi, pt: (bi, 0, 0)),
        scratch_shapes=[pltpu.VMEM((2, 2, n_pages, LANES, page), F32), pltpu.VMEM((2, CMP_STRIDE, n, LANES), F32),
                        pltpu.SemaphoreType.DMA((2,))],
    )
    return pl.pallas_call(
        functools.partial(_compress_sample_kernel, n=n, n_pages=n_pages, page=page),
        grid_spec=grid_spec,
        out_shape=jax.ShapeDtypeStruct((b, n, 2 * LANES), BF16),
        compiler_params=_cparams("arbitrary"),
        name="compress_sample",
    )(page_table.reshape(-1), cache_t, tail, cw["pe"], cw["w"], cw["w2"])


def _topk_mask(score, ids, n_top, axis):
    sel = jnp.zeros(score.shape, jnp.bool_)
    work = score
    firsts = []
    big = float(score.shape[axis])
    for _ in range(n_top):
        m = jnp.max(work, axis=axis, keepdims=True)
        first = jnp.min(jnp.where(work == m, ids, big), axis=axis, keepdims=True)
        pick = (ids == first) & (m > -jnp.inf)
        sel = sel | pick
        work = jnp.where(pick, -jnp.inf, work)
        firsts.append(jnp.where(m > -jnp.inf, first, -1.0))
    return sel, firsts


def _gate_and_pack(o_ref_store, gates, o_c, o_s, o_w, rows):
    lane = lax.broadcasted_iota(I32, (1, LANES), 1)
    heads = []
    for h in range(NSA_HEADS):
        g, r = divmod(h, NSA_REP)
        rs = slice(r * rows, (r + 1) * rows)
        gc, gs, gw = (gates[:, N_BRANCH * h + k:N_BRANCH * h + k + 1] for k in range(N_BRANCH))
        heads.append(gc * o_c[g][rs] + gs * o_s[g][rs] + gw * o_w[g][rs])
    for j in range(NSA_HEADS // 2):
        a, b = heads[2 * j], heads[2 * j + 1]
        if (2 * j) // NSA_REP == 0:
            chunk = jnp.where(lane < HEAD_DIM, a, pltpu.roll(b, HEAD_DIM, 1))
        else:
            chunk = jnp.where(lane < HEAD_DIM, pltpu.roll(a, HEAD_DIM, 1), b)
        o_ref_store(j, chunk)


def _nsa_prompt_kernel(q_ref, gate_ref, kc_ref, kv_ref, poolt_ref, e_ref, o_ref, acc_ref, *, tq, tk, t_len):
    i = pl.program_id(1)
    s0 = i * tq
    rows4 = NSA_REP * tq
    tpos = s0 + lax.broadcasted_iota(I32, (tq, 1), 0)
    rep = lambda x: jnp.concatenate([x] * NSA_REP, axis=0)
    n_cmp = kc_ref.shape[1]
    kc128 = kc_ref[0, :, :LANES]
    vc128 = kc_ref[0, :, LANES:]
    kc_end = lax.broadcasted_iota(I32, (1, n_cmp), 1) * CMP_STRIDE + (CMP_BLOCK - 1)
    bias_c = rep(jnp.where(kc_end <= tpos, 0.0, NEG))
    any_c = rep((tpos >= CMP_BLOCK - 1).astype(F32))
    n_sel = t_len // SEL_BLOCK
    blk = lax.broadcasted_iota(I32, (n_sel, 1), 0)
    blk_f = blk.astype(F32)
    tq_lane = s0 + lax.broadcasted_iota(I32, (1, tq), 1)
    cur = tq_lane // SEL_BLOCK
    valid = blk * SEL_BLOCK <= tq_lane
    forced = (blk == 0) | (blk == cur) | (blk == cur - 1)
    w_start = pl.multiple_of(jnp.maximum(s0 - WINDOW, 0), LANES)
    w_len = WINDOW + tq
    kpos_w = w_start + lax.broadcasted_iota(I32, (1, w_len), 1)
    dist = tpos - kpos_w
    bias_w = rep(jnp.where((dist >= 0) & (dist < WINDOW), 0.0, NEG))
    c_diag = s0 // tk
    kpos_d = c_diag * tk + lax.broadcasted_iota(I32, (1, tk), 1)
    bias_causal = jnp.where(kpos_d <= tpos, 0.0, NEG)

    o_c, o_s, o_w, qgs, selbs = [], [], [], [], []
    for g in range(NSA_KV_HEADS):
        qg = jnp.concatenate([q_ref[0, :, (g * NSA_REP + r) * LANES:(g * NSA_REP + r + 1) * LANES]
                              for r in range(NSA_REP)], axis=0)
        qgs.append(qg)
        s_c = _dot_nt(qg, kc128) + bias_c
        e_c = jnp.exp(s_c - jnp.max(s_c, axis=-1, keepdims=True))
        p_c = e_c * (any_c / jnp.sum(e_c, axis=-1, keepdims=True))
        o_c.append(_dot(p_c.astype(BF16), vc128))
        imp = p_c[0:tq]
        for r in range(1, NSA_REP):
            imp = imp + p_c[r * tq:(r + 1) * tq]
        imp_t = sum(_dot_nt(poolt_ref[...], piece) for piece in _split3(imp))
        score = jnp.where(valid, imp_t + jnp.where(forced, FORCE_BONUS, 0.0), -jnp.inf)
        sel_t, _ = _topk_mask(score, blk_f, min(SEL_TOPK, n_sel), 0)
        selb_t = jnp.concatenate([jnp.where(sel_t, 0.0, NEG), jnp.full(((-n_sel) % LANES, tq), NEG, F32)], axis=0)
        selbs.append(selb_t.T.astype(BF16))

        kw = kv_ref[0, pl.ds(w_start, w_len), 4 * LANES:5 * LANES]
        vw = kv_ref[0, pl.ds(w_start, w_len), 5 * LANES:6 * LANES]
        s_w = _dot_nt(qg, kw) + bias_w
        e_w = jnp.exp(s_w - jnp.max(s_w, axis=-1, keepdims=True))
        o_w.append(_dot(e_w.astype(BF16), vw) * (1.0 / jnp.sum(e_w, axis=-1, keepdims=True)))

    acc_ref[...] = jnp.zeros(acc_ref.shape, F32)

    q_all = jnp.concatenate(qgs, axis=0)
    selb_all = jnp.concatenate(selbs, axis=0)
    rows8 = NSA_KV_HEADS * rows4

    def chunk_step(c, carry, extra_bias):
        m, l = carry
        k0 = pl.multiple_of(c * tk, tk)
        ks = kv_ref[0, pl.ds(k0, tk), 2 * LANES:3 * LANES]
        vs = kv_ref[0, pl.ds(k0, tk), 3 * LANES:4 * LANES]
        bias = _dot(selb_all, e_ref[c])
        if extra_bias is not None:
            bias = bias + jnp.concatenate([extra_bias] * NSA_KV_HEADS, axis=0)
        bias_rows = jnp.concatenate([rep(bias[g * tq:(g + 1) * tq]) for g in range(NSA_KV_HEADS)], axis=0)
        s = _dot_nt(q_all, ks) + bias_rows
        m_new = jnp.maximum(m, jnp.max(s, axis=-1, keepdims=True))
        alpha = jnp.exp(m - m_new)
        p = jnp.exp(s - m_new)
        l_new = alpha * l + jnp.sum(p, axis=-1, keepdims=True)
        acc_ref[...] = alpha * acc_ref[...] + _dot(p.astype(BF16), vs)
        return m_new, l_new

    init = (jnp.full((rows8, 1), NEG, F32), jnp.zeros((rows8, 1), F32))
    carry = lax.fori_loop(0, c_diag, lambda c, cr: chunk_step(c, cr, None), init)
    _, l_fin = chunk_step(c_diag, carry, bias_causal)
    o_all = acc_ref[...] * (1.0 / jnp.maximum(l_fin, 1e-30))
    for g in range(NSA_KV_HEADS):
        o_s.append(o_all[g * rows4:(g + 1) * rows4])

    def store(j, chunk):
        o_ref[0, :, j * LANES:(j + 1) * LANES] = chunk

    _gate_and_pack(store, gate_ref[0], o_c, o_s, o_w, tq)


def _nsa_prompt_call(q3, gates3, kcvc, kvb3, poolt, e_mat, *, tq, tk):
    b, t, _ = q3.shape
    n_cmp = kcvc.shape[1]
    return pl.pallas_call(
        functools.partial(_nsa_prompt_kernel, tq=tq, tk=tk, t_len=t),
        grid=(b, t // tq),
        in_specs=[pl.BlockSpec((1, tq, Q_PAD), lambda bi, i: (bi, i, 0)),
                  pl.BlockSpec((1, tq, LANES), lambda bi, i: (bi, i, 0)),
                  pl.BlockSpec((1, n_cmp, 2 * LANES), lambda bi, i: (bi, 0, 0)),
                  pl.BlockSpec((1, t, NSA_KV_WIDTH), lambda bi, i: (bi, 0, 0)),
                  pl.BlockSpec(poolt.shape, lambda bi, i: (0, 0)),
                  pl.BlockSpec(e_mat.shape, lambda bi, i: (0, 0, 0))],
        out_specs=pl.BlockSpec((1, tq, NSA_WIDTH), lambda bi, i: (bi, i, 0)),
        out_shape=jax.ShapeDtypeStruct((b, t, NSA_WIDTH), F32),
        scratch_shapes=[pltpu.VMEM((NSA_KV_HEADS * NSA_REP * tq, LANES), F32)],
        compiler_params=_cparams("parallel", "arbitrary"),
        name="nsa_prompt",
    )(q3, gates3, kcvc, kvb3, poolt, e_mat)


def _memkv_kernel(x_ref, g_ref, w_ref, gk_ref, seg_ref, o_ref, ob_ref):
    a = _rms(x_ref[...]) * g_ref[...]
    z = _dot(a.astype(BF16), w_ref[...])
    for c in range(MEM_WIDTH // LANES):
        sl = slice(c * LANES, (c + 1) * LANES)
        xc = z[:, sl]
        ssq = _dot((xc * xc).astype(BF16), seg_ref[...])
        kn = xc * lax.rsqrt(ssq * (1.0 / HEAD_DIM) + EPS) * gk_ref[:, sl]
        o_ref[:, sl] = kn
        ob_ref[:, sl] = kn.astype(BF16)
    o_ref[:, MEM_WIDTH:] = z[:, MEM_WIDTH:]
    ob_ref[:, MEM_WIDTH:] = z[:, MEM_WIDTH:].astype(BF16)


def _memkv_call(mem2d, g_mem, w_mem, gmk, seg, *, tm):
    n, dm = mem2d.shape
    full = lambda shape: pl.BlockSpec(shape, lambda i: (0,) * len(shape))
    return pl.pallas_call(
        _memkv_kernel,
        grid=(n // tm,),
        in_specs=[pl.BlockSpec((tm, dm), lambda i: (i, 0)), full((1, dm)), full((dm, 2 * MEM_WIDTH)),
                  full((1, MEM_WIDTH)), full((LANES, LANES))],
        out_specs=[pl.BlockSpec((tm, 2 * MEM_WIDTH), lambda i: (i, 0))] * 2,
        out_shape=[jax.ShapeDtypeStruct((n, 2 * MEM_WIDTH), F32), jax.ShapeDtypeStruct((n, 2 * MEM_WIDTH), BF16)],
        compiler_params=_cparams("parallel"),
        name="memkv",
    )(mem2d, g_mem, w_mem, gmk, seg)


def _mem_attend(qm, mkv, rows):
    lane = lax.broadcasted_iota(I32, (1, LANES), 1)
    chunks = []
    for j in range(MEM_HEADS // 2):
        k128 = mkv[:, j * LANES:(j + 1) * LANES]
        v128 = mkv[:, MEM_WIDTH + j * LANES:MEM_WIDTH + (j + 1) * LANES]
        q2 = jnp.concatenate([qm[2 * j], qm[2 * j + 1]], axis=0)
        s = _dot_nt(q2, k128)
        e = jnp.exp(s - jnp.max(s, axis=-1, keepdims=True))
        p = e / jnp.sum(e, axis=-1, keepdims=True)
        o = _dot(p.astype(BF16), v128)
        chunks.append(jnp.where(lane < HEAD_DIM, o[:rows], o[rows:2 * rows]))
    return chunks


def _memattn_kernel(qm_ref, mkv_ref, o_ref, *, tq):
    qm = [qm_ref[0, :, h * LANES:(h + 1) * LANES] for h in range(MEM_HEADS)]
    for j, chunk in enumerate(_mem_attend(qm, mkv_ref[0], tq)):
        o_ref[0, :, j * LANES:(j + 1) * LANES] = chunk


def _memattn_call(qm3, mkv3, *, tq):
    b, t, _ = qm3.shape
    m = mkv3.shape[1]
    return pl.pallas_call(
        functools.partial(_memattn_kernel, tq=tq),
        grid=(b, t // tq),
        in_specs=[pl.BlockSpec((1, tq, QM_PAD), lambda bi, i: (bi, i, 0)),
                  pl.BlockSpec((1, m, 2 * MEM_WIDTH), lambda bi, i: (bi, 0, 0))],
        out_specs=pl.BlockSpec((1, tq, MEM_WIDTH), lambda bi, i: (bi, i, 0)),
        out_shape=jax.ShapeDtypeStruct((b, t, MEM_WIDTH), F32),
        compiler_params=_cparams("parallel", "parallel"),
        name="memattn",
    )(qm3, mkv3)


def _pad_rows(rows_list):
    x = jnp.concatenate(rows_list, axis=0)
    return jnp.concatenate([x, jnp.zeros((8 - x.shape[0], x.shape[1]), x.dtype)], axis=0)


def _sample_attn1_kernel(q_ref, kc_ref, win_ref, wnew_ref, mkv_ref, qm_ref, pool_ref,
                         oc_ref, ow_ref, om_ref, score_ref, *, t_pos, n_win):
    q = q_ref[0].astype(F32)
    n_cmp = kc_ref.shape[1]
    kc128 = kc_ref[0, :, :LANES]
    vc128 = kc_ref[0, :, LANES:]
    kw_t = win_ref[0, 0].astype(BF16)
    vw_t = win_ref[0, 1].astype(BF16)
    wnew = wnew_ref[0]
    kw_new = wnew[:, :LANES].astype(BF16).astype(F32)
    vw_new = wnew[:, LANES:].astype(BF16).astype(F32)
    kc_end = lax.broadcasted_iota(I32, (1, n_cmp), 1) * CMP_STRIDE + (CMP_BLOCK - 1)
    mask_c = kc_end <= t_pos
    kw_pos = t_pos - n_win + lax.broadcasted_iota(I32, (1, n_win), 1)
    mask_w = (t_pos - kw_pos >= 0) & (t_pos - kw_pos < WINDOW) & (kw_pos >= 0)
    n_slot = pool_ref.shape[1]
    blk = lax.broadcasted_iota(I32, (1, n_slot), 1)
    cur = t_pos // SEL_BLOCK
    valid = blk * SEL_BLOCK <= t_pos
    forced = (blk == 0) | (blk == cur) | (blk == cur - 1)
    for g in range(NSA_KV_HEADS):
        qg_f = _pad_rows([q[:, (g * NSA_REP + r) * LANES:(g * NSA_REP + r + 1) * LANES] for r in range(NSA_REP)])
        qg = qg_f.astype(BF16)
        p_c = _masked_softmax(_dot_nt(qg, kc128), mask_c)
        oc_ref[0, g * NSA_REP:(g + 1) * NSA_REP, :] = _dot(p_c.astype(BF16), vc128)[:NSA_REP]
        imp = jnp.sum(p_c[:NSA_REP], axis=0, keepdims=True)
        imp8 = jnp.broadcast_to(imp, (8, n_cmp))
        imp_b = sum(_dot(piece, pool_ref[...]) for piece in _split3(imp8))[0:1]
        score_ref[0, g:g + 1, :] = jnp.where(valid, imp_b + jnp.where(forced, FORCE_BONUS, 0.0), -jnp.inf)
        s_old = jnp.where(mask_w, _dot(qg, kw_t), NEG)
        s_new = jnp.sum(qg_f * kw_new, axis=-1, keepdims=True)
        m_w = jnp.maximum(jnp.max(s_old, axis=-1, keepdims=True), s_new)
        e_old = jnp.where(mask_w, jnp.exp(s_old - m_w), 0.0)
        e_new = jnp.exp(s_new - m_w)
        den = jnp.sum(e_old, axis=-1, keepdims=True) + e_new
        o_win = (_dot_nt(e_old.astype(BF16), vw_t) + e_new.astype(BF16).astype(F32) * vw_new) / den
        ow_ref[0, g * NSA_REP:(g + 1) * NSA_REP, :] = o_win[:NSA_REP]
    qm = qm_ref[0].astype(F32)
    lane = lax.broadcasted_iota(I32, (1, LANES), 1)
    mk_t = mkv_ref[0, 0].astype(BF16)
    mv_t = mkv_ref[0, 1].astype(BF16)
    for j in range(MEM_HEADS // 2):
        q2 = _pad_rows([qm[:, h * LANES:(h + 1) * LANES] for h in (2 * j, 2 * j + 1)]).astype(BF16)
        s = _dot(q2, mk_t[j * LANES:(j + 1) * LANES, :])
        e = jnp.exp(s - jnp.max(s, axis=-1, keepdims=True))
        p = e / jnp.sum(e, axis=-1, keepdims=True)
        o = _dot_nt(p.astype(BF16), mv_t[j * LANES:(j + 1) * LANES, :])
        om_ref[0, :, j * LANES:(j + 1) * LANES] = jnp.where(lane < HEAD_DIM, o[0:1], o[1:2])


def _sample_attn1_call(q3, kcvc, win_t, wnew3, mkv_t, qm3, pool_mat, *, t_pos):
    b = q3.shape[0]
    n_cmp = kcvc.shape[1]
    n_win = win_t.shape[3]
    m = mkv_t.shape[3]
    per_b = lambda shape: pl.BlockSpec((1,) + shape, lambda bi: (bi,) + (0,) * len(shape))
    return pl.pallas_call(
        functools.partial(_sample_attn1_kernel, t_pos=t_pos, n_win=n_win),
        grid=(b,),
        in_specs=[per_b((1, Q_PAD)), per_b((n_cmp, 2 * LANES)), per_b((2, 2 * HEAD_DIM, n_win)),
                  per_b((1, 2 * LANES)), per_b((2, MEM_WIDTH, m)), per_b((1, QM_PAD)),
                  pl.BlockSpec(pool_mat.shape, lambda bi: (0, 0))],
        out_specs=[per_b((NSA_HEADS, LANES)), per_b((NSA_HEADS, LANES)), per_b((1, MEM_WIDTH)),
                   per_b((NSA_KV_HEADS, pool_mat.shape[1]))],
        out_shape=[jax.ShapeDtypeStruct((b, NSA_HEADS, LANES), F32), jax.ShapeDtypeStruct((b, NSA_HEADS, LANES), F32),
                   jax.ShapeDtypeStruct((b, 1, MEM_WIDTH), F32),
                   jax.ShapeDtypeStruct((b, NSA_KV_HEADS, pool_mat.shape[1]), F32)],
        compiler_params=_cparams("parallel"),
        name="sample_attn1",
    )(q3, kcvc, win_t, wnew3, mkv_t, qm3, pool_mat)


def _sample_topk_kernel(score_ref, idx_ref, *, n_top):
    score = score_ref[...]
    ids = lax.broadcasted_iota(I32, (1, score.shape[1]), 1).astype(F32)
    _, firsts = _topk_mask(score, ids, n_top, 1)
    lane = lax.broadcasted_iota(I32, (1, LANES), 1)
    idx = jnp.full((score.shape[0], LANES), -1, I32)
    for j, f in enumerate(firsts):
        idx = jnp.where(lane == j, f.astype(I32), idx)
    idx_ref[...] = idx


def _sample_topk_call(score2d, *, n_top):
    rows = score2d.shape[0]
    return pl.pallas_call(
        functools.partial(_sample_topk_kernel, n_top=n_top),
        out_shape=jax.ShapeDtypeStruct((rows, LANES), I32),
        compiler_params=pltpu.CompilerParams(vmem_limit_bytes=VMEM_LIMIT),
        name="sample_topk",
    )(score2d)


def _sample_sel_kernel(pt_ref, idx_ref, cache_ref, q_ref, knew_ref, gate_ref, oc_ref, ow_ref, o_ref,
                       buf, sem, *, t_pos, n_pages, n_top, page):
    b = pl.program_id(0)
    nb = pl.num_programs(0)
    blk_per_page = page // SEL_BLOCK
    n_past_blk = n_pages * blk_per_page
    idx_stride = NSA_KV_HEADS * LANES

    def blk_at(bb, g, j):
        return idx_ref[bb * idx_stride + g * LANES + j]

    def blk_copy(bb, slot, g, j, kv):
        blkc = jnp.clip(blk_at(bb, g, j), 0, n_past_blk - 1)
        pg = pt_ref[bb * n_pages + blkc // blk_per_page]
        return pltpu.make_async_copy(cache_ref.at[pg, 2 + kv], buf.at[slot, g, kv, j], sem.at[slot])

    def for_all(fn):
        for g in range(NSA_KV_HEADS):
            for j in range(n_top):
                for kv in range(2):
                    fn(g, j, kv)

    def issue(bb, slot):
        for_all(lambda g, j, kv: blk_copy(bb, slot, g, j, kv).start())

    @pl.when(b == 0)
    def _():
        issue(0, 0)

    @pl.when(b + 1 < nb)
    def _():
        issue(b + 1, (b + 1) % 2)

    slot = b % 2
    for_all(lambda g, j, kv: blk_copy(b, slot, g, j, kv).wait())

    q = q_ref[0].astype(F32)
    knew = knew_ref[0]
    k_new = knew[:, :LANES].astype(BF16).astype(F32)
    v_new = knew[:, LANES:].astype(BF16).astype(F32)
    n_keys = n_top * page
    key_lane = lax.broadcasted_iota(I32, (1, n_keys), 1)
    key_slot = key_lane // page
    key_row = key_lane % page
    cur_blk = t_pos // SEL_BLOCK
    o_s = []
    for g in range(NSA_KV_HEADS):
        qg = _pad_rows([q[:, (g * NSA_REP + r) * LANES:(g * NSA_REP + r + 1) * LANES] for r in range(NSA_REP)])
        blkvec = jnp.full((1, n_keys), -1, I32)
        has_cur = jnp.zeros((1, 1), jnp.bool_)
        for j in range(n_top):
            bj = blk_at(b, g, j)
            blkvec = jnp.where(key_slot == j, bj, blkvec)
            has_cur = has_cur | (bj == cur_blk)
        in_blk = key_row // SEL_BLOCK == blkvec % blk_per_page
        key_pos = (blkvec // blk_per_page) * page + key_row
        vis = (blkvec >= 0) & (blkvec < n_past_blk) & in_blk & (key_pos <= t_pos)
        kt = jnp.concatenate([buf[slot, g, 0, j] for j in range(n_top)], axis=1).astype(BF16)
        vt = jnp.concatenate([buf[slot, g, 1, j] for j in range(n_top)], axis=1).astype(BF16)
        s_past = jnp.where(vis, _dot(qg.astype(BF16), kt), NEG)
        s_new = jnp.where(has_cur, jnp.sum(qg * k_new, axis=-1, keepdims=True), NEG)
        m = jnp.maximum(jnp.max(s_past, axis=-1, keepdims=True), s_new)
        e_p = jnp.where(vis, jnp.exp(s_past - m), 0.0)
        e_n = jnp.where(has_cur, jnp.exp(s_new - m), 0.0)
        den = jnp.maximum(jnp.sum(e_p, axis=-1, keepdims=True) + e_n, 1e-30)
        num = _dot_nt(e_p.astype(BF16), vt) + e_n.astype(BF16).astype(F32) * v_new
        o_s.append(num / den)
    o_c = [jnp.concatenate([oc_ref[0, g * NSA_REP:(g + 1) * NSA_REP, :]] * 2, axis=0) for g in range(NSA_KV_HEADS)]
    o_w = [jnp.concatenate([ow_ref[0, g * NSA_REP:(g + 1) * NSA_REP, :]] * 2, axis=0) for g in range(NSA_KV_HEADS)]

    def store(j, chunk):
        o_ref[0, :, j * LANES:(j + 1) * LANES] = chunk

    _gate_and_pack(store, gate_ref[0], o_c, o_s, o_w, 1)


def _sample_sel_call(page_table, idx, cache_t, q3, knew3, gates3, oc, ow, *, t_pos, n_top):
    b, n_pages = page_table.shape
    page = cache_t.shape[3]
    per_b = lambda shape: pl.BlockSpec((1,) + shape, lambda bi, pt, ix: (bi, 0, 0))
    grid_spec = pltpu.PrefetchScalarGridSpec(
        num_scalar_prefetch=2,
        grid=(b,),
        in_specs=[pl.BlockSpec(memory_space=pl.ANY), per_b((1, Q_PAD)), per_b((1, 2 * LANES)), per_b((1, LANES)),
                  per_b((NSA_HEADS, LANES)), per_b((NSA_HEADS, LANES))],
        out_specs=per_b((1, NSA_WIDTH)),
        scratch_shapes=[pltpu.VMEM((2, NSA_KV_HEADS, 2, n_top, LANES, page), F32), pltpu.SemaphoreType.DMA((2,))],
    )
    return pl.pallas_call(
        functools.partial(_sample_sel_kernel, t_pos=t_pos, n_pages=n_pages, n_top=n_top, page=page),
        grid_spec=grid_spec,
        out_shape=jax.ShapeDtypeStruct((b, 1, NSA_WIDTH), F32),
        compiler_params=_cparams("arbitrary"),
        name="sample_sel",
    )(page_table.reshape(-1), idx.reshape(-1), cache_t, q3, knew3, gates3, oc, ow)


def _finish_kernel(yp_ref, on_ref, om_ref, x_ref, gm_ref, wo_ref, gf_ref, wr_ref, br_ref, tri_ref,
                   h_ref, hn_ref, route_ref, counts_ref, cnt_ref):
    gm = gm_ref[...]
    o1 = POOL_WIDTH
    o2 = o1 + NSA_WIDTH
    mixed = jnp.concatenate([_rms(yp_ref[...]) * gm[:, :o1], _rms(on_ref[...]) * gm[:, o1:o2],
                             _rms(om_ref[...]) * gm[:, o2:]], axis=-1)
    h = x_ref[...] + _dot(mixed.astype(BF16), wo_ref[...])
    h_ref[...] = h
    hn = _rms(h) * gf_ref[...]
    hn_ref[...] = hn
    logits = _dot(hn.astype(BF16), wr_ref[...]) + br_ref[...]
    lane = lax.broadcasted_iota(I32, (1, LANES), 1)
    lane_f = lane.astype(F32)
    is1 = lane < N_EXPERT_GROUPS
    m1 = jnp.max(jnp.where(is1, logits, -jnp.inf), axis=-1, keepdims=True)
    e1 = jnp.where(is1, jnp.exp(logits - m1), 0.0)
    p1 = e1 / jnp.sum(e1, axis=-1, keepdims=True)
    top1_p = jnp.max(p1, axis=-1, keepdims=True)
    grp = jnp.min(jnp.where((p1 == top1_p) & is1, lane_f, float(LANES)), axis=-1, keepdims=True)
    base = N_EXPERT_GROUPS + grp * EXPERTS_PER_GROUP
    in_g = (lane_f >= base) & (lane_f < base + EXPERTS_PER_GROUP)
    l2 = jnp.where(in_g, logits, -jnp.inf)
    v0 = jnp.max(l2, axis=-1, keepdims=True)
    i0 = jnp.min(jnp.where(l2 == v0, lane_f, float(LANES)), axis=-1, keepdims=True)
    l2b = jnp.where(lane_f == i0, -jnp.inf, l2)
    v1 = jnp.max(l2b, axis=-1, keepdims=True)
    i1 = jnp.min(jnp.where(l2b == v1, lane_f, float(LANES)), axis=-1, keepdims=True)
    ex = jnp.exp(v1 - v0)
    w0 = top1_p / (1.0 + ex)
    w1 = top1_p * ex / (1.0 + ex)
    ex0 = i0 - N_EXPERT_GROUPS
    ex1 = i1 - N_EXPERT_GROUPS
    is0 = lane_f == ex0
    is1e = lane_f == ex1
    oh0 = jnp.where(is0, 1.0, 0.0)
    oh1 = jnp.where(is1e, 1.0, 0.0)
    before0 = _dot(tri_ref[...], oh0.astype(BF16))
    before1 = _dot(tri_ref[...], oh1.astype(BF16))
    tot0 = jnp.sum(oh0, axis=0, keepdims=True)
    tot1 = jnp.sum(oh1, axis=0, keepdims=True)

    @pl.when(pl.program_id(0) == 0)
    def _():
        cnt_ref[...] = jnp.zeros(cnt_ref.shape, F32)

    seen = cnt_ref[...]
    rank0 = jnp.sum(jnp.where(is0, before0 + seen, 0.0), axis=-1, keepdims=True)
    rank1 = jnp.sum(jnp.where(is1e, before1 + seen + tot0, 0.0), axis=-1, keepdims=True)
    cnt_ref[...] = seen + tot0 + tot1
    counts_ref[...] = seen + tot0 + tot1
    route = jnp.where(lane == 0, ex0, jnp.where(lane == 1, ex1, jnp.where(lane == 2, w0, jnp.where(lane == 3, w1,
            jnp.where(lane == 4, rank0, jnp.where(lane == 5, rank1, 0.0))))))
    route_ref[...] = route


def _finish_call(yp, on, om, x2d, fw, *, tm):
    n, dm = x2d.shape
    full = lambda shape: pl.BlockSpec(shape, lambda i: (0,) * len(shape))
    rows = lambda w: pl.BlockSpec((tm, w), lambda i: (i, 0))
    tri = jnp.asarray(np.arange(tm)[None, :] < np.arange(tm)[:, None], BF16)
    return pl.pallas_call(
        _finish_kernel,
        grid=(n // tm,),
        in_specs=[rows(POOL_WIDTH), rows(NSA_WIDTH), rows(MEM_WIDTH), rows(dm), full((1, dm)), full((dm, dm)),
                  full((1, dm)), full((dm, LANES)), full((1, LANES)), full((tm, tm))],
        out_specs=[rows(dm), rows(dm), rows(LANES), full((1, LANES))],
        out_shape=[jax.ShapeDtypeStruct((n, dm), F32), jax.ShapeDtypeStruct((n, dm), F32),
                   jax.ShapeDtypeStruct((n, LANES), F32), jax.ShapeDtypeStruct((1, LANES), F32)],
        scratch_shapes=[pltpu.VMEM((1, LANES), F32)],
        compiler_params=_cparams("arbitrary"),
        name="finish",
    )(yp, on, om, x2d, fw["g_mix"], fw["w_out"], fw["g_ffn"], fw["w_r"], fw["b_r"], tri)


def _route_tables(route, counts, tm):
    n = route.shape[0]
    eid = route[:, 0:2].astype(I32)
    rank = route[:, 4:6].astype(I32)
    cnt = counts[0, :N_EXPERTS].astype(I32)
    tiles_per = jnp.maximum((cnt + tm - 1) // tm, 1)
    tile_end = jnp.cumsum(tiles_per)
    tile_start = tile_end - tiles_per
    experts = jnp.arange(N_EXPERTS, dtype=I32)
    start_of = jnp.sum(jnp.where(eid[:, :, None] == experts, tile_start, 0), axis=-1)
    pos = (start_of * tm + rank).reshape(-1)
    n_tiles = (2 * n) // tm + N_EXPERTS
    tj = jnp.arange(n_tiles, dtype=I32)
    tile_expert = jnp.minimum(jnp.sum((tj[:, None] >= tile_end[None, :]).astype(I32), axis=1), N_EXPERTS - 1)
    sel = tile_expert[:, None] == experts
    rows_left = jnp.sum(jnp.where(sel, cnt, 0), axis=-1) - (tj - jnp.sum(jnp.where(sel, tile_start, 0), axis=-1)) * tm
    tile_nvalid = jnp.where(tj < tile_end[-1], jnp.clip(rows_left, 0, tm), 0).astype(I32)
    return tile_expert, tile_nvalid, pos


def _row_wait_all(src_row, dst_row, sem, count):
    for _ in range(count):
        pltpu.make_async_copy(src_row, dst_row, sem).wait()


def _dispatch_kernel(pos_ref, nv_ref, hn_ref, xs_ref, stage, zbuf, sem, zsem, *, tm, tme, n_tiles):
    i = pl.program_id(0)
    nt = pl.num_programs(0)
    wait_slot = lambda s: _row_wait_all(stage.at[s, pl.ds(0, 1), :], xs_ref.at[pl.ds(0, 1), :], sem.at[s], 2 * tm)

    @pl.when(i == 0)
    def _():
        zbuf[...] = jnp.zeros(zbuf.shape, F32)

        def fill(j, c):
            @pl.when(nv_ref[j] < tme)
            def _():
                pltpu.make_async_copy(zbuf, xs_ref.at[pl.ds(pl.multiple_of(j * tme, tme), tme), :], zsem).start()
            return c

        def drain(j, c):
            @pl.when(nv_ref[j] < tme)
            def _():
                pltpu.make_async_copy(zbuf, xs_ref.at[pl.ds(0, tme), :], zsem).wait()
            return c

        lax.fori_loop(0, n_tiles, fill, 0)
        lax.fori_loop(0, n_tiles, drain, 0)

    for slot in range(2):
        @pl.when(i >= 1)
        def _():
            wait_slot(slot)

        stage[slot] = hn_ref[slot * tm:(slot + 1) * tm, :]
        base = (2 * i + slot) * (2 * tm)
        for r in range(tm):
            for k in range(2):
                dst = pos_ref[base + 2 * r + k]
                pltpu.make_async_copy(stage.at[slot, pl.ds(r, 1), :], xs_ref.at[pl.ds(dst, 1), :], sem.at[slot]).start()

    @pl.when(i == nt - 1)
    def _():
        wait_slot(0)
        wait_slot(1)


def _dispatch_call(pos, tile_nvalid, hn, *, tm, tme):
    n, dm = hn.shape
    n_tiles = tile_nvalid.shape[0]
    grid_spec = pltpu.PrefetchScalarGridSpec(
        num_scalar_prefetch=2,
        grid=(n // (2 * tm),),
        in_specs=[pl.BlockSpec((2 * tm, dm), lambda i, pos, nv: (i, 0))],
        out_specs=pl.BlockSpec(memory_space=pl.ANY),
        scratch_shapes=[pltpu.VMEM((2, tm, dm), F32), pltpu.VMEM((tme, dm), F32),
                        pltpu.SemaphoreType.DMA((2,)), pltpu.SemaphoreType.DMA(())],
    )
    return pl.pallas_call(
        functools.partial(_dispatch_kernel, tm=tm, tme=tme, n_tiles=n_tiles),
        grid_spec=grid_spec,
        out_shape=jax.ShapeDtypeStruct((n_tiles * tme, dm), F32),
        compiler_params=_cparams("arbitrary"),
        name="moe_dispatch",
    )(pos, tile_nvalid, hn)


def _expert_kernel(te_ref, nv_ref, x_ref, wg_ref, wu_ref, wd_ref, hns_ref, routes_ref, hs_ref,
                   y_ref, ysample_ref, wgb, wub, wdb, acc_s, *, tm):
    i = pl.program_id(0)
    prev = te_ref[jnp.maximum(i - 1, 0)]

    @pl.when(i == 0)
    def _():
        acc_s[...] = jnp.zeros(acc_s.shape, F32)

    @pl.when((i == 0) | (te_ref[i] != prev))
    def _():
        wgb[...] = wg_ref[0].astype(BF16)
        wub[...] = wu_ref[0].astype(BF16)
        wdb[...] = wd_ref[0].astype(BF16)
        route = routes_ref[...]
        ef = te_ref[i].astype(F32)
        comb = jnp.where(route[:, 0:1] == ef, route[:, 2:3], 0.0) + jnp.where(route[:, 1:2] == ef, route[:, 3:4], 0.0)
        xs = hns_ref[...].astype(BF16)
        hg = _dot(xs, wgb[...])
        hu = _dot(xs, wub[...])
        hmid = hg * jax.nn.sigmoid(hg) * hu * comb
        acc_s[...] += _dot(hmid.astype(BF16), wdb[...])

    @pl.when(i == pl.num_programs(0) - 1)
    def _():
        ysample_ref[...] = hs_ref[...] + acc_s[...]

    nv = nv_ref[i]

    @pl.when(nv == 0)
    def _():
        y_ref[...] = jnp.zeros(y_ref.shape, F32)

    @pl.when(nv > 0)
    def _():
        x = x_ref[...].astype(BF16)
        hg = _dot(x, wgb[...])
        hu = _dot(x, wub[...])
        hmid = hg * jax.nn.sigmoid(hg) * hu
        y_ref[...] = _dot(hmid.astype(BF16), wdb[...])


def _expert_call(tile_expert, tile_nvalid, xs, wg, wu, wd, hn_s, route_s, h_s, *, tm):
    n_tiles = tile_expert.shape[0]
    dm = xs.shape[1]
    ff = wg.shape[2]
    ns = hn_s.shape[0]
    full = lambda shape: pl.BlockSpec(shape, lambda i, te, nv: (0,) * len(shape))
    grid_spec = pltpu.PrefetchScalarGridSpec(
        num_scalar_prefetch=2,
        grid=(n_tiles,),
        in_specs=[pl.BlockSpec((tm, dm), lambda i, te, nv: (i, 0)),
                  pl.BlockSpec((1, dm, ff), lambda i, te, nv: (te[i], 0, 0)),
                  pl.BlockSpec((1, dm, ff), lambda i, te, nv: (te[i], 0, 0)),
                  pl.BlockSpec((1, ff, dm), lambda i, te, nv: (te[i], 0, 0)),
                  full((ns, dm)), full((ns, LANES)), full((ns, dm))],
        out_specs=[pl.BlockSpec((tm, dm), lambda i, te, nv: (i, 0)), full((ns, dm))],
        scratch_shapes=[pltpu.VMEM((dm, ff), BF16), pltpu.VMEM((dm, ff), BF16), pltpu.VMEM((ff, dm), BF16),
                        pltpu.VMEM((ns, dm), F32)],
    )
    return pl.pallas_call(
        functools.partial(_expert_kernel, tm=tm),
        grid_spec=grid_spec,
        out_shape=[jax.ShapeDtypeStruct(xs.shape, F32), jax.ShapeDtypeStruct((ns, dm), F32)],
        compiler_params=_cparams("arbitrary"),
        name="moe_experts",
    )(tile_expert, tile_nvalid, xs, wg, wu, wd, hn_s, route_s, h_s)


def _combine_kernel(pos_ref, ys_ref, h_ref, route_ref, o_ref, gbuf, sem, *, tm):
    i = pl.program_id(0)
    nt = pl.num_programs(0)

    def issue(half_tile, s):
        base = half_tile * (2 * tm)
        for r in range(tm):
            for k in range(2):
                src = pos_ref[base + 2 * r + k]
                pltpu.make_async_copy(ys_ref.at[pl.ds(src, 1), :], gbuf.at[s, k, pl.ds(r, 1), :], sem.at[s]).start()

    @pl.when(i == 0)
    def _():
        issue(0, 0)
        issue(1, 1)

    for s in range(2):
        _row_wait_all(ys_ref.at[pl.ds(0, 1), :], gbuf.at[s, 0, pl.ds(0, 1), :], sem.at[s], 2 * tm)
        rows = slice(s * tm, (s + 1) * tm)
        w0 = route_ref[rows, 2:3]
        w1 = route_ref[rows, 3:4]
        o_ref[rows, :] = h_ref[rows, :] + (w0 * gbuf[s, 0] + w1 * gbuf[s, 1])

        @pl.when(i + 1 < nt)
        def _():
            issue(2 * (i + 1) + s, s)


def _combine_call(pos, ys, h, route, *, tm):
    n, dm = h.shape
    grid_spec = pltpu.PrefetchScalarGridSpec(
        num_scalar_prefetch=1,
        grid=(n // (2 * tm),),
        in_specs=[pl.BlockSpec(memory_space=pl.ANY),
                  pl.BlockSpec((2 * tm, dm), lambda i, pos: (i, 0)),
                  pl.BlockSpec((2 * tm, LANES), lambda i, pos: (i, 0))],
        out_specs=pl.BlockSpec((2 * tm, dm), lambda i, pos: (i, 0)),
        scratch_shapes=[pltpu.VMEM((2, 2, tm, dm), F32), pltpu.SemaphoreType.DMA((2,))],
    )
    return pl.pallas_call(
        functools.partial(_combine_kernel, tm=tm),
        grid_spec=grid_spec,
        out_shape=jax.ShapeDtypeStruct((n, dm), F32),
        compiler_params=_cparams("arbitrary"),
        name="moe_combine",
    )(pos, ys, h, route)


def _prep_weights(l, g_attn, w_in, g_q, g_k, pe_cmp, w_cmp1, w_cmp2, w_pool, s_pool, g_mem, w_mem_kv, g_mq, g_mk,
                  g_mix, w_out, g_ffn, w_router1, b_router1, w_router2, b_router2):
    dm = w_in.shape[1]
    w = w_in[l]
    o1 = POOL_WIDTH
    o2 = o1 + NSA_WIDTH
    o3 = o2 + NSA_KV_WIDTH
    o4 = o3 + GATE_WIDTH
    wq = w[:, o1:o2].reshape(dm, NSA_HEADS, 1, HEAD_DIM)
    q_slot = jnp.asarray(np.eye(NSA_KV_HEADS, dtype=np.float32)[np.arange(NSA_HEADS) // NSA_REP])
    wq_pad = (wq * q_slot[None, :, :, None]).reshape(dm, Q_PAD)
    wqm = w[:, o4:].reshape(dm, MEM_HEADS, 1, HEAD_DIM)
    m_slot = jnp.asarray(np.eye(2, dtype=np.float32)[np.arange(MEM_HEADS) % 2])
    wqm_pad = (wqm * m_slot[None, :, :, None]).reshape(dm, QM_PAD)
    wg_pad = jnp.pad(w[:, o3:o4], ((0, 0), (0, LANES - GATE_WIDTH)))
    w_packed = jnp.concatenate([w[:, :o1], wq_pad, w[:, o2:o3], wqm_pad, wg_pad], axis=1).astype(BF16)
    half = ROPE_DIM // 2
    inv = jnp.power(ROPE_THETA, -jnp.arange(half, dtype=F32) * 2.0 / ROPE_DIM)
    d = np.arange(LANES) % HEAD_DIM
    inv_lane = jnp.where(jnp.asarray(d < ROPE_DIM), inv[jnp.asarray(d % half)], 0.0).reshape(1, LANES)
    seg = jnp.asarray((np.arange(LANES)[:, None] // HEAD_DIM == np.arange(LANES)[None, :] // HEAD_DIM), BF16)
    gk = jnp.concatenate([jnp.tile(g_k[l, br], 2) for br in range(N_BRANCH)]).reshape(1, N_BRANCH * LANES)
    proj = dict(g_attn=g_attn[l].reshape(1, dm), w_in=w_packed, gq=jnp.tile(g_q[l], Q_PAD // HEAD_DIM).reshape(1, Q_PAD),
                gk=gk, gmq=jnp.tile(g_mq[l], QM_PAD // HEAD_DIM).reshape(1, QM_PAD), inv=inv_lane, seg=seg)

    n_grp = len(POOL_WINDOWS)
    pg = POOL_WIDTH // n_grp
    w_bd = (jnp.asarray(np.eye(n_grp, dtype=np.float32))[:, None, :, None] * w_pool[l][:, :, None, :]
            ).reshape(POOL_WIDTH, POOL_WIDTH).astype(BF16)
    pool = dict(w=w_bd, s=s_pool[l].reshape(1, POOL_WIDTH))

    half_rows = CMP_STRIDE * HEAD_DIM
    w1 = jnp.concatenate([w_cmp1[l][:, :half_rows], w_cmp1[l][:, half_rows:]], axis=2).astype(BF16)
    cmp_w = dict(w=w1, pe=pe_cmp[l].reshape(2, 1, CMP_BLOCK * HEAD_DIM), w2=w_cmp2[l].astype(BF16))

    mem = dict(g=g_mem[l].reshape(1, dm), w=w_mem_kv[l].astype(BF16),
               gk=jnp.tile(g_mk[l], MEM_HEADS).reshape(1, MEM_WIDTH))
    w_r = jnp.concatenate([w_router1[l], w_router2[l].reshape(dm, N_EXPERTS)], axis=1)
    w_r = jnp.pad(w_r, ((0, 0), (0, LANES - w_r.shape[1]))).astype(BF16)
    b_r = jnp.concatenate([b_router1[l], b_router2[l].reshape(-1)])
    b_r = jnp.pad(b_r, (0, LANES - b_r.shape[0])).reshape(1, LANES)
    fin = dict(g_mix=g_mix[l].reshape(1, dm), w_out=w_out[l].astype(BF16), g_ffn=g_ffn[l].reshape(1, dm),
               w_r=w_r, b_r=b_r)
    return proj, pool, cmp_w, mem, fin


def _pick(n, prefs):
    for p in prefs:
        if n % p == 0:
            return p
    return n


def kernel(x_prompt, x_sample, cache_kv, cache_win, state_pool, cache_mem_kv, page_table, mem_prompt, g_attn, w_in, g_q, g_k, pe_cmp, w_cmp1, w_cmp2, w_pool, s_pool, g_mem, w_mem_kv, g_mq, g_mk, g_mix, w_out, g_ffn, w_router1, b_router1, w_router2, b_router2, w_gate, w_up, w_down):
    depth = w_in.shape[0]
    bp, t, dm = x_prompt.shape
    bs, ts, _ = x_sample.shape
    n_pages = page_table.shape[1]
    page = cache_kv.shape[2]
    past_len = n_pages * page
    n_win = cache_win.shape[2]
    ff = w_gate.shape[-1]
    tq, tk = 256, 512
    assert ts == 1 and n_win == WINDOW and page % SEL_BLOCK == 0 and t % tk == 0 and t >= WINDOW + tq
    n_sel = t // SEL_BLOCK
    assert min(SEL_TOPK, n_sel) >= 3 and n_sel <= LANES
    poolt = jnp.asarray(np.arange(t // CMP_STRIDE)[None, :] // CMP_PER_SEL == np.arange(n_sel)[:, None], BF16)
    key_blk = np.arange(t // tk)[:, None, None] * (tk // SEL_BLOCK) + np.arange(tk)[None, None, :] // SEL_BLOCK
    e_mat = jnp.asarray(np.arange(LANES)[None, :, None] == key_blk, BF16)
    n_cmp_s = past_len // CMP_STRIDE
    n_slot_s = -(-(past_len // SEL_BLOCK + 1) // LANES) * LANES
    pool_s = jnp.asarray(np.arange(n_cmp_s)[:, None] // CMP_PER_SEL == np.arange(n_slot_s)[None, :], BF16)

    hp, hs = x_prompt, x_sample
    outs = [[] for _ in range(7)]
    for l in range(depth):
        proj_w, pool_w, cmp_w, mem_w, fin_w = _prep_weights(
            l, g_attn, w_in, g_q, g_k, pe_cmp, w_cmp1, w_cmp2, w_pool, s_pool, g_mem, w_mem_kv, g_mq, g_mk,
            g_mix, w_out, g_ffn, w_router1, b_router1, w_router2, b_router2)
        wg = w_gate[l].reshape(N_EXPERTS, dm, ff)
        wu = w_up[l].reshape(N_EXPERTS, dm, ff)
        wd = w_down[l].reshape(N_EXPERTS, ff, dm)
        kw_cols = CMP_STRIDE * 2 * LANES

        n = bp * t
        xp2 = hp.reshape(n, dm)
        u, qb, rows4, rowsw, kvb, gates, qmb, rows4t, rowswt = _proj_call(
            xp2, proj_w, seq_len=t, pos0=0, tm=_pick(t, (512, 256, 128)), feature_major_rows4=True)
        u3 = u.reshape(bp, t, POOL_WIDTH)
        y_pool = _pool_call(u3, pool_w["w"], pool_w["s"], tp=_pick(t, (512, 256, 128)), pos0=0)
        n_slab = 2 * NSA_KV_HEADS
        chunk_w = CMP_STRIDE * HEAD_DIM
        kcvc = _compress_prompt_call(rows4.reshape(bp, t, 4 * LANES), jnp.zeros((bp, n_slab, chunk_w), F32), cmp_w)
        o_nsa = _nsa_prompt_call(qb.reshape(bp, t, Q_PAD), gates.reshape(bp, t, LANES), kcvc,
                                 kvb.reshape(bp, t, NSA_KV_WIDTH), poolt, e_mat, tq=tq, tk=tk)
        m_len = mem_prompt.shape[1]
        mkv, mkvb = _memkv_call(mem_prompt.reshape(bp * m_len, dm), mem_w["g"], mem_w["w"], mem_w["gk"],
                                proj_w["seg"], tm=_pick(bp * m_len, (256, 128)))
        o_mem = _memattn_call(qmb.reshape(bp, t, QM_PAD), mkvb.reshape(bp, m_len, 2 * MEM_WIDTH),
                              tq=_pick(t, (512, 256, 128)))
        tmf = _pick(n, (512, 256, 128))
        h_p, hn_p, route_p, counts_p = _finish_call(y_pool.reshape(n, POOL_WIDTH), o_nsa.reshape(n, NSA_WIDTH),
                                                    o_mem.reshape(n, MEM_WIDTH), xp2, fin_w, tm=tmf)
        tme = 256
        tmd = _pick(n, (256, 128))
        te, nv, pos = _route_tables(route_p, counts_p, tme)
        xs = _dispatch_call(pos, nv, hn_p, tm=tmd, tme=tme)

        xs2 = hs.reshape(bs, dm)
        u_s, qb_s, rows4_s, rowsw_s, _, gates_s, qmb_s = _proj_call(xs2, proj_w, seq_len=1, pos0=past_len, tm=bs)
        ext = jnp.concatenate([state_pool[l], u_s[:, None, :]], axis=1)
        y_pool_s = _pool_call(ext, pool_w["w"], pool_w["s"], tp=POOL_STATE + 1, pos0=past_len - POOL_STATE)[:, -1, :]
        cache_t = jnp.transpose(cache_kv[l], (0, 2, 3, 4, 1)).reshape(
            cache_kv.shape[1], N_KV_SLOTS, NSA_KV_HEADS * HEAD_DIM, page)
        tail_s = jnp.pad(rows4_s[:, :2 * LANES].reshape(bs, n_slab, HEAD_DIM), ((0, 0), (0, 0), (0, chunk_w - HEAD_DIM)))
        kcvc_s = _compress_sample_call(page_table, cache_t, tail_s, cmp_w)
        win_s = jnp.concatenate([cache_win[l].reshape(bs, n_win, 2 * LANES)[:, 1:], rowsw_s[:, None, :]], axis=1)
        win_t = jnp.transpose(cache_win[l], (0, 2, 3, 4, 1)).reshape(bs, 2, NSA_KV_HEADS * HEAD_DIM, n_win)
        mkv_t = jnp.transpose(cache_mem_kv[l], (0, 2, 3, 4, 1)).reshape(bs, 2, MEM_WIDTH, cache_mem_kv.shape[2])
        oc, ow, om_s, score_s = _sample_attn1_call(qb_s.reshape(bs, 1, Q_PAD), kcvc_s, win_t,
                                                   rowsw_s.reshape(bs, 1, 2 * LANES), mkv_t,
                                                   qmb_s.reshape(bs, 1, QM_PAD), pool_s, t_pos=past_len)
        n_top = min(SEL_TOPK, past_len // SEL_BLOCK + 1)
        idx = _sample_topk_call(score_s.reshape(bs * NSA_KV_HEADS, n_slot_s), n_top=n_top)
        o_nsa_s = _sample_sel_call(page_table, idx, cache_t, qb_s.reshape(bs, 1, Q_PAD),
                                   rows4_s[:, 2 * LANES:].reshape(bs, 1, 2 * LANES), gates_s.reshape(bs, 1, LANES),
                                   oc, ow, t_pos=past_len, n_top=n_top)
        h_s, hn_s, route_s, _ = _finish_call(y_pool_s, o_nsa_s.reshape(bs, NSA_WIDTH), om_s.reshape(bs, MEM_WIDTH),
                                          xs2, fin_w, tm=bs)

        ys, y_s = _expert_call(te, nv, xs, wg, wu, wd, hn_s, route_s, h_s, tm=tme)
        y_p = _combine_call(pos, ys, h_p, route_p, tm=tmd).reshape(bp, t, dm)
        y_s = y_s.reshape(bs, 1, dm)

        keep = min(WINDOW, t)
        outs[0].append(jnp.transpose(rows4t.reshape(bp, N_KV_SLOTS, NSA_KV_HEADS, HEAD_DIM, t), (0, 4, 1, 2, 3)))
        outs[1].append(rows4_s.reshape(bs, 1, N_KV_SLOTS, NSA_KV_HEADS, HEAD_DIM))
        outs[2].append(jnp.transpose(rowswt[:, :, t - keep:].reshape(bp, 2, NSA_KV_HEADS, HEAD_DIM, keep), (0, 4, 1, 2, 3)))
        outs[3].append(win_s.reshape(bs, n_win, 2, NSA_KV_HEADS, HEAD_DIM))
        outs[4].append(u3[:, t - POOL_STATE:])
        outs[5].append(ext[:, 1:])
        outs[6].append(mkv.reshape(bp, m_len, 2, MEM_HEADS, HEAD_DIM))
        hp, hs = y_p, y_s
    return (hp, hs) + tuple(jnp.stack(o) for o in outs)
```

```python
import functools

import numpy as np
import jax
import jax.numpy as jnp
from jax import lax
from jax.experimental import pallas as pl
from jax.experimental.pallas import tpu as pltpu

F32 = jnp.float32
BF16 = jnp.bfloat16
I32 = jnp.int32

HEAD_DIM = 64
POOL_WINDOWS = (2, 4, 8, 16)
POOL_STATE = max(POOL_WINDOWS) - 1
NSA_HEADS = 8
NSA_KV_HEADS = 2
NSA_REP = NSA_HEADS // NSA_KV_HEADS
N_BRANCH = 3
CMP_BLOCK = 32
CMP_STRIDE = 16
CMP_HIDDEN = 2 * HEAD_DIM
SEL_BLOCK = 64
SEL_TOPK = 16
CMP_PER_SEL = SEL_BLOCK // CMP_STRIDE
WINDOW = 512
FORCE_BONUS = 1000.0
MEM_HEADS = 4
ROPE_DIM = HEAD_DIM // 4
ROPE_THETA = 500000.0
N_EXPERT_GROUPS = 4
EXPERTS_PER_GROUP = 8
N_EXPERTS = N_EXPERT_GROUPS * EXPERTS_PER_GROUP
EPS = 1e-6
N_KV_SLOTS = 4

LANES = 128
POOL_WIDTH = 256
NSA_WIDTH = NSA_HEADS * HEAD_DIM
NSA_KV_WIDTH = N_BRANCH * 2 * NSA_KV_HEADS * HEAD_DIM
GATE_WIDTH = NSA_HEADS * N_BRANCH
MEM_WIDTH = MEM_HEADS * HEAD_DIM
Q_PAD = NSA_HEADS * LANES
QM_PAD = MEM_HEADS * LANES
C_U = 0
C_Q = C_U + POOL_WIDTH
C_KV = C_Q + Q_PAD
C_QM = C_KV + NSA_KV_WIDTH
C_G = C_QM + QM_PAD
C_END = C_G + LANES

NEG = -1e30
VMEM_LIMIT = 48 * 1024 * 1024

_NT = (((1,), (1,)), ((), ()))


def _cparams(*sem):
    return pltpu.CompilerParams(dimension_semantics=tuple(sem), vmem_limit_bytes=VMEM_LIMIT)


def _dot(a, b):
    return jnp.dot(a, b, preferred_element_type=F32)


def _dot_nt(a, b):
    return lax.dot_general(a, b, _NT, preferred_element_type=F32)


def _rms(x):
    return x * lax.rsqrt(jnp.mean(x * x, axis=-1, keepdims=True) + EPS)


def _masked_softmax(s, mask):
    sm = jnp.where(mask, s, NEG)
    m = jnp.max(sm, axis=-1, keepdims=True)
    e = jnp.where(mask, jnp.exp(sm - m), 0.0)
    return e / jnp.maximum(jnp.sum(e, axis=-1, keepdims=True), 1e-30)


def _split3(x):
    hi = x.astype(BF16)
    r1 = x - hi.astype(F32)
    mid = r1.astype(BF16)
    lo = (r1 - mid.astype(F32)).astype(BF16)
    return hi, mid, lo


def _proj_kernel(x_ref, ga_ref, w_ref, gq_ref, gk_ref, gmq_ref, inv_ref, seg_ref,
                 u_ref, q_ref, rows4_ref, rowsw_ref, kvb_ref, gates_ref, qm_ref, *rest,
                 tm, seq_len, pos0, consecutive):
    *maybe_t_refs, cos_in_tile, sin_in_tile = rest
    i = pl.program_id(0)
    a = _rms(x_ref[...]) * ga_ref[...]
    z = _dot(a.astype(BF16), w_ref[...])
    u_ref[...] = z[:, C_U:C_U + POOL_WIDTH]

    inv = inv_ref[...]

    @pl.when(i == 0)
    def _():
        j = lax.broadcasted_iota(I32, (tm, 1), 0).astype(F32) if consecutive else jnp.zeros((tm, 1), F32)
        cos_in_tile[...] = jnp.cos(j * inv)
        sin_in_tile[...] = jnp.sin(j * inv)

    first = (pos0 + (i * tm) % seq_len).astype(F32) * jnp.broadcast_to(inv, (8, LANES))
    cos_f, sin_f = jnp.cos(first)[0:1], jnp.sin(first)[0:1]
    cos = cos_f * cos_in_tile[...] - sin_f * sin_in_tile[...]
    sin = sin_f * cos_in_tile[...] + cos_f * sin_in_tile[...]
    d = lax.broadcasted_iota(I32, (1, LANES), 1) % HEAD_DIM
    half = ROPE_DIM // 2
    s_next = jnp.where(d < half, -sin, 0.0)
    s_prev = jnp.where((d >= half) & (d < ROPE_DIM), sin, 0.0)
    seg = seg_ref[...]

    def head_norm(xc, g):
        ssq = _dot((xc * xc).astype(BF16), seg)
        return xc * lax.rsqrt(ssq * (1.0 / HEAD_DIM) + EPS) * g

    def rope(xc):
        return xc * cos + pltpu.roll(xc, LANES - half, 1) * s_next + pltpu.roll(xc, half, 1) * s_prev

    scale = HEAD_DIM ** -0.5
    for c in range(NSA_HEADS):
        sl = slice(c * LANES, (c + 1) * LANES)
        qc = rope(head_norm(z[:, C_Q + c * LANES:C_Q + (c + 1) * LANES], gq_ref[:, sl]))
        q_ref[:, sl] = (qc * scale).astype(BF16)
    for br in range(N_BRANCH):
        k0 = C_KV + br * 2 * LANES
        kn = rope(head_norm(z[:, k0:k0 + LANES], gk_ref[:, br * LANES:(br + 1) * LANES]))
        vv = z[:, k0 + LANES:k0 + 2 * LANES]
        kvb_ref[:, br * 2 * LANES:br * 2 * LANES + LANES] = kn.astype(BF16)
        kvb_ref[:, br * 2 * LANES + LANES:(br + 1) * 2 * LANES] = vv.astype(BF16)
        if br < 2:
            rows4_ref[:, br * 2 * LANES:br * 2 * LANES + LANES] = kn
            rows4_ref[:, br * 2 * LANES + LANES:(br + 1) * 2 * LANES] = vv
            if maybe_t_refs:
                maybe_t_refs[0][0, br * 2 * LANES:br * 2 * LANES + LANES, :] = kn.T
                maybe_t_refs[0][0, br * 2 * LANES + LANES:(br + 1) * 2 * LANES, :] = vv.T
        else:
            rowsw_ref[:, :LANES] = kn
            rowsw_ref[:, LANES:] = vv
            if maybe_t_refs:
                maybe_t_refs[1][0, :LANES, :] = kn.T
                maybe_t_refs[1][0, LANES:, :] = vv.T
    for c in range(MEM_HEADS):
        sl = slice(c * LANES, (c + 1) * LANES)
        qmc = head_norm(z[:, C_QM + c * LANES:C_QM + (c + 1) * LANES], gmq_ref[:, sl])
        qm_ref[:, sl] = (qmc * scale).astype(BF16)
    gates_ref[...] = jax.nn.sigmoid(z[:, C_G:C_END])


def _proj_call(x2d, pw, *, seq_len, pos0, tm, feature_major_rows4=False):
    n, dm = x2d.shape
    full = lambda shape: pl.BlockSpec(shape, lambda i: (0,) * len(shape))
    rows = lambda w: pl.BlockSpec((tm, w), lambda i: (i, 0))
    outs = [(POOL_WIDTH, F32), (Q_PAD, BF16), (4 * LANES, F32), (2 * LANES, F32), (NSA_KV_WIDTH, BF16),
            (LANES, F32), (QM_PAD, BF16)]
    out_specs = [rows(w) for w, _ in outs]
    out_shape = [jax.ShapeDtypeStruct((n, w), dt) for w, dt in outs]
    if feature_major_rows4:
        per_seq = seq_len // tm
        for width in (4 * LANES, 2 * LANES):
            out_specs.append(pl.BlockSpec((1, width, tm), lambda i: (i // per_seq, 0, i % per_seq)))
            out_shape.append(jax.ShapeDtypeStruct((n // seq_len, width, seq_len), F32))
    consecutive = seq_len % tm == 0
    assert consecutive or seq_len == 1
    return pl.pallas_call(
        functools.partial(_proj_kernel, tm=tm, seq_len=seq_len, pos0=pos0, consecutive=consecutive),
        grid=(n // tm,),
        in_specs=[rows(dm), full((1, dm)), full((dm, C_END)), full((1, Q_PAD)), full((1, N_BRANCH * LANES)),
                  full((1, QM_PAD)), full((1, LANES)), full((LANES, LANES))],
        out_specs=out_specs,
        out_shape=out_shape,
        scratch_shapes=[pltpu.VMEM((tm, LANES), F32), pltpu.VMEM((tm, LANES), F32)],
        compiler_params=_cparams("arbitrary"),
        name="proj",
    )(x2d, pw["g_attn"], pw["w_in"], pw["gq"], pw["gk"], pw["gmq"], pw["inv"], pw["seg"])


def _pool_kernel(u_ref, halo_ref, w_ref, s_ref, y_ref, *, tp, pos0):
    i = pl.program_id(1)
    u = u_ref[0]
    halo = halo_ref[0] * (i > 0).astype(F32)
    n_h = POOL_STATE + 1
    ext = jnp.concatenate([halo, u], axis=0)
    sums = {1: ext}
    w = 1
    while w < max(POOL_WINDOWS):
        sums[2 * w] = sums[w] + pltpu.roll(sums[w], w, 0)
        w *= 2
    pos = pos0 + i * tp + lax.broadcasted_iota(I32, (tp, 1), 0)
    lane_grp = lax.broadcasted_iota(I32, (1, POOL_WIDTH), 1) // (POOL_WIDTH // len(POOL_WINDOWS))
    mean = jnp.zeros((tp, POOL_WIDTH), F32)
    for gi, wdw in enumerate(POOL_WINDOWS):
        cnt = jnp.minimum(pos + 1, wdw).astype(F32)
        mean = jnp.where(lane_grp == gi, sums[wdw][n_h:] / cnt, mean)
    r = mean - u
    y_ref[0] = _dot(r.astype(BF16), w_ref[...]) * s_ref[...]


def _pool_call(u3, w_bd, s_pool, *, tp, pos0):
    b, t, c = u3.shape
    n_h = POOL_STATE + 1
    return pl.pallas_call(
        functools.partial(_pool_kernel, tp=tp, pos0=pos0),
        grid=(b, t // tp),
        in_specs=[pl.BlockSpec((1, tp, c), lambda bi, i: (bi, i, 0)),
                  pl.BlockSpec((1, n_h, c), lambda bi, i: (bi, jnp.maximum(i * (tp // n_h) - 1, 0), 0)),
                  pl.BlockSpec((c, c), lambda bi, i: (0, 0)),
                  pl.BlockSpec((1, c), lambda bi, i: (0, 0))],
        out_specs=pl.BlockSpec((1, tp, c), lambda bi, i: (bi, i, 0)),
        out_shape=jax.ShapeDtypeStruct((b, t, c), F32),
        compiler_params=_cparams("parallel", "parallel"),
        name="pool",
    )(u3, u3, w_bd, s_pool)


def _split_chunks(x, n_chunks):
    return jnp.transpose(x.reshape(n_chunks, CMP_STRIDE, LANES), (1, 0, 2))


def _compress_core(load_slab, tail_ref, pe_ref, w_ref, w2_ref, n):
    rowid = lax.broadcasted_iota(I32, (n, 1), 0)
    outs = []
    for c in range(2):
        w = w_ref[c]
        pe8 = jnp.broadcast_to(pe_ref[c], (8, 2 * CMP_STRIDE * HEAD_DIM)).astype(BF16)
        pe_first = _dot(pe8[:, :CMP_STRIDE * HEAD_DIM], w)[0:1, :CMP_HIDDEN]
        pe_second = _dot(pe8[:, CMP_STRIDE * HEAD_DIM:], w)[0:1, CMP_HIDDEN:]
        pe_const = pe_first + pe_second
        lane = lax.broadcasted_iota(I32, (1, LANES), 1)
        per_group = [[], []]
        for q in range(CMP_STRIDE // 2):
            a, b = load_slab(c, 2 * q), load_slab(c, 2 * q + 1)
            per_group[0].append(jnp.where(lane < HEAD_DIM, a, pltpu.roll(b, HEAD_DIM, 1)))
            per_group[1].append(jnp.where(lane < HEAD_DIM, pltpu.roll(a, HEAD_DIM, 1), b))
        x2 = jnp.concatenate([jnp.concatenate(pg, axis=1) for pg in per_group], axis=0).astype(BF16)
        z = _dot(x2, w)
        hs = []
        for g in range(NSA_KV_HEADS):
            tail8 = jnp.broadcast_to(tail_ref[0, 2 * c + g:2 * c + g + 1, :], (8, CMP_STRIDE * HEAD_DIM)).astype(BF16)
            second_tail = _dot(tail8, w)[0:1, CMP_HIDDEN:]
            first = z[g * n:(g + 1) * n, :CMP_HIDDEN]
            second = z[g * n:(g + 1) * n, CMP_HIDDEN:]
            shifted = pltpu.roll(second, n - 1, 0)
            h = first + jnp.where(rowid == n - 1, second_tail, shifted) + pe_const
            hs.append(jax.nn.gelu(h, approximate=True))
        o = _dot(jnp.concatenate(hs, axis=0).astype(BF16), w2_ref[c])
        outs += [o[:n], o[n:]]
    return jnp.concatenate(outs, axis=1)


def _compress_prompt_kernel(k_ref, v_ref, tail_ref, pe_ref, w_ref, w2_ref, out_ref, *, n):
    slabs = [_split_chunks(ref[0], n) for ref in (k_ref, v_ref)]
    out_ref[0] = _compress_core(lambda c, r: slabs[c][r], tail_ref, pe_ref, w_ref, w2_ref, n).astype(BF16)


def _compress_prompt_call(rows4_3d, tail, cw):
    b, t, _ = rows4_3d.shape
    n = t // CMP_STRIDE
    full = lambda a: pl.BlockSpec(a.shape, lambda bi: (0,) * a.ndim)
    return pl.pallas_call(
        functools.partial(_compress_prompt_kernel, n=n),
        grid=(b,),
        in_specs=[pl.BlockSpec((1, t, LANES), lambda bi: (bi, 0, 0)),
                  pl.BlockSpec((1, t, LANES), lambda bi: (bi, 0, 1)),
                  pl.BlockSpec((1,) + tail.shape[1:], lambda bi: (bi, 0, 0)),
                  full(cw["pe"]), full(cw["w"]), full(cw["w2"])],
        out_specs=pl.BlockSpec((1, n, 2 * LANES), lambda bi: (bi, 0, 0)),
        out_shape=jax.ShapeDtypeStruct((b, n, 2 * LANES), BF16),
        compiler_params=_cparams("parallel"),
        name="compress_prompt",
    )(rows4_3d, rows4_3d, tail, cw["pe"], cw["w"], cw["w2"])


def _compress_sample_kernel(pt_ref, cache_ref, tail_ref, pe_ref, w_ref, w2_ref, out_ref,
                            buf, slabs, sem, *, n, n_pages, page):
    b = pl.program_id(0)
    nb = pl.num_programs(0)

    def page_copy(bb, slot, p, c):
        return pltpu.make_async_copy(cache_ref.at[pt_ref[bb * n_pages + p], c], buf.at[slot, c, p], sem.at[slot])

    def for_all(fn):
        for p in range(n_pages):
            for c in range(2):
                fn(p, c)

    def issue(bb, slot):
        for_all(lambda p, c: page_copy(bb, slot, p, c).start())

    @pl.when(b == 0)
    def _():
        issue(0, 0)

    @pl.when(b + 1 < nb)
    def _():
        issue(b + 1, (b + 1) % 2)

    slot = b % 2
    for_all(lambda p, c: page_copy(b, slot, p, c).wait())

    unroll = next(u for u in (8, 4, 2, 1) if n_pages % u == 0)

    cpp = page // CMP_STRIDE

    def to_slabs(pp, carry):
        for k in range(unroll):
            p = pp * unroll + k
            for c in range(2):
                by_r = _split_chunks(buf[slot, c, p].T, cpp)
                for r in range(CMP_STRIDE):
                    slabs[c, r, pl.ds(pl.multiple_of(p * cpp, cpp), cpp), :] = by_r[r]
        return carry

    lax.fori_loop(0, n_pages // unroll, to_slabs, 0)
    out_ref[0] = _compress_core(lambda c, r: slabs[c, r], tail_ref, pe_ref, w_ref, w2_ref, n).astype(BF16)


def _compress_sample_call(page_table, cache_t, tail, cw):
    b, n_pages = page_table.shape
    page = cache_t.shape[3]
    n = n_pages * page // CMP_STRIDE
    full = lambda a: pl.BlockSpec(a.shape, lambda bi, pt: (0,) * a.ndim)
    grid_spec = pltpu.PrefetchScalarGridSpec(
        num_scalar_prefetch=1,
        grid=(b,),
        in_specs=[pl.BlockSpec(memory_space=pl.ANY),
                  pl.BlockSpec((1,) + tail.shape[1:], lambda bi, pt: (bi, 0, 0)),
                  full(cw["pe"]), full(cw["w"]), full(cw["w2"])],
        out_specs=pl.BlockSpec((1, n, 2 * LANES), lambda bi, pt: (bi, 0, 0)),
        scratch_shapes=[pltpu.VMEM((2, 2, n_pages, LANES, page), F32), pltpu.VMEM((2, CMP_STRIDE, n, LANES), F32),
                        pltpu.SemaphoreType.DMA((2,))],
    )
    return pl.pallas_call(
        functools.partial(_compress_sample_kernel, n=n, n_pages=n_pages, page=page),
        grid_spec=grid_spec,
        out_shape=jax.ShapeDtypeStruct((b, n, 2 * LANES), BF16),
        compiler_params=_cparams("arbitrary"),
        name="compress_sample",
    )(page_table.reshape(-1), cache_t, tail, cw["pe"], cw["w"], cw["w2"])


def _topk_mask(score, ids, n_top, axis):
    sel = jnp.zeros(score.shape, jnp.bool_)
    work = score
    firsts = []
    big = float(score.shape[axis])
    for _ in range(n_top):
        m = jnp.max(work, axis=axis, keepdims=True)
        first = jnp.min(jnp.where(work == m, ids, big), axis=axis, keepdims=True)
        pick = (ids == first) & (m > -jnp.inf)
        sel = sel | pick
        work = jnp.where(pick, -jnp.inf, work)
        firsts.append(jnp.where(m > -jnp.inf, first, -1.0))
    return sel, firsts


def _gate_and_pack(o_ref_store, gates, o_c, o_s, o_w, rows):
    lane = lax.broadcasted_iota(I32, (1, LANES), 1)
    heads = []
    for h in range(NSA_HEADS):
        g, r = divmod(h, NSA_REP)
        rs = slice(r * rows, (r + 1) * rows)
        gc, gs, gw = (gates[:, N_BRANCH * h + k:N_BRANCH * h + k + 1] for k in range(N_BRANCH))
        heads.append(gc * o_c[g][rs] + gs * o_s[g][rs] + gw * o_w[g][rs])
    for j in range(NSA_HEADS // 2):
        a, b = heads[2 * j], heads[2 * j + 1]
        if (2 * j) // NSA_REP == 0:
            chunk = jnp.where(lane < HEAD_DIM, a, pltpu.roll(b, HEAD_DIM, 1))
        else:
            chunk = jnp.where(lane < HEAD_DIM, pltpu.roll(a, HEAD_DIM, 1), b)
        o_ref_store(j, chunk)


def _nsa_prompt_kernel(q_ref, gate_ref, kc_ref, kv_ref, poolt_ref, e_ref, o_ref, acc_ref, *, tq, tk, t_len):
    i = pl.program_id(1)
    s0 = i * tq
    rows4 = NSA_REP * tq
    tpos = s0 + lax.broadcasted_iota(I32, (tq, 1), 0)
    rep = lambda x: jnp.concatenate([x] * NSA_REP, axis=0)
    n_cmp = kc_ref.shape[1]
    kc128 = kc_ref[0, :, :LANES]
    vc128 = kc_ref[0, :, LANES:]
    kc_end = lax.broadcasted_iota(I32, (1, n_cmp), 1) * CMP_STRIDE + (CMP_BLOCK - 1)
    bias_c = rep(jnp.where(kc_end <= tpos, 0.0, NEG))
    any_c = rep((tpos >= CMP_BLOCK - 1).astype(F32))
    n_sel = t_len // SEL_BLOCK
    blk = lax.broadcasted_iota(I32, (n_sel, 1), 0)
    blk_f = blk.astype(F32)
    tq_lane = s0 + lax.broadcasted_iota(I32, (1, tq), 1)
    cur = tq_lane // SEL_BLOCK
    valid = blk * SEL_BLOCK <= tq_lane
    forced = (blk == 0) | (blk == cur) | (blk == cur - 1)
    w_start = pl.multiple_of(jnp.maximum(s0 - WINDOW, 0), LANES)
    w_len = WINDOW + tq
    kpos_w = w_start + lax.broadcasted_iota(I32, (1, w_len), 1)
    dist = tpos - kpos_w
    bias_w = rep(jnp.where((dist >= 0) & (dist < WINDOW), 0.0, NEG))
    c_diag = s0 // tk
    kpos_d = c_diag * tk + lax.broadcasted_iota(I32, (1, tk), 1)
    bias_causal = jnp.where(kpos_d <= tpos, 0.0, NEG)

    o_c, o_s, o_w, qgs, selbs = [], [], [], [], []
    for g in range(NSA_KV_HEADS):
        qg = jnp.concatenate([q_ref[0, :, (g * NSA_REP + r) * LANES:(g * NSA_REP + r + 1) * LANES]
                              for r in range(NSA_REP)], axis=0)
        qgs.append(qg)
        s_c = _dot_nt(qg, kc128) + bias_c
        e_c = jnp.exp(s_c - jnp.max(s_c, axis=-1, keepdims=True))
        p_c = e_c * (any_c / jnp.sum(e_c, axis=-1, keepdims=True))
        o_c.append(_dot(p_c.astype(BF16), vc128))
        imp = p_c[0:tq]
        for r in range(1, NSA_REP):
            imp = imp + p_c[r * tq:(r + 1) * tq]
        imp_t = sum(_dot_nt(poolt_ref[...], piece) for piece in _split3(imp))
        score = jnp.where(valid, imp_t + jnp.where(forced, FORCE_BONUS, 0.0), -jnp.inf)
        sel_t, _ = _topk_mask(score, blk_f, min(SEL_TOPK, n_sel), 0)
        selb_t = jnp.concatenate([jnp.where(sel_t, 0.0, NEG), jnp.full(((-n_sel) % LANES, tq), NEG, F32)], axis=0)
        selbs.append(selb_t.T.astype(BF16))

        kw = kv_ref[0, pl.ds(w_start, w_len), 4 * LANES:5 * LANES]
        vw = kv_ref[0, pl.ds(w_start, w_len), 5 * LANES:6 * LANES]
        s_w = _dot_nt(qg, kw) + bias_w
        e_w = jnp.exp(s_w - jnp.max(s_w, axis=-1, keepdims=True))
        o_w.append(_dot(e_w.astype(BF16), vw) * (1.0 / jnp.sum(e_w, axis=-1, keepdims=True)))

    acc_ref[...] = jnp.zeros(acc_ref.shape, F32)

    q_all = jnp.concatenate(qgs, axis=0)
    selb_all = jnp.concatenate(selbs, axis=0)
    rows8 = NSA_KV_HEADS * rows4

    def chunk_step(c, carry, extra_bias):
        m, l = carry
        k0 = pl.multiple_of(c * tk, tk)
        ks = kv_ref[0, pl.ds(k0, tk), 2 * LANES:3 * LANES]
        vs = kv_ref[0, pl.ds(k0, tk), 3 * LANES:4 * LANES]
        bias = _dot(selb_all, e_ref[c])
        if extra_bias is not None:
            bias = bias + jnp.concatenate([extra_bias] * NSA_KV_HEADS, axis=0)
        bias_rows = jnp.concatenate([rep(bias[g * tq:(g + 1) * tq]) for g in range(NSA_KV_HEADS)], axis=0)
        s = _dot_nt(q_all, ks) + bias_rows
        m_new = jnp.maximum(m, jnp.max(s, axis=-1, keepdims=True))
        alpha = jnp.exp(m - m_new)
        p = jnp.exp(s - m_new)
        l_new = alpha * l + jnp.sum(p, axis=-1, keepdims=True)
        acc_ref[...] = alpha * acc_ref[...] + _dot(p.astype(BF16), vs)
        return m_new, l_new

    init = (jnp.full((rows8, 1), NEG, F32), jnp.zeros((rows8, 1), F32))
    carry = lax.fori_loop(0, c_diag, lambda c, cr: chunk_step(c, cr, None), init)
    _, l_fin = chunk_step(c_diag, carry, bias_causal)
    o_all = acc_ref[...] * (1.0 / jnp.maximum(l_fin, 1e-30))
    for g in range(NSA_KV_HEADS):
        o_s.append(o_all[g * rows4:(g + 1) * rows4])

    def store(j, chunk):
        o_ref[0, :, j * LANES:(j + 1) * LANES] = chunk

    _gate_and_pack(store, gate_ref[0], o_c, o_s, o_w, tq)


def _nsa_prompt_call(q3, gates3, kcvc, kvb3, poolt, e_mat, *, tq, tk):
    b, t, _ = q3.shape
    n_cmp = kcvc.shape[1]
    return pl.pallas_call(
        functools.partial(_nsa_prompt_kernel, tq=tq, tk=tk, t_len=t),
        grid=(b, t // tq),
        in_specs=[pl.BlockSpec((1, tq, Q_PAD), lambda bi, i: (bi, i, 0)),
                  pl.BlockSpec((1, tq, LANES), lambda bi, i: (bi, i, 0)),
                  pl.BlockSpec((1, n_cmp, 2 * LANES), lambda bi, i: (bi, 0, 0)),
                  pl.BlockSpec((1, t, NSA_KV_WIDTH), lambda bi, i: (bi, 0, 0)),
                  pl.BlockSpec(poolt.shape, lambda bi, i: (0, 0)),
                  pl.BlockSpec(e_mat.shape, lambda bi, i: (0, 0, 0))],
        out_specs=pl.BlockSpec((1, tq, NSA_WIDTH), lambda bi, i: (bi, i, 0)),
        out_shape=jax.ShapeDtypeStruct((b, t, NSA_WIDTH), F32),
        scratch_shapes=[pltpu.VMEM((NSA_KV_HEADS * NSA_REP * tq, LANES), F32)],
        compiler_params=_cparams("parallel", "arbitrary"),
        name="nsa_prompt",
    )(q3, gates3, kcvc, kvb3, poolt, e_mat)


def _memkv_kernel(x_ref, g_ref, w_ref, gk_ref, seg_ref, o_ref, ob_ref):
    a = _rms(x_ref[...]) * g_ref[...]
    z = _dot(a.astype(BF16), w_ref[...])
    for c in range(MEM_WIDTH // LANES):
        sl = slice(c * LANES, (c + 1) * LANES)
        xc = z[:, sl]
        ssq = _dot((xc * xc).astype(BF16), seg_ref[...])
        kn = xc * lax.rsqrt(ssq * (1.0 / HEAD_DIM) + EPS) * gk_ref[:, sl]
        o_ref[:, sl] = kn
        ob_ref[:, sl] = kn.astype(BF16)
    o_ref[:, MEM_WIDTH:] = z[:, MEM_WIDTH:]
    ob_ref[:, MEM_WIDTH:] = z[:, MEM_WIDTH:].astype(BF16)


def _memkv_call(mem2d, g_mem, w_mem, gmk, seg, *, tm):
    n, dm = mem2d.shape
    full = lambda shape: pl.BlockSpec(shape, lambda i: (0,) * len(shape))
    return pl.pallas_call(
        _memkv_kernel,
        grid=(n // tm,),
        in_specs=[pl.BlockSpec((tm, dm), lambda i: (i, 0)), full((1, dm)), full((dm, 2 * MEM_WIDTH)),
                  full((1, MEM_WIDTH)), full((LANES, LANES))],
        out_specs=[pl.BlockSpec((tm, 2 * MEM_WIDTH), lambda i: (i, 0))] * 2,
        out_shape=[jax.ShapeDtypeStruct((n, 2 * MEM_WIDTH), F32), jax.ShapeDtypeStruct((n, 2 * MEM_WIDTH), BF16)],
        compiler_params=_cparams("parallel"),
        name="memkv",
    )(mem2d, g_mem, w_mem, gmk, seg)


def _mem_attend(qm, mkv, rows):
    lane = lax.broadcasted_iota(I32, (1, LANES), 1)
    chunks = []
    for j in range(MEM_HEADS // 2):
        k128 = mkv[:, j * LANES:(j + 1) * LANES]
        v128 = mkv[:, MEM_WIDTH + j * LANES:MEM_WIDTH + (j + 1) * LANES]
        q2 = jnp.concatenate([qm[2 * j], qm[2 * j + 1]], axis=0)
        s = _dot_nt(q2, k128)
        e = jnp.exp(s - jnp.max(s, axis=-1, keepdims=True))
        p = e / jnp.sum(e, axis=-1, keepdims=True)
        o = _dot(p.astype(BF16), v128)
        chunks.append(jnp.where(lane < HEAD_DIM, o[:rows], o[rows:2 * rows]))
    return chunks


def _memattn_kernel(qm_ref, mkv_ref, o_ref, *, tq):
    qm = [qm_ref[0, :, h * LANES:(h + 1) * LANES] for h in range(MEM_HEADS)]
    for j, chunk in enumerate(_mem_attend(qm, mkv_ref[0], tq)):
        o_ref[0, :, j * LANES:(j + 1) * LANES] = chunk


def _memattn_call(qm3, mkv3, *, tq):
    b, t, _ = qm3.shape
    m = mkv3.shape[1]
    return pl.pallas_call(
        functools.partial(_memattn_kernel, tq=tq),
        grid=(b, t // tq),
        in_specs=[pl.BlockSpec((1, tq, QM_PAD), lambda bi, i: (bi, i, 0)),
                  pl.BlockSpec((1, m, 2 * MEM_WIDTH), lambda bi, i: (bi, 0, 0))],
        out_specs=pl.BlockSpec((1, tq, MEM_WIDTH), lambda bi, i: (bi, i, 0)),
        out_shape=jax.ShapeDtypeStruct((b, t, MEM_WIDTH), F32),
        compiler_params=_cparams("parallel", "parallel"),
        name="memattn",
    )(qm3, mkv3)


def _pad_rows(rows_list):
    x = jnp.concatenate(rows_list, axis=0)
    return jnp.concatenate([x, jnp.zeros((8 - x.shape[0], x.shape[1]), x.dtype)], axis=0)


def _sample_attn1_kernel(q_ref, kc_ref, win_ref, wnew_ref, mkv_ref, qm_ref, pool_ref,
                         oc_ref, ow_ref, om_ref, score_ref, *, t_pos, n_win):
    q = q_ref[0].astype(F32)
    n_cmp = kc_ref.shape[1]
    kc128 = kc_ref[0, :, :LANES]
    vc128 = kc_ref[0, :, LANES:]
    kw_t = win_ref[0, 0].astype(BF16)
    vw_t = win_ref[0, 1].astype(BF16)
    wnew = wnew_ref[0]
    kw_new = wnew[:, :LANES].astype(BF16).astype(F32)
    vw_new = wnew[:, LANES:].astype(BF16).astype(F32)
    kc_end = lax.broadcasted_iota(I32, (1, n_cmp), 1) * CMP_STRIDE + (CMP_BLOCK - 1)
    mask_c = kc_end <= t_pos
    kw_pos = t_pos - n_win + lax.broadcasted_iota(I32, (1, n_win), 1)
    mask_w = (t_pos - kw_pos >= 0) & (t_pos - kw_pos < WINDOW) & (kw_pos >= 0)
    n_slot = pool_ref.shape[1]
    blk = lax.broadcasted_iota(I32, (1, n_slot), 1)
    cur = t_pos // SEL_BLOCK
    valid = blk * SEL_BLOCK <= t_pos
    forced = (blk == 0) | (blk == cur) | (blk == cur - 1)
    for g in range(NSA_KV_HEADS):
        qg_f = _pad_rows([q[:, (g * NSA_REP + r) * LANES:(g * NSA_REP + r + 1) * LANES] for r in range(NSA_REP)])
        qg = qg_f.astype(BF16)
        p_c = _masked_softmax(_dot_nt(qg, kc128), mask_c)
        oc_ref[0, g * NSA_REP:(g + 1) * NSA_REP, :] = _dot(p_c.astype(BF16), vc128)[:NSA_REP]
        imp = jnp.sum(p_c[:NSA_REP], axis=0, keepdims=True)
        imp8 = jnp.broadcast_to(imp, (8, n_cmp))
        imp_b = sum(_dot(piece, pool_ref[...]) for piece in _split3(imp8))[0:1]
        score_ref[0, g:g + 1, :] = jnp.where(valid, imp_b + jnp.where(forced, FORCE_BONUS, 0.0), -jnp.inf)
        s_old = jnp.where(mask_w, _dot(qg, kw_t), NEG)
        s_new = jnp.sum(qg_f * kw_new, axis=-1, keepdims=True)
        m_w = jnp.maximum(jnp.max(s_old, axis=-1, keepdims=True), s_new)
        e_old = jnp.where(mask_w, jnp.exp(s_old - m_w), 0.0)
        e_new = jnp.exp(s_new - m_w)
        den = jnp.sum(e_old, axis=-1, keepdims=True) + e_new
        o_win = (_dot_nt(e_old.astype(BF16), vw_t) + e_new.astype(BF16).astype(F32) * vw_new) / den
        ow_ref[0, g * NSA_REP:(g + 1) * NSA_REP, :] = o_win[:NSA_REP]
    qm = qm_ref[0].astype(F32)
    lane = lax.broadcasted_iota(I32, (1, LANES), 1)
    mk_t = mkv_ref[0, 0].astype(BF16)
    mv_t = mkv_ref[0, 1].astype(BF16)
    for j in range(MEM_HEADS // 2):
        q2 = _pad_rows([qm[:, h * LANES:(h + 1) * LANES] for h in (2 * j, 2 * j + 1)]).astype(BF16)
        s = _dot(q2, mk_t[j * LANES:(j + 1) * LANES, :])
        e = jnp.exp(s - jnp.max(s, axis=-1, keepdims=True))
        p = e / jnp.sum(e, axis=-1, keepdims=True)
        o = _dot_nt(p.astype(BF16), mv_t[j * LANES:(j + 1) * LANES, :])
        om_ref[0, :, j * LANES:(j + 1) * LANES] = jnp.where(lane < HEAD_DIM, o[0:1], o[1:2])


def _sample_attn1_call(q3, kcvc, win_t, wnew3, mkv_t, qm3, pool_mat, *, t_pos):
    b = q3.shape[0]
    n_cmp = kcvc.shape[1]
    n_win = win_t.shape[3]
    m = mkv_t.shape[3]
    per_b = lambda shape: pl.BlockSpec((1,) + shape, lambda bi: (bi,) + (0,) * len(shape))
    return pl.pallas_call(
        functools.partial(_sample_attn1_kernel, t_pos=t_pos, n_win=n_win),
        grid=(b,),
        in_specs=[per_b((1, Q_PAD)), per_b((n_cmp, 2 * LANES)), per_b((2, 2 * HEAD_DIM, n_win)),
                  per_b((1, 2 * LANES)), per_b((2, MEM_WIDTH, m)), per_b((1, QM_PAD)),
                  pl.BlockSpec(pool_mat.shape, lambda bi: (0, 0))],
        out_specs=[per_b((NSA_HEADS, LANES)), per_b((NSA_HEADS, LANES)), per_b((1, MEM_WIDTH)),
                   per_b((NSA_KV_HEADS, pool_mat.shape[1]))],
        out_shape=[jax.ShapeDtypeStruct((b, NSA_HEADS, LANES), F32), jax.ShapeDtypeStruct((b, NSA_HEADS, LANES), F32),
                   jax.ShapeDtypeStruct((b, 1, MEM_WIDTH), F32),
                   jax.ShapeDtypeStruct((b, NSA_KV_HEADS, pool_mat.shape[1]), F32)],
        compiler_params=_cparams("parallel"),
        name="sample_attn1",
    )(q3, kcvc, win_t, wnew3, mkv_t, qm3, pool_mat)


def _sample_topk_kernel(score_ref, idx_ref, *, n_top):
    score = score_ref[...]
    ids = lax.broadcasted_iota(I32, (1, score.shape[1]), 1).astype(F32)
    _, firsts = _topk_mask(score, ids, n_top, 1)
    lane = lax.broadcasted_iota(I32, (1, LANES), 1)
    idx = jnp.full((score.shape[0], LANES), -1, I32)
    for j, f in enumerate(firsts):
        idx = jnp.where(lane == j, f.astype(I32), idx)
    idx_ref[...] = idx


def _sample_topk_call(score2d, *, n_top):
    rows = score2d.shape[0]
    return pl.pallas_call(
        functools.partial(_sample_topk_kernel, n_top=n_top),
        out_shape=jax.ShapeDtypeStruct((rows, LANES), I32),
        compiler_params=pltpu.CompilerParams(vmem_limit_bytes=VMEM_LIMIT),
        name="sample_topk",
    )(score2d)


def _sample_sel_kernel(pt_ref, idx_ref, cache_ref, q_ref, knew_ref, gate_ref, oc_ref, ow_ref, o_ref,
                       buf, sem, *, t_pos, n_pages, n_top, page):
    b = pl.program_id(0)
    nb = pl.num_programs(0)
    blk_per_page = page // SEL_BLOCK
    n_past_blk = n_pages * blk_per_page
    idx_stride = NSA_KV_HEADS * LANES

    def blk_at(bb, g, j):
        return idx_ref[bb * idx_stride + g * LANES + j]

    def blk_copy(bb, slot, g, j, kv):
        blkc = jnp.clip(blk_at(bb, g, j), 0, n_past_blk - 1)
        pg = pt_ref[bb * n_pages + blkc // blk_per_page]
        return pltpu.make_async_copy(cache_ref.at[pg, 2 + kv], buf.at[slot, g, kv, j], sem.at[slot])

    def for_all(fn):
        for g in range(NSA_KV_HEADS):
            for j in range(n_top):
                for kv in range(2):
                    fn(g, j, kv)

    def issue(bb, slot):
        for_all(lambda g, j, kv: blk_copy(bb, slot, g, j, kv).start())

    @pl.when(b == 0)
    def _():
        issue(0, 0)

    @pl.when(b + 1 < nb)
    def _():
        issue(b + 1, (b + 1) % 2)

    slot = b % 2
    for_all(lambda g, j, kv: blk_copy(b, slot, g, j, kv).wait())

    q = q_ref[0].astype(F32)
    knew = knew_ref[0]
    k_new = knew[:, :LANES].astype(BF16).astype(F32)
    v_new = knew[:, LANES:].astype(BF16).astype(F32)
    n_keys = n_top * page
    key_lane = lax.broadcasted_iota(I32, (1, n_keys), 1)
    key_slot = key_lane // page
    key_row = key_lane % page
    cur_blk = t_pos // SEL_BLOCK
    o_s = []
    for g in range(NSA_KV_HEADS):
        qg = _pad_rows([q[:, (g * NSA_REP + r) * LANES:(g * NSA_REP + r + 1) * LANES] for r in range(NSA_REP)])
        blkvec = jnp.full((1, n_keys), -1, I32)
        has_cur = jnp.zeros((1, 1), jnp.bool_)
        for j in range(n_top):
            bj = blk_at(b, g, j)
            blkvec = jnp.where(key_slot == j, bj, blkvec)
            has_cur = has_cur | (bj == cur_blk)
        in_blk = key_row // SEL_BLOCK == blkvec % blk_per_page
        key_pos = (blkvec // blk_per_page) * page + key_row
        vis = (blkvec >= 0) & (blkvec < n_past_blk) & in_blk & (key_pos <= t_pos)
        kt = jnp.concatenate([buf[slot, g, 0, j] for j in range(n_top)], axis=1).astype(BF16)
        vt = jnp.concatenate([buf[slot, g, 1, j] for j in range(n_top)], axis=1).astype(BF16)
        s_past = jnp.where(vis, _dot(qg.astype(BF16), kt), NEG)
        s_new = jnp.where(has_cur, jnp.sum(qg * k_new, axis=-1, keepdims=True), NEG)
        m = jnp.maximum(jnp.max(s_past, axis=-1, keepdims=True), s_new)
        e_p = jnp.where(vis, jnp.exp(s_past - m), 0.0)
        e_n = jnp.where(has_cur, jnp.exp(s_new - m), 0.0)
        den = jnp.maximum(jnp.sum(e_p, axis=-1, keepdims=True) + e_n, 1e-30)
        num = _dot_nt(e_p.astype(BF16), vt) + e_n.astype(BF16).astype(F32) * v_new
        o_s.append(num / den)
    o_c = [jnp.concatenate([oc_ref[0, g * NSA_REP:(g + 1) * NSA_REP, :]] * 2, axis=0) for g in range(NSA_KV_HEADS)]
    o_w = [jnp.concatenate([ow_ref[0, g * NSA_REP:(g + 1) * NSA_REP, :]] * 2, axis=0) for g in range(NSA_KV_HEADS)]

    def store(j, chunk):
        o_ref[0, :, j * LANES:(j + 1) * LANES] = chunk

    _gate_and_pack(store, gate_ref[0], o_c, o_s, o_w, 1)


def _sample_sel_call(page_table, idx, cache_t, q3, knew3, gates3, oc, ow, *, t_pos, n_top):
    b, n_pages = page_table.shape
    page = cache_t.shape[3]
    per_b = lambda shape: pl.BlockSpec((1,) + shape, lambda bi, pt, ix: (bi, 0, 0))
    grid_spec = pltpu.PrefetchScalarGridSpec(
        num_scalar_prefetch=2,
        grid=(b,),
        in_specs=[pl.BlockSpec(memory_space=pl.ANY), per_b((1, Q_PAD)), per_b((1, 2 * LANES)), per_b((1, LANES)),
                  per_b((NSA_HEADS, LANES)), per_b((NSA_HEADS, LANES))],
        out_specs=per_b((1, NSA_WIDTH)),
        scratch_shapes=[pltpu.VMEM((2, NSA_KV_HEADS, 2, n_top, LANES, page), F32), pltpu.SemaphoreType.DMA((2,))],
    )
    return pl.pallas_call(
        functools.partial(_sample_sel_kernel, t_pos=t_pos, n_pages=n_pages, n_top=n_top, page=page),
        grid_spec=grid_spec,
        out_shape=jax.ShapeDtypeStruct((b, 1, NSA_WIDTH), F32),
        compiler_params=_cparams("arbitrary"),
        name="sample_sel",
    )(page_table.reshape(-1), idx.reshape(-1), cache_t, q3, knew3, gates3, oc, ow)


def _finish_kernel(yp_ref, on_ref, om_ref, x_ref, gm_ref, wo_ref, gf_ref, wr_ref, br_ref, tri_ref,
                   h_ref, hn_ref, route_ref, counts_ref, cnt_ref):
    gm = gm_ref[...]
    o1 = POOL_WIDTH
    o2 = o1 + NSA_WIDTH
    mixed = jnp.concatenate([_rms(yp_ref[...]) * gm[:, :o1], _rms(on_ref[...]) * gm[:, o1:o2],
                             _rms(om_ref[...]) * gm[:, o2:]], axis=-1)
    h = x_ref[...] + _dot(mixed.astype(BF16), wo_ref[...])
    h_ref[...] = h
    hn = _rms(h) * gf_ref[...]
    hn_ref[...] = hn
    logits = _dot(hn.astype(BF16), wr_ref[...]) + br_ref[...]
    lane = lax.broadcasted_iota(I32, (1, LANES), 1)
    lane_f = lane.astype(F32)
    is1 = lane < N_EXPERT_GROUPS
    m1 = jnp.max(jnp.where(is1, logits, -jnp.inf), axis=-1, keepdims=True)
    e1 = jnp.where(is1, jnp.exp(logits - m1), 0.0)
    p1 = e1 / jnp.sum(e1, axis=-1, keepdims=True)
    top1_p = jnp.max(p1, axis=-1, keepdims=True)
    grp = jnp.min(jnp.where((p1 == top1_p) & is1, lane_f, float(LANES)), axis=-1, keepdims=True)
    base = N_EXPERT_GROUPS + grp * EXPERTS_PER_GROUP
    in_g = (lane_f >= base) & (lane_f < base + EXPERTS_PER_GROUP)
    l2 = jnp.where(in_g, logits, -jnp.inf)
    v0 = jnp.max(l2, axis=-1, keepdims=True)
    i0 = jnp.min(jnp.where(l2 == v0, lane_f, float(LANES)), axis=-1, keepdims=True)
    l2b = jnp.where(lane_f == i0, -jnp.inf, l2)
    v1 = jnp.max(l2b, axis=-1, keepdims=True)
    i1 = jnp.min(jnp.where(l2b == v1, lane_f, float(LANES)), axis=-1, keepdims=True)
    ex = jnp.exp(v1 - v0)
    w0 = top1_p / (1.0 + ex)
    w1 = top1_p * ex / (1.0 + ex)
    ex0 = i0 - N_EXPERT_GROUPS
    ex1 = i1 - N_EXPERT_GROUPS
    is0 = lane_f == ex0
    is1e = lane_f == ex1
    oh0 = jnp.where(is0, 1.0, 0.0)
    oh1 = jnp.where(is1e, 1.0, 0.0)
    before0 = _dot(tri_ref[...], oh0.astype(BF16))
    before1 = _dot(tri_ref[...], oh1.astype(BF16))
    tot0 = jnp.sum(oh0, axis=0, keepdims=True)
    tot1 = jnp.sum(oh1, axis=0, keepdims=True)

    @pl.when(pl.program_id(0) == 0)
    def _():
        cnt_ref[...] = jnp.zeros(cnt_ref.shape, F32)

    seen = cnt_ref[...]
    rank0 = jnp.sum(jnp.where(is0, before0 + seen, 0.0), axis=-1, keepdims=True)
    rank1 = jnp.sum(jnp.where(is1e, before1 + seen + tot0, 0.0), axis=-1, keepdims=True)
    cnt_ref[...] = seen + tot0 + tot1
    counts_ref[...] = seen + tot0 + tot1
    route = jnp.where(lane == 0, ex0, jnp.where(lane == 1, ex1, jnp.where(lane == 2, w0, jnp.where(lane == 3, w1,
            jnp.where(lane == 4, rank0, jnp.where(lane == 5, rank1, 0.0))))))
    route_ref[...] = route


def _finish_call(yp, on, om, x2d, fw, *, tm):
    n, dm = x2d.shape
    full = lambda shape: pl.BlockSpec(shape, lambda i: (0,) * len(shape))
    rows = lambda w: pl.BlockSpec((tm, w), lambda i: (i, 0))
    tri = jnp.asarray(np.arange(tm)[None, :] < np.arange(tm)[:, None], BF16)
    return pl.pallas_call(
        _finish_kernel,
        grid=(n // tm,),
        in_specs=[rows(POOL_WIDTH), rows(NSA_WIDTH), rows(MEM_WIDTH), rows(dm), full((1, dm)), full((dm, dm)),
                  full((1, dm)), full((dm, LANES)), full((1, LANES)), full((tm, tm))],
        out_specs=[rows(dm), rows(dm), rows(LANES), full((1, LANES))],
        out_shape=[jax.ShapeDtypeStruct((n, dm), F32), jax.ShapeDtypeStruct((n, dm), F32),
                   jax.ShapeDtypeStruct((n, LANES), F32), jax.ShapeDtypeStruct((1, LANES), F32)],
        scratch_shapes=[pltpu.VMEM((1, LANES), F32)],
        compiler_params=_cparams("arbitrary"),
        name="finish",
    )(yp, on, om, x2d, fw["g_mix"], fw["w_out"], fw["g_ffn"], fw["w_r"], fw["b_r"], tri)


def _route_tables(route, counts, tm):
    n = route.shape[0]
    eid = route[:, 0:2].astype(I32)
    rank = route[:, 4:6].astype(I32)
    cnt = counts[0, :N_EXPERTS].astype(I32)
    tiles_per = jnp.maximum((cnt + tm - 1) // tm, 1)
    tile_end = jnp.cumsum(tiles_per)
    tile_start = tile_end - tiles_per
    experts = jnp.arange(N_EXPERTS, dtype=I32)
    start_of = jnp.sum(jnp.where(eid[:, :, None] == experts, tile_start, 0), axis=-1)
    pos = (start_of * tm + rank).reshape(-1)
    n_tiles = (2 * n) // tm + N_EXPERTS
    tj = jnp.arange(n_tiles, dtype=I32)
    tile_expert = jnp.minimum(jnp.sum((tj[:, None] >= tile_end[None, :]).astype(I32), axis=1), N_EXPERTS - 1)
    sel = tile_expert[:, None] == experts
    rows_left = jnp.sum(jnp.where(sel, cnt, 0), axis=-1) - (tj - jnp.sum(jnp.where(sel, tile_start, 0), axis=-1)) * tm
    tile_nvalid = jnp.where(tj < tile_end[-1], jnp.clip(rows_left, 0, tm), 0).astype(I32)
    return tile_expert, tile_nvalid, pos


def _row_wait_all(src_row, dst_row, sem, count):
    for _ in range(count):
        pltpu.make_async_copy(src_row, dst_row, sem).wait()


def _pack_bf16_pairs(x):
    w = x.shape[1] // 2
    lo = pltpu.bitcast(x[:, :w].astype(BF16).astype(F32), jnp.uint32)
    hi = pltpu.bitcast(x[:, w:].astype(BF16).astype(F32), jnp.uint32)
    return lax.shift_right_logical(lo, jnp.uint32(16)) | hi


def _unpack_bf16_pairs(p):
    lo = pltpu.bitcast(lax.shift_left(p, jnp.uint32(16)), F32)
    hi = pltpu.bitcast(p & jnp.uint32(0xFFFF0000), F32)
    return jnp.concatenate([lo, hi], axis=1).astype(BF16)


def _dispatch_kernel(pos_ref, nv_ref, hn_ref, xs_ref, stage, zbuf, sem, zsem, *, tm, tme, n_tiles):
    i = pl.program_id(0)
    nt = pl.num_programs(0)
    wait_slot = lambda s: _row_wait_all(stage.at[s, pl.ds(0, 1), :], xs_ref.at[pl.ds(0, 1), :], sem.at[s], 2 * tm)

    @pl.when(i == 0)
    def _():
        zbuf[...] = jnp.zeros(zbuf.shape, zbuf.dtype)

        def fill(j, c):
            @pl.when(nv_ref[j] < tme)
            def _():
                pltpu.make_async_copy(zbuf, xs_ref.at[pl.ds(pl.multiple_of(j * tme, tme), tme), :], zsem).start()
            return c

        def drain(j, c):
            @pl.when(nv_ref[j] < tme)
            def _():
                pltpu.make_async_copy(zbuf, xs_ref.at[pl.ds(0, tme), :], zsem).wait()
            return c

        lax.fori_loop(0, n_tiles, fill, 0)
        lax.fori_loop(0, n_tiles, drain, 0)

    for slot in range(2):
        @pl.when(i >= 1)
        def _():
            wait_slot(slot)

        stage[slot] = _pack_bf16_pairs(hn_ref[slot * tm:(slot + 1) * tm, :])
        base = (2 * i + slot) * (2 * tm)
        for r in range(tm):
            for k in range(2):
                dst = pos_ref[base + 2 * r + k]
                pltpu.make_async_copy(stage.at[slot, pl.ds(r, 1), :], xs_ref.at[pl.ds(dst, 1), :], sem.at[slot]).start()

    @pl.when(i == nt - 1)
    def _():
        wait_slot(0)
        wait_slot(1)


def _dispatch_call(pos, tile_nvalid, hn, *, tm, tme):
    n, dm = hn.shape
    n_tiles = tile_nvalid.shape[0]
    grid_spec = pltpu.PrefetchScalarGridSpec(
        num_scalar_prefetch=2,
        grid=(n // (2 * tm),),
        in_specs=[pl.BlockSpec((2 * tm, dm), lambda i, pos, nv: (i, 0))],
        out_specs=pl.BlockSpec(memory_space=pl.ANY),
        scratch_shapes=[pltpu.VMEM((2, tm, dm // 2), jnp.uint32), pltpu.VMEM((tme, dm // 2), jnp.uint32),
                        pltpu.SemaphoreType.DMA((2,)), pltpu.SemaphoreType.DMA(())],
    )
    return pl.pallas_call(
        functools.partial(_dispatch_kernel, tm=tm, tme=tme, n_tiles=n_tiles),
        grid_spec=grid_spec,
        out_shape=jax.ShapeDtypeStruct((n_tiles * tme, dm // 2), jnp.uint32),
        compiler_params=_cparams("arbitrary"),
        name="moe_dispatch",
    )(pos, tile_nvalid, hn)


def _expert_kernel(te_ref, nv_ref, x_ref, wg_ref, wu_ref, wd_ref, hns_ref, routes_ref, hs_ref,
                   y_ref, ysample_ref, wgb, wub, wdb, acc_s, *, tm):
    i = pl.program_id(0)
    prev = te_ref[jnp.maximum(i - 1, 0)]

    @pl.when(i == 0)
    def _():
        acc_s[...] = jnp.zeros(acc_s.shape, F32)

    @pl.when((i == 0) | (te_ref[i] != prev))
    def _():
        wgb[...] = wg_ref[0].astype(BF16)
        wub[...] = wu_ref[0].astype(BF16)
        wdb[...] = wd_ref[0].astype(BF16)
        route = routes_ref[...]
        ef = te_ref[i].astype(F32)
        comb = jnp.where(route[:, 0:1] == ef, route[:, 2:3], 0.0) + jnp.where(route[:, 1:2] == ef, route[:, 3:4], 0.0)
        xs = hns_ref[...].astype(BF16)
        hg = _dot(xs, wgb[...])
        hu = _dot(xs, wub[...])
        hmid = hg * jax.nn.sigmoid(hg) * hu * comb
        acc_s[...] += _dot(hmid.astype(BF16), wdb[...])

    @pl.when(i == pl.num_programs(0) - 1)
    def _():
        ysample_ref[...] = hs_ref[...] + acc_s[...]

    nv = nv_ref[i]

    @pl.when(nv == 0)
    def _():
        y_ref[...] = jnp.zeros(y_ref.shape, F32)

    @pl.when(nv > 0)
    def _():
        x = _unpack_bf16_pairs(x_ref[...])
        hg = _dot(x, wgb[...])
        hu = _dot(x, wub[...])
        hmid = hg * jax.nn.sigmoid(hg) * hu
        y_ref[...] = _dot(hmid.astype(BF16), wdb[...])


def _expert_call(tile_expert, tile_nvalid, xs, wg, wu, wd, hn_s, route_s, h_s, *, tm):
    n_tiles = tile_expert.shape[0]
    dm = wg.shape[1]
    ff = wg.shape[2]
    ns = hn_s.shape[0]
    full = lambda shape: pl.BlockSpec(shape, lambda i, te, nv: (0,) * len(shape))
    grid_spec = pltpu.PrefetchScalarGridSpec(
        num_scalar_prefetch=2,
        grid=(n_tiles,),
        in_specs=[pl.BlockSpec((tm, dm // 2), lambda i, te, nv: (i, 0)),
                  pl.BlockSpec((1, dm, ff), lambda i, te, nv: (te[i], 0, 0)),
                  pl.BlockSpec((1, dm, ff), lambda i, te, nv: (te[i], 0, 0)),
                  pl.BlockSpec((1, ff, dm), lambda i, te, nv: (te[i], 0, 0)),
                  full((ns, dm)), full((ns, LANES)), full((ns, dm))],
        out_specs=[pl.BlockSpec((tm, dm), lambda i, te, nv: (i, 0)), full((ns, dm))],
        scratch_shapes=[pltpu.VMEM((dm, ff), BF16), pltpu.VMEM((dm, ff), BF16), pltpu.VMEM((ff, dm), BF16),
                        pltpu.VMEM((ns, dm), F32)],
    )
    return pl.pallas_call(
        functools.partial(_expert_kernel, tm=tm),
        grid_spec=grid_spec,
        out_shape=[jax.ShapeDtypeStruct((xs.shape[0], dm), F32), jax.ShapeDtypeStruct((ns, dm), F32)],
        compiler_params=_cparams("arbitrary"),
        name="moe_experts",
    )(tile_expert, tile_nvalid, xs, wg, wu, wd, hn_s, route_s, h_s)


def _combine_kernel(pos_ref, ys_ref, h_ref, route_ref, o_ref, gbuf, sem, *, tm):
    i = pl.program_id(0)
    nt = pl.num_programs(0)

    def issue(half_tile, s):
        base = half_tile * (2 * tm)
        for r in range(tm):
            for k in range(2):
                src = pos_ref[base + 2 * r + k]
                pltpu.make_async_copy(ys_ref.at[pl.ds(src, 1), :], gbuf.at[s, k, pl.ds(r, 1), :], sem.at[s]).start()

    @pl.when(i == 0)
    def _():
        issue(0, 0)
        issue(1, 1)

    for s in range(2):
        _row_wait_all(ys_ref.at[pl.ds(0, 1), :], gbuf.at[s, 0, pl.ds(0, 1), :], sem.at[s], 2 * tm)
        rows = slice(s * tm, (s + 1) * tm)
        w0 = route_ref[rows, 2:3]
        w1 = route_ref[rows, 3:4]
        o_ref[rows, :] = h_ref[rows, :] + (w0 * gbuf[s, 0] + w1 * gbuf[s, 1])

        @pl.when(i + 1 < nt)
        def _():
            issue(2 * (i + 1) + s, s)


def _combine_call(pos, ys, h, route, *, tm):
    n, dm = h.shape
    grid_spec = pltpu.PrefetchScalarGridSpec(
        num_scalar_prefetch=1,
        grid=(n // (2 * tm),),
        in_specs=[pl.BlockSpec(memory_space=pl.ANY),
                  pl.BlockSpec((2 * tm, dm), lambda i, pos: (i, 0)),
                  pl.BlockSpec((2 * tm, LANES), lambda i, pos: (i, 0))],
        out_specs=pl.BlockSpec((2 * tm, dm), lambda i, pos: (i, 0)),
        scratch_shapes=[pltpu.VMEM((2, 2, tm, dm), F32), pltpu.SemaphoreType.DMA((2,))],
    )
    return pl.pallas_call(
        functools.partial(_combine_kernel, tm=tm),
        grid_spec=grid_spec,
        out_shape=jax.ShapeDtypeStruct((n, dm), F32),
        compiler_params=_cparams("arbitrary"),
        name="moe_combine",
    )(pos, ys, h, route)


def _prep_weights(l, g_attn, w_in, g_q, g_k, pe_cmp, w_cmp1, w_cmp2, w_pool, s_pool, g_mem, w_mem_kv, g_mq, g_mk,
                  g_mix, w_out, g_ffn, w_router1, b_router1, w_router2, b_router2):
    dm = w_in.shape[1]
    w = w_in[l]
    o1 = POOL_WIDTH
    o2 = o1 + NSA_WIDTH
    o3 = o2 + NSA_KV_WIDTH
    o4 = o3 + GATE_WIDTH
    wq = w[:, o1:o2].reshape(dm, NSA_HEADS, 1, HEAD_DIM)
    q_slot = jnp.asarray(np.eye(NSA_KV_HEADS, dtype=np.float32)[np.arange(NSA_HEADS) // NSA_REP])
    wq_pad = (wq * q_slot[None, :, :, None]).reshape(dm, Q_PAD)
    wqm = w[:, o4:].reshape(dm, MEM_HEADS, 1, HEAD_DIM)
    m_slot = jnp.asarray(np.eye(2, dtype=np.float32)[np.arange(MEM_HEADS) % 2])
    wqm_pad = (wqm * m_slot[None, :, :, None]).reshape(dm, QM_PAD)
    wg_pad = jnp.pad(w[:, o3:o4], ((0, 0), (0, LANES - GATE_WIDTH)))
    w_packed = jnp.concatenate([w[:, :o1], wq_pad, w[:, o2:o3], wqm_pad, wg_pad], axis=1).astype(BF16)
    half = ROPE_DIM // 2
    inv = jnp.power(ROPE_THETA, -jnp.arange(half, dtype=F32) * 2.0 / ROPE_DIM)
    d = np.arange(LANES) % HEAD_DIM
    inv_lane = jnp.where(jnp.asarray(d < ROPE_DIM), inv[jnp.asarray(d % half)], 0.0).reshape(1, LANES)
    seg = jnp.asarray((np.arange(LANES)[:, None] // HEAD_DIM == np.arange(LANES)[None, :] // HEAD_DIM), BF16)
    gk = jnp.concatenate([jnp.tile(g_k[l, br], 2) for br in range(N_BRANCH)]).reshape(1, N_BRANCH * LANES)
    proj = dict(g_attn=g_attn[l].reshape(1, dm), w_in=w_packed, gq=jnp.tile(g_q[l], Q_PAD // HEAD_DIM).reshape(1, Q_PAD),
                gk=gk, gmq=jnp.tile(g_mq[l], QM_PAD // HEAD_DIM).reshape(1, QM_PAD), inv=inv_lane, seg=seg)

    n_grp = len(POOL_WINDOWS)
    pg = POOL_WIDTH // n_grp
    w_bd = (jnp.asarray(np.eye(n_grp, dtype=np.float32))[:, None, :, None] * w_pool[l][:, :, None, :]
            ).reshape(POOL_WIDTH, POOL_WIDTH).astype(BF16)
    pool = dict(w=w_bd, s=s_pool[l].reshape(1, POOL_WIDTH))

    half_rows = CMP_STRIDE * HEAD_DIM
    w1 = jnp.concatenate([w_cmp1[l][:, :half_rows], w_cmp1[l][:, half_rows:]], axis=2).astype(BF16)
    cmp_w = dict(w=w1, pe=pe_cmp[l].reshape(2, 1, CMP_BLOCK * HEAD_DIM), w2=w_cmp2[l].astype(BF16))

    mem = dict(g=g_mem[l].reshape(1, dm), w=w_mem_kv[l].astype(BF16),
               gk=jnp.tile(g_mk[l], MEM_HEADS).reshape(1, MEM_WIDTH))
    w_r = jnp.concatenate([w_router1[l], w_router2[l].reshape(dm, N_EXPERTS)], axis=1)
    w_r = jnp.pad(w_r, ((0, 0), (0, LANES - w_r.shape[1]))).astype(BF16)
    b_r = jnp.concatenate([b_router1[l], b_router2[l].reshape(-1)])
    b_r = jnp.pad(b_r, (0, LANES - b_r.shape[0])).reshape(1, LANES)
    fin = dict(g_mix=g_mix[l].reshape(1, dm), w_out=w_out[l].astype(BF16), g_ffn=g_ffn[l].reshape(1, dm),
               w_r=w_r, b_r=b_r)
    return proj, pool, cmp_w, mem, fin


def _pick(n, prefs):
    for p in prefs:
        if n % p == 0:
            return p
    return n


def kernel(x_prompt, x_sample, cache_kv, cache_win, state_pool, cache_mem_kv, page_table, mem_prompt, g_attn, w_in, g_q, g_k, pe_cmp, w_cmp1, w_cmp2, w_pool, s_pool, g_mem, w_mem_kv, g_mq, g_mk, g_mix, w_out, g_ffn, w_router1, b_router1, w_router2, b_router2, w_gate, w_up, w_down):
    depth = w_in.shape[0]
    bp, t, dm = x_prompt.shape
    bs, ts, _ = x_sample.shape
    n_pages = page_table.shape[1]
    page = cache_kv.shape[2]
    past_len = n_pages * page
    n_win = cache_win.shape[2]
    ff = w_gate.shape[-1]
    tq, tk = 256, 512
    assert ts == 1 and n_win == WINDOW and page % SEL_BLOCK == 0 and t % tk == 0 and t >= WINDOW + tq
    n_sel = t // SEL_BLOCK
    assert min(SEL_TOPK, n_sel) >= 3 and n_sel <= LANES
    poolt = jnp.asarray(np.arange(t // CMP_STRIDE)[None, :] // CMP_PER_SEL == np.arange(n_sel)[:, None], BF16)
    key_blk = np.arange(t // tk)[:, None, None] * (tk // SEL_BLOCK) + np.arange(tk)[None, None, :] // SEL_BLOCK
    e_mat = jnp.asarray(np.arange(LANES)[None, :, None] == key_blk, BF16)
    n_cmp_s = past_len // CMP_STRIDE
    n_slot_s = -(-(past_len // SEL_BLOCK + 1) // LANES) * LANES
    pool_s = jnp.asarray(np.arange(n_cmp_s)[:, None] // CMP_PER_SEL == np.arange(n_slot_s)[None, :], BF16)

    hp, hs = x_prompt, x_sample
    outs = [[] for _ in range(7)]
    for l in range(depth):
        proj_w, pool_w, cmp_w, mem_w, fin_w = _prep_weights(
            l, g_attn, w_in, g_q, g_k, pe_cmp, w_cmp1, w_cmp2, w_pool, s_pool, g_mem, w_mem_kv, g_mq, g_mk,
            g_mix, w_out, g_ffn, w_router1, b_router1, w_router2, b_router2)
        wg = w_gate[l].reshape(N_EXPERTS, dm, ff)
        wu = w_up[l].reshape(N_EXPERTS, dm, ff)
        wd = w_down[l].reshape(N_EXPERTS, ff, dm)
        kw_cols = CMP_STRIDE * 2 * LANES

        n = bp * t
        xp2 = hp.reshape(n, dm)
        u, qb, rows4, rowsw, kvb, gates, qmb, rows4t, rowswt = _proj_call(
            xp2, proj_w, seq_len=t, pos0=0, tm=_pick(t, (512, 256, 128)), feature_major_rows4=True)
        u3 = u.reshape(bp, t, POOL_WIDTH)
        y_pool = _pool_call(u3, pool_w["w"], pool_w["s"], tp=_pick(t, (512, 256, 128)), pos0=0)
        n_slab = 2 * NSA_KV_HEADS
        chunk_w = CMP_STRIDE * HEAD_DIM
        kcvc = _compress_prompt_call(rows4.reshape(bp, t, 4 * LANES), jnp.zeros((bp, n_slab, chunk_w), F32), cmp_w)
        o_nsa = _nsa_prompt_call(qb.reshape(bp, t, Q_PAD), gates.reshape(bp, t, LANES), kcvc,
                                 kvb.reshape(bp, t, NSA_KV_WIDTH), poolt, e_mat, tq=tq, tk=tk)
        m_len = mem_prompt.shape[1]
        mkv, mkvb = _memkv_call(mem_prompt.reshape(bp * m_len, dm), mem_w["g"], mem_w["w"], mem_w["gk"],
                                proj_w["seg"], tm=_pick(bp * m_len, (256, 128)))
        o_mem = _memattn_call(qmb.reshape(bp, t, QM_PAD), mkvb.reshape(bp, m_len, 2 * MEM_WIDTH),
                              tq=_pick(t, (512, 256, 128)))
        tmf = _pick(n, (512, 256, 128))
        h_p, hn_p, route_p, counts_p = _finish_call(y_pool.reshape(n, POOL_WIDTH), o_nsa.reshape(n, NSA_WIDTH),
                                                    o_mem.reshape(n, MEM_WIDTH), xp2, fin_w, tm=tmf)
        tme = 256
        tmd = _pick(n, (256, 128))
        te, nv, pos = _route_tables(route_p, counts_p, tme)
        xs = _dispatch_call(pos, nv, hn_p, tm=tmd, tme=tme)

        xs2 = hs.reshape(bs, dm)
        u_s, qb_s, rows4_s, rowsw_s, _, gates_s, qmb_s = _proj_call(xs2, proj_w, seq_len=1, pos0=past_len, tm=bs)
        ext = jnp.concatenate([state_pool[l], u_s[:, None, :]], axis=1)
        y_pool_s = _pool_call(ext, pool_w["w"], pool_w["s"], tp=POOL_STATE + 1, pos0=past_len - POOL_STATE)[:, -1, :]
        cache_t = jnp.transpose(cache_kv[l], (0, 2, 3, 4, 1)).reshape(
            cache_kv.shape[1], N_KV_SLOTS, NSA_KV_HEADS * HEAD_DIM, page)
        tail_s = jnp.pad(rows4_s[:, :2 * LANES].reshape(bs, n_slab, HEAD_DIM), ((0, 0), (0, 0), (0, chunk_w - HEAD_DIM)))
        kcvc_s = _compress_sample_call(page_table, cache_t, tail_s, cmp_w)
        win_s = jnp.concatenate([cache_win[l].reshape(bs, n_win, 2 * LANES)[:, 1:], rowsw_s[:, None, :]], axis=1)
        win_t = jnp.transpose(cache_win[l], (0, 2, 3, 4, 1)).reshape(bs, 2, NSA_KV_HEADS * HEAD_DIM, n_win)
        mkv_t = jnp.transpose(cache_mem_kv[l], (0, 2, 3, 4, 1)).reshape(bs, 2, MEM_WIDTH, cache_mem_kv.shape[2])
        oc, ow, om_s, score_s = _sample_attn1_call(qb_s.reshape(bs, 1, Q_PAD), kcvc_s, win_t,
                                                   rowsw_s.reshape(bs, 1, 2 * LANES), mkv_t,
                                                   qmb_s.reshape(bs, 1, QM_PAD), pool_s, t_pos=past_len)
        n_top = min(SEL_TOPK, past_len // SEL_BLOCK + 1)
        idx = _sample_topk_call(score_s.reshape(bs * NSA_KV_HEADS, n_slot_s), n_top=n_top)
        o_nsa_s = _sample_sel_call(page_table, idx, cache_t, qb_s.reshape(bs, 1, Q_PAD),
                                   rows4_s[:, 2 * LANES:].reshape(bs, 1, 2 * LANES), gates_s.reshape(bs, 1, LANES),
                                   oc, ow, t_pos=past_len, n_top=n_top)
        h_s, hn_s, route_s, _ = _finish_call(y_pool_s, o_nsa_s.reshape(bs, NSA_WIDTH), om_s.reshape(bs, MEM_WIDTH),
                                          xs2, fin_w, tm=bs)

        ys, y_s = _expert_call(te, nv, xs, wg, wu, wd, hn_s, route_s, h_s, tm=tme)
        y_p = _combine_call(pos, ys, h_p, route_p, tm=tmd).reshape(bp, t, dm)
        y_s = y_s.reshape(bs, 1, dm)

        keep = min(WINDOW, t)
        outs[0].append(jnp.transpose(rows4t.reshape(bp, N_KV_SLOTS, NSA_KV_HEADS, HEAD_DIM, t), (0, 4, 1, 2, 3)))
        outs[1].append(rows4_s.reshape(bs, 1, N_KV_SLOTS, NSA_KV_HEADS, HEAD_DIM))
        outs[2].append(jnp.transpose(rowswt[:, :, t - keep:].reshape(bp, 2, NSA_KV_HEADS, HEAD_DIM, keep), (0, 4, 1, 2, 3)))
        outs[3].append(win_s.reshape(bs, n_win, 2, NSA_KV_HEADS, HEAD_DIM))
        outs[4].append(u3[:, t - POOL_STATE:])
        outs[5].append(ext[:, 1:])
        outs[6].append(mkv.reshape(bp, m_len, 2, MEM_HEADS, HEAD_DIM))
        hp, hs = y_p, y_s
    return (hp, hs) + tuple(jnp.stack(o) for o in outs)
```

```python
import functools

import numpy as np
import jax
import jax.numpy as jnp
from jax import lax
from jax.experimental import pallas as pl
from jax.experimental.pallas import tpu as pltpu

F32 = jnp.float32
BF16 = jnp.bfloat16
I32 = jnp.int32

HEAD_DIM = 64
POOL_WINDOWS = (2, 4, 8, 16)
POOL_STATE = max(POOL_WINDOWS) - 1
NSA_HEADS = 8
NSA_KV_HEADS = 2
NSA_REP = NSA_HEADS // NSA_KV_HEADS
N_BRANCH = 3
CMP_BLOCK = 32
CMP_STRIDE = 16
CMP_HIDDEN = 2 * HEAD_DIM
SEL_BLOCK = 64
SEL_TOPK = 16
CMP_PER_SEL = SEL_BLOCK // CMP_STRIDE
WINDOW = 512
FORCE_BONUS = 1000.0
MEM_HEADS = 4
ROPE_DIM = HEAD_DIM // 4
ROPE_THETA = 500000.0
N_EXPERT_GROUPS = 4
EXPERTS_PER_GROUP = 8
N_EXPERTS = N_EXPERT_GROUPS * EXPERTS_PER_GROUP
EPS = 1e-6
N_KV_SLOTS = 4

LANES = 128
POOL_WIDTH = 256
NSA_WIDTH = NSA_HEADS * HEAD_DIM
NSA_KV_WIDTH = N_BRANCH * 2 * NSA_KV_HEADS * HEAD_DIM
GATE_WIDTH = NSA_HEADS * N_BRANCH
MEM_WIDTH = MEM_HEADS * HEAD_DIM
Q_PAD = NSA_HEADS * LANES
QM_PAD = MEM_HEADS * LANES
C_U = 0
C_Q = C_U + POOL_WIDTH
C_KV = C_Q + NSA_WIDTH
C_QM = C_KV + NSA_KV_WIDTH
C_G = C_QM + MEM_WIDTH
C_END = C_G + LANES

NEG = -1e30
VMEM_LIMIT = 48 * 1024 * 1024

_NT = (((1,), (1,)), ((), ()))


def _cparams(*sem):
    return pltpu.CompilerParams(dimension_semantics=tuple(sem), vmem_limit_bytes=VMEM_LIMIT)


def _dot(a, b):
    return jnp.dot(a, b, preferred_element_type=F32)


def _dot_nt(a, b):
    return lax.dot_general(a, b, _NT, preferred_element_type=F32)


def _rms(x):
    return x * lax.rsqrt(jnp.mean(x * x, axis=-1, keepdims=True) + EPS)


def _masked_softmax(s, mask):
    sm = jnp.where(mask, s, NEG)
    m = jnp.max(sm, axis=-1, keepdims=True)
    e = jnp.where(mask, jnp.exp(sm - m), 0.0)
    return e / jnp.maximum(jnp.sum(e, axis=-1, keepdims=True), 1e-30)


def _split3(x):
    hi = x.astype(BF16)
    r1 = x - hi.astype(F32)
    mid = r1.astype(BF16)
    lo = (r1 - mid.astype(F32)).astype(BF16)
    return hi, mid, lo


def _proj_kernel(x_ref, ga_ref, w_ref, gq_ref, gk_ref, gmq_ref, inv_ref, seg_ref,
                 u_ref, q_ref, rows4_ref, rowsw_ref, kvb_ref, gates_ref, qm_ref, *rest,
                 tm, seq_len, pos0, consecutive):
    *maybe_t_refs, cos_in_tile, sin_in_tile = rest
    i = pl.program_id(0)
    a = _rms(x_ref[...]) * ga_ref[...]
    z = _dot(a.astype(BF16), w_ref[...])
    u_ref[...] = z[:, C_U:C_U + POOL_WIDTH]

    inv = inv_ref[...]

    @pl.when(i == 0)
    def _():
        j = lax.broadcasted_iota(I32, (tm, 1), 0).astype(F32) if consecutive else jnp.zeros((tm, 1), F32)
        cos_in_tile[...] = jnp.cos(j * inv)
        sin_in_tile[...] = jnp.sin(j * inv)

    first = (pos0 + (i * tm) % seq_len).astype(F32) * jnp.broadcast_to(inv, (8, LANES))
    cos_f, sin_f = jnp.cos(first)[0:1], jnp.sin(first)[0:1]
    cos = cos_f * cos_in_tile[...] - sin_f * sin_in_tile[...]
    sin = sin_f * cos_in_tile[...] + cos_f * sin_in_tile[...]
    d = lax.broadcasted_iota(I32, (1, LANES), 1) % HEAD_DIM
    half = ROPE_DIM // 2
    s_next = jnp.where(d < half, -sin, 0.0)
    s_prev = jnp.where((d >= half) & (d < ROPE_DIM), sin, 0.0)
    seg = seg_ref[...]

    def head_norm(xc, g):
        ssq = _dot((xc * xc).astype(BF16), seg)
        return xc * lax.rsqrt(ssq * (1.0 / HEAD_DIM) + EPS) * g

    def rope(xc):
        return xc * cos + pltpu.roll(xc, LANES - half, 1) * s_next + pltpu.roll(xc, half, 1) * s_prev

    lane_half = lax.broadcasted_iota(I32, (1, LANES), 1) // HEAD_DIM

    def single_head(col0, head, slot_half):
        tile = z[:, col0 + (head // 2) * LANES:col0 + (head // 2 + 1) * LANES]
        if head % 2 != slot_half:
            tile = pltpu.roll(tile, HEAD_DIM, 1)
        return jnp.where(lane_half == slot_half, tile, 0.0)

    scale = HEAD_DIM ** -0.5
    for c in range(NSA_HEADS):
        sl = slice(c * LANES, (c + 1) * LANES)
        qc = rope(head_norm(single_head(C_Q, c, c // NSA_REP), gq_ref[:, sl]))
        q_ref[:, sl] = (qc * scale).astype(BF16)
    for br in range(N_BRANCH):
        k0 = C_KV + br * 2 * LANES
        kn = rope(head_norm(z[:, k0:k0 + LANES], gk_ref[:, br * LANES:(br + 1) * LANES]))
        vv = z[:, k0 + LANES:k0 + 2 * LANES]
        kvb_ref[:, br * 2 * LANES:br * 2 * LANES + LANES] = kn.astype(BF16)
        kvb_ref[:, br * 2 * LANES + LANES:(br + 1) * 2 * LANES] = vv.astype(BF16)
        if br < 2:
            rows4_ref[:, br * 2 * LANES:br * 2 * LANES + LANES] = kn
            rows4_ref[:, br * 2 * LANES + LANES:(br + 1) * 2 * LANES] = vv
            if maybe_t_refs:
                maybe_t_refs[0][0, br * 2 * LANES:br * 2 * LANES + LANES, :] = kn.T
                maybe_t_refs[0][0, br * 2 * LANES + LANES:(br + 1) * 2 * LANES, :] = vv.T
        else:
            rowsw_ref[:, :LANES] = kn
            rowsw_ref[:, LANES:] = vv
            if maybe_t_refs:
                maybe_t_refs[1][0, :LANES, :] = kn.T
                maybe_t_refs[1][0, LANES:, :] = vv.T
    for c in range(MEM_HEADS):
        sl = slice(c * LANES, (c + 1) * LANES)
        qmc = head_norm(single_head(C_QM, c, c % 2), gmq_ref[:, sl])
        qm_ref[:, sl] = (qmc * scale).astype(BF16)
    gates_ref[...] = jax.nn.sigmoid(z[:, C_G:C_END])


def _proj_call(x2d, pw, *, seq_len, pos0, tm, feature_major_rows4=False):
    n, dm = x2d.shape
    full = lambda shape: pl.BlockSpec(shape, lambda i: (0,) * len(shape))
    rows = lambda w: pl.BlockSpec((tm, w), lambda i: (i, 0))
    outs = [(POOL_WIDTH, F32), (Q_PAD, BF16), (4 * LANES, F32), (2 * LANES, F32), (NSA_KV_WIDTH, BF16),
            (LANES, F32), (QM_PAD, BF16)]
    out_specs = [rows(w) for w, _ in outs]
    out_shape = [jax.ShapeDtypeStruct((n, w), dt) for w, dt in outs]
    if feature_major_rows4:
        per_seq = seq_len // tm
        for width in (4 * LANES, 2 * LANES):
            out_specs.append(pl.BlockSpec((1, width, tm), lambda i: (i // per_seq, 0, i % per_seq)))
            out_shape.append(jax.ShapeDtypeStruct((n // seq_len, width, seq_len), F32))
    consecutive = seq_len % tm == 0
    assert consecutive or seq_len == 1
    return pl.pallas_call(
        functools.partial(_proj_kernel, tm=tm, seq_len=seq_len, pos0=pos0, consecutive=consecutive),
        grid=(n // tm,),
        in_specs=[rows(dm), full((1, dm)), full((dm, C_END)), full((1, Q_PAD)), full((1, N_BRANCH * LANES)),
                  full((1, QM_PAD)), full((1, LANES)), full((LANES, LANES))],
        out_specs=out_specs,
        out_shape=out_shape,
        scratch_shapes=[pltpu.VMEM((tm, LANES), F32), pltpu.VMEM((tm, LANES), F32)],
        compiler_params=_cparams("arbitrary"),
        name="proj",
    )(x2d, pw["g_attn"], pw["w_in"], pw["gq"], pw["gk"], pw["gmq"], pw["inv"], pw["seg"])


def _pool_kernel(u_ref, halo_ref, w_ref, s_ref, y_ref, *, tp, pos0):
    i = pl.program_id(1)
    u = u_ref[0]
    halo = halo_ref[0] * (i > 0).astype(F32)
    n_h = POOL_STATE + 1
    ext = jnp.concatenate([halo, u], axis=0)
    sums = {1: ext}
    w = 1
    while w < max(POOL_WINDOWS):
        sums[2 * w] = sums[w] + pltpu.roll(sums[w], w, 0)
        w *= 2
    pos = pos0 + i * tp + lax.broadcasted_iota(I32, (tp, 1), 0)
    lane_grp = lax.broadcasted_iota(I32, (1, POOL_WIDTH), 1) // (POOL_WIDTH // len(POOL_WINDOWS))
    mean = jnp.zeros((tp, POOL_WIDTH), F32)
    for gi, wdw in enumerate(POOL_WINDOWS):
        cnt = jnp.minimum(pos + 1, wdw).astype(F32)
        mean = jnp.where(lane_grp == gi, sums[wdw][n_h:] / cnt, mean)
    r = mean - u
    y_ref[0] = _dot(r.astype(BF16), w_ref[...]) * s_ref[...]


def _pool_call(u3, w_bd, s_pool, *, tp, pos0):
    b, t, c = u3.shape
    n_h = POOL_STATE + 1
    return pl.pallas_call(
        functools.partial(_pool_kernel, tp=tp, pos0=pos0),
        grid=(b, t // tp),
        in_specs=[pl.BlockSpec((1, tp, c), lambda bi, i: (bi, i, 0)),
                  pl.BlockSpec((1, n_h, c), lambda bi, i: (bi, jnp.maximum(i * (tp // n_h) - 1, 0), 0)),
                  pl.BlockSpec((c, c), lambda bi, i: (0, 0)),
                  pl.BlockSpec((1, c), lambda bi, i: (0, 0))],
        out_specs=pl.BlockSpec((1, tp, c), lambda bi, i: (bi, i, 0)),
        out_shape=jax.ShapeDtypeStruct((b, t, c), F32),
        compiler_params=_cparams("parallel", "parallel"),
        name="pool",
    )(u3, u3, w_bd, s_pool)


def _split_chunks(x, n_chunks):
    return jnp.transpose(x.reshape(n_chunks, CMP_STRIDE, LANES), (1, 0, 2))


def _compress_core(load_slab, tail_ref, pe_ref, w_ref, w2_ref, n):
    rowid = lax.broadcasted_iota(I32, (n, 1), 0)
    outs = []
    for c in range(2):
        w = w_ref[c]
        pe8 = jnp.broadcast_to(pe_ref[c], (8, 2 * CMP_STRIDE * HEAD_DIM)).astype(BF16)
        pe_first = _dot(pe8[:, :CMP_STRIDE * HEAD_DIM], w)[0:1, :CMP_HIDDEN]
        pe_second = _dot(pe8[:, CMP_STRIDE * HEAD_DIM:], w)[0:1, CMP_HIDDEN:]
        pe_const = pe_first + pe_second
        lane = lax.broadcasted_iota(I32, (1, LANES), 1)
        per_group = [[], []]
        for q in range(CMP_STRIDE // 2):
            a, b = load_slab(c, 2 * q), load_slab(c, 2 * q + 1)
            per_group[0].append(jnp.where(lane < HEAD_DIM, a, pltpu.roll(b, HEAD_DIM, 1)))
            per_group[1].append(jnp.where(lane < HEAD_DIM, pltpu.roll(a, HEAD_DIM, 1), b))
        x2 = jnp.concatenate([jnp.concatenate(pg, axis=1) for pg in per_group], axis=0).astype(BF16)
        z = _dot(x2, w)
        hs = []
        for g in range(NSA_KV_HEADS):
            tail8 = jnp.broadcast_to(tail_ref[0, 2 * c + g:2 * c + g + 1, :], (8, CMP_STRIDE * HEAD_DIM)).astype(BF16)
            second_tail = _dot(tail8, w)[0:1, CMP_HIDDEN:]
            first = z[g * n:(g + 1) * n, :CMP_HIDDEN]
            second = z[g * n:(g + 1) * n, CMP_HIDDEN:]
            shifted = pltpu.roll(second, n - 1, 0)
            h = first + jnp.where(rowid == n - 1, second_tail, shifted) + pe_const
            hs.append(jax.nn.gelu(h, approximate=True))
        o = _dot(jnp.concatenate(hs, axis=0).astype(BF16), w2_ref[c])
        outs += [o[:n], o[n:]]
    return jnp.concatenate(outs, axis=1)


def _compress_prompt_kernel(k_ref, v_ref, tail_ref, pe_ref, w_ref, w2_ref, out_ref, *, n):
    slabs = [_split_chunks(ref[0], n) for ref in (k_ref, v_ref)]
    out_ref[0] = _compress_core(lambda c, r: slabs[c][r], tail_ref, pe_ref, w_ref, w2_ref, n).astype(BF16)


def _compress_prompt_call(rows4_3d, tail, cw):
    b, t, _ = rows4_3d.shape
    n = t // CMP_STRIDE
    full = lambda a: pl.BlockSpec(a.shape, lambda bi: (0,) * a.ndim)
    return pl.pallas_call(
        functools.partial(_compress_prompt_kernel, n=n),
        grid=(b,),
        in_specs=[pl.BlockSpec((1, t, LANES), lambda bi: (bi, 0, 0)),
                  pl.BlockSpec((1, t, LANES), lambda bi: (bi, 0, 1)),
                  pl.BlockSpec((1,) + tail.shape[1:], lambda bi: (bi, 0, 0)),
                  full(cw["pe"]), full(cw["w"]), full(cw["w2"])],
        out_specs=pl.BlockSpec((1, n, 2 * LANES), lambda bi: (bi, 0, 0)),
        out_shape=jax.ShapeDtypeStruct((b, n, 2 * LANES), BF16),
        compiler_params=_cparams("parallel"),
        name="compress_prompt",
    )(rows4_3d, rows4_3d, tail, cw["pe"], cw["w"], cw["w2"])


def _compress_sample_kernel(pt_ref, cache_ref, tail_ref, pe_ref, w_ref, w2_ref, out_ref,
                            buf, slabs, sem, *, n, n_pages, page):
    b = pl.program_id(0)
    nb = pl.num_programs(0)

    def page_copy(bb, slot, p, c):
        return pltpu.make_async_copy(cache_ref.at[pt_ref[bb * n_pages + p], c], buf.at[slot, c, p], sem.at[slot])

    def for_all(fn):
        for p in range(n_pages):
            for c in range(2):
                fn(p, c)

    def issue(bb, slot):
        for_all(lambda p, c: page_copy(bb, slot, p, c).start())

    @pl.when(b == 0)
    def _():
        issue(0, 0)

    @pl.when(b + 1 < nb)
    def _():
        issue(b + 1, (b + 1) % 2)

    slot = b % 2
    for_all(lambda p, c: page_copy(b, slot, p, c).wait())

    unroll = next(u for u in (8, 4, 2, 1) if n_pages % u == 0)

    cpp = page // CMP_STRIDE

    def to_slabs(pp, carry):
        for k in range(unroll):
            p = pp * unroll + k
            for c in range(2):
                by_r = _split_chunks(buf[slot, c, p].T, cpp)
                for r in range(CMP_STRIDE):
                    slabs[c, r, pl.ds(pl.multiple_of(p * cpp, cpp), cpp), :] = by_r[r]
        return carry

    lax.fori_loop(0, n_pages // unroll, to_slabs, 0)
    out_ref[0] = _compress_core(lambda c, r: slabs[c, r], tail_ref, pe_ref, w_ref, w2_ref, n).astype(BF16)


def _compress_sample_call(page_table, cache_t, tail, cw):
    b, n_pages = page_table.shape
    page = cache_t.shape[3]
    n = n_pages * page // CMP_STRIDE
    full = lambda a: pl.BlockSpec(a.shape, lambda bi, pt: (0,) * a.ndim)
    grid_spec = pltpu.PrefetchScalarGridSpec(
        num_scalar_prefetch=1,
        grid=(b,),
        in_specs=[pl.BlockSpec(memory_space=pl.ANY),
                  pl.BlockSpec((1,) + tail.shape[1:], lambda bi, pt: (bi, 0, 0)),
                  full(cw["pe"]), full(cw["w"]), full(cw["w2"])],
        out_specs=pl.BlockSpec((1, n, 2 * LANES), lambda bi, pt: (bi, 0, 0)),
        scratch_shapes=[pltpu.VMEM((2, 2, n_pages, LANES, page), F32), pltpu.VMEM((2, CMP_STRIDE, n, LANES), F32),
                        pltpu.SemaphoreType.DMA((2,))],
    )
    return pl.pallas_call(
        functools.partial(_compress_sample_kernel, n=n, n_pages=n_pages, page=page),
        grid_spec=grid_spec,
        out_shape=jax.ShapeDtypeStruct((b, n, 2 * LANES), BF16),
        compiler_params=_cparams("arbitrary"),
        name="compress_sample",
    )(page_table.reshape(-1), cache_t, tail, cw["pe"], cw["w"], cw["w2"])


def _topk_mask(score, ids, n_top, axis):
    sel = jnp.zeros(score.shape, jnp.bool_)
    work = score
    firsts = []
    big = float(score.shape[axis])
    for _ in range(n_top):
        m = jnp.max(work, axis=axis, keepdims=True)
        first = jnp.min(jnp.where(work == m, ids, big), axis=axis, keepdims=True)
        pick = (ids == first) & (m > -jnp.inf)
        sel = sel | pick
        work = jnp.where(pick, -jnp.inf, work)
        firsts.append(jnp.where(m > -jnp.inf, first, -1.0))
    return sel, firsts


def _gate_and_pack(o_ref_store, gates, o_c, o_s, o_w, rows):
    lane = lax.broadcasted_iota(I32, (1, LANES), 1)
    heads = []
    for h in range(NSA_HEADS):
        g, r = divmod(h, NSA_REP)
        rs = slice(r * rows, (r + 1) * rows)
        gc, gs, gw = (gates[:, N_BRANCH * h + k:N_BRANCH * h + k + 1] for k in range(N_BRANCH))
        heads.append(gc * o_c[g][rs] + gs * o_s[g][rs] + gw * o_w[g][rs])
    for j in range(NSA_HEADS // 2):
        a, b = heads[2 * j], heads[2 * j + 1]
        if (2 * j) // NSA_REP == 0:
            chunk = jnp.where(lane < HEAD_DIM, a, pltpu.roll(b, HEAD_DIM, 1))
        else:
            chunk = jnp.where(lane < HEAD_DIM, pltpu.roll(a, HEAD_DIM, 1), b)
        o_ref_store(j, chunk)


def _nsa_prompt_kernel(q_ref, gate_ref, kc_ref, kv_ref, poolt_ref, e_ref, o_ref, acc_ref, *, tq, tk, t_len):
    i = pl.program_id(1)
    s0 = i * tq
    rows4 = NSA_REP * tq
    tpos = s0 + lax.broadcasted_iota(I32, (tq, 1), 0)
    rep = lambda x: jnp.concatenate([x] * NSA_REP, axis=0)
    n_cmp = kc_ref.shape[1]
    kc128 = kc_ref[0, :, :LANES]
    vc128 = kc_ref[0, :, LANES:]
    kc_end = lax.broadcasted_iota(I32, (1, n_cmp), 1) * CMP_STRIDE + (CMP_BLOCK - 1)
    bias_c = rep(jnp.where(kc_end <= tpos, 0.0, NEG))
    any_c = rep((tpos >= CMP_BLOCK - 1).astype(F32))
    n_sel = t_len // SEL_BLOCK
    blk = lax.broadcasted_iota(I32, (n_sel, 1), 0)
    blk_f = blk.astype(F32)
    tq_lane = s0 + lax.broadcasted_iota(I32, (1, tq), 1)
    cur = tq_lane // SEL_BLOCK
    valid = blk * SEL_BLOCK <= tq_lane
    forced = (blk == 0) | (blk == cur) | (blk == cur - 1)
    sub = min(tq, LANES)
    w_len = WINDOW + sub
    w_starts, w_biases = [], []
    for j in range(tq // sub):
        w_start = pl.multiple_of(jnp.maximum(s0 + j * sub - WINDOW, 0), LANES)
        dist = tpos[j * sub:(j + 1) * sub] - (w_start + lax.broadcasted_iota(I32, (1, w_len), 1))
        w_starts.append(w_start)
        w_biases.append(rep(jnp.where((dist >= 0) & (dist < WINDOW), 0.0, NEG)))
    c_diag = s0 // tk
    kpos_d = c_diag * tk + lax.broadcasted_iota(I32, (1, tk), 1)
    bias_causal = jnp.where(kpos_d <= tpos, 0.0, NEG)

    o_c, o_s, o_w, qgs, selbs = [], [], [], [], []
    for g in range(NSA_KV_HEADS):
        qg = jnp.concatenate([q_ref[0, :, (g * NSA_REP + r) * LANES:(g * NSA_REP + r + 1) * LANES]
                              for r in range(NSA_REP)], axis=0)
        qgs.append(qg)
        s_c = _dot_nt(qg, kc128) + bias_c
        e_c = jnp.exp(s_c - jnp.max(s_c, axis=-1, keepdims=True))
        p_c = e_c * (any_c / jnp.sum(e_c, axis=-1, keepdims=True))
        o_c.append(_dot(p_c.astype(BF16), vc128))
        imp = p_c[0:tq]
        for r in range(1, NSA_REP):
            imp = imp + p_c[r * tq:(r + 1) * tq]
        imp_t = sum(_dot_nt(poolt_ref[...], piece) for piece in _split3(imp))
        score = jnp.where(valid, imp_t + jnp.where(forced, FORCE_BONUS, 0.0), -jnp.inf)
        sel_t, _ = _topk_mask(score, blk_f, min(SEL_TOPK, n_sel), 0)
        selb_t = jnp.concatenate([jnp.where(sel_t, 0.0, NEG), jnp.full(((-n_sel) % LANES, tq), NEG, F32)], axis=0)
        selbs.append(selb_t.T.astype(BF16))

        slabs = [[None] * (tq // sub) for _ in range(NSA_REP)]
        for j, (w_start, bias_w) in enumerate(zip(w_starts, w_biases)):
            q_sub = jnp.concatenate([qg[r * tq + j * sub:r * tq + (j + 1) * sub] for r in range(NSA_REP)], axis=0)
            kw = kv_ref[0, pl.ds(w_start, w_len), 4 * LANES:5 * LANES]
            vw = kv_ref[0, pl.ds(w_start, w_len), 5 * LANES:6 * LANES]
            s_w = _dot_nt(q_sub, kw) + bias_w
            e_w = jnp.exp(s_w - jnp.max(s_w, axis=-1, keepdims=True))
            o_sub = _dot(e_w.astype(BF16), vw) * (1.0 / jnp.sum(e_w, axis=-1, keepdims=True))
            for r in range(NSA_REP):
                slabs[r][j] = o_sub[r * sub:(r + 1) * sub]
        o_w.append(jnp.concatenate([s for per_head in slabs for s in per_head], axis=0))

    acc_ref[...] = jnp.zeros(acc_ref.shape, F32)

    q_all = jnp.concatenate(qgs, axis=0)
    selb_all = jnp.concatenate(selbs, axis=0)
    rows8 = NSA_KV_HEADS * rows4

    def chunk_step(c, carry, extra_bias):
        m, l = carry
        k0 = pl.multiple_of(c * tk, tk)
        ks = kv_ref[0, pl.ds(k0, tk), 2 * LANES:3 * LANES]
        vs = kv_ref[0, pl.ds(k0, tk), 3 * LANES:4 * LANES]
        bias = _dot(selb_all, e_ref[c])
        if extra_bias is not None:
            bias = bias + jnp.concatenate([extra_bias] * NSA_KV_HEADS, axis=0)
        bias_rows = jnp.concatenate([rep(bias[g * tq:(g + 1) * tq]) for g in range(NSA_KV_HEADS)], axis=0)
        s = _dot_nt(q_all, ks) + bias_rows
        m_new = jnp.maximum(m, jnp.max(s, axis=-1, keepdims=True))
        alpha = jnp.exp(m - m_new)
        p = jnp.exp(s - m_new)
        l_new = alpha * l + jnp.sum(p, axis=-1, keepdims=True)
        acc_ref[...] = alpha * acc_ref[...] + _dot(p.astype(BF16), vs)
        return m_new, l_new

    init = (jnp.full((rows8, 1), NEG, F32), jnp.zeros((rows8, 1), F32))
    carry = lax.fori_loop(0, c_diag, lambda c, cr: chunk_step(c, cr, None), init)
    _, l_fin = chunk_step(c_diag, carry, bias_causal)
    o_all = acc_ref[...] * (1.0 / jnp.maximum(l_fin, 1e-30))
    for g in range(NSA_KV_HEADS):
        o_s.append(o_all[g * rows4:(g + 1) * rows4])

    def store(j, chunk):
        o_ref[0, :, j * LANES:(j + 1) * LANES] = chunk

    _gate_and_pack(store, gate_ref[0], o_c, o_s, o_w, tq)


def _nsa_prompt_call(q3, gates3, kcvc, kvb3, poolt, e_mat, *, tq, tk):
    b, t, _ = q3.shape
    n_cmp = kcvc.shape[1]
    return pl.pallas_call(
        functools.partial(_nsa_prompt_kernel, tq=tq, tk=tk, t_len=t),
        grid=(b, t // tq),
        in_specs=[pl.BlockSpec((1, tq, Q_PAD), lambda bi, i: (bi, i, 0)),
                  pl.BlockSpec((1, tq, LANES), lambda bi, i: (bi, i, 0)),
                  pl.BlockSpec((1, n_cmp, 2 * LANES), lambda bi, i: (bi, 0, 0)),
                  pl.BlockSpec((1, t, NSA_KV_WIDTH), lambda bi, i: (bi, 0, 0)),
                  pl.BlockSpec(poolt.shape, lambda bi, i: (0, 0)),
                  pl.BlockSpec(e_mat.shape, lambda bi, i: (0, 0, 0))],
        out_specs=pl.BlockSpec((1, tq, NSA_WIDTH), lambda bi, i: (bi, i, 0)),
        out_shape=jax.ShapeDtypeStruct((b, t, NSA_WIDTH), F32),
        scratch_shapes=[pltpu.VMEM((NSA_KV_HEADS * NSA_REP * tq, LANES), F32)],
        compiler_params=_cparams("parallel", "arbitrary"),
        name="nsa_prompt",
    )(q3, gates3, kcvc, kvb3, poolt, e_mat)


def _memkv_kernel(x_ref, g_ref, w_ref, gk_ref, seg_ref, o_ref, ob_ref):
    a = _rms(x_ref[...]) * g_ref[...]
    z = _dot(a.astype(BF16), w_ref[...])
    for c in range(MEM_WIDTH // LANES):
        sl = slice(c * LANES, (c + 1) * LANES)
        xc = z[:, sl]
        ssq = _dot((xc * xc).astype(BF16), seg_ref[...])
        kn = xc * lax.rsqrt(ssq * (1.0 / HEAD_DIM) + EPS) * gk_ref[:, sl]
        o_ref[:, sl] = kn
        ob_ref[:, sl] = kn.astype(BF16)
    o_ref[:, MEM_WIDTH:] = z[:, MEM_WIDTH:]
    ob_ref[:, MEM_WIDTH:] = z[:, MEM_WIDTH:].astype(BF16)


def _memkv_call(mem2d, g_mem, w_mem, gmk, seg, *, tm):
    n, dm = mem2d.shape
    full = lambda shape: pl.BlockSpec(shape, lambda i: (0,) * len(shape))
    return pl.pallas_call(
        _memkv_kernel,
        grid=(n // tm,),
        in_specs=[pl.BlockSpec((tm, dm), lambda i: (i, 0)), full((1, dm)), full((dm, 2 * MEM_WIDTH)),
                  full((1, MEM_WIDTH)), full((LANES, LANES))],
        out_specs=[pl.BlockSpec((tm, 2 * MEM_WIDTH), lambda i: (i, 0))] * 2,
        out_shape=[jax.ShapeDtypeStruct((n, 2 * MEM_WIDTH), F32), jax.ShapeDtypeStruct((n, 2 * MEM_WIDTH), BF16)],
        compiler_params=_cparams("parallel"),
        name="memkv",
    )(mem2d, g_mem, w_mem, gmk, seg)


def _mem_attend(qm, mkv, rows):
    lane = lax.broadcasted_iota(I32, (1, LANES), 1)
    chunks = []
    for j in range(MEM_HEADS // 2):
        k128 = mkv[:, j * LANES:(j + 1) * LANES]
        v128 = mkv[:, MEM_WIDTH + j * LANES:MEM_WIDTH + (j + 1) * LANES]
        q2 = jnp.concatenate([qm[2 * j], qm[2 * j + 1]], axis=0)
        s = _dot_nt(q2, k128)
        e = jnp.exp(s - jnp.max(s, axis=-1, keepdims=True))
        p = e / jnp.sum(e, axis=-1, keepdims=True)
        o = _dot(p.astype(BF16), v128)
        chunks.append(jnp.where(lane < HEAD_DIM, o[:rows], o[rows:2 * rows]))
    return chunks


def _memattn_kernel(qm_ref, mkv_ref, o_ref, *, tq):
    qm = [qm_ref[0, :, h * LANES:(h + 1) * LANES] for h in range(MEM_HEADS)]
    for j, chunk in enumerate(_mem_attend(qm, mkv_ref[0], tq)):
        o_ref[0, :, j * LANES:(j + 1) * LANES] = chunk


def _memattn_call(qm3, mkv3, *, tq):
    b, t, _ = qm3.shape
    m = mkv3.shape[1]
    return pl.pallas_call(
        functools.partial(_memattn_kernel, tq=tq),
        grid=(b, t // tq),
        in_specs=[pl.BlockSpec((1, tq, QM_PAD), lambda bi, i: (bi, i, 0)),
                  pl.BlockSpec((1, m, 2 * MEM_WIDTH), lambda bi, i: (bi, 0, 0))],
        out_specs=pl.BlockSpec((1, tq, MEM_WIDTH), lambda bi, i: (bi, i, 0)),
        out_shape=jax.ShapeDtypeStruct((b, t, MEM_WIDTH), F32),
        compiler_params=_cparams("parallel", "parallel"),
        name="memattn",
    )(qm3, mkv3)


def _pad_rows(rows_list):
    x = jnp.concatenate(rows_list, axis=0)
    return jnp.concatenate([x, jnp.zeros((8 - x.shape[0], x.shape[1]), x.dtype)], axis=0)


def _sample_attn1_kernel(q_ref, kc_ref, win_ref, wnew_ref, mkv_ref, qm_ref, pool_ref,
                         oc_ref, ow_ref, om_ref, score_ref, *, t_pos, n_win):
    q = q_ref[0].astype(F32)
    n_cmp = kc_ref.shape[1]
    kc128 = kc_ref[0, :, :LANES]
    vc128 = kc_ref[0, :, LANES:]
    kw_t = win_ref[0, 0].astype(BF16)
    vw_t = win_ref[0, 1].astype(BF16)
    wnew = wnew_ref[0]
    kw_new = wnew[:, :LANES].astype(BF16).astype(F32)
    vw_new = wnew[:, LANES:].astype(BF16).astype(F32)
    kc_end = lax.broadcasted_iota(I32, (1, n_cmp), 1) * CMP_STRIDE + (CMP_BLOCK - 1)
    mask_c = kc_end <= t_pos
    kw_pos = t_pos - n_win + lax.broadcasted_iota(I32, (1, n_win), 1)
    mask_w = (t_pos - kw_pos >= 0) & (t_pos - kw_pos < WINDOW) & (kw_pos >= 0)
    n_slot = pool_ref.shape[1]
    blk = lax.broadcasted_iota(I32, (1, n_slot), 1)
    cur = t_pos // SEL_BLOCK
    valid = blk * SEL_BLOCK <= t_pos
    forced = (blk == 0) | (blk == cur) | (blk == cur - 1)
    for g in range(NSA_KV_HEADS):
        qg_f = _pad_rows([q[:, (g * NSA_REP + r) * LANES:(g * NSA_REP + r + 1) * LANES] for r in range(NSA_REP)])
        qg = qg_f.astype(BF16)
        p_c = _masked_softmax(_dot_nt(qg, kc128), mask_c)
        oc_ref[0, g * NSA_REP:(g + 1) * NSA_REP, :] = _dot(p_c.astype(BF16), vc128)[:NSA_REP]
        imp = jnp.sum(p_c[:NSA_REP], axis=0, keepdims=True)
        imp8 = jnp.broadcast_to(imp, (8, n_cmp))
        imp_b = sum(_dot(piece, pool_ref[...]) for piece in _split3(imp8))[0:1]
        score_ref[0, g:g + 1, :] = jnp.where(valid, imp_b + jnp.where(forced, FORCE_BONUS, 0.0), -jnp.inf)
        s_old = jnp.where(mask_w, _dot(qg, kw_t), NEG)
        s_new = jnp.sum(qg_f * kw_new, axis=-1, keepdims=True)
        m_w = jnp.maximum(jnp.max(s_old, axis=-1, keepdims=True), s_new)
        e_old = jnp.where(mask_w, jnp.exp(s_old - m_w), 0.0)
        e_new = jnp.exp(s_new - m_w)
        den = jnp.sum(e_old, axis=-1, keepdims=True) + e_new
        o_win = (_dot_nt(e_old.astype(BF16), vw_t) + e_new.astype(BF16).astype(F32) * vw_new) / den
        ow_ref[0, g * NSA_REP:(g + 1) * NSA_REP, :] = o_win[:NSA_REP]
    qm = qm_ref[0].astype(F32)
    lane = lax.broadcasted_iota(I32, (1, LANES), 1)
    mk_t = mkv_ref[0, 0].astype(BF16)
    mv_t = mkv_ref[0, 1].astype(BF16)
    for j in range(MEM_HEADS // 2):
        q2 = _pad_rows([qm[:, h * LANES:(h + 1) * LANES] for h in (2 * j, 2 * j + 1)]).astype(BF16)
        s = _dot(q2, mk_t[j * LANES:(j + 1) * LANES, :])
        e = jnp.exp(s - jnp.max(s, axis=-1, keepdims=True))
        p = e / jnp.sum(e, axis=-1, keepdims=True)
        o = _dot_nt(p.astype(BF16), mv_t[j * LANES:(j + 1) * LANES, :])
        om_ref[0, :, j * LANES:(j + 1) * LANES] = jnp.where(lane < HEAD_DIM, o[0:1], o[1:2])


def _sample_attn1_call(q3, kcvc, win_t, wnew3, mkv_t, qm3, pool_mat, *, t_pos):
    b = q3.shape[0]
    n_cmp = kcvc.shape[1]
    n_win = win_t.shape[3]
    m = mkv_t.shape[3]
    per_b = lambda shape: pl.BlockSpec((1,) + shape, lambda bi: (bi,) + (0,) * len(shape))
    return pl.pallas_call(
        functools.partial(_sample_attn1_kernel, t_pos=t_pos, n_win=n_win),
        grid=(b,),
        in_specs=[per_b((1, Q_PAD)), per_b((n_cmp, 2 * LANES)), per_b((2, 2 * HEAD_DIM, n_win)),
                  per_b((1, 2 * LANES)), per_b((2, MEM_WIDTH, m)), per_b((1, QM_PAD)),
                  pl.BlockSpec(pool_mat.shape, lambda bi: (0, 0))],
        out_specs=[per_b((NSA_HEADS, LANES)), per_b((NSA_HEADS, LANES)), per_b((1, MEM_WIDTH)),
                   per_b((NSA_KV_HEADS, pool_mat.shape[1]))],
        out_shape=[jax.ShapeDtypeStruct((b, NSA_HEADS, LANES), F32), jax.ShapeDtypeStruct((b, NSA_HEADS, LANES), F32),
                   jax.ShapeDtypeStruct((b, 1, MEM_WIDTH), F32),
                   jax.ShapeDtypeStruct((b, NSA_KV_HEADS, pool_mat.shape[1]), F32)],
        compiler_params=_cparams("parallel"),
        name="sample_attn1",
    )(q3, kcvc, win_t, wnew3, mkv_t, qm3, pool_mat)


def _sample_topk_kernel(score_ref, idx_ref, *, n_top):
    score = score_ref[...]
    ids = lax.broadcasted_iota(I32, (1, score.shape[1]), 1).astype(F32)
    _, firsts = _topk_mask(score, ids, n_top, 1)
    lane = lax.broadcasted_iota(I32, (1, LANES), 1)
    idx = jnp.full((score.shape[0], LANES), -1, I32)
    for j, f in enumerate(firsts):
        idx = jnp.where(lane == j, f.astype(I32), idx)
    idx_ref[...] = idx


def _sample_topk_call(score2d, *, n_top):
    rows = score2d.shape[0]
    return pl.pallas_call(
        functools.partial(_sample_topk_kernel, n_top=n_top),
        out_shape=jax.ShapeDtypeStruct((rows, LANES), I32),
        compiler_params=pltpu.CompilerParams(vmem_limit_bytes=VMEM_LIMIT),
        name="sample_topk",
    )(score2d)


def _sample_sel_kernel(pt_ref, idx_ref, cache_ref, q_ref, knew_ref, gate_ref, oc_ref, ow_ref, o_ref,
                       buf, sem, *, t_pos, n_pages, n_top, page):
    b = pl.program_id(0)
    nb = pl.num_programs(0)
    blk_per_page = page // SEL_BLOCK
    n_past_blk = n_pages * blk_per_page
    idx_stride = NSA_KV_HEADS * LANES

    def blk_at(bb, g, j):
        return idx_ref[bb * idx_stride + g * LANES + j]

    def blk_copy(bb, slot, g, j, kv):
        blkc = jnp.clip(blk_at(bb, g, j), 0, n_past_blk - 1)
        pg = pt_ref[bb * n_pages + blkc // blk_per_page]
        return pltpu.make_async_copy(cache_ref.at[pg, 2 + kv], buf.at[slot, g, kv, j], sem.at[slot])

    def for_all(fn):
        for g in range(NSA_KV_HEADS):
            for j in range(n_top):
                for kv in range(2):
                    fn(g, j, kv)

    def issue(bb, slot):
        for_all(lambda g, j, kv: blk_copy(bb, slot, g, j, kv).start())

    @pl.when(b == 0)
    def _():
        issue(0, 0)

    @pl.when(b + 1 < nb)
    def _():
        issue(b + 1, (b + 1) % 2)

    slot = b % 2
    for_all(lambda g, j, kv: blk_copy(b, slot, g, j, kv).wait())

    q = q_ref[0].astype(F32)
    knew = knew_ref[0]
    k_new = knew[:, :LANES].astype(BF16).astype(F32)
    v_new = knew[:, LANES:].astype(BF16).astype(F32)
    n_keys = n_top * page
    key_lane = lax.broadcasted_iota(I32, (1, n_keys), 1)
    key_slot = key_lane // page
    key_row = key_lane % page
    cur_blk = t_pos // SEL_BLOCK
    o_s = []
    for g in range(NSA_KV_HEADS):
        qg = _pad_rows([q[:, (g * NSA_REP + r) * LANES:(g * NSA_REP + r + 1) * LANES] for r in range(NSA_REP)])
        blkvec = jnp.full((1, n_keys), -1, I32)
        has_cur = jnp.zeros((1, 1), jnp.bool_)
        for j in range(n_top):
            bj = blk_at(b, g, j)
            blkvec = jnp.where(key_slot == j, bj, blkvec)
            has_cur = has_cur | (bj == cur_blk)
        in_blk = key_row // SEL_BLOCK == blkvec % blk_per_page
        key_pos = (blkvec // blk_per_page) * page + key_row
        vis = (blkvec >= 0) & (blkvec < n_past_blk) & in_blk & (key_pos <= t_pos)
        kt = jnp.concatenate([buf[slot, g, 0, j] for j in range(n_top)], axis=1).astype(BF16)
        vt = jnp.concatenate([buf[slot, g, 1, j] for j in range(n_top)], axis=1).astype(BF16)
        s_past = jnp.where(vis, _dot(qg.astype(BF16), kt), NEG)
        s_new = jnp.where(has_cur, jnp.sum(qg * k_new, axis=-1, keepdims=True), NEG)
        m = jnp.maximum(jnp.max(s_past, axis=-1, keepdims=True), s_new)
        e_p = jnp.where(vis, jnp.exp(s_past - m), 0.0)
        e_n = jnp.where(has_cur, jnp.exp(s_new - m), 0.0)
        den = jnp.maximum(jnp.sum(e_p, axis=-1, keepdims=True) + e_n, 1e-30)
        num = _dot_nt(e_p.astype(BF16), vt) + e_n.astype(BF16).astype(F32) * v_new
        o_s.append(num / den)
    o_c = [jnp.concatenate([oc_ref[0, g * NSA_REP:(g + 1) * NSA_REP, :]] * 2, axis=0) for g in range(NSA_KV_HEADS)]
    o_w = [jnp.concatenate([ow_ref[0, g * NSA_REP:(g + 1) * NSA_REP, :]] * 2, axis=0) for g in range(NSA_KV_HEADS)]

    def store(j, chunk):
        o_ref[0, :, j * LANES:(j + 1) * LANES] = chunk

    _gate_and_pack(store, gate_ref[0], o_c, o_s, o_w, 1)


def _sample_sel_call(page_table, idx, cache_t, q3, knew3, gates3, oc, ow, *, t_pos, n_top):
    b, n_pages = page_table.shape
    page = cache_t.shape[3]
    per_b = lambda shape: pl.BlockSpec((1,) + shape, lambda bi, pt, ix: (bi, 0, 0))
    grid_spec = pltpu.PrefetchScalarGridSpec(
        num_scalar_prefetch=2,
        grid=(b,),
        in_specs=[pl.BlockSpec(memory_space=pl.ANY), per_b((1, Q_PAD)), per_b((1, 2 * LANES)), per_b((1, LANES)),
                  per_b((NSA_HEADS, LANES)), per_b((NSA_HEADS, LANES))],
        out_specs=per_b((1, NSA_WIDTH)),
        scratch_shapes=[pltpu.VMEM((2, NSA_KV_HEADS, 2, n_top, LANES, page), F32), pltpu.SemaphoreType.DMA((2,))],
    )
    return pl.pallas_call(
        functools.partial(_sample_sel_kernel, t_pos=t_pos, n_pages=n_pages, n_top=n_top, page=page),
        grid_spec=grid_spec,
        out_shape=jax.ShapeDtypeStruct((b, 1, NSA_WIDTH), F32),
        compiler_params=_cparams("arbitrary"),
        name="sample_sel",
    )(page_table.reshape(-1), idx.reshape(-1), cache_t, q3, knew3, gates3, oc, ow)


def _finish_kernel(yp_ref, on_ref, om_ref, x_ref, gm_ref, wo_ref, gf_ref, wr_ref, br_ref, tri_ref,
                   h_ref, hn_ref, route_ref, counts_ref, cnt_ref):
    gm = gm_ref[...]
    o1 = POOL_WIDTH
    o2 = o1 + NSA_WIDTH
    mixed = jnp.concatenate([_rms(yp_ref[...]) * gm[:, :o1], _rms(on_ref[...]) * gm[:, o1:o2],
                             _rms(om_ref[...]) * gm[:, o2:]], axis=-1)
    h = x_ref[...] + _dot(mixed.astype(BF16), wo_ref[...])
    h_ref[...] = h
    hn = _rms(h) * gf_ref[...]
    hn_ref[...] = hn
    logits = _dot(hn.astype(BF16), wr_ref[...]) + br_ref[...]
    lane = lax.broadcasted_iota(I32, (1, LANES), 1)
    lane_f = lane.astype(F32)
    is1 = lane < N_EXPERT_GROUPS
    m1 = jnp.max(jnp.where(is1, logits, -jnp.inf), axis=-1, keepdims=True)
    e1 = jnp.where(is1, jnp.exp(logits - m1), 0.0)
    p1 = e1 / jnp.sum(e1, axis=-1, keepdims=True)
    top1_p = jnp.max(p1, axis=-1, keepdims=True)
    grp = jnp.min(jnp.where((p1 == top1_p) & is1, lane_f, float(LANES)), axis=-1, keepdims=True)
    base = N_EXPERT_GROUPS + grp * EXPERTS_PER_GROUP
    in_g = (lane_f >= base) & (lane_f < base + EXPERTS_PER_GROUP)
    l2 = jnp.where(in_g, logits, -jnp.inf)
    v0 = jnp.max(l2, axis=-1, keepdims=True)
    i0 = jnp.min(jnp.where(l2 == v0, lane_f, float(LANES)), axis=-1, keepdims=True)
    l2b = jnp.where(lane_f == i0, -jnp.inf, l2)
    v1 = jnp.max(l2b, axis=-1, keepdims=True)
    i1 = jnp.min(jnp.where(l2b == v1, lane_f, float(LANES)), axis=-1, keepdims=True)
    ex = jnp.exp(v1 - v0)
    w0 = top1_p / (1.0 + ex)
    w1 = top1_p * ex / (1.0 + ex)
    ex0 = i0 - N_EXPERT_GROUPS
    ex1 = i1 - N_EXPERT_GROUPS
    is0 = lane_f == ex0
    is1e = lane_f == ex1
    oh0 = jnp.where(is0, 1.0, 0.0)
    oh1 = jnp.where(is1e, 1.0, 0.0)
    before0 = _dot(tri_ref[...], oh0.astype(BF16))
    before1 = _dot(tri_ref[...], oh1.astype(BF16))
    tot0 = jnp.sum(oh0, axis=0, keepdims=True)
    tot1 = jnp.sum(oh1, axis=0, keepdims=True)

    @pl.when(pl.program_id(0) == 0)
    def _():
        cnt_ref[...] = jnp.zeros(cnt_ref.shape, F32)

    seen = cnt_ref[...]
    rank0 = jnp.sum(jnp.where(is0, before0 + seen, 0.0), axis=-1, keepdims=True)
    rank1 = jnp.sum(jnp.where(is1e, before1 + seen + tot0, 0.0), axis=-1, keepdims=True)
    cnt_ref[...] = seen + tot0 + tot1
    counts_ref[...] = seen + tot0 + tot1
    route = jnp.where(lane == 0, ex0, jnp.where(lane == 1, ex1, jnp.where(lane == 2, w0, jnp.where(lane == 3, w1,
            jnp.where(lane == 4, rank0, jnp.where(lane == 5, rank1, 0.0))))))
    route_ref[...] = route


def _finish_call(yp, on, om, x2d, fw, *, tm):
    n, dm = x2d.shape
    full = lambda shape: pl.BlockSpec(shape, lambda i: (0,) * len(shape))
    rows = lambda w: pl.BlockSpec((tm, w), lambda i: (i, 0))
    tri = jnp.asarray(np.arange(tm)[None, :] < np.arange(tm)[:, None], BF16)
    return pl.pallas_call(
        _finish_kernel,
        grid=(n // tm,),
        in_specs=[rows(POOL_WIDTH), rows(NSA_WIDTH), rows(MEM_WIDTH), rows(dm), full((1, dm)), full((dm, dm)),
                  full((1, dm)), full((dm, LANES)), full((1, LANES)), full((tm, tm))],
        out_specs=[rows(dm), rows(dm), rows(LANES), full((1, LANES))],
        out_shape=[jax.ShapeDtypeStruct((n, dm), F32), jax.ShapeDtypeStruct((n, dm), F32),
                   jax.ShapeDtypeStruct((n, LANES), F32), jax.ShapeDtypeStruct((1, LANES), F32)],
        scratch_shapes=[pltpu.VMEM((1, LANES), F32)],
        compiler_params=_cparams("arbitrary"),
        name="finish",
    )(yp, on, om, x2d, fw["g_mix"], fw["w_out"], fw["g_ffn"], fw["w_r"], fw["b_r"], tri)


def _route_tables(route, counts, tm):
    n = route.shape[0]
    eid = route[:, 0:2].astype(I32)
    rank = route[:, 4:6].astype(I32)
    cnt = counts[0, :N_EXPERTS].astype(I32)
    tiles_per = jnp.maximum((cnt + tm - 1) // tm, 1)
    tile_end = jnp.cumsum(tiles_per)
    tile_start = tile_end - tiles_per
    experts = jnp.arange(N_EXPERTS, dtype=I32)
    start_of = jnp.sum(jnp.where(eid[:, :, None] == experts, tile_start, 0), axis=-1)
    pos = (start_of * tm + rank).reshape(-1)
    n_tiles = (2 * n) // tm + N_EXPERTS
    tj = jnp.arange(n_tiles, dtype=I32)
    tile_expert = jnp.minimum(jnp.sum((tj[:, None] >= tile_end[None, :]).astype(I32), axis=1), N_EXPERTS - 1)
    sel = tile_expert[:, None] == experts
    rows_left = jnp.sum(jnp.where(sel, cnt, 0), axis=-1) - (tj - jnp.sum(jnp.where(sel, tile_start, 0), axis=-1)) * tm
    tile_nvalid = jnp.where(tj < tile_end[-1], jnp.clip(rows_left, 0, tm), 0).astype(I32)
    return tile_expert, tile_nvalid, pos


def _row_wait_all(src_row, dst_row, sem, count):
    for _ in range(count):
        pltpu.make_async_copy(src_row, dst_row, sem).wait()


def _dispatch_kernel(pos_ref, nv_ref, hn_ref, xs_ref, stage, zbuf, sem, zsem, *, tm, tme, n_tiles):
    i = pl.program_id(0)
    nt = pl.num_programs(0)
    wait_slot = lambda s: _row_wait_all(stage.at[s, pl.ds(0, 1), :], xs_ref.at[pl.ds(0, 1), :], sem.at[s], 2 * tm)

    @pl.when(i == 0)
    def _():
        zbuf[...] = jnp.zeros(zbuf.shape, F32)

        def fill(j, c):
            @pl.when(nv_ref[j] < tme)
            def _():
                pltpu.make_async_copy(zbuf, xs_ref.at[pl.ds(pl.multiple_of(j * tme, tme), tme), :], zsem).start()
            return c

        def drain(j, c):
            @pl.when(nv_ref[j] < tme)
            def _():
                pltpu.make_async_copy(zbuf, xs_ref.at[pl.ds(0, tme), :], zsem).wait()
            return c

        lax.fori_loop(0, n_tiles, fill, 0)
        lax.fori_loop(0, n_tiles, drain, 0)

    for slot in range(2):
        @pl.when(i >= 1)
        def _():
            wait_slot(slot)

        stage[slot] = hn_ref[slot * tm:(slot + 1) * tm, :]
        base = (2 * i + slot) * (2 * tm)
        for r in range(tm):
            for k in range(2):
                dst = pos_ref[base + 2 * r + k]
                pltpu.make_async_copy(stage.at[slot, pl.ds(r, 1), :], xs_ref.at[pl.ds(dst, 1), :], sem.at[slot]).start()

    @pl.when(i == nt - 1)
    def _():
        wait_slot(0)
        wait_slot(1)


def _dispatch_call(pos, tile_nvalid, hn, *, tm, tme):
    n, dm = hn.shape
    n_tiles = tile_nvalid.shape[0]
    grid_spec = pltpu.PrefetchScalarGridSpec(
        num_scalar_prefetch=2,
        grid=(n // (2 * tm),),
        in_specs=[pl.BlockSpec((2 * tm, dm), lambda i, pos, nv: (i, 0))],
        out_specs=pl.BlockSpec(memory_space=pl.ANY),
        scratch_shapes=[pltpu.VMEM((2, tm, dm), F32), pltpu.VMEM((tme, dm), F32),
                        pltpu.SemaphoreType.DMA((2,)), pltpu.SemaphoreType.DMA(())],
    )
    return pl.pallas_call(
        functools.partial(_dispatch_kernel, tm=tm, tme=tme, n_tiles=n_tiles),
        grid_spec=grid_spec,
        out_shape=jax.ShapeDtypeStruct((n_tiles * tme, dm), F32),
        compiler_params=_cparams("arbitrary"),
        name="moe_dispatch",
    )(pos, tile_nvalid, hn)


def _expert_kernel(te_ref, nv_ref, x_ref, wg_ref, wu_ref, wd_ref, hns_ref, routes_ref, hs_ref,
                   y_ref, ysample_ref, wgb, wub, wdb, acc_s, *, tm):
    i = pl.program_id(0)
    prev = te_ref[jnp.maximum(i - 1, 0)]

    @pl.when(i == 0)
    def _():
        acc_s[...] = jnp.zeros(acc_s.shape, F32)

    @pl.when((i == 0) | (te_ref[i] != prev))
    def _():
        wgb[...] = wg_ref[0].astype(BF16)
        wub[...] = wu_ref[0].astype(BF16)
        wdb[...] = wd_ref[0].astype(BF16)
        route = routes_ref[...]
        ef = te_ref[i].astype(F32)
        comb = jnp.where(route[:, 0:1] == ef, route[:, 2:3], 0.0) + jnp.where(route[:, 1:2] == ef, route[:, 3:4], 0.0)
        xs = hns_ref[...].astype(BF16)
        hg = _dot(xs, wgb[...])
        hu = _dot(xs, wub[...])
        hmid = hg * jax.nn.sigmoid(hg) * hu * comb
        acc_s[...] += _dot(hmid.astype(BF16), wdb[...])

    @pl.when(i == pl.num_programs(0) - 1)
    def _():
        ysample_ref[...] = hs_ref[...] + acc_s[...]

    nv = nv_ref[i]

    @pl.when(nv == 0)
    def _():
        y_ref[...] = jnp.zeros(y_ref.shape, F32)

    @pl.when(nv > 0)
    def _():
        x = x_ref[...].astype(BF16)
        hg = _dot(x, wgb[...])
        hu = _dot(x, wub[...])
        hmid = hg * jax.nn.sigmoid(hg) * hu
        y_ref[...] = _dot(hmid.astype(BF16), wdb[...])


def _expert_call(tile_expert, tile_nvalid, xs, wg, wu, wd, hn_s, route_s, h_s, *, tm):
    n_tiles = tile_expert.shape[0]
    dm = xs.shape[1]
    ff = wg.shape[2]
    ns = hn_s.shape[0]
    full = lambda shape: pl.BlockSpec(shape, lambda i, te, nv: (0,) * len(shape))
    grid_spec = pltpu.PrefetchScalarGridSpec(
        num_scalar_prefetch=2,
        grid=(n_tiles,),
        in_specs=[pl.BlockSpec((tm, dm), lambda i, te, nv: (i, 0)),
                  pl.BlockSpec((1, dm, ff), lambda i, te, nv: (te[i], 0, 0)),
                  pl.BlockSpec((1, dm, ff), lambda i, te, nv: (te[i], 0, 0)),
                  pl.BlockSpec((1, ff, dm), lambda i, te, nv: (te[i], 0, 0)),
                  full((ns, dm)), full((ns, LANES)), full((ns, dm))],
        out_specs=[pl.BlockSpec((tm, dm), lambda i, te, nv: (i, 0)), full((ns, dm))],
        scratch_shapes=[pltpu.VMEM((dm, ff), BF16), pltpu.VMEM((dm, ff), BF16), pltpu.VMEM((ff, dm), BF16),
                        pltpu.VMEM((ns, dm), F32)],
    )
    return pl.pallas_call(
        functools.partial(_expert_kernel, tm=tm),
        grid_spec=grid_spec,
        out_shape=[jax.ShapeDtypeStruct(xs.shape, F32), jax.ShapeDtypeStruct((ns, dm), F32)],
        compiler_params=_cparams("arbitrary"),
        name="moe_experts",
    )(tile_expert, tile_nvalid, xs, wg, wu, wd, hn_s, route_s, h_s)


def _combine_kernel(pos_ref, ys_ref, h_ref, route_ref, o_ref, gbuf, sem, *, tm):
    i = pl.program_id(0)
    nt = pl.num_programs(0)

    def issue(half_tile, s):
        base = half_tile * (2 * tm)
        for r in range(tm):
            for k in range(2):
                src = pos_ref[base + 2 * r + k]
                pltpu.make_async_copy(ys_ref.at[pl.ds(src, 1), :], gbuf.at[s, k, pl.ds(r, 1), :], sem.at[s]).start()

    @pl.when(i == 0)
    def _():
        issue(0, 0)
        issue(1, 1)

    for s in range(2):
        _row_wait_all(ys_ref.at[pl.ds(0, 1), :], gbuf.at[s, 0, pl.ds(0, 1), :], sem.at[s], 2 * tm)
        rows = slice(s * tm, (s + 1) * tm)
        w0 = route_ref[rows, 2:3]
        w1 = route_ref[rows, 3:4]
        o_ref[rows, :] = h_ref[rows, :] + (w0 * gbuf[s, 0] + w1 * gbuf[s, 1])

        @pl.when(i + 1 < nt)
        def _():
            issue(2 * (i + 1) + s, s)


def _combine_call(pos, ys, h, route, *, tm):
    n, dm = h.shape
    grid_spec = pltpu.PrefetchScalarGridSpec(
        num_scalar_prefetch=1,
        grid=(n // (2 * tm),),
        in_specs=[pl.BlockSpec(memory_space=pl.ANY),
                  pl.BlockSpec((2 * tm, dm), lambda i, pos: (i, 0)),
                  pl.BlockSpec((2 * tm, LANES), lambda i, pos: (i, 0))],
        out_specs=pl.BlockSpec((2 * tm, dm), lambda i, pos: (i, 0)),
        scratch_shapes=[pltpu.VMEM((2, 2, tm, dm), F32), pltpu.SemaphoreType.DMA((2,))],
    )
    return pl.pallas_call(
        functools.partial(_combine_kernel, tm=tm),
        grid_spec=grid_spec,
        out_shape=jax.ShapeDtypeStruct((n, dm), F32),
        compiler_params=_cparams("arbitrary"),
        name="moe_combine",
    )(pos, ys, h, route)


def _prep_weights(l, g_attn, w_in, g_q, g_k, pe_cmp, w_cmp1, w_cmp2, w_pool, s_pool, g_mem, w_mem_kv, g_mq, g_mk,
                  g_mix, w_out, g_ffn, w_router1, b_router1, w_router2, b_router2):
    dm = w_in.shape[1]
    w = w_in[l]
    o1 = POOL_WIDTH
    o2 = o1 + NSA_WIDTH
    o3 = o2 + NSA_KV_WIDTH
    o4 = o3 + GATE_WIDTH
    wg_pad = jnp.pad(w[:, o3:o4], ((0, 0), (0, LANES - GATE_WIDTH)))
    w_packed = jnp.concatenate([w[:, :o3], w[:, o4:], wg_pad], axis=1).astype(BF16)
    half = ROPE_DIM // 2
    inv = jnp.power(ROPE_THETA, -jnp.arange(half, dtype=F32) * 2.0 / ROPE_DIM)
    d = np.arange(LANES) % HEAD_DIM
    inv_lane = jnp.where(jnp.asarray(d < ROPE_DIM), inv[jnp.asarray(d % half)], 0.0).reshape(1, LANES)
    seg = jnp.asarray((np.arange(LANES)[:, None] // HEAD_DIM == np.arange(LANES)[None, :] // HEAD_DIM), BF16)
    gk = jnp.concatenate([jnp.tile(g_k[l, br], 2) for br in range(N_BRANCH)]).reshape(1, N_BRANCH * LANES)
    proj = dict(g_attn=g_attn[l].reshape(1, dm), w_in=w_packed, gq=jnp.tile(g_q[l], Q_PAD // HEAD_DIM).reshape(1, Q_PAD),
                gk=gk, gmq=jnp.tile(g_mq[l], QM_PAD // HEAD_DIM).reshape(1, QM_PAD), inv=inv_lane, seg=seg)

    n_grp = len(POOL_WINDOWS)
    pg = POOL_WIDTH // n_grp
    w_bd = (jnp.asarray(np.eye(n_grp, dtype=np.float32))[:, None, :, None] * w_pool[l][:, :, None, :]
            ).reshape(POOL_WIDTH, POOL_WIDTH).astype(BF16)
    pool = dict(w=w_bd, s=s_pool[l].reshape(1, POOL_WIDTH))

    half_rows = CMP_STRIDE * HEAD_DIM
    w1 = jnp.concatenate([w_cmp1[l][:, :half_rows], w_cmp1[l][:, half_rows:]], axis=2).astype(BF16)
    cmp_w = dict(w=w1, pe=pe_cmp[l].reshape(2, 1, CMP_BLOCK * HEAD_DIM), w2=w_cmp2[l].astype(BF16))

    mem = dict(g=g_mem[l].reshape(1, dm), w=w_mem_kv[l].astype(BF16),
               gk=jnp.tile(g_mk[l], MEM_HEADS).reshape(1, MEM_WIDTH))
    w_r = jnp.concatenate([w_router1[l], w_router2[l].reshape(dm, N_EXPERTS)], axis=1)
    w_r = jnp.pad(w_r, ((0, 0), (0, LANES - w_r.shape[1]))).astype(BF16)
    b_r = jnp.concatenate([b_router1[l], b_router2[l].reshape(-1)])
    b_r = jnp.pad(b_r, (0, LANES - b_r.shape[0])).reshape(1, LANES)
    fin = dict(g_mix=g_mix[l].reshape(1, dm), w_out=w_out[l].astype(BF16), g_ffn=g_ffn[l].reshape(1, dm),
               w_r=w_r, b_r=b_r)
    return proj, pool, cmp_w, mem, fin


def _pick(n, prefs):
    for p in prefs:
        if n % p == 0:
            return p
    return n


def kernel(x_prompt, x_sample, cache_kv, cache_win, state_pool, cache_mem_kv, page_table, mem_prompt, g_attn, w_in, g_q, g_k, pe_cmp, w_cmp1, w_cmp2, w_pool, s_pool, g_mem, w_mem_kv, g_mq, g_mk, g_mix, w_out, g_ffn, w_router1, b_router1, w_router2, b_router2, w_gate, w_up, w_down):
    depth = w_in.shape[0]
    bp, t, dm = x_prompt.shape
    bs, ts, _ = x_sample.shape
    n_pages = page_table.shape[1]
    page = cache_kv.shape[2]
    past_len = n_pages * page
    n_win = cache_win.shape[2]
    ff = w_gate.shape[-1]
    tq, tk = 256, 512
    assert ts == 1 and n_win == WINDOW and page % SEL_BLOCK == 0 and t % tk == 0 and t >= WINDOW + tq
    n_sel = t // SEL_BLOCK
    assert min(SEL_TOPK, n_sel) >= 3 and n_sel <= LANES
    poolt = jnp.asarray(np.arange(t // CMP_STRIDE)[None, :] // CMP_PER_SEL == np.arange(n_sel)[:, None], BF16)
    key_blk = np.arange(t // tk)[:, None, None] * (tk // SEL_BLOCK) + np.arange(tk)[None, None, :] // SEL_BLOCK
    e_mat = jnp.asarray(np.arange(LANES)[None, :, None] == key_blk, BF16)
    n_cmp_s = past_len // CMP_STRIDE
    n_slot_s = -(-(past_len // SEL_BLOCK + 1) // LANES) * LANES
    pool_s = jnp.asarray(np.arange(n_cmp_s)[:, None] // CMP_PER_SEL == np.arange(n_slot_s)[None, :], BF16)

    hp, hs = x_prompt, x_sample
    outs = [[] for _ in range(7)]
    for l in range(depth):
        proj_w, pool_w, cmp_w, mem_w, fin_w = _prep_weights(
            l, g_attn, w_in, g_q, g_k, pe_cmp, w_cmp1, w_cmp2, w_pool, s_pool, g_mem, w_mem_kv, g_mq, g_mk,
            g_mix, w_out, g_ffn, w_router1, b_router1, w_router2, b_router2)
        wg = w_gate[l].reshape(N_EXPERTS, dm, ff)
        wu = w_up[l].reshape(N_EXPERTS, dm, ff)
        wd = w_down[l].reshape(N_EXPERTS, ff, dm)
        kw_cols = CMP_STRIDE * 2 * LANES

        n = bp * t
        xp2 = hp.reshape(n, dm)
        u, qb, rows4, rowsw, kvb, gates, qmb, rows4t, rowswt = _proj_call(
            xp2, proj_w, seq_len=t, pos0=0, tm=_pick(t, (512, 256, 128)), feature_major_rows4=True)
        u3 = u.reshape(bp, t, POOL_WIDTH)
        y_pool = _pool_call(u3, pool_w["w"], pool_w["s"], tp=_pick(t, (512, 256, 128)), pos0=0)
        n_slab = 2 * NSA_KV_HEADS
        chunk_w = CMP_STRIDE * HEAD_DIM
        kcvc = _compress_prompt_call(rows4.reshape(bp, t, 4 * LANES), jnp.zeros((bp, n_slab, chunk_w), F32), cmp_w)
        o_nsa = _nsa_prompt_call(qb.reshape(bp, t, Q_PAD), gates.reshape(bp, t, LANES), kcvc,
                                 kvb.reshape(bp, t, NSA_KV_WIDTH), poolt, e_mat, tq=tq, tk=tk)
        m_len = mem_prompt.shape[1]
        mkv, mkvb = _memkv_call(mem_prompt.reshape(bp * m_len, dm), mem_w["g"], mem_w["w"], mem_w["gk"],
                                proj_w["seg"], tm=_pick(bp * m_len, (256, 128)))
        o_mem = _memattn_call(qmb.reshape(bp, t, QM_PAD), mkvb.reshape(bp, m_len, 2 * MEM_WIDTH),
                              tq=_pick(t, (512, 256, 128)))
        tmf = _pick(n, (512, 256, 128))
        h_p, hn_p, route_p, counts_p = _finish_call(y_pool.reshape(n, POOL_WIDTH), o_nsa.reshape(n, NSA_WIDTH),
                                                    o_mem.reshape(n, MEM_WIDTH), xp2, fin_w, tm=tmf)
        tme = 256
        tmd = _pick(n, (256, 128))
        te, nv, pos = _route_tables(route_p, counts_p, tme)
        xs = _dispatch_call(pos, nv, hn_p, tm=tmd, tme=tme)

        xs2 = hs.reshape(bs, dm)
        u_s, qb_s, rows4_s, rowsw_s, _, gates_s, qmb_s = _proj_call(xs2, proj_w, seq_len=1, pos0=past_len, tm=bs)
        ext = jnp.concatenate([state_pool[l], u_s[:, None, :]], axis=1)
        y_pool_s = _pool_call(ext, pool_w["w"], pool_w["s"], tp=POOL_STATE + 1, pos0=past_len - POOL_STATE)[:, -1, :]
        cache_t = jnp.transpose(cache_kv[l], (0, 2, 3, 4, 1)).reshape(
            cache_kv.shape[1], N_KV_SLOTS, NSA_KV_HEADS * HEAD_DIM, page)
        tail_s = jnp.pad(rows4_s[:, :2 * LANES].reshape(bs, n_slab, HEAD_DIM), ((0, 0), (0, 0), (0, chunk_w - HEAD_DIM)))
        kcvc_s = _compress_sample_call(page_table, cache_t, tail_s, cmp_w)
        win_s = jnp.concatenate([cache_win[l].reshape(bs, n_win, 2 * LANES)[:, 1:], rowsw_s[:, None, :]], axis=1)
        win_t = jnp.transpose(cache_win[l], (0, 2, 3, 4, 1)).reshape(bs, 2, NSA_KV_HEADS * HEAD_DIM, n_win)
        mkv_t = jnp.transpose(cache_mem_kv[l], (0, 2, 3, 4, 1)).reshape(bs, 2, MEM_WIDTH, cache_mem_kv.shape[2])
        oc, ow, om_s, score_s = _sample_attn1_call(qb_s.reshape(bs, 1, Q_PAD), kcvc_s, win_t,
                                                   rowsw_s.reshape(bs, 1, 2 * LANES), mkv_t,
                                                   qmb_s.reshape(bs, 1, QM_PAD), pool_s, t_pos=past_len)
        n_top = min(SEL_TOPK, past_len // SEL_BLOCK + 1)
        idx = _sample_topk_call(score_s.reshape(bs * NSA_KV_HEADS, n_slot_s), n_top=n_top)
        o_nsa_s = _sample_sel_call(page_table, idx, cache_t, qb_s.reshape(bs, 1, Q_PAD),
                                   rows4_s[:, 2 * LANES:].reshape(bs, 1, 2 * LANES), gates_s.reshape(bs, 1, LANES),
                                   oc, ow, t_pos=past_len, n_top=n_top)
        h_s, hn_s, route_s, _ = _finish_call(y_pool_s, o_nsa_s.reshape(bs, NSA_WIDTH), om_s.reshape(bs, MEM_WIDTH),
                                          xs2, fin_w, tm=bs)

        ys, y_s = _expert_call(te, nv, xs, wg, wu, wd, hn_s, route_s, h_s, tm=tme)
        y_p = _combine_call(pos, ys, h_p, route_p, tm=tmd).reshape(bp, t, dm)
        y_s = y_s.reshape(bs, 1, dm)

        keep = min(WINDOW, t)
        outs[0].append(jnp.transpose(rows4t.reshape(bp, N_KV_SLOTS, NSA_KV_HEADS, HEAD_DIM, t), (0, 4, 1, 2, 3)))
        outs[1].append(rows4_s.reshape(bs, 1, N_KV_SLOTS, NSA_KV_HEADS, HEAD_DIM))
        outs[2].append(jnp.transpose(rowswt[:, :, t - keep:].reshape(bp, 2, NSA_KV_HEADS, HEAD_DIM, keep), (0, 4, 1, 2, 3)))
        outs[3].append(win_s.reshape(bs, n_win, 2, NSA_KV_HEADS, HEAD_DIM))
        outs[4].append(u3[:, t - POOL_STATE:])
        outs[5].append(ext[:, 1:])
        outs[6].append(mkv.reshape(bp, m_len, 2, MEM_HEADS, HEAD_DIM))
        hp, hs = y_p, y_s
    return (hp, hs) + tuple(jnp.stack(o) for o in outs)
```

```python
import functools

import numpy as np
import jax
import jax.numpy as jnp
from jax import lax
from jax.experimental import pallas as pl
from jax.experimental.pallas import tpu as pltpu

F32 = jnp.float32
BF16 = jnp.bfloat16
I32 = jnp.int32

HEAD_DIM = 64
POOL_WINDOWS = (2, 4, 8, 16)
POOL_STATE = max(POOL_WINDOWS) - 1
NSA_HEADS = 8
NSA_KV_HEADS = 2
NSA_REP = NSA_HEADS // NSA_KV_HEADS
N_BRANCH = 3
CMP_BLOCK = 32
CMP_STRIDE = 16
CMP_HIDDEN = 2 * HEAD_DIM
SEL_BLOCK = 64
SEL_TOPK = 16
CMP_PER_SEL = SEL_BLOCK // CMP_STRIDE
WINDOW = 512
FORCE_BONUS = 1000.0
MEM_HEADS = 4
ROPE_DIM = HEAD_DIM // 4
ROPE_THETA = 500000.0
N_EXPERT_GROUPS = 4
EXPERTS_PER_GROUP = 8
N_EXPERTS = N_EXPERT_GROUPS * EXPERTS_PER_GROUP
EPS = 1e-6
N_KV_SLOTS = 4

LANES = 128
POOL_WIDTH = 256
NSA_WIDTH = NSA_HEADS * HEAD_DIM
NSA_KV_WIDTH = N_BRANCH * 2 * NSA_KV_HEADS * HEAD_DIM
GATE_WIDTH = NSA_HEADS * N_BRANCH
MEM_WIDTH = MEM_HEADS * HEAD_DIM
Q_PAD = NSA_HEADS * LANES
QM_PAD = MEM_HEADS * LANES
C_U = 0
C_Q = C_U + POOL_WIDTH
C_KV = C_Q + NSA_WIDTH
C_QM = C_KV + NSA_KV_WIDTH
C_G = C_QM + MEM_WIDTH
C_END = C_G + LANES

NEG = -1e30
VMEM_LIMIT = 48 * 1024 * 1024

_NT = (((1,), (1,)), ((), ()))


def _cparams(*sem):
    return pltpu.CompilerParams(dimension_semantics=tuple(sem), vmem_limit_bytes=VMEM_LIMIT)


def _dot(a, b):
    return jnp.dot(a, b, preferred_element_type=F32)


def _dot_nt(a, b):
    return lax.dot_general(a, b, _NT, preferred_element_type=F32)


def _rms(x):
    return x * lax.rsqrt(jnp.mean(x * x, axis=-1, keepdims=True) + EPS)


def _masked_softmax(s, mask):
    sm = jnp.where(mask, s, NEG)
    m = jnp.max(sm, axis=-1, keepdims=True)
    e = jnp.where(mask, jnp.exp(sm - m), 0.0)
    return e / jnp.maximum(jnp.sum(e, axis=-1, keepdims=True), 1e-30)


def _split3(x):
    hi = x.astype(BF16)
    r1 = x - hi.astype(F32)
    mid = r1.astype(BF16)
    lo = (r1 - mid.astype(F32)).astype(BF16)
    return hi, mid, lo


def _proj_kernel(x_ref, ga_ref, w_ref, gq_ref, gk_ref, gmq_ref, inv_ref, seg_ref,
                 u_ref, q_ref, rows4_ref, rowsw_ref, kvb_ref, gates_ref, qm_ref, *rest,
                 tm, seq_len, pos0, consecutive):
    *maybe_t_refs, cos_in_tile, sin_in_tile = rest
    i = pl.program_id(0)
    a = _rms(x_ref[...]) * ga_ref[...]
    z = _dot(a.astype(BF16), w_ref[...])
    u_ref[...] = z[:, C_U:C_U + POOL_WIDTH]

    inv = inv_ref[...]

    @pl.when(i == 0)
    def _():
        j = lax.broadcasted_iota(I32, (tm, 1), 0).astype(F32) if consecutive else jnp.zeros((tm, 1), F32)
        cos_in_tile[...] = jnp.cos(j * inv)
        sin_in_tile[...] = jnp.sin(j * inv)

    first = (pos0 + (i * tm) % seq_len).astype(F32) * jnp.broadcast_to(inv, (8, LANES))
    cos_f, sin_f = jnp.cos(first)[0:1], jnp.sin(first)[0:1]
    cos = cos_f * cos_in_tile[...] - sin_f * sin_in_tile[...]
    sin = sin_f * cos_in_tile[...] + cos_f * sin_in_tile[...]
    d = lax.broadcasted_iota(I32, (1, LANES), 1) % HEAD_DIM
    half = ROPE_DIM // 2
    s_next = jnp.where(d < half, -sin, 0.0)
    s_prev = jnp.where((d >= half) & (d < ROPE_DIM), sin, 0.0)
    seg = seg_ref[...]

    def head_norm(xc, g):
        ssq = _dot((xc * xc).astype(BF16), seg)
        return xc * lax.rsqrt(ssq * (1.0 / HEAD_DIM) + EPS) * g

    def rope(xc):
        return xc * cos + pltpu.roll(xc, LANES - half, 1) * s_next + pltpu.roll(xc, half, 1) * s_prev

    lane_half = lax.broadcasted_iota(I32, (1, LANES), 1) // HEAD_DIM

    def single_head(col0, head, slot_half):
        tile = z[:, col0 + (head // 2) * LANES:col0 + (head // 2 + 1) * LANES]
        if head % 2 != slot_half:
            tile = pltpu.roll(tile, HEAD_DIM, 1)
        return jnp.where(lane_half == slot_half, tile, 0.0)

    scale = HEAD_DIM ** -0.5
    for c in range(NSA_HEADS):
        sl = slice(c * LANES, (c + 1) * LANES)
        qc = rope(head_norm(single_head(C_Q, c, c // NSA_REP), gq_ref[:, sl]))
        q_ref[:, sl] = (qc * scale).astype(BF16)
    for br in range(N_BRANCH):
        k0 = C_KV + br * 2 * LANES
        kn = rope(head_norm(z[:, k0:k0 + LANES], gk_ref[:, br * LANES:(br + 1) * LANES]))
        vv = z[:, k0 + LANES:k0 + 2 * LANES]
        kvb_ref[:, br * 2 * LANES:br * 2 * LANES + LANES] = kn.astype(BF16)
        kvb_ref[:, br * 2 * LANES + LANES:(br + 1) * 2 * LANES] = vv.astype(BF16)
        if br < 2:
            rows4_ref[:, br * 2 * LANES:br * 2 * LANES + LANES] = kn
            rows4_ref[:, br * 2 * LANES + LANES:(br + 1) * 2 * LANES] = vv
            if maybe_t_refs:
                maybe_t_refs[0][0, br * 2 * LANES:br * 2 * LANES + LANES, :] = kn.T
                maybe_t_refs[0][0, br * 2 * LANES + LANES:(br + 1) * 2 * LANES, :] = vv.T
        else:
            rowsw_ref[:, :LANES] = kn
            rowsw_ref[:, LANES:] = vv
            if maybe_t_refs:
                maybe_t_refs[1][0, :LANES, :] = kn.T
                maybe_t_refs[1][0, LANES:, :] = vv.T
    for c in range(MEM_HEADS):
        sl = slice(c * LANES, (c + 1) * LANES)
        qmc = head_norm(single_head(C_QM, c, c % 2), gmq_ref[:, sl])
        qm_ref[:, sl] = (qmc * scale).astype(BF16)
    gates_ref[...] = jax.nn.sigmoid(z[:, C_G:C_END])


def _proj_call(x2d, pw, *, seq_len, pos0, tm, feature_major_rows4=False):
    n, dm = x2d.shape
    full = lambda shape: pl.BlockSpec(shape, lambda i: (0,) * len(shape))
    rows = lambda w: pl.BlockSpec((tm, w), lambda i: (i, 0))
    outs = [(POOL_WIDTH, F32), (Q_PAD, BF16), (4 * LANES, F32), (2 * LANES, F32), (NSA_KV_WIDTH, BF16),
            (LANES, F32), (QM_PAD, BF16)]
    out_specs = [rows(w) for w, _ in outs]
    out_shape = [jax.ShapeDtypeStruct((n, w), dt) for w, dt in outs]
    if feature_major_rows4:
        per_seq = seq_len // tm
        for width in (4 * LANES, 2 * LANES):
            out_specs.append(pl.BlockSpec((1, width, tm), lambda i: (i // per_seq, 0, i % per_seq)))
            out_shape.append(jax.ShapeDtypeStruct((n // seq_len, width, seq_len), F32))
    consecutive = seq_len % tm == 0
    assert consecutive or seq_len == 1
    return pl.pallas_call(
        functools.partial(_proj_kernel, tm=tm, seq_len=seq_len, pos0=pos0, consecutive=consecutive),
        grid=(n // tm,),
        in_specs=[rows(dm), full((1, dm)), full((dm, C_END)), full((1, Q_PAD)), full((1, N_BRANCH * LANES)),
                  full((1, QM_PAD)), full((1, LANES)), full((LANES, LANES))],
        out_specs=out_specs,
        out_shape=out_shape,
        scratch_shapes=[pltpu.VMEM((tm, LANES), F32), pltpu.VMEM((tm, LANES), F32)],
        compiler_params=_cparams("arbitrary"),
        name="proj",
    )(x2d, pw["g_attn"], pw["w_in"], pw["gq"], pw["gk"], pw["gmq"], pw["inv"], pw["seg"])


def _pool_kernel(u_ref, halo_ref, w_ref, s_ref, y_ref, *, tp, pos0):
    i = pl.program_id(1)
    u = u_ref[0]
    halo = halo_ref[0] * (i > 0).astype(F32)
    n_h = POOL_STATE + 1
    ext = jnp.concatenate([halo, u], axis=0)
    sums = {1: ext}
    w = 1
    while w < max(POOL_WINDOWS):
        sums[2 * w] = sums[w] + pltpu.roll(sums[w], w, 0)
        w *= 2
    pos = pos0 + i * tp + lax.broadcasted_iota(I32, (tp, 1), 0)
    lane_grp = lax.broadcasted_iota(I32, (1, POOL_WIDTH), 1) // (POOL_WIDTH // len(POOL_WINDOWS))
    mean = jnp.zeros((tp, POOL_WIDTH), F32)
    for gi, wdw in enumerate(POOL_WINDOWS):
        cnt = jnp.minimum(pos + 1, wdw).astype(F32)
        mean = jnp.where(lane_grp == gi, sums[wdw][n_h:] / cnt, mean)
    r = mean - u
    y_ref[0] = _dot(r.astype(BF16), w_ref[...]) * s_ref[...]


def _pool_call(u3, w_bd, s_pool, *, tp, pos0):
    b, t, c = u3.shape
    n_h = POOL_STATE + 1
    return pl.pallas_call(
        functools.partial(_pool_kernel, tp=tp, pos0=pos0),
        grid=(b, t // tp),
        in_specs=[pl.BlockSpec((1, tp, c), lambda bi, i: (bi, i, 0)),
                  pl.BlockSpec((1, n_h, c), lambda bi, i: (bi, jnp.maximum(i * (tp // n_h) - 1, 0), 0)),
                  pl.BlockSpec((c, c), lambda bi, i: (0, 0)),
                  pl.BlockSpec((1, c), lambda bi, i: (0, 0))],
        out_specs=pl.BlockSpec((1, tp, c), lambda bi, i: (bi, i, 0)),
        out_shape=jax.ShapeDtypeStruct((b, t, c), F32),
        compiler_params=_cparams("parallel", "parallel"),
        name="pool",
    )(u3, u3, w_bd, s_pool)


def _split_chunks(x, n_chunks):
    return jnp.transpose(x.reshape(n_chunks, CMP_STRIDE, LANES), (1, 0, 2))


def _compress_core(load_slab, tail_ref, pe_ref, w_ref, w2_ref, n):
    rowid = lax.broadcasted_iota(I32, (n, 1), 0)
    outs = []
    for c in range(2):
        w = w_ref[c]
        pe8 = jnp.broadcast_to(pe_ref[c], (8, 2 * CMP_STRIDE * HEAD_DIM)).astype(BF16)
        pe_first = _dot(pe8[:, :CMP_STRIDE * HEAD_DIM], w)[0:1, :CMP_HIDDEN]
        pe_second = _dot(pe8[:, CMP_STRIDE * HEAD_DIM:], w)[0:1, CMP_HIDDEN:]
        pe_const = pe_first + pe_second
        lane = lax.broadcasted_iota(I32, (1, LANES), 1)
        per_group = [[], []]
        for q in range(CMP_STRIDE // 2):
            a, b = load_slab(c, 2 * q), load_slab(c, 2 * q + 1)
            per_group[0].append(jnp.where(lane < HEAD_DIM, a, pltpu.roll(b, HEAD_DIM, 1)))
            per_group[1].append(jnp.where(lane < HEAD_DIM, pltpu.roll(a, HEAD_DIM, 1), b))
        x2 = jnp.concatenate([jnp.concatenate(pg, axis=1) for pg in per_group], axis=0).astype(BF16)
        z = _dot(x2, w)
        hs = []
        for g in range(NSA_KV_HEADS):
            tail8 = jnp.broadcast_to(tail_ref[0, 2 * c + g:2 * c + g + 1, :], (8, CMP_STRIDE * HEAD_DIM)).astype(BF16)
            second_tail = _dot(tail8, w)[0:1, CMP_HIDDEN:]
            first = z[g * n:(g + 1) * n, :CMP_HIDDEN]
            second = z[g * n:(g + 1) * n, CMP_HIDDEN:]
            shifted = pltpu.roll(second, n - 1, 0)
            h = first + jnp.where(rowid == n - 1, second_tail, shifted) + pe_const
            hs.append(jax.nn.gelu(h, approximate=True))
        o = _dot(jnp.concatenate(hs, axis=0).astype(BF16), w2_ref[c])
        outs += [o[:n], o[n:]]
    return jnp.concatenate(outs, axis=1)


def _compress_prompt_kernel(k_ref, v_ref, tail_ref, pe_ref, w_ref, w2_ref, out_ref, *, n):
    slabs = [_split_chunks(ref[0], n) for ref in (k_ref, v_ref)]
    out_ref[0] = _compress_core(lambda c, r: slabs[c][r], tail_ref, pe_ref, w_ref, w2_ref, n).astype(BF16)


def _compress_prompt_call(rows4_3d, tail, cw):
    b, t, _ = rows4_3d.shape
    n = t // CMP_STRIDE
    full = lambda a: pl.BlockSpec(a.shape, lambda bi: (0,) * a.ndim)
    return pl.pallas_call(
        functools.partial(_compress_prompt_kernel, n=n),
        grid=(b,),
        in_specs=[pl.BlockSpec((1, t, LANES), lambda bi: (bi, 0, 0)),
                  pl.BlockSpec((1, t, LANES), lambda bi: (bi, 0, 1)),
                  pl.BlockSpec((1,) + tail.shape[1:], lambda bi: (bi, 0, 0)),
                  full(cw["pe"]), full(cw["w"]), full(cw["w2"])],
        out_specs=pl.BlockSpec((1, n, 2 * LANES), lambda bi: (bi, 0, 0)),
        out_shape=jax.ShapeDtypeStruct((b, n, 2 * LANES), BF16),
        compiler_params=_cparams("parallel"),
        name="compress_prompt",
    )(rows4_3d, rows4_3d, tail, cw["pe"], cw["w"], cw["w2"])


def _compress_sample_kernel(pt_ref, cache_ref, tail_ref, pe_ref, w_ref, w2_ref, out_ref,
                            buf, slabs, sem, *, n, n_pages, page):
    b = pl.program_id(0)
    nb = pl.num_programs(0)

    def page_copy(bb, slot, p, c):
        return pltpu.make_async_copy(cache_ref.at[pt_ref[bb * n_pages + p], c], buf.at[slot, c, p], sem.at[slot])

    def for_all(fn):
        for p in range(n_pages):
            for c in range(2):
                fn(p, c)

    def issue(bb, slot):
        for_all(lambda p, c: page_copy(bb, slot, p, c).start())

    @pl.when(b == 0)
    def _():
        issue(0, 0)

    @pl.when(b + 1 < nb)
    def _():
        issue(b + 1, (b + 1) % 2)

    slot = b % 2
    for_all(lambda p, c: page_copy(b, slot, p, c).wait())

    unroll = next(u for u in (8, 4, 2, 1) if n_pages % u == 0)

    cpp = page // CMP_STRIDE

    def to_slabs(pp, carry):
        for k in range(unroll):
            p = pp * unroll + k
            for c in range(2):
                by_r = _split_chunks(buf[slot, c, p].T, cpp)
                for r in range(CMP_STRIDE):
                    slabs[c, r, pl.ds(pl.multiple_of(p * cpp, cpp), cpp), :] = by_r[r]
        return carry

    lax.fori_loop(0, n_pages // unroll, to_slabs, 0)
    out_ref[0] = _compress_core(lambda c, r: slabs[c, r], tail_ref, pe_ref, w_ref, w2_ref, n).astype(BF16)


def _compress_sample_call(page_table, cache_t, tail, cw):
    b, n_pages = page_table.shape
    page = cache_t.shape[3]
    n = n_pages * page // CMP_STRIDE
    full = lambda a: pl.BlockSpec(a.shape, lambda bi, pt: (0,) * a.ndim)
    grid_spec = pltpu.PrefetchScalarGridSpec(
        num_scalar_prefetch=1,
        grid=(b,),
        in_specs=[pl.BlockSpec(memory_space=pl.ANY),
                  pl.BlockSpec((1,) + tail.shape[1:], lambda bi, pt: (bi, 0, 0)),
                  full(cw["pe"]), full(cw["w"]), full(cw["w2"])],
        out_specs=pl.BlockSpec((1, n, 2 * LANES), lambda bi, pt: (bi, 0, 0)),
        scratch_shapes=[pltpu.VMEM((2, 2, n_pages, LANES, page), F32), pltpu.VMEM((2, CMP_STRIDE, n, LANES), F32),
                        pltpu.SemaphoreType.DMA((2,))],
    )
    return pl.pallas_call(
        functools.partial(_compress_sample_kernel, n=n, n_pages=n_pages, page=page),
        grid_spec=grid_spec,
        out_shape=jax.ShapeDtypeStruct((b, n, 2 * LANES), BF16),
        compiler_params=_cparams("arbitrary"),
        name="compress_sample",
    )(page_table.reshape(-1), cache_t, tail, cw["pe"], cw["w"], cw["w2"])


def _topk_mask(score, ids, n_top, axis):
    sel = jnp.zeros(score.shape, jnp.bool_)
    work = score
    firsts = []
    big = float(score.shape[axis])
    for _ in range(n_top):
        m = jnp.max(work, axis=axis, keepdims=True)
        first = jnp.min(jnp.where(work == m, ids, big), axis=axis, keepdims=True)
        pick = (ids == first) & (m > -jnp.inf)
        sel = sel | pick
        work = jnp.where(pick, -jnp.inf, work)
        firsts.append(jnp.where(m > -jnp.inf, first, -1.0))
    return sel, firsts


def _gate_and_pack(o_ref_store, gates, o_c, o_s, o_w, rows):
    lane = lax.broadcasted_iota(I32, (1, LANES), 1)
    heads = []
    for h in range(NSA_HEADS):
        g, r = divmod(h, NSA_REP)
        rs = slice(r * rows, (r + 1) * rows)
        gc, gs, gw = (gates[:, N_BRANCH * h + k:N_BRANCH * h + k + 1] for k in range(N_BRANCH))
        heads.append(gc * o_c[g][rs] + gs * o_s[g][rs] + gw * o_w[g][rs])
    for j in range(NSA_HEADS // 2):
        a, b = heads[2 * j], heads[2 * j + 1]
        if (2 * j) // NSA_REP == 0:
            chunk = jnp.where(lane < HEAD_DIM, a, pltpu.roll(b, HEAD_DIM, 1))
        else:
            chunk = jnp.where(lane < HEAD_DIM, pltpu.roll(a, HEAD_DIM, 1), b)
        o_ref_store(j, chunk)


def _nsa_prompt_kernel(q_ref, gate_ref, kc_ref, kv_ref, poolt_ref, e_ref, o_ref, acc_ref, *, tq, tk, t_len):
    i = pl.program_id(1)
    s0 = i * tq
    rows4 = NSA_REP * tq
    tpos = s0 + lax.broadcasted_iota(I32, (tq, 1), 0)
    rep = lambda x: jnp.concatenate([x] * NSA_REP, axis=0)
    n_cmp = kc_ref.shape[1]
    kc128 = kc_ref[0, :, :LANES]
    vc128 = kc_ref[0, :, LANES:]
    kc_end = lax.broadcasted_iota(I32, (1, n_cmp), 1) * CMP_STRIDE + (CMP_BLOCK - 1)
    bias_c = rep(jnp.where(kc_end <= tpos, 0.0, NEG))
    any_c = rep((tpos >= CMP_BLOCK - 1).astype(F32))
    n_sel = t_len // SEL_BLOCK
    blk = lax.broadcasted_iota(I32, (n_sel, 1), 0)
    blk_f = blk.astype(F32)
    tq_lane = s0 + lax.broadcasted_iota(I32, (1, tq), 1)
    cur = tq_lane // SEL_BLOCK
    valid = blk * SEL_BLOCK <= tq_lane
    forced = (blk == 0) | (blk == cur) | (blk == cur - 1)
    sub = min(tq, LANES)
    w_len = WINDOW + sub
    w_starts, w_biases = [], []
    for j in range(tq // sub):
        w_start = pl.multiple_of(jnp.maximum(s0 + j * sub - WINDOW, 0), LANES)
        dist = tpos[j * sub:(j + 1) * sub] - (w_start + lax.broadcasted_iota(I32, (1, w_len), 1))
        w_starts.append(w_start)
        w_biases.append(rep(jnp.where((dist >= 0) & (dist < WINDOW), 0.0, NEG)))
    c_diag = s0 // tk
    kpos_d = c_diag * tk + lax.broadcasted_iota(I32, (1, tk), 1)
    bias_causal = jnp.where(kpos_d <= tpos, 0.0, NEG)

    o_c, o_s, o_w, qgs, selbs = [], [], [], [], []
    for g in range(NSA_KV_HEADS):
        qg = jnp.concatenate([q_ref[0, :, (g * NSA_REP + r) * LANES:(g * NSA_REP + r + 1) * LANES]
                              for r in range(NSA_REP)], axis=0)
        qgs.append(qg)
        s_c = _dot_nt(qg, kc128) + bias_c
        e_c = jnp.exp(s_c - jnp.max(s_c, axis=-1, keepdims=True))
        p_c = e_c * (any_c / jnp.sum(e_c, axis=-1, keepdims=True))
        o_c.append(_dot(p_c.astype(BF16), vc128))
        imp = p_c[0:tq]
        for r in range(1, NSA_REP):
            imp = imp + p_c[r * tq:(r + 1) * tq]
        imp_t = sum(_dot_nt(poolt_ref[...], piece) for piece in _split3(imp))
        score = jnp.where(valid, imp_t + jnp.where(forced, FORCE_BONUS, 0.0), -jnp.inf)
        sel_t, _ = _topk_mask(score, blk_f, min(SEL_TOPK, n_sel), 0)
        selb_t = jnp.concatenate([jnp.where(sel_t, 0.0, NEG), jnp.full(((-n_sel) % LANES, tq), NEG, F32)], axis=0)
        selbs.append(selb_t.T.astype(BF16))

        slabs = [[None] * (tq // sub) for _ in range(NSA_REP)]
        for j, (w_start, bias_w) in enumerate(zip(w_starts, w_biases)):
            q_sub = jnp.concatenate([qg[r * tq + j * sub:r * tq + (j + 1) * sub] for r in range(NSA_REP)], axis=0)
            kw = kv_ref[0, pl.ds(w_start, w_len), 4 * LANES:5 * LANES]
            vw = kv_ref[0, pl.ds(w_start, w_len), 5 * LANES:6 * LANES]
            s_w = _dot_nt(q_sub, kw) + bias_w
            e_w = jnp.exp(s_w - jnp.max(s_w, axis=-1, keepdims=True))
            o_sub = _dot(e_w.astype(BF16), vw) * (1.0 / jnp.sum(e_w, axis=-1, keepdims=True))
            for r in range(NSA_REP):
                slabs[r][j] = o_sub[r * sub:(r + 1) * sub]
        o_w.append(jnp.concatenate([s for per_head in slabs for s in per_head], axis=0))

    acc_ref[...] = jnp.zeros(acc_ref.shape, F32)

    q_all = jnp.concatenate(qgs, axis=0)
    selb_all = jnp.concatenate(selbs, axis=0)
    rows8 = NSA_KV_HEADS * rows4

    def chunk_step(c, carry, extra_bias):
        m, l = carry
        k0 = pl.multiple_of(c * tk, tk)
        ks = kv_ref[0, pl.ds(k0, tk), 2 * LANES:3 * LANES]
        vs = kv_ref[0, pl.ds(k0, tk), 3 * LANES:4 * LANES]
        bias = _dot(selb_all, e_ref[c])
        if extra_bias is not None:
            bias = bias + jnp.concatenate([extra_bias] * NSA_KV_HEADS, axis=0)
        bias_rows = jnp.concatenate([rep(bias[g * tq:(g + 1) * tq]) for g in range(NSA_KV_HEADS)], axis=0)
        s = _dot_nt(q_all, ks) + bias_rows
        m_new = jnp.maximum(m, jnp.max(s, axis=-1, keepdims=True))
        alpha = jnp.exp(m - m_new)
        p = jnp.exp(s - m_new)
        l_new = alpha * l + jnp.sum(p, axis=-1, keepdims=True)
        acc_ref[...] = alpha * acc_ref[...] + _dot(p.astype(BF16), vs)
        return m_new, l_new

    init = (jnp.full((rows8, 1), NEG, F32), jnp.zeros((rows8, 1), F32))
    carry = lax.fori_loop(0, c_diag, lambda c, cr: chunk_step(c, cr, None), init)
    _, l_fin = chunk_step(c_diag, carry, bias_causal)
    o_all = acc_ref[...] * (1.0 / jnp.maximum(l_fin, 1e-30))
    for g in range(NSA_KV_HEADS):
        o_s.append(o_all[g * rows4:(g + 1) * rows4])

    def store(j, chunk):
        o_ref[0, :, j * LANES:(j + 1) * LANES] = chunk

    _gate_and_pack(store, gate_ref[0], o_c, o_s, o_w, tq)


def _nsa_prompt_call(q3, gates3, kcvc, kvb3, poolt, e_mat, *, tq, tk):
    b, t, _ = q3.shape
    n_cmp = kcvc.shape[1]
    return pl.pallas_call(
        functools.partial(_nsa_prompt_kernel, tq=tq, tk=tk, t_len=t),
        grid=(b, t // tq),
        in_specs=[pl.BlockSpec((1, tq, Q_PAD), lambda bi, i: (bi, i, 0)),
                  pl.BlockSpec((1, tq, LANES), lambda bi, i: (bi, i, 0)),
                  pl.BlockSpec((1, n_cmp, 2 * LANES), lambda bi, i: (bi, 0, 0)),
                  pl.BlockSpec((1, t, NSA_KV_WIDTH), lambda bi, i: (bi, 0, 0)),
                  pl.BlockSpec(poolt.shape, lambda bi, i: (0, 0)),
                  pl.BlockSpec(e_mat.shape, lambda bi, i: (0, 0, 0))],
        out_specs=pl.BlockSpec((1, tq, NSA_WIDTH), lambda bi, i: (bi, i, 0)),
        out_shape=jax.ShapeDtypeStruct((b, t, NSA_WIDTH), F32),
        scratch_shapes=[pltpu.VMEM((NSA_KV_HEADS * NSA_REP * tq, LANES), F32)],
        compiler_params=_cparams("parallel", "arbitrary"),
        name="nsa_prompt",
    )(q3, gates3, kcvc, kvb3, poolt, e_mat)


def _memkv_kernel(x_ref, g_ref, w_ref, gk_ref, seg_ref, o_ref, ob_ref):
    a = _rms(x_ref[...]) * g_ref[...]
    z = _dot(a.astype(BF16), w_ref[...])
    for c in range(MEM_WIDTH // LANES):
        sl = slice(c * LANES, (c + 1) * LANES)
        xc = z[:, sl]
        ssq = _dot((xc * xc).astype(BF16), seg_ref[...])
        kn = xc * lax.rsqrt(ssq * (1.0 / HEAD_DIM) + EPS) * gk_ref[:, sl]
        o_ref[:, sl] = kn
        ob_ref[:, sl] = kn.astype(BF16)
    o_ref[:, MEM_WIDTH:] = z[:, MEM_WIDTH:]
    ob_ref[:, MEM_WIDTH:] = z[:, MEM_WIDTH:].astype(BF16)


def _memkv_call(mem2d, g_mem, w_mem, gmk, seg, *, tm):
    n, dm = mem2d.shape
    full = lambda shape: pl.BlockSpec(shape, lambda i: (0,) * len(shape))
    return pl.pallas_call(
        _memkv_kernel,
        grid=(n // tm,),
        in_specs=[pl.BlockSpec((tm, dm), lambda i: (i, 0)), full((1, dm)), full((dm, 2 * MEM_WIDTH)),
                  full((1, MEM_WIDTH)), full((LANES, LANES))],
        out_specs=[pl.BlockSpec((tm, 2 * MEM_WIDTH), lambda i: (i, 0))] * 2,
        out_shape=[jax.ShapeDtypeStruct((n, 2 * MEM_WIDTH), F32), jax.ShapeDtypeStruct((n, 2 * MEM_WIDTH), BF16)],
        compiler_params=_cparams("parallel"),
        name="memkv",
    )(mem2d, g_mem, w_mem, gmk, seg)


def _mem_attend(qm, mkv, rows):
    lane = lax.broadcasted_iota(I32, (1, LANES), 1)
    chunks = []
    for j in range(MEM_HEADS // 2):
        k128 = mkv[:, j * LANES:(j + 1) * LANES]
        v128 = mkv[:, MEM_WIDTH + j * LANES:MEM_WIDTH + (j + 1) * LANES]
        q2 = jnp.concatenate([qm[2 * j], qm[2 * j + 1]], axis=0)
        s = _dot_nt(q2, k128)
        e = jnp.exp(s - jnp.max(s, axis=-1, keepdims=True))
        p = e / jnp.sum(e, axis=-1, keepdims=True)
        o = _dot(p.astype(BF16), v128)
        chunks.append(jnp.where(lane < HEAD_DIM, o[:rows], o[rows:2 * rows]))
    return chunks


def _memattn_kernel(qm_ref, mkv_ref, o_ref, *, tq):
    qm = [qm_ref[0, :, h * LANES:(h + 1) * LANES] for h in range(MEM_HEADS)]
    for j, chunk in enumerate(_mem_attend(qm, mkv_ref[0], tq)):
        o_ref[0, :, j * LANES:(j + 1) * LANES] = chunk


def _memattn_call(qm3, mkv3, *, tq):
    b, t, _ = qm3.shape
    m = mkv3.shape[1]
    return pl.pallas_call(
        functools.partial(_memattn_kernel, tq=tq),
        grid=(b, t // tq),
        in_specs=[pl.BlockSpec((1, tq, QM_PAD), lambda bi, i: (bi, i, 0)),
                  pl.BlockSpec((1, m, 2 * MEM_WIDTH), lambda bi, i: (bi, 0, 0))],
        out_specs=pl.BlockSpec((1, tq, MEM_WIDTH), lambda bi, i: (bi, i, 0)),
        out_shape=jax.ShapeDtypeStruct((b, t, MEM_WIDTH), F32),
        compiler_params=_cparams("parallel", "parallel"),
        name="memattn",
    )(qm3, mkv3)


def _pad_rows(rows_list):
    x = jnp.concatenate(rows_list, axis=0)
    return jnp.concatenate([x, jnp.zeros((8 - x.shape[0], x.shape[1]), x.dtype)], axis=0)


def _sample_attn1_kernel(q_ref, kc_ref, win_ref, wnew_ref, mkv_ref, qm_ref, pool_ref,
                         oc_ref, ow_ref, om_ref, score_ref, *, t_pos, n_win):
    q = q_ref[0].astype(F32)
    n_cmp = kc_ref.shape[1]
    kc128 = kc_ref[0, :, :LANES]
    vc128 = kc_ref[0, :, LANES:]
    kw_t = win_ref[0, 0].astype(BF16)
    vw_t = win_ref[0, 1].astype(BF16)
    wnew = wnew_ref[0]
    kw_new = wnew[:, :LANES].astype(BF16).astype(F32)
    vw_new = wnew[:, LANES:].astype(BF16).astype(F32)
    kc_end = lax.broadcasted_iota(I32, (1, n_cmp), 1) * CMP_STRIDE + (CMP_BLOCK - 1)
    mask_c = kc_end <= t_pos
    kw_pos = t_pos - n_win + lax.broadcasted_iota(I32, (1, n_win), 1)
    mask_w = (t_pos - kw_pos >= 0) & (t_pos - kw_pos < WINDOW) & (kw_pos >= 0)
    n_slot = pool_ref.shape[1]
    blk = lax.broadcasted_iota(I32, (1, n_slot), 1)
    cur = t_pos // SEL_BLOCK
    valid = blk * SEL_BLOCK <= t_pos
    forced = (blk == 0) | (blk == cur) | (blk == cur - 1)
    for g in range(NSA_KV_HEADS):
        qg_f = _pad_rows([q[:, (g * NSA_REP + r) * LANES:(g * NSA_REP + r + 1) * LANES] for r in range(NSA_REP)])
        qg = qg_f.astype(BF16)
        p_c = _masked_softmax(_dot_nt(qg, kc128), mask_c)
        oc_ref[0, g * NSA_REP:(g + 1) * NSA_REP, :] = _dot(p_c.astype(BF16), vc128)[:NSA_REP]
        imp = jnp.sum(p_c[:NSA_REP], axis=0, keepdims=True)
        imp8 = jnp.broadcast_to(imp, (8, n_cmp))
        imp_b = sum(_dot(piece, pool_ref[...]) for piece in _split3(imp8))[0:1]
        score_ref[0, g:g + 1, :] = jnp.where(valid, imp_b + jnp.where(forced, FORCE_BONUS, 0.0), -jnp.inf)
        s_old = jnp.where(mask_w, _dot(qg, kw_t), NEG)
        s_new = jnp.sum(qg_f * kw_new, axis=-1, keepdims=True)
        m_w = jnp.maximum(jnp.max(s_old, axis=-1, keepdims=True), s_new)
        e_old = jnp.where(mask_w, jnp.exp(s_old - m_w), 0.0)
        e_new = jnp.exp(s_new - m_w)
        den = jnp.sum(e_old, axis=-1, keepdims=True) + e_new
        o_win = (_dot_nt(e_old.astype(BF16), vw_t) + e_new.astype(BF16).astype(F32) * vw_new) / den
        ow_ref[0, g * NSA_REP:(g + 1) * NSA_REP, :] = o_win[:NSA_REP]
    qm = qm_ref[0].astype(F32)
    lane = lax.broadcasted_iota(I32, (1, LANES), 1)
    mk_t = mkv_ref[0, 0].astype(BF16)
    mv_t = mkv_ref[0, 1].astype(BF16)
    for j in range(MEM_HEADS // 2):
        q2 = _pad_rows([qm[:, h * LANES:(h + 1) * LANES] for h in (2 * j, 2 * j + 1)]).astype(BF16)
        s = _dot(q2, mk_t[j * LANES:(j + 1) * LANES, :])
        e = jnp.exp(s - jnp.max(s, axis=-1, keepdims=True))
        p = e / jnp.sum(e, axis=-1, keepdims=True)
        o = _dot_nt(p.astype(BF16), mv_t[j * LANES:(j + 1) * LANES, :])
        om_ref[0, :, j * LANES:(j + 1) * LANES] = jnp.where(lane < HEAD_DIM, o[0:1], o[1:2])


def _sample_attn1_call(q3, kcvc, win_t, wnew3, mkv_t, qm3, pool_mat, *, t_pos):
    b = q3.shape[0]
    n_cmp = kcvc.shape[1]
    n_win = win_t.shape[3]
    m = mkv_t.shape[3]
    per_b = lambda shape: pl.BlockSpec((1,) + shape, lambda bi: (bi,) + (0,) * len(shape))
    return pl.pallas_call(
        functools.partial(_sample_attn1_kernel, t_pos=t_pos, n_win=n_win),
        grid=(b,),
        in_specs=[per_b((1, Q_PAD)), per_b((n_cmp, 2 * LANES)), per_b((2, 2 * HEAD_DIM, n_win)),
                  per_b((1, 2 * LANES)), per_b((2, MEM_WIDTH, m)), per_b((1, QM_PAD)),
                  pl.BlockSpec(pool_mat.shape, lambda bi: (0, 0))],
        out_specs=[per_b((NSA_HEADS, LANES)), per_b((NSA_HEADS, LANES)), per_b((1, MEM_WIDTH)),
                   per_b((NSA_KV_HEADS, pool_mat.shape[1]))],
        out_shape=[jax.ShapeDtypeStruct((b, NSA_HEADS, LANES), F32), jax.ShapeDtypeStruct((b, NSA_HEADS, LANES), F32),
                   jax.ShapeDtypeStruct((b, 1, MEM_WIDTH), F32),
                   jax.ShapeDtypeStruct((b, NSA_KV_HEADS, pool_mat.shape[1]), F32)],
        compiler_params=_cparams("parallel"),
        name="sample_attn1",
    )(q3, kcvc, win_t, wnew3, mkv_t, qm3, pool_mat)


def _sample_topk_kernel(score_ref, idx_ref, *, n_top):
    score = score_ref[...]
    ids = lax.broadcasted_iota(I32, (1, score.shape[1]), 1).astype(F32)
    _, firsts = _topk_mask(score, ids, n_top, 1)
    lane = lax.broadcasted_iota(I32, (1, LANES), 1)
    idx = jnp.full((score.shape[0], LANES), -1, I32)
    for j, f in enumerate(firsts):
        idx = jnp.where(lane == j, f.astype(I32), idx)
    idx_ref[...] = idx


def _sample_topk_call(score2d, *, n_top):
    rows = score2d.shape[0]
    return pl.pallas_call(
        functools.partial(_sample_topk_kernel, n_top=n_top),
        out_shape=jax.ShapeDtypeStruct((rows, LANES), I32),
        compiler_params=pltpu.CompilerParams(vmem_limit_bytes=VMEM_LIMIT),
        name="sample_topk",
    )(score2d)


def _sample_sel_kernel(pt_ref, idx_ref, cache_ref, q_ref, knew_ref, gate_ref, oc_ref, ow_ref, o_ref,
                       buf, sem, *, t_pos, n_pages, n_top, page):
    b = pl.program_id(0)
    nb = pl.num_programs(0)
    blk_per_page = page // SEL_BLOCK
    n_past_blk = n_pages * blk_per_page
    idx_stride = NSA_KV_HEADS * LANES

    def blk_at(bb, g, j):
        return idx_ref[bb * idx_stride + g * LANES + j]

    def blk_copy(bb, slot, g, j, kv):
        blkc = jnp.clip(blk_at(bb, g, j), 0, n_past_blk - 1)
        pg = pt_ref[bb * n_pages + blkc // blk_per_page]
        return pltpu.make_async_copy(cache_ref.at[pg, 2 + kv], buf.at[slot, g, kv, j], sem.at[slot])

    def for_all(fn):
        for g in range(NSA_KV_HEADS):
            for j in range(n_top):
                for kv in range(2):
                    fn(g, j, kv)

    def issue(bb, slot):
        for_all(lambda g, j, kv: blk_copy(bb, slot, g, j, kv).start())

    @pl.when(b == 0)
    def _():
        issue(0, 0)

    @pl.when(b + 1 < nb)
    def _():
        issue(b + 1, (b + 1) % 2)

    slot = b % 2
    for_all(lambda g, j, kv: blk_copy(b, slot, g, j, kv).wait())

    q = q_ref[0].astype(F32)
    knew = knew_ref[0]
    k_new = knew[:, :LANES].astype(BF16).astype(F32)
    v_new = knew[:, LANES:].astype(BF16).astype(F32)
    n_keys = n_top * page
    key_lane = lax.broadcasted_iota(I32, (1, n_keys), 1)
    key_slot = key_lane // page
    key_row = key_lane % page
    cur_blk = t_pos // SEL_BLOCK
    o_s = []
    for g in range(NSA_KV_HEADS):
        qg = _pad_rows([q[:, (g * NSA_REP + r) * LANES:(g * NSA_REP + r + 1) * LANES] for r in range(NSA_REP)])
        blkvec = jnp.full((1, n_keys), -1, I32)
        has_cur = jnp.zeros((1, 1), jnp.bool_)
        for j in range(n_top):
            bj = blk_at(b, g, j)
            blkvec = jnp.where(key_slot == j, bj, blkvec)
            has_cur = has_cur | (bj == cur_blk)
        in_blk = key_row // SEL_BLOCK == blkvec % blk_per_page
        key_pos = (blkvec // blk_per_page) * page + key_row
        vis = (blkvec >= 0) & (blkvec < n_past_blk) & in_blk & (key_pos <= t_pos)
        kt = jnp.concatenate([buf[slot, g, 0, j] for j in range(n_top)], axis=1).astype(BF16)
        vt = jnp.concatenate([buf[slot, g, 1, j] for j in range(n_top)], axis=1).astype(BF16)
        s_past = jnp.where(vis, _dot(qg.astype(BF16), kt), NEG)
        s_new = jnp.where(has_cur, jnp.sum(qg * k_new, axis=-1, keepdims=True), NEG)
        m = jnp.maximum(jnp.max(s_past, axis=-1, keepdims=True), s_new)
        e_p = jnp.where(vis, jnp.exp(s_past - m), 0.0)
        e_n = jnp.where(has_cur, jnp.exp(s_new - m), 0.0)
        den = jnp.maximum(jnp.sum(e_p, axis=-1, keepdims=True) + e_n, 1e-30)
        num = _dot_nt(e_p.astype(BF16), vt) + e_n.astype(BF16).astype(F32) * v_new
        o_s.append(num / den)
    o_c = [jnp.concatenate([oc_ref[0, g * NSA_REP:(g + 1) * NSA_REP, :]] * 2, axis=0) for g in range(NSA_KV_HEADS)]
    o_w = [jnp.concatenate([ow_ref[0, g * NSA_REP:(g + 1) * NSA_REP, :]] * 2, axis=0) for g in range(NSA_KV_HEADS)]

    def store(j, chunk):
        o_ref[0, :, j * LANES:(j + 1) * LANES] = chunk

    _gate_and_pack(store, gate_ref[0], o_c, o_s, o_w, 1)


def _sample_sel_call(page_table, idx, cache_t, q3, knew3, gates3, oc, ow, *, t_pos, n_top):
    b, n_pages = page_table.shape
    page = cache_t.shape[3]
    per_b = lambda shape: pl.BlockSpec((1,) + shape, lambda bi, pt, ix: (bi, 0, 0))
    grid_spec = pltpu.PrefetchScalarGridSpec(
        num_scalar_prefetch=2,
        grid=(b,),
        in_specs=[pl.BlockSpec(memory_space=pl.ANY), per_b((1, Q_PAD)), per_b((1, 2 * LANES)), per_b((1, LANES)),
                  per_b((NSA_HEADS, LANES)), per_b((NSA_HEADS, LANES))],
        out_specs=per_b((1, NSA_WIDTH)),
        scratch_shapes=[pltpu.VMEM((2, NSA_KV_HEADS, 2, n_top, LANES, page), F32), pltpu.SemaphoreType.DMA((2,))],
    )
    return pl.pallas_call(
        functools.partial(_sample_sel_kernel, t_pos=t_pos, n_pages=n_pages, n_top=n_top, page=page),
        grid_spec=grid_spec,
        out_shape=jax.ShapeDtypeStruct((b, 1, NSA_WIDTH), F32),
        compiler_params=_cparams("arbitrary"),
        name="sample_sel",
    )(page_table.reshape(-1), idx.reshape(-1), cache_t, q3, knew3, gates3, oc, ow)


def _finish_kernel(yp_ref, on_ref, om_ref, x_ref, gm_ref, wo_ref, gf_ref, wr_ref, br_ref, tri_ref,
                   h_ref, hn_ref, route_ref, counts_ref, cnt_ref):
    gm = gm_ref[...]
    o1 = POOL_WIDTH
    o2 = o1 + NSA_WIDTH
    mixed = jnp.concatenate([_rms(yp_ref[...]) * gm[:, :o1], _rms(on_ref[...]) * gm[:, o1:o2],
                             _rms(om_ref[...]) * gm[:, o2:]], axis=-1)
    h = x_ref[...] + _dot(mixed.astype(BF16), wo_ref[...])
    h_ref[...] = h
    hn = _rms(h) * gf_ref[...]
    hn_ref[...] = hn
    logits = _dot(hn.astype(BF16), wr_ref[...]) + br_ref[...]
    lane = lax.broadcasted_iota(I32, (1, LANES), 1)
    lane_f = lane.astype(F32)
    is1 = lane < N_EXPERT_GROUPS
    m1 = jnp.max(jnp.where(is1, logits, -jnp.inf), axis=-1, keepdims=True)
    e1 = jnp.where(is1, jnp.exp(logits - m1), 0.0)
    p1 = e1 / jnp.sum(e1, axis=-1, keepdims=True)
    top1_p = jnp.max(p1, axis=-1, keepdims=True)
    grp = jnp.min(jnp.where((p1 == top1_p) & is1, lane_f, float(LANES)), axis=-1, keepdims=True)
    base = N_EXPERT_GROUPS + grp * EXPERTS_PER_GROUP
    in_g = (lane_f >= base) & (lane_f < base + EXPERTS_PER_GROUP)
    l2 = jnp.where(in_g, logits, -jnp.inf)
    v0 = jnp.max(l2, axis=-1, keepdims=True)
    i0 = jnp.min(jnp.where(l2 == v0, lane_f, float(LANES)), axis=-1, keepdims=True)
    l2b = jnp.where(lane_f == i0, -jnp.inf, l2)
    v1 = jnp.max(l2b, axis=-1, keepdims=True)
    i1 = jnp.min(jnp.where(l2b == v1, lane_f, float(LANES)), axis=-1, keepdims=True)
    ex = jnp.exp(v1 - v0)
    w0 = top1_p / (1.0 + ex)
    w1 = top1_p * ex / (1.0 + ex)
    ex0 = i0 - N_EXPERT_GROUPS
    ex1 = i1 - N_EXPERT_GROUPS
    is0 = lane_f == ex0
    is1e = lane_f == ex1
    oh0 = jnp.where(is0, 1.0, 0.0)
    oh1 = jnp.where(is1e, 1.0, 0.0)
    before0 = _dot(tri_ref[...], oh0.astype(BF16))
    before1 = _dot(tri_ref[...], oh1.astype(BF16))
    tot0 = jnp.sum(oh0, axis=0, keepdims=True)
    tot1 = jnp.sum(oh1, axis=0, keepdims=True)

    @pl.when(pl.program_id(0) == 0)
    def _():
        cnt_ref[...] = jnp.zeros(cnt_ref.shape, F32)

    seen = cnt_ref[...]
    rank0 = jnp.sum(jnp.where(is0, before0 + seen, 0.0), axis=-1, keepdims=True)
    rank1 = jnp.sum(jnp.where(is1e, before1 + seen + tot0, 0.0), axis=-1, keepdims=True)
    cnt_ref[...] = seen + tot0 + tot1
    counts_ref[...] = seen + tot0 + tot1
    route = jnp.where(lane == 0, ex0, jnp.where(lane == 1, ex1, jnp.where(lane == 2, w0, jnp.where(lane == 3, w1,
            jnp.where(lane == 4, rank0, jnp.where(lane == 5, rank1, 0.0))))))
    route_ref[...] = route


def _finish_call(yp, on, om, x2d, fw, *, tm):
    n, dm = x2d.shape
    full = lambda shape: pl.BlockSpec(shape, lambda i: (0,) * len(shape))
    rows = lambda w: pl.BlockSpec((tm, w), lambda i: (i, 0))
    tri = jnp.asarray(np.arange(tm)[None, :] < np.arange(tm)[:, None], BF16)
    return pl.pallas_call(
        _finish_kernel,
        grid=(n // tm,),
        in_specs=[rows(POOL_WIDTH), rows(NSA_WIDTH), rows(MEM_WIDTH), rows(dm), full((1, dm)), full((dm, dm)),
                  full((1, dm)), full((dm, LANES)), full((1, LANES)), full((tm, tm))],
        out_specs=[rows(dm), rows(dm), rows(LANES), full((1, LANES))],
        out_shape=[jax.ShapeDtypeStruct((n, dm), F32), jax.ShapeDtypeStruct((n, dm), F32),
                   jax.ShapeDtypeStruct((n, LANES), F32), jax.ShapeDtypeStruct((1, LANES), F32)],
        scratch_shapes=[pltpu.VMEM((1, LANES), F32)],
        compiler_params=_cparams("arbitrary"),
        name="finish",
    )(yp, on, om, x2d, fw["g_mix"], fw["w_out"], fw["g_ffn"], fw["w_r"], fw["b_r"], tri)


def _route_tables(route, counts, tm):
    n = route.shape[0]
    eid = route[:, 0:2].astype(I32)
    rank = route[:, 4:6].astype(I32)
    cnt = counts[0, :N_EXPERTS].astype(I32)
    tiles_per = jnp.maximum((cnt + tm - 1) // tm, 1)
    tile_end = jnp.cumsum(tiles_per)
    tile_start = tile_end - tiles_per
    experts = jnp.arange(N_EXPERTS, dtype=I32)
    start_of = jnp.sum(jnp.where(eid[:, :, None] == experts, tile_start, 0), axis=-1)
    pos = (start_of * tm + rank).reshape(-1)
    n_tiles = (2 * n) // tm + N_EXPERTS
    tj = jnp.arange(n_tiles, dtype=I32)
    tile_expert = jnp.minimum(jnp.sum((tj[:, None] >= tile_end[None, :]).astype(I32), axis=1), N_EXPERTS - 1)
    sel = tile_expert[:, None] == experts
    rows_left = jnp.sum(jnp.where(sel, cnt, 0), axis=-1) - (tj - jnp.sum(jnp.where(sel, tile_start, 0), axis=-1)) * tm
    tile_nvalid = jnp.where(tj < tile_end[-1], jnp.clip(rows_left, 0, tm), 0).astype(I32)
    return tile_expert, tile_nvalid, pos


def _row_wait_all(src_row, dst_row, sem, count):
    for _ in range(count):
        pltpu.make_async_copy(src_row, dst_row, sem).wait()


def _dispatch_kernel(pos_ref, nv_ref, hn_ref, xs_ref, stage, zbuf, sem, zsem, *, tm, tme, n_tiles):
    i = pl.program_id(0)
    nt = pl.num_programs(0)
    wait_slot = lambda s: _row_wait_all(stage.at[s, pl.ds(0, 1), :], xs_ref.at[pl.ds(0, 1), :], sem.at[s], 2 * tm)

    @pl.when(i == 0)
    def _():
        zbuf[...] = jnp.zeros(zbuf.shape, F32)

        def fill(j, c):
            @pl.when(nv_ref[j] < tme)
            def _():
                pltpu.make_async_copy(zbuf, xs_ref.at[pl.ds(pl.multiple_of(j * tme, tme), tme), :], zsem).start()
            return c

        def drain(j, c):
            @pl.when(nv_ref[j] < tme)
            def _():
                pltpu.make_async_copy(zbuf, xs_ref.at[pl.ds(0, tme), :], zsem).wait()
            return c

        lax.fori_loop(0, n_tiles, fill, 0)
        lax.fori_loop(0, n_tiles, drain, 0)

    for slot in range(2):
        @pl.when(i >= 1)
        def _():
            wait_slot(slot)

        stage[slot] = hn_ref[slot * tm:(slot + 1) * tm, :]
        base = (2 * i + slot) * (2 * tm)
        for r in range(tm):
            for k in range(2):
                dst = pos_ref[base + 2 * r + k]
                pltpu.make_async_copy(stage.at[slot, pl.ds(r, 1), :], xs_ref.at[pl.ds(dst, 1), :], sem.at[slot]).start()

    @pl.when(i == nt - 1)
    def _():
        wait_slot(0)
        wait_slot(1)


def _dispatch_call(pos, tile_nvalid, hn, *, tm, tme):
    n, dm = hn.shape
    n_tiles = tile_nvalid.shape[0]
    grid_spec = pltpu.PrefetchScalarGridSpec(
        num_scalar_prefetch=2,
        grid=(n // (2 * tm),),
        in_specs=[pl.BlockSpec((2 * tm, dm), lambda i, pos, nv: (i, 0))],
        out_specs=pl.BlockSpec(memory_space=pl.ANY),
        scratch_shapes=[pltpu.VMEM((2, tm, dm), F32), pltpu.VMEM((tme, dm), F32),
                        pltpu.SemaphoreType.DMA((2,)), pltpu.SemaphoreType.DMA(())],
    )
    return pl.pallas_call(
        functools.partial(_dispatch_kernel, tm=tm, tme=tme, n_tiles=n_tiles),
        grid_spec=grid_spec,
        out_shape=jax.ShapeDtypeStruct((n_tiles * tme, dm), F32),
        compiler_params=_cparams("arbitrary"),
        name="moe_dispatch",
    )(pos, tile_nvalid, hn)


def _expert_kernel(te_ref, nv_ref, x_ref, wg_ref, wu_ref, wd_ref, hns_ref, routes_ref, hs_ref,
                   y_ref, ysample_ref, wgb, wub, wdb, acc_s, *, tm):
    i = pl.program_id(0)
    prev = te_ref[jnp.maximum(i - 1, 0)]

    @pl.when(i == 0)
    def _():
        acc_s[...] = jnp.zeros(acc_s.shape, F32)

    first = (i == 0) | (te_ref[i] != prev)
    nv = nv_ref[i]

    @pl.when(first)
    def _():
        wgb[...] = wg_ref[0].astype(BF16)
        wub[...] = wu_ref[0].astype(BF16)
        wdb[...] = wd_ref[0].astype(BF16)
        route = routes_ref[...]
        ef = te_ref[i].astype(F32)
        comb = jnp.where(route[:, 0:1] == ef, route[:, 2:3], 0.0) + jnp.where(route[:, 1:2] == ef, route[:, 3:4], 0.0)
        x = jnp.concatenate([x_ref[...].astype(BF16), hns_ref[...].astype(BF16)], axis=0)
        hg = _dot(x, wgb[...])
        hu = _dot(x, wub[...])
        hmid = hg * jax.nn.sigmoid(hg) * hu
        hmid = jnp.concatenate([hmid[:tm], hmid[tm:] * comb], axis=0)
        y = _dot(hmid.astype(BF16), wdb[...])
        y_ref[...] = y[:tm]
        acc_s[...] += y[tm:]

    @pl.when(i == pl.num_programs(0) - 1)
    def _():
        ysample_ref[...] = hs_ref[...] + acc_s[...]

    @pl.when(jnp.logical_not(first) & (nv == 0))
    def _():
        y_ref[...] = jnp.zeros(y_ref.shape, F32)

    @pl.when(jnp.logical_not(first) & (nv > 0))
    def _():
        x = x_ref[...].astype(BF16)
        hg = _dot(x, wgb[...])
        hu = _dot(x, wub[...])
        hmid = hg * jax.nn.sigmoid(hg) * hu
        y_ref[...] = _dot(hmid.astype(BF16), wdb[...])


def _expert_call(tile_expert, tile_nvalid, xs, wg, wu, wd, hn_s, route_s, h_s, *, tm):
    n_tiles = tile_expert.shape[0]
    dm = xs.shape[1]
    ff = wg.shape[2]
    ns = hn_s.shape[0]
    full = lambda shape: pl.BlockSpec(shape, lambda i, te, nv: (0,) * len(shape))
    grid_spec = pltpu.PrefetchScalarGridSpec(
        num_scalar_prefetch=2,
        grid=(n_tiles,),
        in_specs=[pl.BlockSpec((tm, dm), lambda i, te, nv: (i, 0)),
                  pl.BlockSpec((1, dm, ff), lambda i, te, nv: (te[i], 0, 0)),
                  pl.BlockSpec((1, dm, ff), lambda i, te, nv: (te[i], 0, 0)),
                  pl.BlockSpec((1, ff, dm), lambda i, te, nv: (te[i], 0, 0)),
                  full((ns, dm)), full((ns, LANES)), full((ns, dm))],
        out_specs=[pl.BlockSpec((tm, dm), lambda i, te, nv: (i, 0)), full((ns, dm))],
        scratch_shapes=[pltpu.VMEM((dm, ff), BF16), pltpu.VMEM((dm, ff), BF16), pltpu.VMEM((ff, dm), BF16),
                        pltpu.VMEM((ns, dm), F32)],
    )
    return pl.pallas_call(
        functools.partial(_expert_kernel, tm=tm),
        grid_spec=grid_spec,
        out_shape=[jax.ShapeDtypeStruct(xs.shape, F32), jax.ShapeDtypeStruct((ns, dm), F32)],
        compiler_params=_cparams("arbitrary"),
        name="moe_experts",
    )(tile_expert, tile_nvalid, xs, wg, wu, wd, hn_s, route_s, h_s)


def _combine_kernel(pos_ref, ys_ref, h_ref, route_ref, o_ref, gbuf, sem, *, tm):
    i = pl.program_id(0)
    nt = pl.num_programs(0)

    def issue(half_tile, s):
        base = half_tile * (2 * tm)
        for r in range(tm):
            for k in range(2):
                src = pos_ref[base + 2 * r + k]
                pltpu.make_async_copy(ys_ref.at[pl.ds(src, 1), :], gbuf.at[s, k, pl.ds(r, 1), :], sem.at[s]).start()

    @pl.when(i == 0)
    def _():
        issue(0, 0)
        issue(1, 1)

    for s in range(2):
        _row_wait_all(ys_ref.at[pl.ds(0, 1), :], gbuf.at[s, 0, pl.ds(0, 1), :], sem.at[s], 2 * tm)
        rows = slice(s * tm, (s + 1) * tm)
        w0 = route_ref[rows, 2:3]
        w1 = route_ref[rows, 3:4]
        o_ref[rows, :] = h_ref[rows, :] + (w0 * gbuf[s, 0] + w1 * gbuf[s, 1])

        @pl.when(i + 1 < nt)
        def _():
            issue(2 * (i + 1) + s, s)


def _combine_call(pos, ys, h, route, *, tm):
    n, dm = h.shape
    grid_spec = pltpu.PrefetchScalarGridSpec(
        num_scalar_prefetch=1,
        grid=(n // (2 * tm),),
        in_specs=[pl.BlockSpec(memory_space=pl.ANY),
                  pl.BlockSpec((2 * tm, dm), lambda i, pos: (i, 0)),
                  pl.BlockSpec((2 * tm, LANES), lambda i, pos: (i, 0))],
        out_specs=pl.BlockSpec((2 * tm, dm), lambda i, pos: (i, 0)),
        scratch_shapes=[pltpu.VMEM((2, 2, tm, dm), F32), pltpu.SemaphoreType.DMA((2,))],
    )
    return pl.pallas_call(
        functools.partial(_combine_kernel, tm=tm),
        grid_spec=grid_spec,
        out_shape=jax.ShapeDtypeStruct((n, dm), F32),
        compiler_params=_cparams("arbitrary"),
        name="moe_combine",
    )(pos, ys, h, route)


def _prep_weights(l, g_attn, w_in, g_q, g_k, pe_cmp, w_cmp1, w_cmp2, w_pool, s_pool, g_mem, w_mem_kv, g_mq, g_mk,
                  g_mix, w_out, g_ffn, w_router1, b_router1, w_router2, b_router2):
    dm = w_in.shape[1]
    w = w_in[l]
    o1 = POOL_WIDTH
    o2 = o1 + NSA_WIDTH
    o3 = o2 + NSA_KV_WIDTH
    o4 = o3 + GATE_WIDTH
    wg_pad = jnp.pad(w[:, o3:o4], ((0, 0), (0, LANES - GATE_WIDTH)))
    w_packed = jnp.concatenate([w[:, :o3], w[:, o4:], wg_pad], axis=1).astype(BF16)
    half = ROPE_DIM // 2
    inv = jnp.power(ROPE_THETA, -jnp.arange(half, dtype=F32) * 2.0 / ROPE_DIM)
    d = np.arange(LANES) % HEAD_DIM
    inv_lane = jnp.where(jnp.asarray(d < ROPE_DIM), inv[jnp.asarray(d % half)], 0.0).reshape(1, LANES)
    seg = jnp.asarray((np.arange(LANES)[:, None] // HEAD_DIM == np.arange(LANES)[None, :] // HEAD_DIM), BF16)
    gk = jnp.concatenate([jnp.tile(g_k[l, br], 2) for br in range(N_BRANCH)]).reshape(1, N_BRANCH * LANES)
    proj = dict(g_attn=g_attn[l].reshape(1, dm), w_in=w_packed, gq=jnp.tile(g_q[l], Q_PAD // HEAD_DIM).reshape(1, Q_PAD),
                gk=gk, gmq=jnp.tile(g_mq[l], QM_PAD // HEAD_DIM).reshape(1, QM_PAD), inv=inv_lane, seg=seg)

    n_grp = len(POOL_WINDOWS)
    pg = POOL_WIDTH // n_grp
    w_bd = (jnp.asarray(np.eye(n_grp, dtype=np.float32))[:, None, :, None] * w_pool[l][:, :, None, :]
            ).reshape(POOL_WIDTH, POOL_WIDTH).astype(BF16)
    pool = dict(w=w_bd, s=s_pool[l].reshape(1, POOL_WIDTH))

    half_rows = CMP_STRIDE * HEAD_DIM
    w1 = jnp.concatenate([w_cmp1[l][:, :half_rows], w_cmp1[l][:, half_rows:]], axis=2).astype(BF16)
    cmp_w = dict(w=w1, pe=pe_cmp[l].reshape(2, 1, CMP_BLOCK * HEAD_DIM), w2=w_cmp2[l].astype(BF16))

    mem = dict(g=g_mem[l].reshape(1, dm), w=w_mem_kv[l].astype(BF16),
               gk=jnp.tile(g_mk[l], MEM_HEADS).reshape(1, MEM_WIDTH))
    w_r = jnp.concatenate([w_router1[l], w_router2[l].reshape(dm, N_EXPERTS)], axis=1)
    w_r = jnp.pad(w_r, ((0, 0), (0, LANES - w_r.shape[1]))).astype(BF16)
    b_r = jnp.concatenate([b_router1[l], b_router2[l].reshape(-1)])
    b_r = jnp.pad(b_r, (0, LANES - b_r.shape[0])).reshape(1, LANES)
    fin = dict(g_mix=g_mix[l].reshape(1, dm), w_out=w_out[l].astype(BF16), g_ffn=g_ffn[l].reshape(1, dm),
               w_r=w_r, b_r=b_r)
    return proj, pool, cmp_w, mem, fin


def _pick(n, prefs):
    for p in prefs:
        if n % p == 0:
            return p
    return n


def kernel(x_prompt, x_sample, cache_kv, cache_win, state_pool, cache_mem_kv, page_table, mem_prompt, g_attn, w_in, g_q, g_k, pe_cmp, w_cmp1, w_cmp2, w_pool, s_pool, g_mem, w_mem_kv, g_mq, g_mk, g_mix, w_out, g_ffn, w_router1, b_router1, w_router2, b_router2, w_gate, w_up, w_down):
    depth = w_in.shape[0]
    bp, t, dm = x_prompt.shape
    bs, ts, _ = x_sample.shape
    n_pages = page_table.shape[1]
    page = cache_kv.shape[2]
    past_len = n_pages * page
    n_win = cache_win.shape[2]
    ff = w_gate.shape[-1]
    tq, tk = 256, 512
    assert ts == 1 and n_win == WINDOW and page % SEL_BLOCK == 0 and t % tk == 0 and t >= WINDOW + tq
    n_sel = t // SEL_BLOCK
    assert min(SEL_TOPK, n_sel) >= 3 and n_sel <= LANES
    poolt = jnp.asarray(np.arange(t // CMP_STRIDE)[None, :] // CMP_PER_SEL == np.arange(n_sel)[:, None], BF16)
    key_blk = np.arange(t // tk)[:, None, None] * (tk // SEL_BLOCK) + np.arange(tk)[None, None, :] // SEL_BLOCK
    e_mat = jnp.asarray(np.arange(LANES)[None, :, None] == key_blk, BF16)
    n_cmp_s = past_len // CMP_STRIDE
    n_slot_s = -(-(past_len // SEL_BLOCK + 1) // LANES) * LANES
    pool_s = jnp.asarray(np.arange(n_cmp_s)[:, None] // CMP_PER_SEL == np.arange(n_slot_s)[None, :], BF16)

    hp, hs = x_prompt, x_sample
    outs = [[] for _ in range(7)]
    for l in range(depth):
        proj_w, pool_w, cmp_w, mem_w, fin_w = _prep_weights(
            l, g_attn, w_in, g_q, g_k, pe_cmp, w_cmp1, w_cmp2, w_pool, s_pool, g_mem, w_mem_kv, g_mq, g_mk,
            g_mix, w_out, g_ffn, w_router1, b_router1, w_router2, b_router2)
        wg = w_gate[l].reshape(N_EXPERTS, dm, ff)
        wu = w_up[l].reshape(N_EXPERTS, dm, ff)
        wd = w_down[l].reshape(N_EXPERTS, ff, dm)
        kw_cols = CMP_STRIDE * 2 * LANES

        n = bp * t
        xp2 = hp.reshape(n, dm)
        u, qb, rows4, rowsw, kvb, gates, qmb, rows4t, rowswt = _proj_call(
            xp2, proj_w, seq_len=t, pos0=0, tm=_pick(t, (512, 256, 128)), feature_major_rows4=True)
        u3 = u.reshape(bp, t, POOL_WIDTH)
        y_pool = _pool_call(u3, pool_w["w"], pool_w["s"], tp=_pick(t, (512, 256, 128)), pos0=0)
        n_slab = 2 * NSA_KV_HEADS
        chunk_w = CMP_STRIDE * HEAD_DIM
        kcvc = _compress_prompt_call(rows4.reshape(bp, t, 4 * LANES), jnp.zeros((bp, n_slab, chunk_w), F32), cmp_w)
        o_nsa = _nsa_prompt_call(qb.reshape(bp, t, Q_PAD), gates.reshape(bp, t, LANES), kcvc,
                                 kvb.reshape(bp, t, NSA_KV_WIDTH), poolt, e_mat, tq=tq, tk=tk)
        m_len = mem_prompt.shape[1]
        mkv, mkvb = _memkv_call(mem_prompt.reshape(bp * m_len, dm), mem_w["g"], mem_w["w"], mem_w["gk"],
                                proj_w["seg"], tm=_pick(bp * m_len, (256, 128)))
        o_mem = _memattn_call(qmb.reshape(bp, t, QM_PAD), mkvb.reshape(bp, m_len, 2 * MEM_WIDTH),
                              tq=_pick(t, (512, 256, 128)))
        tmf = _pick(n, (512, 256, 128))
        h_p, hn_p, route_p, counts_p = _finish_call(y_pool.reshape(n, POOL_WIDTH), o_nsa.reshape(n, NSA_WIDTH),
                                                    o_mem.reshape(n, MEM_WIDTH), xp2, fin_w, tm=tmf)
        tme = 256
        tmd = _pick(n, (256, 128))
        te, nv, pos = _route_tables(route_p, counts_p, tme)
        xs = _dispatch_call(pos, nv, hn_p, tm=tmd, tme=tme)

        xs2 = hs.reshape(bs, dm)
        u_s, qb_s, rows4_s, rowsw_s, _, gates_s, qmb_s = _proj_call(xs2, proj_w, seq_len=1, pos0=past_len, tm=bs)
        ext = jnp.concatenate([state_pool[l], u_s[:, None, :]], axis=1)
        y_pool_s = _pool_call(ext, pool_w["w"], pool_w["s"], tp=POOL_STATE + 1, pos0=past_len - POOL_STATE)[:, -1, :]
        cache_t = jnp.transpose(cache_kv[l], (0, 2, 3, 4, 1)).reshape(
            cache_kv.shape[1], N_KV_SLOTS, NSA_KV_HEADS * HEAD_DIM, page)
        tail_s = jnp.pad(rows4_s[:, :2 * LANES].reshape(bs, n_slab, HEAD_DIM), ((0, 0), (0, 0), (0, chunk_w - HEAD_DIM)))
        kcvc_s = _compress_sample_call(page_table, cache_t, tail_s, cmp_w)
        win_s = jnp.concatenate([cache_win[l].reshape(bs, n_win, 2 * LANES)[:, 1:], rowsw_s[:, None, :]], axis=1)
        win_t = jnp.transpose(cache_win[l], (0, 2, 3, 4, 1)).reshape(bs, 2, NSA_KV_HEADS * HEAD_DIM, n_win)
        mkv_t = jnp.transpose(cache_mem_kv[l], (0, 2, 3, 4, 1)).reshape(bs, 2, MEM_WIDTH, cache_mem_kv.shape[2])
        oc, ow, om_s, score_s = _sample_attn1_call(qb_s.reshape(bs, 1, Q_PAD), kcvc_s, win_t,
                                                   rowsw_s.reshape(bs, 1, 2 * LANES), mkv_t,
                                                   qmb_s.reshape(bs, 1, QM_PAD), pool_s, t_pos=past_len)
        n_top = min(SEL_TOPK, past_len // SEL_BLOCK + 1)
        idx = _sample_topk_call(score_s.reshape(bs * NSA_KV_HEADS, n_slot_s), n_top=n_top)
        o_nsa_s = _sample_sel_call(page_table, idx, cache_t, qb_s.reshape(bs, 1, Q_PAD),
                                   rows4_s[:, 2 * LANES:].reshape(bs, 1, 2 * LANES), gates_s.reshape(bs, 1, LANES),
                                   oc, ow, t_pos=past_len, n_top=n_top)
        h_s, hn_s, route_s, _ = _finish_call(y_pool_s, o_nsa_s.reshape(bs, NSA_WIDTH), om_s.reshape(bs, MEM_WIDTH),
                                          xs2, fin_w, tm=bs)

        ys, y_s = _expert_call(te, nv, xs, wg, wu, wd, hn_s, route_s, h_s, tm=tme)
        y_p = _combine_call(pos, ys, h_p, route_p, tm=tmd).reshape(bp, t, dm)
        y_s = y_s.reshape(bs, 1, dm)

        keep = min(WINDOW, t)
        outs[0].append(jnp.transpose(rows4t.reshape(bp, N_KV_SLOTS, NSA_KV_HEADS, HEAD_DIM, t), (0, 4, 1, 2, 3)))
        outs[1].append(rows4_s.reshape(bs, 1, N_KV_SLOTS, NSA_KV_HEADS, HEAD_DIM))
        outs[2].append(jnp.transpose(rowswt[:, :, t - keep:].reshape(bp, 2, NSA_KV_HEADS, HEAD_DIM, keep), (0, 4, 1, 2, 3)))
        outs[3].append(win_s.reshape(bs, n_win, 2, NSA_KV_HEADS, HEAD_DIM))
        outs[4].append(u3[:, t - POOL_STATE:])
        outs[5].append(ext[:, 1:])
        outs[6].append(mkv.reshape(bp, m_len, 2, MEM_HEADS, HEAD_DIM))
        hp, hs = y_p, y_s
    return (hp, hs) + tuple(jnp.stack(o) for o in outs)
```

```python
import functools

import numpy as np
import jax
import jax.numpy as jnp
from jax import lax
from jax.experimental import pallas as pl
from jax.experimental.pallas import tpu as pltpu

F32 = jnp.float32
BF16 = jnp.bfloat16
I32 = jnp.int32

HEAD_DIM = 64
POOL_WINDOWS = (2, 4, 8, 16)
POOL_STATE = max(POOL_WINDOWS) - 1
NSA_HEADS = 8
NSA_KV_HEADS = 2
NSA_REP = NSA_HEADS // NSA_KV_HEADS
N_BRANCH = 3
CMP_BLOCK = 32
CMP_STRIDE = 16
CMP_HIDDEN = 2 * HEAD_DIM
SEL_BLOCK = 64
SEL_TOPK = 16
CMP_PER_SEL = SEL_BLOCK // CMP_STRIDE
WINDOW = 512
FORCE_BONUS = 1000.0
MEM_HEADS = 4
ROPE_DIM = HEAD_DIM // 4
ROPE_THETA = 500000.0
N_EXPERT_GROUPS = 4
EXPERTS_PER_GROUP = 8
N_EXPERTS = N_EXPERT_GROUPS * EXPERTS_PER_GROUP
EPS = 1e-6
N_KV_SLOTS = 4

LANES = 128
POOL_WIDTH = 256
NSA_WIDTH = NSA_HEADS * HEAD_DIM
NSA_KV_WIDTH = N_BRANCH * 2 * NSA_KV_HEADS * HEAD_DIM
GATE_WIDTH = NSA_HEADS * N_BRANCH
MEM_WIDTH = MEM_HEADS * HEAD_DIM
Q_PAD = NSA_HEADS * LANES
QM_PAD = MEM_HEADS * LANES
C_U = 0
C_Q = C_U + POOL_WIDTH
C_KV = C_Q + NSA_WIDTH
C_QM = C_KV + NSA_KV_WIDTH
C_G = C_QM + MEM_WIDTH
C_END = C_G + LANES

NEG = -1e30
VMEM_LIMIT = 48 * 1024 * 1024

_NT = (((1,), (1,)), ((), ()))


def _cparams(*sem):
    return pltpu.CompilerParams(dimension_semantics=tuple(sem), vmem_limit_bytes=VMEM_LIMIT)


def _dot(a, b):
    return jnp.dot(a, b, preferred_element_type=F32)


def _dot_nt(a, b):
    return lax.dot_general(a, b, _NT, preferred_element_type=F32)


def _rms(x):
    return x * lax.rsqrt(jnp.mean(x * x, axis=-1, keepdims=True) + EPS)


def _masked_softmax(s, mask):
    sm = jnp.where(mask, s, NEG)
    m = jnp.max(sm, axis=-1, keepdims=True)
    e = jnp.where(mask, jnp.exp(sm - m), 0.0)
    return e / jnp.maximum(jnp.sum(e, axis=-1, keepdims=True), 1e-30)


def _split3(x):
    hi = x.astype(BF16)
    r1 = x - hi.astype(F32)
    mid = r1.astype(BF16)
    lo = (r1 - mid.astype(F32)).astype(BF16)
    return hi, mid, lo


def _proj_kernel(x_ref, ga_ref, w_ref, gq_ref, gk_ref, gmq_ref, inv_ref, seg_ref,
                 u_ref, q_ref, rows4_ref, rowsw_ref, kvb_ref, gates_ref, qm_ref, *rest,
                 tm, seq_len, pos0, consecutive):
    *maybe_t_refs, cos_in_tile, sin_in_tile = rest
    i = pl.program_id(0)
    a = _rms(x_ref[...]) * ga_ref[...]
    z = _dot(a.astype(BF16), w_ref[...])
    u_ref[...] = z[:, C_U:C_U + POOL_WIDTH]

    inv = inv_ref[...]

    @pl.when(i == 0)
    def _():
        j = lax.broadcasted_iota(I32, (tm, 1), 0).astype(F32) if consecutive else jnp.zeros((tm, 1), F32)
        cos_in_tile[...] = jnp.cos(j * inv)
        sin_in_tile[...] = jnp.sin(j * inv)

    first = (pos0 + (i * tm) % seq_len).astype(F32) * jnp.broadcast_to(inv, (8, LANES))
    cos_f, sin_f = jnp.cos(first)[0:1], jnp.sin(first)[0:1]
    cos = cos_f * cos_in_tile[...] - sin_f * sin_in_tile[...]
    sin = sin_f * cos_in_tile[...] + cos_f * sin_in_tile[...]
    d = lax.broadcasted_iota(I32, (1, LANES), 1) % HEAD_DIM
    half = ROPE_DIM // 2
    s_next = jnp.where(d < half, -sin, 0.0)
    s_prev = jnp.where((d >= half) & (d < ROPE_DIM), sin, 0.0)
    seg = seg_ref[...]

    def head_norm(xc, g):
        ssq = _dot((xc * xc).astype(BF16), seg)
        return xc * lax.rsqrt(ssq * (1.0 / HEAD_DIM) + EPS) * g

    def rope(xc):
        return xc * cos + pltpu.roll(xc, LANES - half, 1) * s_next + pltpu.roll(xc, half, 1) * s_prev

    lane_half = lax.broadcasted_iota(I32, (1, LANES), 1) // HEAD_DIM

    def single_head(col0, head, slot_half):
        tile = z[:, col0 + (head // 2) * LANES:col0 + (head // 2 + 1) * LANES]
        if head % 2 != slot_half:
            tile = pltpu.roll(tile, HEAD_DIM, 1)
        return jnp.where(lane_half == slot_half, tile, 0.0)

    scale = HEAD_DIM ** -0.5
    for c in range(NSA_HEADS):
        sl = slice(c * LANES, (c + 1) * LANES)
        qc = rope(head_norm(single_head(C_Q, c, c // NSA_REP), gq_ref[:, sl]))
        q_ref[:, sl] = (qc * scale).astype(BF16)
    for br in range(N_BRANCH):
        k0 = C_KV + br * 2 * LANES
        kn = rope(head_norm(z[:, k0:k0 + LANES], gk_ref[:, br * LANES:(br + 1) * LANES]))
        vv = z[:, k0 + LANES:k0 + 2 * LANES]
        kvb_ref[:, br * 2 * LANES:br * 2 * LANES + LANES] = kn.astype(BF16)
        kvb_ref[:, br * 2 * LANES + LANES:(br + 1) * 2 * LANES] = vv.astype(BF16)
        if br < 2:
            rows4_ref[:, br * 2 * LANES:br * 2 * LANES + LANES] = kn
            rows4_ref[:, br * 2 * LANES + LANES:(br + 1) * 2 * LANES] = vv
            if maybe_t_refs:
                maybe_t_refs[0][0, br * 2 * LANES:br * 2 * LANES + LANES, :] = kn.T
                maybe_t_refs[0][0, br * 2 * LANES + LANES:(br + 1) * 2 * LANES, :] = vv.T
        else:
            rowsw_ref[:, :LANES] = kn
            rowsw_ref[:, LANES:] = vv
            if maybe_t_refs:
                maybe_t_refs[1][0, :LANES, :] = kn.T
                maybe_t_refs[1][0, LANES:, :] = vv.T
    for c in range(MEM_HEADS):
        sl = slice(c * LANES, (c + 1) * LANES)
        qmc = head_norm(single_head(C_QM, c, c % 2), gmq_ref[:, sl])
        qm_ref[:, sl] = (qmc * scale).astype(BF16)
    gates_ref[...] = jax.nn.sigmoid(z[:, C_G:C_END])


def _proj_call(x2d, pw, *, seq_len, pos0, tm, feature_major_rows4=False):
    n, dm = x2d.shape
    full = lambda shape: pl.BlockSpec(shape, lambda i: (0,) * len(shape))
    rows = lambda w: pl.BlockSpec((tm, w), lambda i: (i, 0))
    outs = [(POOL_WIDTH, F32), (Q_PAD, BF16), (4 * LANES, F32), (2 * LANES, F32), (NSA_KV_WIDTH, BF16),
            (LANES, F32), (QM_PAD, BF16)]
    out_specs = [rows(w) for w, _ in outs]
    out_shape = [jax.ShapeDtypeStruct((n, w), dt) for w, dt in outs]
    if feature_major_rows4:
        per_seq = seq_len // tm
        for width in (4 * LANES, 2 * LANES):
            out_specs.append(pl.BlockSpec((1, width, tm), lambda i: (i // per_seq, 0, i % per_seq)))
            out_shape.append(jax.ShapeDtypeStruct((n // seq_len, width, seq_len), F32))
    consecutive = seq_len % tm == 0
    assert consecutive or seq_len == 1
    return pl.pallas_call(
        functools.partial(_proj_kernel, tm=tm, seq_len=seq_len, pos0=pos0, consecutive=consecutive),
        grid=(n // tm,),
        in_specs=[rows(dm), full((1, dm)), full((dm, C_END)), full((1, Q_PAD)), full((1, N_BRANCH * LANES)),
                  full((1, QM_PAD)), full((1, LANES)), full((LANES, LANES))],
        out_specs=out_specs,
        out_shape=out_shape,
        scratch_shapes=[pltpu.VMEM((tm, LANES), F32), pltpu.VMEM((tm, LANES), F32)],
        compiler_params=_cparams("arbitrary"),
        name="proj",
    )(x2d, pw["g_attn"], pw["w_in"], pw["gq"], pw["gk"], pw["gmq"], pw["inv"], pw["seg"])


def _pool_kernel(u_ref, halo_ref, w_ref, s_ref, y_ref, *, tp, pos0):
    i = pl.program_id(1)
    u = u_ref[0]
    halo = halo_ref[0] * (i > 0).astype(F32)
    n_h = POOL_STATE + 1
    ext = jnp.concatenate([halo, u], axis=0)
    sums = {1: ext}
    w = 1
    while w < max(POOL_WINDOWS):
        sums[2 * w] = sums[w] + pltpu.roll(sums[w], w, 0)
        w *= 2
    pos = pos0 + i * tp + lax.broadcasted_iota(I32, (tp, 1), 0)
    lane_grp = lax.broadcasted_iota(I32, (1, POOL_WIDTH), 1) // (POOL_WIDTH // len(POOL_WINDOWS))
    mean = jnp.zeros((tp, POOL_WIDTH), F32)
    for gi, wdw in enumerate(POOL_WINDOWS):
        cnt = jnp.minimum(pos + 1, wdw).astype(F32)
        mean = jnp.where(lane_grp == gi, sums[wdw][n_h:] / cnt, mean)
    r = mean - u
    y_ref[0] = _dot(r.astype(BF16), w_ref[...]) * s_ref[...]


def _pool_call(u3, w_bd, s_pool, *, tp, pos0):
    b, t, c = u3.shape
    n_h = POOL_STATE + 1
    return pl.pallas_call(
        functools.partial(_pool_kernel, tp=tp, pos0=pos0),
        grid=(b, t // tp),
        in_specs=[pl.BlockSpec((1, tp, c), lambda bi, i: (bi, i, 0)),
                  pl.BlockSpec((1, n_h, c), lambda bi, i: (bi, jnp.maximum(i * (tp // n_h) - 1, 0), 0)),
                  pl.BlockSpec((c, c), lambda bi, i: (0, 0)),
                  pl.BlockSpec((1, c), lambda bi, i: (0, 0))],
        out_specs=pl.BlockSpec((1, tp, c), lambda bi, i: (bi, i, 0)),
        out_shape=jax.ShapeDtypeStruct((b, t, c), F32),
        compiler_params=_cparams("parallel", "parallel"),
        name="pool",
    )(u3, u3, w_bd, s_pool)


def _split_chunks(x, n_chunks):
    return jnp.transpose(x.reshape(n_chunks, CMP_STRIDE, LANES), (1, 0, 2))


def _compress_core(load_slab, tail_ref, pe_ref, w_ref, w2_ref, n):
    rowid = lax.broadcasted_iota(I32, (n, 1), 0)
    outs = []
    for c in range(2):
        w = w_ref[c]
        pe8 = jnp.broadcast_to(pe_ref[c], (8, 2 * CMP_STRIDE * HEAD_DIM)).astype(BF16)
        pe_first = _dot(pe8[:, :CMP_STRIDE * HEAD_DIM], w)[0:1, :CMP_HIDDEN]
        pe_second = _dot(pe8[:, CMP_STRIDE * HEAD_DIM:], w)[0:1, CMP_HIDDEN:]
        pe_const = pe_first + pe_second
        lane = lax.broadcasted_iota(I32, (1, LANES), 1)
        per_group = [[], []]
        for q in range(CMP_STRIDE // 2):
            a, b = load_slab(c, 2 * q), load_slab(c, 2 * q + 1)
            per_group[0].append(jnp.where(lane < HEAD_DIM, a, pltpu.roll(b, HEAD_DIM, 1)))
            per_group[1].append(jnp.where(lane < HEAD_DIM, pltpu.roll(a, HEAD_DIM, 1), b))
        x2 = jnp.concatenate([jnp.concatenate(pg, axis=1) for pg in per_group], axis=0).astype(BF16)
        z = _dot(x2, w)
        hs = []
        for g in range(NSA_KV_HEADS):
            tail8 = jnp.broadcast_to(tail_ref[0, 2 * c + g:2 * c + g + 1, :], (8, CMP_STRIDE * HEAD_DIM)).astype(BF16)
            second_tail = _dot(tail8, w)[0:1, CMP_HIDDEN:]
            first = z[g * n:(g + 1) * n, :CMP_HIDDEN]
            second = z[g * n:(g + 1) * n, CMP_HIDDEN:]
            shifted = pltpu.roll(second, n - 1, 0)
            h = first + jnp.where(rowid == n - 1, second_tail, shifted) + pe_const
            hs.append(jax.nn.gelu(h, approximate=True))
        o = _dot(jnp.concatenate(hs, axis=0).astype(BF16), w2_ref[c])
        outs += [o[:n], o[n:]]
    return jnp.concatenate(outs, axis=1)


def _compress_prompt_kernel(k_ref, v_ref, tail_ref, pe_ref, w_ref, w2_ref, out_ref, *, n):
    slabs = [_split_chunks(ref[0], n) for ref in (k_ref, v_ref)]
    out_ref[0] = _compress_core(lambda c, r: slabs[c][r], tail_ref, pe_ref, w_ref, w2_ref, n).astype(BF16)


def _compress_prompt_call(rows4_3d, tail, cw):
    b, t, _ = rows4_3d.shape
    n = t // CMP_STRIDE
    full = lambda a: pl.BlockSpec(a.shape, lambda bi: (0,) * a.ndim)
    return pl.pallas_call(
        functools.partial(_compress_prompt_kernel, n=n),
        grid=(b,),
        in_specs=[pl.BlockSpec((1, t, LANES), lambda bi: (bi, 0, 0)),
                  pl.BlockSpec((1, t, LANES), lambda bi: (bi, 0, 1)),
                  pl.BlockSpec((1,) + tail.shape[1:], lambda bi: (bi, 0, 0)),
                  full(cw["pe"]), full(cw["w"]), full(cw["w2"])],
        out_specs=pl.BlockSpec((1, n, 2 * LANES), lambda bi: (bi, 0, 0)),
        out_shape=jax.ShapeDtypeStruct((b, n, 2 * LANES), BF16),
        compiler_params=_cparams("parallel"),
        name="compress_prompt",
    )(rows4_3d, rows4_3d, tail, cw["pe"], cw["w"], cw["w2"])


def _compress_sample_kernel(pt_ref, cache_ref, tail_ref, pe_ref, w_ref, w2_ref, out_ref,
                            buf, slabs, sem, *, n, n_pages, page):
    b = pl.program_id(0)
    nb = pl.num_programs(0)

    def page_copy(bb, slot, p, c):
        return pltpu.make_async_copy(cache_ref.at[pt_ref[bb * n_pages + p], c], buf.at[slot, c, p], sem.at[slot])

    def for_all(fn):
        for p in range(n_pages):
            for c in range(2):
                fn(p, c)

    def issue(bb, slot):
        for_all(lambda p, c: page_copy(bb, slot, p, c).start())

    @pl.when(b == 0)
    def _():
        issue(0, 0)

    @pl.when(b + 1 < nb)
    def _():
        issue(b + 1, (b + 1) % 2)

    slot = b % 2
    for_all(lambda p, c: page_copy(b, slot, p, c).wait())

    unroll = next(u for u in (8, 4, 2, 1) if n_pages % u == 0)

    cpp = page // CMP_STRIDE

    def to_slabs(pp, carry):
        for k in range(unroll):
            p = pp * unroll + k
            for c in range(2):
                by_r = _split_chunks(buf[slot, c, p].T, cpp)
                for r in range(CMP_STRIDE):
                    slabs[c, r, pl.ds(pl.multiple_of(p * cpp, cpp), cpp), :] = by_r[r]
        return carry

    lax.fori_loop(0, n_pages // unroll, to_slabs, 0)
    out_ref[0] = _compress_core(lambda c, r: slabs[c, r], tail_ref, pe_ref, w_ref, w2_ref, n).astype(BF16)


def _compress_sample_call(page_table, cache_t, tail, cw):
    b, n_pages = page_table.shape
    page = cache_t.shape[3]
    n = n_pages * page // CMP_STRIDE
    full = lambda a: pl.BlockSpec(a.shape, lambda bi, pt: (0,) * a.ndim)
    grid_spec = pltpu.PrefetchScalarGridSpec(
        num_scalar_prefetch=1,
        grid=(b,),
        in_specs=[pl.BlockSpec(memory_space=pl.ANY),
                  pl.BlockSpec((1,) + tail.shape[1:], lambda bi, pt: (bi, 0, 0)),
                  full(cw["pe"]), full(cw["w"]), full(cw["w2"])],
        out_specs=pl.BlockSpec((1, n, 2 * LANES), lambda bi, pt: (bi, 0, 0)),
        scratch_shapes=[pltpu.VMEM((2, 2, n_pages, LANES, page), F32), pltpu.VMEM((2, CMP_STRIDE, n, LANES), F32),
                        pltpu.SemaphoreType.DMA((2,))],
    )
    return pl.pallas_call(
        functools.partial(_compress_sample_kernel, n=n, n_pages=n_pages, page=page),
        grid_spec=grid_spec,
        out_shape=jax.ShapeDtypeStruct((b, n, 2 * LANES), BF16),
        compiler_params=_cparams("arbitrary"),
        name="compress_sample",
    )(page_table.reshape(-1), cache_t, tail, cw["pe"], cw["w"], cw["w2"])


def _topk_mask(score, ids, n_top, axis):
    sel = jnp.zeros(score.shape, jnp.bool_)
    work = score
    firsts = []
    big = float(score.shape[axis])
    for _ in range(n_top):
        m = jnp.max(work, axis=axis, keepdims=True)
        first = jnp.min(jnp.where(work == m, ids, big), axis=axis, keepdims=True)
        pick = (ids == first) & (m > -jnp.inf)
        sel = sel | pick
        work = jnp.where(pick, -jnp.inf, work)
        firsts.append(jnp.where(m > -jnp.inf, first, -1.0))
    return sel, firsts


def _gate_and_pack(o_ref_store, gates, o_c, o_s, o_w, rows):
    lane = lax.broadcasted_iota(I32, (1, LANES), 1)
    heads = []
    for h in range(NSA_HEADS):
        g, r = divmod(h, NSA_REP)
        rs = slice(r * rows, (r + 1) * rows)
        gc, gs, gw = (gates[:, N_BRANCH * h + k:N_BRANCH * h + k + 1] for k in range(N_BRANCH))
        heads.append(gc * o_c[g][rs] + gs * o_s[g][rs] + gw * o_w[g][rs])
    for j in range(NSA_HEADS // 2):
        a, b = heads[2 * j], heads[2 * j + 1]
        if (2 * j) // NSA_REP == 0:
            chunk = jnp.where(lane < HEAD_DIM, a, pltpu.roll(b, HEAD_DIM, 1))
        else:
            chunk = jnp.where(lane < HEAD_DIM, pltpu.roll(a, HEAD_DIM, 1), b)
        o_ref_store(j, chunk)


def _nsa_prompt_kernel(q_ref, gate_ref, kc_ref, kv_ref, poolt_ref, e_ref, o_ref, acc_ref, *, tq, tk, t_len):
    i = pl.program_id(1)
    s0 = i * tq
    rows4 = NSA_REP * tq
    tpos = s0 + lax.broadcasted_iota(I32, (tq, 1), 0)
    rep = lambda x: jnp.concatenate([x] * NSA_REP, axis=0)
    n_cmp = kc_ref.shape[1]
    kc128 = kc_ref[0, :, :LANES]
    vc128 = kc_ref[0, :, LANES:]
    kc_end = lax.broadcasted_iota(I32, (1, n_cmp), 1) * CMP_STRIDE + (CMP_BLOCK - 1)
    bias_c = rep(jnp.where(kc_end <= tpos, 0.0, NEG))
    any_c = rep((tpos >= CMP_BLOCK - 1).astype(F32))
    n_sel = t_len // SEL_BLOCK
    blk = lax.broadcasted_iota(I32, (n_sel, 1), 0)
    blk_f = blk.astype(F32)
    tq_lane = s0 + lax.broadcasted_iota(I32, (1, tq), 1)
    cur = tq_lane // SEL_BLOCK
    valid = blk * SEL_BLOCK <= tq_lane
    forced = (blk == 0) | (blk == cur) | (blk == cur - 1)
    sub = min(tq, LANES)
    w_len = WINDOW + sub
    w_starts, w_biases = [], []
    for j in range(tq // sub):
        w_start = pl.multiple_of(jnp.maximum(s0 + j * sub - WINDOW, 0), LANES)
        dist = tpos[j * sub:(j + 1) * sub] - (w_start + lax.broadcasted_iota(I32, (1, w_len), 1))
        w_starts.append(w_start)
        w_biases.append(rep(jnp.where((dist >= 0) & (dist < WINDOW), 0.0, NEG)))
    c_diag = s0 // tk
    kpos_d = c_diag * tk + lax.broadcasted_iota(I32, (1, tk), 1)
    bias_causal = jnp.where(kpos_d <= tpos, 0.0, NEG)

    o_c, o_s, o_w, qgs, selbs = [], [], [], [], []
    for g in range(NSA_KV_HEADS):
        qg = jnp.concatenate([q_ref[0, :, (g * NSA_REP + r) * LANES:(g * NSA_REP + r + 1) * LANES]
                              for r in range(NSA_REP)], axis=0)
        qgs.append(qg)
        s_c = _dot_nt(qg, kc128) + bias_c
        e_c = jnp.exp(s_c - jnp.max(s_c, axis=-1, keepdims=True))
        p_c = e_c * (any_c / jnp.sum(e_c, axis=-1, keepdims=True))
        o_c.append(_dot(p_c.astype(BF16), vc128))
        imp = p_c[0:tq]
        for r in range(1, NSA_REP):
            imp = imp + p_c[r * tq:(r + 1) * tq]
        imp_t = sum(_dot_nt(poolt_ref[...], piece) for piece in _split3(imp))
        score = jnp.where(valid, imp_t + jnp.where(forced, FORCE_BONUS, 0.0), -jnp.inf)
        sel_t, _ = _topk_mask(score, blk_f, min(SEL_TOPK, n_sel), 0)
        selb_t = jnp.concatenate([jnp.where(sel_t, 0.0, NEG), jnp.full(((-n_sel) % LANES, tq), NEG, F32)], axis=0)
        selbs.append(selb_t.T.astype(BF16))

        slabs = [[None] * (tq // sub) for _ in range(NSA_REP)]
        for j, (w_start, bias_w) in enumerate(zip(w_starts, w_biases)):
            q_sub = jnp.concatenate([qg[r * tq + j * sub:r * tq + (j + 1) * sub] for r in range(NSA_REP)], axis=0)
            kw = kv_ref[0, pl.ds(w_start, w_len), 4 * LANES:5 * LANES]
            vw = kv_ref[0, pl.ds(w_start, w_len), 5 * LANES:6 * LANES]
            s_w = _dot_nt(q_sub, kw) + bias_w
            e_w = jnp.exp(s_w - jnp.max(s_w, axis=-1, keepdims=True))
            o_sub = _dot(e_w.astype(BF16), vw) * (1.0 / jnp.sum(e_w, axis=-1, keepdims=True))
            for r in range(NSA_REP):
                slabs[r][j] = o_sub[r * sub:(r + 1) * sub]
        o_w.append(jnp.concatenate([s for per_head in slabs for s in per_head], axis=0))

    acc_ref[...] = jnp.zeros(acc_ref.shape, F32)

    q_all = jnp.concatenate(qgs, axis=0)
    selb_all = jnp.concatenate(selbs, axis=0)
    rows8 = NSA_KV_HEADS * rows4

    def chunk_step(c, carry, extra_bias):
        m, l = carry
        k0 = pl.multiple_of(c * tk, tk)
        ks = kv_ref[0, pl.ds(k0, tk), 2 * LANES:3 * LANES]
        vs = kv_ref[0, pl.ds(k0, tk), 3 * LANES:4 * LANES]
        bias = _dot(selb_all, e_ref[c])
        if extra_bias is not None:
            bias = bias + jnp.concatenate([extra_bias] * NSA_KV_HEADS, axis=0)
        bias_rows = jnp.concatenate([rep(bias[g * tq:(g + 1) * tq]) for g in range(NSA_KV_HEADS)], axis=0)
        s = _dot_nt(q_all, ks) + bias_rows
        m_new = jnp.maximum(m, jnp.max(s, axis=-1, keepdims=True))
        alpha = jnp.exp(m - m_new)
        p = jnp.exp(s - m_new)
        l_new = alpha * l + jnp.sum(p, axis=-1, keepdims=True)
        acc_ref[...] = alpha * acc_ref[...] + _dot(p.astype(BF16), vs)
        return m_new, l_new

    init = (jnp.full((rows8, 1), NEG, F32), jnp.zeros((rows8, 1), F32))
    carry = lax.fori_loop(0, c_diag, lambda c, cr: chunk_step(c, cr, None), init)
    _, l_fin = chunk_step(c_diag, carry, bias_causal)
    o_all = acc_ref[...] * (1.0 / jnp.maximum(l_fin, 1e-30))
    for g in range(NSA_KV_HEADS):
        o_s.append(o_all[g * rows4:(g + 1) * rows4])

    def store(j, chunk):
        o_ref[0, :, j * LANES:(j + 1) * LANES] = chunk

    _gate_and_pack(store, gate_ref[0], o_c, o_s, o_w, tq)


def _nsa_prompt_call(q3, gates3, kcvc, kvb3, poolt, e_mat, *, tq, tk):
    b, t, _ = q3.shape
    n_cmp = kcvc.shape[1]
    return pl.pallas_call(
        functools.partial(_nsa_prompt_kernel, tq=tq, tk=tk, t_len=t),
        grid=(b, t // tq),
        in_specs=[pl.BlockSpec((1, tq, Q_PAD), lambda bi, i: (bi, i, 0)),
                  pl.BlockSpec((1, tq, LANES), lambda bi, i: (bi, i, 0)),
                  pl.BlockSpec((1, n_cmp, 2 * LANES), lambda bi, i: (bi, 0, 0)),
                  pl.BlockSpec((1, t, NSA_KV_WIDTH), lambda bi, i: (bi, 0, 0)),
                  pl.BlockSpec(poolt.shape, lambda bi, i: (0, 0)),
                  pl.BlockSpec(e_mat.shape, lambda bi, i: (0, 0, 0))],
        out_specs=pl.BlockSpec((1, tq, NSA_WIDTH), lambda bi, i: (bi, i, 0)),
        out_shape=jax.ShapeDtypeStruct((b, t, NSA_WIDTH), F32),
        scratch_shapes=[pltpu.VMEM((NSA_KV_HEADS * NSA_REP * tq, LANES), F32)],
        compiler_params=_cparams("parallel", "arbitrary"),
        name="nsa_prompt",
    )(q3, gates3, kcvc, kvb3, poolt, e_mat)


def _memkv_kernel(x_ref, g_ref, w_ref, gk_ref, seg_ref, o_ref, ob_ref):
    a = _rms(x_ref[...]) * g_ref[...]
    z = _dot(a.astype(BF16), w_ref[...])
    for c in range(MEM_WIDTH // LANES):
        sl = slice(c * LANES, (c + 1) * LANES)
        xc = z[:, sl]
        ssq = _dot((xc * xc).astype(BF16), seg_ref[...])
        kn = xc * lax.rsqrt(ssq * (1.0 / HEAD_DIM) + EPS) * gk_ref[:, sl]
        o_ref[:, sl] = kn
        ob_ref[:, sl] = kn.astype(BF16)
    o_ref[:, MEM_WIDTH:] = z[:, MEM_WIDTH:]
    ob_ref[:, MEM_WIDTH:] = z[:, MEM_WIDTH:].astype(BF16)


def _memkv_call(mem2d, g_mem, w_mem, gmk, seg, *, tm):
    n, dm = mem2d.shape
    full = lambda shape: pl.BlockSpec(shape, lambda i: (0,) * len(shape))
    return pl.pallas_call(
        _memkv_kernel,
        grid=(n // tm,),
        in_specs=[pl.BlockSpec((tm, dm), lambda i: (i, 0)), full((1, dm)), full((dm, 2 * MEM_WIDTH)),
                  full((1, MEM_WIDTH)), full((LANES, LANES))],
        out_specs=[pl.BlockSpec((tm, 2 * MEM_WIDTH), lambda i: (i, 0))] * 2,
        out_shape=[jax.ShapeDtypeStruct((n, 2 * MEM_WIDTH), F32), jax.ShapeDtypeStruct((n, 2 * MEM_WIDTH), BF16)],
        compiler_params=_cparams("parallel"),
        name="memkv",
    )(mem2d, g_mem, w_mem, gmk, seg)


def _mem_attend(qm, mkv, rows):
    lane = lax.broadcasted_iota(I32, (1, LANES), 1)
    chunks = []
    for j in range(MEM_HEADS // 2):
        k128 = mkv[:, j * LANES:(j + 1) * LANES]
        v128 = mkv[:, MEM_WIDTH + j * LANES:MEM_WIDTH + (j + 1) * LANES]
        q2 = jnp.concatenate([qm[2 * j], qm[2 * j + 1]], axis=0)
        s = _dot_nt(q2, k128)
        e = jnp.exp(s - jnp.max(s, axis=-1, keepdims=True))
        p = e / jnp.sum(e, axis=-1, keepdims=True)
        o = _dot(p.astype(BF16), v128)
        chunks.append(jnp.where(lane < HEAD_DIM, o[:rows], o[rows:2 * rows]))
    return chunks


def _memattn_kernel(qm_ref, mkv_ref, o_ref, *, tq):
    qm = [qm_ref[0, :, h * LANES:(h + 1) * LANES] for h in range(MEM_HEADS)]
    for j, chunk in enumerate(_mem_attend(qm, mkv_ref[0], tq)):
        o_ref[0, :, j * LANES:(j + 1) * LANES] = chunk


def _memattn_call(qm3, mkv3, *, tq):
    b, t, _ = qm3.shape
    m = mkv3.shape[1]
    return pl.pallas_call(
        functools.partial(_memattn_kernel, tq=tq),
        grid=(b, t // tq),
        in_specs=[pl.BlockSpec((1, tq, QM_PAD), lambda bi, i: (bi, i, 0)),
                  pl.BlockSpec((1, m, 2 * MEM_WIDTH), lambda bi, i: (bi, 0, 0))],
        out_specs=pl.BlockSpec((1, tq, MEM_WIDTH), lambda bi, i: (bi, i, 0)),
        out_shape=jax.ShapeDtypeStruct((b, t, MEM_WIDTH), F32),
        compiler_params=_cparams("parallel", "parallel"),
        name="memattn",
    )(qm3, mkv3)


def _pad_rows(rows_list):
    x = jnp.concatenate(rows_list, axis=0)
    return jnp.concatenate([x, jnp.zeros((8 - x.shape[0], x.shape[1]), x.dtype)], axis=0)


def _sample_attn1_kernel(q_ref, kc_ref, win_ref, wnew_ref, mkv_ref, qm_ref, pool_ref,
                         oc_ref, ow_ref, om_ref, score_ref, *, t_pos, n_win):
    q = q_ref[0].astype(F32)
    n_cmp = kc_ref.shape[1]
    kc128 = kc_ref[0, :, :LANES]
    vc128 = kc_ref[0, :, LANES:]
    kw_t = win_ref[0, 0].astype(BF16)
    vw_t = win_ref[0, 1].astype(BF16)
    wnew = wnew_ref[0]
    kw_new = wnew[:, :LANES].astype(BF16).astype(F32)
    vw_new = wnew[:, LANES:].astype(BF16).astype(F32)
    kc_end = lax.broadcasted_iota(I32, (1, n_cmp), 1) * CMP_STRIDE + (CMP_BLOCK - 1)
    mask_c = kc_end <= t_pos
    kw_pos = t_pos - n_win + lax.broadcasted_iota(I32, (1, n_win), 1)
    mask_w = (t_pos - kw_pos >= 0) & (t_pos - kw_pos < WINDOW) & (kw_pos >= 0)
    n_slot = pool_ref.shape[1]
    blk = lax.broadcasted_iota(I32, (1, n_slot), 1)
    cur = t_pos // SEL_BLOCK
    valid = blk * SEL_BLOCK <= t_pos
    forced = (blk == 0) | (blk == cur) | (blk == cur - 1)
    for g in range(NSA_KV_HEADS):
        qg_f = _pad_rows([q[:, (g * NSA_REP + r) * LANES:(g * NSA_REP + r + 1) * LANES] for r in range(NSA_REP)])
        qg = qg_f.astype(BF16)
        p_c = _masked_softmax(_dot_nt(qg, kc128), mask_c)
        oc_ref[0, g * NSA_REP:(g + 1) * NSA_REP, :] = _dot(p_c.astype(BF16), vc128)[:NSA_REP]
        imp = jnp.sum(p_c[:NSA_REP], axis=0, keepdims=True)
        imp8 = jnp.broadcast_to(imp, (8, n_cmp))
        imp_b = sum(_dot(piece, pool_ref[...]) for piece in _split3(imp8))[0:1]
        score_ref[0, g:g + 1, :] = jnp.where(valid, imp_b + jnp.where(forced, FORCE_BONUS, 0.0), -jnp.inf)
        s_old = jnp.where(mask_w, _dot(qg, kw_t), NEG)
        s_new = jnp.sum(qg_f * kw_new, axis=-1, keepdims=True)
        m_w = jnp.maximum(jnp.max(s_old, axis=-1, keepdims=True), s_new)
        e_old = jnp.where(mask_w, jnp.exp(s_old - m_w), 0.0)
        e_new = jnp.exp(s_new - m_w)
        den = jnp.sum(e_old, axis=-1, keepdims=True) + e_new
        o_win = (_dot_nt(e_old.astype(BF16), vw_t) + e_new.astype(BF16).astype(F32) * vw_new) / den
        ow_ref[0, g * NSA_REP:(g + 1) * NSA_REP, :] = o_win[:NSA_REP]
    qm = qm_ref[0].astype(F32)
    lane = lax.broadcasted_iota(I32, (1, LANES), 1)
    mk_t = mkv_ref[0, 0].astype(BF16)
    mv_t = mkv_ref[0, 1].astype(BF16)
    for j in range(MEM_HEADS // 2):
        q2 = _pad_rows([qm[:, h * LANES:(h + 1) * LANES] for h in (2 * j, 2 * j + 1)]).astype(BF16)
        s = _dot(q2, mk_t[j * LANES:(j + 1) * LANES, :])
        e = jnp.exp(s - jnp.max(s, axis=-1, keepdims=True))
        p = e / jnp.sum(e, axis=-1, keepdims=True)
        o = _dot_nt(p.astype(BF16), mv_t[j * LANES:(j + 1) * LANES, :])
        om_ref[0, :, j * LANES:(j + 1) * LANES] = jnp.where(lane < HEAD_DIM, o[0:1], o[1:2])


def _sample_attn1_call(q3, kcvc, win_t, wnew3, mkv_t, qm3, pool_mat, *, t_pos):
    b = q3.shape[0]
    n_cmp = kcvc.shape[1]
    n_win = win_t.shape[3]
    m = mkv_t.shape[3]
    per_b = lambda shape: pl.BlockSpec((1,) + shape, lambda bi: (bi,) + (0,) * len(shape))
    return pl.pallas_call(
        functools.partial(_sample_attn1_kernel, t_pos=t_pos, n_win=n_win),
        grid=(b,),
        in_specs=[per_b((1, Q_PAD)), per_b((n_cmp, 2 * LANES)), per_b((2, 2 * HEAD_DIM, n_win)),
                  per_b((1, 2 * LANES)), per_b((2, MEM_WIDTH, m)), per_b((1, QM_PAD)),
                  pl.BlockSpec(pool_mat.shape, lambda bi: (0, 0))],
        out_specs=[per_b((NSA_HEADS, LANES)), per_b((NSA_HEADS, LANES)), per_b((1, MEM_WIDTH)),
                   per_b((NSA_KV_HEADS, pool_mat.shape[1]))],
        out_shape=[jax.ShapeDtypeStruct((b, NSA_HEADS, LANES), F32), jax.ShapeDtypeStruct((b, NSA_HEADS, LANES), F32),
                   jax.ShapeDtypeStruct((b, 1, MEM_WIDTH), F32),
                   jax.ShapeDtypeStruct((b, NSA_KV_HEADS, pool_mat.shape[1]), F32)],
        compiler_params=_cparams("parallel"),
        name="sample_attn1",
    )(q3, kcvc, win_t, wnew3, mkv_t, qm3, pool_mat)


def _sample_topk_kernel(score_ref, idx_ref, *, n_top):
    score = score_ref[...]
    ids = lax.broadcasted_iota(I32, (1, score.shape[1]), 1).astype(F32)
    _, firsts = _topk_mask(score, ids, n_top, 1)
    lane = lax.broadcasted_iota(I32, (1, LANES), 1)
    idx = jnp.full((score.shape[0], LANES), -1, I32)
    for j, f in enumerate(firsts):
        idx = jnp.where(lane == j, f.astype(I32), idx)
    idx_ref[...] = idx


def _sample_topk_call(score2d, *, n_top):
    rows = score2d.shape[0]
    return pl.pallas_call(
        functools.partial(_sample_topk_kernel, n_top=n_top),
        out_shape=jax.ShapeDtypeStruct((rows, LANES), I32),
        compiler_params=pltpu.CompilerParams(vmem_limit_bytes=VMEM_LIMIT),
        name="sample_topk",
    )(score2d)


def _sample_sel_kernel(pt_ref, idx_ref, cache_ref, q_ref, knew_ref, gate_ref, oc_ref, ow_ref, o_ref,
                       buf, sem, *, t_pos, n_pages, n_top, page):
    b = pl.program_id(0)
    nb = pl.num_programs(0)
    blk_per_page = page // SEL_BLOCK
    n_past_blk = n_pages * blk_per_page
    idx_stride = NSA_KV_HEADS * LANES

    def blk_at(bb, g, j):
        return idx_ref[bb * idx_stride + g * LANES + j]

    def blk_copy(bb, slot, g, j, kv):
        blkc = jnp.clip(blk_at(bb, g, j), 0, n_past_blk - 1)
        pg = pt_ref[bb * n_pages + blkc // blk_per_page]
        return pltpu.make_async_copy(cache_ref.at[pg, 2 + kv], buf.at[slot, g, kv, j], sem.at[slot])

    def for_all(fn):
        for g in range(NSA_KV_HEADS):
            for j in range(n_top):
                for kv in range(2):
                    fn(g, j, kv)

    def issue(bb, slot):
        for_all(lambda g, j, kv: blk_copy(bb, slot, g, j, kv).start())

    @pl.when(b == 0)
    def _():
        issue(0, 0)

    @pl.when(b + 1 < nb)
    def _():
        issue(b + 1, (b + 1) % 2)

    slot = b % 2
    for_all(lambda g, j, kv: blk_copy(b, slot, g, j, kv).wait())

    q = q_ref[0].astype(F32)
    knew = knew_ref[0]
    k_new = knew[:, :LANES].astype(BF16).astype(F32)
    v_new = knew[:, LANES:].astype(BF16).astype(F32)
    n_keys = n_top * page
    key_lane = lax.broadcasted_iota(I32, (1, n_keys), 1)
    key_slot = key_lane // page
    key_row = key_lane % page
    cur_blk = t_pos // SEL_BLOCK
    o_s = []
    for g in range(NSA_KV_HEADS):
        qg = _pad_rows([q[:, (g * NSA_REP + r) * LANES:(g * NSA_REP + r + 1) * LANES] for r in range(NSA_REP)])
        blkvec = jnp.full((1, n_keys), -1, I32)
        has_cur = jnp.zeros((1, 1), jnp.bool_)
        for j in range(n_top):
            bj = blk_at(b, g, j)
            blkvec = jnp.where(key_slot == j, bj, blkvec)
            has_cur = has_cur | (bj == cur_blk)
        in_blk = key_row // SEL_BLOCK == blkvec % blk_per_page
        key_pos = (blkvec // blk_per_page) * page + key_row
        vis = (blkvec >= 0) & (blkvec < n_past_blk) & in_blk & (key_pos <= t_pos)
        kt = jnp.concatenate([buf[slot, g, 0, j] for j in range(n_top)], axis=1).astype(BF16)
        vt = jnp.concatenate([buf[slot, g, 1, j] for j in range(n_top)], axis=1).astype(BF16)
        s_past = jnp.where(vis, _dot(qg.astype(BF16), kt), NEG)
        s_new = jnp.where(has_cur, jnp.sum(qg * k_new, axis=-1, keepdims=True), NEG)
        m = jnp.maximum(jnp.max(s_past, axis=-1, keepdims=True), s_new)
        e_p = jnp.where(vis, jnp.exp(s_past - m), 0.0)
        e_n = jnp.where(has_cur, jnp.exp(s_new - m), 0.0)
        den = jnp.maximum(jnp.sum(e_p, axis=-1, keepdims=True) + e_n, 1e-30)
        num = _dot_nt(e_p.astype(BF16), vt) + e_n.astype(BF16).astype(F32) * v_new
        o_s.append(num / den)
    o_c = [jnp.concatenate([oc_ref[0, g * NSA_REP:(g + 1) * NSA_REP, :]] * 2, axis=0) for g in range(NSA_KV_HEADS)]
    o_w = [jnp.concatenate([ow_ref[0, g * NSA_REP:(g + 1) * NSA_REP, :]] * 2, axis=0) for g in range(NSA_KV_HEADS)]

    def store(j, chunk):
        o_ref[0, :, j * LANES:(j + 1) * LANES] = chunk

    _gate_and_pack(store, gate_ref[0], o_c, o_s, o_w, 1)


def _sample_sel_call(page_table, idx, cache_t, q3, knew3, gates3, oc, ow, *, t_pos, n_top):
    b, n_pages = page_table.shape
    page = cache_t.shape[3]
    per_b = lambda shape: pl.BlockSpec((1,) + shape, lambda bi, pt, ix: (bi, 0, 0))
    grid_spec = pltpu.PrefetchScalarGridSpec(
        num_scalar_prefetch=2,
        grid=(b,),
        in_specs=[pl.BlockSpec(memory_space=pl.ANY), per_b((1, Q_PAD)), per_b((1, 2 * LANES)), per_b((1, LANES)),
                  per_b((NSA_HEADS, LANES)), per_b((NSA_HEADS, LANES))],
        out_specs=per_b((1, NSA_WIDTH)),
        scratch_shapes=[pltpu.VMEM((2, NSA_KV_HEADS, 2, n_top, LANES, page), F32), pltpu.SemaphoreType.DMA((2,))],
    )
    return pl.pallas_call(
        functools.partial(_sample_sel_kernel, t_pos=t_pos, n_pages=n_pages, n_top=n_top, page=page),
        grid_spec=grid_spec,
        out_shape=jax.ShapeDtypeStruct((b, 1, NSA_WIDTH), F32),
        compiler_params=_cparams("arbitrary"),
        name="sample_sel",
    )(page_table.reshape(-1), idx.reshape(-1), cache_t, q3, knew3, gates3, oc, ow)


def _finish_kernel(yp_ref, on_ref, om_ref, x_ref, gm_ref, wo_ref, gf_ref, wr_ref, br_ref, tri_ref,
                   h_ref, hn_ref, route_ref, counts_ref, cnt_ref):
    gm = gm_ref[...]
    o1 = POOL_WIDTH
    o2 = o1 + NSA_WIDTH
    mixed = jnp.concatenate([_rms(yp_ref[...]) * gm[:, :o1], _rms(on_ref[...]) * gm[:, o1:o2],
                             _rms(om_ref[...]) * gm[:, o2:]], axis=-1)
    h = x_ref[...] + _dot(mixed.astype(BF16), wo_ref[...])
    h_ref[...] = h
    hn = _rms(h) * gf_ref[...]
    hn_ref[...] = hn
    logits = _dot(hn.astype(BF16), wr_ref[...]) + br_ref[...]
    lane = lax.broadcasted_iota(I32, (1, LANES), 1)
    lane_f = lane.astype(F32)
    is1 = lane < N_EXPERT_GROUPS
    m1 = jnp.max(jnp.where(is1, logits, -jnp.inf), axis=-1, keepdims=True)
    e1 = jnp.where(is1, jnp.exp(logits - m1), 0.0)
    p1 = e1 / jnp.sum(e1, axis=-1, keepdims=True)
    top1_p = jnp.max(p1, axis=-1, keepdims=True)
    grp = jnp.min(jnp.where((p1 == top1_p) & is1, lane_f, float(LANES)), axis=-1, keepdims=True)
    base = N_EXPERT_GROUPS + grp * EXPERTS_PER_GROUP
    in_g = (lane_f >= base) & (lane_f < base + EXPERTS_PER_GROUP)
    l2 = jnp.where(in_g, logits, -jnp.inf)
    v0 = jnp.max(l2, axis=-1, keepdims=True)
    i0 = jnp.min(jnp.where(l2 == v0, lane_f, float(LANES)), axis=-1, keepdims=True)
    l2b = jnp.where(lane_f == i0, -jnp.inf, l2)
    v1 = jnp.max(l2b, axis=-1, keepdims=True)
    i1 = jnp.min(jnp.where(l2b == v1, lane_f, float(LANES)), axis=-1, keepdims=True)
    ex = jnp.exp(v1 - v0)
    w0 = top1_p / (1.0 + ex)
    w1 = top1_p * ex / (1.0 + ex)
    ex0 = i0 - N_EXPERT_GROUPS
    ex1 = i1 - N_EXPERT_GROUPS
    is0 = lane_f == ex0
    is1e = lane_f == ex1
    oh0 = jnp.where(is0, 1.0, 0.0)
    oh1 = jnp.where(is1e, 1.0, 0.0)
    before0 = _dot(tri_ref[...], oh0.astype(BF16))
    before1 = _dot(tri_ref[...], oh1.astype(BF16))
    tot0 = jnp.sum(oh0, axis=0, keepdims=True)
    tot1 = jnp.sum(oh1, axis=0, keepdims=True)

    @pl.when(pl.program_id(0) == 0)
    def _():
        cnt_ref[...] = jnp.zeros(cnt_ref.shape, F32)

    seen = cnt_ref[...]
    rank0 = jnp.sum(jnp.where(is0, before0 + seen, 0.0), axis=-1, keepdims=True)
    rank1 = jnp.sum(jnp.where(is1e, before1 + seen + tot0, 0.0), axis=-1, keepdims=True)
    cnt_ref[...] = seen + tot0 + tot1
    counts_ref[...] = seen + tot0 + tot1
    route = jnp.where(lane == 0, ex0, jnp.where(lane == 1, ex1, jnp.where(lane == 2, w0, jnp.where(lane == 3, w1,
            jnp.where(lane == 4, rank0, jnp.where(lane == 5, rank1, 0.0))))))
    route_ref[...] = route


def _finish_call(yp, on, om, x2d, fw, *, tm):
    n, dm = x2d.shape
    full = lambda shape: pl.BlockSpec(shape, lambda i: (0,) * len(shape))
    rows = lambda w: pl.BlockSpec((tm, w), lambda i: (i, 0))
    tri = jnp.asarray(np.arange(tm)[None, :] < np.arange(tm)[:, None], BF16)
    return pl.pallas_call(
        _finish_kernel,
        grid=(n // tm,),
        in_specs=[rows(POOL_WIDTH), rows(NSA_WIDTH), rows(MEM_WIDTH), rows(dm), full((1, dm)), full((dm, dm)),
                  full((1, dm)), full((dm, LANES)), full((1, LANES)), full((tm, tm))],
        out_specs=[rows(dm), rows(dm), rows(LANES), full((1, LANES))],
        out_shape=[jax.ShapeDtypeStruct((n, dm), F32), jax.ShapeDtypeStruct((n, dm), F32),
                   jax.ShapeDtypeStruct((n, LANES), F32), jax.ShapeDtypeStruct((1, LANES), F32)],
        scratch_shapes=[pltpu.VMEM((1, LANES), F32)],
        compiler_params=_cparams("arbitrary"),
        name="finish",
    )(yp, on, om, x2d, fw["g_mix"], fw["w_out"], fw["g_ffn"], fw["w_r"], fw["b_r"], tri)


def _route_tables(route, counts, tm):
    n = route.shape[0]
    eid = route[:, 0:2].astype(I32)
    rank = route[:, 4:6].astype(I32)
    cnt = counts[0, :N_EXPERTS].astype(I32)
    tiles_per = jnp.maximum((cnt + tm - 1) // tm, 1)
    tile_end = jnp.cumsum(tiles_per)
    tile_start = tile_end - tiles_per
    experts = jnp.arange(N_EXPERTS, dtype=I32)
    start_of = jnp.sum(jnp.where(eid[:, :, None] == experts, tile_start, 0), axis=-1)
    pos = (start_of * tm + rank).reshape(-1)
    n_tiles = (2 * n) // tm + N_EXPERTS
    tj = jnp.arange(n_tiles, dtype=I32)
    tile_expert = jnp.minimum(jnp.sum((tj[:, None] >= tile_end[None, :]).astype(I32), axis=1), N_EXPERTS - 1)
    sel = tile_expert[:, None] == experts
    rows_left = jnp.sum(jnp.where(sel, cnt, 0), axis=-1) - (tj - jnp.sum(jnp.where(sel, tile_start, 0), axis=-1)) * tm
    tile_nvalid = jnp.where(tj < tile_end[-1], jnp.clip(rows_left, 0, tm), 0).astype(I32)
    return tile_expert, tile_nvalid, pos


def _row_wait_all(src_row, dst_row, sem, count):
    for _ in range(count):
        pltpu.make_async_copy(src_row, dst_row, sem).wait()


def _dispatch_kernel(pos_ref, nv_ref, hn_ref, xs_ref, stage, zbuf, sem, zsem, *, tm, tme, n_tiles):
    i = pl.program_id(0)
    nt = pl.num_programs(0)
    wait_slot = lambda s: _row_wait_all(stage.at[s, pl.ds(0, 1), :], xs_ref.at[pl.ds(0, 1), :], sem.at[s], 2 * tm)

    @pl.when(i == 0)
    def _():
        zbuf[...] = jnp.zeros(zbuf.shape, F32)

        def fill(j, c):
            @pl.when(nv_ref[j] < tme)
            def _():
                pltpu.make_async_copy(zbuf, xs_ref.at[pl.ds(pl.multiple_of(j * tme, tme), tme), :], zsem).start()
            return c

        def drain(j, c):
            @pl.when(nv_ref[j] < tme)
            def _():
                pltpu.make_async_copy(zbuf, xs_ref.at[pl.ds(0, tme), :], zsem).wait()
            return c

        lax.fori_loop(0, n_tiles, fill, 0)
        lax.fori_loop(0, n_tiles, drain, 0)

    for slot in range(2):
        @pl.when(i >= 1)
        def _():
            wait_slot(slot)

        stage[slot] = hn_ref[slot * tm:(slot + 1) * tm, :]
        base = (2 * i + slot) * (2 * tm)
        for r in range(tm):
            for k in range(2):
                dst = pos_ref[base + 2 * r + k]
                pltpu.make_async_copy(stage.at[slot, pl.ds(r, 1), :], xs_ref.at[pl.ds(dst, 1), :],
                                      sem.at[slot]).start(priority=k)

    @pl.when(i == nt - 1)
    def _():
        wait_slot(0)
        wait_slot(1)


def _dispatch_call(pos, tile_nvalid, hn, *, tm, tme):
    n, dm = hn.shape
    n_tiles = tile_nvalid.shape[0]
    grid_spec = pltpu.PrefetchScalarGridSpec(
        num_scalar_prefetch=2,
        grid=(n // (2 * tm),),
        in_specs=[pl.BlockSpec((2 * tm, dm), lambda i, pos, nv: (i, 0))],
        out_specs=pl.BlockSpec(memory_space=pl.ANY),
        scratch_shapes=[pltpu.VMEM((2, tm, dm), F32), pltpu.VMEM((tme, dm), F32),
                        pltpu.SemaphoreType.DMA((2,)), pltpu.SemaphoreType.DMA(())],
    )
    return pl.pallas_call(
        functools.partial(_dispatch_kernel, tm=tm, tme=tme, n_tiles=n_tiles),
        grid_spec=grid_spec,
        out_shape=jax.ShapeDtypeStruct((n_tiles * tme, dm), F32),
        compiler_params=_cparams("arbitrary"),
        name="moe_dispatch",
    )(pos, tile_nvalid, hn)


def _expert_kernel(te_ref, nv_ref, x_ref, wg_ref, wu_ref, wd_ref, hns_ref, routes_ref, hs_ref,
                   y_ref, ysample_ref, wgb, wub, wdb, acc_s, *, tm):
    i = pl.program_id(0)
    prev = te_ref[jnp.maximum(i - 1, 0)]

    @pl.when(i == 0)
    def _():
        acc_s[...] = jnp.zeros(acc_s.shape, F32)

    first = (i == 0) | (te_ref[i] != prev)
    nv = nv_ref[i]

    @pl.when(first)
    def _():
        wgb[...] = wg_ref[0].astype(BF16)
        wub[...] = wu_ref[0].astype(BF16)
        wdb[...] = wd_ref[0].astype(BF16)
        route = routes_ref[...]
        ef = te_ref[i].astype(F32)
        comb = jnp.where(route[:, 0:1] == ef, route[:, 2:3], 0.0) + jnp.where(route[:, 1:2] == ef, route[:, 3:4], 0.0)
        x = jnp.concatenate([x_ref[...].astype(BF16), hns_ref[...].astype(BF16)], axis=0)
        hg = _dot(x, wgb[...])
        hu = _dot(x, wub[...])
        hmid = hg * jax.nn.sigmoid(hg) * hu
        hmid = jnp.concatenate([hmid[:tm], hmid[tm:] * comb], axis=0)
        y = _dot(hmid.astype(BF16), wdb[...])
        y_ref[...] = y[:tm]
        acc_s[...] += y[tm:]

    @pl.when(i == pl.num_programs(0) - 1)
    def _():
        ysample_ref[...] = hs_ref[...] + acc_s[...]

    @pl.when(jnp.logical_not(first) & (nv == 0))
    def _():
        y_ref[...] = jnp.zeros(y_ref.shape, F32)

    @pl.when(jnp.logical_not(first) & (nv > 0))
    def _():
        x = x_ref[...].astype(BF16)
        hg = _dot(x, wgb[...])
        hu = _dot(x, wub[...])
        hmid = hg * jax.nn.sigmoid(hg) * hu
        y_ref[...] = _dot(hmid.astype(BF16), wdb[...])


def _expert_call(tile_expert, tile_nvalid, xs, wg, wu, wd, hn_s, route_s, h_s, *, tm):
    n_tiles = tile_expert.shape[0]
    dm = xs.shape[1]
    ff = wg.shape[2]
    ns = hn_s.shape[0]
    full = lambda shape: pl.BlockSpec(shape, lambda i, te, nv: (0,) * len(shape))
    grid_spec = pltpu.PrefetchScalarGridSpec(
        num_scalar_prefetch=2,
        grid=(n_tiles,),
        in_specs=[pl.BlockSpec((tm, dm), lambda i, te, nv: (i, 0)),
                  pl.BlockSpec((1, dm, ff), lambda i, te, nv: (te[i], 0, 0)),
                  pl.BlockSpec((1, dm, ff), lambda i, te, nv: (te[i], 0, 0)),
                  pl.BlockSpec((1, ff, dm), lambda i, te, nv: (te[i], 0, 0)),
                  full((ns, dm)), full((ns, LANES)), full((ns, dm))],
        out_specs=[pl.BlockSpec((tm, dm), lambda i, te, nv: (i, 0)), full((ns, dm))],
        scratch_shapes=[pltpu.VMEM((dm, ff), BF16), pltpu.VMEM((dm, ff), BF16), pltpu.VMEM((ff, dm), BF16),
                        pltpu.VMEM((ns, dm), F32)],
    )
    return pl.pallas_call(
        functools.partial(_expert_kernel, tm=tm),
        grid_spec=grid_spec,
        out_shape=[jax.ShapeDtypeStruct(xs.shape, F32), jax.ShapeDtypeStruct((ns, dm), F32)],
        compiler_params=_cparams("arbitrary"),
        name="moe_experts",
    )(tile_expert, tile_nvalid, xs, wg, wu, wd, hn_s, route_s, h_s)


def _combine_kernel(pos_ref, ys_ref, h_ref, route_ref, o_ref, gbuf, sem, *, tm):
    i = pl.program_id(0)
    nt = pl.num_programs(0)

    def issue(half_tile, s):
        base = half_tile * (2 * tm)
        for r in range(tm):
            for k in range(2):
                src = pos_ref[base + 2 * r + k]
                pltpu.make_async_copy(ys_ref.at[pl.ds(src, 1), :], gbuf.at[s, k, pl.ds(r, 1), :],
                                      sem.at[s]).start(priority=k)

    @pl.when(i == 0)
    def _():
        issue(0, 0)
        issue(1, 1)

    for s in range(2):
        _row_wait_all(ys_ref.at[pl.ds(0, 1), :], gbuf.at[s, 0, pl.ds(0, 1), :], sem.at[s], 2 * tm)
        rows = slice(s * tm, (s + 1) * tm)
        w0 = route_ref[rows, 2:3]
        w1 = route_ref[rows, 3:4]
        o_ref[rows, :] = h_ref[rows, :] + (w0 * gbuf[s, 0] + w1 * gbuf[s, 1])

        @pl.when(i + 1 < nt)
        def _():
            issue(2 * (i + 1) + s, s)


def _combine_call(pos, ys, h, route, *, tm):
    n, dm = h.shape
    grid_spec = pltpu.PrefetchScalarGridSpec(
        num_scalar_prefetch=1,
        grid=(n // (2 * tm),),
        in_specs=[pl.BlockSpec(memory_space=pl.ANY),
                  pl.BlockSpec((2 * tm, dm), lambda i, pos: (i, 0)),
                  pl.BlockSpec((2 * tm, LANES), lambda i, pos: (i, 0))],
        out_specs=pl.BlockSpec((2 * tm, dm), lambda i, pos: (i, 0)),
        scratch_shapes=[pltpu.VMEM((2, 2, tm, dm), F32), pltpu.SemaphoreType.DMA((2,))],
    )
    return pl.pallas_call(
        functools.partial(_combine_kernel, tm=tm),
        grid_spec=grid_spec,
        out_shape=jax.ShapeDtypeStruct((n, dm), F32),
        compiler_params=_cparams("arbitrary"),
        name="moe_combine",
    )(pos, ys, h, route)


def _prep_weights(l, g_attn, w_in, g_q, g_k, pe_cmp, w_cmp1, w_cmp2, w_pool, s_pool, g_mem, w_mem_kv, g_mq, g_mk,
                  g_mix, w_out, g_ffn, w_router1, b_router1, w_router2, b_router2):
    dm = w_in.shape[1]
    w = w_in[l]
    o1 = POOL_WIDTH
    o2 = o1 + NSA_WIDTH
    o3 = o2 + NSA_KV_WIDTH
    o4 = o3 + GATE_WIDTH
    wg_pad = jnp.pad(w[:, o3:o4], ((0, 0), (0, LANES - GATE_WIDTH)))
    w_packed = jnp.concatenate([w[:, :o3], w[:, o4:], wg_pad], axis=1).astype(BF16)
    half = ROPE_DIM // 2
    inv = jnp.power(ROPE_THETA, -jnp.arange(half, dtype=F32) * 2.0 / ROPE_DIM)
    d = np.arange(LANES) % HEAD_DIM
    inv_lane = jnp.where(jnp.asarray(d < ROPE_DIM), inv[jnp.asarray(d % half)], 0.0).reshape(1, LANES)
    seg = jnp.asarray((np.arange(LANES)[:, None] // HEAD_DIM == np.arange(LANES)[None, :] // HEAD_DIM), BF16)
    gk = jnp.concatenate([jnp.tile(g_k[l, br], 2) for br in range(N_BRANCH)]).reshape(1, N_BRANCH * LANES)
    proj = dict(g_attn=g_attn[l].reshape(1, dm), w_in=w_packed, gq=jnp.tile(g_q[l], Q_PAD // HEAD_DIM).reshape(1, Q_PAD),
                gk=gk, gmq=jnp.tile(g_mq[l], QM_PAD // HEAD_DIM).reshape(1, QM_PAD), inv=inv_lane, seg=seg)

    n_grp = len(POOL_WINDOWS)
    pg = POOL_WIDTH // n_grp
    w_bd = (jnp.asarray(np.eye(n_grp, dtype=np.float32))[:, None, :, None] * w_pool[l][:, :, None, :]
            ).reshape(POOL_WIDTH, POOL_WIDTH).astype(BF16)
    pool = dict(w=w_bd, s=s_pool[l].reshape(1, POOL_WIDTH))

    half_rows = CMP_STRIDE * HEAD_DIM
    w1 = jnp.concatenate([w_cmp1[l][:, :half_rows], w_cmp1[l][:, half_rows:]], axis=2).astype(BF16)
    cmp_w = dict(w=w1, pe=pe_cmp[l].reshape(2, 1, CMP_BLOCK * HEAD_DIM), w2=w_cmp2[l].astype(BF16))

    mem = dict(g=g_mem[l].reshape(1, dm), w=w_mem_kv[l].astype(BF16),
               gk=jnp.tile(g_mk[l], MEM_HEADS).reshape(1, MEM_WIDTH))
    w_r = jnp.concatenate([w_router1[l], w_router2[l].reshape(dm, N_EXPERTS)], axis=1)
    w_r = jnp.pad(w_r, ((0, 0), (0, LANES - w_r.shape[1]))).astype(BF16)
    b_r = jnp.concatenate([b_router1[l], b_router2[l].reshape(-1)])
    b_r = jnp.pad(b_r, (0, LANES - b_r.shape[0])).reshape(1, LANES)
    fin = dict(g_mix=g_mix[l].reshape(1, dm), w_out=w_out[l].astype(BF16), g_ffn=g_ffn[l].reshape(1, dm),
               w_r=w_r, b_r=b_r)
    return proj, pool, cmp_w, mem, fin


def _pick(n, prefs):
    for p in prefs:
        if n % p == 0:
            return p
    return n


def kernel(x_prompt, x_sample, cache_kv, cache_win, state_pool, cache_mem_kv, page_table, mem_prompt, g_attn, w_in, g_q, g_k, pe_cmp, w_cmp1, w_cmp2, w_pool, s_pool, g_mem, w_mem_kv, g_mq, g_mk, g_mix, w_out, g_ffn, w_router1, b_router1, w_router2, b_router2, w_gate, w_up, w_down):
    depth = w_in.shape[0]
    bp, t, dm = x_prompt.shape
    bs, ts, _ = x_sample.shape
    n_pages = page_table.shape[1]
    page = cache_kv.shape[2]
    past_len = n_pages * page
    n_win = cache_win.shape[2]
    ff = w_gate.shape[-1]
    tq, tk = 256, 512
    assert ts == 1 and n_win == WINDOW and page % SEL_BLOCK == 0 and t % tk == 0 and t >= WINDOW + tq
    n_sel = t // SEL_BLOCK
    assert min(SEL_TOPK, n_sel) >= 3 and n_sel <= LANES
    poolt = jnp.asarray(np.arange(t // CMP_STRIDE)[None, :] // CMP_PER_SEL == np.arange(n_sel)[:, None], BF16)
    key_blk = np.arange(t // tk)[:, None, None] * (tk // SEL_BLOCK) + np.arange(tk)[None, None, :] // SEL_BLOCK
    e_mat = jnp.asarray(np.arange(LANES)[None, :, None] == key_blk, BF16)
    n_cmp_s = past_len // CMP_STRIDE
    n_slot_s = -(-(past_len // SEL_BLOCK + 1) // LANES) * LANES
    pool_s = jnp.asarray(np.arange(n_cmp_s)[:, None] // CMP_PER_SEL == np.arange(n_slot_s)[None, :], BF16)

    hp, hs = x_prompt, x_sample
    outs = [[] for _ in range(7)]
    for l in range(depth):
        proj_w, pool_w, cmp_w, mem_w, fin_w = _prep_weights(
            l, g_attn, w_in, g_q, g_k, pe_cmp, w_cmp1, w_cmp2, w_pool, s_pool, g_mem, w_mem_kv, g_mq, g_mk,
            g_mix, w_out, g_ffn, w_router1, b_router1, w_router2, b_router2)
        wg = w_gate[l].reshape(N_EXPERTS, dm, ff)
        wu = w_up[l].reshape(N_EXPERTS, dm, ff)
        wd = w_down[l].reshape(N_EXPERTS, ff, dm)
        kw_cols = CMP_STRIDE * 2 * LANES

        n = bp * t
        xp2 = hp.reshape(n, dm)
        u, qb, rows4, rowsw, kvb, gates, qmb, rows4t, rowswt = _proj_call(
            xp2, proj_w, seq_len=t, pos0=0, tm=_pick(t, (512, 256, 128)), feature_major_rows4=True)
        u3 = u.reshape(bp, t, POOL_WIDTH)
        y_pool = _pool_call(u3, pool_w["w"], pool_w["s"], tp=_pick(t, (512, 256, 128)), pos0=0)
        n_slab = 2 * NSA_KV_HEADS
        chunk_w = CMP_STRIDE * HEAD_DIM
        kcvc = _compress_prompt_call(rows4.reshape(bp, t, 4 * LANES), jnp.zeros((bp, n_slab, chunk_w), F32), cmp_w)
        o_nsa = _nsa_prompt_call(qb.reshape(bp, t, Q_PAD), gates.reshape(bp, t, LANES), kcvc,
                                 kvb.reshape(bp, t, NSA_KV_WIDTH), poolt, e_mat, tq=tq, tk=tk)
        m_len = mem_prompt.shape[1]
        mkv, mkvb = _memkv_call(mem_prompt.reshape(bp * m_len, dm), mem_w["g"], mem_w["w"], mem_w["gk"],
                                proj_w["seg"], tm=_pick(bp * m_len, (256, 128)))
        o_mem = _memattn_call(qmb.reshape(bp, t, QM_PAD), mkvb.reshape(bp, m_len, 2 * MEM_WIDTH),
                              tq=_pick(t, (512, 256, 128)))
        tmf = _pick(n, (512, 256, 128))
        h_p, hn_p, route_p, counts_p = _finish_call(y_pool.reshape(n, POOL_WIDTH), o_nsa.reshape(n, NSA_WIDTH),
                                                    o_mem.reshape(n, MEM_WIDTH), xp2, fin_w, tm=tmf)
        tme = 256
        tmd = _pick(n, (256, 128))
        te, nv, pos = _route_tables(route_p, counts_p, tme)
        xs = _dispatch_call(pos, nv, hn_p, tm=tmd, tme=tme)

        xs2 = hs.reshape(bs, dm)
        u_s, qb_s, rows4_s, rowsw_s, _, gates_s, qmb_s = _proj_call(xs2, proj_w, seq_len=1, pos0=past_len, tm=bs)
        ext = jnp.concatenate([state_pool[l], u_s[:, None, :]], axis=1)
        y_pool_s = _pool_call(ext, pool_w["w"], pool_w["s"], tp=POOL_STATE + 1, pos0=past_len - POOL_STATE)[:, -1, :]
        cache_t = jnp.transpose(cache_kv[l], (0, 2, 3, 4, 1)).reshape(
            cache_kv.shape[1], N_KV_SLOTS, NSA_KV_HEADS * HEAD_DIM, page)
        tail_s = jnp.pad(rows4_s[:, :2 * LANES].reshape(bs, n_slab, HEAD_DIM), ((0, 0), (0, 0), (0, chunk_w - HEAD_DIM)))
        kcvc_s = _compress_sample_call(page_table, cache_t, tail_s, cmp_w)
        win_s = jnp.concatenate([cache_win[l].reshape(bs, n_win, 2 * LANES)[:, 1:], rowsw_s[:, None, :]], axis=1)
        win_t = jnp.transpose(cache_win[l], (0, 2, 3, 4, 1)).reshape(bs, 2, NSA_KV_HEADS * HEAD_DIM, n_win)
        mkv_t = jnp.transpose(cache_mem_kv[l], (0, 2, 3, 4, 1)).reshape(bs, 2, MEM_WIDTH, cache_mem_kv.shape[2])
        oc, ow, om_s, score_s = _sample_attn1_call(qb_s.reshape(bs, 1, Q_PAD), kcvc_s, win_t,
                                                   rowsw_s.reshape(bs, 1, 2 * LANES), mkv_t,
                                                   qmb_s.reshape(bs, 1, QM_PAD), pool_s, t_pos=past_len)
        n_top = min(SEL_TOPK, past_len // SEL_BLOCK + 1)
        idx = _sample_topk_call(score_s.reshape(bs * NSA_KV_HEADS, n_slot_s), n_top=n_top)
        o_nsa_s = _sample_sel_call(page_table, idx, cache_t, qb_s.reshape(bs, 1, Q_PAD),
                                   rows4_s[:, 2 * LANES:].reshape(bs, 1, 2 * LANES), gates_s.reshape(bs, 1, LANES),
                                   oc, ow, t_pos=past_len, n_top=n_top)
        h_s, hn_s, route_s, _ = _finish_call(y_pool_s, o_nsa_s.reshape(bs, NSA_WIDTH), om_s.reshape(bs, MEM_WIDTH),
                                          xs2, fin_w, tm=bs)

        ys, y_s = _expert_call(te, nv, xs, wg, wu, wd, hn_s, route_s, h_s, tm=tme)
        y_p = _combine_call(pos, ys, h_p, route_p, tm=tmd).reshape(bp, t, dm)
        y_s = y_s.reshape(bs, 1, dm)

        keep = min(WINDOW, t)
        outs[0].append(jnp.transpose(rows4t.reshape(bp, N_KV_SLOTS, NSA_KV_HEADS, HEAD_DIM, t), (0, 4, 1, 2, 3)))
        outs[1].append(rows4_s.reshape(bs, 1, N_KV_SLOTS, NSA_KV_HEADS, HEAD_DIM))
        outs[2].append(jnp.transpose(rowswt[:, :, t - keep:].reshape(bp, 2, NSA_KV_HEADS, HEAD_DIM, keep), (0, 4, 1, 2, 3)))
        outs[3].append(win_s.reshape(bs, n_win, 2, NSA_KV_HEADS, HEAD_DIM))
        outs[4].append(u3[:, t - POOL_STATE:])
        outs[5].append(ext[:, 1:])
        outs[6].append(mkv.reshape(bp, m_len, 2, MEM_HEADS, HEAD_DIM))
        hp, hs = y_p, y_s
    return (hp, hs) + tuple(jnp.stack(o) for o in outs)
```
